```python
import math
import jax, jax.numpy as jnp
from jax import lax
import numpy as np

D_MODEL = 1024
BATCH = 4
SEQ = 8192
DEPTH = 1

D_SSM = D_MODEL // 2
SSM_GROUP = 16
SSM_GROUPS = D_SSM // SSM_GROUP
SSM_STATE = 64
SSM_DT_MIN = 0.001
SSM_DT_MAX = 0.1
N_HEADS = 8
HEAD_DIM = 64
D_ATT = N_HEADS * HEAD_DIM
ROT_DIM = HEAD_DIM // 4
ROPE_THETA = 500000.0
MOBA_BLOCK = 256
MOBA_TOPK = 3
Q_CHUNK = 128
D_IN = D_SSM + 3 * D_ATT + 2 * D_MODEL
PEER_HEADS = 8
PEER_KEYS = 128
PEER_EXPERTS = PEER_KEYS * PEER_KEYS
PEER_QDIM = 256
PEER_HALF = PEER_QDIM // 2
PEER_TOPK = 16
PEER_CHUNK = 128
D_PLE = 256
EPS = 1e-6
NEG = -1e30

kernel_name = "hybrid_s5_moba_peer_block"


def rmsnorm(x, g):
    x32 = x.astype(jnp.float32)
    y = x32 * lax.rsqrt(jnp.mean(x32 * x32, axis=-1, keepdims=True) + EPS)
    return (y * g.astype(jnp.float32)).astype(x.dtype)


def rotary_tables(positions):
    inv_freq = ROPE_THETA ** (-jnp.arange(0, ROT_DIM, 2, dtype=jnp.float32) / ROT_DIM)
    ang = positions.astype(jnp.float32)[..., None] * inv_freq
    return jnp.cos(ang)[:, None], jnp.sin(ang)[:, None]


def apply_partial_rope(t, cos, sin):
    half = ROT_DIM // 2
    t32 = t[..., :ROT_DIM].astype(jnp.float32)
    t1, t2 = t32[..., :half], t32[..., half:]
    rot = jnp.concatenate([t1 * cos - t2 * sin, t2 * cos + t1 * sin], axis=-1).astype(t.dtype)
    return jnp.concatenate([rot, t[..., ROT_DIM:]], axis=-1)


def s5_mixer(u, log_dt, a_re, a_im, b_re, b_im, c_re, c_im, d_skip, w_glu):
    f32 = jnp.float32
    Bsz, S, _ = u.shape
    u32 = u.astype(f32)
    ut = u32.reshape(Bsz, S, SSM_GROUPS, SSM_GROUP).transpose(1, 0, 2, 3)
    dt = jnp.exp(log_dt.astype(f32))[:, None]
    ar, ai = a_re.astype(f32), a_im.astype(f32)
    mag = jnp.exp(dt * ar)
    abar_re, abar_im = mag * jnp.cos(dt * ai), mag * jnp.sin(dt * ai)
    den = ar * ar + ai * ai
    nr, ni = abar_re - 1.0, abar_im
    f_re = (nr * ar + ni * ai) / den
    f_im = (ni * ar - nr * ai) / den
    br, bi = b_re.astype(f32), b_im.astype(f32)
    bb_re = f_re[..., None] * br - f_im[..., None] * bi
    bb_im = f_re[..., None] * bi + f_im[..., None] * br
    xin_re = jnp.einsum('sbgc,gnc->sbgn', ut, bb_re)
    xin_im = jnp.einsum('sbgc,gnc->sbgn', ut, bb_im)
    a_s_re = jnp.broadcast_to(abar_re[None, None], (S, 1, SSM_GROUPS, SSM_STATE))
    a_s_im = jnp.broadcast_to(abar_im[None, None], (S, 1, SSM_GROUPS, SSM_STATE))

    def combine(e1, e2):
        a1r, a1i, b1r, b1i = e1
        a2r, a2i, b2r, b2i = e2
        return (a2r * a1r - a2i * a1i,
                a2r * a1i + a2i * a1r,
                a2r * b1r - a2i * b1i + b2r,
                a2r * b1i + a2i * b1r + b2i)

    _, _, h_re, h_im = lax.associative_scan(combine, (a_s_re, a_s_im, xin_re, xin_im), axis=0)
    y = (jnp.einsum('sbgn,gcn->bsgc', h_re, c_re.astype(f32))
         - jnp.einsum('sbgn,gcn->bsgc', h_im, c_im.astype(f32)))
    y = y.reshape(Bsz, S, D_SSM) + d_skip.astype(f32) * u32
    y = jax.nn.gelu(y)
    y = y * jax.nn.sigmoid(y @ w_glu.astype(f32))
    return y.astype(u.dtype)


def moba_attention(q, k, v):
    f32 = jnp.float32
    Bsz, H, S, hd = q.shape
    nb = -(-S // MOBA_BLOCK)
    pad = nb * MOBA_BLOCK - S
    kp = jnp.pad(k, ((0, 0), (0, 0), (0, pad), (0, 0)))
    vp = jnp.pad(v, ((0, 0), (0, 0), (0, pad), (0, 0)))
    kb = kp.reshape(Bsz, H, nb, MOBA_BLOCK, hd)
    vb = vp.reshape(Bsz, H, nb, MOBA_BLOCK, hd)
    kmean = jnp.mean(kb.astype(f32), axis=3)
    qblk = jnp.arange(S) // MOBA_BLOCK
    gate = jnp.einsum('bhsd,bhnd->bhsn', q.astype(f32), kmean)
    past = jnp.arange(nb)[None, :] < qblk[:, None]
    gate = jnp.where(past, gate, NEG)
    k_sel = min(MOBA_TOPK, nb)
    _, sel = lax.top_k(gate, k_sel)
    sel_ok = sel < qblk[:, None]
    nc = S // Q_CHUNK
    scale = hd ** -0.5
    h_idx = jnp.arange(H)[:, None, None]

    def chunk(i):
        b = i // nc
        start = (i % nc) * Q_CHUNK
        qc = lax.dynamic_slice(q, (b, 0, start, 0), (1, H, Q_CHUNK, hd))[0]
        selc = lax.dynamic_slice(sel, (b, 0, start, 0), (1, H, Q_CHUNK, k_sel))[0]
        okc = lax.dynamic_slice(sel_ok, (b, 0, start, 0), (1, H, Q_CHUNK, k_sel))[0]
        kb_b = lax.dynamic_index_in_dim(kb, b, 0, keepdims=False)
        vb_b = lax.dynamic_index_in_dim(vb, b, 0, keepdims=False)
        kg = kb_b[h_idx, selc]
        vg = vb_b[h_idx, selc]
        own = (start // MOBA_BLOCK) * MOBA_BLOCK
        k_own = lax.dynamic_slice(kp, (b, 0, own, 0), (1, H, MOBA_BLOCK, hd))[0]
        v_own = lax.dynamic_slice(vp, (b, 0, own, 0), (1, H, MOBA_BLOCK, hd))[0]
        qpos = start + jnp.arange(Q_CHUNK)
        kpos = own + jnp.arange(MOBA_BLOCK)
        s_sel = jnp.einsum('hqd,hqjkd->hqjk', qc, kg).astype(f32) * scale
        s_sel = jnp.where(okc[..., None], s_sel, NEG).reshape(H, Q_CHUNK, k_sel * MOBA_BLOCK)
        s_own = jnp.einsum('hqd,hkd->hqk', qc, k_own).astype(f32) * scale
        s_own = jnp.where(kpos[None, :] <= qpos[:, None], s_own, NEG)
        probs = jax.nn.softmax(jnp.concatenate([s_sel, s_own], axis=-1), axis=-1).astype(v.dtype)
        p_sel = probs[..., :k_sel * MOBA_BLOCK].reshape(H, Q_CHUNK, k_sel, MOBA_BLOCK)
        p_own = probs[..., k_sel * MOBA_BLOCK:]
        return (jnp.einsum('hqjk,hqjkd->hqd', p_sel, vg)
                + jnp.einsum('hqk,hkd->hqd', p_own, v_own))

    out = lax.map(chunk, jnp.arange(Bsz * nc))
    return out.reshape(Bsz, nc, H, Q_CHUNK, hd).transpose(0, 2, 1, 3, 4).reshape(Bsz, H, S, hd)


def peer_ffn(h, w_q, keys1, keys2, u_tab, v_tab):
    f32 = jnp.float32
    Bsz, S, D = h.shape
    q = (h @ w_q).astype(f32).reshape(Bsz, S, PEER_HEADS, 2, PEER_HALF)
    s1 = jnp.einsum('bshd,hnd->bshn', q[..., 0, :], keys1.astype(f32))
    s2 = jnp.einsum('bshd,hnd->bshn', q[..., 1, :], keys2.astype(f32))
    v1, i1 = lax.top_k(s1, PEER_TOPK)
    v2, i2 = lax.top_k(s2, PEER_TOPK)
    cand = (v1[..., :, None] + v2[..., None, :]).reshape(Bsz, S, PEER_HEADS, PEER_TOPK * PEER_TOPK)
    top, flat = lax.top_k(cand, PEER_TOPK)
    e1 = jnp.take_along_axis(i1, flat // PEER_TOPK, axis=-1)
    e2 = jnp.take_along_axis(i2, flat % PEER_TOPK, axis=-1)
    experts = e1 * PEER_KEYS + e2
    gates = jax.nn.softmax(top, axis=-1).astype(h.dtype)
    T = Bsz * S
    nct = T // PEER_CHUNK
    hf = h.reshape(nct, PEER_CHUNK, D)
    ef = experts.reshape(nct, PEER_CHUNK, PEER_HEADS, PEER_TOPK)
    gf = gates.reshape(nct, PEER_CHUNK, PEER_HEADS, PEER_TOPK)

    def chunk(args):
        hc, ec, gc = args
        act = jax.nn.gelu(jnp.einsum('td,thkd->thk', hc, u_tab[ec]))
        return jnp.einsum('thk,thkd->td', gc * act, v_tab[ec])

    return lax.map(chunk, (hf, ef, gf)).reshape(Bsz, S, D)


def setup_inputs(seed: int = 0) -> dict:
    key = jax.random.key(seed)
    ks = jax.random.split(key, 32)
    nrm = lambda k, shape, s: jax.random.normal(k, shape, jnp.float32) * s
    L = DEPTH
    n_idx = jnp.arange(SSM_STATE, dtype=jnp.float32)
    return {
        "x": nrm(ks[0], (BATCH, SEQ, D_MODEL), 1.0),
        "p": nrm(ks[1], (DEPTH, BATCH, SEQ, D_PLE), 1.0),
        "positions": jnp.broadcast_to(jnp.arange(SEQ, dtype=jnp.int32), (BATCH, SEQ)),
        "g_mix": 1.0 + nrm(ks[2], (L, D_MODEL), 0.01),
        "w_in": nrm(ks[3], (L, D_MODEL, D_IN), D_MODEL ** -0.5),
        "ssm_log_dt": jax.random.uniform(ks[4], (L, SSM_GROUPS), jnp.float32,
                                         math.log(SSM_DT_MIN), math.log(SSM_DT_MAX)),
        "ssm_a_re": -0.5 + nrm(ks[5], (L, SSM_GROUPS, SSM_STATE), 0.01),
        "ssm_a_im": math.pi * n_idx + nrm(ks[6], (L, SSM_GROUPS, SSM_STATE), 0.01),
        "ssm_b_re": nrm(ks[7], (L, SSM_GROUPS, SSM_STATE, SSM_GROUP), (2 * SSM_GROUP) ** -0.5),
        "ssm_b_im": nrm(ks[8], (L, SSM_GROUPS, SSM_STATE, SSM_GROUP), (2 * SSM_GROUP) ** -0.5),
        "ssm_c_re": nrm(ks[9], (L, SSM_GROUPS, SSM_GROUP, SSM_STATE), SSM_STATE ** -0.5),
        "ssm_c_im": nrm(ks[10], (L, SSM_GROUPS, SSM_GROUP, SSM_STATE), SSM_STATE ** -0.5),
        "ssm_d": nrm(ks[11], (L, D_SSM), 1.0),
        "ssm_w_glu": nrm(ks[12], (L, D_SSM, D_SSM), D_SSM ** -0.5),
        "w_proj_ssm": nrm(ks[13], (L, D_SSM, D_MODEL), D_SSM ** -0.5),
        "w_proj_att": nrm(ks[14], (L, D_ATT, D_MODEL), D_ATT ** -0.5),
        "w_out": nrm(ks[15], (L, D_MODEL, D_MODEL), D_MODEL ** -0.5),
        "g_ffn": 1.0 + nrm(ks[16], (L, D_MODEL), 0.01),
        "peer_w_q": nrm(ks[17], (L, D_MODEL, PEER_HEADS * PEER_QDIM), D_MODEL ** -0.5),
        "peer_keys1": nrm(ks[18], (L, PEER_HEADS, PEER_KEYS, PEER_HALF), PEER_HALF ** -0.5),
        "peer_keys2": nrm(ks[19], (L, PEER_HEADS, PEER_KEYS, PEER_HALF), PEER_HALF ** -0.5),
        "peer_u": nrm(ks[20], (L, PEER_EXPERTS, D_MODEL), D_MODEL ** -0.5),
        "peer_v": nrm(ks[21], (L, PEER_EXPERTS, D_MODEL), PEER_HEADS ** -0.5),
        "g_ple": 1.0 + nrm(ks[22], (L, D_MODEL), 0.01),
        "ple_w_gate": nrm(ks[23], (L, D_MODEL, D_MODEL), D_MODEL ** -0.5),
        "ple_w_proj": nrm(ks[24], (L, D_PLE, D_MODEL), D_PLE ** -0.5),
        "g_final": 1.0 + nrm(ks[25], (D_MODEL,), 0.01),
    }


def reference(x, p, positions, g_mix, w_in, ssm_log_dt, ssm_a_re, ssm_a_im, ssm_b_re, ssm_b_im,
              ssm_c_re, ssm_c_im, ssm_d, ssm_w_glu, w_proj_ssm, w_proj_att, w_out, g_ffn,
              peer_w_q, peer_keys1, peer_keys2, peer_u, peer_v, g_ple, ple_w_gate, ple_w_proj,
              g_final):
    Bsz, S, _ = x.shape
    cos, sin = rotary_tables(positions)
    splits = [D_SSM, D_SSM + D_ATT, D_SSM + 2 * D_ATT, D_SSM + 3 * D_ATT, D_SSM + 3 * D_ATT + D_MODEL]

    def heads(t):
        return t.reshape(Bsz, S, N_HEADS, HEAD_DIM).transpose(0, 2, 1, 3)

    for i in range(DEPTH):
        h = rmsnorm(x, g_mix[i])
        z = h @ w_in[i]
        u_ssm, q, k, v, z_ga, z_gb = jnp.split(z, splits, axis=-1)
        y_a = s5_mixer(u_ssm, ssm_log_dt[i], ssm_a_re[i], ssm_a_im[i], ssm_b_re[i], ssm_b_im[i],
                       ssm_c_re[i], ssm_c_im[i], ssm_d[i], ssm_w_glu[i]) @ w_proj_ssm[i]
        qh = apply_partial_rope(heads(q), cos, sin)
        kh = apply_partial_rope(heads(k), cos, sin)
        att = moba_attention(qh, kh, heads(v))
        y_b = att.transpose(0, 2, 1, 3).reshape(Bsz, S, D_ATT) @ w_proj_att[i]
        merged = jax.nn.sigmoid(z_ga) * y_a + jax.nn.sigmoid(z_gb) * y_b
        x = x + merged @ w_out[i]
        x = x + peer_ffn(rmsnorm(x, g_ffn[i]), peer_w_q[i], peer_keys1[i], peer_keys2[i],
                         peer_u[i], peer_v[i])
        e = p[i] @ ple_w_proj[i]
        x = x + jax.nn.sigmoid(rmsnorm(x, g_ple[i]) @ ple_w_gate[i]) * e
    return rmsnorm(x, g_final)
```

```python
import functools
import math

import jax
import jax.numpy as jnp
from jax import lax
from jax.experimental import pallas as pl
from jax.experimental.pallas import tpu as pltpu

F32 = jnp.float32
BF16 = jnp.bfloat16

D_MODEL = 1024
D_SSM = 512
SSM_GROUP = 16
SSM_GROUPS = 32
SSM_STATE = 64
D_STATE = SSM_GROUPS * SSM_STATE
N_HEADS = 8
HEAD_DIM = 64
D_ATT = 512
ROT_DIM = 16
ROPE_THETA = 500000.0
MOBA_BLOCK = 256
MOBA_TOPK = 3
PEER_HEADS = 8
PEER_KEYS = 128
PEER_QDIM = 256
PEER_HALF = 128
PEER_TOPK = 16
PEER_SEL = PEER_HEADS * PEER_TOPK
D_PLE = 256
EPS = 1e-6
NEG = -1e30
LANES = 128
SUBLANES = 8
VMEM_LIMIT = 48 * 1024 * 1024
HIGHEST = lax.Precision.HIGHEST


def _rms(x, g):
    return x * lax.rsqrt(jnp.mean(x * x, axis=-1, keepdims=True) + EPS) * g


def _dot(a, b):
    return jnp.dot(a, b, preferred_element_type=F32)


def _dot_nt(a, b, precision=None):
    return lax.dot_general(a, b, (((1,), (1,)), ((), ())), precision=precision,
                           preferred_element_type=F32)


IN_TS = 512


def _in_proj_kernel(x_ref, pos_ref, g_ref, w_ref, invf_ref,
                    u_ref, q_ref, k_ref, v_ref, ga_ref, gb_ref):
    h = _rms(x_ref[...], g_ref[...]).astype(BF16)

    def proj(lo, hi):
        return _dot(h, w_ref[:, lo:hi])

    u_ref[...] = proj(0, D_SSM).astype(BF16)
    ang = pos_ref[...].astype(F32) * invf_ref[...]
    cos = jnp.cos(ang)
    sin = jnp.sin(ang)
    lane = lax.broadcasted_iota(jnp.int32, (1, LANES), 1) % HEAD_DIM
    half = ROT_DIM // 2
    sin_hi = jnp.where((lane >= half) & (lane < ROT_DIM), sin, 0.0)
    sin_lo = jnp.where(lane < half, -sin, 0.0)
    reps = D_ATT // LANES
    cos4 = jnp.concatenate([cos] * reps, axis=1)
    sin_hi4 = jnp.concatenate([sin_hi] * reps, axis=1)
    sin_lo4 = jnp.concatenate([sin_lo] * reps, axis=1)

    def rope(t):
        return (t * cos4 + pltpu.roll(t, half, 1) * sin_hi4
                + pltpu.roll(t, D_ATT - half, 1) * sin_lo4)

    q = rope(proj(D_SSM, D_SSM + D_ATT))
    q_ref[...] = (q * (HEAD_DIM ** -0.5)).astype(BF16)
    k_ref[...] = rope(proj(D_SSM + D_ATT, D_SSM + 2 * D_ATT)).astype(BF16)
    v_ref[...] = proj(D_SSM + 2 * D_ATT, D_SSM + 3 * D_ATT).astype(BF16)
    o = D_SSM + 3 * D_ATT
    ga_ref[...] = jax.nn.sigmoid(proj(o, o + D_MODEL)).astype(BF16)
    gb_ref[...] = jax.nn.sigmoid(proj(o + D_MODEL, o + 2 * D_MODEL)).astype(BF16)


def _in_proj(x, positions, g_mix, w_in):
    B, S, _ = x.shape
    ts = min(IN_TS, S)
    inv_freq = ROPE_THETA ** (-jnp.arange(0, ROT_DIM, 2, dtype=F32) / ROT_DIM)
    lane = jnp.arange(LANES) % HEAD_DIM
    invf = jnp.where(lane < ROT_DIM, inv_freq[lane % (ROT_DIM // 2)], 0.0).reshape(1, LANES)
    d_in = w_in.shape[1]
    tok = lambda d: pl.BlockSpec((None, ts, d), lambda b, i: (b, i, 0))
    full = lambda shape: pl.BlockSpec(shape, lambda b, i: (0,) * len(shape))
    outs = pl.pallas_call(
        _in_proj_kernel,
        grid=(B, S // ts),
        in_specs=[tok(D_MODEL), tok(1), full((1, D_MODEL)), full((D_MODEL, d_in)), full((1, LANES))],
        out_specs=[pl.BlockSpec((ts, D_SSM), lambda b, i: (i, b)),
                   tok(D_ATT), tok(D_ATT), tok(D_ATT), tok(D_MODEL), tok(D_MODEL)],
        out_shape=[jax.ShapeDtypeStruct((S, B * D_SSM), BF16),
                   jax.ShapeDtypeStruct((B, S, D_ATT), BF16),
                   jax.ShapeDtypeStruct((B, S, D_ATT), BF16),
                   jax.ShapeDtypeStruct((B, S, D_ATT), BF16),
                   jax.ShapeDtypeStruct((B, S, D_MODEL), BF16),
                   jax.ShapeDtypeStruct((B, S, D_MODEL), BF16)],
        compiler_params=pltpu.CompilerParams(
            dimension_semantics=("parallel", "parallel"), vmem_limit_bytes=VMEM_LIMIT),
        name="in_proj",
    )(x, positions.reshape(B, S, 1), g_mix.reshape(1, D_MODEL), w_in.astype(BF16), invf)
    return outs


S5_TS = 128
S5_BATCH = 4
S5_COLS = 512


def _s5_kernel(u_ref, bre_ref, bim_ref, a1r_ref, a1i_ref, pr_ref, pi_ref,
               cre_ref, cim_ref, d_ref, wglu_ref, y_ref,
               xr, xi, cr, ci, ysc):
    rows = xr.shape[0]
    ts = rows // S5_BATCH

    @pl.when(pl.program_id(0) == 0)
    def _():
        cr[...] = jnp.zeros_like(cr)
        ci[...] = jnp.zeros_like(ci)

    u = u_ref[...]
    xr[...] = _dot(u, bre_ref[...])
    xi[...] = _dot(u, bim_ref[...])

    hi_rows = lax.broadcasted_iota(jnp.int32, (SUBLANES, S5_COLS), 0) >= S5_BATCH
    for cb in range(D_STATE // S5_COLS):
        sl = slice(cb * S5_COLS, (cb + 1) * S5_COLS)
        a_r, a_i = a1r_ref[:, sl], a1i_ref[:, sl]
        p_r, p_i = pr_ref[:, sl], pi_ref[:, sl]

        def body(t, carry):
            c_r, c_i = carry
            r0 = pl.multiple_of(t * SUBLANES, SUBLANES)
            x_r = xr[pl.ds(r0, SUBLANES), sl]
            x_i = xi[pl.ds(r0, SUBLANES), sl]
            s_r = pltpu.roll(x_r, S5_BATCH, 0)
            s_i = pltpu.roll(x_i, S5_BATCH, 0)
            h_r = x_r + (a_r * s_r - a_i * s_i) + (p_r * c_r - p_i * c_i)
            h_i = x_i + (a_r * s_i + a_i * s_r) + (p_r * c_i + p_i * c_r)
            xr[pl.ds(r0, SUBLANES), sl] = h_r
            xi[pl.ds(r0, SUBLANES), sl] = h_i
            n_r = jnp.where(hi_rows, h_r, pltpu.roll(h_r, S5_BATCH, 0))
            n_i = jnp.where(hi_rows, h_i, pltpu.roll(h_i, S5_BATCH, 0))
            return n_r, n_i

        c_r, c_i = lax.fori_loop(0, rows // SUBLANES, body, (cr[:, sl], ci[:, sl]), unroll=2)
        cr[:, sl] = c_r
        ci[:, sl] = c_i

    y = (_dot(xr[...].astype(BF16), cre_ref[...]) - _dot(xi[...].astype(BF16), cim_ref[...])
         + d_ref[...] * u.astype(F32))
    y = jax.nn.gelu(y)
    y = y * jax.nn.sigmoid(_dot(y.astype(BF16), wglu_ref[...]))
    for c in range(D_SSM // LANES):
        ysc[c] = y[:, c * LANES:(c + 1) * LANES]
    for b in range(S5_BATCH):
        for c in range(D_SSM // LANES):
            y_ref[b, :, c * LANES:(c + 1) * LANES] = (
                ysc[c, pl.ds(b, ts, stride=S5_BATCH), :].astype(BF16))


def _s5_tables(log_dt, a_re, a_im, b_re, b_im, c_re, c_im):
    dt = jnp.exp(log_dt.astype(F32))[:, None]
    ar, ai = a_re.astype(F32), a_im.astype(F32)
    mag = jnp.exp(dt * ar)
    abar_re, abar_im = mag * jnp.cos(dt * ai), mag * jnp.sin(dt * ai)
    den = ar * ar + ai * ai
    nr, ni = abar_re - 1.0, abar_im
    f_re = (nr * ar + ni * ai) / den
    f_im = (ni * ar - nr * ai) / den
    br, bi = b_re.astype(F32), b_im.astype(F32)
    bb_re = f_re[..., None] * br - f_im[..., None] * bi
    bb_im = f_re[..., None] * bi + f_im[..., None] * br
    eye = jnp.eye(SSM_GROUPS, dtype=F32)

    def in_blockdiag(bb):
        return jnp.einsum('gnc,gh->gchn', bb, eye).reshape(D_SSM, D_STATE)

    def out_blockdiag(c):
        return jnp.einsum('gcn,gh->gnhc', c.astype(F32), eye).reshape(D_STATE, D_SSM)

    a_r = abar_re.reshape(1, D_STATE)
    a_i = abar_im.reshape(1, D_STATE)
    a2_r = a_r * a_r - a_i * a_i
    a2_i = 2.0 * a_r * a_i
    hi = (jnp.arange(SUBLANES) >= S5_BATCH)[:, None]
    a1r = jnp.where(hi, a_r, 0.0)
    a1i = jnp.where(hi, a_i, 0.0)
    p_r = jnp.where(hi, a2_r, a_r)
    p_i = jnp.where(hi, a2_i, a_i)
    return (in_blockdiag(bb_re).astype(BF16), in_blockdiag(bb_im).astype(BF16),
            a1r, a1i, p_r, p_i,
            out_blockdiag(c_re).astype(BF16), out_blockdiag(c_im).astype(BF16))


def _s5(u_sb, tables, d_skip, w_glu, B, S):
    assert B == S5_BATCH
    ts = min(S5_TS, S)
    rows = ts * B
    bre, bim, a1r, a1i, p_r, p_i, cre, cim = tables
    full = lambda shape: pl.BlockSpec(shape, lambda i: (0,) * len(shape))
    return pl.pallas_call(
        _s5_kernel,
        grid=(S // ts,),
        in_specs=[pl.BlockSpec((rows, D_SSM), lambda i: (i, 0)),
                  full((D_SSM, D_STATE)), full((D_SSM, D_STATE)),
                  full((SUBLANES, D_STATE)), full((SUBLANES, D_STATE)),
                  full((SUBLANES, D_STATE)), full((SUBLANES, D_STATE)),
                  full((D_STATE, D_SSM)), full((D_STATE, D_SSM)),
                  full((1, D_SSM)), full((D_SSM, D_SSM))],
        out_specs=pl.BlockSpec((B, ts, D_SSM), lambda i: (0, i, 0)),
        out_shape=jax.ShapeDtypeStruct((B, S, D_SSM), BF16),
        scratch_shapes=[pltpu.VMEM((rows, D_STATE), F32), pltpu.VMEM((rows, D_STATE), F32),
                        pltpu.VMEM((SUBLANES, D_STATE), F32), pltpu.VMEM((SUBLANES, D_STATE), F32),
                        pltpu.VMEM((D_SSM // LANES, rows, LANES), F32)],
        compiler_params=pltpu.CompilerParams(
            dimension_semantics=("arbitrary",), vmem_limit_bytes=VMEM_LIMIT),
        name="s5",
    )(u_sb.reshape(S * B, D_SSM), bre, bim, a1r, a1i, p_r, p_i, cre, cim,
      d_skip.reshape(1, D_SSM).astype(F32), w_glu.astype(BF16))


def _moba_kernel(q_ref, k_ref, v_ref, o_ref, kmean, kaug_a, kaug_b, m_s, l_s, acc_s):
    qi = pl.program_id(2)
    nb = k_ref.shape[0] // MOBA_BLOCK
    lane = lax.broadcasted_iota(jnp.int32, (1, LANES), 1)
    head_a = lane < HEAD_DIM

    @pl.when(qi == 0)
    def _():
        kmean[...] = jnp.zeros_like(kmean)
        for j in range(nb):
            rows = pl.ds(j * MOBA_BLOCK, MOBA_BLOCK)
            kj = k_ref[rows, :].astype(F32)
            kmean[j:j + 1, :] = jnp.sum(kj, axis=0, keepdims=True) * (1.0 / MOBA_BLOCK)
            kaug_a[rows, :] = jnp.where(head_a, kj, jnp.where(lane - HEAD_DIM == j, 1.0, 0.0)).astype(BF16)
            kaug_b[rows, :] = jnp.where(head_a, jnp.where(lane == j, 1.0, 0.0), kj).astype(BF16)

    qf = q_ref[...].astype(F32)
    row = lax.broadcasted_iota(jnp.int32, (MOBA_BLOCK, MOBA_BLOCK), 0)
    col = lax.broadcasted_iota(jnp.int32, (MOBA_BLOCK, MOBA_BLOCK), 1)
    causal = col <= row
    own = pl.ds(pl.multiple_of(qi * MOBA_BLOCK, MOBA_BLOCK), MOBA_BLOCK)
    outs = []
    for is_a, kaug in ((True, kaug_a), (False, kaug_b)):
        mine = head_a if is_a else jnp.logical_not(head_a)
        q_own = jnp.where(mine, qf, 0.0)
        g = _dot_nt(q_own, kmean[...], precision=HIGHEST)
        g = jnp.where(lane < qi, g, NEG)
        sel = jnp.zeros(g.shape, F32)
        for _ in range(MOBA_TOPK):
            m = jnp.max(g, axis=-1, keepdims=True)
            idx = jnp.min(jnp.where(g == m, lane, LANES), axis=-1, keepdims=True)
            hit = lane == idx
            sel = jnp.where(hit & (idx < qi), 1.0, sel)
            g = jnp.where(hit, -jnp.inf, g)
        bias = jnp.where(sel > 0.0, 0.0, NEG)
        if is_a:
            bias = pltpu.roll(bias, HEAD_DIM, 1)
        q_aug = jnp.where(mine, qf, bias).astype(BF16)

        s = _dot_nt(q_own.astype(BF16), kaug[own, :])
        s = jnp.where(causal, s, NEG)
        m0 = jnp.max(s, axis=-1, keepdims=True)
        p = jnp.exp(s - m0)
        m_s[...] = m0
        l_s[...] = jnp.sum(p, axis=-1, keepdims=True)
        acc_s[...] = _dot(p.astype(BF16), v_ref[own, :])

        def body(j, _):
            rows = pl.ds(pl.multiple_of(j * MOBA_BLOCK, MOBA_BLOCK), MOBA_BLOCK)
            s = _dot_nt(q_aug, kaug[rows, :])
            m_old = m_s[...]
            m_new = jnp.maximum(m_old, jnp.max(s, axis=-1, keepdims=True))
            alpha = jnp.exp(m_old - m_new)
            p = jnp.exp(s - m_new)
            m_s[...] = m_new
            l_s[...] = alpha * l_s[...] + jnp.sum(p, axis=-1, keepdims=True)
            acc_s[...] = alpha * acc_s[...] + _dot(p.astype(BF16), v_ref[rows, :])
            return 0

        lax.fori_loop(0, qi, body, 0)
        outs.append(acc_s[...] / l_s[...])
    o_ref[...] = jnp.where(head_a, outs[0], outs[1]).astype(BF16)


def _moba(q, k, v):
    B, S, _ = q.shape
    nq = S // MOBA_BLOCK
    assert nq <= HEAD_DIM
    blk = pl.BlockSpec((None, MOBA_BLOCK, LANES), lambda b, h, i: (b, i, h))
    seq = pl.BlockSpec((None, S, LANES), lambda b, h, i: (b, 0, h))
    return pl.pallas_call(
        _moba_kernel,
        grid=(B, D_ATT // LANES, nq),
        in_specs=[blk, seq, seq],
        out_specs=blk,
        out_shape=jax.ShapeDtypeStruct((B, S, D_ATT), BF16),
        scratch_shapes=[pltpu.VMEM((LANES, LANES), F32),
                        pltpu.VMEM((S, LANES), BF16), pltpu.VMEM((S, LANES), BF16),
                        pltpu.VMEM((MOBA_BLOCK, 1), F32), pltpu.VMEM((MOBA_BLOCK, 1), F32),
                        pltpu.VMEM((MOBA_BLOCK, LANES), F32)],
        compiler_params=pltpu.CompilerParams(
            dimension_semantics=("parallel", "parallel", "arbitrary"), vmem_limit_bytes=VMEM_LIMIT),
        name="moba",
    )(q, k, v)


MERGE_TS = 256


def _merge_kernel(x_ref, ys_ref, at_ref, ga_ref, gb_ref, wa_ref, wb_ref, wo_ref, g_ref,
                  wq_ref, k1_ref, k2_ref, x1_ref, hq_ref, sc_ref):
    ya = _dot(ys_ref[...], wa_ref[...])
    yb = _dot(at_ref[...], wb_ref[...])
    merged = ga_ref[...].astype(F32) * ya + gb_ref[...].astype(F32) * yb
    x1 = x_ref[...] + _dot(merged.astype(BF16), wo_ref[...])
    x1_ref[...] = x1
    hq = _rms(x1, g_ref[...])
    hq_ref[...] = hq
    qp = _dot(hq.astype(BF16), wq_ref[...])
    for h in range(PEER_HEADS):
        o = h * PEER_QDIM
        sc_ref[:, o:o + PEER_HALF] = _dot_nt(qp[:, o:o + PEER_HALF], k1_ref[h], precision=HIGHEST)
        sc_ref[:, o + PEER_HALF:o + PEER_QDIM] = _dot_nt(
            qp[:, o + PEER_HALF:o + PEER_QDIM], k2_ref[h], precision=HIGHEST)


def _merge(x2d, ys, att, ga, gb, w_proj_ssm, w_proj_att, w_out, g_ffn, peer_w_q, keys1, keys2):
    T = x2d.shape[0]
    ts = min(MERGE_TS, T)
    tok = lambda d: pl.BlockSpec((ts, d), lambda i: (i, 0))
    full = lambda shape: pl.BlockSpec(shape, lambda i: (0,) * len(shape))
    qd = PEER_HEADS * PEER_QDIM
    return pl.pallas_call(
        _merge_kernel,
        grid=(T // ts,),
        in_specs=[tok(D_MODEL), tok(D_SSM), tok(D_ATT), tok(D_MODEL), tok(D_MODEL),
                  full((D_SSM, D_MODEL)), full((D_ATT, D_MODEL)), full((D_MODEL, D_MODEL)),
                  full((1, D_MODEL)), full((D_MODEL, qd)),
                  full((PEER_HEADS, PEER_KEYS, PEER_HALF)), full((PEER_HEADS, PEER_KEYS, PEER_HALF))],
        out_specs=[tok(D_MODEL), tok(D_MODEL), tok(qd)],
        out_shape=[jax.ShapeDtypeStruct((T, D_MODEL), F32),
                   jax.ShapeDtypeStruct((T, D_MODEL), F32),
                   jax.ShapeDtypeStruct((T, qd), F32)],
        compiler_params=pltpu.CompilerParams(
            dimension_semantics=("parallel",), vmem_limit_bytes=VMEM_LIMIT),
        name="merge",
    )(x2d, ys, att, ga, gb, w_proj_ssm.astype(BF16), w_proj_att.astype(BF16), w_out.astype(BF16),
      g_ffn.reshape(1, D_MODEL), peer_w_q.astype(BF16), keys1, keys2)


TOPK_TS = 256


def _extract_topk(s, lane, k):
    width = s.shape[-1]
    lane_w = lane if width == LANES else lax.broadcasted_iota(jnp.int32, (1, width), 1)
    vals = jnp.zeros((s.shape[0], LANES), F32)
    idxs = jnp.zeros((s.shape[0], LANES), jnp.int32)
    for r in range(k):
        m = jnp.max(s, axis=-1, keepdims=True)
        idx = jnp.min(jnp.where(s == m, lane_w, width), axis=-1, keepdims=True)
        vals = jnp.where(lane == r, m, vals)
        idxs = jnp.where(lane == r, idx, idxs)
        s = jnp.where(lane_w == idx, -jnp.inf, s)
    return vals, idxs


def _topk_kernel(sc_ref, idx_ref, gate_ref):
    lane = lax.broadcasted_iota(jnp.int32, (1, LANES), 1)
    kk = PEER_TOPK * PEER_TOPK
    r_i = lax.broadcasted_iota(jnp.int32, (LANES, kk), 0)
    c_i = lax.broadcasted_iota(jnp.int32, (LANES, kk), 1)
    rep1 = jnp.where(c_i // PEER_TOPK == r_i, 1.0, 0.0)
    rep2 = jnp.where(c_i % PEER_TOPK == r_i, 1.0, 0.0)
    idx_out = jnp.zeros(idx_ref.shape, jnp.int32)
    gate_out = jnp.zeros(gate_ref.shape, F32)
    for h in range(PEER_HEADS):
        o = h * PEER_QDIM
        v1, i1 = _extract_topk(sc_ref[:, o:o + PEER_HALF], lane, PEER_TOPK)
        v2, i2 = _extract_topk(sc_ref[:, o + PEER_HALF:o + PEER_QDIM], lane, PEER_TOPK)
        cand = (jnp.dot(v1, rep1, precision=HIGHEST, preferred_element_type=F32)
                + jnp.dot(v2, rep2, precision=HIGHEST, preferred_element_type=F32))
        eid = (jnp.dot(i1.astype(F32), rep1, precision=HIGHEST, preferred_element_type=F32) * PEER_KEYS
               + jnp.dot(i2.astype(F32), rep2, precision=HIGHEST, preferred_element_type=F32))
        lane2 = lax.broadcasted_iota(jnp.int32, (1, kk), 1)
        top = jnp.full((cand.shape[0], LANES), NEG, F32)
        for r in range(PEER_TOPK):
            m = jnp.max(cand, axis=-1, keepdims=True)
            flat = jnp.min(jnp.where(cand == m, lane2, kk), axis=-1, keepdims=True)
            hit = lane2 == flat
            e = jnp.sum(jnp.where(hit, eid, 0.0), axis=-1, keepdims=True)
            top = jnp.where(lane == r, m, top)
            idx_out = jnp.where(lane == h * PEER_TOPK + r, e.astype(jnp.int32), idx_out)
            cand = jnp.where(hit, -jnp.inf, cand)
        p = jnp.exp(top - jnp.max(top, axis=-1, keepdims=True))
        p = p / jnp.sum(p, axis=-1, keepdims=True)
        if h:
            p = pltpu.roll(p, h * PEER_TOPK, 1)
        gate_out = gate_out + p
    idx_ref[...] = idx_out
    gate_ref[...] = gate_out


def _topk(scores):
    T = scores.shape[0]
    ts = min(TOPK_TS, T)
    return pl.pallas_call(
        _topk_kernel,
        grid=(T // ts,),
        in_specs=[pl.BlockSpec((ts, scores.shape[1]), lambda i: (i, 0))],
        out_specs=[pl.BlockSpec((ts, PEER_SEL), lambda i: (i, 0)),
                   pl.BlockSpec((ts, PEER_SEL), lambda i: (i, 0))],
        out_shape=[jax.ShapeDtypeStruct((T, PEER_SEL), jnp.int32),
                   jax.ShapeDtypeStruct((T, PEER_SEL), F32)],
        compiler_params=pltpu.CompilerParams(
            dimension_semantics=("parallel",), vmem_limit_bytes=VMEM_LIMIT),
        name="topk",
    )(scores)


PEER_TT = 8


def _peer_kernel(idx_ref, hq_ref, gate_ref, u_hbm, v_hbm, o_ref, ubuf, vbuf, sems):
    n = PEER_TT * PEER_SEL

    def issue(r, _):
        e = idx_ref[r // PEER_SEL, r % PEER_SEL]
        pltpu.make_async_copy(u_hbm.at[e], ubuf.at[r], sems.at[0]).start()
        pltpu.make_async_copy(v_hbm.at[e], vbuf.at[r], sems.at[1]).start()
        return 0

    lax.fori_loop(0, n, issue, 0)
    gate_t = jnp.transpose(gate_ref[...])
    pltpu.make_async_copy(u_hbm.at[pl.ds(0, n)], ubuf, sems.at[0]).wait()
    acts = []
    for t in range(PEER_TT):
        rows = ubuf[t * PEER_SEL:(t + 1) * PEER_SEL, :]
        s = jnp.sum(rows * hq_ref[t:t + 1, :], axis=-1, keepdims=True)
        acts.append(gate_t[:, t:t + 1] * jax.nn.gelu(s))
    pltpu.make_async_copy(v_hbm.at[pl.ds(0, n)], vbuf, sems.at[1]).wait()
    for t in range(PEER_TT):
        rows = vbuf[t * PEER_SEL:(t + 1) * PEER_SEL, :]
        o_ref[t:t + 1, :] = jnp.sum(acts[t] * rows, axis=0, keepdims=True)


def _peer(idx, hq, gates, peer_u, peer_v):
    T = hq.shape[0]
    n = PEER_TT * PEER_SEL
    tok = lambda d: pl.BlockSpec((PEER_TT, d), lambda i: (i, 0))
    return pl.pallas_call(
        _peer_kernel,
        grid=(T // PEER_TT,),
        in_specs=[pl.BlockSpec((PEER_TT, PEER_SEL), lambda i: (i, 0), memory_space=pltpu.SMEM),
                  tok(D_MODEL), tok(PEER_SEL),
                  pl.BlockSpec(memory_space=pl.ANY), pl.BlockSpec(memory_space=pl.ANY)],
        out_specs=tok(D_MODEL),
        out_shape=jax.ShapeDtypeStruct((T, D_MODEL), F32),
        scratch_shapes=[pltpu.VMEM((n, D_MODEL), F32), pltpu.VMEM((n, D_MODEL), F32),
                        pltpu.SemaphoreType.DMA((2,))],
        compiler_params=pltpu.CompilerParams(
            dimension_semantics=("arbitrary",), vmem_limit_bytes=VMEM_LIMIT),
        name="peer",
    )(idx, hq, gates, peer_u, peer_v)


FINAL_TS = 256


def _final_kernel(x1_ref, pe_ref, p_ref, gp_ref, wg_ref, wp_ref, gf_ref, o_ref):
    x2 = x1_ref[...] + pe_ref[...]
    e = _dot(p_ref[...].astype(BF16), wp_ref[...])
    gate = jax.nn.sigmoid(_dot(_rms(x2, gp_ref[...]).astype(BF16), wg_ref[...]))
    o_ref[...] = _rms(x2 + gate * e, gf_ref[...])


def _final(x1, peer_out, p2d, g_ple, ple_w_gate, ple_w_proj, g_final):
    T = x1.shape[0]
    ts = min(FINAL_TS, T)
    tok = lambda d: pl.BlockSpec((ts, d), lambda i: (i, 0))
    full = lambda shape: pl.BlockSpec(shape, lambda i: (0,) * len(shape))
    return pl.pallas_call(
        _final_kernel,
        grid=(T // ts,),
        in_specs=[tok(D_MODEL), tok(D_MODEL), tok(D_PLE), full((1, D_MODEL)),
                  full((D_MODEL, D_MODEL)), full((D_PLE, D_MODEL)), full((1, D_MODEL))],
        out_specs=tok(D_MODEL),
        out_shape=jax.ShapeDtypeStruct((T, D_MODEL), F32),
        compiler_params=pltpu.CompilerParams(
            dimension_semantics=("parallel",), vmem_limit_bytes=VMEM_LIMIT),
        name="final",
    )(x1, peer_out, p2d, g_ple.reshape(1, D_MODEL), ple_w_gate.astype(BF16),
      ple_w_proj.astype(BF16), g_final.reshape(1, D_MODEL))


def kernel(x, p, positions, g_mix, w_in, ssm_log_dt, ssm_a_re, ssm_a_im, ssm_b_re, ssm_b_im,
           ssm_c_re, ssm_c_im, ssm_d, ssm_w_glu, w_proj_ssm, w_proj_att, w_out, g_ffn,
           peer_w_q, peer_keys1, peer_keys2, peer_u, peer_v, g_ple, ple_w_gate, ple_w_proj,
           g_final):
    B, S, _ = x.shape
    T = B * S
    assert w_in.shape[0] == 1, "the final rmsnorm is fused into the single layer's last stage"
    for i in range(1):
        u_sb, q, k, v, ga, gb = _in_proj(x, positions, g_mix[i], w_in[i])
        tables = _s5_tables(ssm_log_dt[i], ssm_a_re[i], ssm_a_im[i], ssm_b_re[i], ssm_b_im[i],
                            ssm_c_re[i], ssm_c_im[i])
        ys = _s5(u_sb, tables, ssm_d[i], ssm_w_glu[i], B, S)
        att = _moba(q, k, v)
        x1, hq, scores = _merge(
            x.reshape(T, D_MODEL), ys.reshape(T, D_SSM), att.reshape(T, D_ATT),
            ga.reshape(T, D_MODEL), gb.reshape(T, D_MODEL),
            w_proj_ssm[i], w_proj_att[i], w_out[i], g_ffn[i], peer_w_q[i],
            peer_keys1[i], peer_keys2[i])
        idx, gates = _topk(scores)
        peer_out = _peer(idx, hq, gates, peer_u[i], peer_v[i])
        x = _final(x1, peer_out, p[i].reshape(T, D_PLE), g_ple[i], ple_w_gate[i],
                   ple_w_proj[i], g_final).reshape(B, S, D_MODEL)
    return x
```

```python
import functools
import math

import jax
import jax.numpy as jnp
from jax import lax
from jax.experimental import pallas as pl
from jax.experimental.pallas import tpu as pltpu
from jax.experimental.pallas import tpu_sc as plsc

F32 = jnp.float32
BF16 = jnp.bfloat16

D_MODEL = 1024
D_SSM = 512
SSM_GROUP = 16
SSM_GROUPS = 32
SSM_STATE = 64
D_STATE = SSM_GROUPS * SSM_STATE
N_HEADS = 8
HEAD_DIM = 64
D_ATT = 512
ROT_DIM = 16
ROPE_THETA = 500000.0
MOBA_BLOCK = 256
MOBA_TOPK = 3
PEER_HEADS = 8
PEER_KEYS = 128
PEER_QDIM = 256
PEER_HALF = 128
PEER_TOPK = 16
PEER_SEL = PEER_HEADS * PEER_TOPK
D_PLE = 256
EPS = 1e-6
NEG = -1e30
LANES = 128
SUBLANES = 8
VMEM_LIMIT = 48 * 1024 * 1024
HIGHEST = lax.Precision.HIGHEST


def _rms(x, g):
    return x * lax.rsqrt(jnp.mean(x * x, axis=-1, keepdims=True) + EPS) * g


def _dot(a, b):
    return jnp.dot(a, b, preferred_element_type=F32)


def _dot_nt(a, b, precision=None):
    return lax.dot_general(a, b, (((1,), (1,)), ((), ())), precision=precision,
                           preferred_element_type=F32)


IN_TS = 512


def _in_proj_kernel(x_ref, pos_ref, g_ref, w_ref, invf_ref,
                    u_ref, q_ref, k_ref, v_ref, ga_ref, gb_ref):
    h = _rms(x_ref[...], g_ref[...]).astype(BF16)

    def proj(lo, hi):
        return _dot(h, w_ref[:, lo:hi])

    u_ref[...] = proj(0, D_SSM).astype(BF16)
    ang = pos_ref[...].astype(F32) * invf_ref[...]
    cos = jnp.cos(ang)
    sin = jnp.sin(ang)
    lane = lax.broadcasted_iota(jnp.int32, (1, LANES), 1) % HEAD_DIM
    half = ROT_DIM // 2
    sin_hi = jnp.where((lane >= half) & (lane < ROT_DIM), sin, 0.0)
    sin_lo = jnp.where(lane < half, -sin, 0.0)
    reps = D_ATT // LANES
    cos4 = jnp.concatenate([cos] * reps, axis=1)
    sin_hi4 = jnp.concatenate([sin_hi] * reps, axis=1)
    sin_lo4 = jnp.concatenate([sin_lo] * reps, axis=1)

    def rope(t):
        return (t * cos4 + pltpu.roll(t, half, 1) * sin_hi4
                + pltpu.roll(t, D_ATT - half, 1) * sin_lo4)

    q = rope(proj(D_SSM, D_SSM + D_ATT))
    q_ref[...] = (q * (HEAD_DIM ** -0.5)).astype(BF16)
    k_ref[...] = rope(proj(D_SSM + D_ATT, D_SSM + 2 * D_ATT)).astype(BF16)
    v_ref[...] = proj(D_SSM + 2 * D_ATT, D_SSM + 3 * D_ATT).astype(BF16)
    o = D_SSM + 3 * D_ATT
    ga_ref[...] = jax.nn.sigmoid(proj(o, o + D_MODEL)).astype(BF16)
    gb_ref[...] = jax.nn.sigmoid(proj(o + D_MODEL, o + 2 * D_MODEL)).astype(BF16)


def _in_proj(x, positions, g_mix, w_in):
    B, S, _ = x.shape
    ts = min(IN_TS, S)
    inv_freq = ROPE_THETA ** (-jnp.arange(0, ROT_DIM, 2, dtype=F32) / ROT_DIM)
    lane = jnp.arange(LANES) % HEAD_DIM
    invf = jnp.where(lane < ROT_DIM, inv_freq[lane % (ROT_DIM // 2)], 0.0).reshape(1, LANES)
    d_in = w_in.shape[1]
    tok = lambda d: pl.BlockSpec((None, ts, d), lambda b, i: (b, i, 0))
    full = lambda shape: pl.BlockSpec(shape, lambda b, i: (0,) * len(shape))
    outs = pl.pallas_call(
        _in_proj_kernel,
        grid=(B, S // ts),
        in_specs=[tok(D_MODEL), tok(1), full((1, D_MODEL)), full((D_MODEL, d_in)), full((1, LANES))],
        out_specs=[pl.BlockSpec((ts, D_SSM), lambda b, i: (i, b)),
                   tok(D_ATT), tok(D_ATT), tok(D_ATT), tok(D_MODEL), tok(D_MODEL)],
        out_shape=[jax.ShapeDtypeStruct((S, B * D_SSM), BF16),
                   jax.ShapeDtypeStruct((B, S, D_ATT), BF16),
                   jax.ShapeDtypeStruct((B, S, D_ATT), BF16),
                   jax.ShapeDtypeStruct((B, S, D_ATT), BF16),
                   jax.ShapeDtypeStruct((B, S, D_MODEL), BF16),
                   jax.ShapeDtypeStruct((B, S, D_MODEL), BF16)],
        compiler_params=pltpu.CompilerParams(
            dimension_semantics=("parallel", "parallel"), vmem_limit_bytes=VMEM_LIMIT),
        name="in_proj",
    )(x, positions.reshape(B, S, 1), g_mix.reshape(1, D_MODEL), w_in.astype(BF16), invf)
    return outs


S5_TS = 128
S5_BATCH = 4
S5_COLS = 512


def _s5_kernel(u_ref, bre_ref, bim_ref, a1r_ref, a1i_ref, pr_ref, pi_ref,
               cre_ref, cim_ref, d_ref, wglu_ref, y_ref,
               xr, xi, cr, ci, ysc):
    rows = xr.shape[0]
    ts = rows // S5_BATCH

    @pl.when(pl.program_id(0) == 0)
    def _():
        cr[...] = jnp.zeros_like(cr)
        ci[...] = jnp.zeros_like(ci)

    u = u_ref[...]
    xr[...] = _dot(u, bre_ref[...])
    xi[...] = _dot(u, bim_ref[...])

    hi_rows = lax.broadcasted_iota(jnp.int32, (SUBLANES, S5_COLS), 0) >= S5_BATCH
    for cb in range(D_STATE // S5_COLS):
        sl = slice(cb * S5_COLS, (cb + 1) * S5_COLS)
        a_r, a_i = a1r_ref[:, sl], a1i_ref[:, sl]
        p_r, p_i = pr_ref[:, sl], pi_ref[:, sl]

        def body(t, carry):
            c_r, c_i = carry
            r0 = pl.multiple_of(t * SUBLANES, SUBLANES)
            x_r = xr[pl.ds(r0, SUBLANES), sl]
            x_i = xi[pl.ds(r0, SUBLANES), sl]
            s_r = pltpu.roll(x_r, S5_BATCH, 0)
            s_i = pltpu.roll(x_i, S5_BATCH, 0)
            h_r = x_r + (a_r * s_r - a_i * s_i) + (p_r * c_r - p_i * c_i)
            h_i = x_i + (a_r * s_i + a_i * s_r) + (p_r * c_i + p_i * c_r)
            xr[pl.ds(r0, SUBLANES), sl] = h_r
            xi[pl.ds(r0, SUBLANES), sl] = h_i
            n_r = jnp.where(hi_rows, h_r, pltpu.roll(h_r, S5_BATCH, 0))
            n_i = jnp.where(hi_rows, h_i, pltpu.roll(h_i, S5_BATCH, 0))
            return n_r, n_i

        c_r, c_i = lax.fori_loop(0, rows // SUBLANES, body, (cr[:, sl], ci[:, sl]), unroll=2)
        cr[:, sl] = c_r
        ci[:, sl] = c_i

    y = (_dot(xr[...].astype(BF16), cre_ref[...]) - _dot(xi[...].astype(BF16), cim_ref[...])
         + d_ref[...] * u.astype(F32))
    y = jax.nn.gelu(y)
    y = y * jax.nn.sigmoid(_dot(y.astype(BF16), wglu_ref[...]))
    for c in range(D_SSM // LANES):
        ysc[c] = y[:, c * LANES:(c + 1) * LANES]
    for b in range(S5_BATCH):
        for c in range(D_SSM // LANES):
            y_ref[b, :, c * LANES:(c + 1) * LANES] = (
                ysc[c, pl.ds(b, ts, stride=S5_BATCH), :].astype(BF16))


def _s5_tables(log_dt, a_re, a_im, b_re, b_im, c_re, c_im):
    dt = jnp.exp(log_dt.astype(F32))[:, None]
    ar, ai = a_re.astype(F32), a_im.astype(F32)
    mag = jnp.exp(dt * ar)
    abar_re, abar_im = mag * jnp.cos(dt * ai), mag * jnp.sin(dt * ai)
    den = ar * ar + ai * ai
    nr, ni = abar_re - 1.0, abar_im
    f_re = (nr * ar + ni * ai) / den
    f_im = (ni * ar - nr * ai) / den
    br, bi = b_re.astype(F32), b_im.astype(F32)
    bb_re = f_re[..., None] * br - f_im[..., None] * bi
    bb_im = f_re[..., None] * bi + f_im[..., None] * br
    eye = jnp.eye(SSM_GROUPS, dtype=F32)

    def in_blockdiag(bb):
        return jnp.einsum('gnc,gh->gchn', bb, eye).reshape(D_SSM, D_STATE)

    def out_blockdiag(c):
        return jnp.einsum('gcn,gh->gnhc', c.astype(F32), eye).reshape(D_STATE, D_SSM)

    a_r = abar_re.reshape(1, D_STATE)
    a_i = abar_im.reshape(1, D_STATE)
    a2_r = a_r * a_r - a_i * a_i
    a2_i = 2.0 * a_r * a_i
    hi = (jnp.arange(SUBLANES) >= S5_BATCH)[:, None]
    a1r = jnp.where(hi, a_r, 0.0)
    a1i = jnp.where(hi, a_i, 0.0)
    p_r = jnp.where(hi, a2_r, a_r)
    p_i = jnp.where(hi, a2_i, a_i)
    return (in_blockdiag(bb_re).astype(BF16), in_blockdiag(bb_im).astype(BF16),
            a1r, a1i, p_r, p_i,
            out_blockdiag(c_re).astype(BF16), out_blockdiag(c_im).astype(BF16))


def _s5(u_sb, tables, d_skip, w_glu, B, S):
    assert B == S5_BATCH
    ts = min(S5_TS, S)
    rows = ts * B
    bre, bim, a1r, a1i, p_r, p_i, cre, cim = tables
    full = lambda shape: pl.BlockSpec(shape, lambda i: (0,) * len(shape))
    return pl.pallas_call(
        _s5_kernel,
        grid=(S // ts,),
        in_specs=[pl.BlockSpec((rows, D_SSM), lambda i: (i, 0)),
                  full((D_SSM, D_STATE)), full((D_SSM, D_STATE)),
                  full((SUBLANES, D_STATE)), full((SUBLANES, D_STATE)),
                  full((SUBLANES, D_STATE)), full((SUBLANES, D_STATE)),
                  full((D_STATE, D_SSM)), full((D_STATE, D_SSM)),
                  full((1, D_SSM)), full((D_SSM, D_SSM))],
        out_specs=pl.BlockSpec((B, ts, D_SSM), lambda i: (0, i, 0)),
        out_shape=jax.ShapeDtypeStruct((B, S, D_SSM), BF16),
        scratch_shapes=[pltpu.VMEM((rows, D_STATE), F32), pltpu.VMEM((rows, D_STATE), F32),
                        pltpu.VMEM((SUBLANES, D_STATE), F32), pltpu.VMEM((SUBLANES, D_STATE), F32),
                        pltpu.VMEM((D_SSM // LANES, rows, LANES), F32)],
        compiler_params=pltpu.CompilerParams(
            dimension_semantics=("arbitrary",), vmem_limit_bytes=VMEM_LIMIT),
        name="s5",
    )(u_sb.reshape(S * B, D_SSM), bre, bim, a1r, a1i, p_r, p_i, cre, cim,
      d_skip.reshape(1, D_SSM).astype(F32), w_glu.astype(BF16))


def _moba_kernel(q_ref, k_ref, v_ref, o_ref, kmean, kaug_a, kaug_b, m_s, l_s, acc_s):
    qi = pl.program_id(2)
    nb = k_ref.shape[0] // MOBA_BLOCK
    lane = lax.broadcasted_iota(jnp.int32, (1, LANES), 1)
    head_a = lane < HEAD_DIM

    @pl.when(qi == 0)
    def _():
        kmean[...] = jnp.zeros_like(kmean)
        for j in range(nb):
            rows = pl.ds(j * MOBA_BLOCK, MOBA_BLOCK)
            kj = k_ref[rows, :].astype(F32)
            kmean[j:j + 1, :] = jnp.sum(kj, axis=0, keepdims=True) * (1.0 / MOBA_BLOCK)
            kaug_a[rows, :] = jnp.where(head_a, kj, jnp.where(lane - HEAD_DIM == j, 1.0, 0.0)).astype(BF16)
            kaug_b[rows, :] = jnp.where(head_a, jnp.where(lane == j, 1.0, 0.0), kj).astype(BF16)

    qf = q_ref[...].astype(F32)
    row = lax.broadcasted_iota(jnp.int32, (MOBA_BLOCK, MOBA_BLOCK), 0)
    col = lax.broadcasted_iota(jnp.int32, (MOBA_BLOCK, MOBA_BLOCK), 1)
    causal = col <= row
    own = pl.ds(pl.multiple_of(qi * MOBA_BLOCK, MOBA_BLOCK), MOBA_BLOCK)
    outs = []
    for is_a, kaug in ((True, kaug_a), (False, kaug_b)):
        mine = head_a if is_a else jnp.logical_not(head_a)
        q_own = jnp.where(mine, qf, 0.0)
        g = _dot_nt(q_own, kmean[...], precision=HIGHEST)
        g = jnp.where(lane < qi, g, NEG)
        sel = jnp.zeros(g.shape, F32)
        for _ in range(MOBA_TOPK):
            m = jnp.max(g, axis=-1, keepdims=True)
            idx = jnp.min(jnp.where(g == m, lane, LANES), axis=-1, keepdims=True)
            hit = lane == idx
            sel = jnp.where(hit & (idx < qi), 1.0, sel)
            g = jnp.where(hit, -jnp.inf, g)
        bias = jnp.where(sel > 0.0, 0.0, NEG)
        if is_a:
            bias = pltpu.roll(bias, HEAD_DIM, 1)
        q_aug = jnp.where(mine, qf, bias).astype(BF16)

        s = _dot_nt(q_own.astype(BF16), kaug[own, :])
        s = jnp.where(causal, s, NEG)
        m0 = jnp.max(s, axis=-1, keepdims=True)
        p = jnp.exp(s - m0)
        m_s[...] = m0
        l_s[...] = jnp.sum(p, axis=-1, keepdims=True)
        acc_s[...] = _dot(p.astype(BF16), v_ref[own, :])

        def body(j, _):
            rows = pl.ds(pl.multiple_of(j * MOBA_BLOCK, MOBA_BLOCK), MOBA_BLOCK)
            s = _dot_nt(q_aug, kaug[rows, :])
            m_old = m_s[...]
            m_new = jnp.maximum(m_old, jnp.max(s, axis=-1, keepdims=True))
            alpha = jnp.exp(m_old - m_new)
            p = jnp.exp(s - m_new)
            m_s[...] = m_new
            l_s[...] = alpha * l_s[...] + jnp.sum(p, axis=-1, keepdims=True)
            acc_s[...] = alpha * acc_s[...] + _dot(p.astype(BF16), v_ref[rows, :])
            return 0

        lax.fori_loop(0, qi, body, 0)
        outs.append(acc_s[...] / l_s[...])
    o_ref[...] = jnp.where(head_a, outs[0], outs[1]).astype(BF16)


def _moba(q, k, v):
    B, S, _ = q.shape
    nq = S // MOBA_BLOCK
    assert nq <= HEAD_DIM
    blk = pl.BlockSpec((None, MOBA_BLOCK, LANES), lambda b, h, i: (b, i, h))
    seq = pl.BlockSpec((None, S, LANES), lambda b, h, i: (b, 0, h))
    return pl.pallas_call(
        _moba_kernel,
        grid=(B, D_ATT // LANES, nq),
        in_specs=[blk, seq, seq],
        out_specs=blk,
        out_shape=jax.ShapeDtypeStruct((B, S, D_ATT), BF16),
        scratch_shapes=[pltpu.VMEM((LANES, LANES), F32),
                        pltpu.VMEM((S, LANES), BF16), pltpu.VMEM((S, LANES), BF16),
                        pltpu.VMEM((MOBA_BLOCK, 1), F32), pltpu.VMEM((MOBA_BLOCK, 1), F32),
                        pltpu.VMEM((MOBA_BLOCK, LANES), F32)],
        compiler_params=pltpu.CompilerParams(
            dimension_semantics=("parallel", "parallel", "arbitrary"), vmem_limit_bytes=VMEM_LIMIT),
        name="moba",
    )(q, k, v)


MERGE_TS = 256


def _merge_kernel(x_ref, ys_ref, at_ref, ga_ref, gb_ref, wa_ref, wb_ref, wo_ref, g_ref,
                  wq_ref, k1_ref, k2_ref, x1_ref, hq_ref, sc_ref):
    ya = _dot(ys_ref[...], wa_ref[...])
    yb = _dot(at_ref[...], wb_ref[...])
    merged = ga_ref[...].astype(F32) * ya + gb_ref[...].astype(F32) * yb
    x1 = x_ref[...] + _dot(merged.astype(BF16), wo_ref[...])
    x1_ref[...] = x1
    hq = _rms(x1, g_ref[...])
    hq_ref[...] = hq
    qp = _dot(hq.astype(BF16), wq_ref[...])
    for h in range(PEER_HEADS):
        o = h * PEER_QDIM
        sc_ref[:, o:o + PEER_HALF] = _dot_nt(qp[:, o:o + PEER_HALF], k1_ref[h], precision=HIGHEST)
        sc_ref[:, o + PEER_HALF:o + PEER_QDIM] = _dot_nt(
            qp[:, o + PEER_HALF:o + PEER_QDIM], k2_ref[h], precision=HIGHEST)


def _merge(x2d, ys, att, ga, gb, w_proj_ssm, w_proj_att, w_out, g_ffn, peer_w_q, keys1, keys2):
    T = x2d.shape[0]
    ts = min(MERGE_TS, T)
    tok = lambda d: pl.BlockSpec((ts, d), lambda i: (i, 0))
    full = lambda shape: pl.BlockSpec(shape, lambda i: (0,) * len(shape))
    qd = PEER_HEADS * PEER_QDIM
    return pl.pallas_call(
        _merge_kernel,
        grid=(T // ts,),
        in_specs=[tok(D_MODEL), tok(D_SSM), tok(D_ATT), tok(D_MODEL), tok(D_MODEL),
                  full((D_SSM, D_MODEL)), full((D_ATT, D_MODEL)), full((D_MODEL, D_MODEL)),
                  full((1, D_MODEL)), full((D_MODEL, qd)),
                  full((PEER_HEADS, PEER_KEYS, PEER_HALF)), full((PEER_HEADS, PEER_KEYS, PEER_HALF))],
        out_specs=[tok(D_MODEL), tok(D_MODEL), tok(qd)],
        out_shape=[jax.ShapeDtypeStruct((T, D_MODEL), F32),
                   jax.ShapeDtypeStruct((T, D_MODEL), F32),
                   jax.ShapeDtypeStruct((T, qd), F32)],
        compiler_params=pltpu.CompilerParams(
            dimension_semantics=("parallel",), vmem_limit_bytes=VMEM_LIMIT),
        name="merge",
    )(x2d, ys, att, ga, gb, w_proj_ssm.astype(BF16), w_proj_att.astype(BF16), w_out.astype(BF16),
      g_ffn.reshape(1, D_MODEL), peer_w_q.astype(BF16), keys1, keys2)


TOPK_TS = 256


def _extract_topk(s, lane, k):
    width = s.shape[-1]
    lane_w = lane if width == LANES else lax.broadcasted_iota(jnp.int32, (1, width), 1)
    vals = jnp.zeros((s.shape[0], LANES), F32)
    idxs = jnp.zeros((s.shape[0], LANES), jnp.int32)
    for r in range(k):
        m = jnp.max(s, axis=-1, keepdims=True)
        idx = jnp.min(jnp.where(s == m, lane_w, width), axis=-1, keepdims=True)
        vals = jnp.where(lane == r, m, vals)
        idxs = jnp.where(lane == r, idx, idxs)
        s = jnp.where(lane_w == idx, -jnp.inf, s)
    return vals, idxs


def _topk_kernel(sc_ref, idx_ref, gate_ref):
    lane = lax.broadcasted_iota(jnp.int32, (1, LANES), 1)
    kk = PEER_TOPK * PEER_TOPK
    r_i = lax.broadcasted_iota(jnp.int32, (LANES, kk), 0)
    c_i = lax.broadcasted_iota(jnp.int32, (LANES, kk), 1)
    rep1 = jnp.where(c_i // PEER_TOPK == r_i, 1.0, 0.0)
    rep2 = jnp.where(c_i % PEER_TOPK == r_i, 1.0, 0.0)
    idx_out = jnp.zeros(idx_ref.shape, jnp.int32)
    gate_out = jnp.zeros(gate_ref.shape, F32)
    for h in range(PEER_HEADS):
        o = h * PEER_QDIM
        v1, i1 = _extract_topk(sc_ref[:, o:o + PEER_HALF], lane, PEER_TOPK)
        v2, i2 = _extract_topk(sc_ref[:, o + PEER_HALF:o + PEER_QDIM], lane, PEER_TOPK)
        cand = (jnp.dot(v1, rep1, precision=HIGHEST, preferred_element_type=F32)
                + jnp.dot(v2, rep2, precision=HIGHEST, preferred_element_type=F32))
        eid = (jnp.dot(i1.astype(F32), rep1, precision=HIGHEST, preferred_element_type=F32) * PEER_KEYS
               + jnp.dot(i2.astype(F32), rep2, precision=HIGHEST, preferred_element_type=F32))
        lane2 = lax.broadcasted_iota(jnp.int32, (1, kk), 1)
        top = jnp.full((cand.shape[0], LANES), NEG, F32)
        for r in range(PEER_TOPK):
            m = jnp.max(cand, axis=-1, keepdims=True)
            flat = jnp.min(jnp.where(cand == m, lane2, kk), axis=-1, keepdims=True)
            hit = lane2 == flat
            e = jnp.sum(jnp.where(hit, eid, 0.0), axis=-1, keepdims=True)
            top = jnp.where(lane == r, m, top)
            idx_out = jnp.where(lane == h * PEER_TOPK + r, e.astype(jnp.int32), idx_out)
            cand = jnp.where(hit, -jnp.inf, cand)
        p = jnp.exp(top - jnp.max(top, axis=-1, keepdims=True))
        p = p / jnp.sum(p, axis=-1, keepdims=True)
        if h:
            p = pltpu.roll(p, h * PEER_TOPK, 1)
        gate_out = gate_out + p
    idx_ref[...] = idx_out
    gate_ref[...] = gate_out


def _topk(scores):
    T = scores.shape[0]
    ts = min(TOPK_TS, T)
    return pl.pallas_call(
        _topk_kernel,
        grid=(T // ts,),
        in_specs=[pl.BlockSpec((ts, scores.shape[1]), lambda i: (i, 0))],
        out_specs=[pl.BlockSpec((ts, PEER_SEL), lambda i: (i, 0)),
                   pl.BlockSpec((ts, PEER_SEL), lambda i: (i, 0))],
        out_shape=[jax.ShapeDtypeStruct((T, PEER_SEL), jnp.int32),
                   jax.ShapeDtypeStruct((T, PEER_SEL), F32)],
        compiler_params=pltpu.CompilerParams(
            dimension_semantics=("parallel",), vmem_limit_bytes=VMEM_LIMIT),
        name="topk",
    )(scores)


SC_CORES = 2
SC_SUBCORES = 16
SC_LANES = 16
SC_WORKERS = SC_CORES * SC_SUBCORES
PEER_CH = SC_LANES
PEER_NCH = PEER_SEL // PEER_CH
PEER_NCOL = D_MODEL // SC_LANES
GELU_C = 0.7978845608028654


def _gelu_tanh_via_exp(x):
    z = GELU_C * (x + 0.044715 * (x * x * x))
    t = 1.0 - 2.0 / (jnp.exp(2.0 * z) + 1.0)
    return 0.5 * x * (1.0 + t)


def _peer_sc_body(idx_hbm, gate_hbm, h_hbm, u_hbm, v_hbm, o_hbm,
                  idx_v, gate_v, h_v, ubuf, vbuf, pbuf, w_v, out_v, usem, vsem):
    tok_per_w = o_hbm.shape[0] // SC_WORKERS
    base = (lax.axis_index("s") * SC_CORES + lax.axis_index("c")) * tok_per_w
    lane = lax.iota(jnp.int32, SC_LANES)

    def start(tab, buf, sem, c):
        rows = idx_v[pl.ds(c * PEER_CH, PEER_CH)]
        return pltpu.async_copy(tab.at[rows], buf.at[c % 2], sem.at[c % 2])

    def token(t, carry):
        tok = base + t
        pltpu.sync_copy(idx_hbm.at[tok], idx_v)
        pltpu.sync_copy(gate_hbm.at[tok], gate_v)
        pltpu.sync_copy(h_hbm.at[tok], h_v)
        ucp = [start(u_hbm, ubuf, usem, 0), start(u_hbm, ubuf, usem, 1)]
        vcp = [start(v_hbm, vbuf, vsem, 0), start(v_hbm, vbuf, vsem, 1)]
        for c in range(PEER_NCH):
            b = c % 2
            ucp[b].wait()

            def dot_step(cc, accs):
                col = pl.ds(pl.multiple_of(cc * SC_LANES, SC_LANES), SC_LANES)
                hv = h_v[col]
                return tuple(accs[r] + ubuf[b, r, col] * hv for r in range(PEER_CH))

            accs = lax.fori_loop(0, PEER_NCOL, dot_step,
                                 tuple(jnp.zeros((SC_LANES,), F32) for _ in range(PEER_CH)))
            if c + 2 < PEER_NCH:
                ucp[b] = start(u_hbm, ubuf, usem, c + 2)
            for r in range(PEER_CH):
                pbuf[r, :] = accs[r]
            s = jnp.zeros((SC_LANES,), F32)
            for j in range(SC_LANES):
                s = s + plsc.load_gather(pbuf, [lane, jnp.full((SC_LANES,), j, jnp.int32)])
            rows = pl.ds(c * PEER_CH, PEER_CH)
            w_v[rows] = gate_v[rows] * _gelu_tanh_via_exp(s)
        for c in range(PEER_NCH):
            b = c % 2
            vcp[b].wait()
            ws = [plsc.load_gather(w_v, [jnp.full((SC_LANES,), c * PEER_CH + r, jnp.int32)])
                  for r in range(PEER_CH)]

            def acc_step(cc, _):
                col = pl.ds(pl.multiple_of(cc * SC_LANES, SC_LANES), SC_LANES)
                o = jnp.zeros((SC_LANES,), F32) if c == 0 else out_v[col]
                for r in range(PEER_CH):
                    o = o + ws[r] * vbuf[b, r, col]
                out_v[col] = o
                return 0

            lax.fori_loop(0, PEER_NCOL, acc_step, 0)
            if c + 2 < PEER_NCH:
                vcp[b] = start(v_hbm, vbuf, vsem, c + 2)
        pltpu.sync_copy(out_v, o_hbm.at[tok])
        return carry

    lax.fori_loop(0, tok_per_w, token, 0)


def _peer(idx, hq, gates, peer_u, peer_v):
    T = hq.shape[0]
    assert T % SC_WORKERS == 0
    mesh = plsc.VectorSubcoreMesh(core_axis_name="c", subcore_axis_name="s",
                                  num_cores=SC_CORES, num_subcores=SC_SUBCORES)
    return pl.kernel(
        _peer_sc_body,
        out_type=jax.ShapeDtypeStruct((T, D_MODEL), F32),
        mesh=mesh,
        scratch_types=[
            pltpu.VMEM((PEER_SEL,), jnp.int32), pltpu.VMEM((PEER_SEL,), F32),
            pltpu.VMEM((D_MODEL,), F32),
            pltpu.VMEM((2, PEER_CH, D_MODEL), F32), pltpu.VMEM((2, PEER_CH, D_MODEL), F32),
            pltpu.VMEM((PEER_CH, SC_LANES), F32), pltpu.VMEM((PEER_SEL,), F32),
            pltpu.VMEM((D_MODEL,), F32),
            pltpu.SemaphoreType.DMA((2,)), pltpu.SemaphoreType.DMA((2,)),
        ],
        compiler_params=pltpu.CompilerParams(needs_layout_passes=False),
        name="peer_sc",
    )(idx, gates, hq, peer_u, peer_v)


FINAL_TS = 256


def _final_kernel(x1_ref, pe_ref, p_ref, gp_ref, wg_ref, wp_ref, gf_ref, o_ref):
    x2 = x1_ref[...] + pe_ref[...]
    e = _dot(p_ref[...].astype(BF16), wp_ref[...])
    gate = jax.nn.sigmoid(_dot(_rms(x2, gp_ref[...]).astype(BF16), wg_ref[...]))
    o_ref[...] = _rms(x2 + gate * e, gf_ref[...])


def _final(x1, peer_out, p2d, g_ple, ple_w_gate, ple_w_proj, g_final):
    T = x1.shape[0]
    ts = min(FINAL_TS, T)
    tok = lambda d: pl.BlockSpec((ts, d), lambda i: (i, 0))
    full = lambda shape: pl.BlockSpec(shape, lambda i: (0,) * len(shape))
    return pl.pallas_call(
        _final_kernel,
        grid=(T // ts,),
        in_specs=[tok(D_MODEL), tok(D_MODEL), tok(D_PLE), full((1, D_MODEL)),
                  full((D_MODEL, D_MODEL)), full((D_PLE, D_MODEL)), full((1, D_MODEL))],
        out_specs=tok(D_MODEL),
        out_shape=jax.ShapeDtypeStruct((T, D_MODEL), F32),
        compiler_params=pltpu.CompilerParams(
            dimension_semantics=("parallel",), vmem_limit_bytes=VMEM_LIMIT),
        name="final",
    )(x1, peer_out, p2d, g_ple.reshape(1, D_MODEL), ple_w_gate.astype(BF16),
      ple_w_proj.astype(BF16), g_final.reshape(1, D_MODEL))


def kernel(x, p, positions, g_mix, w_in, ssm_log_dt, ssm_a_re, ssm_a_im, ssm_b_re, ssm_b_im,
           ssm_c_re, ssm_c_im, ssm_d, ssm_w_glu, w_proj_ssm, w_proj_att, w_out, g_ffn,
           peer_w_q, peer_keys1, peer_keys2, peer_u, peer_v, g_ple, ple_w_gate, ple_w_proj,
           g_final):
    B, S, _ = x.shape
    T = B * S
    assert w_in.shape[0] == 1, "the final rmsnorm is fused into the single layer's last stage"
    for i in range(1):
        u_sb, q, k, v, ga, gb = _in_proj(x, positions, g_mix[i], w_in[i])
        tables = _s5_tables(ssm_log_dt[i], ssm_a_re[i], ssm_a_im[i], ssm_b_re[i], ssm_b_im[i],
                            ssm_c_re[i], ssm_c_im[i])
        ys = _s5(u_sb, tables, ssm_d[i], ssm_w_glu[i], B, S)
        att = _moba(q, k, v)
        x1, hq, scores = _merge(
            x.reshape(T, D_MODEL), ys.reshape(T, D_SSM), att.reshape(T, D_ATT),
            ga.reshape(T, D_MODEL), gb.reshape(T, D_MODEL),
            w_proj_ssm[i], w_proj_att[i], w_out[i], g_ffn[i], peer_w_q[i],
            peer_keys1[i], peer_keys2[i])
        idx, gates = _topk(scores)
        peer_out = _peer(idx, hq, gates, peer_u[i], peer_v[i])
        x = _final(x1, peer_out, p[i].reshape(T, D_PLE), g_ple[i], ple_w_gate[i],
                   ple_w_proj[i], g_final).reshape(B, S, D_MODEL)
    return x
```

```python
import functools
import math

import jax
import jax.numpy as jnp
from jax import lax
from jax.experimental import pallas as pl
from jax.experimental.pallas import tpu as pltpu
from jax.experimental.pallas import tpu_sc as plsc

F32 = jnp.float32
BF16 = jnp.bfloat16

D_MODEL = 1024
D_SSM = 512
SSM_GROUP = 16
SSM_GROUPS = 32
SSM_STATE = 64
D_STATE = SSM_GROUPS * SSM_STATE
N_HEADS = 8
HEAD_DIM = 64
D_ATT = 512
ROT_DIM = 16
ROPE_THETA = 500000.0
MOBA_BLOCK = 256
MOBA_TOPK = 3
PEER_HEADS = 8
PEER_KEYS = 128
PEER_QDIM = 256
PEER_HALF = 128
PEER_TOPK = 16
PEER_SEL = PEER_HEADS * PEER_TOPK
D_PLE = 256
EPS = 1e-6
NEG = -1e30
LANES = 128
SUBLANES = 8
VMEM_LIMIT = 48 * 1024 * 1024
HIGHEST = lax.Precision.HIGHEST


def _rms(x, g):
    return x * lax.rsqrt(jnp.mean(x * x, axis=-1, keepdims=True) + EPS) * g


def _dot(a, b):
    return jnp.dot(a, b, preferred_element_type=F32)


def _dot_nt(a, b, precision=None):
    return lax.dot_general(a, b, (((1,), (1,)), ((), ())), precision=precision,
                           preferred_element_type=F32)


IN_TS = 512


def _in_proj_kernel(x_ref, pos_ref, g_ref, w_ref, invf_ref,
                    u_ref, q_ref, k_ref, v_ref, ga_ref, gb_ref):
    h = _rms(x_ref[...], g_ref[...]).astype(BF16)

    def proj(lo, hi):
        return _dot(h, w_ref[:, lo:hi])

    u_ref[...] = proj(0, D_SSM).astype(BF16)
    ang = pos_ref[...].astype(F32) * invf_ref[...]
    cos = jnp.cos(ang)
    sin = jnp.sin(ang)
    lane = lax.broadcasted_iota(jnp.int32, (1, LANES), 1) % HEAD_DIM
    half = ROT_DIM // 2
    sin_hi = jnp.where((lane >= half) & (lane < ROT_DIM), sin, 0.0)
    sin_lo = jnp.where(lane < half, -sin, 0.0)
    reps = D_ATT // LANES
    cos4 = jnp.concatenate([cos] * reps, axis=1)
    sin_hi4 = jnp.concatenate([sin_hi] * reps, axis=1)
    sin_lo4 = jnp.concatenate([sin_lo] * reps, axis=1)

    def rope(t):
        return (t * cos4 + pltpu.roll(t, half, 1) * sin_hi4
                + pltpu.roll(t, D_ATT - half, 1) * sin_lo4)

    q = rope(proj(D_SSM, D_SSM + D_ATT))
    q_ref[...] = (q * (HEAD_DIM ** -0.5)).astype(BF16)
    k_ref[...] = rope(proj(D_SSM + D_ATT, D_SSM + 2 * D_ATT)).astype(BF16)
    v_ref[...] = proj(D_SSM + 2 * D_ATT, D_SSM + 3 * D_ATT).astype(BF16)
    o = D_SSM + 3 * D_ATT
    ga_ref[...] = jax.nn.sigmoid(proj(o, o + D_MODEL)).astype(BF16)
    gb_ref[...] = jax.nn.sigmoid(proj(o + D_MODEL, o + 2 * D_MODEL)).astype(BF16)


def _in_proj(x, positions, g_mix, w_in):
    B, S, _ = x.shape
    ts = min(IN_TS, S)
    inv_freq = ROPE_THETA ** (-jnp.arange(0, ROT_DIM, 2, dtype=F32) / ROT_DIM)
    lane = jnp.arange(LANES) % HEAD_DIM
    invf = jnp.where(lane < ROT_DIM, inv_freq[lane % (ROT_DIM // 2)], 0.0).reshape(1, LANES)
    d_in = w_in.shape[1]
    tok = lambda d: pl.BlockSpec((None, ts, d), lambda b, i: (b, i, 0))
    full = lambda shape: pl.BlockSpec(shape, lambda b, i: (0,) * len(shape))
    outs = pl.pallas_call(
        _in_proj_kernel,
        grid=(B, S // ts),
        in_specs=[tok(D_MODEL), tok(1), full((1, D_MODEL)), full((D_MODEL, d_in)), full((1, LANES))],
        out_specs=[pl.BlockSpec((ts, D_SSM), lambda b, i: (i, b)),
                   tok(D_ATT), tok(D_ATT), tok(D_ATT), tok(D_MODEL), tok(D_MODEL)],
        out_shape=[jax.ShapeDtypeStruct((S, B * D_SSM), BF16),
                   jax.ShapeDtypeStruct((B, S, D_ATT), BF16),
                   jax.ShapeDtypeStruct((B, S, D_ATT), BF16),
                   jax.ShapeDtypeStruct((B, S, D_ATT), BF16),
                   jax.ShapeDtypeStruct((B, S, D_MODEL), BF16),
                   jax.ShapeDtypeStruct((B, S, D_MODEL), BF16)],
        compiler_params=pltpu.CompilerParams(
            dimension_semantics=("parallel", "parallel"), vmem_limit_bytes=VMEM_LIMIT),
        name="in_proj",
    )(x, positions.reshape(B, S, 1), g_mix.reshape(1, D_MODEL), w_in.astype(BF16), invf)
    return outs


S5_TS = 128
S5_BATCH = 4
S5_COLS = 512


def _s5_kernel(u_ref, bre_ref, bim_ref, a1r_ref, a1i_ref, pr_ref, pi_ref,
               cre_ref, cim_ref, d_ref, wglu_ref, y_ref,
               xr, xi, cr, ci, ysc):
    rows = xr.shape[0]
    ts = rows // S5_BATCH

    @pl.when(pl.program_id(0) == 0)
    def _():
        cr[...] = jnp.zeros_like(cr)
        ci[...] = jnp.zeros_like(ci)

    u = u_ref[...]
    xr[...] = _dot(u, bre_ref[...])
    xi[...] = _dot(u, bim_ref[...])

    hi_rows = lax.broadcasted_iota(jnp.int32, (SUBLANES, S5_COLS), 0) >= S5_BATCH
    for cb in range(D_STATE // S5_COLS):
        sl = slice(cb * S5_COLS, (cb + 1) * S5_COLS)
        a_r, a_i = a1r_ref[:, sl], a1i_ref[:, sl]
        p_r, p_i = pr_ref[:, sl], pi_ref[:, sl]

        def body(t, carry):
            c_r, c_i = carry
            r0 = pl.multiple_of(t * SUBLANES, SUBLANES)
            x_r = xr[pl.ds(r0, SUBLANES), sl]
            x_i = xi[pl.ds(r0, SUBLANES), sl]
            s_r = pltpu.roll(x_r, S5_BATCH, 0)
            s_i = pltpu.roll(x_i, S5_BATCH, 0)
            h_r = x_r + (a_r * s_r - a_i * s_i) + (p_r * c_r - p_i * c_i)
            h_i = x_i + (a_r * s_i + a_i * s_r) + (p_r * c_i + p_i * c_r)
            xr[pl.ds(r0, SUBLANES), sl] = h_r
            xi[pl.ds(r0, SUBLANES), sl] = h_i
            n_r = jnp.where(hi_rows, h_r, pltpu.roll(h_r, S5_BATCH, 0))
            n_i = jnp.where(hi_rows, h_i, pltpu.roll(h_i, S5_BATCH, 0))
            return n_r, n_i

        c_r, c_i = lax.fori_loop(0, rows // SUBLANES, body, (cr[:, sl], ci[:, sl]), unroll=2)
        cr[:, sl] = c_r
        ci[:, sl] = c_i

    y = (_dot(xr[...].astype(BF16), cre_ref[...]) - _dot(xi[...].astype(BF16), cim_ref[...])
         + d_ref[...] * u.astype(F32))
    y = jax.nn.gelu(y)
    y = y * jax.nn.sigmoid(_dot(y.astype(BF16), wglu_ref[...]))
    for c in range(D_SSM // LANES):
        ysc[c] = y[:, c * LANES:(c + 1) * LANES]
    for b in range(S5_BATCH):
        for c in range(D_SSM // LANES):
            y_ref[b, :, c * LANES:(c + 1) * LANES] = (
                ysc[c, pl.ds(b, ts, stride=S5_BATCH), :].astype(BF16))


def _s5_tables(log_dt, a_re, a_im, b_re, b_im, c_re, c_im):
    dt = jnp.exp(log_dt.astype(F32))[:, None]
    ar, ai = a_re.astype(F32), a_im.astype(F32)
    mag = jnp.exp(dt * ar)
    abar_re, abar_im = mag * jnp.cos(dt * ai), mag * jnp.sin(dt * ai)
    den = ar * ar + ai * ai
    nr, ni = abar_re - 1.0, abar_im
    f_re = (nr * ar + ni * ai) / den
    f_im = (ni * ar - nr * ai) / den
    br, bi = b_re.astype(F32), b_im.astype(F32)
    bb_re = f_re[..., None] * br - f_im[..., None] * bi
    bb_im = f_re[..., None] * bi + f_im[..., None] * br
    eye = jnp.eye(SSM_GROUPS, dtype=F32)

    def in_blockdiag(bb):
        return jnp.einsum('gnc,gh->gchn', bb, eye).reshape(D_SSM, D_STATE)

    def out_blockdiag(c):
        return jnp.einsum('gcn,gh->gnhc', c.astype(F32), eye).reshape(D_STATE, D_SSM)

    a_r = abar_re.reshape(1, D_STATE)
    a_i = abar_im.reshape(1, D_STATE)
    a2_r = a_r * a_r - a_i * a_i
    a2_i = 2.0 * a_r * a_i
    hi = (jnp.arange(SUBLANES) >= S5_BATCH)[:, None]
    a1r = jnp.where(hi, a_r, 0.0)
    a1i = jnp.where(hi, a_i, 0.0)
    p_r = jnp.where(hi, a2_r, a_r)
    p_i = jnp.where(hi, a2_i, a_i)
    return (in_blockdiag(bb_re).astype(BF16), in_blockdiag(bb_im).astype(BF16),
            a1r, a1i, p_r, p_i,
            out_blockdiag(c_re).astype(BF16), out_blockdiag(c_im).astype(BF16))


def _s5(u_sb, tables, d_skip, w_glu, B, S):
    assert B == S5_BATCH
    ts = min(S5_TS, S)
    rows = ts * B
    bre, bim, a1r, a1i, p_r, p_i, cre, cim = tables
    full = lambda shape: pl.BlockSpec(shape, lambda i: (0,) * len(shape))
    return pl.pallas_call(
        _s5_kernel,
        grid=(S // ts,),
        in_specs=[pl.BlockSpec((rows, D_SSM), lambda i: (i, 0)),
                  full((D_SSM, D_STATE)), full((D_SSM, D_STATE)),
                  full((SUBLANES, D_STATE)), full((SUBLANES, D_STATE)),
                  full((SUBLANES, D_STATE)), full((SUBLANES, D_STATE)),
                  full((D_STATE, D_SSM)), full((D_STATE, D_SSM)),
                  full((1, D_SSM)), full((D_SSM, D_SSM))],
        out_specs=pl.BlockSpec((B, ts, D_SSM), lambda i: (0, i, 0)),
        out_shape=jax.ShapeDtypeStruct((B, S, D_SSM), BF16),
        scratch_shapes=[pltpu.VMEM((rows, D_STATE), F32), pltpu.VMEM((rows, D_STATE), F32),
                        pltpu.VMEM((SUBLANES, D_STATE), F32), pltpu.VMEM((SUBLANES, D_STATE), F32),
                        pltpu.VMEM((D_SSM // LANES, rows, LANES), F32)],
        compiler_params=pltpu.CompilerParams(
            dimension_semantics=("arbitrary",), vmem_limit_bytes=VMEM_LIMIT),
        name="s5",
    )(u_sb.reshape(S * B, D_SSM), bre, bim, a1r, a1i, p_r, p_i, cre, cim,
      d_skip.reshape(1, D_SSM).astype(F32), w_glu.astype(BF16))


MOBA_PAIR = 2 * MOBA_BLOCK


def _moba_kernel(q_ref, k_ref, v_ref, o_ref, kmean, kaug_a, kaug_b, vaug_a, vaug_b, m_s, acc_s):
    qi = pl.program_id(2)
    nb = k_ref.shape[0] // MOBA_BLOCK
    nbp = kmean.shape[0]
    lane = lax.broadcasted_iota(jnp.int32, (1, LANES), 1)
    head_a = lane < HEAD_DIM

    @pl.when(qi == 0)
    def _():
        kmean[...] = jnp.zeros_like(kmean)
        for j in range(nb):
            rows = pl.ds(j * MOBA_BLOCK, MOBA_BLOCK)
            kj = k_ref[rows, :].astype(F32)
            vj = v_ref[rows, :].astype(F32)
            kmean[j:j + 1, :] = jnp.sum(kj, axis=0, keepdims=True) * (1.0 / MOBA_BLOCK)
            kaug_a[rows, :] = jnp.where(head_a, kj, jnp.where(lane - HEAD_DIM == j, 1.0, 0.0)).astype(BF16)
            kaug_b[rows, :] = jnp.where(head_a, jnp.where(lane == j, 1.0, 0.0), kj).astype(BF16)
            vaug_a[rows, :] = jnp.where(head_a, vj, 1.0).astype(BF16)
            vaug_b[rows, :] = jnp.where(head_a, 1.0, vj).astype(BF16)

    qf = q_ref[...].astype(F32)
    blk_row = lax.broadcasted_iota(jnp.int32, (nbp, MOBA_BLOCK), 0)
    q_augs = []
    for is_a in (True, False):
        mine = head_a if is_a else jnp.logical_not(head_a)
        q_own = jnp.where(mine, qf, 0.0)
        g = _dot_nt(kmean[...], q_own, precision=HIGHEST)
        g = jnp.where(blk_row < qi, g, NEG)
        sel = jnp.zeros(g.shape, F32)
        for _ in range(MOBA_TOPK):
            m = jnp.max(g, axis=0, keepdims=True)
            idx = jnp.min(jnp.where(g == m, blk_row, nbp), axis=0, keepdims=True)
            hit = blk_row == idx
            sel = jnp.where(hit, jnp.where(idx < qi, 1.0, 0.0), sel)
            g = jnp.where(hit, -jnp.inf, g)
        bias_t = jnp.where(sel > 0.0, 0.0, jnp.where(blk_row == qi, 0.0, NEG))
        bias_t = jnp.concatenate([bias_t, jnp.full((LANES - nbp, MOBA_BLOCK), NEG, F32)], axis=0)
        bias = jnp.transpose(bias_t)
        if is_a:
            bias = pltpu.roll(bias, HEAD_DIM, 1)
        q_augs.append(jnp.where(mine, qf, bias).astype(BF16))

    m_s[...] = jnp.full(m_s.shape, -jnp.inf, F32)
    acc_s[...] = jnp.zeros_like(acc_s)
    qpos = qi * MOBA_BLOCK + lax.broadcasted_iota(jnp.int32, (MOBA_BLOCK, MOBA_PAIR), 0)
    col = lax.broadcasted_iota(jnp.int32, (MOBA_BLOCK, MOBA_PAIR), 1)

    def step(jj, causal):
        rows = pl.ds(pl.multiple_of(jj * MOBA_PAIR, MOBA_PAIR), MOBA_PAIR)
        for hd, (kaug, vaug) in enumerate(((kaug_a, vaug_a), (kaug_b, vaug_b))):
            s = _dot_nt(q_augs[hd], kaug[rows, :])
            if causal:
                s = jnp.where(jj * MOBA_PAIR + col <= qpos, s, NEG)
            m_old = m_s[hd]
            m_new = jnp.maximum(m_old, jnp.max(s, axis=-1, keepdims=True))
            alpha = jnp.exp(m_old - m_new)
            p = jnp.exp(s - m_new)
            m_s[hd] = m_new
            acc_s[hd] = alpha * acc_s[hd] + _dot(p.astype(BF16), vaug[rows, :])

    def body(jj, _):
        step(jj, False)
        return 0

    lax.fori_loop(0, qi // 2, body, 0)
    step(qi // 2, True)
    acc_a, acc_b = acc_s[0], acc_s[1]
    o_ref[...] = jnp.where(head_a, acc_a / pltpu.roll(acc_a, HEAD_DIM, 1),
                           acc_b / pltpu.roll(acc_b, HEAD_DIM, 1)).astype(BF16)


def _moba(q, k, v):
    B, S, _ = q.shape
    nq = S // MOBA_BLOCK
    assert nq <= HEAD_DIM and nq % 2 == 0
    nbp = -(-nq // SUBLANES) * SUBLANES
    blk = pl.BlockSpec((None, MOBA_BLOCK, LANES), lambda b, h, i: (b, i, h))
    seq = pl.BlockSpec((None, S, LANES), lambda b, h, i: (b, 0, h))
    return pl.pallas_call(
        _moba_kernel,
        grid=(B, D_ATT // LANES, nq),
        in_specs=[blk, seq, seq],
        out_specs=blk,
        out_shape=jax.ShapeDtypeStruct((B, S, D_ATT), BF16),
        scratch_shapes=[pltpu.VMEM((nbp, LANES), F32),
                        pltpu.VMEM((S, LANES), BF16), pltpu.VMEM((S, LANES), BF16),
                        pltpu.VMEM((S, LANES), BF16), pltpu.VMEM((S, LANES), BF16),
                        pltpu.VMEM((2, MOBA_BLOCK, 1), F32),
                        pltpu.VMEM((2, MOBA_BLOCK, LANES), F32)],
        compiler_params=pltpu.CompilerParams(
            dimension_semantics=("parallel", "parallel", "arbitrary"), vmem_limit_bytes=VMEM_LIMIT),
        name="moba",
    )(q, k, v)


MERGE_TS = 256


def _merge_kernel(x_ref, ys_ref, at_ref, ga_ref, gb_ref, wa_ref, wb_ref, wo_ref, g_ref,
                  wq_ref, k1_ref, k2_ref, x1_ref, hq_ref, sc_ref):
    ya = _dot(ys_ref[...], wa_ref[...])
    yb = _dot(at_ref[...], wb_ref[...])
    merged = ga_ref[...].astype(F32) * ya + gb_ref[...].astype(F32) * yb
    x1 = x_ref[...] + _dot(merged.astype(BF16), wo_ref[...])
    x1_ref[...] = x1
    hq = _rms(x1, g_ref[...])
    hq_ref[...] = hq
    qp = _dot(hq.astype(BF16), wq_ref[...])
    for h in range(PEER_HEADS):
        o = h * PEER_QDIM
        sc_ref[2 * h] = _dot_nt(k1_ref[h], qp[:, o:o + PEER_HALF], precision=HIGHEST)
        sc_ref[2 * h + 1] = _dot_nt(k2_ref[h], qp[:, o + PEER_HALF:o + PEER_QDIM], precision=HIGHEST)


def _merge(x2d, ys, att, ga, gb, w_proj_ssm, w_proj_att, w_out, g_ffn, peer_w_q, keys1, keys2):
    T = x2d.shape[0]
    ts = min(MERGE_TS, T)
    tok = lambda d: pl.BlockSpec((ts, d), lambda i: (i, 0))
    full = lambda shape: pl.BlockSpec(shape, lambda i: (0,) * len(shape))
    qd = PEER_HEADS * PEER_QDIM
    return pl.pallas_call(
        _merge_kernel,
        grid=(T // ts,),
        in_specs=[tok(D_MODEL), tok(D_SSM), tok(D_ATT), tok(D_MODEL), tok(D_MODEL),
                  full((D_SSM, D_MODEL)), full((D_ATT, D_MODEL)), full((D_MODEL, D_MODEL)),
                  full((1, D_MODEL)), full((D_MODEL, qd)),
                  full((PEER_HEADS, PEER_KEYS, PEER_HALF)), full((PEER_HEADS, PEER_KEYS, PEER_HALF))],
        out_specs=[tok(D_MODEL), tok(D_MODEL),
                   pl.BlockSpec((2 * PEER_HEADS, PEER_KEYS, ts), lambda i: (0, 0, i))],
        out_shape=[jax.ShapeDtypeStruct((T, D_MODEL), F32),
                   jax.ShapeDtypeStruct((T, D_MODEL), F32),
                   jax.ShapeDtypeStruct((2 * PEER_HEADS, PEER_KEYS, T), F32)],
        compiler_params=pltpu.CompilerParams(
            dimension_semantics=("parallel",), vmem_limit_bytes=VMEM_LIMIT),
        name="merge",
    )(x2d, ys, att, ga, gb, w_proj_ssm.astype(BF16), w_proj_att.astype(BF16), w_out.astype(BF16),
      g_ffn.reshape(1, D_MODEL), peer_w_q.astype(BF16), keys1, keys2)


TOPK_TS = 256


def _top_rows(s, row, k):
    vals, idxs = [], []
    for _ in range(k):
        m = jnp.max(s, axis=0, keepdims=True)
        idx = jnp.min(jnp.where(s == m, row, s.shape[0]), axis=0, keepdims=True)
        vals.append(m)
        idxs.append(idx)
        s = jnp.where(row == idx, -jnp.inf, s)
    return vals, idxs


def _stack_rows(rows, row16):
    acc = jnp.zeros(row16.shape, rows[0].dtype)
    for r, v in enumerate(rows):
        acc = jnp.where(row16 == r, v, acc)
    return acc


def _topk_kernel(sc_ref, idx_ref, gate_ref):
    ts = sc_ref.shape[-1]
    row = lax.broadcasted_iota(jnp.int32, (PEER_KEYS, ts), 0)
    row16 = lax.broadcasted_iota(jnp.int32, (PEER_TOPK, ts), 0)
    row8 = lax.broadcasted_iota(jnp.int32, (SUBLANES, ts), 0)
    counts = [PEER_TOPK // (i + 1) for i in range(PEER_TOPK)]
    heights = [PEER_TOPK if c > SUBLANES else SUBLANES for c in counts]
    n_cand = sum(heights)
    rowc = lax.broadcasted_iota(jnp.int32, (n_cand, ts), 0)
    gate_rows, eid_rows = [], []
    for h in range(PEER_HEADS):
        v1, i1 = _top_rows(sc_ref[2 * h], row, PEER_TOPK)
        v2, i2 = _top_rows(sc_ref[2 * h + 1], row, PEER_TOPK)
        v2s = _stack_rows(v2, row16)
        i2s = _stack_rows(i2, row16).astype(F32)
        cand, eid = [], []
        for i in range(PEER_TOPK):
            n = heights[i]
            cand.append(jnp.where((row16 if n == PEER_TOPK else row8) < counts[i],
                                  v1[i] + v2s[:n], -jnp.inf))
            eid.append(i1[i].astype(F32) * PEER_KEYS + i2s[:n])
        cand = jnp.concatenate(cand, axis=0)
        eid = jnp.concatenate(eid, axis=0)
        tops, picks = [], []
        for _ in range(PEER_TOPK):
            m = jnp.max(cand, axis=0, keepdims=True)
            pos = jnp.min(jnp.where(cand == m, rowc, n_cand), axis=0, keepdims=True)
            hit = rowc == pos
            picks.append(jnp.max(jnp.where(hit, eid, -1.0), axis=0, keepdims=True))
            tops.append(m)
            cand = jnp.where(hit, -jnp.inf, cand)
        top = _stack_rows(tops, row16)
        p = jnp.exp(top - jnp.max(top, axis=0, keepdims=True))
        gate_rows.append(p / jnp.sum(p, axis=0, keepdims=True))
        eid_rows.append(_stack_rows(picks, row16))
    gate_ref[...] = jnp.transpose(jnp.concatenate(gate_rows, axis=0))
    idx_ref[...] = jnp.transpose(jnp.concatenate(eid_rows, axis=0)).astype(jnp.int32)


def _topk(scores):
    T = scores.shape[-1]
    ts = min(TOPK_TS, T)
    return pl.pallas_call(
        _topk_kernel,
        grid=(T // ts,),
        in_specs=[pl.BlockSpec((2 * PEER_HEADS, PEER_KEYS, ts), lambda i: (0, 0, i))],
        out_specs=[pl.BlockSpec((ts, PEER_SEL), lambda i: (i, 0)),
                   pl.BlockSpec((ts, PEER_SEL), lambda i: (i, 0))],
        out_shape=[jax.ShapeDtypeStruct((T, PEER_SEL), jnp.int32),
                   jax.ShapeDtypeStruct((T, PEER_SEL), F32)],
        compiler_params=pltpu.CompilerParams(
            dimension_semantics=("parallel",), vmem_limit_bytes=VMEM_LIMIT),
        name="topk",
    )(scores)


SC_CORES = 2
SC_SUBCORES = 16
SC_LANES = 16
SC_WORKERS = SC_CORES * SC_SUBCORES
PEER_CH = SC_LANES
PEER_NCH = PEER_SEL // PEER_CH
PEER_NCOL = D_MODEL // SC_LANES
GELU_C = 0.7978845608028654


def _gelu_tanh_via_exp(x):
    z = GELU_C * (x + 0.044715 * (x * x * x))
    t = 1.0 - 2.0 / (jnp.exp(2.0 * z) + 1.0)
    return 0.5 * x * (1.0 + t)


def _peer_sc_body(idx_hbm, gate_hbm, h_hbm, u_hbm, v_hbm, o_hbm,
                  idx_v, gate_v, h_v, ubuf, vbuf, pbuf, w_v, out_v, usem, vsem):
    tok_per_w = o_hbm.shape[0] // SC_WORKERS
    base = (lax.axis_index("s") * SC_CORES + lax.axis_index("c")) * tok_per_w
    lane = lax.iota(jnp.int32, SC_LANES)

    def start(tab, buf, sem, c):
        rows = idx_v[pl.ds(c * PEER_CH, PEER_CH)]
        return pltpu.async_copy(tab.at[rows], buf.at[c % 2], sem.at[c % 2])

    def token(t, carry):
        tok = base + t
        pltpu.sync_copy(idx_hbm.at[tok], idx_v)
        pltpu.sync_copy(gate_hbm.at[tok], gate_v)
        pltpu.sync_copy(h_hbm.at[tok], h_v)
        ucp = [start(u_hbm, ubuf, usem, 0), start(u_hbm, ubuf, usem, 1)]
        vcp = [start(v_hbm, vbuf, vsem, 0), start(v_hbm, vbuf, vsem, 1)]
        for c in range(PEER_NCH):
            b = c % 2
            ucp[b].wait()

            def dot_step(cc, accs):
                col = pl.ds(pl.multiple_of(cc * SC_LANES, SC_LANES), SC_LANES)
                hv = h_v[col]
                return tuple(accs[r] + ubuf[b, r, col] * hv for r in range(PEER_CH))

            accs = lax.fori_loop(0, PEER_NCOL, dot_step,
                                 tuple(jnp.zeros((SC_LANES,), F32) for _ in range(PEER_CH)))
            if c + 2 < PEER_NCH:
                ucp[b] = start(u_hbm, ubuf, usem, c + 2)
            for r in range(PEER_CH):
                pbuf[r, :] = accs[r]
            s = jnp.zeros((SC_LANES,), F32)
            for j in range(SC_LANES):
                s = s + plsc.load_gather(pbuf, [lane, jnp.full((SC_LANES,), j, jnp.int32)])
            rows = pl.ds(c * PEER_CH, PEER_CH)
            w_v[rows] = gate_v[rows] * _gelu_tanh_via_exp(s)
        for c in range(PEER_NCH):
            b = c % 2
            vcp[b].wait()
            ws = [plsc.load_gather(w_v, [jnp.full((SC_LANES,), c * PEER_CH + r, jnp.int32)])
                  for r in range(PEER_CH)]

            def acc_step(cc, _):
                col = pl.ds(pl.multiple_of(cc * SC_LANES, SC_LANES), SC_LANES)
                o = jnp.zeros((SC_LANES,), F32) if c == 0 else out_v[col]
                for r in range(PEER_CH):
                    o = o + ws[r] * vbuf[b, r, col]
                out_v[col] = o
                return 0

            lax.fori_loop(0, PEER_NCOL, acc_step, 0)
            if c + 2 < PEER_NCH:
                vcp[b] = start(v_hbm, vbuf, vsem, c + 2)
        pltpu.sync_copy(out_v, o_hbm.at[tok])
        return carry

    lax.fori_loop(0, tok_per_w, token, 0)


def _peer(idx, hq, gates, peer_u, peer_v):
    T = hq.shape[0]
    assert T % SC_WORKERS == 0
    mesh = plsc.VectorSubcoreMesh(core_axis_name="c", subcore_axis_name="s",
                                  num_cores=SC_CORES, num_subcores=SC_SUBCORES)
    return pl.kernel(
        _peer_sc_body,
        out_type=jax.ShapeDtypeStruct((T, D_MODEL), F32),
        mesh=mesh,
        scratch_types=[
            pltpu.VMEM((PEER_SEL,), jnp.int32), pltpu.VMEM((PEER_SEL,), F32),
            pltpu.VMEM((D_MODEL,), F32),
            pltpu.VMEM((2, PEER_CH, D_MODEL), F32), pltpu.VMEM((2, PEER_CH, D_MODEL), F32),
            pltpu.VMEM((PEER_CH, SC_LANES), F32), pltpu.VMEM((PEER_SEL,), F32),
            pltpu.VMEM((D_MODEL,), F32),
            pltpu.SemaphoreType.DMA((2,)), pltpu.SemaphoreType.DMA((2,)),
        ],
        compiler_params=pltpu.CompilerParams(needs_layout_passes=False),
        name="peer_sc",
    )(idx, gates, hq, peer_u, peer_v)


FINAL_TS = 256


def _final_kernel(x1_ref, pe_ref, p_ref, gp_ref, wg_ref, wp_ref, gf_ref, o_ref):
    x2 = x1_ref[...] + pe_ref[...]
    e = _dot(p_ref[...].astype(BF16), wp_ref[...])
    gate = jax.nn.sigmoid(_dot(_rms(x2, gp_ref[...]).astype(BF16), wg_ref[...]))
    o_ref[...] = _rms(x2 + gate * e, gf_ref[...])


def _final(x1, peer_out, p2d, g_ple, ple_w_gate, ple_w_proj, g_final):
    T = x1.shape[0]
    ts = min(FINAL_TS, T)
    tok = lambda d: pl.BlockSpec((ts, d), lambda i: (i, 0))
    full = lambda shape: pl.BlockSpec(shape, lambda i: (0,) * len(shape))
    return pl.pallas_call(
        _final_kernel,
        grid=(T // ts,),
        in_specs=[tok(D_MODEL), tok(D_MODEL), tok(D_PLE), full((1, D_MODEL)),
                  full((D_MODEL, D_MODEL)), full((D_PLE, D_MODEL)), full((1, D_MODEL))],
        out_specs=tok(D_MODEL),
        out_shape=jax.ShapeDtypeStruct((T, D_MODEL), F32),
        compiler_params=pltpu.CompilerParams(
            dimension_semantics=("parallel",), vmem_limit_bytes=VMEM_LIMIT),
        name="final",
    )(x1, peer_out, p2d, g_ple.reshape(1, D_MODEL), ple_w_gate.astype(BF16),
      ple_w_proj.astype(BF16), g_final.reshape(1, D_MODEL))


def kernel(x, p, positions, g_mix, w_in, ssm_log_dt, ssm_a_re, ssm_a_im, ssm_b_re, ssm_b_im,
           ssm_c_re, ssm_c_im, ssm_d, ssm_w_glu, w_proj_ssm, w_proj_att, w_out, g_ffn,
           peer_w_q, peer_keys1, peer_keys2, peer_u, peer_v, g_ple, ple_w_gate, ple_w_proj,
           g_final):
    B, S, _ = x.shape
    T = B * S
    assert w_in.shape[0] == 1, "the final rmsnorm is fused into the single layer's last stage"
    for i in range(1):
        u_sb, q, k, v, ga, gb = _in_proj(x, positions, g_mix[i], w_in[i])
        tables = _s5_tables(ssm_log_dt[i], ssm_a_re[i], ssm_a_im[i], ssm_b_re[i], ssm_b_im[i],
                            ssm_c_re[i], ssm_c_im[i])
        ys = _s5(u_sb, tables, ssm_d[i], ssm_w_glu[i], B, S)
        att = _moba(q, k, v)
        x1, hq, scores = _merge(
            x.reshape(T, D_MODEL), ys.reshape(T, D_SSM), att.reshape(T, D_ATT),
            ga.reshape(T, D_MODEL), gb.reshape(T, D_MODEL),
            w_proj_ssm[i], w_proj_att[i], w_out[i], g_ffn[i], peer_w_q[i],
            peer_keys1[i], peer_keys2[i])
        idx, gates = _topk(scores)
        peer_out = _peer(idx, hq, gates, peer_u[i], peer_v[i])
        x = _final(x1, peer_out, p[i].reshape(T, D_PLE), g_ple[i], ple_w_gate[i],
                   ple_w_proj[i], g_final).reshape(B, S, D_MODEL)
    return x
```

```python
import functools
import math

import jax
import jax.numpy as jnp
from jax import lax
from jax.experimental import pallas as pl
from jax.experimental.pallas import tpu as pltpu
from jax.experimental.pallas import tpu_sc as plsc

F32 = jnp.float32
BF16 = jnp.bfloat16

D_MODEL = 1024
D_SSM = 512
SSM_GROUP = 16
SSM_GROUPS = 32
SSM_STATE = 64
D_STATE = SSM_GROUPS * SSM_STATE
N_HEADS = 8
HEAD_DIM = 64
D_ATT = 512
ROT_DIM = 16
ROPE_THETA = 500000.0
MOBA_BLOCK = 256
MOBA_TOPK = 3
PEER_HEADS = 8
PEER_KEYS = 128
PEER_QDIM = 256
PEER_HALF = 128
PEER_TOPK = 16
PEER_SEL = PEER_HEADS * PEER_TOPK
D_PLE = 256
EPS = 1e-6
NEG = -1e30
LANES = 128
SUBLANES = 8
VMEM_LIMIT = 48 * 1024 * 1024
HIGHEST = lax.Precision.HIGHEST


def _rms(x, g):
    return x * lax.rsqrt(jnp.mean(x * x, axis=-1, keepdims=True) + EPS) * g


def _dot(a, b):
    return jnp.dot(a, b, preferred_element_type=F32)


def _dot_nt(a, b, precision=None):
    return lax.dot_general(a, b, (((1,), (1,)), ((), ())), precision=precision,
                           preferred_element_type=F32)


IN_TS = 512


def _in_proj_kernel(x_ref, pos_ref, g_ref, w_ref, invf_ref,
                    u_ref, q_ref, k_ref, v_ref, ga_ref, gb_ref):
    h = _rms(x_ref[...], g_ref[...]).astype(BF16)

    def proj(lo, hi):
        return _dot(h, w_ref[:, lo:hi])

    u_ref[...] = proj(0, D_SSM).astype(BF16)
    ang = pos_ref[...].astype(F32) * invf_ref[...]
    cos = jnp.cos(ang)
    sin = jnp.sin(ang)
    lane = lax.broadcasted_iota(jnp.int32, (1, LANES), 1) % HEAD_DIM
    half = ROT_DIM // 2
    sin_hi = jnp.where((lane >= half) & (lane < ROT_DIM), sin, 0.0)
    sin_lo = jnp.where(lane < half, -sin, 0.0)
    reps = D_ATT // LANES
    cos4 = jnp.concatenate([cos] * reps, axis=1)
    sin_hi4 = jnp.concatenate([sin_hi] * reps, axis=1)
    sin_lo4 = jnp.concatenate([sin_lo] * reps, axis=1)

    def rope(t):
        return (t * cos4 + pltpu.roll(t, half, 1) * sin_hi4
                + pltpu.roll(t, D_ATT - half, 1) * sin_lo4)

    q = rope(proj(D_SSM, D_SSM + D_ATT))
    q_ref[...] = (q * (HEAD_DIM ** -0.5)).astype(BF16)
    k_ref[...] = rope(proj(D_SSM + D_ATT, D_SSM + 2 * D_ATT)).astype(BF16)
    v_ref[...] = proj(D_SSM + 2 * D_ATT, D_SSM + 3 * D_ATT).astype(BF16)
    o = D_SSM + 3 * D_ATT
    ga_ref[...] = jax.nn.sigmoid(proj(o, o + D_MODEL)).astype(BF16)
    gb_ref[...] = jax.nn.sigmoid(proj(o + D_MODEL, o + 2 * D_MODEL)).astype(BF16)


def _in_proj(x, positions, g_mix, w_in):
    B, S, _ = x.shape
    ts = min(IN_TS, S)
    inv_freq = ROPE_THETA ** (-jnp.arange(0, ROT_DIM, 2, dtype=F32) / ROT_DIM)
    lane = jnp.arange(LANES) % HEAD_DIM
    invf = jnp.where(lane < ROT_DIM, inv_freq[lane % (ROT_DIM // 2)], 0.0).reshape(1, LANES)
    d_in = w_in.shape[1]
    tok = lambda d: pl.BlockSpec((None, ts, d), lambda b, i: (b, i, 0))
    full = lambda shape: pl.BlockSpec(shape, lambda b, i: (0,) * len(shape))
    outs = pl.pallas_call(
        _in_proj_kernel,
        grid=(B, S // ts),
        in_specs=[tok(D_MODEL), tok(1), full((1, D_MODEL)), full((D_MODEL, d_in)), full((1, LANES))],
        out_specs=[pl.BlockSpec((ts, D_SSM), lambda b, i: (i, b)),
                   tok(D_ATT), tok(D_ATT), tok(D_ATT), tok(D_MODEL), tok(D_MODEL)],
        out_shape=[jax.ShapeDtypeStruct((S, B * D_SSM), BF16),
                   jax.ShapeDtypeStruct((B, S, D_ATT), BF16),
                   jax.ShapeDtypeStruct((B, S, D_ATT), BF16),
                   jax.ShapeDtypeStruct((B, S, D_ATT), BF16),
                   jax.ShapeDtypeStruct((B, S, D_MODEL), BF16),
                   jax.ShapeDtypeStruct((B, S, D_MODEL), BF16)],
        compiler_params=pltpu.CompilerParams(
            dimension_semantics=("parallel", "parallel"), vmem_limit_bytes=VMEM_LIMIT),
        name="in_proj",
    )(x, positions.reshape(B, S, 1), g_mix.reshape(1, D_MODEL), w_in.astype(BF16), invf)
    return outs


S5_TS = 128
S5_BATCH = 4
S5_COLS = 512


def _s5_kernel(u_ref, bre_ref, bim_ref, a1r_ref, a1i_ref, pr_ref, pi_ref,
               cre_ref, cim_ref, d_ref, wglu_ref, y_ref,
               xr, xi, cr, ci, ysc):
    rows = xr.shape[0]
    ts = rows // S5_BATCH

    @pl.when(pl.program_id(0) == 0)
    def _():
        cr[...] = jnp.zeros_like(cr)
        ci[...] = jnp.zeros_like(ci)

    u = u_ref[...]
    xr[...] = _dot(u, bre_ref[...])
    xi[...] = _dot(u, bim_ref[...])

    hi_rows = lax.broadcasted_iota(jnp.int32, (SUBLANES, S5_COLS), 0) >= S5_BATCH
    for cb in range(D_STATE // S5_COLS):
        sl = slice(cb * S5_COLS, (cb + 1) * S5_COLS)
        a_r, a_i = a1r_ref[:, sl], a1i_ref[:, sl]
        p_r, p_i = pr_ref[:, sl], pi_ref[:, sl]

        def body(t, carry):
            c_r, c_i = carry
            r0 = pl.multiple_of(t * SUBLANES, SUBLANES)
            x_r = xr[pl.ds(r0, SUBLANES), sl]
            x_i = xi[pl.ds(r0, SUBLANES), sl]
            s_r = pltpu.roll(x_r, S5_BATCH, 0)
            s_i = pltpu.roll(x_i, S5_BATCH, 0)
            h_r = x_r + (a_r * s_r - a_i * s_i) + (p_r * c_r - p_i * c_i)
            h_i = x_i + (a_r * s_i + a_i * s_r) + (p_r * c_i + p_i * c_r)
            xr[pl.ds(r0, SUBLANES), sl] = h_r
            xi[pl.ds(r0, SUBLANES), sl] = h_i
            n_r = jnp.where(hi_rows, h_r, pltpu.roll(h_r, S5_BATCH, 0))
            n_i = jnp.where(hi_rows, h_i, pltpu.roll(h_i, S5_BATCH, 0))
            return n_r, n_i

        c_r, c_i = lax.fori_loop(0, rows // SUBLANES, body, (cr[:, sl], ci[:, sl]), unroll=2)
        cr[:, sl] = c_r
        ci[:, sl] = c_i

    y = (_dot(xr[...].astype(BF16), cre_ref[...]) - _dot(xi[...].astype(BF16), cim_ref[...])
         + d_ref[...] * u.astype(F32))
    y = jax.nn.gelu(y)
    y = y * jax.nn.sigmoid(_dot(y.astype(BF16), wglu_ref[...]))
    for c in range(D_SSM // LANES):
        ysc[c] = y[:, c * LANES:(c + 1) * LANES]
    for b in range(S5_BATCH):
        for c in range(D_SSM // LANES):
            y_ref[b, :, c * LANES:(c + 1) * LANES] = (
                ysc[c, pl.ds(b, ts, stride=S5_BATCH), :].astype(BF16))


def _s5_tables(log_dt, a_re, a_im, b_re, b_im, c_re, c_im):
    dt = jnp.exp(log_dt.astype(F32))[:, None]
    ar, ai = a_re.astype(F32), a_im.astype(F32)
    mag = jnp.exp(dt * ar)
    abar_re, abar_im = mag * jnp.cos(dt * ai), mag * jnp.sin(dt * ai)
    den = ar * ar + ai * ai
    nr, ni = abar_re - 1.0, abar_im
    f_re = (nr * ar + ni * ai) / den
    f_im = (ni * ar - nr * ai) / den
    br, bi = b_re.astype(F32), b_im.astype(F32)
    bb_re = f_re[..., None] * br - f_im[..., None] * bi
    bb_im = f_re[..., None] * bi + f_im[..., None] * br
    eye = jnp.eye(SSM_GROUPS, dtype=F32)

    def in_blockdiag(bb):
        return jnp.einsum('gnc,gh->gchn', bb, eye).reshape(D_SSM, D_STATE)

    def out_blockdiag(c):
        return jnp.einsum('gcn,gh->gnhc', c.astype(F32), eye).reshape(D_STATE, D_SSM)

    a_r = abar_re.reshape(1, D_STATE)
    a_i = abar_im.reshape(1, D_STATE)
    a2_r = a_r * a_r - a_i * a_i
    a2_i = 2.0 * a_r * a_i
    hi = (jnp.arange(SUBLANES) >= S5_BATCH)[:, None]
    a1r = jnp.where(hi, a_r, 0.0)
    a1i = jnp.where(hi, a_i, 0.0)
    p_r = jnp.where(hi, a2_r, a_r)
    p_i = jnp.where(hi, a2_i, a_i)
    return (in_blockdiag(bb_re).astype(BF16), in_blockdiag(bb_im).astype(BF16),
            a1r, a1i, p_r, p_i,
            out_blockdiag(c_re).astype(BF16), out_blockdiag(c_im).astype(BF16))


def _s5(u_sb, tables, d_skip, w_glu, B, S):
    assert B == S5_BATCH
    ts = min(S5_TS, S)
    rows = ts * B
    bre, bim, a1r, a1i, p_r, p_i, cre, cim = tables
    full = lambda shape: pl.BlockSpec(shape, lambda i: (0,) * len(shape))
    return pl.pallas_call(
        _s5_kernel,
        grid=(S // ts,),
        in_specs=[pl.BlockSpec((rows, D_SSM), lambda i: (i, 0)),
                  full((D_SSM, D_STATE)), full((D_SSM, D_STATE)),
                  full((SUBLANES, D_STATE)), full((SUBLANES, D_STATE)),
                  full((SUBLANES, D_STATE)), full((SUBLANES, D_STATE)),
                  full((D_STATE, D_SSM)), full((D_STATE, D_SSM)),
                  full((1, D_SSM)), full((D_SSM, D_SSM))],
        out_specs=pl.BlockSpec((B, ts, D_SSM), lambda i: (0, i, 0)),
        out_shape=jax.ShapeDtypeStruct((B, S, D_SSM), BF16),
        scratch_shapes=[pltpu.VMEM((rows, D_STATE), F32), pltpu.VMEM((rows, D_STATE), F32),
                        pltpu.VMEM((SUBLANES, D_STATE), F32), pltpu.VMEM((SUBLANES, D_STATE), F32),
                        pltpu.VMEM((D_SSM // LANES, rows, LANES), F32)],
        compiler_params=pltpu.CompilerParams(
            dimension_semantics=("arbitrary",), vmem_limit_bytes=VMEM_LIMIT),
        name="s5",
    )(u_sb.reshape(S * B, D_SSM), bre, bim, a1r, a1i, p_r, p_i, cre, cim,
      d_skip.reshape(1, D_SSM).astype(F32), w_glu.astype(BF16))


MOBA_PAIR = 2 * MOBA_BLOCK


def _moba_kernel(q_ref, k_ref, v_ref, o_ref, kmean, kaug_a, kaug_b, vaug_a, vaug_b, m_s, acc_s):
    qi = pl.program_id(2)
    nb = k_ref.shape[0] // MOBA_BLOCK
    nbp = kmean.shape[0]
    lane = lax.broadcasted_iota(jnp.int32, (1, LANES), 1)
    head_a = lane < HEAD_DIM

    @pl.when(qi == 0)
    def _():
        kmean[...] = jnp.zeros_like(kmean)
        for j in range(nb):
            rows = pl.ds(j * MOBA_BLOCK, MOBA_BLOCK)
            kj = k_ref[rows, :].astype(F32)
            vj = v_ref[rows, :].astype(F32)
            kmean[j:j + 1, :] = jnp.sum(kj, axis=0, keepdims=True) * (1.0 / MOBA_BLOCK)
            kaug_a[rows, :] = jnp.where(head_a, kj, jnp.where(lane - HEAD_DIM == j, 1.0, 0.0)).astype(BF16)
            kaug_b[rows, :] = jnp.where(head_a, jnp.where(lane == j, 1.0, 0.0), kj).astype(BF16)
            vaug_a[rows, :] = jnp.where(head_a, vj, 1.0).astype(BF16)
            vaug_b[rows, :] = jnp.where(head_a, 1.0, vj).astype(BF16)

    qf = q_ref[...].astype(F32)
    blk_row = lax.broadcasted_iota(jnp.int32, (nbp, MOBA_BLOCK), 0)
    q_augs = []
    for is_a in (True, False):
        mine = head_a if is_a else jnp.logical_not(head_a)
        q_own = jnp.where(mine, qf, 0.0)
        g = _dot_nt(kmean[...], q_own, precision=HIGHEST)
        g = jnp.where(blk_row < qi, g, NEG)
        sel = jnp.zeros(g.shape, F32)
        for _ in range(MOBA_TOPK):
            m = jnp.max(g, axis=0, keepdims=True)
            idx = jnp.min(jnp.where(g == m, blk_row, nbp), axis=0, keepdims=True)
            hit = blk_row == idx
            sel = jnp.where(hit, jnp.where(idx < qi, 1.0, 0.0), sel)
            g = jnp.where(hit, -jnp.inf, g)
        bias_t = jnp.where(sel > 0.0, 0.0, jnp.where(blk_row == qi, 0.0, NEG))
        bias_t = jnp.concatenate([bias_t, jnp.full((LANES - nbp, MOBA_BLOCK), NEG, F32)], axis=0)
        bias = jnp.transpose(bias_t)
        if is_a:
            bias = pltpu.roll(bias, HEAD_DIM, 1)
        q_augs.append(jnp.where(mine, qf, bias).astype(BF16))

    m_s[...] = jnp.full(m_s.shape, -jnp.inf, F32)
    acc_s[...] = jnp.zeros_like(acc_s)
    qpos = qi * MOBA_BLOCK + lax.broadcasted_iota(jnp.int32, (MOBA_BLOCK, MOBA_PAIR), 0)
    col = lax.broadcasted_iota(jnp.int32, (MOBA_BLOCK, MOBA_PAIR), 1)

    def step(jj, causal):
        rows = pl.ds(pl.multiple_of(jj * MOBA_PAIR, MOBA_PAIR), MOBA_PAIR)
        for hd, (kaug, vaug) in enumerate(((kaug_a, vaug_a), (kaug_b, vaug_b))):
            s = _dot_nt(q_augs[hd], kaug[rows, :])
            if causal:
                s = jnp.where(jj * MOBA_PAIR + col <= qpos, s, NEG)
            m_old = m_s[hd]
            m_new = jnp.maximum(m_old, jnp.max(s, axis=-1, keepdims=True))
            alpha = jnp.exp(m_old - m_new)
            p = jnp.exp(s - m_new)
            m_s[hd] = m_new
            acc_s[hd] = alpha * acc_s[hd] + _dot(p.astype(BF16), vaug[rows, :])

    def body(jj, _):
        step(jj, False)
        return 0

    lax.fori_loop(0, qi // 2, body, 0)
    step(qi // 2, True)
    acc_a, acc_b = acc_s[0], acc_s[1]
    o_ref[...] = jnp.where(head_a, acc_a / pltpu.roll(acc_a, HEAD_DIM, 1),
                           acc_b / pltpu.roll(acc_b, HEAD_DIM, 1)).astype(BF16)


def _moba(q, k, v):
    B, S, _ = q.shape
    nq = S // MOBA_BLOCK
    assert nq <= HEAD_DIM and nq % 2 == 0
    nbp = -(-nq // SUBLANES) * SUBLANES
    blk = pl.BlockSpec((None, MOBA_BLOCK, LANES), lambda b, h, i: (b, i, h))
    seq = pl.BlockSpec((None, S, LANES), lambda b, h, i: (b, 0, h))
    return pl.pallas_call(
        _moba_kernel,
        grid=(B, D_ATT // LANES, nq),
        in_specs=[blk, seq, seq],
        out_specs=blk,
        out_shape=jax.ShapeDtypeStruct((B, S, D_ATT), BF16),
        scratch_shapes=[pltpu.VMEM((nbp, LANES), F32),
                        pltpu.VMEM((S, LANES), BF16), pltpu.VMEM((S, LANES), BF16),
                        pltpu.VMEM((S, LANES), BF16), pltpu.VMEM((S, LANES), BF16),
                        pltpu.VMEM((2, MOBA_BLOCK, 1), F32),
                        pltpu.VMEM((2, MOBA_BLOCK, LANES), F32)],
        compiler_params=pltpu.CompilerParams(
            dimension_semantics=("parallel", "parallel", "arbitrary"), vmem_limit_bytes=VMEM_LIMIT),
        name="moba",
    )(q, k, v)


MERGE_TS = 256


def _merge_kernel(x_ref, ys_ref, at_ref, ga_ref, gb_ref, wa_ref, wb_ref, wo_ref, g_ref,
                  wq_ref, k1_ref, k2_ref, x1_ref, hq_ref, sc_ref):
    ya = _dot(ys_ref[...], wa_ref[...])
    yb = _dot(at_ref[...], wb_ref[...])
    merged = ga_ref[...].astype(F32) * ya + gb_ref[...].astype(F32) * yb
    x1 = x_ref[...] + _dot(merged.astype(BF16), wo_ref[...])
    x1_ref[...] = x1
    hq = _rms(x1, g_ref[...])
    hq_ref[...] = hq
    qp = _dot(hq.astype(BF16), wq_ref[...])
    for h in range(PEER_HEADS):
        o = h * PEER_QDIM
        sc_ref[2 * h] = _dot_nt(k1_ref[h], qp[:, o:o + PEER_HALF], precision=HIGHEST)
        sc_ref[2 * h + 1] = _dot_nt(k2_ref[h], qp[:, o + PEER_HALF:o + PEER_QDIM], precision=HIGHEST)


def _merge(x2d, ys, att, ga, gb, w_proj_ssm, w_proj_att, w_out, g_ffn, peer_w_q, keys1, keys2):
    T = x2d.shape[0]
    ts = min(MERGE_TS, T)
    tok = lambda d: pl.BlockSpec((ts, d), lambda i: (i, 0))
    full = lambda shape: pl.BlockSpec(shape, lambda i: (0,) * len(shape))
    qd = PEER_HEADS * PEER_QDIM
    return pl.pallas_call(
        _merge_kernel,
        grid=(T // ts,),
        in_specs=[tok(D_MODEL), tok(D_SSM), tok(D_ATT), tok(D_MODEL), tok(D_MODEL),
                  full((D_SSM, D_MODEL)), full((D_ATT, D_MODEL)), full((D_MODEL, D_MODEL)),
                  full((1, D_MODEL)), full((D_MODEL, qd)),
                  full((PEER_HEADS, PEER_KEYS, PEER_HALF)), full((PEER_HEADS, PEER_KEYS, PEER_HALF))],
        out_specs=[tok(D_MODEL), tok(D_MODEL),
                   pl.BlockSpec((2 * PEER_HEADS, PEER_KEYS, ts), lambda i: (0, 0, i))],
        out_shape=[jax.ShapeDtypeStruct((T, D_MODEL), F32),
                   jax.ShapeDtypeStruct((T, D_MODEL), F32),
                   jax.ShapeDtypeStruct((2 * PEER_HEADS, PEER_KEYS, T), F32)],
        compiler_params=pltpu.CompilerParams(
            dimension_semantics=("parallel",), vmem_limit_bytes=VMEM_LIMIT),
        name="merge",
    )(x2d, ys, att, ga, gb, w_proj_ssm.astype(BF16), w_proj_att.astype(BF16), w_out.astype(BF16),
      g_ffn.reshape(1, D_MODEL), peer_w_q.astype(BF16), keys1, keys2)


TOPK_TS = 256


def _top_rows(s, row, k):
    vals, idxs = [], []
    for _ in range(k):
        m = jnp.max(s, axis=0, keepdims=True)
        idx = jnp.min(jnp.where(s == m, row, s.shape[0]), axis=0, keepdims=True)
        vals.append(m)
        idxs.append(idx)
        s = jnp.where(row == idx, -jnp.inf, s)
    return vals, idxs


def _stack_rows(rows, row16):
    acc = jnp.zeros(row16.shape, rows[0].dtype)
    for r, v in enumerate(rows):
        acc = jnp.where(row16 == r, v, acc)
    return acc


def _topk_kernel(sc_ref, idx_ref, gate_ref):
    ts = sc_ref.shape[-1]
    row = lax.broadcasted_iota(jnp.int32, (PEER_KEYS, ts), 0)
    row16 = lax.broadcasted_iota(jnp.int32, (PEER_TOPK, ts), 0)
    row8 = lax.broadcasted_iota(jnp.int32, (SUBLANES, ts), 0)
    counts = [PEER_TOPK // (i + 1) for i in range(PEER_TOPK)]
    heights = [PEER_TOPK if c > SUBLANES else SUBLANES for c in counts]
    n_cand = sum(heights)
    rowc = lax.broadcasted_iota(jnp.int32, (n_cand, ts), 0)
    gate_rows, eid_rows = [], []
    for h in range(PEER_HEADS):
        v1, i1 = _top_rows(sc_ref[2 * h], row, PEER_TOPK)
        v2, i2 = _top_rows(sc_ref[2 * h + 1], row, PEER_TOPK)
        v2s = _stack_rows(v2, row16)
        i2s = _stack_rows(i2, row16).astype(F32)
        cand, eid = [], []
        for i in range(PEER_TOPK):
            n = heights[i]
            cand.append(jnp.where((row16 if n == PEER_TOPK else row8) < counts[i],
                                  v1[i] + v2s[:n], -jnp.inf))
            eid.append(i1[i].astype(F32) * PEER_KEYS + i2s[:n])
        cand = jnp.concatenate(cand, axis=0)
        eid = jnp.concatenate(eid, axis=0)
        tops, picks = [], []
        for _ in range(PEER_TOPK):
            m = jnp.max(cand, axis=0, keepdims=True)
            pos = jnp.min(jnp.where(cand == m, rowc, n_cand), axis=0, keepdims=True)
            hit = rowc == pos
            picks.append(jnp.max(jnp.where(hit, eid, -1.0), axis=0, keepdims=True))
            tops.append(m)
            cand = jnp.where(hit, -jnp.inf, cand)
        top = _stack_rows(tops, row16)
        p = jnp.exp(top - jnp.max(top, axis=0, keepdims=True))
        gate_rows.append(p / jnp.sum(p, axis=0, keepdims=True))
        eid_rows.append(_stack_rows(picks, row16))
    gate_ref[...] = jnp.transpose(jnp.concatenate(gate_rows, axis=0))
    idx_ref[...] = jnp.transpose(jnp.concatenate(eid_rows, axis=0)).astype(jnp.int32)


def _topk(scores):
    T = scores.shape[-1]
    ts = min(TOPK_TS, T)
    return pl.pallas_call(
        _topk_kernel,
        grid=(T // ts,),
        in_specs=[pl.BlockSpec((2 * PEER_HEADS, PEER_KEYS, ts), lambda i: (0, 0, i))],
        out_specs=[pl.BlockSpec((ts, PEER_SEL), lambda i: (i, 0)),
                   pl.BlockSpec((ts, PEER_SEL), lambda i: (i, 0))],
        out_shape=[jax.ShapeDtypeStruct((T, PEER_SEL), jnp.int32),
                   jax.ShapeDtypeStruct((T, PEER_SEL), F32)],
        compiler_params=pltpu.CompilerParams(
            dimension_semantics=("parallel",), vmem_limit_bytes=VMEM_LIMIT),
        name="topk",
    )(scores)


SC_CORES = 2
SC_SUBCORES = 16
SC_LANES = 16
SC_WORKERS = SC_CORES * SC_SUBCORES
PEER_CH = SC_LANES
PEER_NCH = PEER_SEL // PEER_CH
PEER_NCOL = D_MODEL // SC_LANES
GELU_C = 0.7978845608028654


def _gelu_tanh_via_exp(x):
    z = GELU_C * (x + 0.044715 * (x * x * x))
    t = 1.0 - 2.0 / (jnp.exp(2.0 * z) + 1.0)
    return 0.5 * x * (1.0 + t)


def _peer_sc_body(idx_hbm, gate_hbm, h_hbm, u_hbm, v_hbm, o_hbm,
                  idx_v, gate_v, h_v, ubuf, vbuf, pbuf, w_v, out_v, usem, vsem):
    tok_per_w = o_hbm.shape[0] // SC_WORKERS
    base = (lax.axis_index("s") * SC_CORES + lax.axis_index("c")) * tok_per_w
    lane = lax.iota(jnp.int32, SC_LANES)

    def start(tab, buf, sem, c):
        rows = idx_v[pl.ds(c * PEER_CH, PEER_CH)]
        return pltpu.async_copy(tab.at[rows], buf.at[c % 2], sem.at[c % 2])

    def token(t, carry):
        tok = base + t
        pltpu.sync_copy(idx_hbm.at[tok], idx_v)
        pltpu.sync_copy(gate_hbm.at[tok], gate_v)
        pltpu.sync_copy(h_hbm.at[tok], h_v)
        ucp = [start(u_hbm, ubuf, usem, 0), start(u_hbm, ubuf, usem, 1)]
        vcp = [start(v_hbm, vbuf, vsem, 0), start(v_hbm, vbuf, vsem, 1)]
        for c in range(PEER_NCH):
            b = c % 2
            ucp[b].wait()

            def dot_step(cc, accs):
                col = pl.ds(pl.multiple_of(cc * SC_LANES, SC_LANES), SC_LANES)
                hv = h_v[col]
                return tuple(accs[r] + ubuf[b, r, col] * hv for r in range(PEER_CH))

            accs = lax.fori_loop(0, PEER_NCOL, dot_step,
                                 tuple(jnp.zeros((SC_LANES,), F32) for _ in range(PEER_CH)))
            if c + 2 < PEER_NCH:
                ucp[b] = start(u_hbm, ubuf, usem, c + 2)
            for r in range(PEER_CH):
                pbuf[r, :] = accs[r]
            s = jnp.zeros((SC_LANES,), F32)
            for j in range(SC_LANES):
                s = s + plsc.load_gather(pbuf, [lane, jnp.full((SC_LANES,), j, jnp.int32)])
            rows = pl.ds(c * PEER_CH, PEER_CH)
            w_v[rows] = gate_v[rows] * _gelu_tanh_via_exp(s)
        for c in range(PEER_NCH):
            b = c % 2
            vcp[b].wait()
            ws = [plsc.load_gather(w_v, [jnp.full((SC_LANES,), c * PEER_CH + r, jnp.int32)])
                  for r in range(PEER_CH)]

            def acc_step(cc, _):
                col = pl.ds(pl.multiple_of(cc * SC_LANES, SC_LANES), SC_LANES)
                o = jnp.zeros((SC_LANES,), F32) if c == 0 else out_v[col]
                for r in range(PEER_CH):
                    o = o + ws[r] * vbuf[b, r, col]
                out_v[col] = o
                return 0

            lax.fori_loop(0, PEER_NCOL, acc_step, 0)
            if c + 2 < PEER_NCH:
                vcp[b] = start(v_hbm, vbuf, vsem, c + 2)
        pltpu.sync_copy(out_v, o_hbm.at[tok])
        return carry

    lax.fori_loop(0, tok_per_w, token, 0)


def _peer(idx, hq, gates, peer_u, peer_v):
    T = hq.shape[0]
    assert T % SC_WORKERS == 0
    mesh = plsc.VectorSubcoreMesh(core_axis_name="c", subcore_axis_name="s",
                                  num_cores=SC_CORES, num_subcores=SC_SUBCORES)
    return pl.kernel(
        _peer_sc_body,
        out_type=jax.ShapeDtypeStruct((T, D_MODEL), F32),
        mesh=mesh,
        scratch_types=[
            pltpu.VMEM((PEER_SEL,), jnp.int32), pltpu.VMEM((PEER_SEL,), F32),
            pltpu.VMEM((D_MODEL,), F32),
            pltpu.VMEM((2, PEER_CH, D_MODEL), F32), pltpu.VMEM((2, PEER_CH, D_MODEL), F32),
            pltpu.VMEM((PEER_CH, SC_LANES), F32), pltpu.VMEM((PEER_SEL,), F32),
            pltpu.VMEM((D_MODEL,), F32),
            pltpu.SemaphoreType.DMA((2,)), pltpu.SemaphoreType.DMA((2,)),
        ],
        compiler_params=pltpu.CompilerParams(needs_layout_passes=False),
        name="peer_sc",
    )(idx, gates, hq, peer_u, peer_v)


FINAL_TS = 256


def _final_kernel(x1_ref, pe_ref, p_ref, gp_ref, wg_ref, wp_ref, gf_ref, o_ref):
    x2 = x1_ref[...] + pe_ref[...]
    e = _dot(p_ref[...].astype(BF16), wp_ref[...])
    gate = jax.nn.sigmoid(_dot(_rms(x2, gp_ref[...]).astype(BF16), wg_ref[...]))
    o_ref[...] = _rms(x2 + gate * e, gf_ref[...])


def _final(x1, peer_out, p2d, g_ple, ple_w_gate, ple_w_proj, g_final):
    T = x1.shape[0]
    ts = min(FINAL_TS, T)
    tok = lambda d: pl.BlockSpec((ts, d), lambda i: (i, 0))
    full = lambda shape: pl.BlockSpec(shape, lambda i: (0,) * len(shape))
    return pl.pallas_call(
        _final_kernel,
        grid=(T // ts,),
        in_specs=[tok(D_MODEL), tok(D_MODEL), tok(D_PLE), full((1, D_MODEL)),
                  full((D_MODEL, D_MODEL)), full((D_PLE, D_MODEL)), full((1, D_MODEL))],
        out_specs=tok(D_MODEL),
        out_shape=jax.ShapeDtypeStruct((T, D_MODEL), F32),
        compiler_params=pltpu.CompilerParams(
            dimension_semantics=("parallel",), vmem_limit_bytes=VMEM_LIMIT),
        name="final",
    )(x1, peer_out, p2d, g_ple.reshape(1, D_MODEL), ple_w_gate.astype(BF16),
      ple_w_proj.astype(BF16), g_final.reshape(1, D_MODEL))


def kernel(x, p, positions, g_mix, w_in, ssm_log_dt, ssm_a_re, ssm_a_im, ssm_b_re, ssm_b_im,
           ssm_c_re, ssm_c_im, ssm_d, ssm_w_glu, w_proj_ssm, w_proj_att, w_out, g_ffn,
           peer_w_q, peer_keys1, peer_keys2, peer_u, peer_v, g_ple, ple_w_gate, ple_w_proj,
           g_final):
    B, S, _ = x.shape
    T = B * S
    assert w_in.shape[0] == 1, "the final rmsnorm is fused into the single layer's last stage"
    for i in range(1):
        u_sb, q, k, v, ga, gb = _in_proj(x, positions, g_mix[i], w_in[i])
        tables = _s5_tables(ssm_log_dt[i], ssm_a_re[i], ssm_a_im[i], ssm_b_re[i], ssm_b_im[i],
                            ssm_c_re[i], ssm_c_im[i])
        ys = _s5(u_sb, tables, ssm_d[i], ssm_w_glu[i], B, S)
        outs = []
        for b in range(B):
            att = _moba(q[b:b + 1], k[b:b + 1], v[b:b + 1])
            x1, hq, scores = _merge(
                x[b], ys[b], att[0], ga[b], gb[b],
                w_proj_ssm[i], w_proj_att[i], w_out[i], g_ffn[i], peer_w_q[i],
                peer_keys1[i], peer_keys2[i])
            idx, gates = _topk(scores)
            peer_out = _peer(idx, hq, gates, peer_u[i], peer_v[i])
            outs.append(_final(x1, peer_out, p[i, b], g_ple[i], ple_w_gate[i],
                               ple_w_proj[i], g_final))
        x = jnp.stack(outs)
    return x
```

```python
import functools
import math

import jax
import jax.numpy as jnp
from jax import lax
from jax.experimental import pallas as pl
from jax.experimental.pallas import tpu as pltpu
from jax.experimental.pallas import tpu_sc as plsc

F32 = jnp.float32
BF16 = jnp.bfloat16

D_MODEL = 1024
D_SSM = 512
SSM_GROUP = 16
SSM_GROUPS = 32
SSM_STATE = 64
D_STATE = SSM_GROUPS * SSM_STATE
N_HEADS = 8
HEAD_DIM = 64
D_ATT = 512
ROT_DIM = 16
ROPE_THETA = 500000.0
MOBA_BLOCK = 256
MOBA_TOPK = 3
PEER_HEADS = 8
PEER_KEYS = 128
PEER_QDIM = 256
PEER_HALF = 128
PEER_TOPK = 16
PEER_SEL = PEER_HEADS * PEER_TOPK
D_PLE = 256
EPS = 1e-6
NEG = -1e30
LANES = 128
SUBLANES = 8
VMEM_LIMIT = 48 * 1024 * 1024
HIGHEST = lax.Precision.HIGHEST


def _rms(x, g):
    return x * lax.rsqrt(jnp.mean(x * x, axis=-1, keepdims=True) + EPS) * g


def _dot(a, b):
    return jnp.dot(a, b, preferred_element_type=F32)


def _dot_nt(a, b, precision=None):
    return lax.dot_general(a, b, (((1,), (1,)), ((), ())), precision=precision,
                           preferred_element_type=F32)


IN_TS = 512


def _in_proj_kernel(x_ref, pos_ref, g_ref, w_ref, invf_ref,
                    u_ref, q_ref, k_ref, v_ref, ga_ref, gb_ref):
    h = _rms(x_ref[...], g_ref[...]).astype(BF16)

    def proj(lo, hi):
        return _dot(h, w_ref[:, lo:hi])

    u_ref[...] = proj(0, D_SSM).astype(BF16)
    ang = pos_ref[...].astype(F32) * invf_ref[...]
    cos = jnp.cos(ang)
    sin = jnp.sin(ang)
    lane = lax.broadcasted_iota(jnp.int32, (1, LANES), 1) % HEAD_DIM
    half = ROT_DIM // 2
    sin_hi = jnp.where((lane >= half) & (lane < ROT_DIM), sin, 0.0)
    sin_lo = jnp.where(lane < half, -sin, 0.0)
    reps = D_ATT // LANES
    cos4 = jnp.concatenate([cos] * reps, axis=1)
    sin_hi4 = jnp.concatenate([sin_hi] * reps, axis=1)
    sin_lo4 = jnp.concatenate([sin_lo] * reps, axis=1)

    def rope(t):
        return (t * cos4 + pltpu.roll(t, half, 1) * sin_hi4
                + pltpu.roll(t, D_ATT - half, 1) * sin_lo4)

    q = rope(proj(D_SSM, D_SSM + D_ATT))
    q_ref[...] = (q * (HEAD_DIM ** -0.5)).astype(BF16)
    k_ref[...] = rope(proj(D_SSM + D_ATT, D_SSM + 2 * D_ATT)).astype(BF16)
    v_ref[...] = proj(D_SSM + 2 * D_ATT, D_SSM + 3 * D_ATT).astype(BF16)
    o = D_SSM + 3 * D_ATT
    ga_ref[...] = jax.nn.sigmoid(proj(o, o + D_MODEL)).astype(BF16)
    gb_ref[...] = jax.nn.sigmoid(proj(o + D_MODEL, o + 2 * D_MODEL)).astype(BF16)


def _in_proj(x, positions, g_mix, w_in):
    B, S, _ = x.shape
    ts = min(IN_TS, S)
    inv_freq = ROPE_THETA ** (-jnp.arange(0, ROT_DIM, 2, dtype=F32) / ROT_DIM)
    lane = jnp.arange(LANES) % HEAD_DIM
    invf = jnp.where(lane < ROT_DIM, inv_freq[lane % (ROT_DIM // 2)], 0.0).reshape(1, LANES)
    d_in = w_in.shape[1]
    tok = lambda d: pl.BlockSpec((None, ts, d), lambda b, i: (b, i, 0))
    full = lambda shape: pl.BlockSpec(shape, lambda b, i: (0,) * len(shape))
    outs = pl.pallas_call(
        _in_proj_kernel,
        grid=(B, S // ts),
        in_specs=[tok(D_MODEL), tok(1), full((1, D_MODEL)), full((D_MODEL, d_in)), full((1, LANES))],
        out_specs=[pl.BlockSpec((ts, D_SSM), lambda b, i: (i, b)),
                   tok(D_ATT), tok(D_ATT), tok(D_ATT), tok(D_MODEL), tok(D_MODEL)],
        out_shape=[jax.ShapeDtypeStruct((S, B * D_SSM), BF16),
                   jax.ShapeDtypeStruct((B, S, D_ATT), BF16),
                   jax.ShapeDtypeStruct((B, S, D_ATT), BF16),
                   jax.ShapeDtypeStruct((B, S, D_ATT), BF16),
                   jax.ShapeDtypeStruct((B, S, D_MODEL), BF16),
                   jax.ShapeDtypeStruct((B, S, D_MODEL), BF16)],
        compiler_params=pltpu.CompilerParams(
            dimension_semantics=("parallel", "parallel"), vmem_limit_bytes=VMEM_LIMIT),
        name="in_proj",
    )(x, positions.reshape(B, S, 1), g_mix.reshape(1, D_MODEL), w_in.astype(BF16), invf)
    return outs


S5_TS = 128
S5_BATCH = 4
S5_COLS = 512


def _s5_kernel(u_ref, bre_ref, bim_ref, a1r_ref, a1i_ref, pr_ref, pi_ref,
               cre_ref, cim_ref, d_ref, wglu_ref, y_ref,
               xr, xi, cr, ci, ysc):
    rows = xr.shape[0]
    ts = rows // S5_BATCH

    @pl.when(pl.program_id(0) == 0)
    def _():
        cr[...] = jnp.zeros_like(cr)
        ci[...] = jnp.zeros_like(ci)

    u = u_ref[...]
    xr[...] = _dot(u, bre_ref[...])
    xi[...] = _dot(u, bim_ref[...])

    hi_rows = lax.broadcasted_iota(jnp.int32, (SUBLANES, S5_COLS), 0) >= S5_BATCH
    for cb in range(D_STATE // S5_COLS):
        sl = slice(cb * S5_COLS, (cb + 1) * S5_COLS)
        a_r, a_i = a1r_ref[:, sl], a1i_ref[:, sl]
        p_r, p_i = pr_ref[:, sl], pi_ref[:, sl]

        def body(t, carry):
            c_r, c_i = carry
            r0 = pl.multiple_of(t * SUBLANES, SUBLANES)
            x_r = xr[pl.ds(r0, SUBLANES), sl]
            x_i = xi[pl.ds(r0, SUBLANES), sl]
            s_r = pltpu.roll(x_r, S5_BATCH, 0)
            s_i = pltpu.roll(x_i, S5_BATCH, 0)
            h_r = x_r + (a_r * s_r - a_i * s_i) + (p_r * c_r - p_i * c_i)
            h_i = x_i + (a_r * s_i + a_i * s_r) + (p_r * c_i + p_i * c_r)
            xr[pl.ds(r0, SUBLANES), sl] = h_r
            xi[pl.ds(r0, SUBLANES), sl] = h_i
            n_r = jnp.where(hi_rows, h_r, pltpu.roll(h_r, S5_BATCH, 0))
            n_i = jnp.where(hi_rows, h_i, pltpu.roll(h_i, S5_BATCH, 0))
            return n_r, n_i

        c_r, c_i = lax.fori_loop(0, rows // SUBLANES, body, (cr[:, sl], ci[:, sl]), unroll=2)
        cr[:, sl] = c_r
        ci[:, sl] = c_i

    y = (_dot(xr[...].astype(BF16), cre_ref[...]) - _dot(xi[...].astype(BF16), cim_ref[...])
         + d_ref[...] * u.astype(F32))
    y = jax.nn.gelu(y)
    y = y * jax.nn.sigmoid(_dot(y.astype(BF16), wglu_ref[...]))
    for c in range(D_SSM // LANES):
        ysc[c] = y[:, c * LANES:(c + 1) * LANES]
    for b in range(S5_BATCH):
        for c in range(D_SSM // LANES):
            y_ref[b, :, c * LANES:(c + 1) * LANES] = (
                ysc[c, pl.ds(b, ts, stride=S5_BATCH), :].astype(BF16))


def _s5_tables(log_dt, a_re, a_im, b_re, b_im, c_re, c_im):
    dt = jnp.exp(log_dt.astype(F32))[:, None]
    ar, ai = a_re.astype(F32), a_im.astype(F32)
    mag = jnp.exp(dt * ar)
    abar_re, abar_im = mag * jnp.cos(dt * ai), mag * jnp.sin(dt * ai)
    den = ar * ar + ai * ai
    nr, ni = abar_re - 1.0, abar_im
    f_re = (nr * ar + ni * ai) / den
    f_im = (ni * ar - nr * ai) / den
    br, bi = b_re.astype(F32), b_im.astype(F32)
    bb_re = f_re[..., None] * br - f_im[..., None] * bi
    bb_im = f_re[..., None] * bi + f_im[..., None] * br
    eye = jnp.eye(SSM_GROUPS, dtype=F32)

    def in_blockdiag(bb):
        return jnp.einsum('gnc,gh->gchn', bb, eye).reshape(D_SSM, D_STATE)

    def out_blockdiag(c):
        return jnp.einsum('gcn,gh->gnhc', c.astype(F32), eye).reshape(D_STATE, D_SSM)

    a_r = abar_re.reshape(1, D_STATE)
    a_i = abar_im.reshape(1, D_STATE)
    a2_r = a_r * a_r - a_i * a_i
    a2_i = 2.0 * a_r * a_i
    hi = (jnp.arange(SUBLANES) >= S5_BATCH)[:, None]
    a1r = jnp.where(hi, a_r, 0.0)
    a1i = jnp.where(hi, a_i, 0.0)
    p_r = jnp.where(hi, a2_r, a_r)
    p_i = jnp.where(hi, a2_i, a_i)
    return (in_blockdiag(bb_re).astype(BF16), in_blockdiag(bb_im).astype(BF16),
            a1r, a1i, p_r, p_i,
            out_blockdiag(c_re).astype(BF16), out_blockdiag(c_im).astype(BF16))


def _s5(u_sb, tables, d_skip, w_glu, B, S):
    assert B == S5_BATCH
    ts = min(S5_TS, S)
    rows = ts * B
    bre, bim, a1r, a1i, p_r, p_i, cre, cim = tables
    full = lambda shape: pl.BlockSpec(shape, lambda i: (0,) * len(shape))
    return pl.pallas_call(
        _s5_kernel,
        grid=(S // ts,),
        in_specs=[pl.BlockSpec((rows, D_SSM), lambda i: (i, 0)),
                  full((D_SSM, D_STATE)), full((D_SSM, D_STATE)),
                  full((SUBLANES, D_STATE)), full((SUBLANES, D_STATE)),
                  full((SUBLANES, D_STATE)), full((SUBLANES, D_STATE)),
                  full((D_STATE, D_SSM)), full((D_STATE, D_SSM)),
                  full((1, D_SSM)), full((D_SSM, D_SSM))],
        out_specs=pl.BlockSpec((B, ts, D_SSM), lambda i: (0, i, 0)),
        out_shape=jax.ShapeDtypeStruct((B, S, D_SSM), BF16),
        scratch_shapes=[pltpu.VMEM((rows, D_STATE), F32), pltpu.VMEM((rows, D_STATE), F32),
                        pltpu.VMEM((SUBLANES, D_STATE), F32), pltpu.VMEM((SUBLANES, D_STATE), F32),
                        pltpu.VMEM((D_SSM // LANES, rows, LANES), F32)],
        compiler_params=pltpu.CompilerParams(
            dimension_semantics=("arbitrary",), vmem_limit_bytes=VMEM_LIMIT),
        name="s5",
    )(u_sb.reshape(S * B, D_SSM), bre, bim, a1r, a1i, p_r, p_i, cre, cim,
      d_skip.reshape(1, D_SSM).astype(F32), w_glu.astype(BF16))


MOBA_PAIR = 2 * MOBA_BLOCK


def _moba_kernel(q_ref, k_ref, v_ref, o_ref, kmean, kaug_a, kaug_b, vaug_a, vaug_b, m_s, acc_s):
    qi = pl.program_id(2)
    nb = k_ref.shape[0] // MOBA_BLOCK
    nbp = kmean.shape[0]
    lane = lax.broadcasted_iota(jnp.int32, (1, LANES), 1)
    head_a = lane < HEAD_DIM

    @pl.when(qi == 0)
    def _():
        kmean[...] = jnp.zeros_like(kmean)
        for j in range(nb):
            rows = pl.ds(j * MOBA_BLOCK, MOBA_BLOCK)
            kj = k_ref[rows, :].astype(F32)
            vj = v_ref[rows, :].astype(F32)
            kmean[j:j + 1, :] = jnp.sum(kj, axis=0, keepdims=True) * (1.0 / MOBA_BLOCK)
            kaug_a[rows, :] = jnp.where(head_a, kj, jnp.where(lane - HEAD_DIM == j, 1.0, 0.0)).astype(BF16)
            kaug_b[rows, :] = jnp.where(head_a, jnp.where(lane == j, 1.0, 0.0), kj).astype(BF16)
            vaug_a[rows, :] = jnp.where(head_a, vj, 1.0).astype(BF16)
            vaug_b[rows, :] = jnp.where(head_a, 1.0, vj).astype(BF16)

    qf = q_ref[...].astype(F32)
    blk_row = lax.broadcasted_iota(jnp.int32, (nbp, MOBA_BLOCK), 0)
    q_augs = []
    for is_a in (True, False):
        mine = head_a if is_a else jnp.logical_not(head_a)
        q_own = jnp.where(mine, qf, 0.0)
        g = _dot_nt(kmean[...], q_own, precision=HIGHEST)
        g = jnp.where(blk_row < qi, g, NEG)
        sel = jnp.zeros(g.shape, F32)
        for _ in range(MOBA_TOPK):
            m = jnp.max(g, axis=0, keepdims=True)
            idx = jnp.min(jnp.where(g == m, blk_row, nbp), axis=0, keepdims=True)
            hit = blk_row == idx
            sel = jnp.where(hit, jnp.where(idx < qi, 1.0, 0.0), sel)
            g = jnp.where(hit, -jnp.inf, g)
        bias_t = jnp.where(sel > 0.0, 0.0, jnp.where(blk_row == qi, 0.0, NEG))
        bias_t = jnp.concatenate([bias_t, jnp.full((LANES - nbp, MOBA_BLOCK), NEG, F32)], axis=0)
        bias = jnp.transpose(bias_t)
        if is_a:
            bias = pltpu.roll(bias, HEAD_DIM, 1)
        q_augs.append(jnp.where(mine, qf, bias).astype(BF16))

    m_s[...] = jnp.full(m_s.shape, -jnp.inf, F32)
    acc_s[...] = jnp.zeros_like(acc_s)
    qpos = qi * MOBA_BLOCK + lax.broadcasted_iota(jnp.int32, (MOBA_BLOCK, MOBA_PAIR), 0)
    col = lax.broadcasted_iota(jnp.int32, (MOBA_BLOCK, MOBA_PAIR), 1)

    def step(jj, causal):
        rows = pl.ds(pl.multiple_of(jj * MOBA_PAIR, MOBA_PAIR), MOBA_PAIR)
        for hd, (kaug, vaug) in enumerate(((kaug_a, vaug_a), (kaug_b, vaug_b))):
            s = _dot_nt(q_augs[hd], kaug[rows, :])
            if causal:
                s = jnp.where(jj * MOBA_PAIR + col <= qpos, s, NEG)
            m_old = m_s[hd]
            m_new = jnp.maximum(m_old, jnp.max(s, axis=-1, keepdims=True))
            alpha = jnp.exp(m_old - m_new)
            p = jnp.exp(s - m_new)
            m_s[hd] = m_new
            acc_s[hd] = alpha * acc_s[hd] + _dot(p.astype(BF16), vaug[rows, :])

    def body(jj, _):
        step(jj, False)
        return 0

    lax.fori_loop(0, qi // 2, body, 0)
    step(qi // 2, True)
    acc_a, acc_b = acc_s[0], acc_s[1]
    o_ref[...] = jnp.where(head_a, acc_a / pltpu.roll(acc_a, HEAD_DIM, 1),
                           acc_b / pltpu.roll(acc_b, HEAD_DIM, 1)).astype(BF16)


def _moba(q, k, v):
    B, S, _ = q.shape
    nq = S // MOBA_BLOCK
    assert nq <= HEAD_DIM and nq % 2 == 0
    nbp = -(-nq // SUBLANES) * SUBLANES
    blk = pl.BlockSpec((None, MOBA_BLOCK, LANES), lambda b, h, i: (b, i, h))
    seq = pl.BlockSpec((None, S, LANES), lambda b, h, i: (b, 0, h))
    return pl.pallas_call(
        _moba_kernel,
        grid=(B, D_ATT // LANES, nq),
        in_specs=[blk, seq, seq],
        out_specs=blk,
        out_shape=jax.ShapeDtypeStruct((B, S, D_ATT), BF16),
        scratch_shapes=[pltpu.VMEM((nbp, LANES), F32),
                        pltpu.VMEM((S, LANES), BF16), pltpu.VMEM((S, LANES), BF16),
                        pltpu.VMEM((S, LANES), BF16), pltpu.VMEM((S, LANES), BF16),
                        pltpu.VMEM((2, MOBA_BLOCK, 1), F32),
                        pltpu.VMEM((2, MOBA_BLOCK, LANES), F32)],
        compiler_params=pltpu.CompilerParams(
            dimension_semantics=("parallel", "parallel", "arbitrary"), vmem_limit_bytes=VMEM_LIMIT),
        name="moba",
    )(q, k, v)


MERGE_TS = 256


def _merge_kernel(x_ref, ys_ref, at_ref, ga_ref, gb_ref, wa_ref, wb_ref, wo_ref, g_ref,
                  wq_ref, k1_ref, k2_ref, x1_ref, hq_ref, sc_ref):
    ya = _dot(ys_ref[...], wa_ref[...])
    yb = _dot(at_ref[...], wb_ref[...])
    merged = ga_ref[...].astype(F32) * ya + gb_ref[...].astype(F32) * yb
    x1 = x_ref[...] + _dot(merged.astype(BF16), wo_ref[...])
    x1_ref[...] = x1
    hq = _rms(x1, g_ref[...])
    hq_ref[...] = hq
    qp = _dot(hq.astype(BF16), wq_ref[...])
    for h in range(PEER_HEADS):
        o = h * PEER_QDIM
        sc_ref[2 * h] = _dot_nt(k1_ref[h], qp[:, o:o + PEER_HALF], precision=HIGHEST)
        sc_ref[2 * h + 1] = _dot_nt(k2_ref[h], qp[:, o + PEER_HALF:o + PEER_QDIM], precision=HIGHEST)


def _merge(x2d, ys, att, ga, gb, w_proj_ssm, w_proj_att, w_out, g_ffn, peer_w_q, keys1, keys2):
    T = x2d.shape[0]
    ts = min(MERGE_TS, T)
    tok = lambda d: pl.BlockSpec((ts, d), lambda i: (i, 0))
    full = lambda shape: pl.BlockSpec(shape, lambda i: (0,) * len(shape))
    qd = PEER_HEADS * PEER_QDIM
    return pl.pallas_call(
        _merge_kernel,
        grid=(T // ts,),
        in_specs=[tok(D_MODEL), tok(D_SSM), tok(D_ATT), tok(D_MODEL), tok(D_MODEL),
                  full((D_SSM, D_MODEL)), full((D_ATT, D_MODEL)), full((D_MODEL, D_MODEL)),
                  full((1, D_MODEL)), full((D_MODEL, qd)),
                  full((PEER_HEADS, PEER_KEYS, PEER_HALF)), full((PEER_HEADS, PEER_KEYS, PEER_HALF))],
        out_specs=[tok(D_MODEL), tok(D_MODEL),
                   pl.BlockSpec((2 * PEER_HEADS, PEER_KEYS, ts), lambda i: (0, 0, i))],
        out_shape=[jax.ShapeDtypeStruct((T, D_MODEL), F32),
                   jax.ShapeDtypeStruct((T, D_MODEL), F32),
                   jax.ShapeDtypeStruct((2 * PEER_HEADS, PEER_KEYS, T), F32)],
        compiler_params=pltpu.CompilerParams(
            dimension_semantics=("parallel",), vmem_limit_bytes=VMEM_LIMIT),
        name="merge",
    )(x2d, ys, att, ga, gb, w_proj_ssm.astype(BF16), w_proj_att.astype(BF16), w_out.astype(BF16),
      g_ffn.reshape(1, D_MODEL), peer_w_q.astype(BF16), keys1, keys2)


TOPK_TS = 256


def _top_rows(s, row, k):
    vals, idxs = [], []
    for _ in range(k):
        m = jnp.max(s, axis=0, keepdims=True)
        idx = jnp.min(jnp.where(s == m, row, s.shape[0]), axis=0, keepdims=True)
        vals.append(m)
        idxs.append(idx)
        s = jnp.where(row == idx, -jnp.inf, s)
    return vals, idxs


def _stack_rows(rows, row16):
    acc = jnp.zeros(row16.shape, rows[0].dtype)
    for r, v in enumerate(rows):
        acc = jnp.where(row16 == r, v, acc)
    return acc


def _topk_kernel(sc_ref, idx_ref, gate_ref):
    ts = sc_ref.shape[-1]
    row = lax.broadcasted_iota(jnp.int32, (PEER_KEYS, ts), 0)
    row16 = lax.broadcasted_iota(jnp.int32, (PEER_TOPK, ts), 0)
    row8 = lax.broadcasted_iota(jnp.int32, (SUBLANES, ts), 0)
    counts = [PEER_TOPK // (i + 1) for i in range(PEER_TOPK)]
    heights = [PEER_TOPK if c > SUBLANES else SUBLANES for c in counts]
    n_cand = sum(heights)
    rowc = lax.broadcasted_iota(jnp.int32, (n_cand, ts), 0)
    gate_rows, eid_rows = [], []
    for h in range(PEER_HEADS):
        v1, i1 = _top_rows(sc_ref[2 * h], row, PEER_TOPK)
        v2, i2 = _top_rows(sc_ref[2 * h + 1], row, PEER_TOPK)
        v2s = _stack_rows(v2, row16)
        i2s = _stack_rows(i2, row16).astype(F32)
        cand, eid = [], []
        for i in range(PEER_TOPK):
            n = heights[i]
            cand.append(jnp.where((row16 if n == PEER_TOPK else row8) < counts[i],
                                  v1[i] + v2s[:n], -jnp.inf))
            eid.append(i1[i].astype(F32) * PEER_KEYS + i2s[:n])
        cand = jnp.concatenate(cand, axis=0)
        eid = jnp.concatenate(eid, axis=0)
        tops, picks = [], []
        for _ in range(PEER_TOPK):
            m = jnp.max(cand, axis=0, keepdims=True)
            pos = jnp.min(jnp.where(cand == m, rowc, n_cand), axis=0, keepdims=True)
            hit = rowc == pos
            picks.append(jnp.max(jnp.where(hit, eid, -1.0), axis=0, keepdims=True))
            tops.append(m)
            cand = jnp.where(hit, -jnp.inf, cand)
        top = _stack_rows(tops, row16)
        p = jnp.exp(top - jnp.max(top, axis=0, keepdims=True))
        gate_rows.append(p / jnp.sum(p, axis=0, keepdims=True))
        eid_rows.append(_stack_rows(picks, row16))
    gate_ref[...] = jnp.transpose(jnp.concatenate(gate_rows, axis=0))
    idx_ref[...] = jnp.transpose(jnp.concatenate(eid_rows, axis=0)).astype(jnp.int32)


def _topk(scores):
    T = scores.shape[-1]
    ts = min(TOPK_TS, T)
    return pl.pallas_call(
        _topk_kernel,
        grid=(T // ts,),
        in_specs=[pl.BlockSpec((2 * PEER_HEADS, PEER_KEYS, ts), lambda i: (0, 0, i))],
        out_specs=[pl.BlockSpec((ts, PEER_SEL), lambda i: (i, 0)),
                   pl.BlockSpec((ts, PEER_SEL), lambda i: (i, 0))],
        out_shape=[jax.ShapeDtypeStruct((T, PEER_SEL), jnp.int32),
                   jax.ShapeDtypeStruct((T, PEER_SEL), F32)],
        compiler_params=pltpu.CompilerParams(
            dimension_semantics=("parallel",), vmem_limit_bytes=VMEM_LIMIT),
        name="topk",
    )(scores)


SC_CORES = 2
SC_SUBCORES = 16
SC_LANES = 16
SC_WORKERS = SC_CORES * SC_SUBCORES
PEER_CH = SC_LANES
PEER_NCH = PEER_SEL // PEER_CH
PEER_WORDS = D_MODEL // 2
PEER_NWG = PEER_WORDS // SC_LANES
PEER_RING = 4
PEER_QUAD = 4
HI_MASK = -65536
GELU_C = 0.7978845608028654


def _gelu_tanh_via_exp(x):
    z = GELU_C * (x + 0.044715 * (x * x * x))
    t = 1.0 - 2.0 / (jnp.exp(2.0 * z) + 1.0)
    return 0.5 * x * (1.0 + t)


def _unpack_pair(w):
    lo = plsc.bitcast(lax.shift_left(w, 16), F32)
    hi = plsc.bitcast(lax.bitwise_and(w, HI_MASK), F32)
    return lo, hi


def _peer_sc_body(idx_hbm, gate_hbm, h_hbm, u_hbm, v_hbm, o_hbm,
                  idx_v, gate_v, h_v, ubuf, vbuf, pbuf, w_v, out_v, usem, vsem, msem, osem):
    n_tok = o_hbm.shape[0] // SC_WORKERS
    base = (lax.axis_index("s") * SC_CORES + lax.axis_index("c")) * n_tok
    lane = lax.iota(jnp.int32, SC_LANES)
    zero_rows = jnp.zeros((SC_LANES,), jnp.int32)

    def meta_copies(tok, s):
        return (pltpu.make_async_copy(idx_hbm.at[tok], idx_v.at[s], msem.at[s]),
                pltpu.make_async_copy(gate_hbm.at[tok], gate_v.at[s], msem.at[s]),
                pltpu.make_async_copy(h_hbm.at[tok], h_v.at[s], msem.at[s]))

    def gather(tab, buf, sem, slot, rows):
        return pltpu.make_async_copy(tab.at[rows], buf.at[slot], sem.at[slot])

    def start_ahead(tab, buf, sem, s, c):
        ahead = c + PEER_RING
        src = jnp.where(ahead < PEER_NCH, s, 1 - s)
        ch = ahead % PEER_NCH
        rows = idx_v[src, pl.ds(pl.multiple_of(ch * PEER_CH, PEER_CH), PEER_CH)]
        gather(tab, buf, sem, c % PEER_RING, rows).start()

    def token(t, carry):
        s = t % 2
        tok = base + t
        nxt = base + jnp.minimum(t + 1, n_tok - 1)
        for cp in meta_copies(nxt, 1 - s):
            cp.start()

        def u_chunk(c, carry):
            slot = c % PEER_RING
            gather(u_hbm, ubuf, usem, slot, zero_rows).wait()

            def dot_step(q, accs):
                cols = [pl.ds(pl.multiple_of((q * PEER_QUAD + j) * SC_LANES, SC_LANES), SC_LANES)
                        for j in range(PEER_QUAD)]
                hs = [plsc.bitcast(h_v[s, col], BF16) for col in cols]
                out = []
                for r in range(PEER_CH):
                    p = plsc.bitcast(ubuf[slot, r, cols[0]], BF16) * hs[0]
                    for j in range(1, PEER_QUAD):
                        p = p + plsc.bitcast(ubuf[slot, r, cols[j]], BF16) * hs[j]
                    lo, hi = _unpack_pair(plsc.bitcast(p, jnp.int32))
                    out.append(accs[r] + lo + hi)
                return tuple(out)

            accs = lax.fori_loop(0, PEER_NWG // PEER_QUAD, dot_step,
                                 tuple(jnp.zeros((SC_LANES,), F32) for _ in range(PEER_CH)))

            @pl.when(c == PEER_NCH - PEER_RING)
            def _():
                for cp in meta_copies(nxt, 1 - s):
                    cp.wait()

            start_ahead(u_hbm, ubuf, usem, s, c)
            for r in range(PEER_CH):
                pbuf[r, :] = accs[r]
            tot = jnp.zeros((SC_LANES,), F32)
            for j in range(SC_LANES):
                tot = tot + plsc.load_gather(pbuf, [lane, jnp.full((SC_LANES,), j, jnp.int32)])
            rows = pl.ds(pl.multiple_of(c * PEER_CH, PEER_CH), PEER_CH)
            w_v[rows] = gate_v[s, rows] * _gelu_tanh_via_exp(tot)
            return carry

        lax.fori_loop(0, PEER_NCH, u_chunk, 0)

        @pl.when(t >= 2)
        def _():
            pltpu.make_async_copy(out_v.at[s], o_hbm.at[tok], osem.at[s]).wait()

        def v_chunk(c, carry):
            slot = c % PEER_RING
            gather(v_hbm, vbuf, vsem, slot, zero_rows).wait()
            ws = []
            for r in range(PEER_CH):
                w = plsc.load_gather(w_v, [jnp.full((SC_LANES,), r, jnp.int32) + c * PEER_CH])
                ws.append(plsc.pack(w, w, format=plsc.PackFormat.INTERLEAVED,
                                    preferred_element_type=BF16))
            first = c == 0

            @plsc.parallel_loop(0, PEER_NWG, unroll=2)
            def acc_step(g):
                col = pl.ds(pl.multiple_of(g * SC_LANES, SC_LANES), SC_LANES)
                col_o = pl.ds(pl.multiple_of(PEER_WORDS + g * SC_LANES, SC_LANES), SC_LANES)
                oe = jnp.where(first, 0.0, out_v[s, col])
                oo = jnp.where(first, 0.0, out_v[s, col_o])
                for r0 in range(0, PEER_CH, PEER_QUAD):
                    p = plsc.bitcast(vbuf[slot, r0, col], BF16) * ws[r0]
                    for r in range(r0 + 1, r0 + PEER_QUAD):
                        p = p + plsc.bitcast(vbuf[slot, r, col], BF16) * ws[r]
                    lo, hi = _unpack_pair(plsc.bitcast(p, jnp.int32))
                    oe = oe + lo
                    oo = oo + hi
                out_v[s, col] = oe
                out_v[s, col_o] = oo

            start_ahead(v_hbm, vbuf, vsem, s, c)
            return carry

        lax.fori_loop(0, PEER_NCH, v_chunk, 0)
        pltpu.make_async_copy(out_v.at[s], o_hbm.at[tok], osem.at[s]).start()
        return carry

    for cp in meta_copies(base, 0):
        cp.start()
    for cp in meta_copies(base, 0):
        cp.wait()
    for c in range(PEER_RING):
        rows = idx_v[0, pl.ds(c * PEER_CH, PEER_CH)]
        gather(u_hbm, ubuf, usem, c, rows).start()
        gather(v_hbm, vbuf, vsem, c, rows).start()
    lax.fori_loop(0, n_tok, token, 0)
    for c in range(PEER_RING):
        gather(u_hbm, ubuf, usem, c, zero_rows).wait()
        gather(v_hbm, vbuf, vsem, c, zero_rows).wait()
    for s in range(2):
        pltpu.make_async_copy(out_v.at[s], o_hbm.at[base], osem.at[s]).wait()


def _pack_bf16_pairs(tab):
    b = lax.bitcast_convert_type(tab.astype(BF16), jnp.uint16).astype(jnp.uint32)
    return lax.bitcast_convert_type(b[:, 0::2] | (b[:, 1::2] << 16), jnp.int32)


def _peer(idx, hq, gates, u_words, v_words):
    T = hq.shape[0]
    assert T % (2 * SC_WORKERS) == 0
    mesh = plsc.VectorSubcoreMesh(core_axis_name="c", subcore_axis_name="s",
                                  num_cores=SC_CORES, num_subcores=SC_SUBCORES)
    out_eo = pl.kernel(
        _peer_sc_body,
        out_type=jax.ShapeDtypeStruct((T, D_MODEL), F32),
        mesh=mesh,
        scratch_types=[
            pltpu.VMEM((2, PEER_SEL), jnp.int32), pltpu.VMEM((2, PEER_SEL), F32),
            pltpu.VMEM((2, PEER_WORDS), jnp.int32),
            pltpu.VMEM((PEER_RING, PEER_CH, PEER_WORDS), jnp.int32),
            pltpu.VMEM((PEER_RING, PEER_CH, PEER_WORDS), jnp.int32),
            pltpu.VMEM((PEER_CH, SC_LANES), F32), pltpu.VMEM((PEER_SEL,), F32),
            pltpu.VMEM((2, D_MODEL), F32),
            pltpu.SemaphoreType.DMA((PEER_RING,)), pltpu.SemaphoreType.DMA((PEER_RING,)),
            pltpu.SemaphoreType.DMA((2,)), pltpu.SemaphoreType.DMA((2,)),
        ],
        compiler_params=pltpu.CompilerParams(needs_layout_passes=False),
        name="peer_sc",
    )(idx, gates, _pack_bf16_pairs(hq), u_words, v_words)
    return jnp.stack([out_eo[:, :PEER_WORDS], out_eo[:, PEER_WORDS:]], axis=-1).reshape(T, D_MODEL)


FINAL_TS = 256


def _final_kernel(x1_ref, pe_ref, p_ref, gp_ref, wg_ref, wp_ref, gf_ref, o_ref):
    x2 = x1_ref[...] + pe_ref[...]
    e = _dot(p_ref[...].astype(BF16), wp_ref[...])
    gate = jax.nn.sigmoid(_dot(_rms(x2, gp_ref[...]).astype(BF16), wg_ref[...]))
    o_ref[...] = _rms(x2 + gate * e, gf_ref[...])


def _final(x1, peer_out, p2d, g_ple, ple_w_gate, ple_w_proj, g_final):
    T = x1.shape[0]
    ts = min(FINAL_TS, T)
    tok = lambda d: pl.BlockSpec((ts, d), lambda i: (i, 0))
    full = lambda shape: pl.BlockSpec(shape, lambda i: (0,) * len(shape))
    return pl.pallas_call(
        _final_kernel,
        grid=(T // ts,),
        in_specs=[tok(D_MODEL), tok(D_MODEL), tok(D_PLE), full((1, D_MODEL)),
                  full((D_MODEL, D_MODEL)), full((D_PLE, D_MODEL)), full((1, D_MODEL))],
        out_specs=tok(D_MODEL),
        out_shape=jax.ShapeDtypeStruct((T, D_MODEL), F32),
        compiler_params=pltpu.CompilerParams(
            dimension_semantics=("parallel",), vmem_limit_bytes=VMEM_LIMIT),
        name="final",
    )(x1, peer_out, p2d, g_ple.reshape(1, D_MODEL), ple_w_gate.astype(BF16),
      ple_w_proj.astype(BF16), g_final.reshape(1, D_MODEL))


def kernel(x, p, positions, g_mix, w_in, ssm_log_dt, ssm_a_re, ssm_a_im, ssm_b_re, ssm_b_im,
           ssm_c_re, ssm_c_im, ssm_d, ssm_w_glu, w_proj_ssm, w_proj_att, w_out, g_ffn,
           peer_w_q, peer_keys1, peer_keys2, peer_u, peer_v, g_ple, ple_w_gate, ple_w_proj,
           g_final):
    B, S, _ = x.shape
    T = B * S
    assert w_in.shape[0] == 1, "the final rmsnorm is fused into the single layer's last stage"
    for i in range(1):
        u_sb, q, k, v, ga, gb = _in_proj(x, positions, g_mix[i], w_in[i])
        tables = _s5_tables(ssm_log_dt[i], ssm_a_re[i], ssm_a_im[i], ssm_b_re[i], ssm_b_im[i],
                            ssm_c_re[i], ssm_c_im[i])
        ys = _s5(u_sb, tables, ssm_d[i], ssm_w_glu[i], B, S)
        u_words = _pack_bf16_pairs(peer_u[i])
        v_words = _pack_bf16_pairs(peer_v[i])
        outs = []
        for b in range(B):
            att = _moba(q[b:b + 1], k[b:b + 1], v[b:b + 1])
            x1, hq, scores = _merge(
                x[b], ys[b], att[0], ga[b], gb[b],
                w_proj_ssm[i], w_proj_att[i], w_out[i], g_ffn[i], peer_w_q[i],
                peer_keys1[i], peer_keys2[i])
            idx, gates = _topk(scores)
            peer_out = _peer(idx, hq, gates, u_words, v_words)
            outs.append(_final(x1, peer_out, p[i, b], g_ple[i], ple_w_gate[i],
                               ple_w_proj[i], g_final))
        x = jnp.stack(outs)
    return x
```

```python
import functools
import math

import jax
import jax.numpy as jnp
from jax import lax
from jax.experimental import pallas as pl
from jax.experimental.pallas import tpu as pltpu
from jax.experimental.pallas import tpu_sc as plsc

F32 = jnp.float32
BF16 = jnp.bfloat16

D_MODEL = 1024
D_SSM = 512
SSM_GROUP = 16
SSM_GROUPS = 32
SSM_STATE = 64
D_STATE = SSM_GROUPS * SSM_STATE
N_HEADS = 8
HEAD_DIM = 64
D_ATT = 512
ROT_DIM = 16
ROPE_THETA = 500000.0
MOBA_BLOCK = 256
MOBA_TOPK = 3
PEER_HEADS = 8
PEER_KEYS = 128
PEER_QDIM = 256
PEER_HALF = 128
PEER_TOPK = 16
PEER_SEL = PEER_HEADS * PEER_TOPK
D_PLE = 256
EPS = 1e-6
NEG = -1e30
LANES = 128
SUBLANES = 8
VMEM_LIMIT = 48 * 1024 * 1024
HIGHEST = lax.Precision.HIGHEST


def _rms(x, g):
    return x * lax.rsqrt(jnp.mean(x * x, axis=-1, keepdims=True) + EPS) * g


def _dot(a, b):
    return jnp.dot(a, b, preferred_element_type=F32)


def _dot_nt(a, b, precision=None):
    return lax.dot_general(a, b, (((1,), (1,)), ((), ())), precision=precision,
                           preferred_element_type=F32)


IN_TS = 512


def _in_proj_kernel(x_ref, pos_ref, g_ref, w_ref, invf_ref,
                    u_ref, q_ref, k_ref, v_ref, ga_ref, gb_ref):
    h = _rms(x_ref[...], g_ref[...]).astype(BF16)

    def proj(lo, hi):
        return _dot(h, w_ref[:, lo:hi])

    u_ref[...] = proj(0, D_SSM).astype(BF16)
    ang = pos_ref[...].astype(F32) * invf_ref[...]
    cos = jnp.cos(ang)
    sin = jnp.sin(ang)
    lane = lax.broadcasted_iota(jnp.int32, (1, LANES), 1) % HEAD_DIM
    half = ROT_DIM // 2
    sin_hi = jnp.where((lane >= half) & (lane < ROT_DIM), sin, 0.0)
    sin_lo = jnp.where(lane < half, -sin, 0.0)
    reps = D_ATT // LANES
    cos4 = jnp.concatenate([cos] * reps, axis=1)
    sin_hi4 = jnp.concatenate([sin_hi] * reps, axis=1)
    sin_lo4 = jnp.concatenate([sin_lo] * reps, axis=1)

    def rope(t):
        return (t * cos4 + pltpu.roll(t, half, 1) * sin_hi4
                + pltpu.roll(t, D_ATT - half, 1) * sin_lo4)

    q = rope(proj(D_SSM, D_SSM + D_ATT))
    q_ref[...] = (q * (HEAD_DIM ** -0.5)).astype(BF16)
    k_ref[...] = rope(proj(D_SSM + D_ATT, D_SSM + 2 * D_ATT)).astype(BF16)
    v_ref[...] = proj(D_SSM + 2 * D_ATT, D_SSM + 3 * D_ATT).astype(BF16)
    o = D_SSM + 3 * D_ATT
    ga_ref[...] = jax.nn.sigmoid(proj(o, o + D_MODEL)).astype(BF16)
    gb_ref[...] = jax.nn.sigmoid(proj(o + D_MODEL, o + 2 * D_MODEL)).astype(BF16)


def _in_proj(x, positions, g_mix, w_in):
    B, S, _ = x.shape
    ts = min(IN_TS, S)
    inv_freq = ROPE_THETA ** (-jnp.arange(0, ROT_DIM, 2, dtype=F32) / ROT_DIM)
    lane = jnp.arange(LANES) % HEAD_DIM
    invf = jnp.where(lane < ROT_DIM, inv_freq[lane % (ROT_DIM // 2)], 0.0).reshape(1, LANES)
    d_in = w_in.shape[1]
    tok = lambda d: pl.BlockSpec((None, ts, d), lambda b, i: (b, i, 0))
    full = lambda shape: pl.BlockSpec(shape, lambda b, i: (0,) * len(shape))
    outs = pl.pallas_call(
        _in_proj_kernel,
        grid=(B, S // ts),
        in_specs=[tok(D_MODEL), tok(1), full((1, D_MODEL)), full((D_MODEL, d_in)), full((1, LANES))],
        out_specs=[pl.BlockSpec((ts, D_SSM), lambda b, i: (i, b)),
                   tok(D_ATT), tok(D_ATT), tok(D_ATT), tok(D_MODEL), tok(D_MODEL)],
        out_shape=[jax.ShapeDtypeStruct((S, B * D_SSM), BF16),
                   jax.ShapeDtypeStruct((B, S, D_ATT), BF16),
                   jax.ShapeDtypeStruct((B, S, D_ATT), BF16),
                   jax.ShapeDtypeStruct((B, S, D_ATT), BF16),
                   jax.ShapeDtypeStruct((B, S, D_MODEL), BF16),
                   jax.ShapeDtypeStruct((B, S, D_MODEL), BF16)],
        compiler_params=pltpu.CompilerParams(
            dimension_semantics=("parallel", "parallel"), vmem_limit_bytes=VMEM_LIMIT),
        name="in_proj",
    )(x, positions.reshape(B, S, 1), g_mix.reshape(1, D_MODEL), w_in.astype(BF16), invf)
    return outs


S5_TS = 128
S5_BATCH = 4
S5_COLS = 512


def _s5_kernel(u_ref, bre_ref, bim_ref, a1r_ref, a1i_ref, pr_ref, pi_ref,
               cre_ref, cim_ref, d_ref, wglu_ref, y_ref,
               xr, xi, cr, ci, ysc):
    rows = xr.shape[0]
    ts = rows // S5_BATCH

    @pl.when(pl.program_id(0) == 0)
    def _():
        cr[...] = jnp.zeros_like(cr)
        ci[...] = jnp.zeros_like(ci)

    u = u_ref[...]
    xr[...] = _dot(u, bre_ref[...])
    xi[...] = _dot(u, bim_ref[...])

    hi_rows = lax.broadcasted_iota(jnp.int32, (SUBLANES, S5_COLS), 0) >= S5_BATCH
    for cb in range(D_STATE // S5_COLS):
        sl = slice(cb * S5_COLS, (cb + 1) * S5_COLS)
        a_r, a_i = a1r_ref[:, sl], a1i_ref[:, sl]
        p_r, p_i = pr_ref[:, sl], pi_ref[:, sl]

        def body(t, carry):
            c_r, c_i = carry
            r0 = pl.multiple_of(t * SUBLANES, SUBLANES)
            x_r = xr[pl.ds(r0, SUBLANES), sl]
            x_i = xi[pl.ds(r0, SUBLANES), sl]
            s_r = pltpu.roll(x_r, S5_BATCH, 0)
            s_i = pltpu.roll(x_i, S5_BATCH, 0)
            h_r = x_r + (a_r * s_r - a_i * s_i) + (p_r * c_r - p_i * c_i)
            h_i = x_i + (a_r * s_i + a_i * s_r) + (p_r * c_i + p_i * c_r)
            xr[pl.ds(r0, SUBLANES), sl] = h_r
            xi[pl.ds(r0, SUBLANES), sl] = h_i
            n_r = jnp.where(hi_rows, h_r, pltpu.roll(h_r, S5_BATCH, 0))
            n_i = jnp.where(hi_rows, h_i, pltpu.roll(h_i, S5_BATCH, 0))
            return n_r, n_i

        c_r, c_i = lax.fori_loop(0, rows // SUBLANES, body, (cr[:, sl], ci[:, sl]), unroll=2)
        cr[:, sl] = c_r
        ci[:, sl] = c_i

    y = (_dot(xr[...].astype(BF16), cre_ref[...]) - _dot(xi[...].astype(BF16), cim_ref[...])
         + d_ref[...] * u.astype(F32))
    y = jax.nn.gelu(y)
    y = y * jax.nn.sigmoid(_dot(y.astype(BF16), wglu_ref[...]))
    for c in range(D_SSM // LANES):
        ysc[c] = y[:, c * LANES:(c + 1) * LANES]
    for b in range(S5_BATCH):
        for c in range(D_SSM // LANES):
            y_ref[b, :, c * LANES:(c + 1) * LANES] = (
                ysc[c, pl.ds(b, ts, stride=S5_BATCH), :].astype(BF16))


def _s5_tables(log_dt, a_re, a_im, b_re, b_im, c_re, c_im):
    dt = jnp.exp(log_dt.astype(F32))[:, None]
    ar, ai = a_re.astype(F32), a_im.astype(F32)
    mag = jnp.exp(dt * ar)
    abar_re, abar_im = mag * jnp.cos(dt * ai), mag * jnp.sin(dt * ai)
    den = ar * ar + ai * ai
    nr, ni = abar_re - 1.0, abar_im
    f_re = (nr * ar + ni * ai) / den
    f_im = (ni * ar - nr * ai) / den
    br, bi = b_re.astype(F32), b_im.astype(F32)
    bb_re = f_re[..., None] * br - f_im[..., None] * bi
    bb_im = f_re[..., None] * bi + f_im[..., None] * br
    eye = jnp.eye(SSM_GROUPS, dtype=F32)

    def in_blockdiag(bb):
        return jnp.einsum('gnc,gh->gchn', bb, eye).reshape(D_SSM, D_STATE)

    def out_blockdiag(c):
        return jnp.einsum('gcn,gh->gnhc', c.astype(F32), eye).reshape(D_STATE, D_SSM)

    a_r = abar_re.reshape(1, D_STATE)
    a_i = abar_im.reshape(1, D_STATE)
    a2_r = a_r * a_r - a_i * a_i
    a2_i = 2.0 * a_r * a_i
    hi = (jnp.arange(SUBLANES) >= S5_BATCH)[:, None]
    a1r = jnp.where(hi, a_r, 0.0)
    a1i = jnp.where(hi, a_i, 0.0)
    p_r = jnp.where(hi, a2_r, a_r)
    p_i = jnp.where(hi, a2_i, a_i)
    return (in_blockdiag(bb_re).astype(BF16), in_blockdiag(bb_im).astype(BF16),
            a1r, a1i, p_r, p_i,
            out_blockdiag(c_re).astype(BF16), out_blockdiag(c_im).astype(BF16))


def _s5(u_sb, tables, d_skip, w_glu, B, S):
    assert B == S5_BATCH
    ts = min(S5_TS, S)
    rows = ts * B
    bre, bim, a1r, a1i, p_r, p_i, cre, cim = tables
    full = lambda shape: pl.BlockSpec(shape, lambda i: (0,) * len(shape))
    return pl.pallas_call(
        _s5_kernel,
        grid=(S // ts,),
        in_specs=[pl.BlockSpec((rows, D_SSM), lambda i: (i, 0)),
                  full((D_SSM, D_STATE)), full((D_SSM, D_STATE)),
                  full((SUBLANES, D_STATE)), full((SUBLANES, D_STATE)),
                  full((SUBLANES, D_STATE)), full((SUBLANES, D_STATE)),
                  full((D_STATE, D_SSM)), full((D_STATE, D_SSM)),
                  full((1, D_SSM)), full((D_SSM, D_SSM))],
        out_specs=pl.BlockSpec((B, ts, D_SSM), lambda i: (0, i, 0)),
        out_shape=jax.ShapeDtypeStruct((B, S, D_SSM), BF16),
        scratch_shapes=[pltpu.VMEM((rows, D_STATE), F32), pltpu.VMEM((rows, D_STATE), F32),
                        pltpu.VMEM((SUBLANES, D_STATE), F32), pltpu.VMEM((SUBLANES, D_STATE), F32),
                        pltpu.VMEM((D_SSM // LANES, rows, LANES), F32)],
        compiler_params=pltpu.CompilerParams(
            dimension_semantics=("arbitrary",), vmem_limit_bytes=VMEM_LIMIT),
        name="s5",
    )(u_sb.reshape(S * B, D_SSM), bre, bim, a1r, a1i, p_r, p_i, cre, cim,
      d_skip.reshape(1, D_SSM).astype(F32), w_glu.astype(BF16))


MOBA_PAIR = 2 * MOBA_BLOCK


def _moba_kernel(q_ref, k_ref, v_ref, o_ref, kmean, kaug_a, kaug_b, vaug_a, vaug_b, m_s, acc_s):
    qi = pl.program_id(2)
    nb = k_ref.shape[0] // MOBA_BLOCK
    nbp = kmean.shape[0]
    lane = lax.broadcasted_iota(jnp.int32, (1, LANES), 1)
    head_a = lane < HEAD_DIM

    @pl.when(qi == 0)
    def _():
        kmean[...] = jnp.zeros_like(kmean)
        for j in range(nb):
            rows = pl.ds(j * MOBA_BLOCK, MOBA_BLOCK)
            kj = k_ref[rows, :].astype(F32)
            vj = v_ref[rows, :].astype(F32)
            kmean[j:j + 1, :] = jnp.sum(kj, axis=0, keepdims=True) * (1.0 / MOBA_BLOCK)
            kaug_a[rows, :] = jnp.where(head_a, kj, jnp.where(lane - HEAD_DIM == j, 1.0, 0.0)).astype(BF16)
            kaug_b[rows, :] = jnp.where(head_a, jnp.where(lane == j, 1.0, 0.0), kj).astype(BF16)
            vaug_a[rows, :] = jnp.where(head_a, vj, 1.0).astype(BF16)
            vaug_b[rows, :] = jnp.where(head_a, 1.0, vj).astype(BF16)

    qf = q_ref[...].astype(F32)
    blk_row = lax.broadcasted_iota(jnp.int32, (nbp, MOBA_BLOCK), 0)
    q_augs = []
    for is_a in (True, False):
        mine = head_a if is_a else jnp.logical_not(head_a)
        q_own = jnp.where(mine, qf, 0.0)
        g = _dot_nt(kmean[...], q_own, precision=HIGHEST)
        g = jnp.where(blk_row < qi, g, NEG)
        sel = jnp.zeros(g.shape, F32)
        for _ in range(MOBA_TOPK):
            m = jnp.max(g, axis=0, keepdims=True)
            idx = jnp.min(jnp.where(g == m, blk_row, nbp), axis=0, keepdims=True)
            hit = blk_row == idx
            sel = jnp.where(hit, jnp.where(idx < qi, 1.0, 0.0), sel)
            g = jnp.where(hit, -jnp.inf, g)
        bias_t = jnp.where(sel > 0.0, 0.0, jnp.where(blk_row == qi, 0.0, NEG))
        bias_t = jnp.concatenate([bias_t, jnp.full((LANES - nbp, MOBA_BLOCK), NEG, F32)], axis=0)
        bias = jnp.transpose(bias_t)
        if is_a:
            bias = pltpu.roll(bias, HEAD_DIM, 1)
        q_augs.append(jnp.where(mine, qf, bias).astype(BF16))

    m_s[...] = jnp.full(m_s.shape, -jnp.inf, F32)
    acc_s[...] = jnp.zeros_like(acc_s)
    qpos = qi * MOBA_BLOCK + lax.broadcasted_iota(jnp.int32, (MOBA_BLOCK, MOBA_PAIR), 0)
    col = lax.broadcasted_iota(jnp.int32, (MOBA_BLOCK, MOBA_PAIR), 1)

    def step(jj, causal):
        rows = pl.ds(pl.multiple_of(jj * MOBA_PAIR, MOBA_PAIR), MOBA_PAIR)
        for hd, (kaug, vaug) in enumerate(((kaug_a, vaug_a), (kaug_b, vaug_b))):
            s = _dot_nt(q_augs[hd], kaug[rows, :])
            if causal:
                s = jnp.where(jj * MOBA_PAIR + col <= qpos, s, NEG)
            m_old = m_s[hd]
            m_new = jnp.maximum(m_old, jnp.max(s, axis=-1, keepdims=True))
            alpha = jnp.exp(m_old - m_new)
            p = jnp.exp(s - m_new)
            m_s[hd] = m_new
            acc_s[hd] = alpha * acc_s[hd] + _dot(p.astype(BF16), vaug[rows, :])

    def body(jj, _):
        step(jj, False)
        return 0

    lax.fori_loop(0, qi // 2, body, 0)
    step(qi // 2, True)
    acc_a, acc_b = acc_s[0], acc_s[1]
    o_ref[...] = jnp.where(head_a, acc_a / pltpu.roll(acc_a, HEAD_DIM, 1),
                           acc_b / pltpu.roll(acc_b, HEAD_DIM, 1)).astype(BF16)


def _moba(q, k, v):
    B, S, _ = q.shape
    nq = S // MOBA_BLOCK
    assert nq <= HEAD_DIM and nq % 2 == 0
    nbp = -(-nq // SUBLANES) * SUBLANES
    blk = pl.BlockSpec((None, MOBA_BLOCK, LANES), lambda b, h, i: (b, i, h))
    seq = pl.BlockSpec((None, S, LANES), lambda b, h, i: (b, 0, h))
    return pl.pallas_call(
        _moba_kernel,
        grid=(B, D_ATT // LANES, nq),
        in_specs=[blk, seq, seq],
        out_specs=blk,
        out_shape=jax.ShapeDtypeStruct((B, S, D_ATT), BF16),
        scratch_shapes=[pltpu.VMEM((nbp, LANES), F32),
                        pltpu.VMEM((S, LANES), BF16), pltpu.VMEM((S, LANES), BF16),
                        pltpu.VMEM((S, LANES), BF16), pltpu.VMEM((S, LANES), BF16),
                        pltpu.VMEM((2, MOBA_BLOCK, 1), F32),
                        pltpu.VMEM((2, MOBA_BLOCK, LANES), F32)],
        compiler_params=pltpu.CompilerParams(
            dimension_semantics=("parallel", "parallel", "arbitrary"), vmem_limit_bytes=VMEM_LIMIT),
        name="moba",
    )(q, k, v)


MERGE_TS = 256


def _merge_kernel(x_ref, ys_ref, at_ref, ga_ref, gb_ref, wa_ref, wb_ref, wo_ref, g_ref,
                  wq_ref, k1_ref, k2_ref, x1_ref, hq_ref, sc_ref):
    ya = _dot(ys_ref[...], wa_ref[...])
    yb = _dot(at_ref[...], wb_ref[...])
    merged = ga_ref[...].astype(F32) * ya + gb_ref[...].astype(F32) * yb
    x1 = x_ref[...] + _dot(merged.astype(BF16), wo_ref[...])
    x1_ref[...] = x1
    hq = _rms(x1, g_ref[...])
    hq_ref[...] = hq
    qp = _dot(hq.astype(BF16), wq_ref[...])
    for h in range(PEER_HEADS):
        o = h * PEER_QDIM
        sc_ref[2 * h] = _dot_nt(k1_ref[h], qp[:, o:o + PEER_HALF], precision=HIGHEST)
        sc_ref[2 * h + 1] = _dot_nt(k2_ref[h], qp[:, o + PEER_HALF:o + PEER_QDIM], precision=HIGHEST)


def _merge(x2d, ys, att, ga, gb, w_proj_ssm, w_proj_att, w_out, g_ffn, peer_w_q, keys1, keys2):
    T = x2d.shape[0]
    ts = min(MERGE_TS, T)
    tok = lambda d: pl.BlockSpec((ts, d), lambda i: (i, 0))
    full = lambda shape: pl.BlockSpec(shape, lambda i: (0,) * len(shape))
    qd = PEER_HEADS * PEER_QDIM
    return pl.pallas_call(
        _merge_kernel,
        grid=(T // ts,),
        in_specs=[tok(D_MODEL), tok(D_SSM), tok(D_ATT), tok(D_MODEL), tok(D_MODEL),
                  full((D_SSM, D_MODEL)), full((D_ATT, D_MODEL)), full((D_MODEL, D_MODEL)),
                  full((1, D_MODEL)), full((D_MODEL, qd)),
                  full((PEER_HEADS, PEER_KEYS, PEER_HALF)), full((PEER_HEADS, PEER_KEYS, PEER_HALF))],
        out_specs=[tok(D_MODEL), tok(D_MODEL),
                   pl.BlockSpec((2 * PEER_HEADS, PEER_KEYS, ts), lambda i: (0, 0, i))],
        out_shape=[jax.ShapeDtypeStruct((T, D_MODEL), F32),
                   jax.ShapeDtypeStruct((T, D_MODEL), F32),
                   jax.ShapeDtypeStruct((2 * PEER_HEADS, PEER_KEYS, T), F32)],
        compiler_params=pltpu.CompilerParams(
            dimension_semantics=("parallel",), vmem_limit_bytes=VMEM_LIMIT),
        name="merge",
    )(x2d, ys, att, ga, gb, w_proj_ssm.astype(BF16), w_proj_att.astype(BF16), w_out.astype(BF16),
      g_ffn.reshape(1, D_MODEL), peer_w_q.astype(BF16), keys1, keys2)


TOPK_TS = 256


def _top_rows(s, row, k):
    vals, idxs = [], []
    for _ in range(k):
        m = jnp.max(s, axis=0, keepdims=True)
        idx = jnp.min(jnp.where(s == m, row, s.shape[0]), axis=0, keepdims=True)
        vals.append(m)
        idxs.append(idx)
        s = jnp.where(row == idx, -jnp.inf, s)
    return vals, idxs


def _stack_rows(rows, row16):
    acc = jnp.zeros(row16.shape, rows[0].dtype)
    for r, v in enumerate(rows):
        acc = jnp.where(row16 == r, v, acc)
    return acc


def _topk_kernel(sc_ref, idx_ref, gate_ref):
    ts = sc_ref.shape[-1]
    row = lax.broadcasted_iota(jnp.int32, (PEER_KEYS, ts), 0)
    row16 = lax.broadcasted_iota(jnp.int32, (PEER_TOPK, ts), 0)
    row8 = lax.broadcasted_iota(jnp.int32, (SUBLANES, ts), 0)
    counts = [PEER_TOPK // (i + 1) for i in range(PEER_TOPK)]
    heights = [PEER_TOPK if c > SUBLANES else SUBLANES for c in counts]
    n_cand = sum(heights)
    rowc = lax.broadcasted_iota(jnp.int32, (n_cand, ts), 0)
    gate_rows, eid_rows = [], []
    for h in range(PEER_HEADS):
        v1, i1 = _top_rows(sc_ref[2 * h], row, PEER_TOPK)
        v2, i2 = _top_rows(sc_ref[2 * h + 1], row, PEER_TOPK)
        v2s = _stack_rows(v2, row16)
        i2s = _stack_rows(i2, row16).astype(F32)
        cand, eid = [], []
        for i in range(PEER_TOPK):
            n = heights[i]
            cand.append(jnp.where((row16 if n == PEER_TOPK else row8) < counts[i],
                                  v1[i] + v2s[:n], -jnp.inf))
            eid.append(i1[i].astype(F32) * PEER_KEYS + i2s[:n])
        cand = jnp.concatenate(cand, axis=0)
        eid = jnp.concatenate(eid, axis=0)
        tops, picks = [], []
        for _ in range(PEER_TOPK):
            m = jnp.max(cand, axis=0, keepdims=True)
            pos = jnp.min(jnp.where(cand == m, rowc, n_cand), axis=0, keepdims=True)
            hit = rowc == pos
            picks.append(jnp.max(jnp.where(hit, eid, -1.0), axis=0, keepdims=True))
            tops.append(m)
            cand = jnp.where(hit, -jnp.inf, cand)
        top = _stack_rows(tops, row16)
        p = jnp.exp(top - jnp.max(top, axis=0, keepdims=True))
        gate_rows.append(p / jnp.sum(p, axis=0, keepdims=True))
        eid_rows.append(_stack_rows(picks, row16))
    gate_ref[...] = jnp.transpose(jnp.concatenate(gate_rows, axis=0))
    idx_ref[...] = jnp.transpose(jnp.concatenate(eid_rows, axis=0)).astype(jnp.int32)


def _topk(scores):
    T = scores.shape[-1]
    ts = min(TOPK_TS, T)
    return pl.pallas_call(
        _topk_kernel,
        grid=(T // ts,),
        in_specs=[pl.BlockSpec((2 * PEER_HEADS, PEER_KEYS, ts), lambda i: (0, 0, i))],
        out_specs=[pl.BlockSpec((ts, PEER_SEL), lambda i: (i, 0)),
                   pl.BlockSpec((ts, PEER_SEL), lambda i: (i, 0))],
        out_shape=[jax.ShapeDtypeStruct((T, PEER_SEL), jnp.int32),
                   jax.ShapeDtypeStruct((T, PEER_SEL), F32)],
        compiler_params=pltpu.CompilerParams(
            dimension_semantics=("parallel",), vmem_limit_bytes=VMEM_LIMIT),
        name="topk",
    )(scores)


SC_CORES = 2
SC_SUBCORES = 16
SC_LANES = 16
SC_WORKERS = SC_CORES * SC_SUBCORES
PEER_CH = SC_LANES
PEER_NCH = PEER_SEL // PEER_CH
PEER_WORDS = D_MODEL // 2
PEER_NWG = PEER_WORDS // SC_LANES
PEER_RING = 4
PEER_QUAD = 4
HI_MASK = -65536
GELU_C = 0.7978845608028654


def _gelu_tanh_via_exp(x):
    z = GELU_C * (x + 0.044715 * (x * x * x))
    t = 1.0 - 2.0 / (jnp.exp(2.0 * z) + 1.0)
    return 0.5 * x * (1.0 + t)


def _unpack_pair(w):
    lo = plsc.bitcast(lax.shift_left(w, 16), F32)
    hi = plsc.bitcast(lax.bitwise_and(w, HI_MASK), F32)
    return lo, hi


def _peer_sc_body(idx_hbm, gate_hbm, h_hbm, u_hbm, v_hbm, o_hbm,
                  idx_v, gate_v, h_v, ubuf, vbuf, pbuf, w_v, out_v, usem, vsem, msem, osem):
    n_tok = o_hbm.shape[0] // SC_WORKERS
    base = (lax.axis_index("s") * SC_CORES + lax.axis_index("c")) * n_tok
    lane = lax.iota(jnp.int32, SC_LANES)
    zero_rows = jnp.zeros((SC_LANES,), jnp.int32)

    def meta_copies(tok, s):
        return (pltpu.make_async_copy(idx_hbm.at[tok], idx_v.at[s], msem.at[s]),
                pltpu.make_async_copy(gate_hbm.at[tok], gate_v.at[s], msem.at[s]),
                pltpu.make_async_copy(h_hbm.at[tok], h_v.at[s], msem.at[s]))

    def gather(tab, buf, sem, slot, rows):
        return pltpu.make_async_copy(tab.at[rows], buf.at[slot], sem.at[slot])

    def start_ahead(tab, buf, sem, s, c):
        ahead = c + PEER_RING
        src = jnp.where(ahead < PEER_NCH, s, 1 - s)
        ch = ahead % PEER_NCH
        rows = idx_v[src, pl.ds(pl.multiple_of(ch * PEER_CH, PEER_CH), PEER_CH)]
        gather(tab, buf, sem, c % PEER_RING, rows).start()

    def token(t, carry):
        s = t % 2
        tok = base + t
        nxt = base + jnp.minimum(t + 1, n_tok - 1)
        for cp in meta_copies(nxt, 1 - s):
            cp.start()

        def u_chunk(c, carry):
            slot = c % PEER_RING
            gather(u_hbm, ubuf, usem, slot, zero_rows).wait()

            def dot_step(q, accs):
                cols = [pl.ds(pl.multiple_of((q * PEER_QUAD + j) * SC_LANES, SC_LANES), SC_LANES)
                        for j in range(PEER_QUAD)]
                hs = [plsc.bitcast(h_v[s, col], BF16) for col in cols]
                out = []
                for r in range(PEER_CH):
                    p = plsc.bitcast(ubuf[slot, r, cols[0]], BF16) * hs[0]
                    for j in range(1, PEER_QUAD):
                        p = p + plsc.bitcast(ubuf[slot, r, cols[j]], BF16) * hs[j]
                    lo, hi = _unpack_pair(plsc.bitcast(p, jnp.int32))
                    out.append(accs[r] + lo + hi)
                return tuple(out)

            accs = lax.fori_loop(0, PEER_NWG // PEER_QUAD, dot_step,
                                 tuple(jnp.zeros((SC_LANES,), F32) for _ in range(PEER_CH)))

            @pl.when(c == PEER_NCH - PEER_RING)
            def _():
                for cp in meta_copies(nxt, 1 - s):
                    cp.wait()

            start_ahead(u_hbm, ubuf, usem, s, c)
            for r in range(PEER_CH):
                pbuf[r, :] = accs[r]
            tot = jnp.zeros((SC_LANES,), F32)
            for j in range(SC_LANES):
                tot = tot + plsc.load_gather(pbuf, [lane, jnp.full((SC_LANES,), j, jnp.int32)])
            rows = pl.ds(pl.multiple_of(c * PEER_CH, PEER_CH), PEER_CH)
            w_v[rows] = gate_v[s, rows] * _gelu_tanh_via_exp(tot)
            return carry

        lax.fori_loop(0, PEER_NCH, u_chunk, 0)

        @pl.when(t >= 2)
        def _():
            pltpu.make_async_copy(out_v.at[s], o_hbm.at[tok], osem.at[s]).wait()

        def v_chunk(c, carry):
            slot = c % PEER_RING
            gather(v_hbm, vbuf, vsem, slot, zero_rows).wait()
            ws = []
            for r in range(PEER_CH):
                w = plsc.load_gather(w_v, [jnp.full((SC_LANES,), r, jnp.int32) + c * PEER_CH])
                ws.append(plsc.pack(w, w, format=plsc.PackFormat.INTERLEAVED,
                                    preferred_element_type=BF16))
            first = c == 0

            @plsc.parallel_loop(0, PEER_NWG, unroll=2)
            def acc_step(g):
                col = pl.ds(pl.multiple_of(g * SC_LANES, SC_LANES), SC_LANES)
                col_o = pl.ds(pl.multiple_of(PEER_WORDS + g * SC_LANES, SC_LANES), SC_LANES)
                oe = jnp.where(first, 0.0, out_v[s, col])
                oo = jnp.where(first, 0.0, out_v[s, col_o])
                for r0 in range(0, PEER_CH, PEER_QUAD):
                    p = plsc.bitcast(vbuf[slot, r0, col], BF16) * ws[r0]
                    for r in range(r0 + 1, r0 + PEER_QUAD):
                        p = p + plsc.bitcast(vbuf[slot, r, col], BF16) * ws[r]
                    lo, hi = _unpack_pair(plsc.bitcast(p, jnp.int32))
                    oe = oe + lo
                    oo = oo + hi
                out_v[s, col] = oe
                out_v[s, col_o] = oo

            start_ahead(v_hbm, vbuf, vsem, s, c)
            return carry

        lax.fori_loop(0, PEER_NCH, v_chunk, 0)
        pltpu.make_async_copy(out_v.at[s], o_hbm.at[tok], osem.at[s]).start()
        return carry

    for cp in meta_copies(base, 0):
        cp.start()
    for cp in meta_copies(base, 0):
        cp.wait()
    for c in range(PEER_RING):
        rows = idx_v[0, pl.ds(c * PEER_CH, PEER_CH)]
        gather(u_hbm, ubuf, usem, c, rows).start()
        gather(v_hbm, vbuf, vsem, c, rows).start()
    lax.fori_loop(0, n_tok, token, 0)
    for c in range(PEER_RING):
        gather(u_hbm, ubuf, usem, c, zero_rows).wait()
        gather(v_hbm, vbuf, vsem, c, zero_rows).wait()
    for s in range(2):
        pltpu.make_async_copy(out_v.at[s], o_hbm.at[base], osem.at[s]).wait()


def _pack_bf16_pairs(tab):
    b = lax.bitcast_convert_type(tab.astype(BF16), jnp.uint16).astype(jnp.uint32)
    half = tab.shape[1] // 2
    return lax.bitcast_convert_type(b[:, :half] | (b[:, half:] << 16), jnp.int32)


def _peer(idx, hq, gates, u_words, v_words):
    T = hq.shape[0]
    assert T % (2 * SC_WORKERS) == 0
    mesh = plsc.VectorSubcoreMesh(core_axis_name="c", subcore_axis_name="s",
                                  num_cores=SC_CORES, num_subcores=SC_SUBCORES)
    return pl.kernel(
        _peer_sc_body,
        out_type=jax.ShapeDtypeStruct((T, D_MODEL), F32),
        mesh=mesh,
        scratch_types=[
            pltpu.VMEM((2, PEER_SEL), jnp.int32), pltpu.VMEM((2, PEER_SEL), F32),
            pltpu.VMEM((2, PEER_WORDS), jnp.int32),
            pltpu.VMEM((PEER_RING, PEER_CH, PEER_WORDS), jnp.int32),
            pltpu.VMEM((PEER_RING, PEER_CH, PEER_WORDS), jnp.int32),
            pltpu.VMEM((PEER_CH, SC_LANES), F32), pltpu.VMEM((PEER_SEL,), F32),
            pltpu.VMEM((2, D_MODEL), F32),
            pltpu.SemaphoreType.DMA((PEER_RING,)), pltpu.SemaphoreType.DMA((PEER_RING,)),
            pltpu.SemaphoreType.DMA((2,)), pltpu.SemaphoreType.DMA((2,)),
        ],
        compiler_params=pltpu.CompilerParams(needs_layout_passes=False),
        name="peer_sc",
    )(idx, gates, _pack_bf16_pairs(hq), u_words, v_words)


FINAL_TS = 256


def _final_kernel(x1_ref, pe_ref, p_ref, gp_ref, wg_ref, wp_ref, gf_ref, o_ref):
    x2 = x1_ref[...] + pe_ref[...]
    e = _dot(p_ref[...].astype(BF16), wp_ref[...])
    gate = jax.nn.sigmoid(_dot(_rms(x2, gp_ref[...]).astype(BF16), wg_ref[...]))
    o_ref[...] = _rms(x2 + gate * e, gf_ref[...])


def _final(x1, peer_out, p2d, g_ple, ple_w_gate, ple_w_proj, g_final):
    T = x1.shape[0]
    ts = min(FINAL_TS, T)
    tok = lambda d: pl.BlockSpec((ts, d), lambda i: (i, 0))
    full = lambda shape: pl.BlockSpec(shape, lambda i: (0,) * len(shape))
    return pl.pallas_call(
        _final_kernel,
        grid=(T // ts,),
        in_specs=[tok(D_MODEL), tok(D_MODEL), tok(D_PLE), full((1, D_MODEL)),
                  full((D_MODEL, D_MODEL)), full((D_PLE, D_MODEL)), full((1, D_MODEL))],
        out_specs=tok(D_MODEL),
        out_shape=jax.ShapeDtypeStruct((T, D_MODEL), F32),
        compiler_params=pltpu.CompilerParams(
            dimension_semantics=("parallel",), vmem_limit_bytes=VMEM_LIMIT),
        name="final",
    )(x1, peer_out, p2d, g_ple.reshape(1, D_MODEL), ple_w_gate.astype(BF16),
      ple_w_proj.astype(BF16), g_final.reshape(1, D_MODEL))


def kernel(x, p, positions, g_mix, w_in, ssm_log_dt, ssm_a_re, ssm_a_im, ssm_b_re, ssm_b_im,
           ssm_c_re, ssm_c_im, ssm_d, ssm_w_glu, w_proj_ssm, w_proj_att, w_out, g_ffn,
           peer_w_q, peer_keys1, peer_keys2, peer_u, peer_v, g_ple, ple_w_gate, ple_w_proj,
           g_final):
    B, S, _ = x.shape
    T = B * S
    assert w_in.shape[0] == 1, "the final rmsnorm is fused into the single layer's last stage"
    for i in range(1):
        u_sb, q, k, v, ga, gb = _in_proj(x, positions, g_mix[i], w_in[i])
        tables = _s5_tables(ssm_log_dt[i], ssm_a_re[i], ssm_a_im[i], ssm_b_re[i], ssm_b_im[i],
                            ssm_c_re[i], ssm_c_im[i])
        ys = _s5(u_sb, tables, ssm_d[i], ssm_w_glu[i], B, S)
        u_words = _pack_bf16_pairs(peer_u[i])
        v_words = _pack_bf16_pairs(peer_v[i])
        outs = []
        for b in range(B):
            att = _moba(q[b:b + 1], k[b:b + 1], v[b:b + 1])
            x1, hq, scores = _merge(
                x[b], ys[b], att[0], ga[b], gb[b],
                w_proj_ssm[i], w_proj_att[i], w_out[i], g_ffn[i], peer_w_q[i],
                peer_keys1[i], peer_keys2[i])
            idx, gates = _topk(scores)
            peer_out = _peer(idx, hq, gates, u_words, v_words)
            outs.append(_final(x1, peer_out, p[i, b], g_ple[i], ple_w_gate[i],
                               ple_w_proj[i], g_final))
        x = jnp.stack(outs)
    return x
```

```python
import functools
import math

import jax
import jax.numpy as jnp
from jax import lax
from jax.experimental import pallas as pl
from jax.experimental.pallas import tpu as pltpu
from jax.experimental.pallas import tpu_sc as plsc

F32 = jnp.float32
BF16 = jnp.bfloat16

D_MODEL = 1024
D_SSM = 512
SSM_GROUP = 16
SSM_GROUPS = 32
SSM_STATE = 64
D_STATE = SSM_GROUPS * SSM_STATE
N_HEADS = 8
HEAD_DIM = 64
D_ATT = 512
ROT_DIM = 16
ROPE_THETA = 500000.0
MOBA_BLOCK = 256
MOBA_TOPK = 3
PEER_HEADS = 8
PEER_KEYS = 128
PEER_QDIM = 256
PEER_HALF = 128
PEER_TOPK = 16
PEER_SEL = PEER_HEADS * PEER_TOPK
D_PLE = 256
EPS = 1e-6
NEG = -1e30
LANES = 128
SUBLANES = 8
VMEM_LIMIT = 48 * 1024 * 1024
HIGHEST = lax.Precision.HIGHEST


def _rms(x, g):
    return x * lax.rsqrt(jnp.mean(x * x, axis=-1, keepdims=True) + EPS) * g


def _dot(a, b):
    return jnp.dot(a, b, preferred_element_type=F32)


def _dot_nt(a, b, precision=None):
    return lax.dot_general(a, b, (((1,), (1,)), ((), ())), precision=precision,
                           preferred_element_type=F32)


IN_TS = 512


def _in_proj_kernel(x_ref, pos_ref, g_ref, w_ref, invf_ref,
                    u_ref, q_ref, k_ref, v_ref, ga_ref, gb_ref):
    h = _rms(x_ref[...], g_ref[...]).astype(BF16)

    def proj(lo, hi):
        return _dot(h, w_ref[:, lo:hi])

    u_ref[...] = proj(0, D_SSM).astype(BF16)
    ang = pos_ref[...].astype(F32) * invf_ref[...]
    cos = jnp.cos(ang)
    sin = jnp.sin(ang)
    lane = lax.broadcasted_iota(jnp.int32, (1, LANES), 1) % HEAD_DIM
    half = ROT_DIM // 2
    sin_hi = jnp.where((lane >= half) & (lane < ROT_DIM), sin, 0.0)
    sin_lo = jnp.where(lane < half, -sin, 0.0)
    reps = D_ATT // LANES
    cos4 = jnp.concatenate([cos] * reps, axis=1)
    sin_hi4 = jnp.concatenate([sin_hi] * reps, axis=1)
    sin_lo4 = jnp.concatenate([sin_lo] * reps, axis=1)

    def rope(t):
        return (t * cos4 + pltpu.roll(t, half, 1) * sin_hi4
                + pltpu.roll(t, D_ATT - half, 1) * sin_lo4)

    q = rope(proj(D_SSM, D_SSM + D_ATT))
    q_ref[...] = (q * (HEAD_DIM ** -0.5)).astype(BF16)
    k_ref[...] = rope(proj(D_SSM + D_ATT, D_SSM + 2 * D_ATT)).astype(BF16)
    v_ref[...] = proj(D_SSM + 2 * D_ATT, D_SSM + 3 * D_ATT).astype(BF16)
    o = D_SSM + 3 * D_ATT
    ga_ref[...] = jax.nn.sigmoid(proj(o, o + D_MODEL)).astype(BF16)
    gb_ref[...] = jax.nn.sigmoid(proj(o + D_MODEL, o + 2 * D_MODEL)).astype(BF16)


def _in_proj(x, positions, g_mix, w_in):
    B, S, _ = x.shape
    ts = min(IN_TS, S)
    inv_freq = ROPE_THETA ** (-jnp.arange(0, ROT_DIM, 2, dtype=F32) / ROT_DIM)
    lane = jnp.arange(LANES) % HEAD_DIM
    invf = jnp.where(lane < ROT_DIM, inv_freq[lane % (ROT_DIM // 2)], 0.0).reshape(1, LANES)
    d_in = w_in.shape[1]
    tok = lambda d: pl.BlockSpec((None, ts, d), lambda b, i: (b, i, 0))
    full = lambda shape: pl.BlockSpec(shape, lambda b, i: (0,) * len(shape))
    outs = pl.pallas_call(
        _in_proj_kernel,
        grid=(B, S // ts),
        in_specs=[tok(D_MODEL), tok(1), full((1, D_MODEL)), full((D_MODEL, d_in)), full((1, LANES))],
        out_specs=[pl.BlockSpec((ts, D_SSM), lambda b, i: (i, b)),
                   tok(D_ATT), tok(D_ATT), tok(D_ATT), tok(D_MODEL), tok(D_MODEL)],
        out_shape=[jax.ShapeDtypeStruct((S, B * D_SSM), BF16),
                   jax.ShapeDtypeStruct((B, S, D_ATT), BF16),
                   jax.ShapeDtypeStruct((B, S, D_ATT), BF16),
                   jax.ShapeDtypeStruct((B, S, D_ATT), BF16),
                   jax.ShapeDtypeStruct((B, S, D_MODEL), BF16),
                   jax.ShapeDtypeStruct((B, S, D_MODEL), BF16)],
        compiler_params=pltpu.CompilerParams(
            dimension_semantics=("parallel", "parallel"), vmem_limit_bytes=VMEM_LIMIT),
        name="in_proj",
    )(x, positions.reshape(B, S, 1), g_mix.reshape(1, D_MODEL), w_in.astype(BF16), invf)
    return outs


S5_TS = 128
S5_BATCH = 4
S5_COLS = 512


def _s5_kernel(u_ref, bre_ref, bim_ref, a1r_ref, a1i_ref, pr_ref, pi_ref,
               cre_ref, cim_ref, d_ref, wglu_ref, y_ref,
               xr, xi, cr, ci, ysc):
    rows = xr.shape[0]
    ts = rows // S5_BATCH

    @pl.when(pl.program_id(0) == 0)
    def _():
        cr[...] = jnp.zeros_like(cr)
        ci[...] = jnp.zeros_like(ci)

    u = u_ref[...]
    xr[...] = _dot(u, bre_ref[...])
    xi[...] = _dot(u, bim_ref[...])

    hi_rows = lax.broadcasted_iota(jnp.int32, (SUBLANES, S5_COLS), 0) >= S5_BATCH
    for cb in range(D_STATE // S5_COLS):
        sl = slice(cb * S5_COLS, (cb + 1) * S5_COLS)
        a_r, a_i = a1r_ref[:, sl], a1i_ref[:, sl]
        p_r, p_i = pr_ref[:, sl], pi_ref[:, sl]

        def body(t, carry):
            c_r, c_i = carry
            r0 = pl.multiple_of(t * SUBLANES, SUBLANES)
            x_r = xr[pl.ds(r0, SUBLANES), sl]
            x_i = xi[pl.ds(r0, SUBLANES), sl]
            s_r = pltpu.roll(x_r, S5_BATCH, 0)
            s_i = pltpu.roll(x_i, S5_BATCH, 0)
            h_r = x_r + (a_r * s_r - a_i * s_i) + (p_r * c_r - p_i * c_i)
            h_i = x_i + (a_r * s_i + a_i * s_r) + (p_r * c_i + p_i * c_r)
            xr[pl.ds(r0, SUBLANES), sl] = h_r
            xi[pl.ds(r0, SUBLANES), sl] = h_i
            n_r = jnp.where(hi_rows, h_r, pltpu.roll(h_r, S5_BATCH, 0))
            n_i = jnp.where(hi_rows, h_i, pltpu.roll(h_i, S5_BATCH, 0))
            return n_r, n_i

        c_r, c_i = lax.fori_loop(0, rows // SUBLANES, body, (cr[:, sl], ci[:, sl]), unroll=2)
        cr[:, sl] = c_r
        ci[:, sl] = c_i

    y = (_dot(xr[...].astype(BF16), cre_ref[...]) - _dot(xi[...].astype(BF16), cim_ref[...])
         + d_ref[...] * u.astype(F32))
    y = jax.nn.gelu(y)
    y = y * jax.nn.sigmoid(_dot(y.astype(BF16), wglu_ref[...]))
    for c in range(D_SSM // LANES):
        ysc[c] = y[:, c * LANES:(c + 1) * LANES]
    for b in range(S5_BATCH):
        for c in range(D_SSM // LANES):
            y_ref[b, :, c * LANES:(c + 1) * LANES] = (
                ysc[c, pl.ds(b, ts, stride=S5_BATCH), :].astype(BF16))


def _s5_tables(log_dt, a_re, a_im, b_re, b_im, c_re, c_im):
    dt = jnp.exp(log_dt.astype(F32))[:, None]
    ar, ai = a_re.astype(F32), a_im.astype(F32)
    mag = jnp.exp(dt * ar)
    abar_re, abar_im = mag * jnp.cos(dt * ai), mag * jnp.sin(dt * ai)
    den = ar * ar + ai * ai
    nr, ni = abar_re - 1.0, abar_im
    f_re = (nr * ar + ni * ai) / den
    f_im = (ni * ar - nr * ai) / den
    br, bi = b_re.astype(F32), b_im.astype(F32)
    bb_re = f_re[..., None] * br - f_im[..., None] * bi
    bb_im = f_re[..., None] * bi + f_im[..., None] * br
    eye = jnp.eye(SSM_GROUPS, dtype=F32)

    def in_blockdiag(bb):
        return jnp.einsum('gnc,gh->gchn', bb, eye).reshape(D_SSM, D_STATE)

    def out_blockdiag(c):
        return jnp.einsum('gcn,gh->gnhc', c.astype(F32), eye).reshape(D_STATE, D_SSM)

    a_r = abar_re.reshape(1, D_STATE)
    a_i = abar_im.reshape(1, D_STATE)
    a2_r = a_r * a_r - a_i * a_i
    a2_i = 2.0 * a_r * a_i
    hi = (jnp.arange(SUBLANES) >= S5_BATCH)[:, None]
    a1r = jnp.where(hi, a_r, 0.0)
    a1i = jnp.where(hi, a_i, 0.0)
    p_r = jnp.where(hi, a2_r, a_r)
    p_i = jnp.where(hi, a2_i, a_i)
    return (in_blockdiag(bb_re).astype(BF16), in_blockdiag(bb_im).astype(BF16),
            a1r, a1i, p_r, p_i,
            out_blockdiag(c_re).astype(BF16), out_blockdiag(c_im).astype(BF16))


def _s5(u_sb, tables, d_skip, w_glu, B, S):
    assert B == S5_BATCH
    ts = min(S5_TS, S)
    rows = ts * B
    bre, bim, a1r, a1i, p_r, p_i, cre, cim = tables
    full = lambda shape: pl.BlockSpec(shape, lambda i: (0,) * len(shape))
    return pl.pallas_call(
        _s5_kernel,
        grid=(S // ts,),
        in_specs=[pl.BlockSpec((rows, D_SSM), lambda i: (i, 0)),
                  full((D_SSM, D_STATE)), full((D_SSM, D_STATE)),
                  full((SUBLANES, D_STATE)), full((SUBLANES, D_STATE)),
                  full((SUBLANES, D_STATE)), full((SUBLANES, D_STATE)),
                  full((D_STATE, D_SSM)), full((D_STATE, D_SSM)),
                  full((1, D_SSM)), full((D_SSM, D_SSM))],
        out_specs=pl.BlockSpec((B, ts, D_SSM), lambda i: (0, i, 0)),
        out_shape=jax.ShapeDtypeStruct((B, S, D_SSM), BF16),
        scratch_shapes=[pltpu.VMEM((rows, D_STATE), F32), pltpu.VMEM((rows, D_STATE), F32),
                        pltpu.VMEM((SUBLANES, D_STATE), F32), pltpu.VMEM((SUBLANES, D_STATE), F32),
                        pltpu.VMEM((D_SSM // LANES, rows, LANES), F32)],
        compiler_params=pltpu.CompilerParams(
            dimension_semantics=("arbitrary",), vmem_limit_bytes=VMEM_LIMIT),
        name="s5",
    )(u_sb.reshape(S * B, D_SSM), bre, bim, a1r, a1i, p_r, p_i, cre, cim,
      d_skip.reshape(1, D_SSM).astype(F32), w_glu.astype(BF16))


MOBA_PAIR = 2 * MOBA_BLOCK


def _moba_kernel(q_ref, k_ref, v_ref, o_ref, kmean, kaug_a, kaug_b, vaug_a, vaug_b, m_s, acc_s,
                 s_buf):
    qi = pl.program_id(2)
    nb = k_ref.shape[0] // MOBA_BLOCK
    nbp = kmean.shape[0]
    lane = lax.broadcasted_iota(jnp.int32, (1, LANES), 1)
    head_a = lane < HEAD_DIM

    @pl.when(qi == 0)
    def _():
        kmean[...] = jnp.zeros_like(kmean)
        for j in range(nb):
            rows = pl.ds(j * MOBA_BLOCK, MOBA_BLOCK)
            kj = k_ref[rows, :].astype(F32)
            vj = v_ref[rows, :].astype(F32)
            kmean[j:j + 1, :] = jnp.sum(kj, axis=0, keepdims=True) * (1.0 / MOBA_BLOCK)
            kaug_a[rows, :] = jnp.where(head_a, kj, jnp.where(lane - HEAD_DIM == j, 1.0, 0.0)).astype(BF16)
            kaug_b[rows, :] = jnp.where(head_a, jnp.where(lane == j, 1.0, 0.0), kj).astype(BF16)
            vaug_a[rows, :] = jnp.where(head_a, vj, 1.0).astype(BF16)
            vaug_b[rows, :] = jnp.where(head_a, 1.0, vj).astype(BF16)

    qf = q_ref[...].astype(F32)
    blk_row = lax.broadcasted_iota(jnp.int32, (nbp, MOBA_BLOCK), 0)
    q_augs = []
    for is_a in (True, False):
        mine = head_a if is_a else jnp.logical_not(head_a)
        q_own = jnp.where(mine, qf, 0.0)
        g = _dot_nt(kmean[...], q_own, precision=HIGHEST)
        g = jnp.where(blk_row < qi, g, NEG)
        sel = jnp.zeros(g.shape, F32)
        for _ in range(MOBA_TOPK):
            m = jnp.max(g, axis=0, keepdims=True)
            idx = jnp.min(jnp.where(g == m, blk_row, nbp), axis=0, keepdims=True)
            hit = blk_row == idx
            sel = jnp.where(hit, jnp.where(idx < qi, 1.0, 0.0), sel)
            g = jnp.where(hit, -jnp.inf, g)
        bias_t = jnp.where(sel > 0.0, 0.0, jnp.where(blk_row == qi, 0.0, NEG))
        bias_t = jnp.concatenate([bias_t, jnp.full((LANES - nbp, MOBA_BLOCK), NEG, F32)], axis=0)
        bias = jnp.transpose(bias_t)
        if is_a:
            bias = pltpu.roll(bias, HEAD_DIM, 1)
        q_augs.append(jnp.where(mine, qf, bias).astype(BF16))

    m_s[...] = jnp.full(m_s.shape, -jnp.inf, F32)
    acc_s[...] = jnp.zeros_like(acc_s)
    qpos = qi * MOBA_BLOCK + lax.broadcasted_iota(jnp.int32, (MOBA_BLOCK, MOBA_PAIR), 0)
    col = lax.broadcasted_iota(jnp.int32, (MOBA_BLOCK, MOBA_PAIR), 1)

    def kv_rows(jj):
        return pl.ds(pl.multiple_of(jj * MOBA_PAIR, MOBA_PAIR), MOBA_PAIR)

    def scores(jj, slot):
        for hd, kaug in enumerate((kaug_a, kaug_b)):
            s_buf[slot, hd] = _dot_nt(q_augs[hd], kaug[kv_rows(jj), :])

    def softmax_pv(jj, slot, causal):
        for hd, vaug in enumerate((vaug_a, vaug_b)):
            s = s_buf[slot, hd]
            if causal:
                s = jnp.where(jj * MOBA_PAIR + col <= qpos, s, NEG)
            m_old = m_s[hd]
            m_new = jnp.maximum(m_old, jnp.max(s, axis=-1, keepdims=True))
            alpha = jnp.exp(m_old - m_new)
            p = jnp.exp(s - m_new)
            m_s[hd] = m_new
            acc_s[hd] = alpha * acc_s[hd] + _dot(p.astype(BF16), vaug[kv_rows(jj), :])

    last = qi // 2
    scores(0, 0)

    def body(k, _):
        scores(2 * k + 1, 1)
        softmax_pv(2 * k, 0, False)
        scores(2 * k + 2, 0)
        softmax_pv(2 * k + 1, 1, False)
        return 0

    lax.fori_loop(0, last // 2, body, 0)

    @pl.when(last % 2 == 0)
    def _():
        softmax_pv(last, 0, True)

    @pl.when(last % 2 == 1)
    def _():
        scores(last, 1)
        softmax_pv(last - 1, 0, False)
        softmax_pv(last, 1, True)
    acc_a, acc_b = acc_s[0], acc_s[1]
    o_ref[...] = jnp.where(head_a, acc_a / pltpu.roll(acc_a, HEAD_DIM, 1),
                           acc_b / pltpu.roll(acc_b, HEAD_DIM, 1)).astype(BF16)


def _moba(q, k, v):
    B, S, _ = q.shape
    nq = S // MOBA_BLOCK
    assert nq <= HEAD_DIM and nq % 2 == 0
    nbp = -(-nq // SUBLANES) * SUBLANES
    blk = pl.BlockSpec((None, MOBA_BLOCK, LANES), lambda b, h, i: (b, i, h))
    seq = pl.BlockSpec((None, S, LANES), lambda b, h, i: (b, 0, h))
    return pl.pallas_call(
        _moba_kernel,
        grid=(B, D_ATT // LANES, nq),
        in_specs=[blk, seq, seq],
        out_specs=blk,
        out_shape=jax.ShapeDtypeStruct((B, S, D_ATT), BF16),
        scratch_shapes=[pltpu.VMEM((nbp, LANES), F32),
                        pltpu.VMEM((S, LANES), BF16), pltpu.VMEM((S, LANES), BF16),
                        pltpu.VMEM((S, LANES), BF16), pltpu.VMEM((S, LANES), BF16),
                        pltpu.VMEM((2, MOBA_BLOCK, 1), F32),
                        pltpu.VMEM((2, MOBA_BLOCK, LANES), F32),
                        pltpu.VMEM((2, 2, MOBA_BLOCK, MOBA_PAIR), F32)],
        compiler_params=pltpu.CompilerParams(
            dimension_semantics=("parallel", "parallel", "arbitrary"), vmem_limit_bytes=VMEM_LIMIT),
        name="moba",
    )(q, k, v)


MERGE_TS = 256


def _merge_kernel(x_ref, ys_ref, at_ref, ga_ref, gb_ref, wa_ref, wb_ref, wo_ref, g_ref,
                  wq_ref, k1_ref, k2_ref, x1_ref, hq_ref, sc_ref):
    ya = _dot(ys_ref[...], wa_ref[...])
    yb = _dot(at_ref[...], wb_ref[...])
    merged = ga_ref[...].astype(F32) * ya + gb_ref[...].astype(F32) * yb
    x1 = x_ref[...] + _dot(merged.astype(BF16), wo_ref[...])
    x1_ref[...] = x1
    hq = _rms(x1, g_ref[...])
    hq_ref[...] = hq
    qp = _dot(hq.astype(BF16), wq_ref[...])
    for h in range(PEER_HEADS):
        o = h * PEER_QDIM
        sc_ref[2 * h] = _dot_nt(k1_ref[h], qp[:, o:o + PEER_HALF], precision=HIGHEST)
        sc_ref[2 * h + 1] = _dot_nt(k2_ref[h], qp[:, o + PEER_HALF:o + PEER_QDIM], precision=HIGHEST)


def _merge(x2d, ys, att, ga, gb, w_proj_ssm, w_proj_att, w_out, g_ffn, peer_w_q, keys1, keys2):
    T = x2d.shape[0]
    ts = min(MERGE_TS, T)
    tok = lambda d: pl.BlockSpec((ts, d), lambda i: (i, 0))
    full = lambda shape: pl.BlockSpec(shape, lambda i: (0,) * len(shape))
    qd = PEER_HEADS * PEER_QDIM
    return pl.pallas_call(
        _merge_kernel,
        grid=(T // ts,),
        in_specs=[tok(D_MODEL), tok(D_SSM), tok(D_ATT), tok(D_MODEL), tok(D_MODEL),
                  full((D_SSM, D_MODEL)), full((D_ATT, D_MODEL)), full((D_MODEL, D_MODEL)),
                  full((1, D_MODEL)), full((D_MODEL, qd)),
                  full((PEER_HEADS, PEER_KEYS, PEER_HALF)), full((PEER_HEADS, PEER_KEYS, PEER_HALF))],
        out_specs=[tok(D_MODEL), tok(D_MODEL),
                   pl.BlockSpec((2 * PEER_HEADS, PEER_KEYS, ts), lambda i: (0, 0, i))],
        out_shape=[jax.ShapeDtypeStruct((T, D_MODEL), F32),
                   jax.ShapeDtypeStruct((T, D_MODEL), F32),
                   jax.ShapeDtypeStruct((2 * PEER_HEADS, PEER_KEYS, T), F32)],
        compiler_params=pltpu.CompilerParams(
            dimension_semantics=("parallel",), vmem_limit_bytes=VMEM_LIMIT),
        name="merge",
    )(x2d, ys, att, ga, gb, w_proj_ssm.astype(BF16), w_proj_att.astype(BF16), w_out.astype(BF16),
      g_ffn.reshape(1, D_MODEL), peer_w_q.astype(BF16), keys1, keys2)


TOPK_TS = 256


def _top_rows(s, row, k):
    vals, idxs = [], []
    for _ in range(k):
        m = jnp.max(s, axis=0, keepdims=True)
        idx = jnp.min(jnp.where(s == m, row, s.shape[0]), axis=0, keepdims=True)
        vals.append(m)
        idxs.append(idx)
        s = jnp.where(row == idx, -jnp.inf, s)
    return vals, idxs


def _stack_rows(rows, row16):
    acc = jnp.zeros(row16.shape, rows[0].dtype)
    for r, v in enumerate(rows):
        acc = jnp.where(row16 == r, v, acc)
    return acc


def _topk_kernel(sc_ref, idx_ref, gate_ref):
    ts = sc_ref.shape[-1]
    row = lax.broadcasted_iota(jnp.int32, (PEER_KEYS, ts), 0)
    row16 = lax.broadcasted_iota(jnp.int32, (PEER_TOPK, ts), 0)
    row8 = lax.broadcasted_iota(jnp.int32, (SUBLANES, ts), 0)
    counts = [PEER_TOPK // (i + 1) for i in range(PEER_TOPK)]
    heights = [PEER_TOPK if c > SUBLANES else SUBLANES for c in counts]
    n_cand = sum(heights)
    rowc = lax.broadcasted_iota(jnp.int32, (n_cand, ts), 0)
    gate_rows, eid_rows = [], []
    for h in range(PEER_HEADS):
        v1, i1 = _top_rows(sc_ref[2 * h], row, PEER_TOPK)
        v2, i2 = _top_rows(sc_ref[2 * h + 1], row, PEER_TOPK)
        v2s = _stack_rows(v2, row16)
        i2s = _stack_rows(i2, row16).astype(F32)
        cand, eid = [], []
        for i in range(PEER_TOPK):
            n = heights[i]
            cand.append(jnp.where((row16 if n == PEER_TOPK else row8) < counts[i],
                                  v1[i] + v2s[:n], -jnp.inf))
            eid.append(i1[i].astype(F32) * PEER_KEYS + i2s[:n])
        cand = jnp.concatenate(cand, axis=0)
        eid = jnp.concatenate(eid, axis=0)
        tops, picks = [], []
        for _ in range(PEER_TOPK):
            m = jnp.max(cand, axis=0, keepdims=True)
            pos = jnp.min(jnp.where(cand == m, rowc, n_cand), axis=0, keepdims=True)
            hit = rowc == pos
            picks.append(jnp.max(jnp.where(hit, eid, -1.0), axis=0, keepdims=True))
            tops.append(m)
            cand = jnp.where(hit, -jnp.inf, cand)
        top = _stack_rows(tops, row16)
        p = jnp.exp(top - jnp.max(top, axis=0, keepdims=True))
        gate_rows.append(p / jnp.sum(p, axis=0, keepdims=True))
        eid_rows.append(_stack_rows(picks, row16))
    gate_ref[...] = jnp.transpose(jnp.concatenate(gate_rows, axis=0))
    idx_ref[...] = jnp.transpose(jnp.concatenate(eid_rows, axis=0)).astype(jnp.int32)


def _topk(scores):
    T = scores.shape[-1]
    ts = min(TOPK_TS, T)
    return pl.pallas_call(
        _topk_kernel,
        grid=(T // ts,),
        in_specs=[pl.BlockSpec((2 * PEER_HEADS, PEER_KEYS, ts), lambda i: (0, 0, i))],
        out_specs=[pl.BlockSpec((ts, PEER_SEL), lambda i: (i, 0)),
                   pl.BlockSpec((ts, PEER_SEL), lambda i: (i, 0))],
        out_shape=[jax.ShapeDtypeStruct((T, PEER_SEL), jnp.int32),
                   jax.ShapeDtypeStruct((T, PEER_SEL), F32)],
        compiler_params=pltpu.CompilerParams(
            dimension_semantics=("parallel",), vmem_limit_bytes=VMEM_LIMIT),
        name="topk",
    )(scores)


SC_CORES = 2
SC_SUBCORES = 16
SC_LANES = 16
SC_WORKERS = SC_CORES * SC_SUBCORES
PEER_CH = SC_LANES
PEER_NCH = PEER_SEL // PEER_CH
PEER_WORDS = D_MODEL // 2
PEER_NWG = PEER_WORDS // SC_LANES
PEER_RING = 4
PEER_QUAD = 4
HI_MASK = -65536
GELU_C = 0.7978845608028654


def _gelu_tanh_via_exp(x):
    z = GELU_C * (x + 0.044715 * (x * x * x))
    t = 1.0 - 2.0 / (jnp.exp(2.0 * z) + 1.0)
    return 0.5 * x * (1.0 + t)


def _unpack_pair(w):
    lo = plsc.bitcast(lax.shift_left(w, 16), F32)
    hi = plsc.bitcast(lax.bitwise_and(w, HI_MASK), F32)
    return lo, hi


def _peer_sc_body(idx_hbm, gate_hbm, h_hbm, u_hbm, v_hbm, o_hbm,
                  idx_v, gate_v, h_v, ubuf, vbuf, pbuf, w_v, out_v, usem, vsem, msem, osem):
    n_tok = o_hbm.shape[0] // SC_WORKERS
    base = (lax.axis_index("s") * SC_CORES + lax.axis_index("c")) * n_tok
    lane = lax.iota(jnp.int32, SC_LANES)
    zero_rows = jnp.zeros((SC_LANES,), jnp.int32)

    def meta_copies(tok, s):
        return (pltpu.make_async_copy(idx_hbm.at[tok], idx_v.at[s], msem.at[s]),
                pltpu.make_async_copy(gate_hbm.at[tok], gate_v.at[s], msem.at[s]),
                pltpu.make_async_copy(h_hbm.at[tok], h_v.at[s], msem.at[s]))

    def gather(tab, buf, sem, slot, rows):
        return pltpu.make_async_copy(tab.at[rows], buf.at[slot], sem.at[slot])

    def start_ahead(tab, buf, sem, s, c):
        ahead = c + PEER_RING
        src = jnp.where(ahead < PEER_NCH, s, 1 - s)
        ch = ahead % PEER_NCH
        rows = idx_v[src, pl.ds(pl.multiple_of(ch * PEER_CH, PEER_CH), PEER_CH)]
        gather(tab, buf, sem, c % PEER_RING, rows).start()

    def token(t, carry):
        s = t % 2
        tok = base + t
        nxt = base + jnp.minimum(t + 1, n_tok - 1)
        for cp in meta_copies(nxt, 1 - s):
            cp.start()

        def u_chunk(c, carry):
            slot = c % PEER_RING
            gather(u_hbm, ubuf, usem, slot, zero_rows).wait()

            def dot_step(q, accs):
                cols = [pl.ds(pl.multiple_of((q * PEER_QUAD + j) * SC_LANES, SC_LANES), SC_LANES)
                        for j in range(PEER_QUAD)]
                hs = [plsc.bitcast(h_v[s, col], BF16) for col in cols]
                out = []
                for r in range(PEER_CH):
                    p = plsc.bitcast(ubuf[slot, r, cols[0]], BF16) * hs[0]
                    for j in range(1, PEER_QUAD):
                        p = p + plsc.bitcast(ubuf[slot, r, cols[j]], BF16) * hs[j]
                    lo, hi = _unpack_pair(plsc.bitcast(p, jnp.int32))
                    out.append(accs[r] + lo + hi)
                return tuple(out)

            accs = lax.fori_loop(0, PEER_NWG // PEER_QUAD, dot_step,
                                 tuple(jnp.zeros((SC_LANES,), F32) for _ in range(PEER_CH)))

            @pl.when(c == PEER_NCH - PEER_RING)
            def _():
                for cp in meta_copies(nxt, 1 - s):
                    cp.wait()

            start_ahead(u_hbm, ubuf, usem, s, c)
            for r in range(PEER_CH):
                pbuf[r, :] = accs[r]
            tot = jnp.zeros((SC_LANES,), F32)
            for j in range(SC_LANES):
                tot = tot + plsc.load_gather(pbuf, [lane, jnp.full((SC_LANES,), j, jnp.int32)])
            rows = pl.ds(pl.multiple_of(c * PEER_CH, PEER_CH), PEER_CH)
            w_v[rows] = gate_v[s, rows] * _gelu_tanh_via_exp(tot)
            return carry

        lax.fori_loop(0, PEER_NCH, u_chunk, 0)

        @pl.when(t >= 2)
        def _():
            pltpu.make_async_copy(out_v.at[s], o_hbm.at[tok], osem.at[s]).wait()

        def v_chunk(c, carry):
            slot = c % PEER_RING
            gather(v_hbm, vbuf, vsem, slot, zero_rows).wait()
            ws = []
            for r in range(PEER_CH):
                w = plsc.load_gather(w_v, [jnp.full((SC_LANES,), r, jnp.int32) + c * PEER_CH])
                ws.append(plsc.pack(w, w, format=plsc.PackFormat.INTERLEAVED,
                                    preferred_element_type=BF16))
            first = c == 0

            @plsc.parallel_loop(0, PEER_NWG, unroll=2)
            def acc_step(g):
                col = pl.ds(pl.multiple_of(g * SC_LANES, SC_LANES), SC_LANES)
                col_o = pl.ds(pl.multiple_of(PEER_WORDS + g * SC_LANES, SC_LANES), SC_LANES)
                oe = jnp.where(first, 0.0, out_v[s, col])
                oo = jnp.where(first, 0.0, out_v[s, col_o])
                for r0 in range(0, PEER_CH, PEER_QUAD):
                    p = plsc.bitcast(vbuf[slot, r0, col], BF16) * ws[r0]
                    for r in range(r0 + 1, r0 + PEER_QUAD):
                        p = p + plsc.bitcast(vbuf[slot, r, col], BF16) * ws[r]
                    lo, hi = _unpack_pair(plsc.bitcast(p, jnp.int32))
                    oe = oe + lo
                    oo = oo + hi
                out_v[s, col] = oe
                out_v[s, col_o] = oo

            start_ahead(v_hbm, vbuf, vsem, s, c)
            return carry

        lax.fori_loop(0, PEER_NCH, v_chunk, 0)
        pltpu.make_async_copy(out_v.at[s], o_hbm.at[tok], osem.at[s]).start()
        return carry

    for cp in meta_copies(base, 0):
        cp.start()
    for cp in meta_copies(base, 0):
        cp.wait()
    for c in range(PEER_RING):
        rows = idx_v[0, pl.ds(c * PEER_CH, PEER_CH)]
        gather(u_hbm, ubuf, usem, c, rows).start()
        gather(v_hbm, vbuf, vsem, c, rows).start()
    lax.fori_loop(0, n_tok, token, 0)
    for c in range(PEER_RING):
        gather(u_hbm, ubuf, usem, c, zero_rows).wait()
        gather(v_hbm, vbuf, vsem, c, zero_rows).wait()
    for s in range(2):
        pltpu.make_async_copy(out_v.at[s], o_hbm.at[base], osem.at[s]).wait()


def _pack_bf16_pairs(tab):
    b = lax.bitcast_convert_type(tab.astype(BF16), jnp.uint16).astype(jnp.uint32)
    half = tab.shape[1] // 2
    return lax.bitcast_convert_type(b[:, :half] | (b[:, half:] << 16), jnp.int32)


def _peer(idx, hq, gates, u_words, v_words):
    T = hq.shape[0]
    assert T % (2 * SC_WORKERS) == 0
    mesh = plsc.VectorSubcoreMesh(core_axis_name="c", subcore_axis_name="s",
                                  num_cores=SC_CORES, num_subcores=SC_SUBCORES)
    return pl.kernel(
        _peer_sc_body,
        out_type=jax.ShapeDtypeStruct((T, D_MODEL), F32),
        mesh=mesh,
        scratch_types=[
            pltpu.VMEM((2, PEER_SEL), jnp.int32), pltpu.VMEM((2, PEER_SEL), F32),
            pltpu.VMEM((2, PEER_WORDS), jnp.int32),
            pltpu.VMEM((PEER_RING, PEER_CH, PEER_WORDS), jnp.int32),
            pltpu.VMEM((PEER_RING, PEER_CH, PEER_WORDS), jnp.int32),
            pltpu.VMEM((PEER_CH, SC_LANES), F32), pltpu.VMEM((PEER_SEL,), F32),
            pltpu.VMEM((2, D_MODEL), F32),
            pltpu.SemaphoreType.DMA((PEER_RING,)), pltpu.SemaphoreType.DMA((PEER_RING,)),
            pltpu.SemaphoreType.DMA((2,)), pltpu.SemaphoreType.DMA((2,)),
        ],
        compiler_params=pltpu.CompilerParams(needs_layout_passes=False),
        name="peer_sc",
    )(idx, gates, _pack_bf16_pairs(hq), u_words, v_words)


FINAL_TS = 256


def _final_kernel(x1_ref, pe_ref, p_ref, gp_ref, wg_ref, wp_ref, gf_ref, o_ref):
    x2 = x1_ref[...] + pe_ref[...]
    e = _dot(p_ref[...].astype(BF16), wp_ref[...])
    gate = jax.nn.sigmoid(_dot(_rms(x2, gp_ref[...]).astype(BF16), wg_ref[...]))
    o_ref[...] = _rms(x2 + gate * e, gf_ref[...])


def _final(x1, peer_out, p2d, g_ple, ple_w_gate, ple_w_proj, g_final):
    T = x1.shape[0]
    ts = min(FINAL_TS, T)
    tok = lambda d: pl.BlockSpec((ts, d), lambda i: (i, 0))
    full = lambda shape: pl.BlockSpec(shape, lambda i: (0,) * len(shape))
    return pl.pallas_call(
        _final_kernel,
        grid=(T // ts,),
        in_specs=[tok(D_MODEL), tok(D_MODEL), tok(D_PLE), full((1, D_MODEL)),
                  full((D_MODEL, D_MODEL)), full((D_PLE, D_MODEL)), full((1, D_MODEL))],
        out_specs=tok(D_MODEL),
        out_shape=jax.ShapeDtypeStruct((T, D_MODEL), F32),
        compiler_params=pltpu.CompilerParams(
            dimension_semantics=("parallel",), vmem_limit_bytes=VMEM_LIMIT),
        name="final",
    )(x1, peer_out, p2d, g_ple.reshape(1, D_MODEL), ple_w_gate.astype(BF16),
      ple_w_proj.astype(BF16), g_final.reshape(1, D_MODEL))


def kernel(x, p, positions, g_mix, w_in, ssm_log_dt, ssm_a_re, ssm_a_im, ssm_b_re, ssm_b_im,
           ssm_c_re, ssm_c_im, ssm_d, ssm_w_glu, w_proj_ssm, w_proj_att, w_out, g_ffn,
           peer_w_q, peer_keys1, peer_keys2, peer_u, peer_v, g_ple, ple_w_gate, ple_w_proj,
           g_final):
    B, S, _ = x.shape
    T = B * S
    assert w_in.shape[0] == 1, "the final rmsnorm is fused into the single layer's last stage"
    for i in range(1):
        u_sb, q, k, v, ga, gb = _in_proj(x, positions, g_mix[i], w_in[i])
        tables = _s5_tables(ssm_log_dt[i], ssm_a_re[i], ssm_a_im[i], ssm_b_re[i], ssm_b_im[i],
                            ssm_c_re[i], ssm_c_im[i])
        ys = _s5(u_sb, tables, ssm_d[i], ssm_w_glu[i], B, S)
        u_words = _pack_bf16_pairs(peer_u[i])
        v_words = _pack_bf16_pairs(peer_v[i])
        outs = []
        for b in range(B):
            att = _moba(q[b:b + 1], k[b:b + 1], v[b:b + 1])
            x1, hq, scores = _merge(
                x[b], ys[b], att[0], ga[b], gb[b],
                w_proj_ssm[i], w_proj_att[i], w_out[i], g_ffn[i], peer_w_q[i],
                peer_keys1[i], peer_keys2[i])
            idx, gates = _topk(scores)
            peer_out = _peer(idx, hq, gates, u_words, v_words)
            outs.append(_final(x1, peer_out, p[i, b], g_ple[i], ple_w_gate[i],
                               ple_w_proj[i], g_final))
        x = jnp.stack(outs)
    return x
```

```python
import functools
import math

import jax
import jax.numpy as jnp
from jax import lax
from jax.experimental import pallas as pl
from jax.experimental.pallas import tpu as pltpu
from jax.experimental.pallas import tpu_sc as plsc

F32 = jnp.float32
BF16 = jnp.bfloat16

D_MODEL = 1024
D_SSM = 512
SSM_GROUP = 16
SSM_GROUPS = 32
SSM_STATE = 64
D_STATE = SSM_GROUPS * SSM_STATE
N_HEADS = 8
HEAD_DIM = 64
D_ATT = 512
ROT_DIM = 16
ROPE_THETA = 500000.0
MOBA_BLOCK = 256
MOBA_TOPK = 3
PEER_HEADS = 8
PEER_KEYS = 128
PEER_QDIM = 256
PEER_HALF = 128
PEER_TOPK = 16
PEER_SEL = PEER_HEADS * PEER_TOPK
D_PLE = 256
EPS = 1e-6
NEG = -1e30
LANES = 128
SUBLANES = 8
VMEM_LIMIT = 48 * 1024 * 1024
HIGHEST = lax.Precision.HIGHEST


def _rms(x, g):
    return x * lax.rsqrt(jnp.mean(x * x, axis=-1, keepdims=True) + EPS) * g


def _dot(a, b):
    return jnp.dot(a, b, preferred_element_type=F32)


def _dot_nt(a, b, precision=None):
    return lax.dot_general(a, b, (((1,), (1,)), ((), ())), precision=precision,
                           preferred_element_type=F32)


IN_TS = 512


def _in_proj_kernel(x_ref, pos_ref, g_ref, w_ref, invf_ref,
                    u_ref, q_ref, k_ref, v_ref, ga_ref, gb_ref):
    h = _rms(x_ref[...], g_ref[...]).astype(BF16)

    def proj(lo, hi):
        return _dot(h, w_ref[:, lo:hi])

    u_ref[...] = proj(0, D_SSM).astype(BF16)
    ang = pos_ref[...].astype(F32) * invf_ref[...]
    cos = jnp.cos(ang)
    sin = jnp.sin(ang)
    lane = lax.broadcasted_iota(jnp.int32, (1, LANES), 1) % HEAD_DIM
    half = ROT_DIM // 2
    sin_hi = jnp.where((lane >= half) & (lane < ROT_DIM), sin, 0.0)
    sin_lo = jnp.where(lane < half, -sin, 0.0)
    reps = D_ATT // LANES
    cos4 = jnp.concatenate([cos] * reps, axis=1)
    sin_hi4 = jnp.concatenate([sin_hi] * reps, axis=1)
    sin_lo4 = jnp.concatenate([sin_lo] * reps, axis=1)

    def rope(t):
        return (t * cos4 + pltpu.roll(t, half, 1) * sin_hi4
                + pltpu.roll(t, D_ATT - half, 1) * sin_lo4)

    q = rope(proj(D_SSM, D_SSM + D_ATT))
    q_ref[...] = (q * (HEAD_DIM ** -0.5)).astype(BF16)
    k_ref[...] = rope(proj(D_SSM + D_ATT, D_SSM + 2 * D_ATT)).astype(BF16)
    v_ref[...] = proj(D_SSM + 2 * D_ATT, D_SSM + 3 * D_ATT).astype(BF16)
    o = D_SSM + 3 * D_ATT
    ga_ref[...] = jax.nn.sigmoid(proj(o, o + D_MODEL)).astype(BF16)
    gb_ref[...] = jax.nn.sigmoid(proj(o + D_MODEL, o + 2 * D_MODEL)).astype(BF16)


def _in_proj(x, positions, g_mix, w_in):
    B, S, _ = x.shape
    ts = min(IN_TS, S)
    inv_freq = ROPE_THETA ** (-jnp.arange(0, ROT_DIM, 2, dtype=F32) / ROT_DIM)
    lane = jnp.arange(LANES) % HEAD_DIM
    invf = jnp.where(lane < ROT_DIM, inv_freq[lane % (ROT_DIM // 2)], 0.0).reshape(1, LANES)
    d_in = w_in.shape[1]
    tok = lambda d: pl.BlockSpec((None, ts, d), lambda b, i: (b, i, 0))
    full = lambda shape: pl.BlockSpec(shape, lambda b, i: (0,) * len(shape))
    outs = pl.pallas_call(
        _in_proj_kernel,
        grid=(B, S // ts),
        in_specs=[tok(D_MODEL), tok(1), full((1, D_MODEL)), full((D_MODEL, d_in)), full((1, LANES))],
        out_specs=[pl.BlockSpec((ts, D_SSM), lambda b, i: (i, b)),
                   tok(D_ATT), tok(D_ATT), tok(D_ATT), tok(D_MODEL), tok(D_MODEL)],
        out_shape=[jax.ShapeDtypeStruct((S, B * D_SSM), BF16),
                   jax.ShapeDtypeStruct((B, S, D_ATT), BF16),
                   jax.ShapeDtypeStruct((B, S, D_ATT), BF16),
                   jax.ShapeDtypeStruct((B, S, D_ATT), BF16),
                   jax.ShapeDtypeStruct((B, S, D_MODEL), BF16),
                   jax.ShapeDtypeStruct((B, S, D_MODEL), BF16)],
        compiler_params=pltpu.CompilerParams(
            dimension_semantics=("parallel", "parallel"), vmem_limit_bytes=VMEM_LIMIT),
        name="in_proj",
    )(x, positions.reshape(B, S, 1), g_mix.reshape(1, D_MODEL), w_in.astype(BF16), invf)
    return outs


S5_TS = 128
S5_BATCH = 4
S5_COLS = 512


def _s5_kernel(u_ref, bre_ref, bim_ref, a1r_ref, a1i_ref, pr_ref, pi_ref,
               cre_ref, cim_ref, d_ref, wglu_ref, y_ref,
               xr, xi, cr, ci, ysc):
    rows = xr.shape[0]
    ts = rows // S5_BATCH

    @pl.when(pl.program_id(0) == 0)
    def _():
        cr[...] = jnp.zeros_like(cr)
        ci[...] = jnp.zeros_like(ci)

    u = u_ref[...]
    xr[...] = _dot(u, bre_ref[...])
    xi[...] = _dot(u, bim_ref[...])

    hi_rows = lax.broadcasted_iota(jnp.int32, (SUBLANES, S5_COLS), 0) >= S5_BATCH
    for cb in range(D_STATE // S5_COLS):
        sl = slice(cb * S5_COLS, (cb + 1) * S5_COLS)
        a_r, a_i = a1r_ref[:, sl], a1i_ref[:, sl]
        p_r, p_i = pr_ref[:, sl], pi_ref[:, sl]

        def body(t, carry):
            c_r, c_i = carry
            r0 = pl.multiple_of(t * SUBLANES, SUBLANES)
            x_r = xr[pl.ds(r0, SUBLANES), sl]
            x_i = xi[pl.ds(r0, SUBLANES), sl]
            s_r = pltpu.roll(x_r, S5_BATCH, 0)
            s_i = pltpu.roll(x_i, S5_BATCH, 0)
            h_r = x_r + (a_r * s_r - a_i * s_i) + (p_r * c_r - p_i * c_i)
            h_i = x_i + (a_r * s_i + a_i * s_r) + (p_r * c_i + p_i * c_r)
            xr[pl.ds(r0, SUBLANES), sl] = h_r
            xi[pl.ds(r0, SUBLANES), sl] = h_i
            n_r = jnp.where(hi_rows, h_r, pltpu.roll(h_r, S5_BATCH, 0))
            n_i = jnp.where(hi_rows, h_i, pltpu.roll(h_i, S5_BATCH, 0))
            return n_r, n_i

        c_r, c_i = lax.fori_loop(0, rows // SUBLANES, body, (cr[:, sl], ci[:, sl]), unroll=2)
        cr[:, sl] = c_r
        ci[:, sl] = c_i

    y = (_dot(xr[...].astype(BF16), cre_ref[...]) - _dot(xi[...].astype(BF16), cim_ref[...])
         + d_ref[...] * u.astype(F32))
    y = jax.nn.gelu(y)
    y = y * jax.nn.sigmoid(_dot(y.astype(BF16), wglu_ref[...]))
    for c in range(D_SSM // LANES):
        ysc[c] = y[:, c * LANES:(c + 1) * LANES]
    for b in range(S5_BATCH):
        for c in range(D_SSM // LANES):
            y_ref[b, :, c * LANES:(c + 1) * LANES] = (
                ysc[c, pl.ds(b, ts, stride=S5_BATCH), :].astype(BF16))


def _s5_tables(log_dt, a_re, a_im, b_re, b_im, c_re, c_im):
    dt = jnp.exp(log_dt.astype(F32))[:, None]
    ar, ai = a_re.astype(F32), a_im.astype(F32)
    mag = jnp.exp(dt * ar)
    abar_re, abar_im = mag * jnp.cos(dt * ai), mag * jnp.sin(dt * ai)
    den = ar * ar + ai * ai
    nr, ni = abar_re - 1.0, abar_im
    f_re = (nr * ar + ni * ai) / den
    f_im = (ni * ar - nr * ai) / den
    br, bi = b_re.astype(F32), b_im.astype(F32)
    bb_re = f_re[..., None] * br - f_im[..., None] * bi
    bb_im = f_re[..., None] * bi + f_im[..., None] * br
    eye = jnp.eye(SSM_GROUPS, dtype=F32)

    def in_blockdiag(bb):
        return jnp.einsum('gnc,gh->gchn', bb, eye).reshape(D_SSM, D_STATE)

    def out_blockdiag(c):
        return jnp.einsum('gcn,gh->gnhc', c.astype(F32), eye).reshape(D_STATE, D_SSM)

    a_r = abar_re.reshape(1, D_STATE)
    a_i = abar_im.reshape(1, D_STATE)
    a2_r = a_r * a_r - a_i * a_i
    a2_i = 2.0 * a_r * a_i
    hi = (jnp.arange(SUBLANES) >= S5_BATCH)[:, None]
    a1r = jnp.where(hi, a_r, 0.0)
    a1i = jnp.where(hi, a_i, 0.0)
    p_r = jnp.where(hi, a2_r, a_r)
    p_i = jnp.where(hi, a2_i, a_i)
    return (in_blockdiag(bb_re).astype(BF16), in_blockdiag(bb_im).astype(BF16),
            a1r, a1i, p_r, p_i,
            out_blockdiag(c_re).astype(BF16), out_blockdiag(c_im).astype(BF16))


def _s5(u_sb, tables, d_skip, w_glu, B, S):
    assert B == S5_BATCH
    ts = min(S5_TS, S)
    rows = ts * B
    bre, bim, a1r, a1i, p_r, p_i, cre, cim = tables
    full = lambda shape: pl.BlockSpec(shape, lambda i: (0,) * len(shape))
    return pl.pallas_call(
        _s5_kernel,
        grid=(S // ts,),
        in_specs=[pl.BlockSpec((rows, D_SSM), lambda i: (i, 0)),
                  full((D_SSM, D_STATE)), full((D_SSM, D_STATE)),
                  full((SUBLANES, D_STATE)), full((SUBLANES, D_STATE)),
                  full((SUBLANES, D_STATE)), full((SUBLANES, D_STATE)),
                  full((D_STATE, D_SSM)), full((D_STATE, D_SSM)),
                  full((1, D_SSM)), full((D_SSM, D_SSM))],
        out_specs=pl.BlockSpec((B, ts, D_SSM), lambda i: (0, i, 0)),
        out_shape=jax.ShapeDtypeStruct((B, S, D_SSM), BF16),
        scratch_shapes=[pltpu.VMEM((rows, D_STATE), F32), pltpu.VMEM((rows, D_STATE), F32),
                        pltpu.VMEM((SUBLANES, D_STATE), F32), pltpu.VMEM((SUBLANES, D_STATE), F32),
                        pltpu.VMEM((D_SSM // LANES, rows, LANES), F32)],
        compiler_params=pltpu.CompilerParams(
            dimension_semantics=("arbitrary",), vmem_limit_bytes=VMEM_LIMIT),
        name="s5",
    )(u_sb.reshape(S * B, D_SSM), bre, bim, a1r, a1i, p_r, p_i, cre, cim,
      d_skip.reshape(1, D_SSM).astype(F32), w_glu.astype(BF16))


MOBA_PAIR = 2 * MOBA_BLOCK


def _moba_kernel(q0, q_ref, k_ref, v_ref, o_ref, kmean, kaug_a, kaug_b, vaug_a, vaug_b, m_s, acc_s,
                 s_buf):
    qi = pl.program_id(1) + q0
    nb = k_ref.shape[0] // MOBA_BLOCK
    nbp = kmean.shape[0]
    lane = lax.broadcasted_iota(jnp.int32, (1, LANES), 1)
    head_a = lane < HEAD_DIM

    @pl.when(pl.program_id(1) == 0)
    def _():
        kmean[...] = jnp.zeros_like(kmean)
        for j in range(nb):
            rows = pl.ds(j * MOBA_BLOCK, MOBA_BLOCK)
            kj = k_ref[rows, :].astype(F32)
            vj = v_ref[rows, :].astype(F32)
            kmean[j:j + 1, :] = jnp.sum(kj, axis=0, keepdims=True) * (1.0 / MOBA_BLOCK)
            kaug_a[rows, :] = jnp.where(head_a, kj, jnp.where(lane - HEAD_DIM == j, 1.0, 0.0)).astype(BF16)
            kaug_b[rows, :] = jnp.where(head_a, jnp.where(lane == j, 1.0, 0.0), kj).astype(BF16)
            vaug_a[rows, :] = jnp.where(head_a, vj, 1.0).astype(BF16)
            vaug_b[rows, :] = jnp.where(head_a, 1.0, vj).astype(BF16)

    qf = q_ref[...].astype(F32)
    blk_row = lax.broadcasted_iota(jnp.int32, (nbp, MOBA_BLOCK), 0)
    q_augs = []
    for is_a in (True, False):
        mine = head_a if is_a else jnp.logical_not(head_a)
        q_own = jnp.where(mine, qf, 0.0)
        g = _dot_nt(kmean[...], q_own, precision=HIGHEST)
        g = jnp.where(blk_row < qi, g, NEG)
        sel = jnp.zeros(g.shape, F32)
        for _ in range(MOBA_TOPK):
            m = jnp.max(g, axis=0, keepdims=True)
            idx = jnp.min(jnp.where(g == m, blk_row, nbp), axis=0, keepdims=True)
            hit = blk_row == idx
            sel = jnp.where(hit, jnp.where(idx < qi, 1.0, 0.0), sel)
            g = jnp.where(hit, -jnp.inf, g)
        bias_t = jnp.where(sel > 0.0, 0.0, jnp.where(blk_row == qi, 0.0, NEG))
        bias_t = jnp.concatenate([bias_t, jnp.full((LANES - nbp, MOBA_BLOCK), NEG, F32)], axis=0)
        bias = jnp.transpose(bias_t)
        if is_a:
            bias = pltpu.roll(bias, HEAD_DIM, 1)
        q_augs.append(jnp.where(mine, qf, bias).astype(BF16))

    m_s[...] = jnp.full(m_s.shape, -jnp.inf, F32)
    acc_s[...] = jnp.zeros_like(acc_s)
    qpos = qi * MOBA_BLOCK + lax.broadcasted_iota(jnp.int32, (MOBA_BLOCK, MOBA_PAIR), 0)
    col = lax.broadcasted_iota(jnp.int32, (MOBA_BLOCK, MOBA_PAIR), 1)

    def kv_rows(jj):
        return pl.ds(pl.multiple_of(jj * MOBA_PAIR, MOBA_PAIR), MOBA_PAIR)

    def scores(jj, slot):
        for hd, kaug in enumerate((kaug_a, kaug_b)):
            s_buf[slot, hd] = _dot_nt(q_augs[hd], kaug[kv_rows(jj), :])

    def softmax_pv(jj, slot, causal):
        for hd, vaug in enumerate((vaug_a, vaug_b)):
            s = s_buf[slot, hd]
            if causal:
                s = jnp.where(jj * MOBA_PAIR + col <= qpos, s, NEG)
            m_old = m_s[hd]
            m_new = jnp.maximum(m_old, jnp.max(s, axis=-1, keepdims=True))
            alpha = jnp.exp(m_old - m_new)
            p = jnp.exp(s - m_new)
            m_s[hd] = m_new
            acc_s[hd] = alpha * acc_s[hd] + _dot(p.astype(BF16), vaug[kv_rows(jj), :])

    last = qi // 2
    scores(0, 0)

    def body(k, _):
        scores(2 * k + 1, 1)
        softmax_pv(2 * k, 0, False)
        scores(2 * k + 2, 0)
        softmax_pv(2 * k + 1, 1, False)
        return 0

    lax.fori_loop(0, last // 2, body, 0)

    @pl.when(last % 2 == 0)
    def _():
        softmax_pv(last, 0, True)

    @pl.when(last % 2 == 1)
    def _():
        scores(last, 1)
        softmax_pv(last - 1, 0, False)
        softmax_pv(last, 1, True)
    acc_a, acc_b = acc_s[0], acc_s[1]
    o_ref[...] = jnp.where(head_a, acc_a / pltpu.roll(acc_a, HEAD_DIM, 1),
                           acc_b / pltpu.roll(acc_b, HEAD_DIM, 1)).astype(BF16)


def _moba(q, k, v, b, q0, nq):
    skv = (q0 + nq) * MOBA_BLOCK
    nb = skv // MOBA_BLOCK
    assert nb <= HEAD_DIM and nb % 2 == 0 and q.shape[1] % skv == 0
    nbp = -(-nb // SUBLANES) * SUBLANES
    seq = pl.BlockSpec((None, skv, LANES), lambda h, i: (b, 0, h))
    return pl.pallas_call(
        functools.partial(_moba_kernel, q0),
        grid=(D_ATT // LANES, nq),
        in_specs=[pl.BlockSpec((None, MOBA_BLOCK, LANES), lambda h, i: (b, q0 + i, h)), seq, seq],
        out_specs=pl.BlockSpec((MOBA_BLOCK, LANES), lambda h, i: (i, h)),
        out_shape=jax.ShapeDtypeStruct((nq * MOBA_BLOCK, D_ATT), BF16),
        scratch_shapes=[pltpu.VMEM((nbp, LANES), F32),
                        pltpu.VMEM((skv, LANES), BF16), pltpu.VMEM((skv, LANES), BF16),
                        pltpu.VMEM((skv, LANES), BF16), pltpu.VMEM((skv, LANES), BF16),
                        pltpu.VMEM((2, MOBA_BLOCK, 1), F32),
                        pltpu.VMEM((2, MOBA_BLOCK, LANES), F32),
                        pltpu.VMEM((2, 2, MOBA_BLOCK, MOBA_PAIR), F32)],
        compiler_params=pltpu.CompilerParams(
            dimension_semantics=("parallel", "arbitrary"), vmem_limit_bytes=VMEM_LIMIT),
        name="moba",
    )(q, k, v)


MERGE_TS = 256


def _merge_kernel(x_ref, ys_ref, at_ref, ga_ref, gb_ref, wa_ref, wb_ref, wo_ref, g_ref,
                  wq_ref, k1_ref, k2_ref, x1_ref, hq_ref, sc_ref):
    ya = _dot(ys_ref[...], wa_ref[...])
    yb = _dot(at_ref[...], wb_ref[...])
    merged = ga_ref[...].astype(F32) * ya + gb_ref[...].astype(F32) * yb
    x1 = x_ref[...] + _dot(merged.astype(BF16), wo_ref[...])
    x1_ref[...] = x1
    hq = _rms(x1, g_ref[...])
    hq_ref[...] = hq
    qp = _dot(hq.astype(BF16), wq_ref[...])
    for h in range(PEER_HEADS):
        o = h * PEER_QDIM
        sc_ref[2 * h] = _dot_nt(k1_ref[h], qp[:, o:o + PEER_HALF], precision=HIGHEST)
        sc_ref[2 * h + 1] = _dot_nt(k2_ref[h], qp[:, o + PEER_HALF:o + PEER_QDIM], precision=HIGHEST)


def _merge(x2d, ys, att, ga, gb, w_proj_ssm, w_proj_att, w_out, g_ffn, peer_w_q, keys1, keys2):
    T = x2d.shape[0]
    ts = min(MERGE_TS, T)
    tok = lambda d: pl.BlockSpec((ts, d), lambda i: (i, 0))
    full = lambda shape: pl.BlockSpec(shape, lambda i: (0,) * len(shape))
    qd = PEER_HEADS * PEER_QDIM
    return pl.pallas_call(
        _merge_kernel,
        grid=(T // ts,),
        in_specs=[tok(D_MODEL), tok(D_SSM), tok(D_ATT), tok(D_MODEL), tok(D_MODEL),
                  full((D_SSM, D_MODEL)), full((D_ATT, D_MODEL)), full((D_MODEL, D_MODEL)),
                  full((1, D_MODEL)), full((D_MODEL, qd)),
                  full((PEER_HEADS, PEER_KEYS, PEER_HALF)), full((PEER_HEADS, PEER_KEYS, PEER_HALF))],
        out_specs=[tok(D_MODEL), tok(D_MODEL),
                   pl.BlockSpec((2 * PEER_HEADS, PEER_KEYS, ts), lambda i: (0, 0, i))],
        out_shape=[jax.ShapeDtypeStruct((T, D_MODEL), F32),
                   jax.ShapeDtypeStruct((T, D_MODEL), F32),
                   jax.ShapeDtypeStruct((2 * PEER_HEADS, PEER_KEYS, T), F32)],
        compiler_params=pltpu.CompilerParams(
            dimension_semantics=("parallel",), vmem_limit_bytes=VMEM_LIMIT),
        name="merge",
    )(x2d, ys, att, ga, gb, w_proj_ssm.astype(BF16), w_proj_att.astype(BF16), w_out.astype(BF16),
      g_ffn.reshape(1, D_MODEL), peer_w_q.astype(BF16), keys1, keys2)


TOPK_TS = 256


def _top_rows(s, row, k):
    vals, idxs = [], []
    for _ in range(k):
        m = jnp.max(s, axis=0, keepdims=True)
        idx = jnp.min(jnp.where(s == m, row, s.shape[0]), axis=0, keepdims=True)
        vals.append(m)
        idxs.append(idx)
        s = jnp.where(row == idx, -jnp.inf, s)
    return vals, idxs


def _stack_rows(rows, row16):
    acc = jnp.zeros(row16.shape, rows[0].dtype)
    for r, v in enumerate(rows):
        acc = jnp.where(row16 == r, v, acc)
    return acc


def _topk_kernel(sc_ref, idx_ref, gate_ref):
    ts = sc_ref.shape[-1]
    row = lax.broadcasted_iota(jnp.int32, (PEER_KEYS, ts), 0)
    row16 = lax.broadcasted_iota(jnp.int32, (PEER_TOPK, ts), 0)
    row8 = lax.broadcasted_iota(jnp.int32, (SUBLANES, ts), 0)
    counts = [PEER_TOPK // (i + 1) for i in range(PEER_TOPK)]
    heights = [PEER_TOPK if c > SUBLANES else SUBLANES for c in counts]
    n_cand = sum(heights)
    rowc = lax.broadcasted_iota(jnp.int32, (n_cand, ts), 0)
    gate_rows, eid_rows = [], []
    for h in range(PEER_HEADS):
        v1, i1 = _top_rows(sc_ref[2 * h], row, PEER_TOPK)
        v2, i2 = _top_rows(sc_ref[2 * h + 1], row, PEER_TOPK)
        v2s = _stack_rows(v2, row16)
        i2s = _stack_rows(i2, row16).astype(F32)
        cand, eid = [], []
        for i in range(PEER_TOPK):
            n = heights[i]
            cand.append(jnp.where((row16 if n == PEER_TOPK else row8) < counts[i],
                                  v1[i] + v2s[:n], -jnp.inf))
            eid.append(i1[i].astype(F32) * PEER_KEYS + i2s[:n])
        cand = jnp.concatenate(cand, axis=0)
        eid = jnp.concatenate(eid, axis=0)
        tops, picks = [], []
        for _ in range(PEER_TOPK):
            m = jnp.max(cand, axis=0, keepdims=True)
            pos = jnp.min(jnp.where(cand == m, rowc, n_cand), axis=0, keepdims=True)
            hit = rowc == pos
            picks.append(jnp.max(jnp.where(hit, eid, -1.0), axis=0, keepdims=True))
            tops.append(m)
            cand = jnp.where(hit, -jnp.inf, cand)
        top = _stack_rows(tops, row16)
        p = jnp.exp(top - jnp.max(top, axis=0, keepdims=True))
        gate_rows.append(p / jnp.sum(p, axis=0, keepdims=True))
        eid_rows.append(_stack_rows(picks, row16))
    gate_ref[...] = jnp.transpose(jnp.concatenate(gate_rows, axis=0))
    idx_ref[...] = jnp.transpose(jnp.concatenate(eid_rows, axis=0)).astype(jnp.int32)


def _topk(scores):
    T = scores.shape[-1]
    ts = min(TOPK_TS, T)
    return pl.pallas_call(
        _topk_kernel,
        grid=(T // ts,),
        in_specs=[pl.BlockSpec((2 * PEER_HEADS, PEER_KEYS, ts), lambda i: (0, 0, i))],
        out_specs=[pl.BlockSpec((ts, PEER_SEL), lambda i: (i, 0)),
                   pl.BlockSpec((ts, PEER_SEL), lambda i: (i, 0))],
        out_shape=[jax.ShapeDtypeStruct((T, PEER_SEL), jnp.int32),
                   jax.ShapeDtypeStruct((T, PEER_SEL), F32)],
        compiler_params=pltpu.CompilerParams(
            dimension_semantics=("parallel",), vmem_limit_bytes=VMEM_LIMIT),
        name="topk",
    )(scores)


SC_CORES = 2
SC_SUBCORES = 16
SC_LANES = 16
SC_WORKERS = SC_CORES * SC_SUBCORES
PEER_CH = SC_LANES
PEER_NCH = PEER_SEL // PEER_CH
PEER_WORDS = D_MODEL // 2
PEER_NWG = PEER_WORDS // SC_LANES
PEER_RING = 4
PEER_QUAD = 4
HI_MASK = -65536
GELU_C = 0.7978845608028654


def _gelu_tanh_via_exp(x):
    z = GELU_C * (x + 0.044715 * (x * x * x))
    t = 1.0 - 2.0 / (jnp.exp(2.0 * z) + 1.0)
    return 0.5 * x * (1.0 + t)


def _unpack_pair(w):
    lo = plsc.bitcast(lax.shift_left(w, 16), F32)
    hi = plsc.bitcast(lax.bitwise_and(w, HI_MASK), F32)
    return lo, hi


def _peer_sc_body(idx_hbm, gate_hbm, h_hbm, u_hbm, v_hbm, o_hbm,
                  idx_v, gate_v, h_v, ubuf, vbuf, pbuf, w_v, out_v, usem, vsem, msem, osem):
    n_tok = o_hbm.shape[0] // SC_WORKERS
    base = (lax.axis_index("s") * SC_CORES + lax.axis_index("c")) * n_tok
    lane = lax.iota(jnp.int32, SC_LANES)
    zero_rows = jnp.zeros((SC_LANES,), jnp.int32)

    def meta_copies(tok, s):
        return (pltpu.make_async_copy(idx_hbm.at[tok], idx_v.at[s], msem.at[s]),
                pltpu.make_async_copy(gate_hbm.at[tok], gate_v.at[s], msem.at[s]),
                pltpu.make_async_copy(h_hbm.at[tok], h_v.at[s], msem.at[s]))

    def gather(tab, buf, sem, slot, rows):
        return pltpu.make_async_copy(tab.at[rows], buf.at[slot], sem.at[slot])

    def start_ahead(tab, buf, sem, s, c):
        ahead = c + PEER_RING
        src = jnp.where(ahead < PEER_NCH, s, 1 - s)
        ch = ahead % PEER_NCH
        rows = idx_v[src, pl.ds(pl.multiple_of(ch * PEER_CH, PEER_CH), PEER_CH)]
        gather(tab, buf, sem, c % PEER_RING, rows).start()

    def token(t, carry):
        s = t % 2
        tok = base + t
        nxt = base + jnp.minimum(t + 1, n_tok - 1)
        for cp in meta_copies(nxt, 1 - s):
            cp.start()

        def u_chunk(c, carry):
            slot = c % PEER_RING
            gather(u_hbm, ubuf, usem, slot, zero_rows).wait()

            def dot_step(q, accs):
                cols = [pl.ds(pl.multiple_of((q * PEER_QUAD + j) * SC_LANES, SC_LANES), SC_LANES)
                        for j in range(PEER_QUAD)]
                hs = [plsc.bitcast(h_v[s, col], BF16) for col in cols]
                out = []
                for r in range(PEER_CH):
                    p = plsc.bitcast(ubuf[slot, r, cols[0]], BF16) * hs[0]
                    for j in range(1, PEER_QUAD):
                        p = p + plsc.bitcast(ubuf[slot, r, cols[j]], BF16) * hs[j]
                    lo, hi = _unpack_pair(plsc.bitcast(p, jnp.int32))
                    out.append(accs[r] + lo + hi)
                return tuple(out)

            accs = lax.fori_loop(0, PEER_NWG // PEER_QUAD, dot_step,
                                 tuple(jnp.zeros((SC_LANES,), F32) for _ in range(PEER_CH)))

            @pl.when(c == PEER_NCH - PEER_RING)
            def _():
                for cp in meta_copies(nxt, 1 - s):
                    cp.wait()

            start_ahead(u_hbm, ubuf, usem, s, c)
            for r in range(PEER_CH):
                pbuf[r, :] = accs[r]
            tot = jnp.zeros((SC_LANES,), F32)
            for j in range(SC_LANES):
                tot = tot + plsc.load_gather(pbuf, [lane, jnp.full((SC_LANES,), j, jnp.int32)])
            rows = pl.ds(pl.multiple_of(c * PEER_CH, PEER_CH), PEER_CH)
            w_v[rows] = gate_v[s, rows] * _gelu_tanh_via_exp(tot)
            return carry

        lax.fori_loop(0, PEER_NCH, u_chunk, 0)

        @pl.when(t >= 2)
        def _():
            pltpu.make_async_copy(out_v.at[s], o_hbm.at[tok], osem.at[s]).wait()

        def v_chunk(c, carry):
            slot = c % PEER_RING
            gather(v_hbm, vbuf, vsem, slot, zero_rows).wait()
            ws = []
            for r in range(PEER_CH):
                w = plsc.load_gather(w_v, [jnp.full((SC_LANES,), r, jnp.int32) + c * PEER_CH])
                ws.append(plsc.pack(w, w, format=plsc.PackFormat.INTERLEAVED,
                                    preferred_element_type=BF16))
            first = c == 0

            @plsc.parallel_loop(0, PEER_NWG, unroll=2)
            def acc_step(g):
                col = pl.ds(pl.multiple_of(g * SC_LANES, SC_LANES), SC_LANES)
                col_o = pl.ds(pl.multiple_of(PEER_WORDS + g * SC_LANES, SC_LANES), SC_LANES)
                oe = jnp.where(first, 0.0, out_v[s, col])
                oo = jnp.where(first, 0.0, out_v[s, col_o])
                for r0 in range(0, PEER_CH, PEER_QUAD):
                    p = plsc.bitcast(vbuf[slot, r0, col], BF16) * ws[r0]
                    for r in range(r0 + 1, r0 + PEER_QUAD):
                        p = p + plsc.bitcast(vbuf[slot, r, col], BF16) * ws[r]
                    lo, hi = _unpack_pair(plsc.bitcast(p, jnp.int32))
                    oe = oe + lo
                    oo = oo + hi
                out_v[s, col] = oe
                out_v[s, col_o] = oo

            start_ahead(v_hbm, vbuf, vsem, s, c)
            return carry

        lax.fori_loop(0, PEER_NCH, v_chunk, 0)
        pltpu.make_async_copy(out_v.at[s], o_hbm.at[tok], osem.at[s]).start()
        return carry

    for cp in meta_copies(base, 0):
        cp.start()
    for cp in meta_copies(base, 0):
        cp.wait()
    for c in range(PEER_RING):
        rows = idx_v[0, pl.ds(c * PEER_CH, PEER_CH)]
        gather(u_hbm, ubuf, usem, c, rows).start()
        gather(v_hbm, vbuf, vsem, c, rows).start()
    lax.fori_loop(0, n_tok, token, 0)
    for c in range(PEER_RING):
        gather(u_hbm, ubuf, usem, c, zero_rows).wait()
        gather(v_hbm, vbuf, vsem, c, zero_rows).wait()
    for s in range(2):
        pltpu.make_async_copy(out_v.at[s], o_hbm.at[base], osem.at[s]).wait()


def _pack_bf16_pairs(tab):
    b = lax.bitcast_convert_type(tab.astype(BF16), jnp.uint16).astype(jnp.uint32)
    half = tab.shape[1] // 2
    return lax.bitcast_convert_type(b[:, :half] | (b[:, half:] << 16), jnp.int32)


def _peer(idx, hq, gates, u_words, v_words):
    T = hq.shape[0]
    assert T % (2 * SC_WORKERS) == 0
    mesh = plsc.VectorSubcoreMesh(core_axis_name="c", subcore_axis_name="s",
                                  num_cores=SC_CORES, num_subcores=SC_SUBCORES)
    return pl.kernel(
        _peer_sc_body,
        out_type=jax.ShapeDtypeStruct((T, D_MODEL), F32),
        mesh=mesh,
        scratch_types=[
            pltpu.VMEM((2, PEER_SEL), jnp.int32), pltpu.VMEM((2, PEER_SEL), F32),
            pltpu.VMEM((2, PEER_WORDS), jnp.int32),
            pltpu.VMEM((PEER_RING, PEER_CH, PEER_WORDS), jnp.int32),
            pltpu.VMEM((PEER_RING, PEER_CH, PEER_WORDS), jnp.int32),
            pltpu.VMEM((PEER_CH, SC_LANES), F32), pltpu.VMEM((PEER_SEL,), F32),
            pltpu.VMEM((2, D_MODEL), F32),
            pltpu.SemaphoreType.DMA((PEER_RING,)), pltpu.SemaphoreType.DMA((PEER_RING,)),
            pltpu.SemaphoreType.DMA((2,)), pltpu.SemaphoreType.DMA((2,)),
        ],
        compiler_params=pltpu.CompilerParams(needs_layout_passes=False),
        name="peer_sc",
    )(idx, gates, _pack_bf16_pairs(hq), u_words, v_words)


FINAL_TS = 256


def _final_kernel(x1_ref, pe_ref, p_ref, gp_ref, wg_ref, wp_ref, gf_ref, o_ref):
    x2 = x1_ref[...] + pe_ref[...]
    e = _dot(p_ref[...].astype(BF16), wp_ref[...])
    gate = jax.nn.sigmoid(_dot(_rms(x2, gp_ref[...]).astype(BF16), wg_ref[...]))
    o_ref[...] = _rms(x2 + gate * e, gf_ref[...])


def _final(x1, peer_out, p2d, g_ple, ple_w_gate, ple_w_proj, g_final):
    T = x1.shape[0]
    ts = min(FINAL_TS, T)
    tok = lambda d: pl.BlockSpec((ts, d), lambda i: (i, 0))
    full = lambda shape: pl.BlockSpec(shape, lambda i: (0,) * len(shape))
    return pl.pallas_call(
        _final_kernel,
        grid=(T // ts,),
        in_specs=[tok(D_MODEL), tok(D_MODEL), tok(D_PLE), full((1, D_MODEL)),
                  full((D_MODEL, D_MODEL)), full((D_PLE, D_MODEL)), full((1, D_MODEL))],
        out_specs=tok(D_MODEL),
        out_shape=jax.ShapeDtypeStruct((T, D_MODEL), F32),
        compiler_params=pltpu.CompilerParams(
            dimension_semantics=("parallel",), vmem_limit_bytes=VMEM_LIMIT),
        name="final",
    )(x1, peer_out, p2d, g_ple.reshape(1, D_MODEL), ple_w_gate.astype(BF16),
      ple_w_proj.astype(BF16), g_final.reshape(1, D_MODEL))


CHUNKS_PER_SEQ = 2


def kernel(x, p, positions, g_mix, w_in, ssm_log_dt, ssm_a_re, ssm_a_im, ssm_b_re, ssm_b_im,
           ssm_c_re, ssm_c_im, ssm_d, ssm_w_glu, w_proj_ssm, w_proj_att, w_out, g_ffn,
           peer_w_q, peer_keys1, peer_keys2, peer_u, peer_v, g_ple, ple_w_gate, ple_w_proj,
           g_final):
    B, S, _ = x.shape
    T = B * S
    assert w_in.shape[0] == 1, "the final rmsnorm is fused into the single layer's last stage"
    for i in range(1):
        u_sb, q, k, v, ga, gb = _in_proj(x, positions, g_mix[i], w_in[i])
        tables = _s5_tables(ssm_log_dt[i], ssm_a_re[i], ssm_a_im[i], ssm_b_re[i], ssm_b_im[i],
                            ssm_c_re[i], ssm_c_im[i])
        ys = _s5(u_sb, tables, ssm_d[i], ssm_w_glu[i], B, S)
        u_words = _pack_bf16_pairs(peer_u[i])
        v_words = _pack_bf16_pairs(peer_v[i])
        outs = []
        nq = S // MOBA_BLOCK // CHUNKS_PER_SEQ
        for b in range(B):
            for c in range(CHUNKS_PER_SEQ):
                tok = slice(c * nq * MOBA_BLOCK, (c + 1) * nq * MOBA_BLOCK)
                att = _moba(q, k, v, b, c * nq, nq)
                x1, hq, scores = _merge(
                    x[b, tok], ys[b, tok], att, ga[b, tok], gb[b, tok],
                    w_proj_ssm[i], w_proj_att[i], w_out[i], g_ffn[i], peer_w_q[i],
                    peer_keys1[i], peer_keys2[i])
                idx, gates = _topk(scores)
                peer_out = _peer(idx, hq, gates, u_words, v_words)
                outs.append(_final(x1, peer_out, p[i, b, tok], g_ple[i], ple_w_gate[i],
                                   ple_w_proj[i], g_final))
        x = jnp.concatenate(outs).reshape(B, S, D_MODEL)
    return x
```

```python
import functools
import math

import jax
import jax.numpy as jnp
from jax import lax
from jax.experimental import pallas as pl
from jax.experimental.pallas import tpu as pltpu
from jax.experimental.pallas import tpu_sc as plsc

F32 = jnp.float32
BF16 = jnp.bfloat16

D_MODEL = 1024
D_SSM = 512
SSM_GROUP = 16
SSM_GROUPS = 32
SSM_STATE = 64
D_STATE = SSM_GROUPS * SSM_STATE
N_HEADS = 8
HEAD_DIM = 64
D_ATT = 512
ROT_DIM = 16
ROPE_THETA = 500000.0
MOBA_BLOCK = 256
MOBA_TOPK = 3
PEER_HEADS = 8
PEER_KEYS = 128
PEER_QDIM = 256
PEER_HALF = 128
PEER_TOPK = 16
PEER_SEL = PEER_HEADS * PEER_TOPK
D_PLE = 256
EPS = 1e-6
NEG = -1e30
LANES = 128
SUBLANES = 8
VMEM_LIMIT = 48 * 1024 * 1024
HIGHEST = lax.Precision.HIGHEST


def _rms(x, g):
    return x * lax.rsqrt(jnp.mean(x * x, axis=-1, keepdims=True) + EPS) * g


def _dot(a, b):
    return jnp.dot(a, b, preferred_element_type=F32)


def _dot_nt(a, b, precision=None):
    return lax.dot_general(a, b, (((1,), (1,)), ((), ())), precision=precision,
                           preferred_element_type=F32)


IN_TS = 512


def _in_proj_kernel(x_ref, pos_ref, g_ref, w_ref, invf_ref,
                    u_ref, q_ref, k_ref, v_ref, ga_ref, gb_ref):
    h = _rms(x_ref[...], g_ref[...]).astype(BF16)

    def proj(lo, hi):
        return _dot(h, w_ref[:, lo:hi])

    u_ref[...] = proj(0, D_SSM).astype(BF16)
    ang = pos_ref[...].astype(F32) * invf_ref[...]
    cos = jnp.cos(ang)
    sin = jnp.sin(ang)
    lane = lax.broadcasted_iota(jnp.int32, (1, LANES), 1) % HEAD_DIM
    half = ROT_DIM // 2
    sin_hi = jnp.where((lane >= half) & (lane < ROT_DIM), sin, 0.0)
    sin_lo = jnp.where(lane < half, -sin, 0.0)
    reps = D_ATT // LANES
    cos4 = jnp.concatenate([cos] * reps, axis=1)
    sin_hi4 = jnp.concatenate([sin_hi] * reps, axis=1)
    sin_lo4 = jnp.concatenate([sin_lo] * reps, axis=1)

    def rope(t):
        return (t * cos4 + pltpu.roll(t, half, 1) * sin_hi4
                + pltpu.roll(t, D_ATT - half, 1) * sin_lo4)

    q = rope(proj(D_SSM, D_SSM + D_ATT))
    q_ref[...] = (q * (HEAD_DIM ** -0.5)).astype(BF16)
    k_ref[...] = rope(proj(D_SSM + D_ATT, D_SSM + 2 * D_ATT)).astype(BF16)
    v_ref[...] = proj(D_SSM + 2 * D_ATT, D_SSM + 3 * D_ATT).astype(BF16)
    o = D_SSM + 3 * D_ATT
    ga_ref[...] = jax.nn.sigmoid(proj(o, o + D_MODEL)).astype(BF16)
    gb_ref[...] = jax.nn.sigmoid(proj(o + D_MODEL, o + 2 * D_MODEL)).astype(BF16)


def _in_proj(x, positions, g_mix, w_in, t0, nt):
    B, S, _ = x.shape
    ts = min(IN_TS, nt)
    assert nt % ts == 0 and t0 % ts == 0
    i0 = t0 // ts
    inv_freq = ROPE_THETA ** (-jnp.arange(0, ROT_DIM, 2, dtype=F32) / ROT_DIM)
    lane = jnp.arange(LANES) % HEAD_DIM
    invf = jnp.where(lane < ROT_DIM, inv_freq[lane % (ROT_DIM // 2)], 0.0).reshape(1, LANES)
    d_in = w_in.shape[1]
    src = lambda d: pl.BlockSpec((None, ts, d), lambda b, i: (b, i0 + i, 0))
    tok = lambda d: pl.BlockSpec((None, ts, d), lambda b, i: (b, i, 0))
    full = lambda shape: pl.BlockSpec(shape, lambda b, i: (0,) * len(shape))
    return pl.pallas_call(
        _in_proj_kernel,
        grid=(B, nt // ts),
        in_specs=[src(D_MODEL), src(1), full((1, D_MODEL)), full((D_MODEL, d_in)), full((1, LANES))],
        out_specs=[pl.BlockSpec((ts, D_SSM), lambda b, i: (i, b)),
                   tok(D_ATT), tok(D_ATT), tok(D_ATT), tok(D_MODEL), tok(D_MODEL)],
        out_shape=[jax.ShapeDtypeStruct((nt, B * D_SSM), BF16),
                   jax.ShapeDtypeStruct((B, nt, D_ATT), BF16),
                   jax.ShapeDtypeStruct((B, nt, D_ATT), BF16),
                   jax.ShapeDtypeStruct((B, nt, D_ATT), BF16),
                   jax.ShapeDtypeStruct((B, nt, D_MODEL), BF16),
                   jax.ShapeDtypeStruct((B, nt, D_MODEL), BF16)],
        compiler_params=pltpu.CompilerParams(
            dimension_semantics=("parallel", "parallel"), vmem_limit_bytes=VMEM_LIMIT),
        name="in_proj",
    )(x, positions.reshape(B, S, 1), g_mix.reshape(1, D_MODEL), w_in, invf)


S5_TS = 128
S5_BATCH = 4
S5_COLS = 512


def _s5_kernel(u_ref, c0_ref, bre_ref, bim_ref, a1r_ref, a1i_ref, pr_ref, pi_ref,
               cre_ref, cim_ref, d_ref, wglu_ref, y_ref, c1_ref,
               xr, xi, cr, ci, ysc):
    rows = xr.shape[0]
    ts = rows // S5_BATCH

    @pl.when(pl.program_id(0) == 0)
    def _():
        cr[...] = c0_ref[0]
        ci[...] = c0_ref[1]

    u = u_ref[...]
    xr[...] = _dot(u, bre_ref[...])
    xi[...] = _dot(u, bim_ref[...])

    hi_rows = lax.broadcasted_iota(jnp.int32, (SUBLANES, S5_COLS), 0) >= S5_BATCH
    for cb in range(D_STATE // S5_COLS):
        sl = slice(cb * S5_COLS, (cb + 1) * S5_COLS)
        a_r, a_i = a1r_ref[:, sl], a1i_ref[:, sl]
        p_r, p_i = pr_ref[:, sl], pi_ref[:, sl]

        def body(t, carry):
            c_r, c_i = carry
            r0 = pl.multiple_of(t * SUBLANES, SUBLANES)
            x_r = xr[pl.ds(r0, SUBLANES), sl]
            x_i = xi[pl.ds(r0, SUBLANES), sl]
            s_r = pltpu.roll(x_r, S5_BATCH, 0)
            s_i = pltpu.roll(x_i, S5_BATCH, 0)
            h_r = x_r + (a_r * s_r - a_i * s_i) + (p_r * c_r - p_i * c_i)
            h_i = x_i + (a_r * s_i + a_i * s_r) + (p_r * c_i + p_i * c_r)
            xr[pl.ds(r0, SUBLANES), sl] = h_r
            xi[pl.ds(r0, SUBLANES), sl] = h_i
            n_r = jnp.where(hi_rows, h_r, pltpu.roll(h_r, S5_BATCH, 0))
            n_i = jnp.where(hi_rows, h_i, pltpu.roll(h_i, S5_BATCH, 0))
            return n_r, n_i

        c_r, c_i = lax.fori_loop(0, rows // SUBLANES, body, (cr[:, sl], ci[:, sl]), unroll=2)
        cr[:, sl] = c_r
        ci[:, sl] = c_i

    y = (_dot(xr[...].astype(BF16), cre_ref[...]) - _dot(xi[...].astype(BF16), cim_ref[...])
         + d_ref[...] * u.astype(F32))
    y = jax.nn.gelu(y)
    y = y * jax.nn.sigmoid(_dot(y.astype(BF16), wglu_ref[...]))
    for c in range(D_SSM // LANES):
        ysc[c] = y[:, c * LANES:(c + 1) * LANES]
    for b in range(S5_BATCH):
        for c in range(D_SSM // LANES):
            y_ref[b, :, c * LANES:(c + 1) * LANES] = (
                ysc[c, pl.ds(b, ts, stride=S5_BATCH), :].astype(BF16))

    @pl.when(pl.program_id(0) == pl.num_programs(0) - 1)
    def _():
        c1_ref[0] = cr[...]
        c1_ref[1] = ci[...]


def _s5_tables(log_dt, a_re, a_im, b_re, b_im, c_re, c_im):
    dt = jnp.exp(log_dt.astype(F32))[:, None]
    ar, ai = a_re.astype(F32), a_im.astype(F32)
    mag = jnp.exp(dt * ar)
    abar_re, abar_im = mag * jnp.cos(dt * ai), mag * jnp.sin(dt * ai)
    den = ar * ar + ai * ai
    nr, ni = abar_re - 1.0, abar_im
    f_re = (nr * ar + ni * ai) / den
    f_im = (ni * ar - nr * ai) / den
    br, bi = b_re.astype(F32), b_im.astype(F32)
    bb_re = f_re[..., None] * br - f_im[..., None] * bi
    bb_im = f_re[..., None] * bi + f_im[..., None] * br
    eye = jnp.eye(SSM_GROUPS, dtype=F32)

    def in_blockdiag(bb):
        return jnp.einsum('gnc,gh->gchn', bb, eye).reshape(D_SSM, D_STATE)

    def out_blockdiag(c):
        return jnp.einsum('gcn,gh->gnhc', c.astype(F32), eye).reshape(D_STATE, D_SSM)

    a_r = abar_re.reshape(1, D_STATE)
    a_i = abar_im.reshape(1, D_STATE)
    a2_r = a_r * a_r - a_i * a_i
    a2_i = 2.0 * a_r * a_i
    hi = (jnp.arange(SUBLANES) >= S5_BATCH)[:, None]
    a1r = jnp.where(hi, a_r, 0.0)
    a1i = jnp.where(hi, a_i, 0.0)
    p_r = jnp.where(hi, a2_r, a_r)
    p_i = jnp.where(hi, a2_i, a_i)
    return (in_blockdiag(bb_re).astype(BF16), in_blockdiag(bb_im).astype(BF16),
            a1r, a1i, p_r, p_i,
            out_blockdiag(c_re).astype(BF16), out_blockdiag(c_im).astype(BF16))


def _s5(u_sb, carry, tables, d_skip, w_glu, B):
    assert B == S5_BATCH
    nt = u_sb.shape[0]
    ts = min(S5_TS, nt)
    rows = ts * B
    bre, bim, a1r, a1i, p_r, p_i, cre, cim = tables
    full = lambda shape: pl.BlockSpec(shape, lambda i: (0,) * len(shape))
    return pl.pallas_call(
        _s5_kernel,
        grid=(nt // ts,),
        in_specs=[pl.BlockSpec((rows, D_SSM), lambda i: (i, 0)),
                  full((2, SUBLANES, D_STATE)),
                  full((D_SSM, D_STATE)), full((D_SSM, D_STATE)),
                  full((SUBLANES, D_STATE)), full((SUBLANES, D_STATE)),
                  full((SUBLANES, D_STATE)), full((SUBLANES, D_STATE)),
                  full((D_STATE, D_SSM)), full((D_STATE, D_SSM)),
                  full((1, D_SSM)), full((D_SSM, D_SSM))],
        out_specs=[pl.BlockSpec((B, ts, D_SSM), lambda i: (0, i, 0)),
                   full((2, SUBLANES, D_STATE))],
        out_shape=[jax.ShapeDtypeStruct((B, nt, D_SSM), BF16),
                   jax.ShapeDtypeStruct((2, SUBLANES, D_STATE), F32)],
        scratch_shapes=[pltpu.VMEM((rows, D_STATE), F32), pltpu.VMEM((rows, D_STATE), F32),
                        pltpu.VMEM((SUBLANES, D_STATE), F32), pltpu.VMEM((SUBLANES, D_STATE), F32),
                        pltpu.VMEM((D_SSM // LANES, rows, LANES), F32)],
        compiler_params=pltpu.CompilerParams(
            dimension_semantics=("arbitrary",), vmem_limit_bytes=VMEM_LIMIT),
        name="s5",
    )(u_sb.reshape(nt * B, D_SSM), carry, bre, bim, a1r, a1i, p_r, p_i, cre, cim, d_skip, w_glu)


MOBA_PAIR = 2 * MOBA_BLOCK


def _moba_kernel(q0, q_ref, k_ref, v_ref, o_ref, kmean, kaug_a, kaug_b, vaug_a, vaug_b, m_s, acc_s,
                 s_buf):
    qi = pl.program_id(2) + q0
    nb = k_ref.shape[0] // MOBA_BLOCK
    nbp = kmean.shape[0]
    lane = lax.broadcasted_iota(jnp.int32, (1, LANES), 1)
    head_a = lane < HEAD_DIM

    @pl.when(pl.program_id(2) == 0)
    def _():
        kmean[...] = jnp.zeros_like(kmean)
        for j in range(nb):
            rows = pl.ds(j * MOBA_BLOCK, MOBA_BLOCK)
            kj = k_ref[rows, :].astype(F32)
            vj = v_ref[rows, :].astype(F32)
            kmean[j:j + 1, :] = jnp.sum(kj, axis=0, keepdims=True) * (1.0 / MOBA_BLOCK)
            kaug_a[rows, :] = jnp.where(head_a, kj, jnp.where(lane - HEAD_DIM == j, 1.0, 0.0)).astype(BF16)
            kaug_b[rows, :] = jnp.where(head_a, jnp.where(lane == j, 1.0, 0.0), kj).astype(BF16)
            vaug_a[rows, :] = jnp.where(head_a, vj, 1.0).astype(BF16)
            vaug_b[rows, :] = jnp.where(head_a, 1.0, vj).astype(BF16)

    qf = q_ref[...].astype(F32)
    blk_row = lax.broadcasted_iota(jnp.int32, (nbp, MOBA_BLOCK), 0)
    q_augs = []
    for is_a in (True, False):
        mine = head_a if is_a else jnp.logical_not(head_a)
        q_own = jnp.where(mine, qf, 0.0)
        g = _dot_nt(kmean[...], q_own, precision=HIGHEST)
        g = jnp.where(blk_row < qi, g, NEG)
        sel = jnp.zeros(g.shape, F32)
        for _ in range(MOBA_TOPK):
            m = jnp.max(g, axis=0, keepdims=True)
            idx = jnp.min(jnp.where(g == m, blk_row, nbp), axis=0, keepdims=True)
            hit = blk_row == idx
            sel = jnp.where(hit, jnp.where(idx < qi, 1.0, 0.0), sel)
            g = jnp.where(hit, -jnp.inf, g)
        bias_t = jnp.where(sel > 0.0, 0.0, jnp.where(blk_row == qi, 0.0, NEG))
        bias_t = jnp.concatenate([bias_t, jnp.full((LANES - nbp, MOBA_BLOCK), NEG, F32)], axis=0)
        bias = jnp.transpose(bias_t)
        if is_a:
            bias = pltpu.roll(bias, HEAD_DIM, 1)
        q_augs.append(jnp.where(mine, qf, bias).astype(BF16))

    m_s[...] = jnp.full(m_s.shape, -jnp.inf, F32)
    acc_s[...] = jnp.zeros_like(acc_s)
    qpos = qi * MOBA_BLOCK + lax.broadcasted_iota(jnp.int32, (MOBA_BLOCK, MOBA_PAIR), 0)
    col = lax.broadcasted_iota(jnp.int32, (MOBA_BLOCK, MOBA_PAIR), 1)

    def kv_rows(jj):
        return pl.ds(pl.multiple_of(jj * MOBA_PAIR, MOBA_PAIR), MOBA_PAIR)

    def scores(jj, slot):
        for hd, kaug in enumerate((kaug_a, kaug_b)):
            s_buf[slot, hd] = _dot_nt(q_augs[hd], kaug[kv_rows(jj), :])

    def softmax_pv(jj, slot, causal):
        for hd, vaug in enumerate((vaug_a, vaug_b)):
            s = s_buf[slot, hd]
            if causal:
                s = jnp.where(jj * MOBA_PAIR + col <= qpos, s, NEG)
            m_old = m_s[hd]
            m_new = jnp.maximum(m_old, jnp.max(s, axis=-1, keepdims=True))
            alpha = jnp.exp(m_old - m_new)
            p = jnp.exp(s - m_new)
            m_s[hd] = m_new
            acc_s[hd] = alpha * acc_s[hd] + _dot(p.astype(BF16), vaug[kv_rows(jj), :])

    last = qi // 2
    scores(0, 0)

    def body(k, _):
        scores(2 * k + 1, 1)
        softmax_pv(2 * k, 0, False)
        scores(2 * k + 2, 0)
        softmax_pv(2 * k + 1, 1, False)
        return 0

    lax.fori_loop(0, last // 2, body, 0)

    @pl.when(last % 2 == 0)
    def _():
        softmax_pv(last, 0, True)

    @pl.when(last % 2 == 1)
    def _():
        scores(last, 1)
        softmax_pv(last - 1, 0, False)
        softmax_pv(last, 1, True)
    acc_a, acc_b = acc_s[0], acc_s[1]
    o_ref[...] = jnp.where(head_a, acc_a / pltpu.roll(acc_a, HEAD_DIM, 1),
                           acc_b / pltpu.roll(acc_b, HEAD_DIM, 1)).astype(BF16)


def _moba(q, k, v, q0):
    B = q.shape[0]
    nq = q.shape[1] // MOBA_BLOCK
    skv = (q0 + nq) * MOBA_BLOCK
    nb = skv // MOBA_BLOCK
    assert nb <= HEAD_DIM and nb % 2 == 0 and skv <= k.shape[1]
    nbp = -(-nb // SUBLANES) * SUBLANES
    blk = pl.BlockSpec((None, MOBA_BLOCK, LANES), lambda b, h, i: (b, i, h))
    seq = pl.BlockSpec((None, skv, LANES), lambda b, h, i: (b, 0, h))
    return pl.pallas_call(
        functools.partial(_moba_kernel, q0),
        grid=(B, D_ATT // LANES, nq),
        in_specs=[blk, seq, seq],
        out_specs=blk,
        out_shape=jax.ShapeDtypeStruct(q.shape, BF16),
        scratch_shapes=[pltpu.VMEM((nbp, LANES), F32),
                        pltpu.VMEM((skv, LANES), BF16), pltpu.VMEM((skv, LANES), BF16),
                        pltpu.VMEM((skv, LANES), BF16), pltpu.VMEM((skv, LANES), BF16),
                        pltpu.VMEM((2, MOBA_BLOCK, 1), F32),
                        pltpu.VMEM((2, MOBA_BLOCK, LANES), F32),
                        pltpu.VMEM((2, 2, MOBA_BLOCK, MOBA_PAIR), F32)],
        compiler_params=pltpu.CompilerParams(
            dimension_semantics=("parallel", "parallel", "arbitrary"), vmem_limit_bytes=VMEM_LIMIT),
        name="moba",
    )(q, k, v)


MERGE_TS = 256


def _merge_kernel(x_ref, ys_ref, at_ref, ga_ref, gb_ref, wa_ref, wb_ref, wo_ref, g_ref,
                  wq_ref, k1_ref, k2_ref, x1_ref, hq_ref, sc_ref):
    ya = _dot(ys_ref[...], wa_ref[...])
    yb = _dot(at_ref[...], wb_ref[...])
    merged = ga_ref[...].astype(F32) * ya + gb_ref[...].astype(F32) * yb
    x1 = x_ref[...] + _dot(merged.astype(BF16), wo_ref[...])
    x1_ref[...] = x1
    hq = _rms(x1, g_ref[...])
    hq_ref[...] = hq
    qp = _dot(hq.astype(BF16), wq_ref[...])
    for h in range(PEER_HEADS):
        o = h * PEER_QDIM
        sc_ref[2 * h] = _dot_nt(k1_ref[h], qp[:, o:o + PEER_HALF], precision=HIGHEST)
        sc_ref[2 * h + 1] = _dot_nt(k2_ref[h], qp[:, o + PEER_HALF:o + PEER_QDIM], precision=HIGHEST)


def _merge(x, ys, att, ga, gb, t0, w_proj_ssm, w_proj_att, w_out, g_ffn, peer_w_q, keys1, keys2):
    B, nt = ys.shape[0], ys.shape[1]
    ts = min(MERGE_TS, nt)
    nblk = nt // ts
    i0 = t0 // ts
    tok = lambda d: pl.BlockSpec((None, ts, d), lambda b, i: (b, i, 0))
    row = lambda d: pl.BlockSpec((ts, d), lambda b, i: (b * nblk + i, 0))
    full = lambda shape: pl.BlockSpec(shape, lambda b, i: (0,) * len(shape))
    qd = PEER_HEADS * PEER_QDIM
    return pl.pallas_call(
        _merge_kernel,
        grid=(B, nblk),
        in_specs=[pl.BlockSpec((None, ts, D_MODEL), lambda b, i: (b, i0 + i, 0)),
                  tok(D_SSM), tok(D_ATT), tok(D_MODEL), tok(D_MODEL),
                  full((D_SSM, D_MODEL)), full((D_ATT, D_MODEL)), full((D_MODEL, D_MODEL)),
                  full((1, D_MODEL)), full((D_MODEL, qd)),
                  full((PEER_HEADS, PEER_KEYS, PEER_HALF)), full((PEER_HEADS, PEER_KEYS, PEER_HALF))],
        out_specs=[row(D_MODEL), row(D_MODEL),
                   pl.BlockSpec((2 * PEER_HEADS, PEER_KEYS, ts), lambda b, i: (0, 0, b * nblk + i))],
        out_shape=[jax.ShapeDtypeStruct((B * nt, D_MODEL), F32),
                   jax.ShapeDtypeStruct((B * nt, D_MODEL), F32),
                   jax.ShapeDtypeStruct((2 * PEER_HEADS, PEER_KEYS, B * nt), F32)],
        compiler_params=pltpu.CompilerParams(
            dimension_semantics=("parallel", "parallel"), vmem_limit_bytes=VMEM_LIMIT),
        name="merge",
    )(x, ys, att, ga, gb, w_proj_ssm, w_proj_att, w_out, g_ffn, peer_w_q, keys1, keys2)


TOPK_TS = 256


def _top_rows(s, row, k):
    vals, idxs = [], []
    for _ in range(k):
        m = jnp.max(s, axis=0, keepdims=True)
        idx = jnp.min(jnp.where(s == m, row, s.shape[0]), axis=0, keepdims=True)
        vals.append(m)
        idxs.append(idx)
        s = jnp.where(row == idx, -jnp.inf, s)
    return vals, idxs


def _stack_rows(rows, row16):
    acc = jnp.zeros(row16.shape, rows[0].dtype)
    for r, v in enumerate(rows):
        acc = jnp.where(row16 == r, v, acc)
    return acc


def _topk_kernel(sc_ref, idx_ref, gate_ref):
    ts = sc_ref.shape[-1]
    row = lax.broadcasted_iota(jnp.int32, (PEER_KEYS, ts), 0)
    row16 = lax.broadcasted_iota(jnp.int32, (PEER_TOPK, ts), 0)
    row8 = lax.broadcasted_iota(jnp.int32, (SUBLANES, ts), 0)
    counts = [PEER_TOPK // (i + 1) for i in range(PEER_TOPK)]
    heights = [PEER_TOPK if c > SUBLANES else SUBLANES for c in counts]
    n_cand = sum(heights)
    rowc = lax.broadcasted_iota(jnp.int32, (n_cand, ts), 0)
    gate_rows, eid_rows = [], []
    for h in range(PEER_HEADS):
        v1, i1 = _top_rows(sc_ref[2 * h], row, PEER_TOPK)
        v2, i2 = _top_rows(sc_ref[2 * h + 1], row, PEER_TOPK)
        v2s = _stack_rows(v2, row16)
        i2s = _stack_rows(i2, row16).astype(F32)
        cand, eid = [], []
        for i in range(PEER_TOPK):
            n = heights[i]
            cand.append(jnp.where((row16 if n == PEER_TOPK else row8) < counts[i],
                                  v1[i] + v2s[:n], -jnp.inf))
            eid.append(i1[i].astype(F32) * PEER_KEYS + i2s[:n])
        cand = jnp.concatenate(cand, axis=0)
        eid = jnp.concatenate(eid, axis=0)
        tops, picks = [], []
        for _ in range(PEER_TOPK):
            m = jnp.max(cand, axis=0, keepdims=True)
            pos = jnp.min(jnp.where(cand == m, rowc, n_cand), axis=0, keepdims=True)
            hit = rowc == pos
            picks.append(jnp.max(jnp.where(hit, eid, -1.0), axis=0, keepdims=True))
            tops.append(m)
            cand = jnp.where(hit, -jnp.inf, cand)
        top = _stack_rows(tops, row16)
        p = jnp.exp(top - jnp.max(top, axis=0, keepdims=True))
        gate_rows.append(p / jnp.sum(p, axis=0, keepdims=True))
        eid_rows.append(_stack_rows(picks, row16))
    gate_ref[...] = jnp.transpose(jnp.concatenate(gate_rows, axis=0))
    idx_ref[...] = jnp.transpose(jnp.concatenate(eid_rows, axis=0)).astype(jnp.int32)


def _topk(scores):
    T = scores.shape[-1]
    ts = min(TOPK_TS, T)
    return pl.pallas_call(
        _topk_kernel,
        grid=(T // ts,),
        in_specs=[pl.BlockSpec((2 * PEER_HEADS, PEER_KEYS, ts), lambda i: (0, 0, i))],
        out_specs=[pl.BlockSpec((ts, PEER_SEL), lambda i: (i, 0)),
                   pl.BlockSpec((ts, PEER_SEL), lambda i: (i, 0))],
        out_shape=[jax.ShapeDtypeStruct((T, PEER_SEL), jnp.int32),
                   jax.ShapeDtypeStruct((T, PEER_SEL), F32)],
        compiler_params=pltpu.CompilerParams(
            dimension_semantics=("parallel",), vmem_limit_bytes=VMEM_LIMIT),
        name="topk",
    )(scores)


SC_CORES = 2
SC_SUBCORES = 16
SC_LANES = 16
SC_WORKERS = SC_CORES * SC_SUBCORES
PEER_CH = SC_LANES
PEER_NCH = PEER_SEL // PEER_CH
PEER_WORDS = D_MODEL // 2
PEER_NWG = PEER_WORDS // SC_LANES
PEER_RING = 4
PEER_QUAD = 4
HI_MASK = -65536
GELU_C = 0.7978845608028654


def _gelu_tanh_via_exp(x):
    z = GELU_C * (x + 0.044715 * (x * x * x))
    t = 1.0 - 2.0 / (jnp.exp(2.0 * z) + 1.0)
    return 0.5 * x * (1.0 + t)


def _unpack_pair(w):
    lo = plsc.bitcast(lax.shift_left(w, 16), F32)
    hi = plsc.bitcast(lax.bitwise_and(w, HI_MASK), F32)
    return lo, hi


def _peer_sc_body(idx_hbm, gate_hbm, h_hbm, u_hbm, v_hbm, o_hbm,
                  idx_v, gate_v, h_v, ubuf, vbuf, pbuf, w_v, out_v, usem, vsem, msem, osem):
    n_tok = o_hbm.shape[0] // SC_WORKERS
    base = (lax.axis_index("s") * SC_CORES + lax.axis_index("c")) * n_tok
    lane = lax.iota(jnp.int32, SC_LANES)
    zero_rows = jnp.zeros((SC_LANES,), jnp.int32)

    def meta_copies(tok, s):
        return (pltpu.make_async_copy(idx_hbm.at[tok], idx_v.at[s], msem.at[s]),
                pltpu.make_async_copy(gate_hbm.at[tok], gate_v.at[s], msem.at[s]),
                pltpu.make_async_copy(h_hbm.at[tok], h_v.at[s], msem.at[s]))

    def gather(tab, buf, sem, slot, rows):
        return pltpu.make_async_copy(tab.at[rows], buf.at[slot], sem.at[slot])

    def start_ahead(tab, buf, sem, s, c):
        ahead = c + PEER_RING
        src = jnp.where(ahead < PEER_NCH, s, 1 - s)
        ch = ahead % PEER_NCH
        rows = idx_v[src, pl.ds(pl.multiple_of(ch * PEER_CH, PEER_CH), PEER_CH)]
        gather(tab, buf, sem, c % PEER_RING, rows).start()

    def token(t, carry):
        s = t % 2
        tok = base + t
        nxt = base + jnp.minimum(t + 1, n_tok - 1)
        for cp in meta_copies(nxt, 1 - s):
            cp.start()

        def u_chunk(c, carry):
            slot = c % PEER_RING
            gather(u_hbm, ubuf, usem, slot, zero_rows).wait()

            def dot_step(q, accs):
                cols = [pl.ds(pl.multiple_of((q * PEER_QUAD + j) * SC_LANES, SC_LANES), SC_LANES)
                        for j in range(PEER_QUAD)]
                hs = [plsc.bitcast(h_v[s, col], BF16) for col in cols]
                out = []
                for r in range(PEER_CH):
                    p = plsc.bitcast(ubuf[slot, r, cols[0]], BF16) * hs[0]
                    for j in range(1, PEER_QUAD):
                        p = p + plsc.bitcast(ubuf[slot, r, cols[j]], BF16) * hs[j]
                    lo, hi = _unpack_pair(plsc.bitcast(p, jnp.int32))
                    out.append(accs[r] + lo + hi)
                return tuple(out)

            accs = lax.fori_loop(0, PEER_NWG // PEER_QUAD, dot_step,
                                 tuple(jnp.zeros((SC_LANES,), F32) for _ in range(PEER_CH)))

            @pl.when(c == PEER_NCH - PEER_RING)
            def _():
                for cp in meta_copies(nxt, 1 - s):
                    cp.wait()

            start_ahead(u_hbm, ubuf, usem, s, c)
            for r in range(PEER_CH):
                pbuf[r, :] = accs[r]
            tot = jnp.zeros((SC_LANES,), F32)
            for j in range(SC_LANES):
                tot = tot + plsc.load_gather(pbuf, [lane, jnp.full((SC_LANES,), j, jnp.int32)])
            rows = pl.ds(pl.multiple_of(c * PEER_CH, PEER_CH), PEER_CH)
            w_v[rows] = gate_v[s, rows] * _gelu_tanh_via_exp(tot)
            return carry

        lax.fori_loop(0, PEER_NCH, u_chunk, 0)

        @pl.when(t >= 2)
        def _():
            pltpu.make_async_copy(out_v.at[s], o_hbm.at[tok], osem.at[s]).wait()

        def v_chunk(c, carry):
            slot = c % PEER_RING
            gather(v_hbm, vbuf, vsem, slot, zero_rows).wait()
            ws = []
            for r in range(PEER_CH):
                w = plsc.load_gather(w_v, [jnp.full((SC_LANES,), r, jnp.int32) + c * PEER_CH])
                ws.append(plsc.pack(w, w, format=plsc.PackFormat.INTERLEAVED,
                                    preferred_element_type=BF16))
            first = c == 0

            @plsc.parallel_loop(0, PEER_NWG, unroll=2)
            def acc_step(g):
                col = pl.ds(pl.multiple_of(g * SC_LANES, SC_LANES), SC_LANES)
                col_hi = pl.ds(pl.multiple_of(PEER_WORDS + g * SC_LANES, SC_LANES), SC_LANES)
                o_lo = jnp.where(first, 0.0, out_v[s, col])
                o_hi = jnp.where(first, 0.0, out_v[s, col_hi])
                for r0 in range(0, PEER_CH, PEER_QUAD):
                    p = plsc.bitcast(vbuf[slot, r0, col], BF16) * ws[r0]
                    for r in range(r0 + 1, r0 + PEER_QUAD):
                        p = p + plsc.bitcast(vbuf[slot, r, col], BF16) * ws[r]
                    lo, hi = _unpack_pair(plsc.bitcast(p, jnp.int32))
                    o_lo = o_lo + lo
                    o_hi = o_hi + hi
                out_v[s, col] = o_lo
                out_v[s, col_hi] = o_hi

            start_ahead(v_hbm, vbuf, vsem, s, c)
            return carry

        lax.fori_loop(0, PEER_NCH, v_chunk, 0)
        pltpu.make_async_copy(out_v.at[s], o_hbm.at[tok], osem.at[s]).start()
        return carry

    for cp in meta_copies(base, 0):
        cp.start()
    for cp in meta_copies(base, 0):
        cp.wait()
    for c in range(PEER_RING):
        rows = idx_v[0, pl.ds(c * PEER_CH, PEER_CH)]
        gather(u_hbm, ubuf, usem, c, rows).start()
        gather(v_hbm, vbuf, vsem, c, rows).start()
    lax.fori_loop(0, n_tok, token, 0)
    for c in range(PEER_RING):
        gather(u_hbm, ubuf, usem, c, zero_rows).wait()
        gather(v_hbm, vbuf, vsem, c, zero_rows).wait()
    for s in range(2):
        pltpu.make_async_copy(out_v.at[s], o_hbm.at[base], osem.at[s]).wait()


def _pack_bf16_pairs(tab):
    b = lax.bitcast_convert_type(tab.astype(BF16), jnp.uint16).astype(jnp.uint32)
    half = tab.shape[1] // 2
    return lax.bitcast_convert_type(b[:, :half] | (b[:, half:] << 16), jnp.int32)


def _peer(idx, hq, gates, u_words, v_words):
    T = hq.shape[0]
    assert T % (2 * SC_WORKERS) == 0
    mesh = plsc.VectorSubcoreMesh(core_axis_name="c", subcore_axis_name="s",
                                  num_cores=SC_CORES, num_subcores=SC_SUBCORES)
    return pl.kernel(
        _peer_sc_body,
        out_type=jax.ShapeDtypeStruct((T, D_MODEL), F32),
        mesh=mesh,
        scratch_types=[
            pltpu.VMEM((2, PEER_SEL), jnp.int32), pltpu.VMEM((2, PEER_SEL), F32),
            pltpu.VMEM((2, PEER_WORDS), jnp.int32),
            pltpu.VMEM((PEER_RING, PEER_CH, PEER_WORDS), jnp.int32),
            pltpu.VMEM((PEER_RING, PEER_CH, PEER_WORDS), jnp.int32),
            pltpu.VMEM((PEER_CH, SC_LANES), F32), pltpu.VMEM((PEER_SEL,), F32),
            pltpu.VMEM((2, D_MODEL), F32),
            pltpu.SemaphoreType.DMA((PEER_RING,)), pltpu.SemaphoreType.DMA((PEER_RING,)),
            pltpu.SemaphoreType.DMA((2,)), pltpu.SemaphoreType.DMA((2,)),
        ],
        compiler_params=pltpu.CompilerParams(needs_layout_passes=False),
        name="peer_sc",
    )(idx, gates, _pack_bf16_pairs(hq), u_words, v_words)


FINAL_TS = 256


def _final_kernel(x1_ref, pe_ref, p_ref, gp_ref, wg_ref, wp_ref, gf_ref, o_ref):
    x2 = x1_ref[...] + pe_ref[...]
    e = _dot(p_ref[...].astype(BF16), wp_ref[...])
    gate = jax.nn.sigmoid(_dot(_rms(x2, gp_ref[...]).astype(BF16), wg_ref[...]))
    o_ref[...] = _rms(x2 + gate * e, gf_ref[...])


def _final(x1, peer_out, p, t0, nt, g_ple, ple_w_gate, ple_w_proj, g_final):
    B = p.shape[0]
    ts = min(FINAL_TS, nt)
    nblk = nt // ts
    i0 = t0 // ts
    row = lambda d: pl.BlockSpec((ts, d), lambda b, i: (b * nblk + i, 0))
    full = lambda shape: pl.BlockSpec(shape, lambda b, i: (0,) * len(shape))
    return pl.pallas_call(
        _final_kernel,
        grid=(B, nblk),
        in_specs=[row(D_MODEL), row(D_MODEL),
                  pl.BlockSpec((None, ts, D_PLE), lambda b, i: (b, i0 + i, 0)),
                  full((1, D_MODEL)), full((D_MODEL, D_MODEL)), full((D_PLE, D_MODEL)),
                  full((1, D_MODEL))],
        out_specs=pl.BlockSpec((None, ts, D_MODEL), lambda b, i: (b, i, 0)),
        out_shape=jax.ShapeDtypeStruct((B, nt, D_MODEL), F32),
        compiler_params=pltpu.CompilerParams(
            dimension_semantics=("parallel", "parallel"), vmem_limit_bytes=VMEM_LIMIT),
        name="final",
    )(x1, peer_out, p, g_ple, ple_w_gate, ple_w_proj, g_final)


CHUNK_STEPS = (512, 512, 1024, 1024, 1024, 1024, 1024, 1024, 1024)


def kernel(x, p, positions, g_mix, w_in, ssm_log_dt, ssm_a_re, ssm_a_im, ssm_b_re, ssm_b_im,
           ssm_c_re, ssm_c_im, ssm_d, ssm_w_glu, w_proj_ssm, w_proj_att, w_out, g_ffn,
           peer_w_q, peer_keys1, peer_keys2, peer_u, peer_v, g_ple, ple_w_gate, ple_w_proj,
           g_final):
    B, S, _ = x.shape
    assert w_in.shape[0] == 1, "the final rmsnorm is fused into the single layer's last stage"
    steps = CHUNK_STEPS if sum(CHUNK_STEPS) == S else (S,)
    i = 0
    tables = _s5_tables(ssm_log_dt[i], ssm_a_re[i], ssm_a_im[i], ssm_b_re[i], ssm_b_im[i],
                        ssm_c_re[i], ssm_c_im[i])
    w_in_b, w_glu_b = w_in[i].astype(BF16), ssm_w_glu[i].astype(BF16)
    d_skip = ssm_d[i].reshape(1, D_SSM).astype(F32)
    merge_w = (w_proj_ssm[i].astype(BF16), w_proj_att[i].astype(BF16), w_out[i].astype(BF16),
               g_ffn[i].reshape(1, D_MODEL), peer_w_q[i].astype(BF16), peer_keys1[i], peer_keys2[i])
    final_w = (g_ple[i].reshape(1, D_MODEL), ple_w_gate[i].astype(BF16),
               ple_w_proj[i].astype(BF16), g_final.reshape(1, D_MODEL))
    u_words = _pack_bf16_pairs(peer_u[i])
    v_words = _pack_bf16_pairs(peer_v[i])
    k_all = jnp.zeros((B, S, D_ATT), BF16)
    v_all = jnp.zeros((B, S, D_ATT), BF16)
    carry = jnp.zeros((2, SUBLANES, D_STATE), F32)
    outs = []
    t0 = 0
    for nt in steps:
        u_sb, q, k, v, ga, gb = _in_proj(x, positions, g_mix[i], w_in_b, t0, nt)
        k_all = lax.dynamic_update_slice(k_all, k, (0, t0, 0))
        v_all = lax.dynamic_update_slice(v_all, v, (0, t0, 0))
        ys, carry = _s5(u_sb, carry, tables, d_skip, w_glu_b, B)
        att = _moba(q, k_all, v_all, t0 // MOBA_BLOCK)
        x1, hq, scores = _merge(x, ys, att, ga, gb, t0, *merge_w)
        idx, gates = _topk(scores)
        peer_out = _peer(idx, hq, gates, u_words, v_words)
        outs.append(_final(x1, peer_out, p[i], t0, nt, *final_w))
        t0 += nt
    return jnp.concatenate(outs, axis=1)
```

```python
import functools
import math

import jax
import jax.numpy as jnp
from jax import lax
from jax.experimental import pallas as pl
from jax.experimental.pallas import tpu as pltpu
from jax.experimental.pallas import tpu_sc as plsc

F32 = jnp.float32
BF16 = jnp.bfloat16

D_MODEL = 1024
D_SSM = 512
SSM_GROUP = 16
SSM_GROUPS = 32
SSM_STATE = 64
D_STATE = SSM_GROUPS * SSM_STATE
N_HEADS = 8
HEAD_DIM = 64
D_ATT = 512
ROT_DIM = 16
ROPE_THETA = 500000.0
MOBA_BLOCK = 256
MOBA_TOPK = 3
PEER_HEADS = 8
PEER_KEYS = 128
PEER_QDIM = 256
PEER_HALF = 128
PEER_TOPK = 16
PEER_SEL = PEER_HEADS * PEER_TOPK
D_PLE = 256
EPS = 1e-6
NEG = -1e30
LANES = 128
SUBLANES = 8
VMEM_LIMIT = 48 * 1024 * 1024
HIGHEST = lax.Precision.HIGHEST


def _rms(x, g):
    return x * lax.rsqrt(jnp.mean(x * x, axis=-1, keepdims=True) + EPS) * g


def _dot(a, b):
    return jnp.dot(a, b, preferred_element_type=F32)


def _dot_nt(a, b, precision=None):
    return lax.dot_general(a, b, (((1,), (1,)), ((), ())), precision=precision,
                           preferred_element_type=F32)


IN_TS = 512


def _in_proj_kernel(x_ref, pos_ref, g_ref, w_ref, invf_ref, after_ref,
                    u_ref, q_ref, k_ref, v_ref, ga_ref, gb_ref):
    del after_ref
    h = _rms(x_ref[...], g_ref[...]).astype(BF16)

    def proj(lo, hi):
        return _dot(h, w_ref[:, lo:hi])

    u_ref[...] = proj(0, D_SSM).astype(BF16)
    ang = pos_ref[...].astype(F32) * invf_ref[...]
    cos = jnp.cos(ang)
    sin = jnp.sin(ang)
    lane = lax.broadcasted_iota(jnp.int32, (1, LANES), 1) % HEAD_DIM
    half = ROT_DIM // 2
    sin_hi = jnp.where((lane >= half) & (lane < ROT_DIM), sin, 0.0)
    sin_lo = jnp.where(lane < half, -sin, 0.0)
    reps = D_ATT // LANES
    cos4 = jnp.concatenate([cos] * reps, axis=1)
    sin_hi4 = jnp.concatenate([sin_hi] * reps, axis=1)
    sin_lo4 = jnp.concatenate([sin_lo] * reps, axis=1)

    def rope(t):
        return (t * cos4 + pltpu.roll(t, half, 1) * sin_hi4
                + pltpu.roll(t, D_ATT - half, 1) * sin_lo4)

    q = rope(proj(D_SSM, D_SSM + D_ATT))
    q_ref[...] = (q * (HEAD_DIM ** -0.5)).astype(BF16)
    k_ref[...] = rope(proj(D_SSM + D_ATT, D_SSM + 2 * D_ATT)).astype(BF16)
    v_ref[...] = proj(D_SSM + 2 * D_ATT, D_SSM + 3 * D_ATT).astype(BF16)
    o = D_SSM + 3 * D_ATT
    ga_ref[...] = jax.nn.sigmoid(proj(o, o + D_MODEL)).astype(BF16)
    gb_ref[...] = jax.nn.sigmoid(proj(o + D_MODEL, o + 2 * D_MODEL)).astype(BF16)


def _in_proj(x, positions, g_mix, w_in, t0, nt, after):
    B, S, _ = x.shape
    ts = min(IN_TS, nt)
    assert nt % ts == 0 and t0 % ts == 0
    i0 = t0 // ts
    inv_freq = ROPE_THETA ** (-jnp.arange(0, ROT_DIM, 2, dtype=F32) / ROT_DIM)
    lane = jnp.arange(LANES) % HEAD_DIM
    invf = jnp.where(lane < ROT_DIM, inv_freq[lane % (ROT_DIM // 2)], 0.0).reshape(1, LANES)
    d_in = w_in.shape[1]
    src = lambda d: pl.BlockSpec((None, ts, d), lambda b, i: (b, i0 + i, 0))
    tok = lambda d: pl.BlockSpec((None, ts, d), lambda b, i: (b, i, 0))
    full = lambda shape: pl.BlockSpec(shape, lambda b, i: (0,) * len(shape))
    return pl.pallas_call(
        _in_proj_kernel,
        grid=(B, nt // ts),
        in_specs=[src(D_MODEL), src(1), full((1, D_MODEL)), full((D_MODEL, d_in)), full((1, LANES)),
                  pl.BlockSpec(memory_space=pl.ANY)],
        out_specs=[pl.BlockSpec((ts, D_SSM), lambda b, i: (i, b)),
                   tok(D_ATT), tok(D_ATT), tok(D_ATT), tok(D_MODEL), tok(D_MODEL)],
        out_shape=[jax.ShapeDtypeStruct((nt, B * D_SSM), BF16),
                   jax.ShapeDtypeStruct((B, nt, D_ATT), BF16),
                   jax.ShapeDtypeStruct((B, nt, D_ATT), BF16),
                   jax.ShapeDtypeStruct((B, nt, D_ATT), BF16),
                   jax.ShapeDtypeStruct((B, nt, D_MODEL), BF16),
                   jax.ShapeDtypeStruct((B, nt, D_MODEL), BF16)],
        compiler_params=pltpu.CompilerParams(
            dimension_semantics=("parallel", "parallel"), vmem_limit_bytes=VMEM_LIMIT),
        name="in_proj",
    )(x, positions.reshape(B, S, 1), g_mix.reshape(1, D_MODEL), w_in, invf, after)


S5_TS = 128
S5_BATCH = 4
S5_COLS = 512


def _s5_kernel(u_ref, c0_ref, bre_ref, bim_ref, a1r_ref, a1i_ref, pr_ref, pi_ref,
               cre_ref, cim_ref, d_ref, wglu_ref, y_ref, c1_ref,
               xr, xi, cr, ci, ysc):
    rows = xr.shape[0]
    ts = rows // S5_BATCH

    @pl.when(pl.program_id(0) == 0)
    def _():
        cr[...] = c0_ref[0]
        ci[...] = c0_ref[1]

    u = u_ref[...]
    xr[...] = _dot(u, bre_ref[...])
    xi[...] = _dot(u, bim_ref[...])

    hi_rows = lax.broadcasted_iota(jnp.int32, (SUBLANES, S5_COLS), 0) >= S5_BATCH
    for cb in range(D_STATE // S5_COLS):
        sl = slice(cb * S5_COLS, (cb + 1) * S5_COLS)
        a_r, a_i = a1r_ref[:, sl], a1i_ref[:, sl]
        p_r, p_i = pr_ref[:, sl], pi_ref[:, sl]

        def body(t, carry):
            c_r, c_i = carry
            r0 = pl.multiple_of(t * SUBLANES, SUBLANES)
            x_r = xr[pl.ds(r0, SUBLANES), sl]
            x_i = xi[pl.ds(r0, SUBLANES), sl]
            s_r = pltpu.roll(x_r, S5_BATCH, 0)
            s_i = pltpu.roll(x_i, S5_BATCH, 0)
            h_r = x_r + (a_r * s_r - a_i * s_i) + (p_r * c_r - p_i * c_i)
            h_i = x_i + (a_r * s_i + a_i * s_r) + (p_r * c_i + p_i * c_r)
            xr[pl.ds(r0, SUBLANES), sl] = h_r
            xi[pl.ds(r0, SUBLANES), sl] = h_i
            n_r = jnp.where(hi_rows, h_r, pltpu.roll(h_r, S5_BATCH, 0))
            n_i = jnp.where(hi_rows, h_i, pltpu.roll(h_i, S5_BATCH, 0))
            return n_r, n_i

        c_r, c_i = lax.fori_loop(0, rows // SUBLANES, body, (cr[:, sl], ci[:, sl]), unroll=2)
        cr[:, sl] = c_r
        ci[:, sl] = c_i

    y = (_dot(xr[...].astype(BF16), cre_ref[...]) - _dot(xi[...].astype(BF16), cim_ref[...])
         + d_ref[...] * u.astype(F32))
    y = jax.nn.gelu(y)
    y = y * jax.nn.sigmoid(_dot(y.astype(BF16), wglu_ref[...]))
    for c in range(D_SSM // LANES):
        ysc[c] = y[:, c * LANES:(c + 1) * LANES]
    for b in range(S5_BATCH):
        for c in range(D_SSM // LANES):
            y_ref[b, :, c * LANES:(c + 1) * LANES] = (
                ysc[c, pl.ds(b, ts, stride=S5_BATCH), :].astype(BF16))

    @pl.when(pl.program_id(0) == pl.num_programs(0) - 1)
    def _():
        c1_ref[0] = cr[...]
        c1_ref[1] = ci[...]


def _s5_tables(log_dt, a_re, a_im, b_re, b_im, c_re, c_im):
    dt = jnp.exp(log_dt.astype(F32))[:, None]
    ar, ai = a_re.astype(F32), a_im.astype(F32)
    mag = jnp.exp(dt * ar)
    abar_re, abar_im = mag * jnp.cos(dt * ai), mag * jnp.sin(dt * ai)
    den = ar * ar + ai * ai
    nr, ni = abar_re - 1.0, abar_im
    f_re = (nr * ar + ni * ai) / den
    f_im = (ni * ar - nr * ai) / den
    br, bi = b_re.astype(F32), b_im.astype(F32)
    bb_re = f_re[..., None] * br - f_im[..., None] * bi
    bb_im = f_re[..., None] * bi + f_im[..., None] * br
    eye = jnp.eye(SSM_GROUPS, dtype=F32)

    def in_blockdiag(bb):
        return jnp.einsum('gnc,gh->gchn', bb, eye).reshape(D_SSM, D_STATE)

    def out_blockdiag(c):
        return jnp.einsum('gcn,gh->gnhc', c.astype(F32), eye).reshape(D_STATE, D_SSM)

    a_r = abar_re.reshape(1, D_STATE)
    a_i = abar_im.reshape(1, D_STATE)
    a2_r = a_r * a_r - a_i * a_i
    a2_i = 2.0 * a_r * a_i
    hi = (jnp.arange(SUBLANES) >= S5_BATCH)[:, None]
    a1r = jnp.where(hi, a_r, 0.0)
    a1i = jnp.where(hi, a_i, 0.0)
    p_r = jnp.where(hi, a2_r, a_r)
    p_i = jnp.where(hi, a2_i, a_i)
    return (in_blockdiag(bb_re).astype(BF16), in_blockdiag(bb_im).astype(BF16),
            a1r, a1i, p_r, p_i,
            out_blockdiag(c_re).astype(BF16), out_blockdiag(c_im).astype(BF16))


def _s5(u_sb, carry, tables, d_skip, w_glu, B):
    assert B == S5_BATCH
    nt = u_sb.shape[0]
    ts = min(S5_TS, nt)
    rows = ts * B
    bre, bim, a1r, a1i, p_r, p_i, cre, cim = tables
    full = lambda shape: pl.BlockSpec(shape, lambda i: (0,) * len(shape))
    return pl.pallas_call(
        _s5_kernel,
        grid=(nt // ts,),
        in_specs=[pl.BlockSpec((rows, D_SSM), lambda i: (i, 0)),
                  full((2, SUBLANES, D_STATE)),
                  full((D_SSM, D_STATE)), full((D_SSM, D_STATE)),
                  full((SUBLANES, D_STATE)), full((SUBLANES, D_STATE)),
                  full((SUBLANES, D_STATE)), full((SUBLANES, D_STATE)),
                  full((D_STATE, D_SSM)), full((D_STATE, D_SSM)),
                  full((1, D_SSM)), full((D_SSM, D_SSM))],
        out_specs=[pl.BlockSpec((B, ts, D_SSM), lambda i: (0, i, 0)),
                   full((2, SUBLANES, D_STATE))],
        out_shape=[jax.ShapeDtypeStruct((B, nt, D_SSM), BF16),
                   jax.ShapeDtypeStruct((2, SUBLANES, D_STATE), F32)],
        scratch_shapes=[pltpu.VMEM((rows, D_STATE), F32), pltpu.VMEM((rows, D_STATE), F32),
                        pltpu.VMEM((SUBLANES, D_STATE), F32), pltpu.VMEM((SUBLANES, D_STATE), F32),
                        pltpu.VMEM((D_SSM // LANES, rows, LANES), F32)],
        compiler_params=pltpu.CompilerParams(
            dimension_semantics=("arbitrary",), vmem_limit_bytes=VMEM_LIMIT),
        name="s5",
    )(u_sb.reshape(nt * B, D_SSM), carry, bre, bim, a1r, a1i, p_r, p_i, cre, cim, d_skip, w_glu)


MOBA_PAIR = 2 * MOBA_BLOCK


def _moba_kernel(q0, q_ref, k_ref, v_ref, o_ref, kmean, kaug_a, kaug_b, vaug_a, vaug_b, m_s, acc_s,
                 s_buf):
    qi = pl.program_id(2) + q0
    nb = k_ref.shape[0] // MOBA_BLOCK
    nbp = kmean.shape[0]
    lane = lax.broadcasted_iota(jnp.int32, (1, LANES), 1)
    head_a = lane < HEAD_DIM

    @pl.when(pl.program_id(2) == 0)
    def _():
        kmean[...] = jnp.zeros_like(kmean)
        for j in range(nb):
            rows = pl.ds(j * MOBA_BLOCK, MOBA_BLOCK)
            kj = k_ref[rows, :].astype(F32)
            vj = v_ref[rows, :].astype(F32)
            kmean[j:j + 1, :] = jnp.sum(kj, axis=0, keepdims=True) * (1.0 / MOBA_BLOCK)
            kaug_a[rows, :] = jnp.where(head_a, kj, jnp.where(lane - HEAD_DIM == j, 1.0, 0.0)).astype(BF16)
            kaug_b[rows, :] = jnp.where(head_a, jnp.where(lane == j, 1.0, 0.0), kj).astype(BF16)
            vaug_a[rows, :] = jnp.where(head_a, vj, 1.0).astype(BF16)
            vaug_b[rows, :] = jnp.where(head_a, 1.0, vj).astype(BF16)

    qf = q_ref[...].astype(F32)
    blk_row = lax.broadcasted_iota(jnp.int32, (nbp, MOBA_BLOCK), 0)
    q_augs = []
    for is_a in (True, False):
        mine = head_a if is_a else jnp.logical_not(head_a)
        q_own = jnp.where(mine, qf, 0.0)
        g = _dot_nt(kmean[...], q_own, precision=HIGHEST)
        g = jnp.where(blk_row < qi, g, NEG)
        sel = jnp.zeros(g.shape, F32)
        for _ in range(MOBA_TOPK):
            m = jnp.max(g, axis=0, keepdims=True)
            idx = jnp.min(jnp.where(g == m, blk_row, nbp), axis=0, keepdims=True)
            hit = blk_row == idx
            sel = jnp.where(hit, jnp.where(idx < qi, 1.0, 0.0), sel)
            g = jnp.where(hit, -jnp.inf, g)
        bias_t = jnp.where(sel > 0.0, 0.0, jnp.where(blk_row == qi, 0.0, NEG))
        bias_t = jnp.concatenate([bias_t, jnp.full((LANES - nbp, MOBA_BLOCK), NEG, F32)], axis=0)
        bias = jnp.transpose(bias_t)
        if is_a:
            bias = pltpu.roll(bias, HEAD_DIM, 1)
        q_augs.append(jnp.where(mine, qf, bias).astype(BF16))

    m_s[...] = jnp.full(m_s.shape, -jnp.inf, F32)
    acc_s[...] = jnp.zeros_like(acc_s)
    qpos = qi * MOBA_BLOCK + lax.broadcasted_iota(jnp.int32, (MOBA_BLOCK, MOBA_PAIR), 0)
    col = lax.broadcasted_iota(jnp.int32, (MOBA_BLOCK, MOBA_PAIR), 1)

    def kv_rows(jj):
        return pl.ds(pl.multiple_of(jj * MOBA_PAIR, MOBA_PAIR), MOBA_PAIR)

    def scores(jj, slot):
        for hd, kaug in enumerate((kaug_a, kaug_b)):
            s_buf[slot, hd] = _dot_nt(q_augs[hd], kaug[kv_rows(jj), :])

    def softmax_pv(jj, slot, causal):
        for hd, vaug in enumerate((vaug_a, vaug_b)):
            s = s_buf[slot, hd]
            if causal:
                s = jnp.where(jj * MOBA_PAIR + col <= qpos, s, NEG)
            m_old = m_s[hd]
            m_new = jnp.maximum(m_old, jnp.max(s, axis=-1, keepdims=True))
            alpha = jnp.exp(m_old - m_new)
            p = jnp.exp(s - m_new)
            m_s[hd] = m_new
            acc_s[hd] = alpha * acc_s[hd] + _dot(p.astype(BF16), vaug[kv_rows(jj), :])

    last = qi // 2
    scores(0, 0)

    def body(k, _):
        scores(2 * k + 1, 1)
        softmax_pv(2 * k, 0, False)
        scores(2 * k + 2, 0)
        softmax_pv(2 * k + 1, 1, False)
        return 0

    lax.fori_loop(0, last // 2, body, 0)

    @pl.when(last % 2 == 0)
    def _():
        softmax_pv(last, 0, True)

    @pl.when(last % 2 == 1)
    def _():
        scores(last, 1)
        softmax_pv(last - 1, 0, False)
        softmax_pv(last, 1, True)
    acc_a, acc_b = acc_s[0], acc_s[1]
    o_ref[...] = jnp.where(head_a, acc_a / pltpu.roll(acc_a, HEAD_DIM, 1),
                           acc_b / pltpu.roll(acc_b, HEAD_DIM, 1)).astype(BF16)


def _moba(q, k, v, q0):
    B = q.shape[0]
    nq = q.shape[1] // MOBA_BLOCK
    skv = (q0 + nq) * MOBA_BLOCK
    nb = skv // MOBA_BLOCK
    assert nb <= HEAD_DIM and nb % 2 == 0 and skv <= k.shape[1]
    nbp = -(-nb // SUBLANES) * SUBLANES
    blk = pl.BlockSpec((None, MOBA_BLOCK, LANES), lambda b, h, i: (b, i, h))
    seq = pl.BlockSpec((None, skv, LANES), lambda b, h, i: (b, 0, h))
    return pl.pallas_call(
        functools.partial(_moba_kernel, q0),
        grid=(B, D_ATT // LANES, nq),
        in_specs=[blk, seq, seq],
        out_specs=blk,
        out_shape=jax.ShapeDtypeStruct(q.shape, BF16),
        scratch_shapes=[pltpu.VMEM((nbp, LANES), F32),
                        pltpu.VMEM((skv, LANES), BF16), pltpu.VMEM((skv, LANES), BF16),
                        pltpu.VMEM((skv, LANES), BF16), pltpu.VMEM((skv, LANES), BF16),
                        pltpu.VMEM((2, MOBA_BLOCK, 1), F32),
                        pltpu.VMEM((2, MOBA_BLOCK, LANES), F32),
                        pltpu.VMEM((2, 2, MOBA_BLOCK, MOBA_PAIR), F32)],
        compiler_params=pltpu.CompilerParams(
            dimension_semantics=("parallel", "parallel", "arbitrary"), vmem_limit_bytes=VMEM_LIMIT),
        name="moba",
    )(q, k, v)


MERGE_TS = 256


def _merge_kernel(x_ref, ys_ref, at_ref, ga_ref, gb_ref, wa_ref, wb_ref, wo_ref, g_ref,
                  wq_ref, k1_ref, k2_ref, x1_ref, hq_ref, sc_ref):
    ya = _dot(ys_ref[...], wa_ref[...])
    yb = _dot(at_ref[...], wb_ref[...])
    merged = ga_ref[...].astype(F32) * ya + gb_ref[...].astype(F32) * yb
    x1 = x_ref[...] + _dot(merged.astype(BF16), wo_ref[...])
    x1_ref[...] = x1
    hq = _rms(x1, g_ref[...])
    hq_ref[...] = hq
    qp = _dot(hq.astype(BF16), wq_ref[...])
    for h in range(PEER_HEADS):
        o = h * PEER_QDIM
        sc_ref[2 * h] = _dot_nt(k1_ref[h], qp[:, o:o + PEER_HALF], precision=HIGHEST)
        sc_ref[2 * h + 1] = _dot_nt(k2_ref[h], qp[:, o + PEER_HALF:o + PEER_QDIM], precision=HIGHEST)


def _merge(x, ys, att, ga, gb, t0, w_proj_ssm, w_proj_att, w_out, g_ffn, peer_w_q, keys1, keys2):
    B, nt = ys.shape[0], ys.shape[1]
    ts = min(MERGE_TS, nt)
    nblk = nt // ts
    i0 = t0 // ts
    tok = lambda d: pl.BlockSpec((None, ts, d), lambda b, i: (b, i, 0))
    row = lambda d: pl.BlockSpec((ts, d), lambda b, i: (b * nblk + i, 0))
    full = lambda shape: pl.BlockSpec(shape, lambda b, i: (0,) * len(shape))
    qd = PEER_HEADS * PEER_QDIM
    return pl.pallas_call(
        _merge_kernel,
        grid=(B, nblk),
        in_specs=[pl.BlockSpec((None, ts, D_MODEL), lambda b, i: (b, i0 + i, 0)),
                  tok(D_SSM), tok(D_ATT), tok(D_MODEL), tok(D_MODEL),
                  full((D_SSM, D_MODEL)), full((D_ATT, D_MODEL)), full((D_MODEL, D_MODEL)),
                  full((1, D_MODEL)), full((D_MODEL, qd)),
                  full((PEER_HEADS, PEER_KEYS, PEER_HALF)), full((PEER_HEADS, PEER_KEYS, PEER_HALF))],
        out_specs=[row(D_MODEL), row(D_MODEL),
                   pl.BlockSpec((2 * PEER_HEADS, PEER_KEYS, ts), lambda b, i: (0, 0, b * nblk + i))],
        out_shape=[jax.ShapeDtypeStruct((B * nt, D_MODEL), F32),
                   jax.ShapeDtypeStruct((B * nt, D_MODEL), F32),
                   jax.ShapeDtypeStruct((2 * PEER_HEADS, PEER_KEYS, B * nt), F32)],
        compiler_params=pltpu.CompilerParams(
            dimension_semantics=("parallel", "parallel"), vmem_limit_bytes=VMEM_LIMIT),
        name="merge",
    )(x, ys, att, ga, gb, w_proj_ssm, w_proj_att, w_out, g_ffn, peer_w_q, keys1, keys2)


TOPK_TS = 256


def _top_rows(s, row, k):
    vals, idxs = [], []
    for _ in range(k):
        m = jnp.max(s, axis=0, keepdims=True)
        idx = jnp.min(jnp.where(s == m, row, s.shape[0]), axis=0, keepdims=True)
        vals.append(m)
        idxs.append(idx)
        s = jnp.where(row == idx, -jnp.inf, s)
    return vals, idxs


def _stack_rows(rows, row16):
    acc = jnp.zeros(row16.shape, rows[0].dtype)
    for r, v in enumerate(rows):
        acc = jnp.where(row16 == r, v, acc)
    return acc


def _topk_kernel(sc_ref, idx_ref, gate_ref):
    ts = sc_ref.shape[-1]
    row = lax.broadcasted_iota(jnp.int32, (PEER_KEYS, ts), 0)
    row16 = lax.broadcasted_iota(jnp.int32, (PEER_TOPK, ts), 0)
    row8 = lax.broadcasted_iota(jnp.int32, (SUBLANES, ts), 0)
    counts = [PEER_TOPK // (i + 1) for i in range(PEER_TOPK)]
    heights = [PEER_TOPK if c > SUBLANES else SUBLANES for c in counts]
    n_cand = sum(heights)
    rowc = lax.broadcasted_iota(jnp.int32, (n_cand, ts), 0)
    gate_rows, eid_rows = [], []
    for h in range(PEER_HEADS):
        v1, i1 = _top_rows(sc_ref[2 * h], row, PEER_TOPK)
        v2, i2 = _top_rows(sc_ref[2 * h + 1], row, PEER_TOPK)
        v2s = _stack_rows(v2, row16)
        i2s = _stack_rows(i2, row16).astype(F32)
        cand, eid = [], []
        for i in range(PEER_TOPK):
            n = heights[i]
            cand.append(jnp.where((row16 if n == PEER_TOPK else row8) < counts[i],
                                  v1[i] + v2s[:n], -jnp.inf))
            eid.append(i1[i].astype(F32) * PEER_KEYS + i2s[:n])
        cand = jnp.concatenate(cand, axis=0)
        eid = jnp.concatenate(eid, axis=0)
        tops, picks = [], []
        for _ in range(PEER_TOPK):
            m = jnp.max(cand, axis=0, keepdims=True)
            pos = jnp.min(jnp.where(cand == m, rowc, n_cand), axis=0, keepdims=True)
            hit = rowc == pos
            picks.append(jnp.max(jnp.where(hit, eid, -1.0), axis=0, keepdims=True))
            tops.append(m)
            cand = jnp.where(hit, -jnp.inf, cand)
        top = _stack_rows(tops, row16)
        p = jnp.exp(top - jnp.max(top, axis=0, keepdims=True))
        gate_rows.append(p / jnp.sum(p, axis=0, keepdims=True))
        eid_rows.append(_stack_rows(picks, row16))
    gate_ref[...] = jnp.transpose(jnp.concatenate(gate_rows, axis=0))
    idx_ref[...] = jnp.transpose(jnp.concatenate(eid_rows, axis=0)).astype(jnp.int32)


def _topk(scores):
    T = scores.shape[-1]
    ts = min(TOPK_TS, T)
    return pl.pallas_call(
        _topk_kernel,
        grid=(T // ts,),
        in_specs=[pl.BlockSpec((2 * PEER_HEADS, PEER_KEYS, ts), lambda i: (0, 0, i))],
        out_specs=[pl.BlockSpec((ts, PEER_SEL), lambda i: (i, 0)),
                   pl.BlockSpec((ts, PEER_SEL), lambda i: (i, 0))],
        out_shape=[jax.ShapeDtypeStruct((T, PEER_SEL), jnp.int32),
                   jax.ShapeDtypeStruct((T, PEER_SEL), F32)],
        compiler_params=pltpu.CompilerParams(
            dimension_semantics=("parallel",), vmem_limit_bytes=VMEM_LIMIT),
        name="topk",
    )(scores)


SC_CORES = 2
SC_SUBCORES = 16
SC_LANES = 16
SC_WORKERS = SC_CORES * SC_SUBCORES
PEER_CH = SC_LANES
PEER_NCH = PEER_SEL // PEER_CH
PEER_WORDS = D_MODEL // 2
PEER_NWG = PEER_WORDS // SC_LANES
PEER_RING = 4
PEER_QUAD = 4
HI_MASK = -65536
GELU_C = 0.7978845608028654


def _gelu_tanh_via_exp(x):
    z = GELU_C * (x + 0.044715 * (x * x * x))
    t = 1.0 - 2.0 / (jnp.exp(2.0 * z) + 1.0)
    return 0.5 * x * (1.0 + t)


def _unpack_pair(w):
    lo = plsc.bitcast(lax.shift_left(w, 16), F32)
    hi = plsc.bitcast(lax.bitwise_and(w, HI_MASK), F32)
    return lo, hi


def _peer_sc_body(idx_hbm, gate_hbm, h_hbm, u_hbm, v_hbm, o_hbm,
                  idx_v, gate_v, h_v, ubuf, vbuf, pbuf, w_v, out_v, usem, vsem, msem, osem):
    n_tok = o_hbm.shape[0] // SC_WORKERS
    base = (lax.axis_index("s") * SC_CORES + lax.axis_index("c")) * n_tok
    lane = lax.iota(jnp.int32, SC_LANES)
    zero_rows = jnp.zeros((SC_LANES,), jnp.int32)

    def meta_copies(tok, s):
        return (pltpu.make_async_copy(idx_hbm.at[tok], idx_v.at[s], msem.at[s]),
                pltpu.make_async_copy(gate_hbm.at[tok], gate_v.at[s], msem.at[s]),
                pltpu.make_async_copy(h_hbm.at[tok], h_v.at[s], msem.at[s]))

    def gather(tab, buf, sem, slot, rows):
        return pltpu.make_async_copy(tab.at[rows], buf.at[slot], sem.at[slot])

    def start_ahead(tab, buf, sem, s, c):
        ahead = c + PEER_RING
        src = jnp.where(ahead < PEER_NCH, s, 1 - s)
        ch = ahead % PEER_NCH
        rows = idx_v[src, pl.ds(pl.multiple_of(ch * PEER_CH, PEER_CH), PEER_CH)]
        gather(tab, buf, sem, c % PEER_RING, rows).start()

    def token(t, carry):
        s = t % 2
        tok = base + t
        nxt = base + jnp.minimum(t + 1, n_tok - 1)
        for cp in meta_copies(nxt, 1 - s):
            cp.start()

        def u_chunk(c, carry):
            slot = c % PEER_RING
            gather(u_hbm, ubuf, usem, slot, zero_rows).wait()

            def dot_step(q, accs):
                cols = [pl.ds(pl.multiple_of((q * PEER_QUAD + j) * SC_LANES, SC_LANES), SC_LANES)
                        for j in range(PEER_QUAD)]
                hs = [plsc.bitcast(h_v[s, col], BF16) for col in cols]
                out = []
                for r in range(PEER_CH):
                    p = plsc.bitcast(ubuf[slot, r, cols[0]], BF16) * hs[0]
                    for j in range(1, PEER_QUAD):
                        p = p + plsc.bitcast(ubuf[slot, r, cols[j]], BF16) * hs[j]
                    lo, hi = _unpack_pair(plsc.bitcast(p, jnp.int32))
                    out.append(accs[r] + lo + hi)
                return tuple(out)

            accs = lax.fori_loop(0, PEER_NWG // PEER_QUAD, dot_step,
                                 tuple(jnp.zeros((SC_LANES,), F32) for _ in range(PEER_CH)))

            @pl.when(c == PEER_NCH - PEER_RING)
            def _():
                for cp in meta_copies(nxt, 1 - s):
                    cp.wait()

            start_ahead(u_hbm, ubuf, usem, s, c)
            for r in range(PEER_CH):
                pbuf[r, :] = accs[r]
            tot = jnp.zeros((SC_LANES,), F32)
            for j in range(SC_LANES):
                tot = tot + plsc.load_gather(pbuf, [lane, jnp.full((SC_LANES,), j, jnp.int32)])
            rows = pl.ds(pl.multiple_of(c * PEER_CH, PEER_CH), PEER_CH)
            w_v[rows] = gate_v[s, rows] * _gelu_tanh_via_exp(tot)
            return carry

        lax.fori_loop(0, PEER_NCH, u_chunk, 0)

        @pl.when(t >= 2)
        def _():
            pltpu.make_async_copy(out_v.at[s], o_hbm.at[tok], osem.at[s]).wait()

        def v_chunk(c, carry):
            slot = c % PEER_RING
            gather(v_hbm, vbuf, vsem, slot, zero_rows).wait()
            ws = []
            for r in range(PEER_CH):
                w = plsc.load_gather(w_v, [jnp.full((SC_LANES,), r, jnp.int32) + c * PEER_CH])
                ws.append(plsc.pack(w, w, format=plsc.PackFormat.INTERLEAVED,
                                    preferred_element_type=BF16))
            first = c == 0

            @plsc.parallel_loop(0, PEER_NWG, unroll=2)
            def acc_step(g):
                col = pl.ds(pl.multiple_of(g * SC_LANES, SC_LANES), SC_LANES)
                col_hi = pl.ds(pl.multiple_of(PEER_WORDS + g * SC_LANES, SC_LANES), SC_LANES)
                o_lo = jnp.where(first, 0.0, out_v[s, col])
                o_hi = jnp.where(first, 0.0, out_v[s, col_hi])
                for r0 in range(0, PEER_CH, PEER_QUAD):
                    p = plsc.bitcast(vbuf[slot, r0, col], BF16) * ws[r0]
                    for r in range(r0 + 1, r0 + PEER_QUAD):
                        p = p + plsc.bitcast(vbuf[slot, r, col], BF16) * ws[r]
                    lo, hi = _unpack_pair(plsc.bitcast(p, jnp.int32))
                    o_lo = o_lo + lo
                    o_hi = o_hi + hi
                out_v[s, col] = o_lo
                out_v[s, col_hi] = o_hi

            start_ahead(v_hbm, vbuf, vsem, s, c)
            return carry

        lax.fori_loop(0, PEER_NCH, v_chunk, 0)
        pltpu.make_async_copy(out_v.at[s], o_hbm.at[tok], osem.at[s]).start()
        return carry

    for cp in meta_copies(base, 0):
        cp.start()
    for cp in meta_copies(base, 0):
        cp.wait()
    for c in range(PEER_RING):
        rows = idx_v[0, pl.ds(c * PEER_CH, PEER_CH)]
        gather(u_hbm, ubuf, usem, c, rows).start()
        gather(v_hbm, vbuf, vsem, c, rows).start()
    lax.fori_loop(0, n_tok, token, 0)
    for c in range(PEER_RING):
        gather(u_hbm, ubuf, usem, c, zero_rows).wait()
        gather(v_hbm, vbuf, vsem, c, zero_rows).wait()
    for s in range(2):
        pltpu.make_async_copy(out_v.at[s], o_hbm.at[base], osem.at[s]).wait()


def _pack_bf16_pairs(tab):
    b = lax.bitcast_convert_type(tab.astype(BF16), jnp.uint16).astype(jnp.uint32)
    half = tab.shape[1] // 2
    return lax.bitcast_convert_type(b[:, :half] | (b[:, half:] << 16), jnp.int32)


def _peer(idx, hq, gates, u_words, v_words):
    T = hq.shape[0]
    assert T % (2 * SC_WORKERS) == 0
    mesh = plsc.VectorSubcoreMesh(core_axis_name="c", subcore_axis_name="s",
                                  num_cores=SC_CORES, num_subcores=SC_SUBCORES)
    return pl.kernel(
        _peer_sc_body,
        out_type=jax.ShapeDtypeStruct((T, D_MODEL), F32),
        mesh=mesh,
        scratch_types=[
            pltpu.VMEM((2, PEER_SEL), jnp.int32), pltpu.VMEM((2, PEER_SEL), F32),
            pltpu.VMEM((2, PEER_WORDS), jnp.int32),
            pltpu.VMEM((PEER_RING, PEER_CH, PEER_WORDS), jnp.int32),
            pltpu.VMEM((PEER_RING, PEER_CH, PEER_WORDS), jnp.int32),
            pltpu.VMEM((PEER_CH, SC_LANES), F32), pltpu.VMEM((PEER_SEL,), F32),
            pltpu.VMEM((2, D_MODEL), F32),
            pltpu.SemaphoreType.DMA((PEER_RING,)), pltpu.SemaphoreType.DMA((PEER_RING,)),
            pltpu.SemaphoreType.DMA((2,)), pltpu.SemaphoreType.DMA((2,)),
        ],
        compiler_params=pltpu.CompilerParams(needs_layout_passes=False),
        name="peer_sc",
    )(idx, gates, _pack_bf16_pairs(hq), u_words, v_words)


FINAL_TS = 256


def _final_kernel(x1_ref, pe_ref, p_ref, gp_ref, wg_ref, wp_ref, gf_ref, o_ref):
    x2 = x1_ref[...] + pe_ref[...]
    e = _dot(p_ref[...].astype(BF16), wp_ref[...])
    gate = jax.nn.sigmoid(_dot(_rms(x2, gp_ref[...]).astype(BF16), wg_ref[...]))
    o_ref[...] = _rms(x2 + gate * e, gf_ref[...])


def _final(x1, peer_out, p, t0, nt, g_ple, ple_w_gate, ple_w_proj, g_final):
    B = p.shape[0]
    ts = min(FINAL_TS, nt)
    nblk = nt // ts
    i0 = t0 // ts
    row = lambda d: pl.BlockSpec((ts, d), lambda b, i: (b * nblk + i, 0))
    full = lambda shape: pl.BlockSpec(shape, lambda b, i: (0,) * len(shape))
    return pl.pallas_call(
        _final_kernel,
        grid=(B, nblk),
        in_specs=[row(D_MODEL), row(D_MODEL),
                  pl.BlockSpec((None, ts, D_PLE), lambda b, i: (b, i0 + i, 0)),
                  full((1, D_MODEL)), full((D_MODEL, D_MODEL)), full((D_PLE, D_MODEL)),
                  full((1, D_MODEL))],
        out_specs=pl.BlockSpec((None, ts, D_MODEL), lambda b, i: (b, i, 0)),
        out_shape=jax.ShapeDtypeStruct((B, nt, D_MODEL), F32),
        compiler_params=pltpu.CompilerParams(
            dimension_semantics=("parallel", "parallel"), vmem_limit_bytes=VMEM_LIMIT),
        name="final",
    )(x1, peer_out, p, g_ple, ple_w_gate, ple_w_proj, g_final)


CHUNK_STEPS = (512, 512, 1024, 1024, 1024, 1024, 1024, 1024, 1024)


def kernel(x, p, positions, g_mix, w_in, ssm_log_dt, ssm_a_re, ssm_a_im, ssm_b_re, ssm_b_im,
           ssm_c_re, ssm_c_im, ssm_d, ssm_w_glu, w_proj_ssm, w_proj_att, w_out, g_ffn,
           peer_w_q, peer_keys1, peer_keys2, peer_u, peer_v, g_ple, ple_w_gate, ple_w_proj,
           g_final):
    B, S, _ = x.shape
    assert w_in.shape[0] == 1, "the final rmsnorm is fused into the single layer's last stage"
    steps = CHUNK_STEPS if sum(CHUNK_STEPS) == S else (S,)
    i = 0
    tables = _s5_tables(ssm_log_dt[i], ssm_a_re[i], ssm_a_im[i], ssm_b_re[i], ssm_b_im[i],
                        ssm_c_re[i], ssm_c_im[i])
    w_in_b, w_glu_b = w_in[i].astype(BF16), ssm_w_glu[i].astype(BF16)
    d_skip = ssm_d[i].reshape(1, D_SSM).astype(F32)
    merge_w = (w_proj_ssm[i].astype(BF16), w_proj_att[i].astype(BF16), w_out[i].astype(BF16),
               g_ffn[i].reshape(1, D_MODEL), peer_w_q[i].astype(BF16), peer_keys1[i], peer_keys2[i])
    final_w = (g_ple[i].reshape(1, D_MODEL), ple_w_gate[i].astype(BF16),
               ple_w_proj[i].astype(BF16), g_final.reshape(1, D_MODEL))
    u_words = _pack_bf16_pairs(peer_u[i])
    v_words = _pack_bf16_pairs(peer_v[i])
    k_all = jnp.zeros((B, S, D_ATT), BF16)
    v_all = jnp.zeros((B, S, D_ATT), BF16)
    carry = jnp.zeros((2, SUBLANES, D_STATE), F32)
    outs = []
    t0 = 0
    after = carry
    for nt in steps:
        u_sb, q, k, v, ga, gb = _in_proj(x, positions, g_mix[i], w_in_b, t0, nt, after)
        k_all = lax.dynamic_update_slice(k_all, k, (0, t0, 0))
        v_all = lax.dynamic_update_slice(v_all, v, (0, t0, 0))
        ys, carry = _s5(u_sb, carry, tables, d_skip, w_glu_b, B)
        att = _moba(q, k_all, v_all, t0 // MOBA_BLOCK)
        x1, hq, scores = _merge(x, ys, att, ga, gb, t0, *merge_w)
        idx, gates = _topk(scores)
        after = gates
        peer_out = _peer(idx, hq, gates, u_words, v_words)
        outs.append(_final(x1, peer_out, p[i], t0, nt, *final_w))
        t0 += nt
    return jnp.concatenate(outs, axis=1)
```

```python
import functools
import math

import jax
import jax.numpy as jnp
from jax import lax
from jax.experimental import pallas as pl
from jax.experimental.pallas import tpu as pltpu
from jax.experimental.pallas import tpu_sc as plsc

F32 = jnp.float32
BF16 = jnp.bfloat16

D_MODEL = 1024
D_SSM = 512
SSM_GROUP = 16
SSM_GROUPS = 32
SSM_STATE = 64
D_STATE = SSM_GROUPS * SSM_STATE
N_HEADS = 8
HEAD_DIM = 64
D_ATT = 512
ROT_DIM = 16
ROPE_THETA = 500000.0
MOBA_BLOCK = 256
MOBA_TOPK = 3
PEER_HEADS = 8
PEER_KEYS = 128
PEER_QDIM = 256
PEER_HALF = 128
PEER_TOPK = 16
PEER_SEL = PEER_HEADS * PEER_TOPK
D_PLE = 256
EPS = 1e-6
NEG = -1e30
LANES = 128
SUBLANES = 8
VMEM_LIMIT = 48 * 1024 * 1024
HIGHEST = lax.Precision.HIGHEST


def _rms(x, g):
    return x * lax.rsqrt(jnp.mean(x * x, axis=-1, keepdims=True) + EPS) * g


def _dot(a, b):
    return jnp.dot(a, b, preferred_element_type=F32)


def _dot_nt(a, b, precision=None):
    return lax.dot_general(a, b, (((1,), (1,)), ((), ())), precision=precision,
                           preferred_element_type=F32)


IN_TS = 512


def _in_proj_kernel(x_ref, pos_ref, g_ref, w_ref, invf_ref, after_a, after_b,
                    u_ref, q_ref, k_ref, v_ref, ga_ref, gb_ref):
    del after_a, after_b
    h = _rms(x_ref[...], g_ref[...]).astype(BF16)

    def proj(lo, hi):
        return _dot(h, w_ref[:, lo:hi])

    u_ref[...] = proj(0, D_SSM).astype(BF16)
    ang = pos_ref[...].astype(F32) * invf_ref[...]
    cos = jnp.cos(ang)
    sin = jnp.sin(ang)
    lane = lax.broadcasted_iota(jnp.int32, (1, LANES), 1) % HEAD_DIM
    half = ROT_DIM // 2
    sin_hi = jnp.where((lane >= half) & (lane < ROT_DIM), sin, 0.0)
    sin_lo = jnp.where(lane < half, -sin, 0.0)
    reps = D_ATT // LANES
    cos4 = jnp.concatenate([cos] * reps, axis=1)
    sin_hi4 = jnp.concatenate([sin_hi] * reps, axis=1)
    sin_lo4 = jnp.concatenate([sin_lo] * reps, axis=1)

    def rope(t):
        return (t * cos4 + pltpu.roll(t, half, 1) * sin_hi4
                + pltpu.roll(t, D_ATT - half, 1) * sin_lo4)

    q = rope(proj(D_SSM, D_SSM + D_ATT))
    q_ref[...] = (q * (HEAD_DIM ** -0.5)).astype(BF16)
    k_ref[...] = rope(proj(D_SSM + D_ATT, D_SSM + 2 * D_ATT)).astype(BF16)
    v_ref[...] = proj(D_SSM + 2 * D_ATT, D_SSM + 3 * D_ATT).astype(BF16)
    o = D_SSM + 3 * D_ATT
    ga_ref[...] = jax.nn.sigmoid(proj(o, o + D_MODEL)).astype(BF16)
    gb_ref[...] = jax.nn.sigmoid(proj(o + D_MODEL, o + 2 * D_MODEL)).astype(BF16)


def _in_proj(x, positions, g_mix, w_in, t0, nt, after):
    B, S, _ = x.shape
    ts = min(IN_TS, nt)
    assert nt % ts == 0 and t0 % ts == 0
    i0 = t0 // ts
    inv_freq = ROPE_THETA ** (-jnp.arange(0, ROT_DIM, 2, dtype=F32) / ROT_DIM)
    lane = jnp.arange(LANES) % HEAD_DIM
    invf = jnp.where(lane < ROT_DIM, inv_freq[lane % (ROT_DIM // 2)], 0.0).reshape(1, LANES)
    d_in = w_in.shape[1]
    src = lambda d: pl.BlockSpec((None, ts, d), lambda b, i: (b, i0 + i, 0))
    tok = lambda d: pl.BlockSpec((None, ts, d), lambda b, i: (b, i, 0))
    full = lambda shape: pl.BlockSpec(shape, lambda b, i: (0,) * len(shape))
    return pl.pallas_call(
        _in_proj_kernel,
        grid=(B, nt // ts),
        in_specs=[src(D_MODEL), src(1), full((1, D_MODEL)), full((D_MODEL, d_in)), full((1, LANES)),
                  pl.BlockSpec(memory_space=pl.ANY), pl.BlockSpec(memory_space=pl.ANY)],
        out_specs=[pl.BlockSpec((ts, D_SSM), lambda b, i: (i, b)),
                   tok(D_ATT), tok(D_ATT), tok(D_ATT), tok(D_MODEL), tok(D_MODEL)],
        out_shape=[jax.ShapeDtypeStruct((nt, B * D_SSM), BF16),
                   jax.ShapeDtypeStruct((B, nt, D_ATT), BF16),
                   jax.ShapeDtypeStruct((B, nt, D_ATT), BF16),
                   jax.ShapeDtypeStruct((B, nt, D_ATT), BF16),
                   jax.ShapeDtypeStruct((B, nt, D_MODEL), BF16),
                   jax.ShapeDtypeStruct((B, nt, D_MODEL), BF16)],
        compiler_params=pltpu.CompilerParams(
            dimension_semantics=("parallel", "parallel"), vmem_limit_bytes=VMEM_LIMIT),
        name="in_proj",
    )(x, positions.reshape(B, S, 1), g_mix.reshape(1, D_MODEL), w_in, invf, *after)


S5_TS = 128
S5_BATCH = 4
S5_COLS = 512


def _s5_kernel(u_ref, c0_ref, bre_ref, bim_ref, a1r_ref, a1i_ref, pr_ref, pi_ref,
               cre_ref, cim_ref, d_ref, wglu_ref, y_ref, c1_ref,
               xr, xi, cr, ci, ysc):
    rows = xr.shape[0]
    ts = rows // S5_BATCH

    @pl.when(pl.program_id(0) == 0)
    def _():
        cr[...] = c0_ref[0]
        ci[...] = c0_ref[1]

    u = u_ref[...]
    xr[...] = _dot(u, bre_ref[...])
    xi[...] = _dot(u, bim_ref[...])

    hi_rows = lax.broadcasted_iota(jnp.int32, (SUBLANES, S5_COLS), 0) >= S5_BATCH
    for cb in range(D_STATE // S5_COLS):
        sl = slice(cb * S5_COLS, (cb + 1) * S5_COLS)
        a_r, a_i = a1r_ref[:, sl], a1i_ref[:, sl]
        p_r, p_i = pr_ref[:, sl], pi_ref[:, sl]

        def body(t, carry):
            c_r, c_i = carry
            r0 = pl.multiple_of(t * SUBLANES, SUBLANES)
            x_r = xr[pl.ds(r0, SUBLANES), sl]
            x_i = xi[pl.ds(r0, SUBLANES), sl]
            s_r = pltpu.roll(x_r, S5_BATCH, 0)
            s_i = pltpu.roll(x_i, S5_BATCH, 0)
            h_r = x_r + (a_r * s_r - a_i * s_i) + (p_r * c_r - p_i * c_i)
            h_i = x_i + (a_r * s_i + a_i * s_r) + (p_r * c_i + p_i * c_r)
            xr[pl.ds(r0, SUBLANES), sl] = h_r
            xi[pl.ds(r0, SUBLANES), sl] = h_i
            n_r = jnp.where(hi_rows, h_r, pltpu.roll(h_r, S5_BATCH, 0))
            n_i = jnp.where(hi_rows, h_i, pltpu.roll(h_i, S5_BATCH, 0))
            return n_r, n_i

        c_r, c_i = lax.fori_loop(0, rows // SUBLANES, body, (cr[:, sl], ci[:, sl]), unroll=2)
        cr[:, sl] = c_r
        ci[:, sl] = c_i

    y = (_dot(xr[...].astype(BF16), cre_ref[...]) - _dot(xi[...].astype(BF16), cim_ref[...])
         + d_ref[...] * u.astype(F32))
    y = jax.nn.gelu(y)
    y = y * jax.nn.sigmoid(_dot(y.astype(BF16), wglu_ref[...]))
    for c in range(D_SSM // LANES):
        ysc[c] = y[:, c * LANES:(c + 1) * LANES]
    for b in range(S5_BATCH):
        for c in range(D_SSM // LANES):
            y_ref[b, :, c * LANES:(c + 1) * LANES] = (
                ysc[c, pl.ds(b, ts, stride=S5_BATCH), :].astype(BF16))

    @pl.when(pl.program_id(0) == pl.num_programs(0) - 1)
    def _():
        c1_ref[0] = cr[...]
        c1_ref[1] = ci[...]


def _s5_tables(log_dt, a_re, a_im, b_re, b_im, c_re, c_im):
    dt = jnp.exp(log_dt.astype(F32))[:, None]
    ar, ai = a_re.astype(F32), a_im.astype(F32)
    mag = jnp.exp(dt * ar)
    abar_re, abar_im = mag * jnp.cos(dt * ai), mag * jnp.sin(dt * ai)
    den = ar * ar + ai * ai
    nr, ni = abar_re - 1.0, abar_im
    f_re = (nr * ar + ni * ai) / den
    f_im = (ni * ar - nr * ai) / den
    br, bi = b_re.astype(F32), b_im.astype(F32)
    bb_re = f_re[..., None] * br - f_im[..., None] * bi
    bb_im = f_re[..., None] * bi + f_im[..., None] * br
    eye = jnp.eye(SSM_GROUPS, dtype=F32)

    def in_blockdiag(bb):
        return jnp.einsum('gnc,gh->gchn', bb, eye).reshape(D_SSM, D_STATE)

    def out_blockdiag(c):
        return jnp.einsum('gcn,gh->gnhc', c.astype(F32), eye).reshape(D_STATE, D_SSM)

    a_r = abar_re.reshape(1, D_STATE)
    a_i = abar_im.reshape(1, D_STATE)
    a2_r = a_r * a_r - a_i * a_i
    a2_i = 2.0 * a_r * a_i
    hi = (jnp.arange(SUBLANES) >= S5_BATCH)[:, None]
    a1r = jnp.where(hi, a_r, 0.0)
    a1i = jnp.where(hi, a_i, 0.0)
    p_r = jnp.where(hi, a2_r, a_r)
    p_i = jnp.where(hi, a2_i, a_i)
    return (in_blockdiag(bb_re).astype(BF16), in_blockdiag(bb_im).astype(BF16),
            a1r, a1i, p_r, p_i,
            out_blockdiag(c_re).astype(BF16), out_blockdiag(c_im).astype(BF16))


def _s5(u_sb, carry, tables, d_skip, w_glu, B):
    assert B == S5_BATCH
    nt = u_sb.shape[0]
    ts = min(S5_TS, nt)
    rows = ts * B
    bre, bim, a1r, a1i, p_r, p_i, cre, cim = tables
    full = lambda shape: pl.BlockSpec(shape, lambda i: (0,) * len(shape))
    return pl.pallas_call(
        _s5_kernel,
        grid=(nt // ts,),
        in_specs=[pl.BlockSpec((rows, D_SSM), lambda i: (i, 0)),
                  full((2, SUBLANES, D_STATE)),
                  full((D_SSM, D_STATE)), full((D_SSM, D_STATE)),
                  full((SUBLANES, D_STATE)), full((SUBLANES, D_STATE)),
                  full((SUBLANES, D_STATE)), full((SUBLANES, D_STATE)),
                  full((D_STATE, D_SSM)), full((D_STATE, D_SSM)),
                  full((1, D_SSM)), full((D_SSM, D_SSM))],
        out_specs=[pl.BlockSpec((B, ts, D_SSM), lambda i: (0, i, 0)),
                   full((2, SUBLANES, D_STATE))],
        out_shape=[jax.ShapeDtypeStruct((B, nt, D_SSM), BF16),
                   jax.ShapeDtypeStruct((2, SUBLANES, D_STATE), F32)],
        scratch_shapes=[pltpu.VMEM((rows, D_STATE), F32), pltpu.VMEM((rows, D_STATE), F32),
                        pltpu.VMEM((SUBLANES, D_STATE), F32), pltpu.VMEM((SUBLANES, D_STATE), F32),
                        pltpu.VMEM((D_SSM // LANES, rows, LANES), F32)],
        compiler_params=pltpu.CompilerParams(
            dimension_semantics=("arbitrary",), vmem_limit_bytes=VMEM_LIMIT),
        name="s5",
    )(u_sb.reshape(nt * B, D_SSM), carry, bre, bim, a1r, a1i, p_r, p_i, cre, cim, d_skip, w_glu)


MOBA_PAIR = 2 * MOBA_BLOCK


def _moba_kernel(q0, q_ref, k_ref, v_ref, o_ref, kmean, kaug_a, kaug_b, vaug_a, vaug_b, m_s, acc_s,
                 s_buf):
    qi = pl.program_id(2) + q0
    nb = k_ref.shape[0] // MOBA_BLOCK
    nbp = kmean.shape[0]
    lane = lax.broadcasted_iota(jnp.int32, (1, LANES), 1)
    head_a = lane < HEAD_DIM

    @pl.when(pl.program_id(2) == 0)
    def _():
        kmean[...] = jnp.zeros_like(kmean)
        for j in range(nb):
            rows = pl.ds(j * MOBA_BLOCK, MOBA_BLOCK)
            kj = k_ref[rows, :].astype(F32)
            vj = v_ref[rows, :].astype(F32)
            kmean[j:j + 1, :] = jnp.sum(kj, axis=0, keepdims=True) * (1.0 / MOBA_BLOCK)
            kaug_a[rows, :] = jnp.where(head_a, kj, jnp.where(lane - HEAD_DIM == j, 1.0, 0.0)).astype(BF16)
            kaug_b[rows, :] = jnp.where(head_a, jnp.where(lane == j, 1.0, 0.0), kj).astype(BF16)
            vaug_a[rows, :] = jnp.where(head_a, vj, 1.0).astype(BF16)
            vaug_b[rows, :] = jnp.where(head_a, 1.0, vj).astype(BF16)

    qf = q_ref[...].astype(F32)
    blk_row = lax.broadcasted_iota(jnp.int32, (nbp, MOBA_BLOCK), 0)
    q_augs = []
    for is_a in (True, False):
        mine = head_a if is_a else jnp.logical_not(head_a)
        q_own = jnp.where(mine, qf, 0.0)
        g = _dot_nt(kmean[...], q_own, precision=HIGHEST)
        g = jnp.where(blk_row < qi, g, NEG)
        sel = jnp.zeros(g.shape, F32)
        for _ in range(MOBA_TOPK):
            m = jnp.max(g, axis=0, keepdims=True)
            idx = jnp.min(jnp.where(g == m, blk_row, nbp), axis=0, keepdims=True)
            hit = blk_row == idx
            sel = jnp.where(hit, jnp.where(idx < qi, 1.0, 0.0), sel)
            g = jnp.where(hit, -jnp.inf, g)
        bias_t = jnp.where(sel > 0.0, 0.0, jnp.where(blk_row == qi, 0.0, NEG))
        bias_t = jnp.concatenate([bias_t, jnp.full((LANES - nbp, MOBA_BLOCK), NEG, F32)], axis=0)
        bias = jnp.transpose(bias_t)
        if is_a:
            bias = pltpu.roll(bias, HEAD_DIM, 1)
        q_augs.append(jnp.where(mine, qf, bias).astype(BF16))

    m_s[...] = jnp.full(m_s.shape, -jnp.inf, F32)
    acc_s[...] = jnp.zeros_like(acc_s)
    qpos = qi * MOBA_BLOCK + lax.broadcasted_iota(jnp.int32, (MOBA_BLOCK, MOBA_PAIR), 0)
    col = lax.broadcasted_iota(jnp.int32, (MOBA_BLOCK, MOBA_PAIR), 1)

    def kv_rows(jj):
        return pl.ds(pl.multiple_of(jj * MOBA_PAIR, MOBA_PAIR), MOBA_PAIR)

    def scores(jj, slot):
        for hd, kaug in enumerate((kaug_a, kaug_b)):
            s_buf[slot, hd] = _dot_nt(q_augs[hd], kaug[kv_rows(jj), :])

    def softmax_pv(jj, slot, causal):
        for hd, vaug in enumerate((vaug_a, vaug_b)):
            s = s_buf[slot, hd]
            if causal:
                s = jnp.where(jj * MOBA_PAIR + col <= qpos, s, NEG)
            m_old = m_s[hd]
            m_new = jnp.maximum(m_old, jnp.max(s, axis=-1, keepdims=True))
            alpha = jnp.exp(m_old - m_new)
            p = jnp.exp(s - m_new)
            m_s[hd] = m_new
            acc_s[hd] = alpha * acc_s[hd] + _dot(p.astype(BF16), vaug[kv_rows(jj), :])

    last = qi // 2
    scores(0, 0)

    def body(k, _):
        scores(2 * k + 1, 1)
        softmax_pv(2 * k, 0, False)
        scores(2 * k + 2, 0)
        softmax_pv(2 * k + 1, 1, False)
        return 0

    lax.fori_loop(0, last // 2, body, 0)

    @pl.when(last % 2 == 0)
    def _():
        softmax_pv(last, 0, True)

    @pl.when(last % 2 == 1)
    def _():
        scores(last, 1)
        softmax_pv(last - 1, 0, False)
        softmax_pv(last, 1, True)
    acc_a, acc_b = acc_s[0], acc_s[1]
    o_ref[...] = jnp.where(head_a, acc_a / pltpu.roll(acc_a, HEAD_DIM, 1),
                           acc_b / pltpu.roll(acc_b, HEAD_DIM, 1)).astype(BF16)


def _moba(q, k, v, q0):
    B = q.shape[0]
    nq = q.shape[1] // MOBA_BLOCK
    skv = (q0 + nq) * MOBA_BLOCK
    nb = skv // MOBA_BLOCK
    assert nb <= HEAD_DIM and nb % 2 == 0 and skv <= k.shape[1]
    nbp = -(-nb // SUBLANES) * SUBLANES
    blk = pl.BlockSpec((None, MOBA_BLOCK, LANES), lambda b, h, i: (b, i, h))
    seq = pl.BlockSpec((None, skv, LANES), lambda b, h, i: (b, 0, h))
    return pl.pallas_call(
        functools.partial(_moba_kernel, q0),
        grid=(B, D_ATT // LANES, nq),
        in_specs=[blk, seq, seq],
        out_specs=blk,
        out_shape=jax.ShapeDtypeStruct(q.shape, BF16),
        scratch_shapes=[pltpu.VMEM((nbp, LANES), F32),
                        pltpu.VMEM((skv, LANES), BF16), pltpu.VMEM((skv, LANES), BF16),
                        pltpu.VMEM((skv, LANES), BF16), pltpu.VMEM((skv, LANES), BF16),
                        pltpu.VMEM((2, MOBA_BLOCK, 1), F32),
                        pltpu.VMEM((2, MOBA_BLOCK, LANES), F32),
                        pltpu.VMEM((2, 2, MOBA_BLOCK, MOBA_PAIR), F32)],
        compiler_params=pltpu.CompilerParams(
            dimension_semantics=("parallel", "parallel", "arbitrary"), vmem_limit_bytes=VMEM_LIMIT),
        name="moba",
    )(q, k, v)


MERGE_TS = 256


def _merge_kernel(x_ref, ys_ref, at_ref, ga_ref, gb_ref, wa_ref, wb_ref, wo_ref, g_ref,
                  wq_ref, k1_ref, k2_ref, x1_ref, hq_ref, sc_ref):
    ya = _dot(ys_ref[...], wa_ref[...])
    yb = _dot(at_ref[...], wb_ref[...])
    merged = ga_ref[...].astype(F32) * ya + gb_ref[...].astype(F32) * yb
    x1 = x_ref[...] + _dot(merged.astype(BF16), wo_ref[...])
    x1_ref[...] = x1
    hq = _rms(x1, g_ref[...])
    hq_ref[...] = hq
    qp = _dot(hq.astype(BF16), wq_ref[...])
    for h in range(PEER_HEADS):
        o = h * PEER_QDIM
        sc_ref[2 * h] = _dot_nt(k1_ref[h], qp[:, o:o + PEER_HALF], precision=HIGHEST)
        sc_ref[2 * h + 1] = _dot_nt(k2_ref[h], qp[:, o + PEER_HALF:o + PEER_QDIM], precision=HIGHEST)


def _merge(x, ys, att, ga, gb, t0, w_proj_ssm, w_proj_att, w_out, g_ffn, peer_w_q, keys1, keys2):
    B, nt = ys.shape[0], ys.shape[1]
    ts = min(MERGE_TS, nt)
    nblk = nt // ts
    i0 = t0 // ts
    tok = lambda d: pl.BlockSpec((None, ts, d), lambda b, i: (b, i, 0))
    row = lambda d: pl.BlockSpec((ts, d), lambda b, i: (b * nblk + i, 0))
    full = lambda shape: pl.BlockSpec(shape, lambda b, i: (0,) * len(shape))
    qd = PEER_HEADS * PEER_QDIM
    return pl.pallas_call(
        _merge_kernel,
        grid=(B, nblk),
        in_specs=[pl.BlockSpec((None, ts, D_MODEL), lambda b, i: (b, i0 + i, 0)),
                  tok(D_SSM), tok(D_ATT), tok(D_MODEL), tok(D_MODEL),
                  full((D_SSM, D_MODEL)), full((D_ATT, D_MODEL)), full((D_MODEL, D_MODEL)),
                  full((1, D_MODEL)), full((D_MODEL, qd)),
                  full((PEER_HEADS, PEER_KEYS, PEER_HALF)), full((PEER_HEADS, PEER_KEYS, PEER_HALF))],
        out_specs=[row(D_MODEL), row(D_MODEL),
                   pl.BlockSpec((2 * PEER_HEADS, PEER_KEYS, ts), lambda b, i: (0, 0, b * nblk + i))],
        out_shape=[jax.ShapeDtypeStruct((B * nt, D_MODEL), F32),
                   jax.ShapeDtypeStruct((B * nt, D_MODEL), F32),
                   jax.ShapeDtypeStruct((2 * PEER_HEADS, PEER_KEYS, B * nt), F32)],
        compiler_params=pltpu.CompilerParams(
            dimension_semantics=("parallel", "parallel"), vmem_limit_bytes=VMEM_LIMIT),
        name="merge",
    )(x, ys, att, ga, gb, w_proj_ssm, w_proj_att, w_out, g_ffn, peer_w_q, keys1, keys2)


TOPK_TS = 256


def _top_rows(s, row, k):
    vals, idxs = [], []
    for _ in range(k):
        m = jnp.max(s, axis=0, keepdims=True)
        idx = jnp.min(jnp.where(s == m, row, s.shape[0]), axis=0, keepdims=True)
        vals.append(m)
        idxs.append(idx)
        s = jnp.where(row == idx, -jnp.inf, s)
    return vals, idxs


def _stack_rows(rows, row16):
    acc = jnp.zeros(row16.shape, rows[0].dtype)
    for r, v in enumerate(rows):
        acc = jnp.where(row16 == r, v, acc)
    return acc


def _topk_kernel(sc_ref, idx_ref, gate_ref):
    ts = sc_ref.shape[-1]
    row = lax.broadcasted_iota(jnp.int32, (PEER_KEYS, ts), 0)
    row16 = lax.broadcasted_iota(jnp.int32, (PEER_TOPK, ts), 0)
    row8 = lax.broadcasted_iota(jnp.int32, (SUBLANES, ts), 0)
    counts = [PEER_TOPK // (i + 1) for i in range(PEER_TOPK)]
    heights = [PEER_TOPK if c > SUBLANES else SUBLANES for c in counts]
    n_cand = sum(heights)
    rowc = lax.broadcasted_iota(jnp.int32, (n_cand, ts), 0)
    gate_rows, eid_rows = [], []
    for h in range(PEER_HEADS):
        v1, i1 = _top_rows(sc_ref[2 * h], row, PEER_TOPK)
        v2, i2 = _top_rows(sc_ref[2 * h + 1], row, PEER_TOPK)
        v2s = _stack_rows(v2, row16)
        i2s = _stack_rows(i2, row16).astype(F32)
        cand, eid = [], []
        for i in range(PEER_TOPK):
            n = heights[i]
            cand.append(jnp.where((row16 if n == PEER_TOPK else row8) < counts[i],
                                  v1[i] + v2s[:n], -jnp.inf))
            eid.append(i1[i].astype(F32) * PEER_KEYS + i2s[:n])
        cand = jnp.concatenate(cand, axis=0)
        eid = jnp.concatenate(eid, axis=0)
        tops, picks = [], []
        for _ in range(PEER_TOPK):
            m = jnp.max(cand, axis=0, keepdims=True)
            pos = jnp.min(jnp.where(cand == m, rowc, n_cand), axis=0, keepdims=True)
            hit = rowc == pos
            picks.append(jnp.max(jnp.where(hit, eid, -1.0), axis=0, keepdims=True))
            tops.append(m)
            cand = jnp.where(hit, -jnp.inf, cand)
        top = _stack_rows(tops, row16)
        p = jnp.exp(top - jnp.max(top, axis=0, keepdims=True))
        gate_rows.append(p / jnp.sum(p, axis=0, keepdims=True))
        eid_rows.append(_stack_rows(picks, row16))
    gate_ref[...] = jnp.transpose(jnp.concatenate(gate_rows, axis=0))
    idx_ref[...] = jnp.transpose(jnp.concatenate(eid_rows, axis=0)).astype(jnp.int32)


def _topk(scores):
    T = scores.shape[-1]
    ts = min(TOPK_TS, T)
    return pl.pallas_call(
        _topk_kernel,
        grid=(T // ts,),
        in_specs=[pl.BlockSpec((2 * PEER_HEADS, PEER_KEYS, ts), lambda i: (0, 0, i))],
        out_specs=[pl.BlockSpec((ts, PEER_SEL), lambda i: (i, 0)),
                   pl.BlockSpec((ts, PEER_SEL), lambda i: (i, 0))],
        out_shape=[jax.ShapeDtypeStruct((T, PEER_SEL), jnp.int32),
                   jax.ShapeDtypeStruct((T, PEER_SEL), F32)],
        compiler_params=pltpu.CompilerParams(
            dimension_semantics=("parallel",), vmem_limit_bytes=VMEM_LIMIT),
        name="topk",
    )(scores)


SC_CORES = 2
SC_SUBCORES = 16
SC_LANES = 16
SC_WORKERS = SC_CORES * SC_SUBCORES
PEER_CH = SC_LANES
PEER_NCH = PEER_SEL // PEER_CH
PEER_WORDS = D_MODEL // 2
PEER_NWG = PEER_WORDS // SC_LANES
PEER_RING = 4
PEER_QUAD = 4
HI_MASK = -65536
GELU_C = 0.7978845608028654


def _gelu_tanh_via_exp(x):
    z = GELU_C * (x + 0.044715 * (x * x * x))
    t = 1.0 - 2.0 / (jnp.exp(2.0 * z) + 1.0)
    return 0.5 * x * (1.0 + t)


def _unpack_pair(w):
    lo = plsc.bitcast(lax.shift_left(w, 16), F32)
    hi = plsc.bitcast(lax.bitwise_and(w, HI_MASK), F32)
    return lo, hi


def _peer_sc_body(idx_hbm, gate_hbm, h_hbm, u_hbm, v_hbm, o_hbm,
                  idx_v, gate_v, h_v, ubuf, vbuf, pbuf, w_v, out_v, usem, vsem, msem, osem):
    n_tok = o_hbm.shape[0] // SC_WORKERS
    base = (lax.axis_index("s") * SC_CORES + lax.axis_index("c")) * n_tok
    lane = lax.iota(jnp.int32, SC_LANES)
    zero_rows = jnp.zeros((SC_LANES,), jnp.int32)

    def meta_copies(tok, s):
        return (pltpu.make_async_copy(idx_hbm.at[tok], idx_v.at[s], msem.at[s]),
                pltpu.make_async_copy(gate_hbm.at[tok], gate_v.at[s], msem.at[s]),
                pltpu.make_async_copy(h_hbm.at[tok], h_v.at[s], msem.at[s]))

    def gather(tab, buf, sem, slot, rows):
        return pltpu.make_async_copy(tab.at[rows], buf.at[slot], sem.at[slot])

    def start_ahead(tab, buf, sem, s, c):
        ahead = c + PEER_RING
        src = jnp.where(ahead < PEER_NCH, s, 1 - s)
        ch = ahead % PEER_NCH
        rows = idx_v[src, pl.ds(pl.multiple_of(ch * PEER_CH, PEER_CH), PEER_CH)]
        gather(tab, buf, sem, c % PEER_RING, rows).start()

    def token(t, carry):
        s = t % 2
        tok = base + t
        nxt = base + jnp.minimum(t + 1, n_tok - 1)
        for cp in meta_copies(nxt, 1 - s):
            cp.start()

        def u_chunk(c, carry):
            slot = c % PEER_RING
            gather(u_hbm, ubuf, usem, slot, zero_rows).wait()

            def dot_step(q, accs):
                cols = [pl.ds(pl.multiple_of((q * PEER_QUAD + j) * SC_LANES, SC_LANES), SC_LANES)
                        for j in range(PEER_QUAD)]
                hs = [plsc.bitcast(h_v[s, col], BF16) for col in cols]
                out = []
                for r in range(PEER_CH):
                    p = plsc.bitcast(ubuf[slot, r, cols[0]], BF16) * hs[0]
                    for j in range(1, PEER_QUAD):
                        p = p + plsc.bitcast(ubuf[slot, r, cols[j]], BF16) * hs[j]
                    lo, hi = _unpack_pair(plsc.bitcast(p, jnp.int32))
                    out.append(accs[r] + lo + hi)
                return tuple(out)

            accs = lax.fori_loop(0, PEER_NWG // PEER_QUAD, dot_step,
                                 tuple(jnp.zeros((SC_LANES,), F32) for _ in range(PEER_CH)))

            @pl.when(c == PEER_NCH - PEER_RING)
            def _():
                for cp in meta_copies(nxt, 1 - s):
                    cp.wait()

            start_ahead(u_hbm, ubuf, usem, s, c)
            for r in range(PEER_CH):
                pbuf[r, :] = accs[r]
            tot = jnp.zeros((SC_LANES,), F32)
            for j in range(SC_LANES):
                tot = tot + plsc.load_gather(pbuf, [lane, jnp.full((SC_LANES,), j, jnp.int32)])
            rows = pl.ds(pl.multiple_of(c * PEER_CH, PEER_CH), PEER_CH)
            w_v[rows] = gate_v[s, rows] * _gelu_tanh_via_exp(tot)
            return carry

        lax.fori_loop(0, PEER_NCH, u_chunk, 0)

        @pl.when(t >= 2)
        def _():
            pltpu.make_async_copy(out_v.at[s], o_hbm.at[tok], osem.at[s]).wait()

        def v_chunk(c, carry):
            slot = c % PEER_RING
            gather(v_hbm, vbuf, vsem, slot, zero_rows).wait()
            ws = []
            for r in range(PEER_CH):
                w = plsc.load_gather(w_v, [jnp.full((SC_LANES,), r, jnp.int32) + c * PEER_CH])
                ws.append(plsc.pack(w, w, format=plsc.PackFormat.INTERLEAVED,
                                    preferred_element_type=BF16))
            first = c == 0

            @plsc.parallel_loop(0, PEER_NWG, unroll=2)
            def acc_step(g):
                col = pl.ds(pl.multiple_of(g * SC_LANES, SC_LANES), SC_LANES)
                col_hi = pl.ds(pl.multiple_of(PEER_WORDS + g * SC_LANES, SC_LANES), SC_LANES)
                o_lo = jnp.where(first, 0.0, out_v[s, col])
                o_hi = jnp.where(first, 0.0, out_v[s, col_hi])
                for r0 in range(0, PEER_CH, PEER_QUAD):
                    p = plsc.bitcast(vbuf[slot, r0, col], BF16) * ws[r0]
                    for r in range(r0 + 1, r0 + PEER_QUAD):
                        p = p + plsc.bitcast(vbuf[slot, r, col], BF16) * ws[r]
                    lo, hi = _unpack_pair(plsc.bitcast(p, jnp.int32))
                    o_lo = o_lo + lo
                    o_hi = o_hi + hi
                out_v[s, col] = o_lo
                out_v[s, col_hi] = o_hi

            start_ahead(v_hbm, vbuf, vsem, s, c)
            return carry

        lax.fori_loop(0, PEER_NCH, v_chunk, 0)
        pltpu.make_async_copy(out_v.at[s], o_hbm.at[tok], osem.at[s]).start()
        return carry

    for cp in meta_copies(base, 0):
        cp.start()
    for cp in meta_copies(base, 0):
        cp.wait()
    for c in range(PEER_RING):
        rows = idx_v[0, pl.ds(c * PEER_CH, PEER_CH)]
        gather(u_hbm, ubuf, usem, c, rows).start()
        gather(v_hbm, vbuf, vsem, c, rows).start()
    lax.fori_loop(0, n_tok, token, 0)
    for c in range(PEER_RING):
        gather(u_hbm, ubuf, usem, c, zero_rows).wait()
        gather(v_hbm, vbuf, vsem, c, zero_rows).wait()
    for s in range(2):
        pltpu.make_async_copy(out_v.at[s], o_hbm.at[base], osem.at[s]).wait()


def _pack_bf16_pairs(tab):
    b = lax.bitcast_convert_type(tab.astype(BF16), jnp.uint16).astype(jnp.uint32)
    half = tab.shape[1] // 2
    return lax.bitcast_convert_type(b[:, :half] | (b[:, half:] << 16), jnp.int32)


def _peer(idx, hq, gates, u_words, v_words):
    T = hq.shape[0]
    assert T % (2 * SC_WORKERS) == 0
    mesh = plsc.VectorSubcoreMesh(core_axis_name="c", subcore_axis_name="s",
                                  num_cores=SC_CORES, num_subcores=SC_SUBCORES)
    return pl.kernel(
        _peer_sc_body,
        out_type=jax.ShapeDtypeStruct((T, D_MODEL), F32),
        mesh=mesh,
        scratch_types=[
            pltpu.VMEM((2, PEER_SEL), jnp.int32), pltpu.VMEM((2, PEER_SEL), F32),
            pltpu.VMEM((2, PEER_WORDS), jnp.int32),
            pltpu.VMEM((PEER_RING, PEER_CH, PEER_WORDS), jnp.int32),
            pltpu.VMEM((PEER_RING, PEER_CH, PEER_WORDS), jnp.int32),
            pltpu.VMEM((PEER_CH, SC_LANES), F32), pltpu.VMEM((PEER_SEL,), F32),
            pltpu.VMEM((2, D_MODEL), F32),
            pltpu.SemaphoreType.DMA((PEER_RING,)), pltpu.SemaphoreType.DMA((PEER_RING,)),
            pltpu.SemaphoreType.DMA((2,)), pltpu.SemaphoreType.DMA((2,)),
        ],
        compiler_params=pltpu.CompilerParams(needs_layout_passes=False),
        name="peer_sc",
    )(idx, gates, _pack_bf16_pairs(hq), u_words, v_words)


FINAL_TS = 256


def _final_kernel(x1_ref, pe_ref, p_ref, gp_ref, wg_ref, wp_ref, gf_ref, o_ref):
    x2 = x1_ref[...] + pe_ref[...]
    e = _dot(p_ref[...].astype(BF16), wp_ref[...])
    gate = jax.nn.sigmoid(_dot(_rms(x2, gp_ref[...]).astype(BF16), wg_ref[...]))
    o_ref[...] = _rms(x2 + gate * e, gf_ref[...])


def _final(x1, peer_out, p, t0, nt, g_ple, ple_w_gate, ple_w_proj, g_final):
    B = p.shape[0]
    ts = min(FINAL_TS, nt)
    nblk = nt // ts
    i0 = t0 // ts
    row = lambda d: pl.BlockSpec((ts, d), lambda b, i: (b * nblk + i, 0))
    full = lambda shape: pl.BlockSpec(shape, lambda b, i: (0,) * len(shape))
    return pl.pallas_call(
        _final_kernel,
        grid=(B, nblk),
        in_specs=[row(D_MODEL), row(D_MODEL),
                  pl.BlockSpec((None, ts, D_PLE), lambda b, i: (b, i0 + i, 0)),
                  full((1, D_MODEL)), full((D_MODEL, D_MODEL)), full((D_PLE, D_MODEL)),
                  full((1, D_MODEL))],
        out_specs=pl.BlockSpec((None, ts, D_MODEL), lambda b, i: (b, i, 0)),
        out_shape=jax.ShapeDtypeStruct((B, nt, D_MODEL), F32),
        compiler_params=pltpu.CompilerParams(
            dimension_semantics=("parallel", "parallel"), vmem_limit_bytes=VMEM_LIMIT),
        name="final",
    )(x1, peer_out, p, g_ple, ple_w_gate, ple_w_proj, g_final)


CHUNK_STEPS = (512, 512, 1024, 1024, 1024, 1024, 1024, 1024, 1024)


def kernel(x, p, positions, g_mix, w_in, ssm_log_dt, ssm_a_re, ssm_a_im, ssm_b_re, ssm_b_im,
           ssm_c_re, ssm_c_im, ssm_d, ssm_w_glu, w_proj_ssm, w_proj_att, w_out, g_ffn,
           peer_w_q, peer_keys1, peer_keys2, peer_u, peer_v, g_ple, ple_w_gate, ple_w_proj,
           g_final):
    B, S, _ = x.shape
    assert w_in.shape[0] == 1, "the final rmsnorm is fused into the single layer's last stage"
    steps = CHUNK_STEPS if sum(CHUNK_STEPS) == S else (S,)
    i = 0
    tables = _s5_tables(ssm_log_dt[i], ssm_a_re[i], ssm_a_im[i], ssm_b_re[i], ssm_b_im[i],
                        ssm_c_re[i], ssm_c_im[i])
    w_in_b, w_glu_b = w_in[i].astype(BF16), ssm_w_glu[i].astype(BF16)
    d_skip = ssm_d[i].reshape(1, D_SSM).astype(F32)
    merge_w = (w_proj_ssm[i].astype(BF16), w_proj_att[i].astype(BF16), w_out[i].astype(BF16),
               g_ffn[i].reshape(1, D_MODEL), peer_w_q[i].astype(BF16), peer_keys1[i], peer_keys2[i])
    final_w = (g_ple[i].reshape(1, D_MODEL), ple_w_gate[i].astype(BF16),
               ple_w_proj[i].astype(BF16), g_final.reshape(1, D_MODEL))
    u_words = _pack_bf16_pairs(peer_u[i])
    v_words = _pack_bf16_pairs(peer_v[i])
    k_all = jnp.zeros((B, S, D_ATT), BF16)
    v_all = jnp.zeros((B, S, D_ATT), BF16)
    carry = jnp.zeros((2, SUBLANES, D_STATE), F32)
    outs = []
    t0 = 0
    after = (carry, carry)
    for nt in steps:
        u_sb, q, k, v, ga, gb = _in_proj(x, positions, g_mix[i], w_in_b, t0, nt, after)
        k_all = lax.dynamic_update_slice(k_all, k, (0, t0, 0))
        v_all = lax.dynamic_update_slice(v_all, v, (0, t0, 0))
        ys, carry = _s5(u_sb, carry, tables, d_skip, w_glu_b, B)
        att = _moba(q, k_all, v_all, t0 // MOBA_BLOCK)
        x1, hq, scores = _merge(x, ys, att, ga, gb, t0, *merge_w)
        idx, gates = _topk(scores)
        after = (gates, outs[-1] if outs else carry)
        peer_out = _peer(idx, hq, gates, u_words, v_words)
        outs.append(_final(x1, peer_out, p[i], t0, nt, *final_w))
        t0 += nt
    return jnp.concatenate(outs, axis=1)
```

```python
import functools
import math

import jax
import jax.numpy as jnp
from jax import lax
from jax.experimental import pallas as pl
from jax.experimental.pallas import tpu as pltpu
from jax.experimental.pallas import tpu_sc as plsc

F32 = jnp.float32
BF16 = jnp.bfloat16

D_MODEL = 1024
D_SSM = 512
SSM_GROUP = 16
SSM_GROUPS = 32
SSM_STATE = 64
D_STATE = SSM_GROUPS * SSM_STATE
N_HEADS = 8
HEAD_DIM = 64
D_ATT = 512
ROT_DIM = 16
ROPE_THETA = 500000.0
MOBA_BLOCK = 256
MOBA_TOPK = 3
PEER_HEADS = 8
PEER_KEYS = 128
PEER_QDIM = 256
PEER_HALF = 128
PEER_TOPK = 16
PEER_SEL = PEER_HEADS * PEER_TOPK
D_PLE = 256
EPS = 1e-6
NEG = -1e30
LANES = 128
SUBLANES = 8
VMEM_LIMIT = 48 * 1024 * 1024
HIGHEST = lax.Precision.HIGHEST


def _rms(x, g):
    return x * lax.rsqrt(jnp.mean(x * x, axis=-1, keepdims=True) + EPS) * g


def _dot(a, b):
    return jnp.dot(a, b, preferred_element_type=F32)


def _dot_nt(a, b, precision=None):
    return lax.dot_general(a, b, (((1,), (1,)), ((), ())), precision=precision,
                           preferred_element_type=F32)


IN_TS = 512


def _in_proj_kernel(x_ref, pos_ref, g_ref, w_ref, invf_ref, after_a, after_b,
                    u_ref, q_ref, k_ref, v_ref, ga_ref, gb_ref):
    del after_a, after_b
    h = _rms(x_ref[...], g_ref[...]).astype(BF16)

    def proj(lo, hi):
        return _dot(h, w_ref[:, lo:hi])

    u_ref[...] = proj(0, D_SSM).astype(BF16)
    ang = pos_ref[...].astype(F32) * invf_ref[...]
    cos = jnp.cos(ang)
    sin = jnp.sin(ang)
    lane = lax.broadcasted_iota(jnp.int32, (1, LANES), 1) % HEAD_DIM
    half = ROT_DIM // 2
    sin_hi = jnp.where((lane >= half) & (lane < ROT_DIM), sin, 0.0)
    sin_lo = jnp.where(lane < half, -sin, 0.0)
    reps = D_ATT // LANES
    cos4 = jnp.concatenate([cos] * reps, axis=1)
    sin_hi4 = jnp.concatenate([sin_hi] * reps, axis=1)
    sin_lo4 = jnp.concatenate([sin_lo] * reps, axis=1)

    def rope(t):
        return (t * cos4 + pltpu.roll(t, half, 1) * sin_hi4
                + pltpu.roll(t, D_ATT - half, 1) * sin_lo4)

    q = rope(proj(D_SSM, D_SSM + D_ATT))
    q_ref[...] = (q * (HEAD_DIM ** -0.5)).astype(BF16)
    k_ref[...] = rope(proj(D_SSM + D_ATT, D_SSM + 2 * D_ATT)).astype(BF16)
    v_ref[...] = proj(D_SSM + 2 * D_ATT, D_SSM + 3 * D_ATT).astype(BF16)
    o = D_SSM + 3 * D_ATT
    ga_ref[...] = jax.nn.sigmoid(proj(o, o + D_MODEL)).astype(BF16)
    gb_ref[...] = jax.nn.sigmoid(proj(o + D_MODEL, o + 2 * D_MODEL)).astype(BF16)


def _in_proj(x, positions, g_mix, w_in, t0, nt, after):
    B, S, _ = x.shape
    ts = min(IN_TS, nt)
    assert nt % ts == 0 and t0 % ts == 0
    i0 = t0 // ts
    inv_freq = ROPE_THETA ** (-jnp.arange(0, ROT_DIM, 2, dtype=F32) / ROT_DIM)
    lane = jnp.arange(LANES) % HEAD_DIM
    invf = jnp.where(lane < ROT_DIM, inv_freq[lane % (ROT_DIM // 2)], 0.0).reshape(1, LANES)
    d_in = w_in.shape[1]
    src = lambda d: pl.BlockSpec((None, ts, d), lambda b, i: (b, i0 + i, 0))
    tok = lambda d: pl.BlockSpec((None, ts, d), lambda b, i: (b, i, 0))
    full = lambda shape: pl.BlockSpec(shape, lambda b, i: (0,) * len(shape))
    return pl.pallas_call(
        _in_proj_kernel,
        grid=(B, nt // ts),
        in_specs=[src(D_MODEL), src(1), full((1, D_MODEL)), full((D_MODEL, d_in)), full((1, LANES)),
                  pl.BlockSpec(memory_space=pl.ANY), pl.BlockSpec(memory_space=pl.ANY)],
        out_specs=[pl.BlockSpec((ts, D_SSM), lambda b, i: (i, b)),
                   tok(D_ATT), tok(D_ATT), tok(D_ATT), tok(D_MODEL), tok(D_MODEL)],
        out_shape=[jax.ShapeDtypeStruct((nt, B * D_SSM), BF16),
                   jax.ShapeDtypeStruct((B, nt, D_ATT), BF16),
                   jax.ShapeDtypeStruct((B, nt, D_ATT), BF16),
                   jax.ShapeDtypeStruct((B, nt, D_ATT), BF16),
                   jax.ShapeDtypeStruct((B, nt, D_MODEL), BF16),
                   jax.ShapeDtypeStruct((B, nt, D_MODEL), BF16)],
        compiler_params=pltpu.CompilerParams(
            dimension_semantics=("parallel", "parallel"), vmem_limit_bytes=VMEM_LIMIT),
        name="in_proj",
    )(x, positions.reshape(B, S, 1), g_mix.reshape(1, D_MODEL), w_in, invf, *after)


S5_TS = 128
S5_BATCH = 4
S5_COLS = 512


def _s5_kernel(u_ref, c0_ref, bre_ref, bim_ref, a1r_ref, a1i_ref, pr_ref, pi_ref,
               cre_ref, cim_ref, d_ref, wglu_ref, y_ref, c1_ref,
               xr, xi, cr, ci, ysc):
    rows = xr.shape[0]
    ts = rows // S5_BATCH

    @pl.when(pl.program_id(0) == 0)
    def _():
        cr[...] = c0_ref[0]
        ci[...] = c0_ref[1]

    u = u_ref[...]
    xr[...] = _dot(u, bre_ref[...])
    xi[...] = _dot(u, bim_ref[...])

    hi_rows = lax.broadcasted_iota(jnp.int32, (SUBLANES, S5_COLS), 0) >= S5_BATCH
    for cb in range(D_STATE // S5_COLS):
        sl = slice(cb * S5_COLS, (cb + 1) * S5_COLS)
        a_r, a_i = a1r_ref[:, sl], a1i_ref[:, sl]
        p_r, p_i = pr_ref[:, sl], pi_ref[:, sl]

        def body(t, carry):
            c_r, c_i = carry
            r0 = pl.multiple_of(t * SUBLANES, SUBLANES)
            x_r = xr[pl.ds(r0, SUBLANES), sl]
            x_i = xi[pl.ds(r0, SUBLANES), sl]
            s_r = pltpu.roll(x_r, S5_BATCH, 0)
            s_i = pltpu.roll(x_i, S5_BATCH, 0)
            h_r = x_r + (a_r * s_r - a_i * s_i) + (p_r * c_r - p_i * c_i)
            h_i = x_i + (a_r * s_i + a_i * s_r) + (p_r * c_i + p_i * c_r)
            xr[pl.ds(r0, SUBLANES), sl] = h_r
            xi[pl.ds(r0, SUBLANES), sl] = h_i
            n_r = jnp.where(hi_rows, h_r, pltpu.roll(h_r, S5_BATCH, 0))
            n_i = jnp.where(hi_rows, h_i, pltpu.roll(h_i, S5_BATCH, 0))
            return n_r, n_i

        c_r, c_i = lax.fori_loop(0, rows // SUBLANES, body, (cr[:, sl], ci[:, sl]), unroll=2)
        cr[:, sl] = c_r
        ci[:, sl] = c_i

    y = (_dot(xr[...].astype(BF16), cre_ref[...]) - _dot(xi[...].astype(BF16), cim_ref[...])
         + d_ref[...] * u.astype(F32))
    y = jax.nn.gelu(y)
    y = y * jax.nn.sigmoid(_dot(y.astype(BF16), wglu_ref[...]))
    for c in range(D_SSM // LANES):
        ysc[c] = y[:, c * LANES:(c + 1) * LANES]
    for b in range(S5_BATCH):
        for c in range(D_SSM // LANES):
            y_ref[b, :, c * LANES:(c + 1) * LANES] = (
                ysc[c, pl.ds(b, ts, stride=S5_BATCH), :].astype(BF16))

    @pl.when(pl.program_id(0) == pl.num_programs(0) - 1)
    def _():
        c1_ref[0] = cr[...]
        c1_ref[1] = ci[...]


def _s5_tables(log_dt, a_re, a_im, b_re, b_im, c_re, c_im):
    dt = jnp.exp(log_dt.astype(F32))[:, None]
    ar, ai = a_re.astype(F32), a_im.astype(F32)
    mag = jnp.exp(dt * ar)
    abar_re, abar_im = mag * jnp.cos(dt * ai), mag * jnp.sin(dt * ai)
    den = ar * ar + ai * ai
    nr, ni = abar_re - 1.0, abar_im
    f_re = (nr * ar + ni * ai) / den
    f_im = (ni * ar - nr * ai) / den
    br, bi = b_re.astype(F32), b_im.astype(F32)
    bb_re = f_re[..., None] * br - f_im[..., None] * bi
    bb_im = f_re[..., None] * bi + f_im[..., None] * br
    eye = jnp.eye(SSM_GROUPS, dtype=F32)

    def in_blockdiag(bb):
        return jnp.einsum('gnc,gh->gchn', bb, eye).reshape(D_SSM, D_STATE)

    def out_blockdiag(c):
        return jnp.einsum('gcn,gh->gnhc', c.astype(F32), eye).reshape(D_STATE, D_SSM)

    a_r = abar_re.reshape(1, D_STATE)
    a_i = abar_im.reshape(1, D_STATE)
    a2_r = a_r * a_r - a_i * a_i
    a2_i = 2.0 * a_r * a_i
    hi = (jnp.arange(SUBLANES) >= S5_BATCH)[:, None]
    a1r = jnp.where(hi, a_r, 0.0)
    a1i = jnp.where(hi, a_i, 0.0)
    p_r = jnp.where(hi, a2_r, a_r)
    p_i = jnp.where(hi, a2_i, a_i)
    return (in_blockdiag(bb_re).astype(BF16), in_blockdiag(bb_im).astype(BF16),
            a1r, a1i, p_r, p_i,
            out_blockdiag(c_re).astype(BF16), out_blockdiag(c_im).astype(BF16))


def _s5(u_sb, carry, tables, d_skip, w_glu, B):
    assert B == S5_BATCH
    nt = u_sb.shape[0]
    ts = min(S5_TS, nt)
    rows = ts * B
    bre, bim, a1r, a1i, p_r, p_i, cre, cim = tables
    full = lambda shape: pl.BlockSpec(shape, lambda i: (0,) * len(shape))
    return pl.pallas_call(
        _s5_kernel,
        grid=(nt // ts,),
        in_specs=[pl.BlockSpec((rows, D_SSM), lambda i: (i, 0)),
                  full((2, SUBLANES, D_STATE)),
                  full((D_SSM, D_STATE)), full((D_SSM, D_STATE)),
                  full((SUBLANES, D_STATE)), full((SUBLANES, D_STATE)),
                  full((SUBLANES, D_STATE)), full((SUBLANES, D_STATE)),
                  full((D_STATE, D_SSM)), full((D_STATE, D_SSM)),
                  full((1, D_SSM)), full((D_SSM, D_SSM))],
        out_specs=[pl.BlockSpec((B, ts, D_SSM), lambda i: (0, i, 0)),
                   full((2, SUBLANES, D_STATE))],
        out_shape=[jax.ShapeDtypeStruct((B, nt, D_SSM), BF16),
                   jax.ShapeDtypeStruct((2, SUBLANES, D_STATE), F32)],
        scratch_shapes=[pltpu.VMEM((rows, D_STATE), F32), pltpu.VMEM((rows, D_STATE), F32),
                        pltpu.VMEM((SUBLANES, D_STATE), F32), pltpu.VMEM((SUBLANES, D_STATE), F32),
                        pltpu.VMEM((D_SSM // LANES, rows, LANES), F32)],
        compiler_params=pltpu.CompilerParams(
            dimension_semantics=("arbitrary",), vmem_limit_bytes=VMEM_LIMIT),
        name="s5",
    )(u_sb.reshape(nt * B, D_SSM), carry, bre, bim, a1r, a1i, p_r, p_i, cre, cim, d_skip, w_glu)


MOBA_PAIR = 2 * MOBA_BLOCK


def _moba_kernel(q0, q_ref, k_ref, v_ref, o_ref, kmean, kaug_a, kaug_b, vaug_a, vaug_b, m_s, acc_s,
                 s_buf):
    qi = pl.program_id(2) + q0
    nb = k_ref.shape[0] // MOBA_BLOCK
    nbp = kmean.shape[0]
    lane = lax.broadcasted_iota(jnp.int32, (1, LANES), 1)
    head_a = lane < HEAD_DIM

    @pl.when(pl.program_id(2) == 0)
    def _():
        kmean[...] = jnp.zeros_like(kmean)
        for j in range(nb):
            rows = pl.ds(j * MOBA_BLOCK, MOBA_BLOCK)
            kj = k_ref[rows, :].astype(F32)
            vj = v_ref[rows, :].astype(F32)
            kmean[j:j + 1, :] = jnp.sum(kj, axis=0, keepdims=True) * (1.0 / MOBA_BLOCK)
            kaug_a[rows, :] = jnp.where(head_a, kj, jnp.where(lane - HEAD_DIM == j, 1.0, 0.0)).astype(BF16)
            kaug_b[rows, :] = jnp.where(head_a, jnp.where(lane == j, 1.0, 0.0), kj).astype(BF16)
            vaug_a[rows, :] = jnp.where(head_a, vj, 1.0).astype(BF16)
            vaug_b[rows, :] = jnp.where(head_a, 1.0, vj).astype(BF16)

    qf = q_ref[...].astype(F32)
    blk_row = lax.broadcasted_iota(jnp.int32, (nbp, MOBA_BLOCK), 0)
    q_augs = []
    for is_a in (True, False):
        mine = head_a if is_a else jnp.logical_not(head_a)
        q_own = jnp.where(mine, qf, 0.0)
        g = _dot_nt(kmean[...], q_own, precision=HIGHEST)
        g = jnp.where(blk_row < qi, g, NEG)
        sel = jnp.zeros(g.shape, F32)
        for _ in range(MOBA_TOPK):
            m = jnp.max(g, axis=0, keepdims=True)
            idx = jnp.min(jnp.where(g == m, blk_row, nbp), axis=0, keepdims=True)
            hit = blk_row == idx
            sel = jnp.where(hit, jnp.where(idx < qi, 1.0, 0.0), sel)
            g = jnp.where(hit, -jnp.inf, g)
        bias_t = jnp.where(sel > 0.0, 0.0, jnp.where(blk_row == qi, 0.0, NEG))
        bias_t = jnp.concatenate([bias_t, jnp.full((LANES - nbp, MOBA_BLOCK), NEG, F32)], axis=0)
        bias = jnp.transpose(bias_t)
        if is_a:
            bias = pltpu.roll(bias, HEAD_DIM, 1)
        q_augs.append(jnp.where(mine, qf, bias).astype(BF16))

    m_s[...] = jnp.full(m_s.shape, -jnp.inf, F32)
    acc_s[...] = jnp.zeros_like(acc_s)
    qpos = qi * MOBA_BLOCK + lax.broadcasted_iota(jnp.int32, (MOBA_BLOCK, MOBA_PAIR), 0)
    col = lax.broadcasted_iota(jnp.int32, (MOBA_BLOCK, MOBA_PAIR), 1)

    def kv_rows(jj):
        return pl.ds(pl.multiple_of(jj * MOBA_PAIR, MOBA_PAIR), MOBA_PAIR)

    def scores(jj, slot):
        for hd, kaug in enumerate((kaug_a, kaug_b)):
            s_buf[slot, hd] = _dot_nt(q_augs[hd], kaug[kv_rows(jj), :])

    def softmax_pv(jj, slot, causal):
        for hd, vaug in enumerate((vaug_a, vaug_b)):
            s = s_buf[slot, hd]
            if causal:
                s = jnp.where(jj * MOBA_PAIR + col <= qpos, s, NEG)
            m_old = m_s[hd]
            m_new = jnp.maximum(m_old, jnp.max(s, axis=-1, keepdims=True))
            alpha = jnp.exp(m_old - m_new)
            p = jnp.exp(s - m_new)
            m_s[hd] = m_new
            acc_s[hd] = alpha * acc_s[hd] + _dot(p.astype(BF16), vaug[kv_rows(jj), :])

    last = qi // 2
    scores(0, 0)

    def body(k, _):
        scores(2 * k + 1, 1)
        softmax_pv(2 * k, 0, False)
        scores(2 * k + 2, 0)
        softmax_pv(2 * k + 1, 1, False)
        return 0

    lax.fori_loop(0, last // 2, body, 0)

    @pl.when(last % 2 == 0)
    def _():
        softmax_pv(last, 0, True)

    @pl.when(last % 2 == 1)
    def _():
        scores(last, 1)
        softmax_pv(last - 1, 0, False)
        softmax_pv(last, 1, True)
    acc_a, acc_b = acc_s[0], acc_s[1]
    o_ref[...] = jnp.where(head_a, acc_a / pltpu.roll(acc_a, HEAD_DIM, 1),
                           acc_b / pltpu.roll(acc_b, HEAD_DIM, 1)).astype(BF16)


def _moba(q, k, v, q0):
    B = q.shape[0]
    nq = q.shape[1] // MOBA_BLOCK
    skv = (q0 + nq) * MOBA_BLOCK
    nb = skv // MOBA_BLOCK
    assert nb <= HEAD_DIM and nb % 2 == 0 and skv <= k.shape[1]
    nbp = -(-nb // SUBLANES) * SUBLANES
    blk = pl.BlockSpec((None, MOBA_BLOCK, LANES), lambda b, h, i: (b, i, h))
    seq = pl.BlockSpec((None, skv, LANES), lambda b, h, i: (b, 0, h))
    return pl.pallas_call(
        functools.partial(_moba_kernel, q0),
        grid=(B, D_ATT // LANES, nq),
        in_specs=[blk, seq, seq],
        out_specs=blk,
        out_shape=jax.ShapeDtypeStruct(q.shape, BF16),
        scratch_shapes=[pltpu.VMEM((nbp, LANES), F32),
                        pltpu.VMEM((skv, LANES), BF16), pltpu.VMEM((skv, LANES), BF16),
                        pltpu.VMEM((skv, LANES), BF16), pltpu.VMEM((skv, LANES), BF16),
                        pltpu.VMEM((2, MOBA_BLOCK, 1), F32),
                        pltpu.VMEM((2, MOBA_BLOCK, LANES), F32),
                        pltpu.VMEM((2, 2, MOBA_BLOCK, MOBA_PAIR), F32)],
        compiler_params=pltpu.CompilerParams(
            dimension_semantics=("parallel", "parallel", "arbitrary"), vmem_limit_bytes=VMEM_LIMIT),
        name="moba",
    )(q, k, v)


MERGE_TS = 256


def _merge_kernel(x_ref, ys_ref, at_ref, ga_ref, gb_ref, wa_ref, wb_ref, wo_ref, g_ref,
                  wq_ref, k1_ref, k2_ref, x1_ref, hq_ref, sc_ref):
    ya = _dot(ys_ref[...], wa_ref[...])
    yb = _dot(at_ref[...], wb_ref[...])
    merged = ga_ref[...].astype(F32) * ya + gb_ref[...].astype(F32) * yb
    x1 = x_ref[...] + _dot(merged.astype(BF16), wo_ref[...])
    x1_ref[...] = x1
    hq = _rms(x1, g_ref[...])
    hq_ref[...] = hq
    qp = _dot(hq.astype(BF16), wq_ref[...])
    for h in range(PEER_HEADS):
        o = h * PEER_QDIM
        sc_ref[2 * h] = _dot_nt(k1_ref[h], qp[:, o:o + PEER_HALF], precision=HIGHEST)
        sc_ref[2 * h + 1] = _dot_nt(k2_ref[h], qp[:, o + PEER_HALF:o + PEER_QDIM], precision=HIGHEST)


def _merge(x, ys, att, ga, gb, t0, w_proj_ssm, w_proj_att, w_out, g_ffn, peer_w_q, keys1, keys2):
    B, nt = ys.shape[0], ys.shape[1]
    ts = min(MERGE_TS, nt)
    nblk = nt // ts
    i0 = t0 // ts
    tok = lambda d: pl.BlockSpec((None, ts, d), lambda b, i: (b, i, 0))
    row = lambda d: pl.BlockSpec((ts, d), lambda b, i: (b * nblk + i, 0))
    full = lambda shape: pl.BlockSpec(shape, lambda b, i: (0,) * len(shape))
    qd = PEER_HEADS * PEER_QDIM
    return pl.pallas_call(
        _merge_kernel,
        grid=(B, nblk),
        in_specs=[pl.BlockSpec((None, ts, D_MODEL), lambda b, i: (b, i0 + i, 0)),
                  tok(D_SSM), tok(D_ATT), tok(D_MODEL), tok(D_MODEL),
                  full((D_SSM, D_MODEL)), full((D_ATT, D_MODEL)), full((D_MODEL, D_MODEL)),
                  full((1, D_MODEL)), full((D_MODEL, qd)),
                  full((PEER_HEADS, PEER_KEYS, PEER_HALF)), full((PEER_HEADS, PEER_KEYS, PEER_HALF))],
        out_specs=[row(D_MODEL), row(D_MODEL),
                   pl.BlockSpec((2 * PEER_HEADS, PEER_KEYS, ts), lambda b, i: (0, 0, b * nblk + i))],
        out_shape=[jax.ShapeDtypeStruct((B * nt, D_MODEL), F32),
                   jax.ShapeDtypeStruct((B * nt, D_MODEL), F32),
                   jax.ShapeDtypeStruct((2 * PEER_HEADS, PEER_KEYS, B * nt), F32)],
        compiler_params=pltpu.CompilerParams(
            dimension_semantics=("parallel", "parallel"), vmem_limit_bytes=VMEM_LIMIT),
        name="merge",
    )(x, ys, att, ga, gb, w_proj_ssm, w_proj_att, w_out, g_ffn, peer_w_q, keys1, keys2)


TOPK_TS = 256


def _top_rows(s, row, k):
    vals, idxs = [], []
    for _ in range(k):
        m = jnp.max(s, axis=0, keepdims=True)
        idx = jnp.min(jnp.where(s == m, row, s.shape[0]), axis=0, keepdims=True)
        vals.append(m)
        idxs.append(idx)
        s = jnp.where(row == idx, -jnp.inf, s)
    return vals, idxs


def _stack_rows(rows, row16):
    acc = jnp.zeros(row16.shape, rows[0].dtype)
    for r, v in enumerate(rows):
        acc = jnp.where(row16 == r, v, acc)
    return acc


def _topk_kernel(sc_ref, idx_ref, gate_ref):
    ts = sc_ref.shape[-1]
    row = lax.broadcasted_iota(jnp.int32, (PEER_KEYS, ts), 0)
    row16 = lax.broadcasted_iota(jnp.int32, (PEER_TOPK, ts), 0)
    row8 = lax.broadcasted_iota(jnp.int32, (SUBLANES, ts), 0)
    counts = [PEER_TOPK // (i + 1) for i in range(PEER_TOPK)]
    heights = [PEER_TOPK if c > SUBLANES else SUBLANES for c in counts]
    n_cand = sum(heights)
    rowc = lax.broadcasted_iota(jnp.int32, (n_cand, ts), 0)
    gate_rows, eid_rows = [], []
    for h in range(PEER_HEADS):
        v1, i1 = _top_rows(sc_ref[2 * h], row, PEER_TOPK)
        v2, i2 = _top_rows(sc_ref[2 * h + 1], row, PEER_TOPK)
        v2s = _stack_rows(v2, row16)
        i2s = _stack_rows(i2, row16).astype(F32)
        cand, eid = [], []
        for i in range(PEER_TOPK):
            n = heights[i]
            cand.append(jnp.where((row16 if n == PEER_TOPK else row8) < counts[i],
                                  v1[i] + v2s[:n], -jnp.inf))
            eid.append(i1[i].astype(F32) * PEER_KEYS + i2s[:n])
        cand = jnp.concatenate(cand, axis=0)
        eid = jnp.concatenate(eid, axis=0)
        tops, picks = [], []
        for _ in range(PEER_TOPK):
            m = jnp.max(cand, axis=0, keepdims=True)
            pos = jnp.min(jnp.where(cand == m, rowc, n_cand), axis=0, keepdims=True)
            hit = rowc == pos
            picks.append(jnp.max(jnp.where(hit, eid, -1.0), axis=0, keepdims=True))
            tops.append(m)
            cand = jnp.where(hit, -jnp.inf, cand)
        top = _stack_rows(tops, row16)
        p = jnp.exp(top - jnp.max(top, axis=0, keepdims=True))
        gate_rows.append(p / jnp.sum(p, axis=0, keepdims=True))
        eid_rows.append(_stack_rows(picks, row16))
    gate_ref[...] = jnp.transpose(jnp.concatenate(gate_rows, axis=0))
    idx_ref[...] = jnp.transpose(jnp.concatenate(eid_rows, axis=0)).astype(jnp.int32)


def _topk(scores):
    T = scores.shape[-1]
    ts = min(TOPK_TS, T)
    return pl.pallas_call(
        _topk_kernel,
        grid=(T // ts,),
        in_specs=[pl.BlockSpec((2 * PEER_HEADS, PEER_KEYS, ts), lambda i: (0, 0, i))],
        out_specs=[pl.BlockSpec((ts, PEER_SEL), lambda i: (i, 0)),
                   pl.BlockSpec((ts, PEER_SEL), lambda i: (i, 0))],
        out_shape=[jax.ShapeDtypeStruct((T, PEER_SEL), jnp.int32),
                   jax.ShapeDtypeStruct((T, PEER_SEL), F32)],
        compiler_params=pltpu.CompilerParams(
            dimension_semantics=("parallel",), vmem_limit_bytes=VMEM_LIMIT),
        name="topk",
    )(scores)


SC_CORES = 2
SC_SUBCORES = 16
SC_LANES = 16
SC_WORKERS = SC_CORES * SC_SUBCORES
PEER_CH = SC_LANES
PEER_NCH = PEER_SEL // PEER_CH
PEER_WORDS = D_MODEL // 2
PEER_NWG = PEER_WORDS // SC_LANES
PEER_RING = 4
PEER_QUAD = 4
HI_MASK = -65536
GELU_C = 0.7978845608028654


def _gelu_tanh_via_exp(x):
    z = GELU_C * (x + 0.044715 * (x * x * x))
    t = 1.0 - 2.0 / (jnp.exp(2.0 * z) + 1.0)
    return 0.5 * x * (1.0 + t)


def _unpack_pair(w):
    lo = plsc.bitcast(lax.shift_left(w, 16), F32)
    hi = plsc.bitcast(lax.bitwise_and(w, HI_MASK), F32)
    return lo, hi


def _peer_sc_body(idx_hbm, gate_hbm, h_hbm, u_hbm, v_hbm, o_hbm,
                  idx_v, gate_v, h_v, ubuf, vbuf, pbuf, w_v, out_v, usem, vsem, msem, osem):
    n_tok = o_hbm.shape[0] // SC_WORKERS
    base = (lax.axis_index("s") * SC_CORES + lax.axis_index("c")) * n_tok
    lane = lax.iota(jnp.int32, SC_LANES)
    zero_rows = jnp.zeros((SC_LANES,), jnp.int32)

    def meta_copies(tok, s):
        return (pltpu.make_async_copy(idx_hbm.at[tok], idx_v.at[s], msem.at[s]),
                pltpu.make_async_copy(gate_hbm.at[tok], gate_v.at[s], msem.at[s]),
                pltpu.make_async_copy(h_hbm.at[tok], h_v.at[s], msem.at[s]))

    def gather(tab, buf, sem, slot, rows):
        return pltpu.make_async_copy(tab.at[rows], buf.at[slot], sem.at[slot])

    def start_ahead(tab, buf, sem, s, c):
        ahead = c + PEER_RING
        src = jnp.where(ahead < PEER_NCH, s, 1 - s)
        ch = ahead % PEER_NCH
        rows = idx_v[src, pl.ds(pl.multiple_of(ch * PEER_CH, PEER_CH), PEER_CH)]
        gather(tab, buf, sem, c % PEER_RING, rows).start()

    def token(t, carry):
        s = t % 2
        tok = base + t
        nxt = base + jnp.minimum(t + 1, n_tok - 1)
        for cp in meta_copies(nxt, 1 - s):
            cp.start()

        def u_chunk(c, carry):
            slot = c % PEER_RING
            gather(u_hbm, ubuf, usem, slot, zero_rows).wait()

            def dot_step(q, accs):
                cols = [pl.ds(pl.multiple_of((q * PEER_QUAD + j) * SC_LANES, SC_LANES), SC_LANES)
                        for j in range(PEER_QUAD)]
                hs = [plsc.bitcast(h_v[s, col], BF16) for col in cols]
                out = []
                for r in range(PEER_CH):
                    p = plsc.bitcast(ubuf[slot, r, cols[0]], BF16) * hs[0]
                    for j in range(1, PEER_QUAD):
                        p = p + plsc.bitcast(ubuf[slot, r, cols[j]], BF16) * hs[j]
                    lo, hi = _unpack_pair(plsc.bitcast(p, jnp.int32))
                    out.append(accs[r] + lo + hi)
                return tuple(out)

            accs = lax.fori_loop(0, PEER_NWG // PEER_QUAD, dot_step,
                                 tuple(jnp.zeros((SC_LANES,), F32) for _ in range(PEER_CH)))

            @pl.when(c == PEER_NCH - PEER_RING)
            def _():
                for cp in meta_copies(nxt, 1 - s):
                    cp.wait()

            start_ahead(u_hbm, ubuf, usem, s, c)
            for r in range(PEER_CH):
                pbuf[r, :] = accs[r]
            tot = jnp.zeros((SC_LANES,), F32)
            for j in range(SC_LANES):
                tot = tot + plsc.load_gather(pbuf, [lane, jnp.full((SC_LANES,), j, jnp.int32)])
            rows = pl.ds(pl.multiple_of(c * PEER_CH, PEER_CH), PEER_CH)
            w_v[rows] = gate_v[s, rows] * _gelu_tanh_via_exp(tot)
            return carry

        lax.fori_loop(0, PEER_NCH, u_chunk, 0)

        @pl.when(t >= 2)
        def _():
            pltpu.make_async_copy(out_v.at[s], o_hbm.at[tok], osem.at[s]).wait()

        def v_chunk(c, carry):
            slot = c % PEER_RING
            gather(v_hbm, vbuf, vsem, slot, zero_rows).wait()
            ws = []
            for r in range(PEER_CH):
                w = plsc.load_gather(w_v, [jnp.full((SC_LANES,), r, jnp.int32) + c * PEER_CH])
                ws.append(plsc.pack(w, w, format=plsc.PackFormat.INTERLEAVED,
                                    preferred_element_type=BF16))
            first = c == 0

            @plsc.parallel_loop(0, PEER_NWG, unroll=2)
            def acc_step(g):
                col = pl.ds(pl.multiple_of(g * SC_LANES, SC_LANES), SC_LANES)
                col_hi = pl.ds(pl.multiple_of(PEER_WORDS + g * SC_LANES, SC_LANES), SC_LANES)
                o_lo = jnp.where(first, 0.0, out_v[s, col])
                o_hi = jnp.where(first, 0.0, out_v[s, col_hi])
                for r0 in range(0, PEER_CH, PEER_QUAD):
                    p = plsc.bitcast(vbuf[slot, r0, col], BF16) * ws[r0]
                    for r in range(r0 + 1, r0 + PEER_QUAD):
                        p = p + plsc.bitcast(vbuf[slot, r, col], BF16) * ws[r]
                    lo, hi = _unpack_pair(plsc.bitcast(p, jnp.int32))
                    o_lo = o_lo + lo
                    o_hi = o_hi + hi
                out_v[s, col] = o_lo
                out_v[s, col_hi] = o_hi

            start_ahead(v_hbm, vbuf, vsem, s, c)
            return carry

        lax.fori_loop(0, PEER_NCH, v_chunk, 0)
        pltpu.make_async_copy(out_v.at[s], o_hbm.at[tok], osem.at[s]).start()
        return carry

    for cp in meta_copies(base, 0):
        cp.start()
    for cp in meta_copies(base, 0):
        cp.wait()
    for c in range(PEER_RING):
        rows = idx_v[0, pl.ds(c * PEER_CH, PEER_CH)]
        gather(u_hbm, ubuf, usem, c, rows).start()
        gather(v_hbm, vbuf, vsem, c, rows).start()
    lax.fori_loop(0, n_tok, token, 0)
    for c in range(PEER_RING):
        gather(u_hbm, ubuf, usem, c, zero_rows).wait()
        gather(v_hbm, vbuf, vsem, c, zero_rows).wait()
    for s in range(2):
        pltpu.make_async_copy(out_v.at[s], o_hbm.at[base], osem.at[s]).wait()


def _pack_bf16_pairs(tab):
    b = lax.bitcast_convert_type(tab.astype(BF16), jnp.uint16).astype(jnp.uint32)
    half = tab.shape[1] // 2
    return lax.bitcast_convert_type(b[:, :half] | (b[:, half:] << 16), jnp.int32)


def _peer(idx, hq, gates, u_words, v_words):
    T = hq.shape[0]
    assert T % (2 * SC_WORKERS) == 0
    mesh = plsc.VectorSubcoreMesh(core_axis_name="c", subcore_axis_name="s",
                                  num_cores=SC_CORES, num_subcores=SC_SUBCORES)
    return pl.kernel(
        _peer_sc_body,
        out_type=jax.ShapeDtypeStruct((T, D_MODEL), F32),
        mesh=mesh,
        scratch_types=[
            pltpu.VMEM((2, PEER_SEL), jnp.int32), pltpu.VMEM((2, PEER_SEL), F32),
            pltpu.VMEM((2, PEER_WORDS), jnp.int32),
            pltpu.VMEM((PEER_RING, PEER_CH, PEER_WORDS), jnp.int32),
            pltpu.VMEM((PEER_RING, PEER_CH, PEER_WORDS), jnp.int32),
            pltpu.VMEM((PEER_CH, SC_LANES), F32), pltpu.VMEM((PEER_SEL,), F32),
            pltpu.VMEM((2, D_MODEL), F32),
            pltpu.SemaphoreType.DMA((PEER_RING,)), pltpu.SemaphoreType.DMA((PEER_RING,)),
            pltpu.SemaphoreType.DMA((2,)), pltpu.SemaphoreType.DMA((2,)),
        ],
        compiler_params=pltpu.CompilerParams(needs_layout_passes=False),
        name="peer_sc",
    )(idx, gates, _pack_bf16_pairs(hq), u_words, v_words)


FINAL_TS = 256


def _final_kernel(x1_ref, pe_ref, p_ref, gp_ref, wg_ref, wp_ref, gf_ref, o_ref):
    x2 = x1_ref[...] + pe_ref[...]
    e = _dot(p_ref[...].astype(BF16), wp_ref[...])
    gate = jax.nn.sigmoid(_dot(_rms(x2, gp_ref[...]).astype(BF16), wg_ref[...]))
    o_ref[...] = _rms(x2 + gate * e, gf_ref[...])


def _final(x1, peer_out, p, t0, nt, g_ple, ple_w_gate, ple_w_proj, g_final):
    B = p.shape[0]
    ts = min(FINAL_TS, nt)
    nblk = nt // ts
    i0 = t0 // ts
    row = lambda d: pl.BlockSpec((ts, d), lambda b, i: (b * nblk + i, 0))
    full = lambda shape: pl.BlockSpec(shape, lambda b, i: (0,) * len(shape))
    return pl.pallas_call(
        _final_kernel,
        grid=(B, nblk),
        in_specs=[row(D_MODEL), row(D_MODEL),
                  pl.BlockSpec((None, ts, D_PLE), lambda b, i: (b, i0 + i, 0)),
                  full((1, D_MODEL)), full((D_MODEL, D_MODEL)), full((D_PLE, D_MODEL)),
                  full((1, D_MODEL))],
        out_specs=pl.BlockSpec((None, ts, D_MODEL), lambda b, i: (b, i, 0)),
        out_shape=jax.ShapeDtypeStruct((B, nt, D_MODEL), F32),
        compiler_params=pltpu.CompilerParams(
            dimension_semantics=("parallel", "parallel"), vmem_limit_bytes=VMEM_LIMIT),
        name="final",
    )(x1, peer_out, p, g_ple, ple_w_gate, ple_w_proj, g_final)


CHUNK_STEPS = (512, 512, 1024, 1024, 1024, 1024, 1024, 1024, 512, 512)


def kernel(x, p, positions, g_mix, w_in, ssm_log_dt, ssm_a_re, ssm_a_im, ssm_b_re, ssm_b_im,
           ssm_c_re, ssm_c_im, ssm_d, ssm_w_glu, w_proj_ssm, w_proj_att, w_out, g_ffn,
           peer_w_q, peer_keys1, peer_keys2, peer_u, peer_v, g_ple, ple_w_gate, ple_w_proj,
           g_final):
    B, S, _ = x.shape
    assert w_in.shape[0] == 1, "the final rmsnorm is fused into the single layer's last stage"
    steps = CHUNK_STEPS if sum(CHUNK_STEPS) == S else (S,)
    i = 0
    tables = _s5_tables(ssm_log_dt[i], ssm_a_re[i], ssm_a_im[i], ssm_b_re[i], ssm_b_im[i],
                        ssm_c_re[i], ssm_c_im[i])
    w_in_b, w_glu_b = w_in[i].astype(BF16), ssm_w_glu[i].astype(BF16)
    d_skip = ssm_d[i].reshape(1, D_SSM).astype(F32)
    merge_w = (w_proj_ssm[i].astype(BF16), w_proj_att[i].astype(BF16), w_out[i].astype(BF16),
               g_ffn[i].reshape(1, D_MODEL), peer_w_q[i].astype(BF16), peer_keys1[i], peer_keys2[i])
    final_w = (g_ple[i].reshape(1, D_MODEL), ple_w_gate[i].astype(BF16),
               ple_w_proj[i].astype(BF16), g_final.reshape(1, D_MODEL))
    u_words = _pack_bf16_pairs(peer_u[i])
    v_words = _pack_bf16_pairs(peer_v[i])
    k_all = jnp.zeros((B, S, D_ATT), BF16)
    v_all = jnp.zeros((B, S, D_ATT), BF16)
    carry = jnp.zeros((2, SUBLANES, D_STATE), F32)
    outs = []
    t0 = 0
    after = (carry, carry)
    for nt in steps:
        u_sb, q, k, v, ga, gb = _in_proj(x, positions, g_mix[i], w_in_b, t0, nt, after)
        k_all = lax.dynamic_update_slice(k_all, k, (0, t0, 0))
        v_all = lax.dynamic_update_slice(v_all, v, (0, t0, 0))
        ys, carry = _s5(u_sb, carry, tables, d_skip, w_glu_b, B)
        att = _moba(q, k_all, v_all, t0 // MOBA_BLOCK)
        x1, hq, scores = _merge(x, ys, att, ga, gb, t0, *merge_w)
        idx, gates = _topk(scores)
        after = (gates, outs[-1] if outs else carry)
        peer_out = _peer(idx, hq, gates, u_words, v_words)
        outs.append(_final(x1, peer_out, p[i], t0, nt, *final_w))
        t0 += nt
    return jnp.concatenate(outs, axis=1)
```

```python
import functools
import math

import jax
import jax.numpy as jnp
from jax import lax
from jax.experimental import pallas as pl
from jax.experimental.pallas import tpu as pltpu
from jax.experimental.pallas import tpu_sc as plsc

F32 = jnp.float32
BF16 = jnp.bfloat16

D_MODEL = 1024
D_SSM = 512
SSM_GROUP = 16
SSM_GROUPS = 32
SSM_STATE = 64
D_STATE = SSM_GROUPS * SSM_STATE
N_HEADS = 8
HEAD_DIM = 64
D_ATT = 512
ROT_DIM = 16
ROPE_THETA = 500000.0
MOBA_BLOCK = 256
MOBA_TOPK = 3
PEER_HEADS = 8
PEER_KEYS = 128
PEER_QDIM = 256
PEER_HALF = 128
PEER_TOPK = 16
PEER_SEL = PEER_HEADS * PEER_TOPK
D_PLE = 256
EPS = 1e-6
NEG = -1e30
LANES = 128
SUBLANES = 8
VMEM_LIMIT = 48 * 1024 * 1024
HIGHEST = lax.Precision.HIGHEST


def _rms(x, g):
    return x * lax.rsqrt(jnp.mean(x * x, axis=-1, keepdims=True) + EPS) * g


def _dot(a, b):
    return jnp.dot(a, b, preferred_element_type=F32)


def _dot_nt(a, b, precision=None):
    return lax.dot_general(a, b, (((1,), (1,)), ((), ())), precision=precision,
                           preferred_element_type=F32)


IN_TS = 512


def _in_proj_kernel(x_ref, pos_ref, g_ref, w_ref, invf_ref, after_a, after_b,
                    u_ref, q_ref, k_ref, v_ref, ga_ref, gb_ref):
    del after_a, after_b
    h = _rms(x_ref[...], g_ref[...]).astype(BF16)

    def proj(lo, hi):
        return _dot(h, w_ref[:, lo:hi])

    u_ref[...] = proj(0, D_SSM).astype(BF16)
    ang = pos_ref[...].astype(F32) * invf_ref[...]
    cos = jnp.cos(ang)
    sin = jnp.sin(ang)
    lane = lax.broadcasted_iota(jnp.int32, (1, LANES), 1) % HEAD_DIM
    half = ROT_DIM // 2
    sin_hi = jnp.where((lane >= half) & (lane < ROT_DIM), sin, 0.0)
    sin_lo = jnp.where(lane < half, -sin, 0.0)
    reps = D_ATT // LANES
    cos4 = jnp.concatenate([cos] * reps, axis=1)
    sin_hi4 = jnp.concatenate([sin_hi] * reps, axis=1)
    sin_lo4 = jnp.concatenate([sin_lo] * reps, axis=1)

    def rope(t):
        return (t * cos4 + pltpu.roll(t, half, 1) * sin_hi4
                + pltpu.roll(t, D_ATT - half, 1) * sin_lo4)

    q = rope(proj(D_SSM, D_SSM + D_ATT))
    q_ref[...] = (q * (HEAD_DIM ** -0.5)).astype(BF16)
    k_ref[...] = rope(proj(D_SSM + D_ATT, D_SSM + 2 * D_ATT)).astype(BF16)
    v_ref[...] = proj(D_SSM + 2 * D_ATT, D_SSM + 3 * D_ATT).astype(BF16)
    o = D_SSM + 3 * D_ATT
    ga_ref[...] = jax.nn.sigmoid(proj(o, o + D_MODEL)).astype(BF16)
    gb_ref[...] = jax.nn.sigmoid(proj(o + D_MODEL, o + 2 * D_MODEL)).astype(BF16)


def _in_proj(x, positions, g_mix, w_in, t0, nt, after):
    B, S, _ = x.shape
    ts = min(IN_TS, nt)
    assert nt % ts == 0 and t0 % ts == 0
    i0 = t0 // ts
    inv_freq = ROPE_THETA ** (-jnp.arange(0, ROT_DIM, 2, dtype=F32) / ROT_DIM)
    lane = jnp.arange(LANES) % HEAD_DIM
    invf = jnp.where(lane < ROT_DIM, inv_freq[lane % (ROT_DIM // 2)], 0.0).reshape(1, LANES)
    d_in = w_in.shape[1]
    src = lambda d: pl.BlockSpec((None, ts, d), lambda b, i: (b, i0 + i, 0))
    tok = lambda d: pl.BlockSpec((None, ts, d), lambda b, i: (b, i, 0))
    full = lambda shape: pl.BlockSpec(shape, lambda b, i: (0,) * len(shape))
    return pl.pallas_call(
        _in_proj_kernel,
        grid=(B, nt // ts),
        in_specs=[src(D_MODEL), src(1), full((1, D_MODEL)), full((D_MODEL, d_in)), full((1, LANES)),
                  pl.BlockSpec(memory_space=pl.ANY), pl.BlockSpec(memory_space=pl.ANY)],
        out_specs=[pl.BlockSpec((ts, D_SSM), lambda b, i: (i, b)),
                   tok(D_ATT), tok(D_ATT), tok(D_ATT), tok(D_MODEL), tok(D_MODEL)],
        out_shape=[jax.ShapeDtypeStruct((nt, B * D_SSM), BF16),
                   jax.ShapeDtypeStruct((B, nt, D_ATT), BF16),
                   jax.ShapeDtypeStruct((B, nt, D_ATT), BF16),
                   jax.ShapeDtypeStruct((B, nt, D_ATT), BF16),
                   jax.ShapeDtypeStruct((B, nt, D_MODEL), BF16),
                   jax.ShapeDtypeStruct((B, nt, D_MODEL), BF16)],
        compiler_params=pltpu.CompilerParams(
            dimension_semantics=("parallel", "parallel"), vmem_limit_bytes=VMEM_LIMIT),
        name="in_proj",
    )(x, positions.reshape(B, S, 1), g_mix.reshape(1, D_MODEL), w_in, invf, *after)


S5_TS = 128
S5_BATCH = 4
S5_COLS = 512


def _s5_kernel(u_ref, c0_ref, bre_ref, bim_ref, a1r_ref, a1i_ref, pr_ref, pi_ref,
               cre_ref, cim_ref, d_ref, wglu_ref, y_ref, c1_ref,
               xr, xi, cr, ci, ysc):
    rows = xr.shape[0]
    ts = rows // S5_BATCH

    @pl.when(pl.program_id(0) == 0)
    def _():
        cr[...] = c0_ref[0]
        ci[...] = c0_ref[1]

    u = u_ref[...]
    xr[...] = _dot(u, bre_ref[...])
    xi[...] = _dot(u, bim_ref[...])

    hi_rows = lax.broadcasted_iota(jnp.int32, (SUBLANES, S5_COLS), 0) >= S5_BATCH
    for cb in range(D_STATE // S5_COLS):
        sl = slice(cb * S5_COLS, (cb + 1) * S5_COLS)
        a_r, a_i = a1r_ref[:, sl], a1i_ref[:, sl]
        p_r, p_i = pr_ref[:, sl], pi_ref[:, sl]

        def body(t, carry):
            c_r, c_i = carry
            r0 = pl.multiple_of(t * SUBLANES, SUBLANES)
            x_r = xr[pl.ds(r0, SUBLANES), sl]
            x_i = xi[pl.ds(r0, SUBLANES), sl]
            s_r = pltpu.roll(x_r, S5_BATCH, 0)
            s_i = pltpu.roll(x_i, S5_BATCH, 0)
            h_r = x_r + (a_r * s_r - a_i * s_i) + (p_r * c_r - p_i * c_i)
            h_i = x_i + (a_r * s_i + a_i * s_r) + (p_r * c_i + p_i * c_r)
            xr[pl.ds(r0, SUBLANES), sl] = h_r
            xi[pl.ds(r0, SUBLANES), sl] = h_i
            n_r = jnp.where(hi_rows, h_r, pltpu.roll(h_r, S5_BATCH, 0))
            n_i = jnp.where(hi_rows, h_i, pltpu.roll(h_i, S5_BATCH, 0))
            return n_r, n_i

        c_r, c_i = lax.fori_loop(0, rows // SUBLANES, body, (cr[:, sl], ci[:, sl]), unroll=2)
        cr[:, sl] = c_r
        ci[:, sl] = c_i

    y = (_dot(xr[...].astype(BF16), cre_ref[...]) - _dot(xi[...].astype(BF16), cim_ref[...])
         + d_ref[...] * u.astype(F32))
    y = jax.nn.gelu(y)
    y = y * jax.nn.sigmoid(_dot(y.astype(BF16), wglu_ref[...]))
    for c in range(D_SSM // LANES):
        ysc[c] = y[:, c * LANES:(c + 1) * LANES]
    for b in range(S5_BATCH):
        for c in range(D_SSM // LANES):
            y_ref[b, :, c * LANES:(c + 1) * LANES] = (
                ysc[c, pl.ds(b, ts, stride=S5_BATCH), :].astype(BF16))

    @pl.when(pl.program_id(0) == pl.num_programs(0) - 1)
    def _():
        c1_ref[0] = cr[...]
        c1_ref[1] = ci[...]


def _s5_tables(log_dt, a_re, a_im, b_re, b_im, c_re, c_im):
    dt = jnp.exp(log_dt.astype(F32))[:, None]
    ar, ai = a_re.astype(F32), a_im.astype(F32)
    mag = jnp.exp(dt * ar)
    abar_re, abar_im = mag * jnp.cos(dt * ai), mag * jnp.sin(dt * ai)
    den = ar * ar + ai * ai
    nr, ni = abar_re - 1.0, abar_im
    f_re = (nr * ar + ni * ai) / den
    f_im = (ni * ar - nr * ai) / den
    br, bi = b_re.astype(F32), b_im.astype(F32)
    bb_re = f_re[..., None] * br - f_im[..., None] * bi
    bb_im = f_re[..., None] * bi + f_im[..., None] * br
    eye = jnp.eye(SSM_GROUPS, dtype=F32)

    def in_blockdiag(bb):
        return jnp.einsum('gnc,gh->gchn', bb, eye).reshape(D_SSM, D_STATE)

    def out_blockdiag(c):
        return jnp.einsum('gcn,gh->gnhc', c.astype(F32), eye).reshape(D_STATE, D_SSM)

    a_r = abar_re.reshape(1, D_STATE)
    a_i = abar_im.reshape(1, D_STATE)
    a2_r = a_r * a_r - a_i * a_i
    a2_i = 2.0 * a_r * a_i
    hi = (jnp.arange(SUBLANES) >= S5_BATCH)[:, None]
    a1r = jnp.where(hi, a_r, 0.0)
    a1i = jnp.where(hi, a_i, 0.0)
    p_r = jnp.where(hi, a2_r, a_r)
    p_i = jnp.where(hi, a2_i, a_i)
    return (in_blockdiag(bb_re).astype(BF16), in_blockdiag(bb_im).astype(BF16),
            a1r, a1i, p_r, p_i,
            out_blockdiag(c_re).astype(BF16), out_blockdiag(c_im).astype(BF16))


def _s5(u_sb, carry, tables, d_skip, w_glu, B):
    assert B == S5_BATCH
    nt = u_sb.shape[0]
    ts = min(S5_TS, nt)
    rows = ts * B
    bre, bim, a1r, a1i, p_r, p_i, cre, cim = tables
    full = lambda shape: pl.BlockSpec(shape, lambda i: (0,) * len(shape))
    return pl.pallas_call(
        _s5_kernel,
        grid=(nt // ts,),
        in_specs=[pl.BlockSpec((rows, D_SSM), lambda i: (i, 0)),
                  full((2, SUBLANES, D_STATE)),
                  full((D_SSM, D_STATE)), full((D_SSM, D_STATE)),
                  full((SUBLANES, D_STATE)), full((SUBLANES, D_STATE)),
                  full((SUBLANES, D_STATE)), full((SUBLANES, D_STATE)),
                  full((D_STATE, D_SSM)), full((D_STATE, D_SSM)),
                  full((1, D_SSM)), full((D_SSM, D_SSM))],
        out_specs=[pl.BlockSpec((B, ts, D_SSM), lambda i: (0, i, 0)),
                   full((2, SUBLANES, D_STATE))],
        out_shape=[jax.ShapeDtypeStruct((B, nt, D_SSM), BF16),
                   jax.ShapeDtypeStruct((2, SUBLANES, D_STATE), F32)],
        scratch_shapes=[pltpu.VMEM((rows, D_STATE), F32), pltpu.VMEM((rows, D_STATE), F32),
                        pltpu.VMEM((SUBLANES, D_STATE), F32), pltpu.VMEM((SUBLANES, D_STATE), F32),
                        pltpu.VMEM((D_SSM // LANES, rows, LANES), F32)],
        compiler_params=pltpu.CompilerParams(
            dimension_semantics=("arbitrary",), vmem_limit_bytes=VMEM_LIMIT),
        name="s5",
    )(u_sb.reshape(nt * B, D_SSM), carry, bre, bim, a1r, a1i, p_r, p_i, cre, cim, d_skip, w_glu)


MOBA_PAIR = 2 * MOBA_BLOCK


def _moba_kernel(q0, q_ref, k_ref, v_ref, o_ref, kmean, kaug_a, kaug_b, vaug_a, vaug_b, m_s, acc_s,
                 s_buf):
    qi = pl.program_id(2) + q0
    nb = k_ref.shape[0] // MOBA_BLOCK
    nbp = kmean.shape[0]
    lane = lax.broadcasted_iota(jnp.int32, (1, LANES), 1)
    head_a = lane < HEAD_DIM

    @pl.when(pl.program_id(2) == 0)
    def _():
        kmean[...] = jnp.zeros_like(kmean)
        for j in range(nb):
            rows = pl.ds(j * MOBA_BLOCK, MOBA_BLOCK)
            kj = k_ref[rows, :].astype(F32)
            vj = v_ref[rows, :].astype(F32)
            kmean[j:j + 1, :] = jnp.sum(kj, axis=0, keepdims=True) * (1.0 / MOBA_BLOCK)
            kaug_a[rows, :] = jnp.where(head_a, kj, jnp.where(lane - HEAD_DIM == j, 1.0, 0.0)).astype(BF16)
            kaug_b[rows, :] = jnp.where(head_a, jnp.where(lane == j, 1.0, 0.0), kj).astype(BF16)
            vaug_a[rows, :] = jnp.where(head_a, vj, 1.0).astype(BF16)
            vaug_b[rows, :] = jnp.where(head_a, 1.0, vj).astype(BF16)

    qf = q_ref[...].astype(F32)
    blk_row = lax.broadcasted_iota(jnp.int32, (nbp, MOBA_BLOCK), 0)
    q_augs = []
    for is_a in (True, False):
        mine = head_a if is_a else jnp.logical_not(head_a)
        q_own = jnp.where(mine, qf, 0.0)
        g = _dot_nt(kmean[...], q_own, precision=HIGHEST)
        g = jnp.where(blk_row < qi, g, NEG)
        sel = jnp.zeros(g.shape, F32)
        for _ in range(MOBA_TOPK):
            m = jnp.max(g, axis=0, keepdims=True)
            idx = jnp.min(jnp.where(g == m, blk_row, nbp), axis=0, keepdims=True)
            hit = blk_row == idx
            sel = jnp.where(hit, jnp.where(idx < qi, 1.0, 0.0), sel)
            g = jnp.where(hit, -jnp.inf, g)
        bias_t = jnp.where(sel > 0.0, 0.0, jnp.where(blk_row == qi, 0.0, NEG))
        bias_t = jnp.concatenate([bias_t, jnp.full((LANES - nbp, MOBA_BLOCK), NEG, F32)], axis=0)
        bias = jnp.transpose(bias_t)
        if is_a:
            bias = pltpu.roll(bias, HEAD_DIM, 1)
        q_augs.append(jnp.where(mine, qf, bias).astype(BF16))

    m_s[...] = jnp.full(m_s.shape, -jnp.inf, F32)
    acc_s[...] = jnp.zeros_like(acc_s)
    qpos = qi * MOBA_BLOCK + lax.broadcasted_iota(jnp.int32, (MOBA_BLOCK, MOBA_PAIR), 0)
    col = lax.broadcasted_iota(jnp.int32, (MOBA_BLOCK, MOBA_PAIR), 1)

    def kv_rows(jj):
        return pl.ds(pl.multiple_of(jj * MOBA_PAIR, MOBA_PAIR), MOBA_PAIR)

    def scores(jj, slot):
        for hd, kaug in enumerate((kaug_a, kaug_b)):
            s_buf[slot, hd] = _dot_nt(q_augs[hd], kaug[kv_rows(jj), :])

    def softmax_pv(jj, slot, causal):
        for hd, vaug in enumerate((vaug_a, vaug_b)):
            s = s_buf[slot, hd]
            if causal:
                s = jnp.where(jj * MOBA_PAIR + col <= qpos, s, NEG)
            m_old = m_s[hd]
            m_new = jnp.maximum(m_old, jnp.max(s, axis=-1, keepdims=True))
            alpha = jnp.exp(m_old - m_new)
            p = jnp.exp(s - m_new)
            m_s[hd] = m_new
            acc_s[hd] = alpha * acc_s[hd] + _dot(p.astype(BF16), vaug[kv_rows(jj), :])

    last = qi // 2
    scores(0, 0)

    def body(k, _):
        scores(2 * k + 1, 1)
        softmax_pv(2 * k, 0, False)
        scores(2 * k + 2, 0)
        softmax_pv(2 * k + 1, 1, False)
        return 0

    lax.fori_loop(0, last // 2, body, 0)

    @pl.when(last % 2 == 0)
    def _():
        softmax_pv(last, 0, True)

    @pl.when(last % 2 == 1)
    def _():
        scores(last, 1)
        softmax_pv(last - 1, 0, False)
        softmax_pv(last, 1, True)
    acc_a, acc_b = acc_s[0], acc_s[1]
    o_ref[...] = jnp.where(head_a, acc_a / pltpu.roll(acc_a, HEAD_DIM, 1),
                           acc_b / pltpu.roll(acc_b, HEAD_DIM, 1)).astype(BF16)


def _moba(q, k, v, q0):
    B = q.shape[0]
    nq = q.shape[1] // MOBA_BLOCK
    skv = (q0 + nq) * MOBA_BLOCK
    nb = skv // MOBA_BLOCK
    assert nb <= HEAD_DIM and nb % 2 == 0 and skv <= k.shape[1]
    nbp = -(-nb // SUBLANES) * SUBLANES
    blk = pl.BlockSpec((None, MOBA_BLOCK, LANES), lambda b, h, i: (b, i, h))
    seq = pl.BlockSpec((None, skv, LANES), lambda b, h, i: (b, 0, h))
    return pl.pallas_call(
        functools.partial(_moba_kernel, q0),
        grid=(B, D_ATT // LANES, nq),
        in_specs=[blk, seq, seq],
        out_specs=blk,
        out_shape=jax.ShapeDtypeStruct(q.shape, BF16),
        scratch_shapes=[pltpu.VMEM((nbp, LANES), F32),
                        pltpu.VMEM((skv, LANES), BF16), pltpu.VMEM((skv, LANES), BF16),
                        pltpu.VMEM((skv, LANES), BF16), pltpu.VMEM((skv, LANES), BF16),
                        pltpu.VMEM((2, MOBA_BLOCK, 1), F32),
                        pltpu.VMEM((2, MOBA_BLOCK, LANES), F32),
                        pltpu.VMEM((2, 2, MOBA_BLOCK, MOBA_PAIR), F32)],
        compiler_params=pltpu.CompilerParams(
            dimension_semantics=("parallel", "parallel", "arbitrary"), vmem_limit_bytes=VMEM_LIMIT),
        name="moba",
    )(q, k, v)


MERGE_TS = 256


def _merge_kernel(x_ref, ys_ref, at_ref, ga_ref, gb_ref, wa_ref, wb_ref, wo_ref, g_ref,
                  wq_ref, k1_ref, k2_ref, x1_ref, hq_ref, sc_ref):
    ya = _dot(ys_ref[...], wa_ref[...])
    yb = _dot(at_ref[...], wb_ref[...])
    merged = ga_ref[...].astype(F32) * ya + gb_ref[...].astype(F32) * yb
    x1 = x_ref[...] + _dot(merged.astype(BF16), wo_ref[...])
    x1_ref[...] = x1
    hq = _rms(x1, g_ref[...])
    hq_ref[...] = hq
    qp = _dot(hq.astype(BF16), wq_ref[...])
    for h in range(PEER_HEADS):
        o = h * PEER_QDIM
        sc_ref[2 * h] = _dot_nt(k1_ref[h], qp[:, o:o + PEER_HALF], precision=HIGHEST)
        sc_ref[2 * h + 1] = _dot_nt(k2_ref[h], qp[:, o + PEER_HALF:o + PEER_QDIM], precision=HIGHEST)


def _merge(x, ys, att, ga, gb, t0, w_proj_ssm, w_proj_att, w_out, g_ffn, peer_w_q, keys1, keys2):
    B, nt = ys.shape[0], ys.shape[1]
    ts = min(MERGE_TS, nt)
    nblk = nt // ts
    i0 = t0 // ts
    tok = lambda d: pl.BlockSpec((None, ts, d), lambda b, i: (b, i, 0))
    row = lambda d: pl.BlockSpec((ts, d), lambda b, i: (b * nblk + i, 0))
    full = lambda shape: pl.BlockSpec(shape, lambda b, i: (0,) * len(shape))
    qd = PEER_HEADS * PEER_QDIM
    return pl.pallas_call(
        _merge_kernel,
        grid=(B, nblk),
        in_specs=[pl.BlockSpec((None, ts, D_MODEL), lambda b, i: (b, i0 + i, 0)),
                  tok(D_SSM), tok(D_ATT), tok(D_MODEL), tok(D_MODEL),
                  full((D_SSM, D_MODEL)), full((D_ATT, D_MODEL)), full((D_MODEL, D_MODEL)),
                  full((1, D_MODEL)), full((D_MODEL, qd)),
                  full((PEER_HEADS, PEER_KEYS, PEER_HALF)), full((PEER_HEADS, PEER_KEYS, PEER_HALF))],
        out_specs=[row(D_MODEL), row(D_MODEL),
                   pl.BlockSpec((2 * PEER_HEADS, PEER_KEYS, ts), lambda b, i: (0, 0, b * nblk + i))],
        out_shape=[jax.ShapeDtypeStruct((B * nt, D_MODEL), F32),
                   jax.ShapeDtypeStruct((B * nt, D_MODEL), F32),
                   jax.ShapeDtypeStruct((2 * PEER_HEADS, PEER_KEYS, B * nt), F32)],
        compiler_params=pltpu.CompilerParams(
            dimension_semantics=("parallel", "parallel"), vmem_limit_bytes=VMEM_LIMIT),
        name="merge",
    )(x, ys, att, ga, gb, w_proj_ssm, w_proj_att, w_out, g_ffn, peer_w_q, keys1, keys2)


TOPK_TS = 256


def _top_rows(s, row, k):
    vals, idxs = [], []
    for _ in range(k):
        m = jnp.max(s, axis=0, keepdims=True)
        idx = jnp.min(jnp.where(s == m, row, s.shape[0]), axis=0, keepdims=True)
        vals.append(m)
        idxs.append(idx)
        s = jnp.where(row == idx, -jnp.inf, s)
    return vals, idxs


def _stack_rows(rows, row16):
    acc = jnp.zeros(row16.shape, rows[0].dtype)
    for r, v in enumerate(rows):
        acc = jnp.where(row16 == r, v, acc)
    return acc


def _topk_kernel(sc_ref, idx_ref, gate_ref):
    ts = sc_ref.shape[-1]
    row = lax.broadcasted_iota(jnp.int32, (PEER_KEYS, ts), 0)
    row16 = lax.broadcasted_iota(jnp.int32, (PEER_TOPK, ts), 0)
    row8 = lax.broadcasted_iota(jnp.int32, (SUBLANES, ts), 0)
    counts = [PEER_TOPK // (i + 1) for i in range(PEER_TOPK)]
    heights = [PEER_TOPK if c > SUBLANES else SUBLANES for c in counts]
    n_cand = sum(heights)
    rowc = lax.broadcasted_iota(jnp.int32, (n_cand, ts), 0)
    gate_rows, eid_rows = [], []
    for h in range(PEER_HEADS):
        v1, i1 = _top_rows(sc_ref[2 * h], row, PEER_TOPK)
        v2, i2 = _top_rows(sc_ref[2 * h + 1], row, PEER_TOPK)
        v2s = _stack_rows(v2, row16)
        i2s = _stack_rows(i2, row16).astype(F32)
        cand, eid = [], []
        for i in range(PEER_TOPK):
            n = heights[i]
            cand.append(jnp.where((row16 if n == PEER_TOPK else row8) < counts[i],
                                  v1[i] + v2s[:n], -jnp.inf))
            eid.append(i1[i].astype(F32) * PEER_KEYS + i2s[:n])
        cand = jnp.concatenate(cand, axis=0)
        eid = jnp.concatenate(eid, axis=0)
        tops, picks = [], []
        for _ in range(PEER_TOPK):
            m = jnp.max(cand, axis=0, keepdims=True)
            pos = jnp.min(jnp.where(cand == m, rowc, n_cand), axis=0, keepdims=True)
            hit = rowc == pos
            picks.append(jnp.max(jnp.where(hit, eid, -1.0), axis=0, keepdims=True))
            tops.append(m)
            cand = jnp.where(hit, -jnp.inf, cand)
        top = _stack_rows(tops, row16)
        p = jnp.exp(top - jnp.max(top, axis=0, keepdims=True))
        gate_rows.append(p / jnp.sum(p, axis=0, keepdims=True))
        eid_rows.append(_stack_rows(picks, row16))
    gate_ref[...] = jnp.transpose(jnp.concatenate(gate_rows, axis=0))
    idx_ref[...] = jnp.transpose(jnp.concatenate(eid_rows, axis=0)).astype(jnp.int32)


def _topk(scores):
    T = scores.shape[-1]
    ts = min(TOPK_TS, T)
    return pl.pallas_call(
        _topk_kernel,
        grid=(T // ts,),
        in_specs=[pl.BlockSpec((2 * PEER_HEADS, PEER_KEYS, ts), lambda i: (0, 0, i))],
        out_specs=[pl.BlockSpec((ts, PEER_SEL), lambda i: (i, 0)),
                   pl.BlockSpec((ts, PEER_SEL), lambda i: (i, 0))],
        out_shape=[jax.ShapeDtypeStruct((T, PEER_SEL), jnp.int32),
                   jax.ShapeDtypeStruct((T, PEER_SEL), F32)],
        compiler_params=pltpu.CompilerParams(
            dimension_semantics=("parallel",), vmem_limit_bytes=VMEM_LIMIT),
        name="topk",
    )(scores)


SC_CORES = 2
SC_SUBCORES = 16
SC_LANES = 16
SC_WORKERS = SC_CORES * SC_SUBCORES
PEER_CH = SC_LANES
PEER_NCH = PEER_SEL // PEER_CH
PEER_WORDS = D_MODEL // 2
PEER_NWG = PEER_WORDS // SC_LANES
PEER_RING = 4
PEER_QUAD = 4
HI_MASK = -65536
GELU_C = 0.7978845608028654


def _gelu_tanh_via_exp(x):
    z = GELU_C * (x + 0.044715 * (x * x * x))
    t = 1.0 - 2.0 / (jnp.exp(2.0 * z) + 1.0)
    return 0.5 * x * (1.0 + t)


def _unpack_pair(w):
    lo = plsc.bitcast(lax.shift_left(w, 16), F32)
    hi = plsc.bitcast(lax.bitwise_and(w, HI_MASK), F32)
    return lo, hi


def _peer_sc_body(idx_hbm, gate_hbm, h_hbm, u_hbm, v_hbm, o_hbm,
                  idx_v, gate_v, h_v, ubuf, vbuf, pbuf, w_v, out_v, usem, vsem, msem, osem):
    n_tok = o_hbm.shape[0] // SC_WORKERS
    base = (lax.axis_index("s") * SC_CORES + lax.axis_index("c")) * n_tok
    lane = lax.iota(jnp.int32, SC_LANES)
    zero_rows = jnp.zeros((SC_LANES,), jnp.int32)

    def meta_copies(tok, s):
        return (pltpu.make_async_copy(idx_hbm.at[tok], idx_v.at[s], msem.at[s]),
                pltpu.make_async_copy(gate_hbm.at[tok], gate_v.at[s], msem.at[s]),
                pltpu.make_async_copy(h_hbm.at[tok], h_v.at[s], msem.at[s]))

    def gather(tab, buf, sem, slot, rows):
        return pltpu.make_async_copy(tab.at[rows], buf.at[slot], sem.at[slot])

    def start_ahead(tab, buf, sem, s, c):
        ahead = c + PEER_RING
        src = jnp.where(ahead < PEER_NCH, s, 1 - s)
        ch = ahead % PEER_NCH
        rows = idx_v[src, pl.ds(pl.multiple_of(ch * PEER_CH, PEER_CH), PEER_CH)]
        gather(tab, buf, sem, c % PEER_RING, rows).start()

    def token(t, carry):
        s = t % 2
        tok = base + t
        nxt = base + jnp.minimum(t + 1, n_tok - 1)
        for cp in meta_copies(nxt, 1 - s):
            cp.start()

        def u_chunk(c, carry):
            slot = c % PEER_RING
            gather(u_hbm, ubuf, usem, slot, zero_rows).wait()

            def dot_step(q, accs):
                cols = [pl.ds(pl.multiple_of((q * PEER_QUAD + j) * SC_LANES, SC_LANES), SC_LANES)
                        for j in range(PEER_QUAD)]
                hs = [plsc.bitcast(h_v[s, col], BF16) for col in cols]
                out = []
                for r in range(PEER_CH):
                    p = plsc.bitcast(ubuf[slot, r, cols[0]], BF16) * hs[0]
                    for j in range(1, PEER_QUAD):
                        p = p + plsc.bitcast(ubuf[slot, r, cols[j]], BF16) * hs[j]
                    lo, hi = _unpack_pair(plsc.bitcast(p, jnp.int32))
                    out.append(accs[r] + lo + hi)
                return tuple(out)

            accs = lax.fori_loop(0, PEER_NWG // PEER_QUAD, dot_step,
                                 tuple(jnp.zeros((SC_LANES,), F32) for _ in range(PEER_CH)))

            @pl.when(c == PEER_NCH - PEER_RING)
            def _():
                for cp in meta_copies(nxt, 1 - s):
                    cp.wait()

            start_ahead(u_hbm, ubuf, usem, s, c)
            for r in range(PEER_CH):
                pbuf[r, :] = accs[r]
            tot = jnp.zeros((SC_LANES,), F32)
            for j in range(SC_LANES):
                tot = tot + plsc.load_gather(pbuf, [lane, jnp.full((SC_LANES,), j, jnp.int32)])
            rows = pl.ds(pl.multiple_of(c * PEER_CH, PEER_CH), PEER_CH)
            w_v[rows] = gate_v[s, rows] * _gelu_tanh_via_exp(tot)
            return carry

        lax.fori_loop(0, PEER_NCH, u_chunk, 0)

        @pl.when(t >= 2)
        def _():
            pltpu.make_async_copy(out_v.at[s], o_hbm.at[tok], osem.at[s]).wait()

        def v_chunk(c, carry):
            slot = c % PEER_RING
            gather(v_hbm, vbuf, vsem, slot, zero_rows).wait()
            ws = []
            for r in range(PEER_CH):
                w = plsc.load_gather(w_v, [jnp.full((SC_LANES,), r, jnp.int32) + c * PEER_CH])
                ws.append(plsc.pack(w, w, format=plsc.PackFormat.INTERLEAVED,
                                    preferred_element_type=BF16))
            first = c == 0

            @plsc.parallel_loop(0, PEER_NWG, unroll=2)
            def acc_step(g):
                col = pl.ds(pl.multiple_of(g * SC_LANES, SC_LANES), SC_LANES)
                col_hi = pl.ds(pl.multiple_of(PEER_WORDS + g * SC_LANES, SC_LANES), SC_LANES)
                o_lo = jnp.where(first, 0.0, out_v[s, col])
                o_hi = jnp.where(first, 0.0, out_v[s, col_hi])
                for r0 in range(0, PEER_CH, PEER_QUAD):
                    p = plsc.bitcast(vbuf[slot, r0, col], BF16) * ws[r0]
                    for r in range(r0 + 1, r0 + PEER_QUAD):
                        p = p + plsc.bitcast(vbuf[slot, r, col], BF16) * ws[r]
                    lo, hi = _unpack_pair(plsc.bitcast(p, jnp.int32))
                    o_lo = o_lo + lo
                    o_hi = o_hi + hi
                out_v[s, col] = o_lo
                out_v[s, col_hi] = o_hi

            start_ahead(v_hbm, vbuf, vsem, s, c)
            return carry

        lax.fori_loop(0, PEER_NCH, v_chunk, 0)
        pltpu.make_async_copy(out_v.at[s], o_hbm.at[tok], osem.at[s]).start()
        return carry

    for cp in meta_copies(base, 0):
        cp.start()
    for cp in meta_copies(base, 0):
        cp.wait()
    for c in range(PEER_RING):
        rows = idx_v[0, pl.ds(c * PEER_CH, PEER_CH)]
        gather(u_hbm, ubuf, usem, c, rows).start()
        gather(v_hbm, vbuf, vsem, c, rows).start()
    lax.fori_loop(0, n_tok, token, 0)
    for c in range(PEER_RING):
        gather(u_hbm, ubuf, usem, c, zero_rows).wait()
        gather(v_hbm, vbuf, vsem, c, zero_rows).wait()
    for s in range(2):
        pltpu.make_async_copy(out_v.at[s], o_hbm.at[base], osem.at[s]).wait()


def _pack_bf16_pairs(tab):
    b = lax.bitcast_convert_type(tab.astype(BF16), jnp.uint16).astype(jnp.uint32)
    half = tab.shape[1] // 2
    return lax.bitcast_convert_type(b[:, :half] | (b[:, half:] << 16), jnp.int32)


def _peer(idx, hq, gates, u_words, v_words):
    T = hq.shape[0]
    assert T % (2 * SC_WORKERS) == 0
    mesh = plsc.VectorSubcoreMesh(core_axis_name="c", subcore_axis_name="s",
                                  num_cores=SC_CORES, num_subcores=SC_SUBCORES)
    return pl.kernel(
        _peer_sc_body,
        out_type=jax.ShapeDtypeStruct((T, D_MODEL), F32),
        mesh=mesh,
        scratch_types=[
            pltpu.VMEM((2, PEER_SEL), jnp.int32), pltpu.VMEM((2, PEER_SEL), F32),
            pltpu.VMEM((2, PEER_WORDS), jnp.int32),
            pltpu.VMEM((PEER_RING, PEER_CH, PEER_WORDS), jnp.int32),
            pltpu.VMEM((PEER_RING, PEER_CH, PEER_WORDS), jnp.int32),
            pltpu.VMEM((PEER_CH, SC_LANES), F32), pltpu.VMEM((PEER_SEL,), F32),
            pltpu.VMEM((2, D_MODEL), F32),
            pltpu.SemaphoreType.DMA((PEER_RING,)), pltpu.SemaphoreType.DMA((PEER_RING,)),
            pltpu.SemaphoreType.DMA((2,)), pltpu.SemaphoreType.DMA((2,)),
        ],
        compiler_params=pltpu.CompilerParams(needs_layout_passes=False),
        name="peer_sc",
    )(idx, gates, _pack_bf16_pairs(hq), u_words, v_words)


FINAL_TS = 256


def _final_kernel(x1_ref, pe_ref, p_ref, gp_ref, wg_ref, wp_ref, gf_ref, o_ref):
    x2 = x1_ref[...] + pe_ref[...]
    e = _dot(p_ref[...].astype(BF16), wp_ref[...])
    gate = jax.nn.sigmoid(_dot(_rms(x2, gp_ref[...]).astype(BF16), wg_ref[...]))
    o_ref[...] = _rms(x2 + gate * e, gf_ref[...])


def _final(x1, peer_out, p, t0, nt, g_ple, ple_w_gate, ple_w_proj, g_final):
    B = p.shape[0]
    ts = min(FINAL_TS, nt)
    nblk = nt // ts
    i0 = t0 // ts
    row = lambda d: pl.BlockSpec((ts, d), lambda b, i: (b * nblk + i, 0))
    full = lambda shape: pl.BlockSpec(shape, lambda b, i: (0,) * len(shape))
    return pl.pallas_call(
        _final_kernel,
        grid=(B, nblk),
        in_specs=[row(D_MODEL), row(D_MODEL),
                  pl.BlockSpec((None, ts, D_PLE), lambda b, i: (b, i0 + i, 0)),
                  full((1, D_MODEL)), full((D_MODEL, D_MODEL)), full((D_PLE, D_MODEL)),
                  full((1, D_MODEL))],
        out_specs=pl.BlockSpec((None, ts, D_MODEL), lambda b, i: (b, i, 0)),
        out_shape=jax.ShapeDtypeStruct((B, nt, D_MODEL), F32),
        compiler_params=pltpu.CompilerParams(
            dimension_semantics=("parallel", "parallel"), vmem_limit_bytes=VMEM_LIMIT),
        name="final",
    )(x1, peer_out, p, g_ple, ple_w_gate, ple_w_proj, g_final)


CHUNK_STEPS = (512, 512, 1024, 1024, 1024, 1024, 1024, 1024, 512, 512)


def kernel(x, p, positions, g_mix, w_in, ssm_log_dt, ssm_a_re, ssm_a_im, ssm_b_re, ssm_b_im,
           ssm_c_re, ssm_c_im, ssm_d, ssm_w_glu, w_proj_ssm, w_proj_att, w_out, g_ffn,
           peer_w_q, peer_keys1, peer_keys2, peer_u, peer_v, g_ple, ple_w_gate, ple_w_proj,
           g_final):
    B, S, _ = x.shape
    assert w_in.shape[0] == 1, "the final rmsnorm is fused into the single layer's last stage"
    steps = CHUNK_STEPS if sum(CHUNK_STEPS) == S else (S,)
    i = 0
    tables = _s5_tables(ssm_log_dt[i], ssm_a_re[i], ssm_a_im[i], ssm_b_re[i], ssm_b_im[i],
                        ssm_c_re[i], ssm_c_im[i])
    w_in_b, w_glu_b = w_in[i].astype(BF16), ssm_w_glu[i].astype(BF16)
    d_skip = ssm_d[i].reshape(1, D_SSM).astype(F32)
    merge_w = (w_proj_ssm[i].astype(BF16), w_proj_att[i].astype(BF16), w_out[i].astype(BF16),
               g_ffn[i].reshape(1, D_MODEL), peer_w_q[i].astype(BF16), peer_keys1[i], peer_keys2[i])
    final_w = (g_ple[i].reshape(1, D_MODEL), ple_w_gate[i].astype(BF16),
               ple_w_proj[i].astype(BF16), g_final.reshape(1, D_MODEL))
    u_words = _pack_bf16_pairs(peer_u[i])
    v_words = _pack_bf16_pairs(peer_v[i])
    k_all = jnp.zeros((B, S, D_ATT), BF16)
    v_all = jnp.zeros((B, S, D_ATT), BF16)
    carry = jnp.zeros((2, SUBLANES, D_STATE), F32)
    outs = []
    t0 = 0
    after = (carry, carry)
    for nt in steps:
        u_sb, q, k, v, ga, gb = _in_proj(x, positions, g_mix[i], w_in_b, t0, nt, after)
        k_all = lax.dynamic_update_slice(k_all, k, (0, t0, 0))
        v_all = lax.dynamic_update_slice(v_all, v, (0, t0, 0))
        ys, carry = _s5(u_sb, carry, tables, d_skip, w_glu_b, B)
        att = _moba(q, k_all, v_all, t0 // MOBA_BLOCK)
        x1, hq, scores = _merge(x, ys, att, ga, gb, t0, *merge_w)
        idx, gates = _topk(scores)
        after = (gates, outs[-2] if len(outs) > 1 else carry)
        peer_out = _peer(idx, hq, gates, u_words, v_words)
        outs.append(_final(x1, peer_out, p[i], t0, nt, *final_w))
        t0 += nt
    return jnp.concatenate(outs, axis=1)
```

```python
import functools
import math

import jax
import jax.numpy as jnp
from jax import lax
from jax.experimental import pallas as pl
from jax.experimental.pallas import tpu as pltpu
from jax.experimental.pallas import tpu_sc as plsc

F32 = jnp.float32
BF16 = jnp.bfloat16

D_MODEL = 1024
D_SSM = 512
SSM_GROUP = 16
SSM_GROUPS = 32
SSM_STATE = 64
D_STATE = SSM_GROUPS * SSM_STATE
N_HEADS = 8
HEAD_DIM = 64
D_ATT = 512
ROT_DIM = 16
ROPE_THETA = 500000.0
MOBA_BLOCK = 256
MOBA_TOPK = 3
PEER_HEADS = 8
PEER_KEYS = 128
PEER_QDIM = 256
PEER_HALF = 128
PEER_TOPK = 16
PEER_SEL = PEER_HEADS * PEER_TOPK
D_PLE = 256
EPS = 1e-6
NEG = -1e30
LANES = 128
SUBLANES = 8
VMEM_LIMIT = 48 * 1024 * 1024
HIGHEST = lax.Precision.HIGHEST


def _rms(x, g):
    return x * lax.rsqrt(jnp.mean(x * x, axis=-1, keepdims=True) + EPS) * g


def _dot(a, b):
    return jnp.dot(a, b, preferred_element_type=F32)


def _dot_nt(a, b, precision=None):
    return lax.dot_general(a, b, (((1,), (1,)), ((), ())), precision=precision,
                           preferred_element_type=F32)


IN_TS = 512


def _in_proj_kernel(x_ref, pos_ref, g_ref, w_ref, invf_ref, after_a, after_b,
                    u_ref, q_ref, k_ref, v_ref, ga_ref, gb_ref):
    del after_a, after_b
    h = _rms(x_ref[...], g_ref[...]).astype(BF16)

    def proj(lo, hi):
        return _dot(h, w_ref[:, lo:hi])

    u_ref[...] = proj(0, D_SSM).astype(BF16)
    ang = pos_ref[...].astype(F32) * invf_ref[...]
    cos = jnp.cos(ang)
    sin = jnp.sin(ang)
    lane = lax.broadcasted_iota(jnp.int32, (1, LANES), 1) % HEAD_DIM
    half = ROT_DIM // 2
    sin_hi = jnp.where((lane >= half) & (lane < ROT_DIM), sin, 0.0)
    sin_lo = jnp.where(lane < half, -sin, 0.0)
    reps = D_ATT // LANES
    cos4 = jnp.concatenate([cos] * reps, axis=1)
    sin_hi4 = jnp.concatenate([sin_hi] * reps, axis=1)
    sin_lo4 = jnp.concatenate([sin_lo] * reps, axis=1)

    def rope(t):
        return (t * cos4 + pltpu.roll(t, half, 1) * sin_hi4
                + pltpu.roll(t, D_ATT - half, 1) * sin_lo4)

    q = rope(proj(D_SSM, D_SSM + D_ATT))
    q_ref[...] = (q * (HEAD_DIM ** -0.5)).astype(BF16)
    k_ref[...] = rope(proj(D_SSM + D_ATT, D_SSM + 2 * D_ATT)).astype(BF16)
    v_ref[...] = proj(D_SSM + 2 * D_ATT, D_SSM + 3 * D_ATT).astype(BF16)
    o = D_SSM + 3 * D_ATT
    ga_ref[...] = jax.nn.sigmoid(proj(o, o + D_MODEL)).astype(BF16)
    gb_ref[...] = jax.nn.sigmoid(proj(o + D_MODEL, o + 2 * D_MODEL)).astype(BF16)


def _in_proj(x, positions, g_mix, w_in, t0, nt, after):
    B, S, _ = x.shape
    ts = min(IN_TS, nt)
    assert nt % ts == 0 and t0 % ts == 0
    i0 = t0 // ts
    inv_freq = ROPE_THETA ** (-jnp.arange(0, ROT_DIM, 2, dtype=F32) / ROT_DIM)
    lane = jnp.arange(LANES) % HEAD_DIM
    invf = jnp.where(lane < ROT_DIM, inv_freq[lane % (ROT_DIM // 2)], 0.0).reshape(1, LANES)
    d_in = w_in.shape[1]
    src = lambda d: pl.BlockSpec((None, ts, d), lambda b, i: (b, i0 + i, 0))
    tok = lambda d: pl.BlockSpec((None, ts, d), lambda b, i: (b, i, 0))
    full = lambda shape: pl.BlockSpec(shape, lambda b, i: (0,) * len(shape))
    return pl.pallas_call(
        _in_proj_kernel,
        grid=(B, nt // ts),
        in_specs=[src(D_MODEL), src(1), full((1, D_MODEL)), full((D_MODEL, d_in)), full((1, LANES)),
                  pl.BlockSpec(memory_space=pl.ANY), pl.BlockSpec(memory_space=pl.ANY)],
        out_specs=[pl.BlockSpec((ts, D_SSM), lambda b, i: (i, b)),
                   tok(D_ATT), tok(D_ATT), tok(D_ATT), tok(D_MODEL), tok(D_MODEL)],
        out_shape=[jax.ShapeDtypeStruct((nt, B * D_SSM), BF16),
                   jax.ShapeDtypeStruct((B, nt, D_ATT), BF16),
                   jax.ShapeDtypeStruct((B, nt, D_ATT), BF16),
                   jax.ShapeDtypeStruct((B, nt, D_ATT), BF16),
                   jax.ShapeDtypeStruct((B, nt, D_MODEL), BF16),
                   jax.ShapeDtypeStruct((B, nt, D_MODEL), BF16)],
        compiler_params=pltpu.CompilerParams(
            dimension_semantics=("parallel", "parallel"), vmem_limit_bytes=VMEM_LIMIT),
        name="in_proj",
    )(x, positions.reshape(B, S, 1), g_mix.reshape(1, D_MODEL), w_in, invf, *after)


S5_TS = 128
S5_BATCH = 4
S5_COLS = 512


def _s5_kernel(u_ref, c0_ref, bre_ref, bim_ref, a1r_ref, a1i_ref, pr_ref, pi_ref,
               cre_ref, cim_ref, d_ref, wglu_ref, y_ref, c1_ref,
               xr, xi, cr, ci, ysc):
    rows = xr.shape[0]
    ts = rows // S5_BATCH

    @pl.when(pl.program_id(0) == 0)
    def _():
        cr[...] = c0_ref[0]
        ci[...] = c0_ref[1]

    u = u_ref[...]
    xr[...] = _dot(u, bre_ref[...])
    xi[...] = _dot(u, bim_ref[...])

    hi_rows = lax.broadcasted_iota(jnp.int32, (SUBLANES, S5_COLS), 0) >= S5_BATCH
    for cb in range(D_STATE // S5_COLS):
        sl = slice(cb * S5_COLS, (cb + 1) * S5_COLS)
        a_r, a_i = a1r_ref[:, sl], a1i_ref[:, sl]
        p_r, p_i = pr_ref[:, sl], pi_ref[:, sl]

        def body(t, carry):
            c_r, c_i = carry
            r0 = pl.multiple_of(t * SUBLANES, SUBLANES)
            x_r = xr[pl.ds(r0, SUBLANES), sl]
            x_i = xi[pl.ds(r0, SUBLANES), sl]
            s_r = pltpu.roll(x_r, S5_BATCH, 0)
            s_i = pltpu.roll(x_i, S5_BATCH, 0)
            h_r = x_r + (a_r * s_r - a_i * s_i) + (p_r * c_r - p_i * c_i)
            h_i = x_i + (a_r * s_i + a_i * s_r) + (p_r * c_i + p_i * c_r)
            xr[pl.ds(r0, SUBLANES), sl] = h_r
            xi[pl.ds(r0, SUBLANES), sl] = h_i
            n_r = jnp.where(hi_rows, h_r, pltpu.roll(h_r, S5_BATCH, 0))
            n_i = jnp.where(hi_rows, h_i, pltpu.roll(h_i, S5_BATCH, 0))
            return n_r, n_i

        c_r, c_i = lax.fori_loop(0, rows // SUBLANES, body, (cr[:, sl], ci[:, sl]), unroll=2)
        cr[:, sl] = c_r
        ci[:, sl] = c_i

    y = (_dot(xr[...].astype(BF16), cre_ref[...]) - _dot(xi[...].astype(BF16), cim_ref[...])
         + d_ref[...] * u.astype(F32))
    y = jax.nn.gelu(y)
    y = y * jax.nn.sigmoid(_dot(y.astype(BF16), wglu_ref[...]))
    for c in range(D_SSM // LANES):
        ysc[c] = y[:, c * LANES:(c + 1) * LANES]
    for b in range(S5_BATCH):
        for c in range(D_SSM // LANES):
            y_ref[b, :, c * LANES:(c + 1) * LANES] = (
                ysc[c, pl.ds(b, ts, stride=S5_BATCH), :].astype(BF16))

    @pl.when(pl.program_id(0) == pl.num_programs(0) - 1)
    def _():
        c1_ref[0] = cr[...]
        c1_ref[1] = ci[...]


def _s5_tables(log_dt, a_re, a_im, b_re, b_im, c_re, c_im):
    dt = jnp.exp(log_dt.astype(F32))[:, None]
    ar, ai = a_re.astype(F32), a_im.astype(F32)
    mag = jnp.exp(dt * ar)
    abar_re, abar_im = mag * jnp.cos(dt * ai), mag * jnp.sin(dt * ai)
    den = ar * ar + ai * ai
    nr, ni = abar_re - 1.0, abar_im
    f_re = (nr * ar + ni * ai) / den
    f_im = (ni * ar - nr * ai) / den
    br, bi = b_re.astype(F32), b_im.astype(F32)
    bb_re = f_re[..., None] * br - f_im[..., None] * bi
    bb_im = f_re[..., None] * bi + f_im[..., None] * br
    eye = jnp.eye(SSM_GROUPS, dtype=F32)

    def in_blockdiag(bb):
        return jnp.einsum('gnc,gh->gchn', bb, eye).reshape(D_SSM, D_STATE)

    def out_blockdiag(c):
        return jnp.einsum('gcn,gh->gnhc', c.astype(F32), eye).reshape(D_STATE, D_SSM)

    a_r = abar_re.reshape(1, D_STATE)
    a_i = abar_im.reshape(1, D_STATE)
    a2_r = a_r * a_r - a_i * a_i
    a2_i = 2.0 * a_r * a_i
    hi = (jnp.arange(SUBLANES) >= S5_BATCH)[:, None]
    a1r = jnp.where(hi, a_r, 0.0)
    a1i = jnp.where(hi, a_i, 0.0)
    p_r = jnp.where(hi, a2_r, a_r)
    p_i = jnp.where(hi, a2_i, a_i)
    return (in_blockdiag(bb_re).astype(BF16), in_blockdiag(bb_im).astype(BF16),
            a1r, a1i, p_r, p_i,
            out_blockdiag(c_re).astype(BF16), out_blockdiag(c_im).astype(BF16))


def _s5(u_sb, carry, tables, d_skip, w_glu, B):
    assert B == S5_BATCH
    nt = u_sb.shape[0]
    ts = min(S5_TS, nt)
    rows = ts * B
    bre, bim, a1r, a1i, p_r, p_i, cre, cim = tables
    full = lambda shape: pl.BlockSpec(shape, lambda i: (0,) * len(shape))
    return pl.pallas_call(
        _s5_kernel,
        grid=(nt // ts,),
        in_specs=[pl.BlockSpec((rows, D_SSM), lambda i: (i, 0)),
                  full((2, SUBLANES, D_STATE)),
                  full((D_SSM, D_STATE)), full((D_SSM, D_STATE)),
                  full((SUBLANES, D_STATE)), full((SUBLANES, D_STATE)),
                  full((SUBLANES, D_STATE)), full((SUBLANES, D_STATE)),
                  full((D_STATE, D_SSM)), full((D_STATE, D_SSM)),
                  full((1, D_SSM)), full((D_SSM, D_SSM))],
        out_specs=[pl.BlockSpec((B, ts, D_SSM), lambda i: (0, i, 0)),
                   full((2, SUBLANES, D_STATE))],
        out_shape=[jax.ShapeDtypeStruct((B, nt, D_SSM), BF16),
                   jax.ShapeDtypeStruct((2, SUBLANES, D_STATE), F32)],
        scratch_shapes=[pltpu.VMEM((rows, D_STATE), F32), pltpu.VMEM((rows, D_STATE), F32),
                        pltpu.VMEM((SUBLANES, D_STATE), F32), pltpu.VMEM((SUBLANES, D_STATE), F32),
                        pltpu.VMEM((D_SSM // LANES, rows, LANES), F32)],
        compiler_params=pltpu.CompilerParams(
            dimension_semantics=("arbitrary",), vmem_limit_bytes=VMEM_LIMIT),
        name="s5",
    )(u_sb.reshape(nt * B, D_SSM), carry, bre, bim, a1r, a1i, p_r, p_i, cre, cim, d_skip, w_glu)


MOBA_PAIR = 2 * MOBA_BLOCK


def _moba_kernel(q0, q_ref, k_ref, v_ref, o_ref, kmean, kaug_a, kaug_b, vaug_a, vaug_b, m_s, acc_s,
                 s_buf):
    qi = pl.program_id(2) + q0
    nb = k_ref.shape[0] // MOBA_BLOCK
    nbp = kmean.shape[0]
    lane = lax.broadcasted_iota(jnp.int32, (1, LANES), 1)
    head_a = lane < HEAD_DIM

    @pl.when(pl.program_id(2) == 0)
    def _():
        kmean[...] = jnp.zeros_like(kmean)
        for j in range(nb):
            rows = pl.ds(j * MOBA_BLOCK, MOBA_BLOCK)
            kj = k_ref[rows, :].astype(F32)
            vj = v_ref[rows, :].astype(F32)
            kmean[j:j + 1, :] = jnp.sum(kj, axis=0, keepdims=True) * (1.0 / MOBA_BLOCK)
            kaug_a[rows, :] = jnp.where(head_a, kj, jnp.where(lane - HEAD_DIM == j, 1.0, 0.0)).astype(BF16)
            kaug_b[rows, :] = jnp.where(head_a, jnp.where(lane == j, 1.0, 0.0), kj).astype(BF16)
            vaug_a[rows, :] = jnp.where(head_a, vj, 1.0).astype(BF16)
            vaug_b[rows, :] = jnp.where(head_a, 1.0, vj).astype(BF16)

    qf = q_ref[...].astype(F32)
    blk_row = lax.broadcasted_iota(jnp.int32, (nbp, MOBA_BLOCK), 0)
    q_augs = []
    for is_a in (True, False):
        mine = head_a if is_a else jnp.logical_not(head_a)
        q_own = jnp.where(mine, qf, 0.0)
        g = _dot_nt(kmean[...], q_own, precision=HIGHEST)
        g = jnp.where(blk_row < qi, g, NEG)
        sel = jnp.zeros(g.shape, F32)
        for _ in range(MOBA_TOPK):
            m = jnp.max(g, axis=0, keepdims=True)
            idx = jnp.min(jnp.where(g == m, blk_row, nbp), axis=0, keepdims=True)
            hit = blk_row == idx
            sel = jnp.where(hit, jnp.where(idx < qi, 1.0, 0.0), sel)
            g = jnp.where(hit, -jnp.inf, g)
        bias_t = jnp.where(sel > 0.0, 0.0, jnp.where(blk_row == qi, 0.0, NEG))
        bias_t = jnp.concatenate([bias_t, jnp.full((LANES - nbp, MOBA_BLOCK), NEG, F32)], axis=0)
        bias = jnp.transpose(bias_t)
        if is_a:
            bias = pltpu.roll(bias, HEAD_DIM, 1)
        q_augs.append(jnp.where(mine, qf, bias).astype(BF16))

    m_s[...] = jnp.full(m_s.shape, -jnp.inf, F32)
    acc_s[...] = jnp.zeros_like(acc_s)
    qpos = qi * MOBA_BLOCK + lax.broadcasted_iota(jnp.int32, (MOBA_BLOCK, MOBA_PAIR), 0)
    col = lax.broadcasted_iota(jnp.int32, (MOBA_BLOCK, MOBA_PAIR), 1)

    def kv_rows(jj):
        return pl.ds(pl.multiple_of(jj * MOBA_PAIR, MOBA_PAIR), MOBA_PAIR)

    def scores(jj, slot):
        for hd, kaug in enumerate((kaug_a, kaug_b)):
            s_buf[slot, hd] = _dot_nt(q_augs[hd], kaug[kv_rows(jj), :])

    def softmax_pv(jj, slot, causal):
        for hd, vaug in enumerate((vaug_a, vaug_b)):
            s = s_buf[slot, hd]
            if causal:
                s = jnp.where(jj * MOBA_PAIR + col <= qpos, s, NEG)
            m_old = m_s[hd]
            m_new = jnp.maximum(m_old, jnp.max(s, axis=-1, keepdims=True))
            alpha = jnp.exp(m_old - m_new)
            p = jnp.exp(s - m_new)
            m_s[hd] = m_new
            acc_s[hd] = alpha * acc_s[hd] + _dot(p.astype(BF16), vaug[kv_rows(jj), :])

    last = qi // 2
    scores(0, 0)

    def body(k, _):
        scores(2 * k + 1, 1)
        softmax_pv(2 * k, 0, False)
        scores(2 * k + 2, 0)
        softmax_pv(2 * k + 1, 1, False)
        return 0

    lax.fori_loop(0, last // 2, body, 0)

    @pl.when(last % 2 == 0)
    def _():
        softmax_pv(last, 0, True)

    @pl.when(last % 2 == 1)
    def _():
        scores(last, 1)
        softmax_pv(last - 1, 0, False)
        softmax_pv(last, 1, True)
    acc_a, acc_b = acc_s[0], acc_s[1]
    o_ref[...] = jnp.where(head_a, acc_a / pltpu.roll(acc_a, HEAD_DIM, 1),
                           acc_b / pltpu.roll(acc_b, HEAD_DIM, 1)).astype(BF16)


def _moba(q, k, v, q0):
    B = q.shape[0]
    nq = q.shape[1] // MOBA_BLOCK
    skv = (q0 + nq) * MOBA_BLOCK
    nb = skv // MOBA_BLOCK
    assert nb <= HEAD_DIM and nb % 2 == 0 and skv <= k.shape[1]
    nbp = -(-nb // SUBLANES) * SUBLANES
    blk = pl.BlockSpec((None, MOBA_BLOCK, LANES), lambda b, h, i: (b, i, h))
    seq = pl.BlockSpec((None, skv, LANES), lambda b, h, i: (b, 0, h))
    return pl.pallas_call(
        functools.partial(_moba_kernel, q0),
        grid=(B, D_ATT // LANES, nq),
        in_specs=[blk, seq, seq],
        out_specs=blk,
        out_shape=jax.ShapeDtypeStruct(q.shape, BF16),
        scratch_shapes=[pltpu.VMEM((nbp, LANES), F32),
                        pltpu.VMEM((skv, LANES), BF16), pltpu.VMEM((skv, LANES), BF16),
                        pltpu.VMEM((skv, LANES), BF16), pltpu.VMEM((skv, LANES), BF16),
                        pltpu.VMEM((2, MOBA_BLOCK, 1), F32),
                        pltpu.VMEM((2, MOBA_BLOCK, LANES), F32),
                        pltpu.VMEM((2, 2, MOBA_BLOCK, MOBA_PAIR), F32)],
        compiler_params=pltpu.CompilerParams(
            dimension_semantics=("parallel", "parallel", "arbitrary"), vmem_limit_bytes=VMEM_LIMIT),
        name="moba",
    )(q, k, v)


MERGE_TS = 256


def _merge_kernel(x_ref, ys_ref, at_ref, ga_ref, gb_ref, wa_ref, wb_ref, wo_ref, g_ref,
                  wq_ref, k1_ref, k2_ref, x1_ref, hq_ref, sc_ref):
    ya = _dot(ys_ref[...], wa_ref[...])
    yb = _dot(at_ref[...], wb_ref[...])
    merged = ga_ref[...].astype(F32) * ya + gb_ref[...].astype(F32) * yb
    x1 = x_ref[...] + _dot(merged.astype(BF16), wo_ref[...])
    x1_ref[...] = x1
    hq = _rms(x1, g_ref[...])
    hq_ref[...] = hq
    qp = _dot(hq.astype(BF16), wq_ref[...])
    for h in range(PEER_HEADS):
        o = h * PEER_QDIM
        sc_ref[2 * h] = _dot_nt(k1_ref[h], qp[:, o:o + PEER_HALF], precision=HIGHEST)
        sc_ref[2 * h + 1] = _dot_nt(k2_ref[h], qp[:, o + PEER_HALF:o + PEER_QDIM], precision=HIGHEST)


def _merge(x, ys, att, ga, gb, t0, w_proj_ssm, w_proj_att, w_out, g_ffn, peer_w_q, keys1, keys2):
    B, nt = ys.shape[0], ys.shape[1]
    ts = min(MERGE_TS, nt)
    nblk = nt // ts
    i0 = t0 // ts
    tok = lambda d: pl.BlockSpec((None, ts, d), lambda b, i: (b, i, 0))
    row = lambda d: pl.BlockSpec((ts, d), lambda b, i: (b * nblk + i, 0))
    full = lambda shape: pl.BlockSpec(shape, lambda b, i: (0,) * len(shape))
    qd = PEER_HEADS * PEER_QDIM
    return pl.pallas_call(
        _merge_kernel,
        grid=(B, nblk),
        in_specs=[pl.BlockSpec((None, ts, D_MODEL), lambda b, i: (b, i0 + i, 0)),
                  tok(D_SSM), tok(D_ATT), tok(D_MODEL), tok(D_MODEL),
                  full((D_SSM, D_MODEL)), full((D_ATT, D_MODEL)), full((D_MODEL, D_MODEL)),
                  full((1, D_MODEL)), full((D_MODEL, qd)),
                  full((PEER_HEADS, PEER_KEYS, PEER_HALF)), full((PEER_HEADS, PEER_KEYS, PEER_HALF))],
        out_specs=[row(D_MODEL), row(D_MODEL),
                   pl.BlockSpec((2 * PEER_HEADS, PEER_KEYS, ts), lambda b, i: (0, 0, b * nblk + i))],
        out_shape=[jax.ShapeDtypeStruct((B * nt, D_MODEL), F32),
                   jax.ShapeDtypeStruct((B * nt, D_MODEL), F32),
                   jax.ShapeDtypeStruct((2 * PEER_HEADS, PEER_KEYS, B * nt), F32)],
        compiler_params=pltpu.CompilerParams(
            dimension_semantics=("parallel", "parallel"), vmem_limit_bytes=VMEM_LIMIT),
        name="merge",
    )(x, ys, att, ga, gb, w_proj_ssm, w_proj_att, w_out, g_ffn, peer_w_q, keys1, keys2)


TOPK_TS = 256


def _top_rows(s, row, k):
    vals, idxs = [], []
    for _ in range(k):
        m = jnp.max(s, axis=0, keepdims=True)
        idx = jnp.min(jnp.where(s == m, row, s.shape[0]), axis=0, keepdims=True)
        vals.append(m)
        idxs.append(idx)
        s = jnp.where(row == idx, -jnp.inf, s)
    return vals, idxs


def _stack_rows(rows, row16):
    acc = jnp.zeros(row16.shape, rows[0].dtype)
    for r, v in enumerate(rows):
        acc = jnp.where(row16 == r, v, acc)
    return acc


def _topk_kernel(sc_ref, idx_ref, gate_ref):
    ts = sc_ref.shape[-1]
    row = lax.broadcasted_iota(jnp.int32, (PEER_KEYS, ts), 0)
    row16 = lax.broadcasted_iota(jnp.int32, (PEER_TOPK, ts), 0)
    row8 = lax.broadcasted_iota(jnp.int32, (SUBLANES, ts), 0)
    counts = [PEER_TOPK // (i + 1) for i in range(PEER_TOPK)]
    heights = [PEER_TOPK if c > SUBLANES else SUBLANES for c in counts]
    n_cand = sum(heights)
    rowc = lax.broadcasted_iota(jnp.int32, (n_cand, ts), 0)
    gate_rows, eid_rows = [], []
    for h in range(PEER_HEADS):
        v1, i1 = _top_rows(sc_ref[2 * h], row, PEER_TOPK)
        v2, i2 = _top_rows(sc_ref[2 * h + 1], row, PEER_TOPK)
        v2s = _stack_rows(v2, row16)
        i2s = _stack_rows(i2, row16).astype(F32)
        cand, eid = [], []
        for i in range(PEER_TOPK):
            n = heights[i]
            cand.append(jnp.where((row16 if n == PEER_TOPK else row8) < counts[i],
                                  v1[i] + v2s[:n], -jnp.inf))
            eid.append(i1[i].astype(F32) * PEER_KEYS + i2s[:n])
        cand = jnp.concatenate(cand, axis=0)
        eid = jnp.concatenate(eid, axis=0)
        tops, picks = [], []
        for _ in range(PEER_TOPK):
            m = jnp.max(cand, axis=0, keepdims=True)
            pos = jnp.min(jnp.where(cand == m, rowc, n_cand), axis=0, keepdims=True)
            hit = rowc == pos
            picks.append(jnp.max(jnp.where(hit, eid, -1.0), axis=0, keepdims=True))
            tops.append(m)
            cand = jnp.where(hit, -jnp.inf, cand)
        top = _stack_rows(tops, row16)
        p = jnp.exp(top - jnp.max(top, axis=0, keepdims=True))
        gate_rows.append(p / jnp.sum(p, axis=0, keepdims=True))
        eid_rows.append(_stack_rows(picks, row16))
    gate_ref[...] = jnp.transpose(jnp.concatenate(gate_rows, axis=0))
    idx_ref[...] = jnp.transpose(jnp.concatenate(eid_rows, axis=0)).astype(jnp.int32)


def _topk(scores):
    T = scores.shape[-1]
    ts = min(TOPK_TS, T)
    return pl.pallas_call(
        _topk_kernel,
        grid=(T // ts,),
        in_specs=[pl.BlockSpec((2 * PEER_HEADS, PEER_KEYS, ts), lambda i: (0, 0, i))],
        out_specs=[pl.BlockSpec((ts, PEER_SEL), lambda i: (i, 0)),
                   pl.BlockSpec((ts, PEER_SEL), lambda i: (i, 0))],
        out_shape=[jax.ShapeDtypeStruct((T, PEER_SEL), jnp.int32),
                   jax.ShapeDtypeStruct((T, PEER_SEL), F32)],
        compiler_params=pltpu.CompilerParams(
            dimension_semantics=("parallel",), vmem_limit_bytes=VMEM_LIMIT),
        name="topk",
    )(scores)


SC_CORES = 2
SC_SUBCORES = 16
SC_LANES = 16
SC_WORKERS = SC_CORES * SC_SUBCORES
PEER_CH = SC_LANES
PEER_NCH = PEER_SEL // PEER_CH
PEER_WORDS = D_MODEL // 2
PEER_NWG = PEER_WORDS // SC_LANES
PEER_RING = 4
PEER_QUAD = 4
HI_MASK = -65536
GELU_C = 0.7978845608028654


def _gelu_tanh_via_exp(x):
    z = GELU_C * (x + 0.044715 * (x * x * x))
    t = 1.0 - 2.0 / (jnp.exp(2.0 * z) + 1.0)
    return 0.5 * x * (1.0 + t)


def _unpack_pair(w):
    lo = plsc.bitcast(lax.shift_left(w, 16), F32)
    hi = plsc.bitcast(lax.bitwise_and(w, HI_MASK), F32)
    return lo, hi


def _peer_sc_body(idx_hbm, gate_hbm, h_hbm, uv_hbm, o_hbm,
                  idx_v, gate_v, h_v, buf, out_v, gsem, msem, osem):
    n_tok = o_hbm.shape[0] // SC_WORKERS
    base = (lax.axis_index("s") * SC_CORES + lax.axis_index("c")) * n_tok
    lane = lax.iota(jnp.int32, SC_LANES)
    zero_rows = jnp.zeros((SC_LANES,), jnp.int32)

    def meta_copies(tok, s):
        return (pltpu.make_async_copy(idx_hbm.at[tok], idx_v.at[s], msem.at[s]),
                pltpu.make_async_copy(gate_hbm.at[tok], gate_v.at[s], msem.at[s]),
                pltpu.make_async_copy(h_hbm.at[tok], h_v.at[s], msem.at[s]))

    def gather(slot, rows):
        return pltpu.make_async_copy(uv_hbm.at[rows], buf.at[slot], gsem.at[slot])

    def token(t, carry):
        s = t % 2
        tok = base + t
        nxt = base + jnp.minimum(t + 1, n_tok - 1)
        for cp in meta_copies(nxt, 1 - s):
            cp.start()

        @pl.when(t >= 2)
        def _():
            pltpu.make_async_copy(out_v.at[s], o_hbm.at[tok], osem.at[s]).wait()

        def chunk(c, carry):
            slot = c % PEER_RING
            gather(slot, zero_rows).wait()

            def dot_step(q, accs):
                cols = [pl.ds(pl.multiple_of((q * PEER_QUAD + j) * SC_LANES, SC_LANES), SC_LANES)
                        for j in range(PEER_QUAD)]
                hs = [plsc.bitcast(h_v[s, col], BF16) for col in cols]
                out = []
                for r in range(PEER_CH):
                    p = plsc.bitcast(buf[slot, r, cols[0]], BF16) * hs[0]
                    for j in range(1, PEER_QUAD):
                        p = p + plsc.bitcast(buf[slot, r, cols[j]], BF16) * hs[j]
                    lo, hi = _unpack_pair(plsc.bitcast(p, jnp.int32))
                    out.append(accs[r] + lo + hi)
                return tuple(out)

            accs = lax.fori_loop(0, PEER_NWG // PEER_QUAD, dot_step,
                                 tuple(jnp.zeros((SC_LANES,), F32) for _ in range(PEER_CH)))
            tot = jnp.zeros((SC_LANES,), F32)
            for r in range(PEER_CH):
                tot = jnp.where(lane == r, jnp.sum(accs[r]), tot)
            rows = pl.ds(pl.multiple_of(c * PEER_CH, PEER_CH), PEER_CH)
            wvec = gate_v[s, rows] * _gelu_tanh_via_exp(tot)
            ws = []
            for r in range(PEER_CH):
                w = wvec.at[jnp.full((SC_LANES,), r, jnp.int32)].get(mode="promise_in_bounds")
                ws.append(plsc.pack(w, w, format=plsc.PackFormat.INTERLEAVED,
                                    preferred_element_type=BF16))
            first = c == 0

            @plsc.parallel_loop(0, PEER_NWG, unroll=2)
            def acc_step(g):
                col = pl.ds(pl.multiple_of(g * SC_LANES, SC_LANES), SC_LANES)
                col_v = pl.ds(pl.multiple_of(PEER_WORDS + g * SC_LANES, SC_LANES), SC_LANES)
                o_lo = jnp.where(first, 0.0, out_v[s, col])
                o_hi = jnp.where(first, 0.0, out_v[s, col_v])
                for r0 in range(0, PEER_CH, PEER_QUAD):
                    p = plsc.bitcast(buf[slot, r0, col_v], BF16) * ws[r0]
                    for r in range(r0 + 1, r0 + PEER_QUAD):
                        p = p + plsc.bitcast(buf[slot, r, col_v], BF16) * ws[r]
                    lo, hi = _unpack_pair(plsc.bitcast(p, jnp.int32))
                    o_lo = o_lo + lo
                    o_hi = o_hi + hi
                out_v[s, col] = o_lo
                out_v[s, col_v] = o_hi

            @pl.when(c == PEER_NCH - PEER_RING)
            def _():
                for cp in meta_copies(nxt, 1 - s):
                    cp.wait()

            ahead = c + PEER_RING
            src = jnp.where(ahead < PEER_NCH, s, 1 - s)
            nrows = idx_v[src, pl.ds(pl.multiple_of((ahead % PEER_NCH) * PEER_CH, PEER_CH), PEER_CH)]
            gather(slot, nrows).start()
            return carry

        lax.fori_loop(0, PEER_NCH, chunk, 0)
        pltpu.make_async_copy(out_v.at[s], o_hbm.at[tok], osem.at[s]).start()
        return carry

    for cp in meta_copies(base, 0):
        cp.start()
    for cp in meta_copies(base, 0):
        cp.wait()
    for c in range(PEER_RING):
        gather(c, idx_v[0, pl.ds(c * PEER_CH, PEER_CH)]).start()
    lax.fori_loop(0, n_tok, token, 0)
    for c in range(PEER_RING):
        gather(c, zero_rows).wait()
    for s in range(2):
        pltpu.make_async_copy(out_v.at[s], o_hbm.at[base], osem.at[s]).wait()


def _pack_bf16_pairs(tab):
    b = lax.bitcast_convert_type(tab.astype(BF16), jnp.uint16).astype(jnp.uint32)
    half = tab.shape[1] // 2
    return lax.bitcast_convert_type(b[:, :half] | (b[:, half:] << 16), jnp.int32)


def _peer(idx, hq, gates, uv_words):
    T = hq.shape[0]
    assert T % (2 * SC_WORKERS) == 0
    mesh = plsc.VectorSubcoreMesh(core_axis_name="c", subcore_axis_name="s",
                                  num_cores=SC_CORES, num_subcores=SC_SUBCORES)
    return pl.kernel(
        _peer_sc_body,
        out_type=jax.ShapeDtypeStruct((T, D_MODEL), F32),
        mesh=mesh,
        scratch_types=[
            pltpu.VMEM((2, PEER_SEL), jnp.int32), pltpu.VMEM((2, PEER_SEL), F32),
            pltpu.VMEM((2, PEER_WORDS), jnp.int32),
            pltpu.VMEM((PEER_RING, PEER_CH, 2 * PEER_WORDS), jnp.int32),
            pltpu.VMEM((2, D_MODEL), F32),
            pltpu.SemaphoreType.DMA((PEER_RING,)),
            pltpu.SemaphoreType.DMA((2,)), pltpu.SemaphoreType.DMA((2,)),
        ],
        compiler_params=pltpu.CompilerParams(needs_layout_passes=False),
        name="peer_sc",
    )(idx, gates, _pack_bf16_pairs(hq), uv_words)


FINAL_TS = 256


def _final_kernel(x1_ref, pe_ref, p_ref, gp_ref, wg_ref, wp_ref, gf_ref, o_ref):
    x2 = x1_ref[...] + pe_ref[...]
    e = _dot(p_ref[...].astype(BF16), wp_ref[...])
    gate = jax.nn.sigmoid(_dot(_rms(x2, gp_ref[...]).astype(BF16), wg_ref[...]))
    o_ref[...] = _rms(x2 + gate * e, gf_ref[...])


def _final(x1, peer_out, p, t0, nt, g_ple, ple_w_gate, ple_w_proj, g_final):
    B = p.shape[0]
    ts = min(FINAL_TS, nt)
    nblk = nt // ts
    i0 = t0 // ts
    row = lambda d: pl.BlockSpec((ts, d), lambda b, i: (b * nblk + i, 0))
    full = lambda shape: pl.BlockSpec(shape, lambda b, i: (0,) * len(shape))
    return pl.pallas_call(
        _final_kernel,
        grid=(B, nblk),
        in_specs=[row(D_MODEL), row(D_MODEL),
                  pl.BlockSpec((None, ts, D_PLE), lambda b, i: (b, i0 + i, 0)),
                  full((1, D_MODEL)), full((D_MODEL, D_MODEL)), full((D_PLE, D_MODEL)),
                  full((1, D_MODEL))],
        out_specs=pl.BlockSpec((None, ts, D_MODEL), lambda b, i: (b, i, 0)),
        out_shape=jax.ShapeDtypeStruct((B, nt, D_MODEL), F32),
        compiler_params=pltpu.CompilerParams(
            dimension_semantics=("parallel", "parallel"), vmem_limit_bytes=VMEM_LIMIT),
        name="final",
    )(x1, peer_out, p, g_ple, ple_w_gate, ple_w_proj, g_final)


CHUNK_STEPS = (512, 512, 1024, 1024, 1024, 1024, 1024, 1024, 512, 512)


def kernel(x, p, positions, g_mix, w_in, ssm_log_dt, ssm_a_re, ssm_a_im, ssm_b_re, ssm_b_im,
           ssm_c_re, ssm_c_im, ssm_d, ssm_w_glu, w_proj_ssm, w_proj_att, w_out, g_ffn,
           peer_w_q, peer_keys1, peer_keys2, peer_u, peer_v, g_ple, ple_w_gate, ple_w_proj,
           g_final):
    B, S, _ = x.shape
    assert w_in.shape[0] == 1, "the final rmsnorm is fused into the single layer's last stage"
    steps = CHUNK_STEPS if sum(CHUNK_STEPS) == S else (S,)
    i = 0
    tables = _s5_tables(ssm_log_dt[i], ssm_a_re[i], ssm_a_im[i], ssm_b_re[i], ssm_b_im[i],
                        ssm_c_re[i], ssm_c_im[i])
    w_in_b, w_glu_b = w_in[i].astype(BF16), ssm_w_glu[i].astype(BF16)
    d_skip = ssm_d[i].reshape(1, D_SSM).astype(F32)
    merge_w = (w_proj_ssm[i].astype(BF16), w_proj_att[i].astype(BF16), w_out[i].astype(BF16),
               g_ffn[i].reshape(1, D_MODEL), peer_w_q[i].astype(BF16), peer_keys1[i], peer_keys2[i])
    final_w = (g_ple[i].reshape(1, D_MODEL), ple_w_gate[i].astype(BF16),
               ple_w_proj[i].astype(BF16), g_final.reshape(1, D_MODEL))
    uv_words = jnp.concatenate([_pack_bf16_pairs(peer_u[i]), _pack_bf16_pairs(peer_v[i])], axis=1)
    k_all = jnp.zeros((B, S, D_ATT), BF16)
    v_all = jnp.zeros((B, S, D_ATT), BF16)
    carry = jnp.zeros((2, SUBLANES, D_STATE), F32)
    outs = []
    t0 = 0
    after = (carry, carry)
    for nt in steps:
        u_sb, q, k, v, ga, gb = _in_proj(x, positions, g_mix[i], w_in_b, t0, nt, after)
        k_all = lax.dynamic_update_slice(k_all, k, (0, t0, 0))
        v_all = lax.dynamic_update_slice(v_all, v, (0, t0, 0))
        ys, carry = _s5(u_sb, carry, tables, d_skip, w_glu_b, B)
        att = _moba(q, k_all, v_all, t0 // MOBA_BLOCK)
        x1, hq, scores = _merge(x, ys, att, ga, gb, t0, *merge_w)
        idx, gates = _topk(scores)
        after = (gates, outs[-2] if len(outs) > 1 else carry)
        peer_out = _peer(idx, hq, gates, uv_words)
        outs.append(_final(x1, peer_out, p[i], t0, nt, *final_w))
        t0 += nt
    return jnp.concatenate(outs, axis=1)
```

```python
import functools
import math

import jax
import jax.numpy as jnp
from jax import lax
from jax.experimental import pallas as pl
from jax.experimental.pallas import tpu as pltpu
from jax.experimental.pallas import tpu_sc as plsc

F32 = jnp.float32
BF16 = jnp.bfloat16

D_MODEL = 1024
D_SSM = 512
SSM_GROUP = 16
SSM_GROUPS = 32
SSM_STATE = 64
D_STATE = SSM_GROUPS * SSM_STATE
N_HEADS = 8
HEAD_DIM = 64
D_ATT = 512
ROT_DIM = 16
ROPE_THETA = 500000.0
MOBA_BLOCK = 256
MOBA_TOPK = 3
PEER_HEADS = 8
PEER_KEYS = 128
PEER_QDIM = 256
PEER_HALF = 128
PEER_TOPK = 16
PEER_SEL = PEER_HEADS * PEER_TOPK
D_PLE = 256
EPS = 1e-6
NEG = -1e30
LANES = 128
SUBLANES = 8
VMEM_LIMIT = 48 * 1024 * 1024
HIGHEST = lax.Precision.HIGHEST


def _rms(x, g):
    return x * lax.rsqrt(jnp.mean(x * x, axis=-1, keepdims=True) + EPS) * g


def _dot(a, b):
    return jnp.dot(a, b, preferred_element_type=F32)


def _dot_nt(a, b, precision=None):
    return lax.dot_general(a, b, (((1,), (1,)), ((), ())), precision=precision,
                           preferred_element_type=F32)


IN_TS = 512


def _in_proj_kernel(x_ref, pos_ref, g_ref, w_ref, invf_ref, after_a, after_b,
                    u_ref, q_ref, k_ref, v_ref, ga_ref, gb_ref):
    del after_a, after_b
    h = _rms(x_ref[...], g_ref[...]).astype(BF16)

    def proj(lo, hi):
        return _dot(h, w_ref[:, lo:hi])

    u_ref[...] = proj(0, D_SSM).astype(BF16)
    ang = pos_ref[...].astype(F32) * invf_ref[...]
    cos = jnp.cos(ang)
    sin = jnp.sin(ang)
    lane = lax.broadcasted_iota(jnp.int32, (1, LANES), 1) % HEAD_DIM
    half = ROT_DIM // 2
    sin_hi = jnp.where((lane >= half) & (lane < ROT_DIM), sin, 0.0)
    sin_lo = jnp.where(lane < half, -sin, 0.0)
    reps = D_ATT // LANES
    cos4 = jnp.concatenate([cos] * reps, axis=1)
    sin_hi4 = jnp.concatenate([sin_hi] * reps, axis=1)
    sin_lo4 = jnp.concatenate([sin_lo] * reps, axis=1)

    def rope(t):
        return (t * cos4 + pltpu.roll(t, half, 1) * sin_hi4
                + pltpu.roll(t, D_ATT - half, 1) * sin_lo4)

    q = rope(proj(D_SSM, D_SSM + D_ATT))
    q_ref[...] = (q * (HEAD_DIM ** -0.5)).astype(BF16)
    k_ref[...] = rope(proj(D_SSM + D_ATT, D_SSM + 2 * D_ATT)).astype(BF16)
    v_ref[...] = proj(D_SSM + 2 * D_ATT, D_SSM + 3 * D_ATT).astype(BF16)
    o = D_SSM + 3 * D_ATT
    ga_ref[...] = jax.nn.sigmoid(proj(o, o + D_MODEL)).astype(BF16)
    gb_ref[...] = jax.nn.sigmoid(proj(o + D_MODEL, o + 2 * D_MODEL)).astype(BF16)


def _in_proj(x, positions, g_mix, w_in, t0, nt, after):
    B, S, _ = x.shape
    ts = min(IN_TS, nt)
    assert nt % ts == 0 and t0 % ts == 0
    i0 = t0 // ts
    inv_freq = ROPE_THETA ** (-jnp.arange(0, ROT_DIM, 2, dtype=F32) / ROT_DIM)
    lane = jnp.arange(LANES) % HEAD_DIM
    invf = jnp.where(lane < ROT_DIM, inv_freq[lane % (ROT_DIM // 2)], 0.0).reshape(1, LANES)
    d_in = w_in.shape[1]
    src = lambda d: pl.BlockSpec((None, ts, d), lambda b, i: (b, i0 + i, 0))
    tok = lambda d: pl.BlockSpec((None, ts, d), lambda b, i: (b, i, 0))
    full = lambda shape: pl.BlockSpec(shape, lambda b, i: (0,) * len(shape))
    return pl.pallas_call(
        _in_proj_kernel,
        grid=(B, nt // ts),
        in_specs=[src(D_MODEL), src(1), full((1, D_MODEL)), full((D_MODEL, d_in)), full((1, LANES)),
                  pl.BlockSpec(memory_space=pl.ANY), pl.BlockSpec(memory_space=pl.ANY)],
        out_specs=[pl.BlockSpec((ts, D_SSM), lambda b, i: (i, b)),
                   tok(D_ATT), tok(D_ATT), tok(D_ATT), tok(D_MODEL), tok(D_MODEL)],
        out_shape=[jax.ShapeDtypeStruct((nt, B * D_SSM), BF16),
                   jax.ShapeDtypeStruct((B, nt, D_ATT), BF16),
                   jax.ShapeDtypeStruct((B, nt, D_ATT), BF16),
                   jax.ShapeDtypeStruct((B, nt, D_ATT), BF16),
                   jax.ShapeDtypeStruct((B, nt, D_MODEL), BF16),
                   jax.ShapeDtypeStruct((B, nt, D_MODEL), BF16)],
        compiler_params=pltpu.CompilerParams(
            dimension_semantics=("parallel", "parallel"), vmem_limit_bytes=VMEM_LIMIT),
        name="in_proj",
    )(x, positions.reshape(B, S, 1), g_mix.reshape(1, D_MODEL), w_in, invf, *after)


S5_TS = 128
S5_BATCH = 4
S5_COLS = 512


def _s5_kernel(u_ref, c0_ref, bre_ref, bim_ref, a1r_ref, a1i_ref, pr_ref, pi_ref,
               cre_ref, cim_ref, d_ref, wglu_ref, y_ref, c1_ref,
               xr, xi, cr, ci, ysc):
    rows = xr.shape[0]
    ts = rows // S5_BATCH

    @pl.when(pl.program_id(0) == 0)
    def _():
        cr[...] = c0_ref[0]
        ci[...] = c0_ref[1]

    u = u_ref[...]
    xr[...] = _dot(u, bre_ref[...])
    xi[...] = _dot(u, bim_ref[...])

    hi_rows = lax.broadcasted_iota(jnp.int32, (SUBLANES, S5_COLS), 0) >= S5_BATCH
    for cb in range(D_STATE // S5_COLS):
        sl = slice(cb * S5_COLS, (cb + 1) * S5_COLS)
        a_r, a_i = a1r_ref[:, sl], a1i_ref[:, sl]
        p_r, p_i = pr_ref[:, sl], pi_ref[:, sl]

        def body(t, carry):
            c_r, c_i = carry
            r0 = pl.multiple_of(t * SUBLANES, SUBLANES)
            x_r = xr[pl.ds(r0, SUBLANES), sl]
            x_i = xi[pl.ds(r0, SUBLANES), sl]
            s_r = pltpu.roll(x_r, S5_BATCH, 0)
            s_i = pltpu.roll(x_i, S5_BATCH, 0)
            h_r = x_r + (a_r * s_r - a_i * s_i) + (p_r * c_r - p_i * c_i)
            h_i = x_i + (a_r * s_i + a_i * s_r) + (p_r * c_i + p_i * c_r)
            xr[pl.ds(r0, SUBLANES), sl] = h_r
            xi[pl.ds(r0, SUBLANES), sl] = h_i
            n_r = jnp.where(hi_rows, h_r, pltpu.roll(h_r, S5_BATCH, 0))
            n_i = jnp.where(hi_rows, h_i, pltpu.roll(h_i, S5_BATCH, 0))
            return n_r, n_i

        c_r, c_i = lax.fori_loop(0, rows // SUBLANES, body, (cr[:, sl], ci[:, sl]), unroll=2)
        cr[:, sl] = c_r
        ci[:, sl] = c_i

    y = (_dot(xr[...].astype(BF16), cre_ref[...]) - _dot(xi[...].astype(BF16), cim_ref[...])
         + d_ref[...] * u.astype(F32))
    y = jax.nn.gelu(y)
    y = y * jax.nn.sigmoid(_dot(y.astype(BF16), wglu_ref[...]))
    for c in range(D_SSM // LANES):
        ysc[c] = y[:, c * LANES:(c + 1) * LANES]
    for b in range(S5_BATCH):
        for c in range(D_SSM // LANES):
            y_ref[b, :, c * LANES:(c + 1) * LANES] = (
                ysc[c, pl.ds(b, ts, stride=S5_BATCH), :].astype(BF16))

    @pl.when(pl.program_id(0) == pl.num_programs(0) - 1)
    def _():
        c1_ref[0] = cr[...]
        c1_ref[1] = ci[...]


def _s5_tables(log_dt, a_re, a_im, b_re, b_im, c_re, c_im):
    dt = jnp.exp(log_dt.astype(F32))[:, None]
    ar, ai = a_re.astype(F32), a_im.astype(F32)
    mag = jnp.exp(dt * ar)
    abar_re, abar_im = mag * jnp.cos(dt * ai), mag * jnp.sin(dt * ai)
    den = ar * ar + ai * ai
    nr, ni = abar_re - 1.0, abar_im
    f_re = (nr * ar + ni * ai) / den
    f_im = (ni * ar - nr * ai) / den
    br, bi = b_re.astype(F32), b_im.astype(F32)
    bb_re = f_re[..., None] * br - f_im[..., None] * bi
    bb_im = f_re[..., None] * bi + f_im[..., None] * br
    eye = jnp.eye(SSM_GROUPS, dtype=F32)

    def in_blockdiag(bb):
        return jnp.einsum('gnc,gh->gchn', bb, eye).reshape(D_SSM, D_STATE)

    def out_blockdiag(c):
        return jnp.einsum('gcn,gh->gnhc', c.astype(F32), eye).reshape(D_STATE, D_SSM)

    a_r = abar_re.reshape(1, D_STATE)
    a_i = abar_im.reshape(1, D_STATE)
    a2_r = a_r * a_r - a_i * a_i
    a2_i = 2.0 * a_r * a_i
    hi = (jnp.arange(SUBLANES) >= S5_BATCH)[:, None]
    a1r = jnp.where(hi, a_r, 0.0)
    a1i = jnp.where(hi, a_i, 0.0)
    p_r = jnp.where(hi, a2_r, a_r)
    p_i = jnp.where(hi, a2_i, a_i)
    return (in_blockdiag(bb_re).astype(BF16), in_blockdiag(bb_im).astype(BF16),
            a1r, a1i, p_r, p_i,
            out_blockdiag(c_re).astype(BF16), out_blockdiag(c_im).astype(BF16))


def _s5(u_sb, carry, tables, d_skip, w_glu, B):
    assert B == S5_BATCH
    nt = u_sb.shape[0]
    ts = min(S5_TS, nt)
    rows = ts * B
    bre, bim, a1r, a1i, p_r, p_i, cre, cim = tables
    full = lambda shape: pl.BlockSpec(shape, lambda i: (0,) * len(shape))
    return pl.pallas_call(
        _s5_kernel,
        grid=(nt // ts,),
        in_specs=[pl.BlockSpec((rows, D_SSM), lambda i: (i, 0)),
                  full((2, SUBLANES, D_STATE)),
                  full((D_SSM, D_STATE)), full((D_SSM, D_STATE)),
                  full((SUBLANES, D_STATE)), full((SUBLANES, D_STATE)),
                  full((SUBLANES, D_STATE)), full((SUBLANES, D_STATE)),
                  full((D_STATE, D_SSM)), full((D_STATE, D_SSM)),
                  full((1, D_SSM)), full((D_SSM, D_SSM))],
        out_specs=[pl.BlockSpec((B, ts, D_SSM), lambda i: (0, i, 0)),
                   full((2, SUBLANES, D_STATE))],
        out_shape=[jax.ShapeDtypeStruct((B, nt, D_SSM), BF16),
                   jax.ShapeDtypeStruct((2, SUBLANES, D_STATE), F32)],
        scratch_shapes=[pltpu.VMEM((rows, D_STATE), F32), pltpu.VMEM((rows, D_STATE), F32),
                        pltpu.VMEM((SUBLANES, D_STATE), F32), pltpu.VMEM((SUBLANES, D_STATE), F32),
                        pltpu.VMEM((D_SSM // LANES, rows, LANES), F32)],
        compiler_params=pltpu.CompilerParams(
            dimension_semantics=("arbitrary",), vmem_limit_bytes=VMEM_LIMIT),
        name="s5",
    )(u_sb.reshape(nt * B, D_SSM), carry, bre, bim, a1r, a1i, p_r, p_i, cre, cim, d_skip, w_glu)


MOBA_PAIR = 2 * MOBA_BLOCK


def _moba_kernel(q0, q_ref, k_ref, v_ref, o_ref, kmean, kaug_a, kaug_b, vaug_a, vaug_b, m_s, acc_s,
                 s_buf):
    qi = pl.program_id(2) + q0
    nb = k_ref.shape[0] // MOBA_BLOCK
    nbp = kmean.shape[0]
    lane = lax.broadcasted_iota(jnp.int32, (1, LANES), 1)
    head_a = lane < HEAD_DIM

    @pl.when(pl.program_id(2) == 0)
    def _():
        kmean[...] = jnp.zeros_like(kmean)
        for j in range(nb):
            rows = pl.ds(j * MOBA_BLOCK, MOBA_BLOCK)
            kj = k_ref[rows, :].astype(F32)
            vj = v_ref[rows, :].astype(F32)
            kmean[j:j + 1, :] = jnp.sum(kj, axis=0, keepdims=True) * (1.0 / MOBA_BLOCK)
            kaug_a[rows, :] = jnp.where(head_a, kj, jnp.where(lane - HEAD_DIM == j, 1.0, 0.0)).astype(BF16)
            kaug_b[rows, :] = jnp.where(head_a, jnp.where(lane == j, 1.0, 0.0), kj).astype(BF16)
            vaug_a[rows, :] = jnp.where(head_a, vj, 1.0).astype(BF16)
            vaug_b[rows, :] = jnp.where(head_a, 1.0, vj).astype(BF16)

    qf = q_ref[...].astype(F32)
    blk_row = lax.broadcasted_iota(jnp.int32, (nbp, MOBA_BLOCK), 0)
    q_augs = []
    for is_a in (True, False):
        mine = head_a if is_a else jnp.logical_not(head_a)
        q_own = jnp.where(mine, qf, 0.0)
        g = _dot_nt(kmean[...], q_own, precision=HIGHEST)
        g = jnp.where(blk_row < qi, g, NEG)
        sel = jnp.zeros(g.shape, F32)
        for _ in range(MOBA_TOPK):
            m = jnp.max(g, axis=0, keepdims=True)
            idx = jnp.min(jnp.where(g == m, blk_row, nbp), axis=0, keepdims=True)
            hit = blk_row == idx
            sel = jnp.where(hit, jnp.where(idx < qi, 1.0, 0.0), sel)
            g = jnp.where(hit, -jnp.inf, g)
        bias_t = jnp.where(sel > 0.0, 0.0, jnp.where(blk_row == qi, 0.0, NEG))
        bias_t = jnp.concatenate([bias_t, jnp.full((LANES - nbp, MOBA_BLOCK), NEG, F32)], axis=0)
        bias = jnp.transpose(bias_t)
        if is_a:
            bias = pltpu.roll(bias, HEAD_DIM, 1)
        q_augs.append(jnp.where(mine, qf, bias).astype(BF16))

    m_s[...] = jnp.full(m_s.shape, -jnp.inf, F32)
    acc_s[...] = jnp.zeros_like(acc_s)
    qpos = qi * MOBA_BLOCK + lax.broadcasted_iota(jnp.int32, (MOBA_BLOCK, MOBA_PAIR), 0)
    col = lax.broadcasted_iota(jnp.int32, (MOBA_BLOCK, MOBA_PAIR), 1)

    def kv_rows(jj):
        return pl.ds(pl.multiple_of(jj * MOBA_PAIR, MOBA_PAIR), MOBA_PAIR)

    def scores(jj, slot):
        for hd, kaug in enumerate((kaug_a, kaug_b)):
            s_buf[slot, hd] = _dot_nt(q_augs[hd], kaug[kv_rows(jj), :])

    def softmax_pv(jj, slot, causal):
        for hd, vaug in enumerate((vaug_a, vaug_b)):
            s = s_buf[slot, hd]
            if causal:
                s = jnp.where(jj * MOBA_PAIR + col <= qpos, s, NEG)
            m_old = m_s[hd]
            m_new = jnp.maximum(m_old, jnp.max(s, axis=-1, keepdims=True))
            alpha = jnp.exp(m_old - m_new)
            p = jnp.exp(s - m_new)
            m_s[hd] = m_new
            acc_s[hd] = alpha * acc_s[hd] + _dot(p.astype(BF16), vaug[kv_rows(jj), :])

    last = qi // 2
    scores(0, 0)

    def body(k, _):
        scores(2 * k + 1, 1)
        softmax_pv(2 * k, 0, False)
        scores(2 * k + 2, 0)
        softmax_pv(2 * k + 1, 1, False)
        return 0

    lax.fori_loop(0, last // 2, body, 0)

    @pl.when(last % 2 == 0)
    def _():
        softmax_pv(last, 0, True)

    @pl.when(last % 2 == 1)
    def _():
        scores(last, 1)
        softmax_pv(last - 1, 0, False)
        softmax_pv(last, 1, True)
    acc_a, acc_b = acc_s[0], acc_s[1]
    o_ref[...] = jnp.where(head_a, acc_a / pltpu.roll(acc_a, HEAD_DIM, 1),
                           acc_b / pltpu.roll(acc_b, HEAD_DIM, 1)).astype(BF16)


def _moba(q, k, v, q0):
    B = q.shape[0]
    nq = q.shape[1] // MOBA_BLOCK
    skv = (q0 + nq) * MOBA_BLOCK
    nb = skv // MOBA_BLOCK
    assert nb <= HEAD_DIM and nb % 2 == 0 and skv <= k.shape[1]
    nbp = -(-nb // SUBLANES) * SUBLANES
    blk = pl.BlockSpec((None, MOBA_BLOCK, LANES), lambda b, h, i: (b, i, h))
    seq = pl.BlockSpec((None, skv, LANES), lambda b, h, i: (b, 0, h))
    return pl.pallas_call(
        functools.partial(_moba_kernel, q0),
        grid=(B, D_ATT // LANES, nq),
        in_specs=[blk, seq, seq],
        out_specs=blk,
        out_shape=jax.ShapeDtypeStruct(q.shape, BF16),
        scratch_shapes=[pltpu.VMEM((nbp, LANES), F32),
                        pltpu.VMEM((skv, LANES), BF16), pltpu.VMEM((skv, LANES), BF16),
                        pltpu.VMEM((skv, LANES), BF16), pltpu.VMEM((skv, LANES), BF16),
                        pltpu.VMEM((2, MOBA_BLOCK, 1), F32),
                        pltpu.VMEM((2, MOBA_BLOCK, LANES), F32),
                        pltpu.VMEM((2, 2, MOBA_BLOCK, MOBA_PAIR), F32)],
        compiler_params=pltpu.CompilerParams(
            dimension_semantics=("parallel", "parallel", "arbitrary"), vmem_limit_bytes=VMEM_LIMIT),
        name="moba",
    )(q, k, v)


MERGE_TS = 256


def _bf16_bits(x):
    b = pltpu.bitcast(x, jnp.int32)
    r = b + 0x7FFF + (lax.shift_right_logical(b, 16) & 1)
    return lax.shift_right_logical(r, 16)


def _merge_kernel(x_ref, ys_ref, at_ref, ga_ref, gb_ref, wa_ref, wb_ref, wo_ref, g_ref,
                  wq_ref, k1_ref, k2_ref, x1_ref, hw_ref, sc_ref):
    ya = _dot(ys_ref[...], wa_ref[...])
    yb = _dot(at_ref[...], wb_ref[...])
    merged = ga_ref[...].astype(F32) * ya + gb_ref[...].astype(F32) * yb
    x1 = x_ref[...] + _dot(merged.astype(BF16), wo_ref[...])
    x1_ref[...] = x1
    hq = _rms(x1, g_ref[...])
    half = D_MODEL // 2
    hw_ref[...] = _bf16_bits(hq[:, :half]) | lax.shift_left(_bf16_bits(hq[:, half:]), 16)
    qp = _dot(hq.astype(BF16), wq_ref[...])
    for h in range(PEER_HEADS):
        o = h * PEER_QDIM
        sc_ref[2 * h] = _dot_nt(k1_ref[h], qp[:, o:o + PEER_HALF], precision=HIGHEST)
        sc_ref[2 * h + 1] = _dot_nt(k2_ref[h], qp[:, o + PEER_HALF:o + PEER_QDIM], precision=HIGHEST)


def _merge(x, ys, att, ga, gb, t0, w_proj_ssm, w_proj_att, w_out, g_ffn, peer_w_q, keys1, keys2):
    B, nt = ys.shape[0], ys.shape[1]
    ts = min(MERGE_TS, nt)
    nblk = nt // ts
    i0 = t0 // ts
    tok = lambda d: pl.BlockSpec((None, ts, d), lambda b, i: (b, i, 0))
    row = lambda d: pl.BlockSpec((ts, d), lambda b, i: (b * nblk + i, 0))
    full = lambda shape: pl.BlockSpec(shape, lambda b, i: (0,) * len(shape))
    qd = PEER_HEADS * PEER_QDIM
    return pl.pallas_call(
        _merge_kernel,
        grid=(B, nblk),
        in_specs=[pl.BlockSpec((None, ts, D_MODEL), lambda b, i: (b, i0 + i, 0)),
                  tok(D_SSM), tok(D_ATT), tok(D_MODEL), tok(D_MODEL),
                  full((D_SSM, D_MODEL)), full((D_ATT, D_MODEL)), full((D_MODEL, D_MODEL)),
                  full((1, D_MODEL)), full((D_MODEL, qd)),
                  full((PEER_HEADS, PEER_KEYS, PEER_HALF)), full((PEER_HEADS, PEER_KEYS, PEER_HALF))],
        out_specs=[row(D_MODEL), row(D_MODEL // 2),
                   pl.BlockSpec((2 * PEER_HEADS, PEER_KEYS, ts), lambda b, i: (0, 0, b * nblk + i))],
        out_shape=[jax.ShapeDtypeStruct((B * nt, D_MODEL), F32),
                   jax.ShapeDtypeStruct((B * nt, D_MODEL // 2), jnp.int32),
                   jax.ShapeDtypeStruct((2 * PEER_HEADS, PEER_KEYS, B * nt), F32)],
        compiler_params=pltpu.CompilerParams(
            dimension_semantics=("parallel", "parallel"), vmem_limit_bytes=VMEM_LIMIT),
        name="merge",
    )(x, ys, att, ga, gb, w_proj_ssm, w_proj_att, w_out, g_ffn, peer_w_q, keys1, keys2)


TOPK_TS = 256


def _top_rows(s, row, k):
    vals, idxs = [], []
    for _ in range(k):
        m = jnp.max(s, axis=0, keepdims=True)
        idx = jnp.min(jnp.where(s == m, row, s.shape[0]), axis=0, keepdims=True)
        vals.append(m)
        idxs.append(idx)
        s = jnp.where(row == idx, -jnp.inf, s)
    return vals, idxs


def _stack_rows(rows, row16):
    acc = jnp.zeros(row16.shape, rows[0].dtype)
    for r, v in enumerate(rows):
        acc = jnp.where(row16 == r, v, acc)
    return acc


def _topk_kernel(sc_ref, idx_ref, gate_ref):
    ts = sc_ref.shape[-1]
    row = lax.broadcasted_iota(jnp.int32, (PEER_KEYS, ts), 0)
    row16 = lax.broadcasted_iota(jnp.int32, (PEER_TOPK, ts), 0)
    row8 = lax.broadcasted_iota(jnp.int32, (SUBLANES, ts), 0)
    counts = [PEER_TOPK // (i + 1) for i in range(PEER_TOPK)]
    heights = [PEER_TOPK if c > SUBLANES else SUBLANES for c in counts]
    n_cand = sum(heights)
    rowc = lax.broadcasted_iota(jnp.int32, (n_cand, ts), 0)
    gate_rows, eid_rows = [], []
    for h in range(PEER_HEADS):
        v1, i1 = _top_rows(sc_ref[2 * h], row, PEER_TOPK)
        v2, i2 = _top_rows(sc_ref[2 * h + 1], row, PEER_TOPK)
        v2s = _stack_rows(v2, row16)
        i2s = _stack_rows(i2, row16).astype(F32)
        cand, eid = [], []
        for i in range(PEER_TOPK):
            n = heights[i]
            cand.append(jnp.where((row16 if n == PEER_TOPK else row8) < counts[i],
                                  v1[i] + v2s[:n], -jnp.inf))
            eid.append(i1[i].astype(F32) * PEER_KEYS + i2s[:n])
        cand = jnp.concatenate(cand, axis=0)
        eid = jnp.concatenate(eid, axis=0)
        tops, picks = [], []
        for _ in range(PEER_TOPK):
            m = jnp.max(cand, axis=0, keepdims=True)
            pos = jnp.min(jnp.where(cand == m, rowc, n_cand), axis=0, keepdims=True)
            hit = rowc == pos
            picks.append(jnp.max(jnp.where(hit, eid, -1.0), axis=0, keepdims=True))
            tops.append(m)
            cand = jnp.where(hit, -jnp.inf, cand)
        top = _stack_rows(tops, row16)
        p = jnp.exp(top - jnp.max(top, axis=0, keepdims=True))
        gate_rows.append(p / jnp.sum(p, axis=0, keepdims=True))
        eid_rows.append(_stack_rows(picks, row16))
    gate_ref[...] = jnp.transpose(jnp.concatenate(gate_rows, axis=0))
    idx_ref[...] = jnp.transpose(jnp.concatenate(eid_rows, axis=0)).astype(jnp.int32)


def _topk(scores):
    T = scores.shape[-1]
    ts = min(TOPK_TS, T)
    return pl.pallas_call(
        _topk_kernel,
        grid=(T // ts,),
        in_specs=[pl.BlockSpec((2 * PEER_HEADS, PEER_KEYS, ts), lambda i: (0, 0, i))],
        out_specs=[pl.BlockSpec((ts, PEER_SEL), lambda i: (i, 0)),
                   pl.BlockSpec((ts, PEER_SEL), lambda i: (i, 0))],
        out_shape=[jax.ShapeDtypeStruct((T, PEER_SEL), jnp.int32),
                   jax.ShapeDtypeStruct((T, PEER_SEL), F32)],
        compiler_params=pltpu.CompilerParams(
            dimension_semantics=("parallel",), vmem_limit_bytes=VMEM_LIMIT),
        name="topk",
    )(scores)


SC_CORES = 2
SC_SUBCORES = 16
SC_LANES = 16
SC_WORKERS = SC_CORES * SC_SUBCORES
PEER_CH = SC_LANES
PEER_NCH = PEER_SEL // PEER_CH
PEER_WORDS = D_MODEL // 2
PEER_NWG = PEER_WORDS // SC_LANES
PEER_RING = 4
PEER_QUAD = 4
HI_MASK = -65536
GELU_C = 0.7978845608028654


def _gelu_tanh_via_exp(x):
    z = GELU_C * (x + 0.044715 * (x * x * x))
    t = 1.0 - 2.0 / (jnp.exp(2.0 * z) + 1.0)
    return 0.5 * x * (1.0 + t)


def _unpack_pair(w):
    lo = plsc.bitcast(lax.shift_left(w, 16), F32)
    hi = plsc.bitcast(lax.bitwise_and(w, HI_MASK), F32)
    return lo, hi


def _peer_sc_body(idx_hbm, gate_hbm, h_hbm, uv_hbm, o_hbm,
                  idx_v, gate_v, h_v, buf, out_v, gsem, msem, osem):
    n_tok = o_hbm.shape[0] // SC_WORKERS
    base = (lax.axis_index("s") * SC_CORES + lax.axis_index("c")) * n_tok
    lane = lax.iota(jnp.int32, SC_LANES)
    zero_rows = jnp.zeros((SC_LANES,), jnp.int32)

    def meta_copies(tok, s):
        return (pltpu.make_async_copy(idx_hbm.at[tok], idx_v.at[s], msem.at[s]),
                pltpu.make_async_copy(gate_hbm.at[tok], gate_v.at[s], msem.at[s]),
                pltpu.make_async_copy(h_hbm.at[tok], h_v.at[s], msem.at[s]))

    def gather(slot, rows):
        return pltpu.make_async_copy(uv_hbm.at[rows], buf.at[slot], gsem.at[slot])

    def token(t, carry):
        s = t % 2
        tok = base + t
        nxt = base + jnp.minimum(t + 1, n_tok - 1)
        for cp in meta_copies(nxt, 1 - s):
            cp.start()

        @pl.when(t >= 2)
        def _():
            pltpu.make_async_copy(out_v.at[s], o_hbm.at[tok], osem.at[s]).wait()

        def chunk(c, carry):
            slot = c % PEER_RING
            gather(slot, zero_rows).wait()

            def dot_step(q, accs):
                cols = [pl.ds(pl.multiple_of((q * PEER_QUAD + j) * SC_LANES, SC_LANES), SC_LANES)
                        for j in range(PEER_QUAD)]
                hs = [plsc.bitcast(h_v[s, col], BF16) for col in cols]
                out = []
                for r in range(PEER_CH):
                    p = plsc.bitcast(buf[slot, r, cols[0]], BF16) * hs[0]
                    for j in range(1, PEER_QUAD):
                        p = p + plsc.bitcast(buf[slot, r, cols[j]], BF16) * hs[j]
                    lo, hi = _unpack_pair(plsc.bitcast(p, jnp.int32))
                    out.append(accs[r] + lo + hi)
                return tuple(out)

            accs = lax.fori_loop(0, PEER_NWG // PEER_QUAD, dot_step,
                                 tuple(jnp.zeros((SC_LANES,), F32) for _ in range(PEER_CH)))
            tot = jnp.zeros((SC_LANES,), F32)
            for r in range(PEER_CH):
                tot = jnp.where(lane == r, jnp.sum(accs[r]), tot)
            rows = pl.ds(pl.multiple_of(c * PEER_CH, PEER_CH), PEER_CH)
            wvec = gate_v[s, rows] * _gelu_tanh_via_exp(tot)
            ws = []
            for r in range(PEER_CH):
                w = wvec.at[jnp.full((SC_LANES,), r, jnp.int32)].get(mode="promise_in_bounds")
                ws.append(plsc.pack(w, w, format=plsc.PackFormat.INTERLEAVED,
                                    preferred_element_type=BF16))
            first = c == 0

            @plsc.parallel_loop(0, PEER_NWG, unroll=2)
            def acc_step(g):
                col = pl.ds(pl.multiple_of(g * SC_LANES, SC_LANES), SC_LANES)
                col_v = pl.ds(pl.multiple_of(PEER_WORDS + g * SC_LANES, SC_LANES), SC_LANES)
                o_lo = jnp.where(first, 0.0, out_v[s, col])
                o_hi = jnp.where(first, 0.0, out_v[s, col_v])
                for r0 in range(0, PEER_CH, PEER_QUAD):
                    p = plsc.bitcast(buf[slot, r0, col_v], BF16) * ws[r0]
                    for r in range(r0 + 1, r0 + PEER_QUAD):
                        p = p + plsc.bitcast(buf[slot, r, col_v], BF16) * ws[r]
                    lo, hi = _unpack_pair(plsc.bitcast(p, jnp.int32))
                    o_lo = o_lo + lo
                    o_hi = o_hi + hi
                out_v[s, col] = o_lo
                out_v[s, col_v] = o_hi

            @pl.when(c == PEER_NCH - PEER_RING)
            def _():
                for cp in meta_copies(nxt, 1 - s):
                    cp.wait()

            ahead = c + PEER_RING
            src = jnp.where(ahead < PEER_NCH, s, 1 - s)
            nrows = idx_v[src, pl.ds(pl.multiple_of((ahead % PEER_NCH) * PEER_CH, PEER_CH), PEER_CH)]
            gather(slot, nrows).start()
            return carry

        lax.fori_loop(0, PEER_NCH, chunk, 0)
        pltpu.make_async_copy(out_v.at[s], o_hbm.at[tok], osem.at[s]).start()
        return carry

    for cp in meta_copies(base, 0):
        cp.start()
    for cp in meta_copies(base, 0):
        cp.wait()
    for c in range(PEER_RING):
        gather(c, idx_v[0, pl.ds(c * PEER_CH, PEER_CH)]).start()
    lax.fori_loop(0, n_tok, token, 0)
    for c in range(PEER_RING):
        gather(c, zero_rows).wait()
    for s in range(2):
        pltpu.make_async_copy(out_v.at[s], o_hbm.at[base], osem.at[s]).wait()


def _pack_bf16_pairs(tab):
    b = lax.bitcast_convert_type(tab.astype(BF16), jnp.uint16).astype(jnp.uint32)
    half = tab.shape[1] // 2
    return lax.bitcast_convert_type(b[:, :half] | (b[:, half:] << 16), jnp.int32)


def _peer(idx, h_words, gates, uv_words):
    T = h_words.shape[0]
    assert T % (2 * SC_WORKERS) == 0
    mesh = plsc.VectorSubcoreMesh(core_axis_name="c", subcore_axis_name="s",
                                  num_cores=SC_CORES, num_subcores=SC_SUBCORES)
    return pl.kernel(
        _peer_sc_body,
        out_type=jax.ShapeDtypeStruct((T, D_MODEL), F32),
        mesh=mesh,
        scratch_types=[
            pltpu.VMEM((2, PEER_SEL), jnp.int32), pltpu.VMEM((2, PEER_SEL), F32),
            pltpu.VMEM((2, PEER_WORDS), jnp.int32),
            pltpu.VMEM((PEER_RING, PEER_CH, 2 * PEER_WORDS), jnp.int32),
            pltpu.VMEM((2, D_MODEL), F32),
            pltpu.SemaphoreType.DMA((PEER_RING,)),
            pltpu.SemaphoreType.DMA((2,)), pltpu.SemaphoreType.DMA((2,)),
        ],
        compiler_params=pltpu.CompilerParams(needs_layout_passes=False),
        name="peer_sc",
    )(idx, gates, h_words, uv_words)


FINAL_TS = 256


def _final_kernel(x1_ref, pe_ref, p_ref, gp_ref, wg_ref, wp_ref, gf_ref, o_ref):
    x2 = x1_ref[...] + pe_ref[...]
    e = _dot(p_ref[...].astype(BF16), wp_ref[...])
    gate = jax.nn.sigmoid(_dot(_rms(x2, gp_ref[...]).astype(BF16), wg_ref[...]))
    o_ref[...] = _rms(x2 + gate * e, gf_ref[...])


def _final(x1, peer_out, p, t0, nt, g_ple, ple_w_gate, ple_w_proj, g_final):
    B = p.shape[0]
    ts = min(FINAL_TS, nt)
    nblk = nt // ts
    i0 = t0 // ts
    row = lambda d: pl.BlockSpec((ts, d), lambda b, i: (b * nblk + i, 0))
    full = lambda shape: pl.BlockSpec(shape, lambda b, i: (0,) * len(shape))
    return pl.pallas_call(
        _final_kernel,
        grid=(B, nblk),
        in_specs=[row(D_MODEL), row(D_MODEL),
                  pl.BlockSpec((None, ts, D_PLE), lambda b, i: (b, i0 + i, 0)),
                  full((1, D_MODEL)), full((D_MODEL, D_MODEL)), full((D_PLE, D_MODEL)),
                  full((1, D_MODEL))],
        out_specs=pl.BlockSpec((None, ts, D_MODEL), lambda b, i: (b, i, 0)),
        out_shape=jax.ShapeDtypeStruct((B, nt, D_MODEL), F32),
        compiler_params=pltpu.CompilerParams(
            dimension_semantics=("parallel", "parallel"), vmem_limit_bytes=VMEM_LIMIT),
        name="final",
    )(x1, peer_out, p, g_ple, ple_w_gate, ple_w_proj, g_final)


CHUNK_STEPS = (512, 512, 1024, 1024, 1024, 1024, 1024, 1024, 512, 512)


def kernel(x, p, positions, g_mix, w_in, ssm_log_dt, ssm_a_re, ssm_a_im, ssm_b_re, ssm_b_im,
           ssm_c_re, ssm_c_im, ssm_d, ssm_w_glu, w_proj_ssm, w_proj_att, w_out, g_ffn,
           peer_w_q, peer_keys1, peer_keys2, peer_u, peer_v, g_ple, ple_w_gate, ple_w_proj,
           g_final):
    B, S, _ = x.shape
    assert w_in.shape[0] == 1, "the final rmsnorm is fused into the single layer's last stage"
    steps = CHUNK_STEPS if sum(CHUNK_STEPS) == S else (S,)
    i = 0
    tables = _s5_tables(ssm_log_dt[i], ssm_a_re[i], ssm_a_im[i], ssm_b_re[i], ssm_b_im[i],
                        ssm_c_re[i], ssm_c_im[i])
    w_in_b, w_glu_b = w_in[i].astype(BF16), ssm_w_glu[i].astype(BF16)
    d_skip = ssm_d[i].reshape(1, D_SSM).astype(F32)
    merge_w = (w_proj_ssm[i].astype(BF16), w_proj_att[i].astype(BF16), w_out[i].astype(BF16),
               g_ffn[i].reshape(1, D_MODEL), peer_w_q[i].astype(BF16), peer_keys1[i], peer_keys2[i])
    final_w = (g_ple[i].reshape(1, D_MODEL), ple_w_gate[i].astype(BF16),
               ple_w_proj[i].astype(BF16), g_final.reshape(1, D_MODEL))
    uv_words = jnp.concatenate([_pack_bf16_pairs(peer_u[i]), _pack_bf16_pairs(peer_v[i])], axis=1)
    k_all = jnp.zeros((B, S, D_ATT), BF16)
    v_all = jnp.zeros((B, S, D_ATT), BF16)
    carry = jnp.zeros((2, SUBLANES, D_STATE), F32)
    outs = []
    t0 = 0
    after = (carry, carry)
    for nt in steps:
        u_sb, q, k, v, ga, gb = _in_proj(x, positions, g_mix[i], w_in_b, t0, nt, after)
        k_all = lax.dynamic_update_slice(k_all, k, (0, t0, 0))
        v_all = lax.dynamic_update_slice(v_all, v, (0, t0, 0))
        ys, carry = _s5(u_sb, carry, tables, d_skip, w_glu_b, B)
        att = _moba(q, k_all, v_all, t0 // MOBA_BLOCK)
        x1, h_words, scores = _merge(x, ys, att, ga, gb, t0, *merge_w)
        idx, gates = _topk(scores)
        after = (gates, outs[-2] if len(outs) > 1 else carry)
        peer_out = _peer(idx, h_words, gates, uv_words)
        outs.append(_final(x1, peer_out, p[i], t0, nt, *final_w))
        t0 += nt
    return jnp.concatenate(outs, axis=1)
```

```python
import functools
import math

import jax
import jax.numpy as jnp
from jax import lax
from jax.experimental import pallas as pl
from jax.experimental.pallas import tpu as pltpu
from jax.experimental.pallas import tpu_sc as plsc

F32 = jnp.float32
BF16 = jnp.bfloat16

D_MODEL = 1024
D_SSM = 512
SSM_GROUP = 16
SSM_GROUPS = 32
SSM_STATE = 64
D_STATE = SSM_GROUPS * SSM_STATE
N_HEADS = 8
HEAD_DIM = 64
D_ATT = 512
ROT_DIM = 16
ROPE_THETA = 500000.0
MOBA_BLOCK = 256
MOBA_TOPK = 3
PEER_HEADS = 8
PEER_KEYS = 128
PEER_QDIM = 256
PEER_HALF = 128
PEER_TOPK = 16
PEER_SEL = PEER_HEADS * PEER_TOPK
D_PLE = 256
EPS = 1e-6
NEG = -1e30
LANES = 128
SUBLANES = 8
VMEM_LIMIT = 48 * 1024 * 1024
HIGHEST = lax.Precision.HIGHEST


def _rms(x, g):
    return x * lax.rsqrt(jnp.mean(x * x, axis=-1, keepdims=True) + EPS) * g


def _dot(a, b):
    return jnp.dot(a, b, preferred_element_type=F32)


def _dot_nt(a, b, precision=None):
    return lax.dot_general(a, b, (((1,), (1,)), ((), ())), precision=precision,
                           preferred_element_type=F32)


IN_TS = 512


def _in_proj_kernel(x_ref, pos_ref, g_ref, w_ref, invf_ref, after_a, after_b,
                    u_ref, q_ref, k_ref, v_ref, ga_ref, gb_ref):
    del after_a, after_b
    h = _rms(x_ref[...], g_ref[...]).astype(BF16)

    def proj(lo, hi):
        return _dot(h, w_ref[:, lo:hi])

    u_ref[...] = proj(0, D_SSM).astype(BF16)
    ang = pos_ref[...].astype(F32) * invf_ref[...]
    cos = jnp.cos(ang)
    sin = jnp.sin(ang)
    lane = lax.broadcasted_iota(jnp.int32, (1, LANES), 1) % HEAD_DIM
    half = ROT_DIM // 2
    sin_hi = jnp.where((lane >= half) & (lane < ROT_DIM), sin, 0.0)
    sin_lo = jnp.where(lane < half, -sin, 0.0)
    reps = D_ATT // LANES
    cos4 = jnp.concatenate([cos] * reps, axis=1)
    sin_hi4 = jnp.concatenate([sin_hi] * reps, axis=1)
    sin_lo4 = jnp.concatenate([sin_lo] * reps, axis=1)

    def rope(t):
        return (t * cos4 + pltpu.roll(t, half, 1) * sin_hi4
                + pltpu.roll(t, D_ATT - half, 1) * sin_lo4)

    q = rope(proj(D_SSM, D_SSM + D_ATT))
    q_ref[...] = (q * (HEAD_DIM ** -0.5)).astype(BF16)
    k_ref[...] = rope(proj(D_SSM + D_ATT, D_SSM + 2 * D_ATT)).astype(BF16)
    v_ref[...] = proj(D_SSM + 2 * D_ATT, D_SSM + 3 * D_ATT).astype(BF16)
    o = D_SSM + 3 * D_ATT
    ga_ref[...] = jax.nn.sigmoid(proj(o, o + D_MODEL)).astype(BF16)
    gb_ref[...] = jax.nn.sigmoid(proj(o + D_MODEL, o + 2 * D_MODEL)).astype(BF16)


def _in_proj(x, positions, g_mix, w_in, t0, nt, after):
    B, S, _ = x.shape
    ts = min(IN_TS, nt)
    assert nt % ts == 0 and t0 % ts == 0
    i0 = t0 // ts
    inv_freq = ROPE_THETA ** (-jnp.arange(0, ROT_DIM, 2, dtype=F32) / ROT_DIM)
    lane = jnp.arange(LANES) % HEAD_DIM
    invf = jnp.where(lane < ROT_DIM, inv_freq[lane % (ROT_DIM // 2)], 0.0).reshape(1, LANES)
    d_in = w_in.shape[1]
    src = lambda d: pl.BlockSpec((None, ts, d), lambda b, i: (b, i0 + i, 0))
    tok = lambda d: pl.BlockSpec((None, ts, d), lambda b, i: (b, i, 0))
    full = lambda shape: pl.BlockSpec(shape, lambda b, i: (0,) * len(shape))
    return pl.pallas_call(
        _in_proj_kernel,
        grid=(B, nt // ts),
        in_specs=[src(D_MODEL), src(1), full((1, D_MODEL)), full((D_MODEL, d_in)), full((1, LANES)),
                  pl.BlockSpec(memory_space=pl.ANY), pl.BlockSpec(memory_space=pl.ANY)],
        out_specs=[pl.BlockSpec((ts, D_SSM), lambda b, i: (i, b)),
                   tok(D_ATT), tok(D_ATT), tok(D_ATT), tok(D_MODEL), tok(D_MODEL)],
        out_shape=[jax.ShapeDtypeStruct((nt, B * D_SSM), BF16),
                   jax.ShapeDtypeStruct((B, nt, D_ATT), BF16),
                   jax.ShapeDtypeStruct((B, nt, D_ATT), BF16),
                   jax.ShapeDtypeStruct((B, nt, D_ATT), BF16),
                   jax.ShapeDtypeStruct((B, nt, D_MODEL), BF16),
                   jax.ShapeDtypeStruct((B, nt, D_MODEL), BF16)],
        compiler_params=pltpu.CompilerParams(
            dimension_semantics=("parallel", "parallel"), vmem_limit_bytes=VMEM_LIMIT),
        name="in_proj",
    )(x, positions.reshape(B, S, 1), g_mix.reshape(1, D_MODEL), w_in, invf, *after)


S5_TS = 128
S5_BATCH = 4
S5_COLS = 512


def _s5_kernel(u_ref, c0_ref, bre_ref, bim_ref, a1r_ref, a1i_ref, pr_ref, pi_ref,
               cre_ref, cim_ref, d_ref, wglu_ref, y_ref, c1_ref,
               xr, xi, cr, ci, ysc):
    rows = xr.shape[0]
    ts = rows // S5_BATCH

    @pl.when(pl.program_id(0) == 0)
    def _():
        cr[...] = c0_ref[0]
        ci[...] = c0_ref[1]

    u = u_ref[...]
    xr[...] = _dot(u, bre_ref[...])
    xi[...] = _dot(u, bim_ref[...])

    hi_rows = lax.broadcasted_iota(jnp.int32, (SUBLANES, S5_COLS), 0) >= S5_BATCH
    for cb in range(D_STATE // S5_COLS):
        sl = slice(cb * S5_COLS, (cb + 1) * S5_COLS)
        a_r, a_i = a1r_ref[:, sl], a1i_ref[:, sl]
        p_r, p_i = pr_ref[:, sl], pi_ref[:, sl]

        def body(t, carry):
            c_r, c_i = carry
            r0 = pl.multiple_of(t * SUBLANES, SUBLANES)
            x_r = xr[pl.ds(r0, SUBLANES), sl]
            x_i = xi[pl.ds(r0, SUBLANES), sl]
            s_r = pltpu.roll(x_r, S5_BATCH, 0)
            s_i = pltpu.roll(x_i, S5_BATCH, 0)
            h_r = x_r + (a_r * s_r - a_i * s_i) + (p_r * c_r - p_i * c_i)
            h_i = x_i + (a_r * s_i + a_i * s_r) + (p_r * c_i + p_i * c_r)
            xr[pl.ds(r0, SUBLANES), sl] = h_r
            xi[pl.ds(r0, SUBLANES), sl] = h_i
            n_r = jnp.where(hi_rows, h_r, pltpu.roll(h_r, S5_BATCH, 0))
            n_i = jnp.where(hi_rows, h_i, pltpu.roll(h_i, S5_BATCH, 0))
            return n_r, n_i

        c_r, c_i = lax.fori_loop(0, rows // SUBLANES, body, (cr[:, sl], ci[:, sl]), unroll=2)
        cr[:, sl] = c_r
        ci[:, sl] = c_i

    y = (_dot(xr[...].astype(BF16), cre_ref[...]) - _dot(xi[...].astype(BF16), cim_ref[...])
         + d_ref[...] * u.astype(F32))
    y = jax.nn.gelu(y)
    y = y * jax.nn.sigmoid(_dot(y.astype(BF16), wglu_ref[...]))
    for c in range(D_SSM // LANES):
        ysc[c] = y[:, c * LANES:(c + 1) * LANES]
    for b in range(S5_BATCH):
        for c in range(D_SSM // LANES):
            y_ref[b, :, c * LANES:(c + 1) * LANES] = (
                ysc[c, pl.ds(b, ts, stride=S5_BATCH), :].astype(BF16))

    @pl.when(pl.program_id(0) == pl.num_programs(0) - 1)
    def _():
        c1_ref[0] = cr[...]
        c1_ref[1] = ci[...]


def _s5_tables(log_dt, a_re, a_im, b_re, b_im, c_re, c_im):
    dt = jnp.exp(log_dt.astype(F32))[:, None]
    ar, ai = a_re.astype(F32), a_im.astype(F32)
    mag = jnp.exp(dt * ar)
    abar_re, abar_im = mag * jnp.cos(dt * ai), mag * jnp.sin(dt * ai)
    den = ar * ar + ai * ai
    nr, ni = abar_re - 1.0, abar_im
    f_re = (nr * ar + ni * ai) / den
    f_im = (ni * ar - nr * ai) / den
    br, bi = b_re.astype(F32), b_im.astype(F32)
    bb_re = f_re[..., None] * br - f_im[..., None] * bi
    bb_im = f_re[..., None] * bi + f_im[..., None] * br
    eye = jnp.eye(SSM_GROUPS, dtype=F32)

    def in_blockdiag(bb):
        return jnp.einsum('gnc,gh->gchn', bb, eye).reshape(D_SSM, D_STATE)

    def out_blockdiag(c):
        return jnp.einsum('gcn,gh->gnhc', c.astype(F32), eye).reshape(D_STATE, D_SSM)

    a_r = abar_re.reshape(1, D_STATE)
    a_i = abar_im.reshape(1, D_STATE)
    a2_r = a_r * a_r - a_i * a_i
    a2_i = 2.0 * a_r * a_i
    hi = (jnp.arange(SUBLANES) >= S5_BATCH)[:, None]
    a1r = jnp.where(hi, a_r, 0.0)
    a1i = jnp.where(hi, a_i, 0.0)
    p_r = jnp.where(hi, a2_r, a_r)
    p_i = jnp.where(hi, a2_i, a_i)
    return (in_blockdiag(bb_re).astype(BF16), in_blockdiag(bb_im).astype(BF16),
            a1r, a1i, p_r, p_i,
            out_blockdiag(c_re).astype(BF16), out_blockdiag(c_im).astype(BF16))


def _s5(u_sb, carry, tables, d_skip, w_glu, B):
    assert B == S5_BATCH
    nt = u_sb.shape[0]
    ts = min(S5_TS, nt)
    rows = ts * B
    bre, bim, a1r, a1i, p_r, p_i, cre, cim = tables
    full = lambda shape: pl.BlockSpec(shape, lambda i: (0,) * len(shape))
    return pl.pallas_call(
        _s5_kernel,
        grid=(nt // ts,),
        in_specs=[pl.BlockSpec((rows, D_SSM), lambda i: (i, 0)),
                  full((2, SUBLANES, D_STATE)),
                  full((D_SSM, D_STATE)), full((D_SSM, D_STATE)),
                  full((SUBLANES, D_STATE)), full((SUBLANES, D_STATE)),
                  full((SUBLANES, D_STATE)), full((SUBLANES, D_STATE)),
                  full((D_STATE, D_SSM)), full((D_STATE, D_SSM)),
                  full((1, D_SSM)), full((D_SSM, D_SSM))],
        out_specs=[pl.BlockSpec((B, ts, D_SSM), lambda i: (0, i, 0)),
                   full((2, SUBLANES, D_STATE))],
        out_shape=[jax.ShapeDtypeStruct((B, nt, D_SSM), BF16),
                   jax.ShapeDtypeStruct((2, SUBLANES, D_STATE), F32)],
        scratch_shapes=[pltpu.VMEM((rows, D_STATE), F32), pltpu.VMEM((rows, D_STATE), F32),
                        pltpu.VMEM((SUBLANES, D_STATE), F32), pltpu.VMEM((SUBLANES, D_STATE), F32),
                        pltpu.VMEM((D_SSM // LANES, rows, LANES), F32)],
        compiler_params=pltpu.CompilerParams(
            dimension_semantics=("arbitrary",), vmem_limit_bytes=VMEM_LIMIT),
        name="s5",
    )(u_sb.reshape(nt * B, D_SSM), carry, bre, bim, a1r, a1i, p_r, p_i, cre, cim, d_skip, w_glu)


MOBA_PAIR = 2 * MOBA_BLOCK


def _moba_kernel(q0, q_ref, k_ref, v_ref, o_ref, kmean, kaug_a, kaug_b, vaug_a, vaug_b, m_s, acc_s,
                 s_buf):
    qi = pl.program_id(2) + q0
    nb = k_ref.shape[0] // MOBA_BLOCK
    nbp = kmean.shape[0]
    lane = lax.broadcasted_iota(jnp.int32, (1, LANES), 1)
    head_a = lane < HEAD_DIM

    @pl.when(pl.program_id(2) == 0)
    def _():
        kmean[...] = jnp.zeros_like(kmean)
        for j in range(nb):
            rows = pl.ds(j * MOBA_BLOCK, MOBA_BLOCK)
            kj = k_ref[rows, :].astype(F32)
            vj = v_ref[rows, :].astype(F32)
            kmean[j:j + 1, :] = jnp.sum(kj, axis=0, keepdims=True) * (1.0 / MOBA_BLOCK)
            kaug_a[rows, :] = jnp.where(head_a, kj, jnp.where(lane - HEAD_DIM == j, 1.0, 0.0)).astype(BF16)
            kaug_b[rows, :] = jnp.where(head_a, jnp.where(lane == j, 1.0, 0.0), kj).astype(BF16)
            vaug_a[rows, :] = jnp.where(head_a, vj, 1.0).astype(BF16)
            vaug_b[rows, :] = jnp.where(head_a, 1.0, vj).astype(BF16)

    qf = q_ref[...].astype(F32)
    blk_row = lax.broadcasted_iota(jnp.int32, (nbp, MOBA_BLOCK), 0)
    q_augs = []
    for is_a in (True, False):
        mine = head_a if is_a else jnp.logical_not(head_a)
        q_own = jnp.where(mine, qf, 0.0)
        g = _dot_nt(kmean[...], q_own, precision=HIGHEST)
        g = jnp.where(blk_row < qi, g, NEG)
        sel = jnp.zeros(g.shape, F32)
        for _ in range(MOBA_TOPK):
            m = jnp.max(g, axis=0, keepdims=True)
            idx = jnp.min(jnp.where(g == m, blk_row, nbp), axis=0, keepdims=True)
            hit = blk_row == idx
            sel = jnp.where(hit, jnp.where(idx < qi, 1.0, 0.0), sel)
            g = jnp.where(hit, -jnp.inf, g)
        bias_t = jnp.where(sel > 0.0, 0.0, jnp.where(blk_row == qi, 0.0, NEG))
        bias_t = jnp.concatenate([bias_t, jnp.full((LANES - nbp, MOBA_BLOCK), NEG, F32)], axis=0)
        bias = jnp.transpose(bias_t)
        if is_a:
            bias = pltpu.roll(bias, HEAD_DIM, 1)
        q_augs.append(jnp.where(mine, qf, bias).astype(BF16))

    m_s[...] = jnp.full(m_s.shape, -jnp.inf, F32)
    acc_s[...] = jnp.zeros_like(acc_s)
    qpos = qi * MOBA_BLOCK + lax.broadcasted_iota(jnp.int32, (MOBA_BLOCK, MOBA_PAIR), 0)
    col = lax.broadcasted_iota(jnp.int32, (MOBA_BLOCK, MOBA_PAIR), 1)

    def kv_rows(jj):
        return pl.ds(pl.multiple_of(jj * MOBA_PAIR, MOBA_PAIR), MOBA_PAIR)

    def scores(jj, slot):
        for hd, kaug in enumerate((kaug_a, kaug_b)):
            s_buf[slot, hd] = _dot_nt(q_augs[hd], kaug[kv_rows(jj), :])

    def softmax_pv(jj, slot, causal):
        for hd, vaug in enumerate((vaug_a, vaug_b)):
            s = s_buf[slot, hd]
            if causal:
                s = jnp.where(jj * MOBA_PAIR + col <= qpos, s, NEG)
            m_old = m_s[hd]
            m_new = jnp.maximum(m_old, jnp.max(s, axis=-1, keepdims=True))
            alpha = jnp.exp(m_old - m_new)
            p = jnp.exp(s - m_new)
            m_s[hd] = m_new
            acc_s[hd] = alpha * acc_s[hd] + _dot(p.astype(BF16), vaug[kv_rows(jj), :])

    last = qi // 2
    scores(0, 0)

    def body(k, _):
        scores(2 * k + 1, 1)
        softmax_pv(2 * k, 0, False)
        scores(2 * k + 2, 0)
        softmax_pv(2 * k + 1, 1, False)
        return 0

    lax.fori_loop(0, last // 2, body, 0)

    @pl.when(last % 2 == 0)
    def _():
        softmax_pv(last, 0, True)

    @pl.when(last % 2 == 1)
    def _():
        scores(last, 1)
        softmax_pv(last - 1, 0, False)
        softmax_pv(last, 1, True)
    acc_a, acc_b = acc_s[0], acc_s[1]
    o_ref[...] = jnp.where(head_a, acc_a / pltpu.roll(acc_a, HEAD_DIM, 1),
                           acc_b / pltpu.roll(acc_b, HEAD_DIM, 1)).astype(BF16)


def _moba(q, k, v, q0):
    B = q.shape[0]
    nq = q.shape[1] // MOBA_BLOCK
    skv = (q0 + nq) * MOBA_BLOCK
    nb = skv // MOBA_BLOCK
    assert nb <= HEAD_DIM and nb % 2 == 0 and skv <= k.shape[1]
    nbp = -(-nb // SUBLANES) * SUBLANES
    blk = pl.BlockSpec((None, MOBA_BLOCK, LANES), lambda b, h, i: (b, i, h))
    seq = pl.BlockSpec((None, skv, LANES), lambda b, h, i: (b, 0, h))
    return pl.pallas_call(
        functools.partial(_moba_kernel, q0),
        grid=(B, D_ATT // LANES, nq),
        in_specs=[blk, seq, seq],
        out_specs=blk,
        out_shape=jax.ShapeDtypeStruct(q.shape, BF16),
        scratch_shapes=[pltpu.VMEM((nbp, LANES), F32),
                        pltpu.VMEM((skv, LANES), BF16), pltpu.VMEM((skv, LANES), BF16),
                        pltpu.VMEM((skv, LANES), BF16), pltpu.VMEM((skv, LANES), BF16),
                        pltpu.VMEM((2, MOBA_BLOCK, 1), F32),
                        pltpu.VMEM((2, MOBA_BLOCK, LANES), F32),
                        pltpu.VMEM((2, 2, MOBA_BLOCK, MOBA_PAIR), F32)],
        compiler_params=pltpu.CompilerParams(
            dimension_semantics=("parallel", "parallel", "arbitrary"), vmem_limit_bytes=VMEM_LIMIT),
        name="moba",
    )(q, k, v)


MERGE_TS = 256


def _bf16_bits(x):
    b = pltpu.bitcast(x, jnp.int32)
    r = b + 0x7FFF + (lax.shift_right_logical(b, 16) & 1)
    return lax.shift_right_logical(r, 16)


def _merge_kernel(x_ref, ys_ref, at_ref, ga_ref, gb_ref, wa_ref, wb_ref, wo_ref, g_ref,
                  wq_ref, k1_ref, k2_ref, x1_ref, hw_ref, sc_ref):
    ya = _dot(ys_ref[...], wa_ref[...])
    yb = _dot(at_ref[...], wb_ref[...])
    merged = ga_ref[...].astype(F32) * ya + gb_ref[...].astype(F32) * yb
    x1 = x_ref[...] + _dot(merged.astype(BF16), wo_ref[...])
    x1_ref[...] = x1
    hq = _rms(x1, g_ref[...])
    half = D_MODEL // 2
    hw_ref[...] = _bf16_bits(hq[:, :half]) | lax.shift_left(_bf16_bits(hq[:, half:]), 16)
    qp = _dot(hq.astype(BF16), wq_ref[...])
    for h in range(PEER_HEADS):
        o = h * PEER_QDIM
        sc_ref[2 * h] = _dot_nt(k1_ref[h], qp[:, o:o + PEER_HALF], precision=HIGHEST)
        sc_ref[2 * h + 1] = _dot_nt(k2_ref[h], qp[:, o + PEER_HALF:o + PEER_QDIM], precision=HIGHEST)


def _merge(x, ys, att, ga, gb, t0, w_proj_ssm, w_proj_att, w_out, g_ffn, peer_w_q, keys1, keys2):
    B, nt = ys.shape[0], ys.shape[1]
    ts = min(MERGE_TS, nt)
    nblk = nt // ts
    i0 = t0 // ts
    tok = lambda d: pl.BlockSpec((None, ts, d), lambda b, i: (b, i, 0))
    row = lambda d: pl.BlockSpec((ts, d), lambda b, i: (b * nblk + i, 0))
    full = lambda shape: pl.BlockSpec(shape, lambda b, i: (0,) * len(shape))
    qd = PEER_HEADS * PEER_QDIM
    return pl.pallas_call(
        _merge_kernel,
        grid=(B, nblk),
        in_specs=[pl.BlockSpec((None, ts, D_MODEL), lambda b, i: (b, i0 + i, 0)),
                  tok(D_SSM), tok(D_ATT), tok(D_MODEL), tok(D_MODEL),
                  full((D_SSM, D_MODEL)), full((D_ATT, D_MODEL)), full((D_MODEL, D_MODEL)),
                  full((1, D_MODEL)), full((D_MODEL, qd)),
                  full((PEER_HEADS, PEER_KEYS, PEER_HALF)), full((PEER_HEADS, PEER_KEYS, PEER_HALF))],
        out_specs=[row(D_MODEL), row(D_MODEL // 2),
                   pl.BlockSpec((2 * PEER_HEADS, PEER_KEYS, ts), lambda b, i: (0, 0, b * nblk + i))],
        out_shape=[jax.ShapeDtypeStruct((B * nt, D_MODEL), F32),
                   jax.ShapeDtypeStruct((B * nt, D_MODEL // 2), jnp.int32),
                   jax.ShapeDtypeStruct((2 * PEER_HEADS, PEER_KEYS, B * nt), F32)],
        compiler_params=pltpu.CompilerParams(
            dimension_semantics=("parallel", "parallel"), vmem_limit_bytes=VMEM_LIMIT),
        name="merge",
    )(x, ys, att, ga, gb, w_proj_ssm, w_proj_att, w_out, g_ffn, peer_w_q, keys1, keys2)


TOPK_TS = 256


def _top_rows(s, row, k):
    vals, idxs = [], []
    for _ in range(k):
        m = jnp.max(s, axis=0, keepdims=True)
        idx = jnp.min(jnp.where(s == m, row, s.shape[0]), axis=0, keepdims=True)
        vals.append(m)
        idxs.append(idx)
        s = jnp.where(row == idx, -jnp.inf, s)
    return vals, idxs


def _stack_rows(rows, row16):
    acc = jnp.zeros(row16.shape, rows[0].dtype)
    for r, v in enumerate(rows):
        acc = jnp.where(row16 == r, v, acc)
    return acc


def _topk_kernel(sc_ref, idx_ref, gate_ref):
    ts = sc_ref.shape[-1]
    row = lax.broadcasted_iota(jnp.int32, (PEER_KEYS, ts), 0)
    row16 = lax.broadcasted_iota(jnp.int32, (PEER_TOPK, ts), 0)
    row8 = lax.broadcasted_iota(jnp.int32, (SUBLANES, ts), 0)
    counts = [PEER_TOPK // (i + 1) for i in range(PEER_TOPK)]
    heights = [PEER_TOPK if c > SUBLANES else SUBLANES for c in counts]
    n_cand = sum(heights)
    rowc = lax.broadcasted_iota(jnp.int32, (n_cand, ts), 0)
    gate_rows, eid_rows = [], []
    for h in range(PEER_HEADS):
        v1, i1 = _top_rows(sc_ref[2 * h], row, PEER_TOPK)
        v2, i2 = _top_rows(sc_ref[2 * h + 1], row, PEER_TOPK)
        v2s = _stack_rows(v2, row16)
        i2s = _stack_rows(i2, row16).astype(F32)
        cand, eid = [], []
        for i in range(PEER_TOPK):
            n = heights[i]
            cand.append(jnp.where((row16 if n == PEER_TOPK else row8) < counts[i],
                                  v1[i] + v2s[:n], -jnp.inf))
            eid.append(i1[i].astype(F32) * PEER_KEYS + i2s[:n])
        cand = jnp.concatenate(cand, axis=0)
        eid = jnp.concatenate(eid, axis=0)
        tops, picks = [], []
        for _ in range(PEER_TOPK):
            m = jnp.max(cand, axis=0, keepdims=True)
            pos = jnp.min(jnp.where(cand == m, rowc, n_cand), axis=0, keepdims=True)
            hit = rowc == pos
            picks.append(jnp.max(jnp.where(hit, eid, -1.0), axis=0, keepdims=True))
            tops.append(m)
            cand = jnp.where(hit, -jnp.inf, cand)
        top = _stack_rows(tops, row16)
        p = jnp.exp(top - jnp.max(top, axis=0, keepdims=True))
        gate_rows.append(p / jnp.sum(p, axis=0, keepdims=True))
        eid_rows.append(_stack_rows(picks, row16))
    gate_ref[...] = jnp.transpose(jnp.concatenate(gate_rows, axis=0))
    idx_ref[...] = jnp.transpose(jnp.concatenate(eid_rows, axis=0)).astype(jnp.int32)


def _topk(scores):
    T = scores.shape[-1]
    ts = min(TOPK_TS, T)
    return pl.pallas_call(
        _topk_kernel,
        grid=(T // ts,),
        in_specs=[pl.BlockSpec((2 * PEER_HEADS, PEER_KEYS, ts), lambda i: (0, 0, i))],
        out_specs=[pl.BlockSpec((ts, PEER_SEL), lambda i: (i, 0)),
                   pl.BlockSpec((ts, PEER_SEL), lambda i: (i, 0))],
        out_shape=[jax.ShapeDtypeStruct((T, PEER_SEL), jnp.int32),
                   jax.ShapeDtypeStruct((T, PEER_SEL), F32)],
        compiler_params=pltpu.CompilerParams(
            dimension_semantics=("parallel",), vmem_limit_bytes=VMEM_LIMIT),
        name="topk",
    )(scores)


SC_CORES = 2
SC_SUBCORES = 16
SC_LANES = 16
SC_WORKERS = SC_CORES * SC_SUBCORES
PEER_CH = SC_LANES
PEER_NCH = PEER_SEL // PEER_CH
PEER_WORDS = D_MODEL // 2
PEER_NWG = PEER_WORDS // SC_LANES
PEER_RING = 4
PEER_QUAD = 4
HI_MASK = -65536
GELU_C = 0.7978845608028654


def _gelu_tanh_via_exp(x):
    z = GELU_C * (x + 0.044715 * (x * x * x))
    t = 1.0 - 2.0 / (jnp.exp(2.0 * z) + 1.0)
    return 0.5 * x * (1.0 + t)


def _unpack_pair(w):
    lo = plsc.bitcast(lax.shift_left(w, 16), F32)
    hi = plsc.bitcast(lax.bitwise_and(w, HI_MASK), F32)
    return lo, hi


def _peer_sc_body(idx_hbm, gate_hbm, h_hbm, uv_hbm, after_hbm, o_hbm,
                  idx_v, gate_v, h_v, buf, out_v, gsem, msem, osem):
    n_tok = o_hbm.shape[0] // SC_WORKERS
    base = (lax.axis_index("s") * SC_CORES + lax.axis_index("c")) * n_tok
    lane = lax.iota(jnp.int32, SC_LANES)
    zero_rows = jnp.zeros((SC_LANES,), jnp.int32)

    def meta_copies(tok, s):
        return (pltpu.make_async_copy(idx_hbm.at[tok], idx_v.at[s], msem.at[s]),
                pltpu.make_async_copy(gate_hbm.at[tok], gate_v.at[s], msem.at[s]),
                pltpu.make_async_copy(h_hbm.at[tok], h_v.at[s], msem.at[s]))

    def gather(slot, rows):
        return pltpu.make_async_copy(uv_hbm.at[rows], buf.at[slot], gsem.at[slot])

    def token(t, carry):
        s = t % 2
        tok = base + t
        nxt = base + jnp.minimum(t + 1, n_tok - 1)
        for cp in meta_copies(nxt, 1 - s):
            cp.start()

        @pl.when(t >= 2)
        def _():
            pltpu.make_async_copy(out_v.at[s], o_hbm.at[tok], osem.at[s]).wait()

        def chunk(c, carry):
            slot = c % PEER_RING
            gather(slot, zero_rows).wait()

            def dot_step(q, accs):
                cols = [pl.ds(pl.multiple_of((q * PEER_QUAD + j) * SC_LANES, SC_LANES), SC_LANES)
                        for j in range(PEER_QUAD)]
                hs = [plsc.bitcast(h_v[s, col], BF16) for col in cols]
                out = []
                for r in range(PEER_CH):
                    p = plsc.bitcast(buf[slot, r, cols[0]], BF16) * hs[0]
                    for j in range(1, PEER_QUAD):
                        p = p + plsc.bitcast(buf[slot, r, cols[j]], BF16) * hs[j]
                    lo, hi = _unpack_pair(plsc.bitcast(p, jnp.int32))
                    out.append(accs[r] + lo + hi)
                return tuple(out)

            accs = lax.fori_loop(0, PEER_NWG // PEER_QUAD, dot_step,
                                 tuple(jnp.zeros((SC_LANES,), F32) for _ in range(PEER_CH)))
            tot = jnp.zeros((SC_LANES,), F32)
            for r in range(PEER_CH):
                tot = jnp.where(lane == r, jnp.sum(accs[r]), tot)
            rows = pl.ds(pl.multiple_of(c * PEER_CH, PEER_CH), PEER_CH)
            wvec = gate_v[s, rows] * _gelu_tanh_via_exp(tot)
            ws = []
            for r in range(PEER_CH):
                w = wvec.at[jnp.full((SC_LANES,), r, jnp.int32)].get(mode="promise_in_bounds")
                ws.append(plsc.pack(w, w, format=plsc.PackFormat.INTERLEAVED,
                                    preferred_element_type=BF16))
            first = c == 0

            @plsc.parallel_loop(0, PEER_NWG, unroll=2)
            def acc_step(g):
                col = pl.ds(pl.multiple_of(g * SC_LANES, SC_LANES), SC_LANES)
                col_v = pl.ds(pl.multiple_of(PEER_WORDS + g * SC_LANES, SC_LANES), SC_LANES)
                o_lo = jnp.where(first, 0.0, out_v[s, col])
                o_hi = jnp.where(first, 0.0, out_v[s, col_v])
                for r0 in range(0, PEER_CH, PEER_QUAD):
                    p = plsc.bitcast(buf[slot, r0, col_v], BF16) * ws[r0]
                    for r in range(r0 + 1, r0 + PEER_QUAD):
                        p = p + plsc.bitcast(buf[slot, r, col_v], BF16) * ws[r]
                    lo, hi = _unpack_pair(plsc.bitcast(p, jnp.int32))
                    o_lo = o_lo + lo
                    o_hi = o_hi + hi
                out_v[s, col] = o_lo
                out_v[s, col_v] = o_hi

            @pl.when(c == PEER_NCH - PEER_RING)
            def _():
                for cp in meta_copies(nxt, 1 - s):
                    cp.wait()

            ahead = c + PEER_RING
            src = jnp.where(ahead < PEER_NCH, s, 1 - s)
            nrows = idx_v[src, pl.ds(pl.multiple_of((ahead % PEER_NCH) * PEER_CH, PEER_CH), PEER_CH)]
            gather(slot, nrows).start()
            return carry

        lax.fori_loop(0, PEER_NCH, chunk, 0)
        pltpu.make_async_copy(out_v.at[s], o_hbm.at[tok], osem.at[s]).start()
        return carry

    for cp in meta_copies(base, 0):
        cp.start()
    for cp in meta_copies(base, 0):
        cp.wait()
    for c in range(PEER_RING):
        gather(c, idx_v[0, pl.ds(c * PEER_CH, PEER_CH)]).start()
    lax.fori_loop(0, n_tok, token, 0)
    for c in range(PEER_RING):
        gather(c, zero_rows).wait()
    for s in range(2):
        pltpu.make_async_copy(out_v.at[s], o_hbm.at[base], osem.at[s]).wait()


def _pack_bf16_pairs(tab):
    b = lax.bitcast_convert_type(tab.astype(BF16), jnp.uint16).astype(jnp.uint32)
    half = tab.shape[1] // 2
    return lax.bitcast_convert_type(b[:, :half] | (b[:, half:] << 16), jnp.int32)


def _peer(idx, h_words, gates, uv_words, after):
    T = h_words.shape[0]
    assert T % (2 * SC_WORKERS) == 0
    mesh = plsc.VectorSubcoreMesh(core_axis_name="c", subcore_axis_name="s",
                                  num_cores=SC_CORES, num_subcores=SC_SUBCORES)
    return pl.kernel(
        _peer_sc_body,
        out_type=jax.ShapeDtypeStruct((T, D_MODEL), F32),
        mesh=mesh,
        scratch_types=[
            pltpu.VMEM((2, PEER_SEL), jnp.int32), pltpu.VMEM((2, PEER_SEL), F32),
            pltpu.VMEM((2, PEER_WORDS), jnp.int32),
            pltpu.VMEM((PEER_RING, PEER_CH, 2 * PEER_WORDS), jnp.int32),
            pltpu.VMEM((2, D_MODEL), F32),
            pltpu.SemaphoreType.DMA((PEER_RING,)),
            pltpu.SemaphoreType.DMA((2,)), pltpu.SemaphoreType.DMA((2,)),
        ],
        compiler_params=pltpu.CompilerParams(needs_layout_passes=False),
        name="peer_sc",
    )(idx, gates, h_words, uv_words, after)


FINAL_TS = 256


def _final_kernel(x1_ref, pe_ref, p_ref, gp_ref, wg_ref, wp_ref, gf_ref, o_ref):
    x2 = x1_ref[...] + pe_ref[...]
    e = _dot(p_ref[...].astype(BF16), wp_ref[...])
    gate = jax.nn.sigmoid(_dot(_rms(x2, gp_ref[...]).astype(BF16), wg_ref[...]))
    o_ref[...] = _rms(x2 + gate * e, gf_ref[...])


def _final(x1, peer_out, p, t0, nt, g_ple, ple_w_gate, ple_w_proj, g_final):
    B = p.shape[0]
    ts = min(FINAL_TS, nt)
    nblk = nt // ts
    i0 = t0 // ts
    row = lambda d: pl.BlockSpec((ts, d), lambda b, i: (b * nblk + i, 0))
    full = lambda shape: pl.BlockSpec(shape, lambda b, i: (0,) * len(shape))
    return pl.pallas_call(
        _final_kernel,
        grid=(B, nblk),
        in_specs=[row(D_MODEL), row(D_MODEL),
                  pl.BlockSpec((None, ts, D_PLE), lambda b, i: (b, i0 + i, 0)),
                  full((1, D_MODEL)), full((D_MODEL, D_MODEL)), full((D_PLE, D_MODEL)),
                  full((1, D_MODEL))],
        out_specs=pl.BlockSpec((None, ts, D_MODEL), lambda b, i: (b, i, 0)),
        out_shape=jax.ShapeDtypeStruct((B, nt, D_MODEL), F32),
        compiler_params=pltpu.CompilerParams(
            dimension_semantics=("parallel", "parallel"), vmem_limit_bytes=VMEM_LIMIT),
        name="final",
    )(x1, peer_out, p, g_ple, ple_w_gate, ple_w_proj, g_final)


CHUNK_STEPS = (512, 512, 1024, 1024, 1024, 1024, 1024, 1024, 512, 512)


def kernel(x, p, positions, g_mix, w_in, ssm_log_dt, ssm_a_re, ssm_a_im, ssm_b_re, ssm_b_im,
           ssm_c_re, ssm_c_im, ssm_d, ssm_w_glu, w_proj_ssm, w_proj_att, w_out, g_ffn,
           peer_w_q, peer_keys1, peer_keys2, peer_u, peer_v, g_ple, ple_w_gate, ple_w_proj,
           g_final):
    B, S, _ = x.shape
    assert w_in.shape[0] == 1, "the final rmsnorm is fused into the single layer's last stage"
    steps = CHUNK_STEPS if sum(CHUNK_STEPS) == S else (S,)
    i = 0
    tables = _s5_tables(ssm_log_dt[i], ssm_a_re[i], ssm_a_im[i], ssm_b_re[i], ssm_b_im[i],
                        ssm_c_re[i], ssm_c_im[i])
    w_in_b, w_glu_b = w_in[i].astype(BF16), ssm_w_glu[i].astype(BF16)
    d_skip = ssm_d[i].reshape(1, D_SSM).astype(F32)
    merge_w = (w_proj_ssm[i].astype(BF16), w_proj_att[i].astype(BF16), w_out[i].astype(BF16),
               g_ffn[i].reshape(1, D_MODEL), peer_w_q[i].astype(BF16), peer_keys1[i], peer_keys2[i])
    final_w = (g_ple[i].reshape(1, D_MODEL), ple_w_gate[i].astype(BF16),
               ple_w_proj[i].astype(BF16), g_final.reshape(1, D_MODEL))
    uv_words = jnp.concatenate([_pack_bf16_pairs(peer_u[i]), _pack_bf16_pairs(peer_v[i])], axis=1)
    k_all = jnp.zeros((B, S, D_ATT), BF16)
    v_all = jnp.zeros((B, S, D_ATT), BF16)
    carry = jnp.zeros((2, SUBLANES, D_STATE), F32)
    outs = []
    t0 = 0
    after = (carry, carry)
    peer_prev = carry
    for nt in steps:
        u_sb, q, k, v, ga, gb = _in_proj(x, positions, g_mix[i], w_in_b, t0, nt, after)
        k_all = lax.dynamic_update_slice(k_all, k, (0, t0, 0))
        v_all = lax.dynamic_update_slice(v_all, v, (0, t0, 0))
        ys, carry = _s5(u_sb, carry, tables, d_skip, w_glu_b, B)
        att = _moba(q, k_all, v_all, t0 // MOBA_BLOCK)
        x1, h_words, scores = _merge(x, ys, att, ga, gb, t0, *merge_w)
        idx, gates = _topk(scores)
        after = (gates, outs[-2] if len(outs) > 1 else carry)
        peer_out = _peer(idx, h_words, gates, uv_words, peer_prev)
        peer_prev = peer_out
        outs.append(_final(x1, peer_out, p[i], t0, nt, *final_w))
        t0 += nt
    return jnp.concatenate(outs, axis=1)
```

```python
import functools
import math

import jax
import jax.numpy as jnp
from jax import lax
from jax.experimental import pallas as pl
from jax.experimental.pallas import tpu as pltpu
from jax.experimental.pallas import tpu_sc as plsc

F32 = jnp.float32
BF16 = jnp.bfloat16

D_MODEL = 1024
D_SSM = 512
SSM_GROUP = 16
SSM_GROUPS = 32
SSM_STATE = 64
D_STATE = SSM_GROUPS * SSM_STATE
N_HEADS = 8
HEAD_DIM = 64
D_ATT = 512
ROT_DIM = 16
ROPE_THETA = 500000.0
MOBA_BLOCK = 256
MOBA_TOPK = 3
PEER_HEADS = 8
PEER_KEYS = 128
PEER_QDIM = 256
PEER_HALF = 128
PEER_TOPK = 16
PEER_SEL = PEER_HEADS * PEER_TOPK
D_PLE = 256
EPS = 1e-6
NEG = -1e30
LANES = 128
SUBLANES = 8
VMEM_LIMIT = 48 * 1024 * 1024
HIGHEST = lax.Precision.HIGHEST


def _rms(x, g):
    return x * lax.rsqrt(jnp.mean(x * x, axis=-1, keepdims=True) + EPS) * g


def _dot(a, b):
    return jnp.dot(a, b, preferred_element_type=F32)


def _dot_nt(a, b, precision=None):
    return lax.dot_general(a, b, (((1,), (1,)), ((), ())), precision=precision,
                           preferred_element_type=F32)


IN_TS = 512


def _in_proj_kernel(x_ref, pos_ref, g_ref, w_ref, invf_ref, after_a, after_b,
                    u_ref, q_ref, k_ref, v_ref, ga_ref, gb_ref):
    del after_a, after_b
    h = _rms(x_ref[...], g_ref[...]).astype(BF16)

    def proj(lo, hi):
        return _dot(h, w_ref[:, lo:hi])

    u_ref[...] = proj(0, D_SSM).astype(BF16)
    ang = pos_ref[...].astype(F32) * invf_ref[...]
    cos = jnp.cos(ang)
    sin = jnp.sin(ang)
    lane = lax.broadcasted_iota(jnp.int32, (1, LANES), 1) % HEAD_DIM
    half = ROT_DIM // 2
    sin_hi = jnp.where((lane >= half) & (lane < ROT_DIM), sin, 0.0)
    sin_lo = jnp.where(lane < half, -sin, 0.0)
    reps = D_ATT // LANES
    cos4 = jnp.concatenate([cos] * reps, axis=1)
    sin_hi4 = jnp.concatenate([sin_hi] * reps, axis=1)
    sin_lo4 = jnp.concatenate([sin_lo] * reps, axis=1)

    def rope(t):
        return (t * cos4 + pltpu.roll(t, half, 1) * sin_hi4
                + pltpu.roll(t, D_ATT - half, 1) * sin_lo4)

    q = rope(proj(D_SSM, D_SSM + D_ATT))
    q_ref[...] = (q * (HEAD_DIM ** -0.5)).astype(BF16)
    k_ref[...] = rope(proj(D_SSM + D_ATT, D_SSM + 2 * D_ATT)).astype(BF16)
    v_ref[...] = proj(D_SSM + 2 * D_ATT, D_SSM + 3 * D_ATT).astype(BF16)
    o = D_SSM + 3 * D_ATT
    ga_ref[...] = jax.nn.sigmoid(proj(o, o + D_MODEL)).astype(BF16)
    gb_ref[...] = jax.nn.sigmoid(proj(o + D_MODEL, o + 2 * D_MODEL)).astype(BF16)


def _in_proj(x, positions, g_mix, w_in, t0, nt, after):
    B, S, _ = x.shape
    ts = min(IN_TS, nt)
    assert nt % ts == 0 and t0 % ts == 0
    i0 = t0 // ts
    inv_freq = ROPE_THETA ** (-jnp.arange(0, ROT_DIM, 2, dtype=F32) / ROT_DIM)
    lane = jnp.arange(LANES) % HEAD_DIM
    invf = jnp.where(lane < ROT_DIM, inv_freq[lane % (ROT_DIM // 2)], 0.0).reshape(1, LANES)
    d_in = w_in.shape[1]
    src = lambda d: pl.BlockSpec((None, ts, d), lambda b, i: (b, i0 + i, 0))
    tok = lambda d: pl.BlockSpec((None, ts, d), lambda b, i: (b, i, 0))
    full = lambda shape: pl.BlockSpec(shape, lambda b, i: (0,) * len(shape))
    return pl.pallas_call(
        _in_proj_kernel,
        grid=(B, nt // ts),
        in_specs=[src(D_MODEL), src(1), full((1, D_MODEL)), full((D_MODEL, d_in)), full((1, LANES)),
                  pl.BlockSpec(memory_space=pl.ANY), pl.BlockSpec(memory_space=pl.ANY)],
        out_specs=[pl.BlockSpec((ts, D_SSM), lambda b, i: (i, b)),
                   tok(D_ATT), tok(D_ATT), tok(D_ATT), tok(D_MODEL), tok(D_MODEL)],
        out_shape=[jax.ShapeDtypeStruct((nt, B * D_SSM), BF16),
                   jax.ShapeDtypeStruct((B, nt, D_ATT), BF16),
                   jax.ShapeDtypeStruct((B, nt, D_ATT), BF16),
                   jax.ShapeDtypeStruct((B, nt, D_ATT), BF16),
                   jax.ShapeDtypeStruct((B, nt, D_MODEL), BF16),
                   jax.ShapeDtypeStruct((B, nt, D_MODEL), BF16)],
        compiler_params=pltpu.CompilerParams(
            dimension_semantics=("parallel", "parallel"), vmem_limit_bytes=VMEM_LIMIT),
        name="in_proj",
    )(x, positions.reshape(B, S, 1), g_mix.reshape(1, D_MODEL), w_in, invf, *after)


S5_TS = 128
S5_BATCH = 4
S5_COLS = 512


def _s5_kernel(u_ref, c0_ref, bre_ref, bim_ref, a1r_ref, a1i_ref, pr_ref, pi_ref,
               cre_ref, cim_ref, d_ref, wglu_ref, y_ref, c1_ref,
               xr, xi, cr, ci, ysc):
    rows = xr.shape[0]
    ts = rows // S5_BATCH

    @pl.when(pl.program_id(0) == 0)
    def _():
        cr[...] = c0_ref[0]
        ci[...] = c0_ref[1]

    u = u_ref[...]
    xr[...] = _dot(u, bre_ref[...])
    xi[...] = _dot(u, bim_ref[...])

    hi_rows = lax.broadcasted_iota(jnp.int32, (SUBLANES, S5_COLS), 0) >= S5_BATCH
    for cb in range(D_STATE // S5_COLS):
        sl = slice(cb * S5_COLS, (cb + 1) * S5_COLS)
        a_r, a_i = a1r_ref[:, sl], a1i_ref[:, sl]
        p_r, p_i = pr_ref[:, sl], pi_ref[:, sl]

        def body(t, carry):
            c_r, c_i = carry
            r0 = pl.multiple_of(t * SUBLANES, SUBLANES)
            x_r = xr[pl.ds(r0, SUBLANES), sl]
            x_i = xi[pl.ds(r0, SUBLANES), sl]
            s_r = pltpu.roll(x_r, S5_BATCH, 0)
            s_i = pltpu.roll(x_i, S5_BATCH, 0)
            h_r = x_r + (a_r * s_r - a_i * s_i) + (p_r * c_r - p_i * c_i)
            h_i = x_i + (a_r * s_i + a_i * s_r) + (p_r * c_i + p_i * c_r)
            xr[pl.ds(r0, SUBLANES), sl] = h_r
            xi[pl.ds(r0, SUBLANES), sl] = h_i
            n_r = jnp.where(hi_rows, h_r, pltpu.roll(h_r, S5_BATCH, 0))
            n_i = jnp.where(hi_rows, h_i, pltpu.roll(h_i, S5_BATCH, 0))
            return n_r, n_i

        c_r, c_i = lax.fori_loop(0, rows // SUBLANES, body, (cr[:, sl], ci[:, sl]), unroll=2)
        cr[:, sl] = c_r
        ci[:, sl] = c_i

    y = (_dot(xr[...].astype(BF16), cre_ref[...]) - _dot(xi[...].astype(BF16), cim_ref[...])
         + d_ref[...] * u.astype(F32))
    y = jax.nn.gelu(y)
    y = y * jax.nn.sigmoid(_dot(y.astype(BF16), wglu_ref[...]))
    for c in range(D_SSM // LANES):
        ysc[c] = y[:, c * LANES:(c + 1) * LANES]
    for b in range(S5_BATCH):
        for c in range(D_SSM // LANES):
            y_ref[b, :, c * LANES:(c + 1) * LANES] = (
                ysc[c, pl.ds(b, ts, stride=S5_BATCH), :].astype(BF16))

    @pl.when(pl.program_id(0) == pl.num_programs(0) - 1)
    def _():
        c1_ref[0] = cr[...]
        c1_ref[1] = ci[...]


def _s5_tables(log_dt, a_re, a_im, b_re, b_im, c_re, c_im):
    dt = jnp.exp(log_dt.astype(F32))[:, None]
    ar, ai = a_re.astype(F32), a_im.astype(F32)
    mag = jnp.exp(dt * ar)
    abar_re, abar_im = mag * jnp.cos(dt * ai), mag * jnp.sin(dt * ai)
    den = ar * ar + ai * ai
    nr, ni = abar_re - 1.0, abar_im
    f_re = (nr * ar + ni * ai) / den
    f_im = (ni * ar - nr * ai) / den
    br, bi = b_re.astype(F32), b_im.astype(F32)
    bb_re = f_re[..., None] * br - f_im[..., None] * bi
    bb_im = f_re[..., None] * bi + f_im[..., None] * br
    eye = jnp.eye(SSM_GROUPS, dtype=F32)

    def in_blockdiag(bb):
        return jnp.einsum('gnc,gh->gchn', bb, eye).reshape(D_SSM, D_STATE)

    def out_blockdiag(c):
        return jnp.einsum('gcn,gh->gnhc', c.astype(F32), eye).reshape(D_STATE, D_SSM)

    a_r = abar_re.reshape(1, D_STATE)
    a_i = abar_im.reshape(1, D_STATE)
    a2_r = a_r * a_r - a_i * a_i
    a2_i = 2.0 * a_r * a_i
    hi = (jnp.arange(SUBLANES) >= S5_BATCH)[:, None]
    a1r = jnp.where(hi, a_r, 0.0)
    a1i = jnp.where(hi, a_i, 0.0)
    p_r = jnp.where(hi, a2_r, a_r)
    p_i = jnp.where(hi, a2_i, a_i)
    return (in_blockdiag(bb_re).astype(BF16), in_blockdiag(bb_im).astype(BF16),
            a1r, a1i, p_r, p_i,
            out_blockdiag(c_re).astype(BF16), out_blockdiag(c_im).astype(BF16))


def _s5(u_sb, carry, tables, d_skip, w_glu, B):
    assert B == S5_BATCH
    nt = u_sb.shape[0]
    ts = min(S5_TS, nt)
    rows = ts * B
    bre, bim, a1r, a1i, p_r, p_i, cre, cim = tables
    full = lambda shape: pl.BlockSpec(shape, lambda i: (0,) * len(shape))
    return pl.pallas_call(
        _s5_kernel,
        grid=(nt // ts,),
        in_specs=[pl.BlockSpec((rows, D_SSM), lambda i: (i, 0)),
                  full((2, SUBLANES, D_STATE)),
                  full((D_SSM, D_STATE)), full((D_SSM, D_STATE)),
                  full((SUBLANES, D_STATE)), full((SUBLANES, D_STATE)),
                  full((SUBLANES, D_STATE)), full((SUBLANES, D_STATE)),
                  full((D_STATE, D_SSM)), full((D_STATE, D_SSM)),
                  full((1, D_SSM)), full((D_SSM, D_SSM))],
        out_specs=[pl.BlockSpec((B, ts, D_SSM), lambda i: (0, i, 0)),
                   full((2, SUBLANES, D_STATE))],
        out_shape=[jax.ShapeDtypeStruct((B, nt, D_SSM), BF16),
                   jax.ShapeDtypeStruct((2, SUBLANES, D_STATE), F32)],
        scratch_shapes=[pltpu.VMEM((rows, D_STATE), F32), pltpu.VMEM((rows, D_STATE), F32),
                        pltpu.VMEM((SUBLANES, D_STATE), F32), pltpu.VMEM((SUBLANES, D_STATE), F32),
                        pltpu.VMEM((D_SSM // LANES, rows, LANES), F32)],
        compiler_params=pltpu.CompilerParams(
            dimension_semantics=("arbitrary",), vmem_limit_bytes=VMEM_LIMIT),
        name="s5",
    )(u_sb.reshape(nt * B, D_SSM), carry, bre, bim, a1r, a1i, p_r, p_i, cre, cim, d_skip, w_glu)


MOBA_PAIR = 2 * MOBA_BLOCK


def _moba_kernel(q0, q_ref, k_ref, v_ref, o_ref, kmean, kaug_a, kaug_b, vaug_a, vaug_b, m_s, acc_s,
                 s_buf):
    qi = pl.program_id(2) + q0
    nb = k_ref.shape[0] // MOBA_BLOCK
    nbp = kmean.shape[0]
    lane = lax.broadcasted_iota(jnp.int32, (1, LANES), 1)
    head_a = lane < HEAD_DIM

    @pl.when(pl.program_id(2) == 0)
    def _():
        kmean[...] = jnp.zeros_like(kmean)
        for j in range(nb):
            rows = pl.ds(j * MOBA_BLOCK, MOBA_BLOCK)
            kj = k_ref[rows, :].astype(F32)
            vj = v_ref[rows, :].astype(F32)
            kmean[j:j + 1, :] = jnp.sum(kj, axis=0, keepdims=True) * (1.0 / MOBA_BLOCK)
            kaug_a[rows, :] = jnp.where(head_a, kj, jnp.where(lane - HEAD_DIM == j, 1.0, 0.0)).astype(BF16)
            kaug_b[rows, :] = jnp.where(head_a, jnp.where(lane == j, 1.0, 0.0), kj).astype(BF16)
            vaug_a[rows, :] = jnp.where(head_a, vj, 1.0).astype(BF16)
            vaug_b[rows, :] = jnp.where(head_a, 1.0, vj).astype(BF16)

    qf = q_ref[...].astype(F32)
    blk_row = lax.broadcasted_iota(jnp.int32, (nbp, MOBA_BLOCK), 0)
    q_augs = []
    for is_a in (True, False):
        mine = head_a if is_a else jnp.logical_not(head_a)
        q_own = jnp.where(mine, qf, 0.0)
        g = _dot_nt(kmean[...], q_own, precision=HIGHEST)
        g = jnp.where(blk_row < qi, g, NEG)
        sel = jnp.zeros(g.shape, F32)
        for _ in range(MOBA_TOPK):
            m = jnp.max(g, axis=0, keepdims=True)
            idx = jnp.min(jnp.where(g == m, blk_row, nbp), axis=0, keepdims=True)
            hit = blk_row == idx
            sel = jnp.where(hit, jnp.where(idx < qi, 1.0, 0.0), sel)
            g = jnp.where(hit, -jnp.inf, g)
        bias_t = jnp.where(sel > 0.0, 0.0, jnp.where(blk_row == qi, 0.0, NEG))
        bias_t = jnp.concatenate([bias_t, jnp.full((LANES - nbp, MOBA_BLOCK), NEG, F32)], axis=0)
        bias = jnp.transpose(bias_t)
        if is_a:
            bias = pltpu.roll(bias, HEAD_DIM, 1)
        q_augs.append(jnp.where(mine, qf, bias).astype(BF16))

    m_s[...] = jnp.full(m_s.shape, -jnp.inf, F32)
    acc_s[...] = jnp.zeros_like(acc_s)
    qpos = qi * MOBA_BLOCK + lax.broadcasted_iota(jnp.int32, (MOBA_BLOCK, MOBA_PAIR), 0)
    col = lax.broadcasted_iota(jnp.int32, (MOBA_BLOCK, MOBA_PAIR), 1)

    def kv_rows(jj):
        return pl.ds(pl.multiple_of(jj * MOBA_PAIR, MOBA_PAIR), MOBA_PAIR)

    def scores(jj, slot):
        for hd, kaug in enumerate((kaug_a, kaug_b)):
            s_buf[slot, hd] = _dot_nt(q_augs[hd], kaug[kv_rows(jj), :])

    def softmax_pv(jj, slot, causal):
        for hd, vaug in enumerate((vaug_a, vaug_b)):
            s = s_buf[slot, hd]
            if causal:
                s = jnp.where(jj * MOBA_PAIR + col <= qpos, s, NEG)
            m_old = m_s[hd]
            m_new = jnp.maximum(m_old, jnp.max(s, axis=-1, keepdims=True))
            alpha = jnp.exp(m_old - m_new)
            p = jnp.exp(s - m_new)
            m_s[hd] = m_new
            acc_s[hd] = alpha * acc_s[hd] + _dot(p.astype(BF16), vaug[kv_rows(jj), :])

    last = qi // 2
    scores(0, 0)

    def body(k, _):
        scores(2 * k + 1, 1)
        softmax_pv(2 * k, 0, False)
        scores(2 * k + 2, 0)
        softmax_pv(2 * k + 1, 1, False)
        return 0

    lax.fori_loop(0, last // 2, body, 0)

    @pl.when(last % 2 == 0)
    def _():
        softmax_pv(last, 0, True)

    @pl.when(last % 2 == 1)
    def _():
        scores(last, 1)
        softmax_pv(last - 1, 0, False)
        softmax_pv(last, 1, True)
    acc_a, acc_b = acc_s[0], acc_s[1]
    o_ref[...] = jnp.where(head_a, acc_a / pltpu.roll(acc_a, HEAD_DIM, 1),
                           acc_b / pltpu.roll(acc_b, HEAD_DIM, 1)).astype(BF16)


def _moba(q, k, v, q0):
    B = q.shape[0]
    nq = q.shape[1] // MOBA_BLOCK
    skv = (q0 + nq) * MOBA_BLOCK
    nb = skv // MOBA_BLOCK
    assert nb <= HEAD_DIM and nb % 2 == 0 and skv <= k.shape[1]
    nbp = -(-nb // SUBLANES) * SUBLANES
    blk = pl.BlockSpec((None, MOBA_BLOCK, LANES), lambda b, h, i: (b, i, h))
    seq = pl.BlockSpec((None, skv, LANES), lambda b, h, i: (b, 0, h))
    return pl.pallas_call(
        functools.partial(_moba_kernel, q0),
        grid=(B, D_ATT // LANES, nq),
        in_specs=[blk, seq, seq],
        out_specs=blk,
        out_shape=jax.ShapeDtypeStruct(q.shape, BF16),
        scratch_shapes=[pltpu.VMEM((nbp, LANES), F32),
                        pltpu.VMEM((skv, LANES), BF16), pltpu.VMEM((skv, LANES), BF16),
                        pltpu.VMEM((skv, LANES), BF16), pltpu.VMEM((skv, LANES), BF16),
                        pltpu.VMEM((2, MOBA_BLOCK, 1), F32),
                        pltpu.VMEM((2, MOBA_BLOCK, LANES), F32),
                        pltpu.VMEM((2, 2, MOBA_BLOCK, MOBA_PAIR), F32)],
        compiler_params=pltpu.CompilerParams(
            dimension_semantics=("parallel", "parallel", "arbitrary"), vmem_limit_bytes=VMEM_LIMIT),
        name="moba",
    )(q, k, v)


MERGE_TS = 256


def _bf16_bits(x):
    b = pltpu.bitcast(x, jnp.int32)
    r = b + 0x7FFF + (lax.shift_right_logical(b, 16) & 1)
    return lax.shift_right_logical(r, 16)


def _merge_kernel(x_ref, ys_ref, at_ref, ga_ref, gb_ref, wa_ref, wb_ref, wo_ref, g_ref,
                  wq_ref, k1_ref, k2_ref, x1_ref, hw_ref, sc_ref):
    ya = _dot(ys_ref[...], wa_ref[...])
    yb = _dot(at_ref[...], wb_ref[...])
    merged = ga_ref[...].astype(F32) * ya + gb_ref[...].astype(F32) * yb
    x1 = x_ref[...] + _dot(merged.astype(BF16), wo_ref[...])
    x1_ref[...] = x1
    hq = _rms(x1, g_ref[...])
    half = D_MODEL // 2
    hw_ref[...] = _bf16_bits(hq[:, :half]) | lax.shift_left(_bf16_bits(hq[:, half:]), 16)
    qp = _dot(hq.astype(BF16), wq_ref[...])
    for h in range(PEER_HEADS):
        o = h * PEER_QDIM
        sc_ref[2 * h] = _dot_nt(k1_ref[h], qp[:, o:o + PEER_HALF], precision=HIGHEST)
        sc_ref[2 * h + 1] = _dot_nt(k2_ref[h], qp[:, o + PEER_HALF:o + PEER_QDIM], precision=HIGHEST)


def _merge(x, ys, att, ga, gb, t0, w_proj_ssm, w_proj_att, w_out, g_ffn, peer_w_q, keys1, keys2):
    B, nt = ys.shape[0], ys.shape[1]
    ts = min(MERGE_TS, nt)
    nblk = nt // ts
    i0 = t0 // ts
    tok = lambda d: pl.BlockSpec((None, ts, d), lambda b, i: (b, i, 0))
    row = lambda d: pl.BlockSpec((ts, d), lambda b, i: (b * nblk + i, 0))
    full = lambda shape: pl.BlockSpec(shape, lambda b, i: (0,) * len(shape))
    qd = PEER_HEADS * PEER_QDIM
    return pl.pallas_call(
        _merge_kernel,
        grid=(B, nblk),
        in_specs=[pl.BlockSpec((None, ts, D_MODEL), lambda b, i: (b, i0 + i, 0)),
                  tok(D_SSM), tok(D_ATT), tok(D_MODEL), tok(D_MODEL),
                  full((D_SSM, D_MODEL)), full((D_ATT, D_MODEL)), full((D_MODEL, D_MODEL)),
                  full((1, D_MODEL)), full((D_MODEL, qd)),
                  full((PEER_HEADS, PEER_KEYS, PEER_HALF)), full((PEER_HEADS, PEER_KEYS, PEER_HALF))],
        out_specs=[row(D_MODEL), row(D_MODEL // 2),
                   pl.BlockSpec((2 * PEER_HEADS, PEER_KEYS, ts), lambda b, i: (0, 0, b * nblk + i))],
        out_shape=[jax.ShapeDtypeStruct((B * nt, D_MODEL), F32),
                   jax.ShapeDtypeStruct((B * nt, D_MODEL // 2), jnp.int32),
                   jax.ShapeDtypeStruct((2 * PEER_HEADS, PEER_KEYS, B * nt), F32)],
        compiler_params=pltpu.CompilerParams(
            dimension_semantics=("parallel", "parallel"), vmem_limit_bytes=VMEM_LIMIT),
        name="merge",
    )(x, ys, att, ga, gb, w_proj_ssm, w_proj_att, w_out, g_ffn, peer_w_q, keys1, keys2)


TOPK_TS = 256


def _top_rows(s, row, k):
    vals, idxs = [], []
    for _ in range(k):
        m = jnp.max(s, axis=0, keepdims=True)
        idx = jnp.min(jnp.where(s == m, row, s.shape[0]), axis=0, keepdims=True)
        vals.append(m)
        idxs.append(idx)
        s = jnp.where(row == idx, -jnp.inf, s)
    return vals, idxs


def _stack_rows(rows, row16):
    acc = jnp.zeros(row16.shape, rows[0].dtype)
    for r, v in enumerate(rows):
        acc = jnp.where(row16 == r, v, acc)
    return acc


def _topk_kernel(sc_ref, idx_ref, gate_ref):
    ts = sc_ref.shape[-1]
    row = lax.broadcasted_iota(jnp.int32, (PEER_KEYS, ts), 0).astype(F32)
    row16 = lax.broadcasted_iota(jnp.int32, (PEER_TOPK, ts), 0)
    row8 = lax.broadcasted_iota(jnp.int32, (SUBLANES, ts), 0)
    counts = [PEER_TOPK // (i + 1) for i in range(PEER_TOPK)]
    heights = [PEER_TOPK if c > SUBLANES else SUBLANES for c in counts]
    n_cand = sum(heights)
    rowc = lax.broadcasted_iota(jnp.int32, (n_cand, ts), 0).astype(F32)
    gate_rows, eid_rows = [], []
    for h in range(PEER_HEADS):
        v1, i1 = _top_rows(sc_ref[2 * h], row, PEER_TOPK)
        v2, i2 = _top_rows(sc_ref[2 * h + 1], row, PEER_TOPK)
        v2s = _stack_rows(v2, row16)
        i2s = _stack_rows(i2, row16)
        cand, eid = [], []
        for i in range(PEER_TOPK):
            n = heights[i]
            cand.append(jnp.where((row16 if n == PEER_TOPK else row8) < counts[i],
                                  v1[i] + v2s[:n], -jnp.inf))
            eid.append(i1[i] * PEER_KEYS + i2s[:n])
        cand = jnp.concatenate(cand, axis=0)
        eid = jnp.concatenate(eid, axis=0)
        tops, picks = [], []
        for _ in range(PEER_TOPK):
            m = jnp.max(cand, axis=0, keepdims=True)
            pos = jnp.min(jnp.where(cand == m, rowc, n_cand), axis=0, keepdims=True)
            hit = rowc == pos
            picks.append(jnp.max(jnp.where(hit, eid, -1.0), axis=0, keepdims=True))
            tops.append(m)
            cand = jnp.where(hit, -jnp.inf, cand)
        top = _stack_rows(tops, row16)
        p = jnp.exp(top - jnp.max(top, axis=0, keepdims=True))
        gate_rows.append(p / jnp.sum(p, axis=0, keepdims=True))
        eid_rows.append(_stack_rows(picks, row16))
    gate_ref[...] = jnp.transpose(jnp.concatenate(gate_rows, axis=0))
    idx_ref[...] = jnp.transpose(jnp.concatenate(eid_rows, axis=0)).astype(jnp.int32)


def _topk(scores):
    T = scores.shape[-1]
    ts = min(TOPK_TS, T)
    return pl.pallas_call(
        _topk_kernel,
        grid=(T // ts,),
        in_specs=[pl.BlockSpec((2 * PEER_HEADS, PEER_KEYS, ts), lambda i: (0, 0, i))],
        out_specs=[pl.BlockSpec((ts, PEER_SEL), lambda i: (i, 0)),
                   pl.BlockSpec((ts, PEER_SEL), lambda i: (i, 0))],
        out_shape=[jax.ShapeDtypeStruct((T, PEER_SEL), jnp.int32),
                   jax.ShapeDtypeStruct((T, PEER_SEL), F32)],
        compiler_params=pltpu.CompilerParams(
            dimension_semantics=("parallel",), vmem_limit_bytes=VMEM_LIMIT),
        name="topk",
    )(scores)


SC_CORES = 2
SC_SUBCORES = 16
SC_LANES = 16
SC_WORKERS = SC_CORES * SC_SUBCORES
PEER_CH = SC_LANES
PEER_NCH = PEER_SEL // PEER_CH
PEER_WORDS = D_MODEL // 2
PEER_NWG = PEER_WORDS // SC_LANES
PEER_RING = 4
PEER_QUAD = 4
HI_MASK = -65536
GELU_C = 0.7978845608028654


def _gelu_tanh_via_exp(x):
    z = GELU_C * (x + 0.044715 * (x * x * x))
    t = 1.0 - 2.0 / (jnp.exp(2.0 * z) + 1.0)
    return 0.5 * x * (1.0 + t)


def _unpack_pair(w):
    lo = plsc.bitcast(lax.shift_left(w, 16), F32)
    hi = plsc.bitcast(lax.bitwise_and(w, HI_MASK), F32)
    return lo, hi


def _peer_sc_body(idx_hbm, gate_hbm, h_hbm, uv_hbm, after_hbm, o_hbm,
                  idx_v, gate_v, h_v, buf, out_v, gsem, msem, osem):
    n_tok = o_hbm.shape[0] // SC_WORKERS
    base = (lax.axis_index("s") * SC_CORES + lax.axis_index("c")) * n_tok
    lane = lax.iota(jnp.int32, SC_LANES)
    zero_rows = jnp.zeros((SC_LANES,), jnp.int32)

    def meta_copies(tok, s):
        return (pltpu.make_async_copy(idx_hbm.at[tok], idx_v.at[s], msem.at[s]),
                pltpu.make_async_copy(gate_hbm.at[tok], gate_v.at[s], msem.at[s]),
                pltpu.make_async_copy(h_hbm.at[tok], h_v.at[s], msem.at[s]))

    def gather(slot, rows):
        return pltpu.make_async_copy(uv_hbm.at[rows], buf.at[slot], gsem.at[slot])

    def token(t, carry):
        s = t % 2
        tok = base + t
        nxt = base + jnp.minimum(t + 1, n_tok - 1)
        for cp in meta_copies(nxt, 1 - s):
            cp.start()

        @pl.when(t >= 2)
        def _():
            pltpu.make_async_copy(out_v.at[s], o_hbm.at[tok], osem.at[s]).wait()

        def chunk(c, carry):
            slot = c % PEER_RING
            gather(slot, zero_rows).wait()

            def dot_step(q, accs):
                cols = [pl.ds(pl.multiple_of((q * PEER_QUAD + j) * SC_LANES, SC_LANES), SC_LANES)
                        for j in range(PEER_QUAD)]
                hs = [plsc.bitcast(h_v[s, col], BF16) for col in cols]
                out = []
                for r in range(PEER_CH):
                    p = plsc.bitcast(buf[slot, r, cols[0]], BF16) * hs[0]
                    for j in range(1, PEER_QUAD):
                        p = p + plsc.bitcast(buf[slot, r, cols[j]], BF16) * hs[j]
                    lo, hi = _unpack_pair(plsc.bitcast(p, jnp.int32))
                    out.append(accs[r] + lo + hi)
                return tuple(out)

            accs = lax.fori_loop(0, PEER_NWG // PEER_QUAD, dot_step,
                                 tuple(jnp.zeros((SC_LANES,), F32) for _ in range(PEER_CH)))
            tot = jnp.zeros((SC_LANES,), F32)
            for r in range(PEER_CH):
                tot = jnp.where(lane == r, jnp.sum(accs[r]), tot)
            rows = pl.ds(pl.multiple_of(c * PEER_CH, PEER_CH), PEER_CH)
            wvec = gate_v[s, rows] * _gelu_tanh_via_exp(tot)
            ws = []
            for r in range(PEER_CH):
                w = wvec.at[jnp.full((SC_LANES,), r, jnp.int32)].get(mode="promise_in_bounds")
                ws.append(plsc.pack(w, w, format=plsc.PackFormat.INTERLEAVED,
                                    preferred_element_type=BF16))
            first = c == 0

            @plsc.parallel_loop(0, PEER_NWG, unroll=2)
            def acc_step(g):
                col = pl.ds(pl.multiple_of(g * SC_LANES, SC_LANES), SC_LANES)
                col_v = pl.ds(pl.multiple_of(PEER_WORDS + g * SC_LANES, SC_LANES), SC_LANES)
                o_lo = jnp.where(first, 0.0, out_v[s, col])
                o_hi = jnp.where(first, 0.0, out_v[s, col_v])
                for r0 in range(0, PEER_CH, PEER_QUAD):
                    p = plsc.bitcast(buf[slot, r0, col_v], BF16) * ws[r0]
                    for r in range(r0 + 1, r0 + PEER_QUAD):
                        p = p + plsc.bitcast(buf[slot, r, col_v], BF16) * ws[r]
                    lo, hi = _unpack_pair(plsc.bitcast(p, jnp.int32))
                    o_lo = o_lo + lo
                    o_hi = o_hi + hi
                out_v[s, col] = o_lo
                out_v[s, col_v] = o_hi

            @pl.when(c == PEER_NCH - PEER_RING)
            def _():
                for cp in meta_copies(nxt, 1 - s):
                    cp.wait()

            ahead = c + PEER_RING
            src = jnp.where(ahead < PEER_NCH, s, 1 - s)
            nrows = idx_v[src, pl.ds(pl.multiple_of((ahead % PEER_NCH) * PEER_CH, PEER_CH), PEER_CH)]
            gather(slot, nrows).start()
            return carry

        lax.fori_loop(0, PEER_NCH, chunk, 0)
        pltpu.make_async_copy(out_v.at[s], o_hbm.at[tok], osem.at[s]).start()
        return carry

    for cp in meta_copies(base, 0):
        cp.start()
    for cp in meta_copies(base, 0):
        cp.wait()
    for c in range(PEER_RING):
        gather(c, idx_v[0, pl.ds(c * PEER_CH, PEER_CH)]).start()
    lax.fori_loop(0, n_tok, token, 0)
    for c in range(PEER_RING):
        gather(c, zero_rows).wait()
    for s in range(2):
        pltpu.make_async_copy(out_v.at[s], o_hbm.at[base], osem.at[s]).wait()


def _pack_bf16_pairs(tab):
    b = lax.bitcast_convert_type(tab.astype(BF16), jnp.uint16).astype(jnp.uint32)
    half = tab.shape[1] // 2
    return lax.bitcast_convert_type(b[:, :half] | (b[:, half:] << 16), jnp.int32)


def _peer(idx, h_words, gates, uv_words, after):
    T = h_words.shape[0]
    assert T % (2 * SC_WORKERS) == 0
    mesh = plsc.VectorSubcoreMesh(core_axis_name="c", subcore_axis_name="s",
                                  num_cores=SC_CORES, num_subcores=SC_SUBCORES)
    return pl.kernel(
        _peer_sc_body,
        out_type=jax.ShapeDtypeStruct((T, D_MODEL), F32),
        mesh=mesh,
        scratch_types=[
            pltpu.VMEM((2, PEER_SEL), jnp.int32), pltpu.VMEM((2, PEER_SEL), F32),
            pltpu.VMEM((2, PEER_WORDS), jnp.int32),
            pltpu.VMEM((PEER_RING, PEER_CH, 2 * PEER_WORDS), jnp.int32),
            pltpu.VMEM((2, D_MODEL), F32),
            pltpu.SemaphoreType.DMA((PEER_RING,)),
            pltpu.SemaphoreType.DMA((2,)), pltpu.SemaphoreType.DMA((2,)),
        ],
        compiler_params=pltpu.CompilerParams(needs_layout_passes=False),
        name="peer_sc",
    )(idx, gates, h_words, uv_words, after)


FINAL_TS = 256


def _final_kernel(x1_ref, pe_ref, p_ref, gp_ref, wg_ref, wp_ref, gf_ref, o_ref):
    x2 = x1_ref[...] + pe_ref[...]
    e = _dot(p_ref[...].astype(BF16), wp_ref[...])
    gate = jax.nn.sigmoid(_dot(_rms(x2, gp_ref[...]).astype(BF16), wg_ref[...]))
    o_ref[...] = _rms(x2 + gate * e, gf_ref[...])


def _final(x1, peer_out, p, t0, nt, g_ple, ple_w_gate, ple_w_proj, g_final):
    B = p.shape[0]
    ts = min(FINAL_TS, nt)
    nblk = nt // ts
    i0 = t0 // ts
    row = lambda d: pl.BlockSpec((ts, d), lambda b, i: (b * nblk + i, 0))
    full = lambda shape: pl.BlockSpec(shape, lambda b, i: (0,) * len(shape))
    return pl.pallas_call(
        _final_kernel,
        grid=(B, nblk),
        in_specs=[row(D_MODEL), row(D_MODEL),
                  pl.BlockSpec((None, ts, D_PLE), lambda b, i: (b, i0 + i, 0)),
                  full((1, D_MODEL)), full((D_MODEL, D_MODEL)), full((D_PLE, D_MODEL)),
                  full((1, D_MODEL))],
        out_specs=pl.BlockSpec((None, ts, D_MODEL), lambda b, i: (b, i, 0)),
        out_shape=jax.ShapeDtypeStruct((B, nt, D_MODEL), F32),
        compiler_params=pltpu.CompilerParams(
            dimension_semantics=("parallel", "parallel"), vmem_limit_bytes=VMEM_LIMIT),
        name="final",
    )(x1, peer_out, p, g_ple, ple_w_gate, ple_w_proj, g_final)


CHUNK_STEPS = (512, 512, 1024, 1024, 1024, 1024, 1024, 1024, 512, 512)


def kernel(x, p, positions, g_mix, w_in, ssm_log_dt, ssm_a_re, ssm_a_im, ssm_b_re, ssm_b_im,
           ssm_c_re, ssm_c_im, ssm_d, ssm_w_glu, w_proj_ssm, w_proj_att, w_out, g_ffn,
           peer_w_q, peer_keys1, peer_keys2, peer_u, peer_v, g_ple, ple_w_gate, ple_w_proj,
           g_final):
    B, S, _ = x.shape
    assert w_in.shape[0] == 1, "the final rmsnorm is fused into the single layer's last stage"
    steps = CHUNK_STEPS if sum(CHUNK_STEPS) == S else (S,)
    i = 0
    tables = _s5_tables(ssm_log_dt[i], ssm_a_re[i], ssm_a_im[i], ssm_b_re[i], ssm_b_im[i],
                        ssm_c_re[i], ssm_c_im[i])
    w_in_b, w_glu_b = w_in[i].astype(BF16), ssm_w_glu[i].astype(BF16)
    d_skip = ssm_d[i].reshape(1, D_SSM).astype(F32)
    merge_w = (w_proj_ssm[i].astype(BF16), w_proj_att[i].astype(BF16), w_out[i].astype(BF16),
               g_ffn[i].reshape(1, D_MODEL), peer_w_q[i].astype(BF16), peer_keys1[i], peer_keys2[i])
    final_w = (g_ple[i].reshape(1, D_MODEL), ple_w_gate[i].astype(BF16),
               ple_w_proj[i].astype(BF16), g_final.reshape(1, D_MODEL))
    uv_words = jnp.concatenate([_pack_bf16_pairs(peer_u[i]), _pack_bf16_pairs(peer_v[i])], axis=1)
    k_all = jnp.zeros((B, S, D_ATT), BF16)
    v_all = jnp.zeros((B, S, D_ATT), BF16)
    carry = jnp.zeros((2, SUBLANES, D_STATE), F32)
    outs = []
    t0 = 0
    after = (carry, carry)
    peer_prev = carry
    for nt in steps:
        u_sb, q, k, v, ga, gb = _in_proj(x, positions, g_mix[i], w_in_b, t0, nt, after)
        k_all = lax.dynamic_update_slice(k_all, k, (0, t0, 0))
        v_all = lax.dynamic_update_slice(v_all, v, (0, t0, 0))
        ys, carry = _s5(u_sb, carry, tables, d_skip, w_glu_b, B)
        att = _moba(q, k_all, v_all, t0 // MOBA_BLOCK)
        x1, h_words, scores = _merge(x, ys, att, ga, gb, t0, *merge_w)
        idx, gates = _topk(scores)
        after = (gates, outs[-2] if len(outs) > 1 else carry)
        peer_out = _peer(idx, h_words, gates, uv_words, peer_prev)
        peer_prev = peer_out
        outs.append(_final(x1, peer_out, p[i], t0, nt, *final_w))
        t0 += nt
    return jnp.concatenate(outs, axis=1)
```

```python
import functools
import math

import jax
import jax.numpy as jnp
from jax import lax
from jax.experimental import pallas as pl
from jax.experimental.pallas import tpu as pltpu
from jax.experimental.pallas import tpu_sc as plsc

F32 = jnp.float32
BF16 = jnp.bfloat16

D_MODEL = 1024
D_SSM = 512
SSM_GROUP = 16
SSM_GROUPS = 32
SSM_STATE = 64
D_STATE = SSM_GROUPS * SSM_STATE
N_HEADS = 8
HEAD_DIM = 64
D_ATT = 512
ROT_DIM = 16
ROPE_THETA = 500000.0
MOBA_BLOCK = 256
MOBA_TOPK = 3
PEER_HEADS = 8
PEER_KEYS = 128
PEER_QDIM = 256
PEER_HALF = 128
PEER_TOPK = 16
PEER_SEL = PEER_HEADS * PEER_TOPK
D_PLE = 256
EPS = 1e-6
NEG = -1e30
LANES = 128
SUBLANES = 8
VMEM_LIMIT = 48 * 1024 * 1024
HIGHEST = lax.Precision.HIGHEST


def _rms(x, g):
    return x * lax.rsqrt(jnp.mean(x * x, axis=-1, keepdims=True) + EPS) * g


def _dot(a, b):
    return jnp.dot(a, b, preferred_element_type=F32)


def _dot_nt(a, b, precision=None):
    return lax.dot_general(a, b, (((1,), (1,)), ((), ())), precision=precision,
                           preferred_element_type=F32)


IN_TS = 512


def _in_proj_kernel(x_ref, pos_ref, g_ref, w_ref, invf_ref, after_a, after_b,
                    u_ref, q_ref, k_ref, v_ref, ga_ref, gb_ref):
    del after_a, after_b
    h = _rms(x_ref[...], g_ref[...]).astype(BF16)

    def proj(lo, hi):
        return _dot(h, w_ref[:, lo:hi])

    u_ref[...] = proj(0, D_SSM).astype(BF16)
    ang = pos_ref[...].astype(F32) * invf_ref[...]
    cos = jnp.cos(ang)
    sin = jnp.sin(ang)
    lane = lax.broadcasted_iota(jnp.int32, (1, LANES), 1) % HEAD_DIM
    half = ROT_DIM // 2
    sin_hi = jnp.where((lane >= half) & (lane < ROT_DIM), sin, 0.0)
    sin_lo = jnp.where(lane < half, -sin, 0.0)
    reps = D_ATT // LANES
    cos4 = jnp.concatenate([cos] * reps, axis=1)
    sin_hi4 = jnp.concatenate([sin_hi] * reps, axis=1)
    sin_lo4 = jnp.concatenate([sin_lo] * reps, axis=1)

    def rope(t):
        return (t * cos4 + pltpu.roll(t, half, 1) * sin_hi4
                + pltpu.roll(t, D_ATT - half, 1) * sin_lo4)

    q = rope(proj(D_SSM, D_SSM + D_ATT))
    q_ref[...] = (q * (HEAD_DIM ** -0.5)).astype(BF16)
    k_ref[...] = rope(proj(D_SSM + D_ATT, D_SSM + 2 * D_ATT)).astype(BF16)
    v_ref[...] = proj(D_SSM + 2 * D_ATT, D_SSM + 3 * D_ATT).astype(BF16)
    o = D_SSM + 3 * D_ATT
    ga_ref[...] = jax.nn.sigmoid(proj(o, o + D_MODEL)).astype(BF16)
    gb_ref[...] = jax.nn.sigmoid(proj(o + D_MODEL, o + 2 * D_MODEL)).astype(BF16)


def _in_proj(x, positions, g_mix, w_in, t0, nt, after):
    B, S, _ = x.shape
    ts = min(IN_TS, nt)
    assert nt % ts == 0 and t0 % ts == 0
    i0 = t0 // ts
    inv_freq = ROPE_THETA ** (-jnp.arange(0, ROT_DIM, 2, dtype=F32) / ROT_DIM)
    lane = jnp.arange(LANES) % HEAD_DIM
    invf = jnp.where(lane < ROT_DIM, inv_freq[lane % (ROT_DIM // 2)], 0.0).reshape(1, LANES)
    d_in = w_in.shape[1]
    src = lambda d: pl.BlockSpec((None, ts, d), lambda b, i: (b, i0 + i, 0))
    tok = lambda d: pl.BlockSpec((None, ts, d), lambda b, i: (b, i, 0))
    full = lambda shape: pl.BlockSpec(shape, lambda b, i: (0,) * len(shape))
    return pl.pallas_call(
        _in_proj_kernel,
        grid=(B, nt // ts),
        in_specs=[src(D_MODEL), src(1), full((1, D_MODEL)), full((D_MODEL, d_in)), full((1, LANES)),
                  pl.BlockSpec(memory_space=pl.ANY), pl.BlockSpec(memory_space=pl.ANY)],
        out_specs=[pl.BlockSpec((ts, D_SSM), lambda b, i: (i, b)),
                   tok(D_ATT), tok(D_ATT), tok(D_ATT), tok(D_MODEL), tok(D_MODEL)],
        out_shape=[jax.ShapeDtypeStruct((nt, B * D_SSM), BF16),
                   jax.ShapeDtypeStruct((B, nt, D_ATT), BF16),
                   jax.ShapeDtypeStruct((B, nt, D_ATT), BF16),
                   jax.ShapeDtypeStruct((B, nt, D_ATT), BF16),
                   jax.ShapeDtypeStruct((B, nt, D_MODEL), BF16),
                   jax.ShapeDtypeStruct((B, nt, D_MODEL), BF16)],
        compiler_params=pltpu.CompilerParams(
            dimension_semantics=("parallel", "parallel"), vmem_limit_bytes=VMEM_LIMIT),
        name="in_proj",
    )(x, positions.reshape(B, S, 1), g_mix.reshape(1, D_MODEL), w_in, invf, *after)


S5_TS = 128
S5_BATCH = 4
S5_COLS = 512


def _s5_kernel(u_ref, c0_ref, bre_ref, bim_ref, a1r_ref, a1i_ref, pr_ref, pi_ref,
               cre_ref, cim_ref, d_ref, wglu_ref, y_ref, c1_ref,
               xr, xi, cr, ci, ysc):
    rows = xr.shape[0]
    ts = rows // S5_BATCH

    @pl.when(pl.program_id(0) == 0)
    def _():
        cr[...] = c0_ref[0]
        ci[...] = c0_ref[1]

    u = u_ref[...]
    xr[...] = _dot(u, bre_ref[...])
    xi[...] = _dot(u, bim_ref[...])

    hi_rows = lax.broadcasted_iota(jnp.int32, (SUBLANES, S5_COLS), 0) >= S5_BATCH
    for cb in range(D_STATE // S5_COLS):
        sl = slice(cb * S5_COLS, (cb + 1) * S5_COLS)
        a_r, a_i = a1r_ref[:, sl], a1i_ref[:, sl]
        p_r, p_i = pr_ref[:, sl], pi_ref[:, sl]

        def body(t, carry):
            c_r, c_i = carry
            r0 = pl.multiple_of(t * SUBLANES, SUBLANES)
            x_r = xr[pl.ds(r0, SUBLANES), sl]
            x_i = xi[pl.ds(r0, SUBLANES), sl]
            s_r = pltpu.roll(x_r, S5_BATCH, 0)
            s_i = pltpu.roll(x_i, S5_BATCH, 0)
            h_r = x_r + (a_r * s_r - a_i * s_i) + (p_r * c_r - p_i * c_i)
            h_i = x_i + (a_r * s_i + a_i * s_r) + (p_r * c_i + p_i * c_r)
            xr[pl.ds(r0, SUBLANES), sl] = h_r
            xi[pl.ds(r0, SUBLANES), sl] = h_i
            n_r = jnp.where(hi_rows, h_r, pltpu.roll(h_r, S5_BATCH, 0))
            n_i = jnp.where(hi_rows, h_i, pltpu.roll(h_i, S5_BATCH, 0))
            return n_r, n_i

        c_r, c_i = lax.fori_loop(0, rows // SUBLANES, body, (cr[:, sl], ci[:, sl]), unroll=2)
        cr[:, sl] = c_r
        ci[:, sl] = c_i

    y = (_dot(xr[...].astype(BF16), cre_ref[...]) - _dot(xi[...].astype(BF16), cim_ref[...])
         + d_ref[...] * u.astype(F32))
    y = jax.nn.gelu(y)
    y = y * jax.nn.sigmoid(_dot(y.astype(BF16), wglu_ref[...]))
    for c in range(D_SSM // LANES):
        ysc[c] = y[:, c * LANES:(c + 1) * LANES]
    for b in range(S5_BATCH):
        for c in range(D_SSM // LANES):
            y_ref[b, :, c * LANES:(c + 1) * LANES] = (
                ysc[c, pl.ds(b, ts, stride=S5_BATCH), :].astype(BF16))

    @pl.when(pl.program_id(0) == pl.num_programs(0) - 1)
    def _():
        c1_ref[0] = cr[...]
        c1_ref[1] = ci[...]


def _s5_tables(log_dt, a_re, a_im, b_re, b_im, c_re, c_im):
    dt = jnp.exp(log_dt.astype(F32))[:, None]
    ar, ai = a_re.astype(F32), a_im.astype(F32)
    mag = jnp.exp(dt * ar)
    abar_re, abar_im = mag * jnp.cos(dt * ai), mag * jnp.sin(dt * ai)
    den = ar * ar + ai * ai
    nr, ni = abar_re - 1.0, abar_im
    f_re = (nr * ar + ni * ai) / den
    f_im = (ni * ar - nr * ai) / den
    br, bi = b_re.astype(F32), b_im.astype(F32)
    bb_re = f_re[..., None] * br - f_im[..., None] * bi
    bb_im = f_re[..., None] * bi + f_im[..., None] * br
    eye = jnp.eye(SSM_GROUPS, dtype=F32)

    def in_blockdiag(bb):
        return jnp.einsum('gnc,gh->gchn', bb, eye).reshape(D_SSM, D_STATE)

    def out_blockdiag(c):
        return jnp.einsum('gcn,gh->gnhc', c.astype(F32), eye).reshape(D_STATE, D_SSM)

    a_r = abar_re.reshape(1, D_STATE)
    a_i = abar_im.reshape(1, D_STATE)
    a2_r = a_r * a_r - a_i * a_i
    a2_i = 2.0 * a_r * a_i
    hi = (jnp.arange(SUBLANES) >= S5_BATCH)[:, None]
    a1r = jnp.where(hi, a_r, 0.0)
    a1i = jnp.where(hi, a_i, 0.0)
    p_r = jnp.where(hi, a2_r, a_r)
    p_i = jnp.where(hi, a2_i, a_i)
    return (in_blockdiag(bb_re).astype(BF16), in_blockdiag(bb_im).astype(BF16),
            a1r, a1i, p_r, p_i,
            out_blockdiag(c_re).astype(BF16), out_blockdiag(c_im).astype(BF16))


def _s5(u_sb, carry, tables, d_skip, w_glu, B):
    assert B == S5_BATCH
    nt = u_sb.shape[0]
    ts = min(S5_TS, nt)
    rows = ts * B
    bre, bim, a1r, a1i, p_r, p_i, cre, cim = tables
    full = lambda shape: pl.BlockSpec(shape, lambda i: (0,) * len(shape))
    return pl.pallas_call(
        _s5_kernel,
        grid=(nt // ts,),
        in_specs=[pl.BlockSpec((rows, D_SSM), lambda i: (i, 0)),
                  full((2, SUBLANES, D_STATE)),
                  full((D_SSM, D_STATE)), full((D_SSM, D_STATE)),
                  full((SUBLANES, D_STATE)), full((SUBLANES, D_STATE)),
                  full((SUBLANES, D_STATE)), full((SUBLANES, D_STATE)),
                  full((D_STATE, D_SSM)), full((D_STATE, D_SSM)),
                  full((1, D_SSM)), full((D_SSM, D_SSM))],
        out_specs=[pl.BlockSpec((B, ts, D_SSM), lambda i: (0, i, 0)),
                   full((2, SUBLANES, D_STATE))],
        out_shape=[jax.ShapeDtypeStruct((B, nt, D_SSM), BF16),
                   jax.ShapeDtypeStruct((2, SUBLANES, D_STATE), F32)],
        scratch_shapes=[pltpu.VMEM((rows, D_STATE), F32), pltpu.VMEM((rows, D_STATE), F32),
                        pltpu.VMEM((SUBLANES, D_STATE), F32), pltpu.VMEM((SUBLANES, D_STATE), F32),
                        pltpu.VMEM((D_SSM // LANES, rows, LANES), F32)],
        compiler_params=pltpu.CompilerParams(
            dimension_semantics=("arbitrary",), vmem_limit_bytes=VMEM_LIMIT),
        name="s5",
    )(u_sb.reshape(nt * B, D_SSM), carry, bre, bim, a1r, a1i, p_r, p_i, cre, cim, d_skip, w_glu)


MOBA_PAIR = 2 * MOBA_BLOCK


def _moba_kernel(q0, q_ref, k_ref, v_ref, o_ref, kmean, kaug_a, kaug_b, vaug_a, vaug_b, qaug,
                 m_s, acc_s, s_buf):
    qi = pl.program_id(2) + q0
    nb = k_ref.shape[0] // MOBA_BLOCK
    nbp = kmean.shape[0]
    lane = lax.broadcasted_iota(jnp.int32, (1, LANES), 1)
    head_a = lane < HEAD_DIM

    @pl.when(pl.program_id(2) == 0)
    def _():
        kmean[...] = jnp.zeros_like(kmean)
        for j in range(nb):
            rows = pl.ds(j * MOBA_BLOCK, MOBA_BLOCK)
            kj = k_ref[rows, :].astype(F32)
            vj = v_ref[rows, :].astype(F32)
            kmean[j:j + 1, :] = jnp.sum(kj, axis=0, keepdims=True) * (1.0 / MOBA_BLOCK)
            kaug_a[rows, :] = jnp.where(head_a, kj, jnp.where(lane - HEAD_DIM == j, 1.0, 0.0)).astype(BF16)
            kaug_b[rows, :] = jnp.where(head_a, jnp.where(lane == j, 1.0, 0.0), kj).astype(BF16)
            vaug_a[rows, :] = jnp.where(head_a, vj, 1.0).astype(BF16)
            vaug_b[rows, :] = jnp.where(head_a, 1.0, vj).astype(BF16)
        blk_row = lax.broadcasted_iota(jnp.int32, (nbp, MOBA_BLOCK), 0)
        for t in range(q_ref.shape[0] // MOBA_BLOCK):
            qt = q0 + t
            qf = q_ref[t * MOBA_BLOCK:(t + 1) * MOBA_BLOCK, :].astype(F32)
            for hd, is_a in enumerate((True, False)):
                mine = head_a if is_a else jnp.logical_not(head_a)
                q_own = jnp.where(mine, qf, 0.0)
                g = _dot_nt(kmean[...], q_own, precision=HIGHEST)
                g = jnp.where(blk_row < qt, g, NEG)
                sel = jnp.zeros(g.shape, F32)
                for _ in range(MOBA_TOPK):
                    m = jnp.max(g, axis=0, keepdims=True)
                    idx = jnp.min(jnp.where(g == m, blk_row, nbp), axis=0, keepdims=True)
                    hit = blk_row == idx
                    sel = jnp.where(hit, jnp.where(idx < qt, 1.0, 0.0), sel)
                    g = jnp.where(hit, -jnp.inf, g)
                bias_t = jnp.where(sel > 0.0, 0.0, jnp.where(blk_row == qt, 0.0, NEG))
                bias_t = jnp.concatenate([bias_t, jnp.full((LANES - nbp, MOBA_BLOCK), NEG, F32)], axis=0)
                bias = jnp.transpose(bias_t)
                if is_a:
                    bias = pltpu.roll(bias, HEAD_DIM, 1)
                qaug[hd, t * MOBA_BLOCK:(t + 1) * MOBA_BLOCK, :] = jnp.where(mine, qf, bias).astype(BF16)

    tile_rows = pl.ds(pl.multiple_of(pl.program_id(2) * MOBA_BLOCK, MOBA_BLOCK), MOBA_BLOCK)
    q_augs = [qaug[0, tile_rows, :], qaug[1, tile_rows, :]]

    m_s[...] = jnp.full(m_s.shape, -jnp.inf, F32)
    acc_s[...] = jnp.zeros_like(acc_s)
    qpos = qi * MOBA_BLOCK + lax.broadcasted_iota(jnp.int32, (MOBA_BLOCK, MOBA_PAIR), 0)
    col = lax.broadcasted_iota(jnp.int32, (MOBA_BLOCK, MOBA_PAIR), 1)

    def kv_rows(jj):
        return pl.ds(pl.multiple_of(jj * MOBA_PAIR, MOBA_PAIR), MOBA_PAIR)

    def scores(jj, slot):
        for hd, kaug in enumerate((kaug_a, kaug_b)):
            s_buf[slot, hd] = _dot_nt(q_augs[hd], kaug[kv_rows(jj), :])

    def softmax_pv(jj, slot, causal):
        for hd, vaug in enumerate((vaug_a, vaug_b)):
            s = s_buf[slot, hd]
            if causal:
                s = jnp.where(jj * MOBA_PAIR + col <= qpos, s, NEG)
            m_old = m_s[hd]
            m_new = jnp.maximum(m_old, jnp.max(s, axis=-1, keepdims=True))
            alpha = jnp.exp(m_old - m_new)
            p = jnp.exp(s - m_new)
            m_s[hd] = m_new
            acc_s[hd] = alpha * acc_s[hd] + _dot(p.astype(BF16), vaug[kv_rows(jj), :])

    last = qi // 2
    scores(0, 0)

    def body(k, _):
        scores(2 * k + 1, 1)
        softmax_pv(2 * k, 0, False)
        scores(2 * k + 2, 0)
        softmax_pv(2 * k + 1, 1, False)
        return 0

    lax.fori_loop(0, last // 2, body, 0)

    @pl.when(last % 2 == 0)
    def _():
        softmax_pv(last, 0, True)

    @pl.when(last % 2 == 1)
    def _():
        scores(last, 1)
        softmax_pv(last - 1, 0, False)
        softmax_pv(last, 1, True)
    acc_a, acc_b = acc_s[0], acc_s[1]
    o_ref[...] = jnp.where(head_a, acc_a / pltpu.roll(acc_a, HEAD_DIM, 1),
                           acc_b / pltpu.roll(acc_b, HEAD_DIM, 1)).astype(BF16)


def _moba(q, k, v, q0):
    B = q.shape[0]
    nq = q.shape[1] // MOBA_BLOCK
    skv = (q0 + nq) * MOBA_BLOCK
    nb = skv // MOBA_BLOCK
    assert nb <= HEAD_DIM and nb % 2 == 0 and skv <= k.shape[1]
    nbp = -(-nb // SUBLANES) * SUBLANES
    blk = pl.BlockSpec((None, MOBA_BLOCK, LANES), lambda b, h, i: (b, i, h))
    seq = pl.BlockSpec((None, skv, LANES), lambda b, h, i: (b, 0, h))
    return pl.pallas_call(
        functools.partial(_moba_kernel, q0),
        grid=(B, D_ATT // LANES, nq),
        in_specs=[pl.BlockSpec((None, nq * MOBA_BLOCK, LANES), lambda b, h, i: (b, 0, h)), seq, seq],
        out_specs=blk,
        out_shape=jax.ShapeDtypeStruct(q.shape, BF16),
        scratch_shapes=[pltpu.VMEM((nbp, LANES), F32),
                        pltpu.VMEM((skv, LANES), BF16), pltpu.VMEM((skv, LANES), BF16),
                        pltpu.VMEM((skv, LANES), BF16), pltpu.VMEM((skv, LANES), BF16),
                        pltpu.VMEM((2, nq * MOBA_BLOCK, LANES), BF16),
                        pltpu.VMEM((2, MOBA_BLOCK, 1), F32),
                        pltpu.VMEM((2, MOBA_BLOCK, LANES), F32),
                        pltpu.VMEM((2, 2, MOBA_BLOCK, MOBA_PAIR), F32)],
        compiler_params=pltpu.CompilerParams(
            dimension_semantics=("parallel", "parallel", "arbitrary"), vmem_limit_bytes=VMEM_LIMIT),
        name="moba",
    )(q, k, v)


MERGE_TS = 256


def _bf16_bits(x):
    b = pltpu.bitcast(x, jnp.int32)
    r = b + 0x7FFF + (lax.shift_right_logical(b, 16) & 1)
    return lax.shift_right_logical(r, 16)


def _merge_kernel(x_ref, ys_ref, at_ref, ga_ref, gb_ref, wa_ref, wb_ref, wo_ref, g_ref,
                  wq_ref, k1_ref, k2_ref, x1_ref, hw_ref, sc_ref):
    ya = _dot(ys_ref[...], wa_ref[...])
    yb = _dot(at_ref[...], wb_ref[...])
    merged = ga_ref[...].astype(F32) * ya + gb_ref[...].astype(F32) * yb
    x1 = x_ref[...] + _dot(merged.astype(BF16), wo_ref[...])
    x1_ref[...] = x1
    hq = _rms(x1, g_ref[...])
    half = D_MODEL // 2
    hw_ref[...] = _bf16_bits(hq[:, :half]) | lax.shift_left(_bf16_bits(hq[:, half:]), 16)
    qp = _dot(hq.astype(BF16), wq_ref[...])
    for h in range(PEER_HEADS):
        o = h * PEER_QDIM
        sc_ref[2 * h] = _dot_nt(k1_ref[h], qp[:, o:o + PEER_HALF], precision=HIGHEST)
        sc_ref[2 * h + 1] = _dot_nt(k2_ref[h], qp[:, o + PEER_HALF:o + PEER_QDIM], precision=HIGHEST)


def _merge(x, ys, att, ga, gb, t0, w_proj_ssm, w_proj_att, w_out, g_ffn, peer_w_q, keys1, keys2):
    B, nt = ys.shape[0], ys.shape[1]
    ts = min(MERGE_TS, nt)
    nblk = nt // ts
    i0 = t0 // ts
    tok = lambda d: pl.BlockSpec((None, ts, d), lambda b, i: (b, i, 0))
    row = lambda d: pl.BlockSpec((ts, d), lambda b, i: (b * nblk + i, 0))
    full = lambda shape: pl.BlockSpec(shape, lambda b, i: (0,) * len(shape))
    qd = PEER_HEADS * PEER_QDIM
    return pl.pallas_call(
        _merge_kernel,
        grid=(B, nblk),
        in_specs=[pl.BlockSpec((None, ts, D_MODEL), lambda b, i: (b, i0 + i, 0)),
                  tok(D_SSM), tok(D_ATT), tok(D_MODEL), tok(D_MODEL),
                  full((D_SSM, D_MODEL)), full((D_ATT, D_MODEL)), full((D_MODEL, D_MODEL)),
                  full((1, D_MODEL)), full((D_MODEL, qd)),
                  full((PEER_HEADS, PEER_KEYS, PEER_HALF)), full((PEER_HEADS, PEER_KEYS, PEER_HALF))],
        out_specs=[row(D_MODEL), row(D_MODEL // 2),
                   pl.BlockSpec((2 * PEER_HEADS, PEER_KEYS, ts), lambda b, i: (0, 0, b * nblk + i))],
        out_shape=[jax.ShapeDtypeStruct((B * nt, D_MODEL), F32),
                   jax.ShapeDtypeStruct((B * nt, D_MODEL // 2), jnp.int32),
                   jax.ShapeDtypeStruct((2 * PEER_HEADS, PEER_KEYS, B * nt), F32)],
        compiler_params=pltpu.CompilerParams(
            dimension_semantics=("parallel", "parallel"), vmem_limit_bytes=VMEM_LIMIT),
        name="merge",
    )(x, ys, att, ga, gb, w_proj_ssm, w_proj_att, w_out, g_ffn, peer_w_q, keys1, keys2)


TOPK_TS = 256


def _top_rows(s, row, k):
    vals, idxs = [], []
    for _ in range(k):
        m = jnp.max(s, axis=0, keepdims=True)
        idx = jnp.min(jnp.where(s == m, row, s.shape[0]), axis=0, keepdims=True)
        vals.append(m)
        idxs.append(idx)
        s = jnp.where(row == idx, -jnp.inf, s)
    return vals, idxs


def _stack_rows(rows, row16):
    acc = jnp.zeros(row16.shape, rows[0].dtype)
    for r, v in enumerate(rows):
        acc = jnp.where(row16 == r, v, acc)
    return acc


def _topk_kernel(sc_ref, idx_ref, gate_ref):
    ts = sc_ref.shape[-1]
    row = lax.broadcasted_iota(jnp.int32, (PEER_KEYS, ts), 0).astype(F32)
    row16 = lax.broadcasted_iota(jnp.int32, (PEER_TOPK, ts), 0)
    row8 = lax.broadcasted_iota(jnp.int32, (SUBLANES, ts), 0)
    counts = [PEER_TOPK // (i + 1) for i in range(PEER_TOPK)]
    heights = [PEER_TOPK if c > SUBLANES else SUBLANES for c in counts]
    n_cand = sum(heights)
    rowc = lax.broadcasted_iota(jnp.int32, (n_cand, ts), 0).astype(F32)
    gate_rows, eid_rows = [], []
    for h in range(PEER_HEADS):
        v1, i1 = _top_rows(sc_ref[2 * h], row, PEER_TOPK)
        v2, i2 = _top_rows(sc_ref[2 * h + 1], row, PEER_TOPK)
        v2s = _stack_rows(v2, row16)
        i2s = _stack_rows(i2, row16)
        cand, eid = [], []
        for i in range(PEER_TOPK):
            n = heights[i]
            cand.append(jnp.where((row16 if n == PEER_TOPK else row8) < counts[i],
                                  v1[i] + v2s[:n], -jnp.inf))
            eid.append(i1[i] * PEER_KEYS + i2s[:n])
        cand = jnp.concatenate(cand, axis=0)
        eid = jnp.concatenate(eid, axis=0)
        tops, picks = [], []
        for _ in range(PEER_TOPK):
            m = jnp.max(cand, axis=0, keepdims=True)
            pos = jnp.min(jnp.where(cand == m, rowc, n_cand), axis=0, keepdims=True)
            hit = rowc == pos
            picks.append(jnp.max(jnp.where(hit, eid, -1.0), axis=0, keepdims=True))
            tops.append(m)
            cand = jnp.where(hit, -jnp.inf, cand)
        top = _stack_rows(tops, row16)
        p = jnp.exp(top - jnp.max(top, axis=0, keepdims=True))
        gate_rows.append(p / jnp.sum(p, axis=0, keepdims=True))
        eid_rows.append(_stack_rows(picks, row16))
    gate_ref[...] = jnp.transpose(jnp.concatenate(gate_rows, axis=0))
    idx_ref[...] = jnp.transpose(jnp.concatenate(eid_rows, axis=0)).astype(jnp.int32)


def _topk(scores):
    T = scores.shape[-1]
    ts = min(TOPK_TS, T)
    return pl.pallas_call(
        _topk_kernel,
        grid=(T // ts,),
        in_specs=[pl.BlockSpec((2 * PEER_HEADS, PEER_KEYS, ts), lambda i: (0, 0, i))],
        out_specs=[pl.BlockSpec((ts, PEER_SEL), lambda i: (i, 0)),
                   pl.BlockSpec((ts, PEER_SEL), lambda i: (i, 0))],
        out_shape=[jax.ShapeDtypeStruct((T, PEER_SEL), jnp.int32),
                   jax.ShapeDtypeStruct((T, PEER_SEL), F32)],
        compiler_params=pltpu.CompilerParams(
            dimension_semantics=("parallel",), vmem_limit_bytes=VMEM_LIMIT),
        name="topk",
    )(scores)


SC_CORES = 2
SC_SUBCORES = 16
SC_LANES = 16
SC_WORKERS = SC_CORES * SC_SUBCORES
PEER_CH = SC_LANES
PEER_NCH = PEER_SEL // PEER_CH
PEER_WORDS = D_MODEL // 2
PEER_NWG = PEER_WORDS // SC_LANES
PEER_RING = 4
PEER_QUAD = 4
HI_MASK = -65536
GELU_C = 0.7978845608028654


def _gelu_tanh_via_exp(x):
    z = GELU_C * (x + 0.044715 * (x * x * x))
    t = 1.0 - 2.0 / (jnp.exp(2.0 * z) + 1.0)
    return 0.5 * x * (1.0 + t)


def _unpack_pair(w):
    lo = plsc.bitcast(lax.shift_left(w, 16), F32)
    hi = plsc.bitcast(lax.bitwise_and(w, HI_MASK), F32)
    return lo, hi


def _peer_sc_body(idx_hbm, gate_hbm, h_hbm, uv_hbm, after_hbm, o_hbm,
                  idx_v, gate_v, h_v, buf, out_v, gsem, msem, osem):
    n_tok = o_hbm.shape[0] // SC_WORKERS
    base = (lax.axis_index("s") * SC_CORES + lax.axis_index("c")) * n_tok
    lane = lax.iota(jnp.int32, SC_LANES)
    zero_rows = jnp.zeros((SC_LANES,), jnp.int32)

    def meta_copies(tok, s):
        return (pltpu.make_async_copy(idx_hbm.at[tok], idx_v.at[s], msem.at[s]),
                pltpu.make_async_copy(gate_hbm.at[tok], gate_v.at[s], msem.at[s]),
                pltpu.make_async_copy(h_hbm.at[tok], h_v.at[s], msem.at[s]))

    def gather(slot, rows):
        return pltpu.make_async_copy(uv_hbm.at[rows], buf.at[slot], gsem.at[slot])

    def token(t, carry):
        s = t % 2
        tok = base + t
        nxt = base + jnp.minimum(t + 1, n_tok - 1)
        for cp in meta_copies(nxt, 1 - s):
            cp.start()

        @pl.when(t >= 2)
        def _():
            pltpu.make_async_copy(out_v.at[s], o_hbm.at[tok], osem.at[s]).wait()

        def chunk(c, carry):
            slot = c % PEER_RING
            gather(slot, zero_rows).wait()

            def dot_step(q, accs):
                cols = [pl.ds(pl.multiple_of((q * PEER_QUAD + j) * SC_LANES, SC_LANES), SC_LANES)
                        for j in range(PEER_QUAD)]
                hs = [plsc.bitcast(h_v[s, col], BF16) for col in cols]
                out = []
                for r in range(PEER_CH):
                    p = plsc.bitcast(buf[slot, r, cols[0]], BF16) * hs[0]
                    for j in range(1, PEER_QUAD):
                        p = p + plsc.bitcast(buf[slot, r, cols[j]], BF16) * hs[j]
                    lo, hi = _unpack_pair(plsc.bitcast(p, jnp.int32))
                    out.append(accs[r] + lo + hi)
                return tuple(out)

            accs = lax.fori_loop(0, PEER_NWG // PEER_QUAD, dot_step,
                                 tuple(jnp.zeros((SC_LANES,), F32) for _ in range(PEER_CH)))
            tot = jnp.zeros((SC_LANES,), F32)
            for r in range(PEER_CH):
                tot = jnp.where(lane == r, jnp.sum(accs[r]), tot)
            rows = pl.ds(pl.multiple_of(c * PEER_CH, PEER_CH), PEER_CH)
            wvec = gate_v[s, rows] * _gelu_tanh_via_exp(tot)
            ws = []
            for r in range(PEER_CH):
                w = wvec.at[jnp.full((SC_LANES,), r, jnp.int32)].get(mode="promise_in_bounds")
                ws.append(plsc.pack(w, w, format=plsc.PackFormat.INTERLEAVED,
                                    preferred_element_type=BF16))
            first = c == 0

            @plsc.parallel_loop(0, PEER_NWG, unroll=2)
            def acc_step(g):
                col = pl.ds(pl.multiple_of(g * SC_LANES, SC_LANES), SC_LANES)
                col_v = pl.ds(pl.multiple_of(PEER_WORDS + g * SC_LANES, SC_LANES), SC_LANES)
                o_lo = jnp.where(first, 0.0, out_v[s, col])
                o_hi = jnp.where(first, 0.0, out_v[s, col_v])
                for r0 in range(0, PEER_CH, PEER_QUAD):
                    p = plsc.bitcast(buf[slot, r0, col_v], BF16) * ws[r0]
                    for r in range(r0 + 1, r0 + PEER_QUAD):
                        p = p + plsc.bitcast(buf[slot, r, col_v], BF16) * ws[r]
                    lo, hi = _unpack_pair(plsc.bitcast(p, jnp.int32))
                    o_lo = o_lo + lo
                    o_hi = o_hi + hi
                out_v[s, col] = o_lo
                out_v[s, col_v] = o_hi

            @pl.when(c == PEER_NCH - PEER_RING)
            def _():
                for cp in meta_copies(nxt, 1 - s):
                    cp.wait()

            ahead = c + PEER_RING
            src = jnp.where(ahead < PEER_NCH, s, 1 - s)
            nrows = idx_v[src, pl.ds(pl.multiple_of((ahead % PEER_NCH) * PEER_CH, PEER_CH), PEER_CH)]
            gather(slot, nrows).start()
            return carry

        lax.fori_loop(0, PEER_NCH, chunk, 0)
        pltpu.make_async_copy(out_v.at[s], o_hbm.at[tok], osem.at[s]).start()
        return carry

    for cp in meta_copies(base, 0):
        cp.start()
    for cp in meta_copies(base, 0):
        cp.wait()
    for c in range(PEER_RING):
        gather(c, idx_v[0, pl.ds(c * PEER_CH, PEER_CH)]).start()
    lax.fori_loop(0, n_tok, token, 0)
    for c in range(PEER_RING):
        gather(c, zero_rows).wait()
    for s in range(2):
        pltpu.make_async_copy(out_v.at[s], o_hbm.at[base], osem.at[s]).wait()


def _pack_bf16_pairs(tab):
    b = lax.bitcast_convert_type(tab.astype(BF16), jnp.uint16).astype(jnp.uint32)
    half = tab.shape[1] // 2
    return lax.bitcast_convert_type(b[:, :half] | (b[:, half:] << 16), jnp.int32)


def _peer(idx, h_words, gates, uv_words, after):
    T = h_words.shape[0]
    assert T % (2 * SC_WORKERS) == 0
    mesh = plsc.VectorSubcoreMesh(core_axis_name="c", subcore_axis_name="s",
                                  num_cores=SC_CORES, num_subcores=SC_SUBCORES)
    return pl.kernel(
        _peer_sc_body,
        out_type=jax.ShapeDtypeStruct((T, D_MODEL), F32),
        mesh=mesh,
        scratch_types=[
            pltpu.VMEM((2, PEER_SEL), jnp.int32), pltpu.VMEM((2, PEER_SEL), F32),
            pltpu.VMEM((2, PEER_WORDS), jnp.int32),
            pltpu.VMEM((PEER_RING, PEER_CH, 2 * PEER_WORDS), jnp.int32),
            pltpu.VMEM((2, D_MODEL), F32),
            pltpu.SemaphoreType.DMA((PEER_RING,)),
            pltpu.SemaphoreType.DMA((2,)), pltpu.SemaphoreType.DMA((2,)),
        ],
        compiler_params=pltpu.CompilerParams(needs_layout_passes=False),
        name="peer_sc",
    )(idx, gates, h_words, uv_words, after)


FINAL_TS = 256


def _final_kernel(x1_ref, pe_ref, p_ref, gp_ref, wg_ref, wp_ref, gf_ref, o_ref):
    x2 = x1_ref[...] + pe_ref[...]
    e = _dot(p_ref[...].astype(BF16), wp_ref[...])
    gate = jax.nn.sigmoid(_dot(_rms(x2, gp_ref[...]).astype(BF16), wg_ref[...]))
    o_ref[...] = _rms(x2 + gate * e, gf_ref[...])


def _final(x1, peer_out, p, t0, nt, g_ple, ple_w_gate, ple_w_proj, g_final):
    B = p.shape[0]
    ts = min(FINAL_TS, nt)
    nblk = nt // ts
    i0 = t0 // ts
    row = lambda d: pl.BlockSpec((ts, d), lambda b, i: (b * nblk + i, 0))
    full = lambda shape: pl.BlockSpec(shape, lambda b, i: (0,) * len(shape))
    return pl.pallas_call(
        _final_kernel,
        grid=(B, nblk),
        in_specs=[row(D_MODEL), row(D_MODEL),
                  pl.BlockSpec((None, ts, D_PLE), lambda b, i: (b, i0 + i, 0)),
                  full((1, D_MODEL)), full((D_MODEL, D_MODEL)), full((D_PLE, D_MODEL)),
                  full((1, D_MODEL))],
        out_specs=pl.BlockSpec((None, ts, D_MODEL), lambda b, i: (b, i, 0)),
        out_shape=jax.ShapeDtypeStruct((B, nt, D_MODEL), F32),
        compiler_params=pltpu.CompilerParams(
            dimension_semantics=("parallel", "parallel"), vmem_limit_bytes=VMEM_LIMIT),
        name="final",
    )(x1, peer_out, p, g_ple, ple_w_gate, ple_w_proj, g_final)


CHUNK_STEPS = (512, 512, 1024, 1024, 1024, 1024, 1024, 1024, 512, 512)


def kernel(x, p, positions, g_mix, w_in, ssm_log_dt, ssm_a_re, ssm_a_im, ssm_b_re, ssm_b_im,
           ssm_c_re, ssm_c_im, ssm_d, ssm_w_glu, w_proj_ssm, w_proj_att, w_out, g_ffn,
           peer_w_q, peer_keys1, peer_keys2, peer_u, peer_v, g_ple, ple_w_gate, ple_w_proj,
           g_final):
    B, S, _ = x.shape
    assert w_in.shape[0] == 1, "the final rmsnorm is fused into the single layer's last stage"
    steps = CHUNK_STEPS if sum(CHUNK_STEPS) == S else (S,)
    i = 0
    tables = _s5_tables(ssm_log_dt[i], ssm_a_re[i], ssm_a_im[i], ssm_b_re[i], ssm_b_im[i],
                        ssm_c_re[i], ssm_c_im[i])
    w_in_b, w_glu_b = w_in[i].astype(BF16), ssm_w_glu[i].astype(BF16)
    d_skip = ssm_d[i].reshape(1, D_SSM).astype(F32)
    merge_w = (w_proj_ssm[i].astype(BF16), w_proj_att[i].astype(BF16), w_out[i].astype(BF16),
               g_ffn[i].reshape(1, D_MODEL), peer_w_q[i].astype(BF16), peer_keys1[i], peer_keys2[i])
    final_w = (g_ple[i].reshape(1, D_MODEL), ple_w_gate[i].astype(BF16),
               ple_w_proj[i].astype(BF16), g_final.reshape(1, D_MODEL))
    uv_words = jnp.concatenate([_pack_bf16_pairs(peer_u[i]), _pack_bf16_pairs(peer_v[i])], axis=1)
    k_all = jnp.zeros((B, S, D_ATT), BF16)
    v_all = jnp.zeros((B, S, D_ATT), BF16)
    carry = jnp.zeros((2, SUBLANES, D_STATE), F32)
    outs = []
    t0 = 0
    after = (carry, carry)
    peer_prev = carry
    for nt in steps:
        u_sb, q, k, v, ga, gb = _in_proj(x, positions, g_mix[i], w_in_b, t0, nt, after)
        k_all = lax.dynamic_update_slice(k_all, k, (0, t0, 0))
        v_all = lax.dynamic_update_slice(v_all, v, (0, t0, 0))
        ys, carry = _s5(u_sb, carry, tables, d_skip, w_glu_b, B)
        att = _moba(q, k_all, v_all, t0 // MOBA_BLOCK)
        x1, h_words, scores = _merge(x, ys, att, ga, gb, t0, *merge_w)
        idx, gates = _topk(scores)
        after = (gates, outs[-2] if len(outs) > 1 else carry)
        peer_out = _peer(idx, h_words, gates, uv_words, peer_prev)
        peer_prev = peer_out
        outs.append(_final(x1, peer_out, p[i], t0, nt, *final_w))
        t0 += nt
    return jnp.concatenate(outs, axis=1)
```

```python
import functools
import math

import jax
import jax.numpy as jnp
from jax import lax
from jax.experimental import pallas as pl
from jax.experimental.pallas import tpu as pltpu
from jax.experimental.pallas import tpu_sc as plsc

F32 = jnp.float32
BF16 = jnp.bfloat16

D_MODEL = 1024
D_SSM = 512
SSM_GROUP = 16
SSM_GROUPS = 32
SSM_STATE = 64
D_STATE = SSM_GROUPS * SSM_STATE
N_HEADS = 8
HEAD_DIM = 64
D_ATT = 512
ROT_DIM = 16
ROPE_THETA = 500000.0
MOBA_BLOCK = 256
MOBA_TOPK = 3
PEER_HEADS = 8
PEER_KEYS = 128
PEER_QDIM = 256
PEER_HALF = 128
PEER_TOPK = 16
PEER_SEL = PEER_HEADS * PEER_TOPK
D_PLE = 256
EPS = 1e-6
NEG = -1e30
LANES = 128
SUBLANES = 8
VMEM_LIMIT = 48 * 1024 * 1024
HIGHEST = lax.Precision.HIGHEST


def _rms(x, g):
    return x * lax.rsqrt(jnp.mean(x * x, axis=-1, keepdims=True) + EPS) * g


def _dot(a, b):
    return jnp.dot(a, b, preferred_element_type=F32)


def _dot_nt(a, b, precision=None):
    return lax.dot_general(a, b, (((1,), (1,)), ((), ())), precision=precision,
                           preferred_element_type=F32)


IN_TS = 512


def _in_proj_kernel(x_ref, pos_ref, g_ref, w_ref, invf_ref, after_a, after_b,
                    u_ref, q_ref, k_ref, v_ref, ga_ref, gb_ref):
    del after_a, after_b
    h = _rms(x_ref[...], g_ref[...]).astype(BF16)

    def proj(lo, hi):
        return _dot(h, w_ref[:, lo:hi])

    u_ref[...] = proj(0, D_SSM).astype(BF16)
    ang = pos_ref[...].astype(F32) * invf_ref[...]
    cos = jnp.cos(ang)
    sin = jnp.sin(ang)
    lane = lax.broadcasted_iota(jnp.int32, (1, LANES), 1) % HEAD_DIM
    half = ROT_DIM // 2
    sin_hi = jnp.where((lane >= half) & (lane < ROT_DIM), sin, 0.0)
    sin_lo = jnp.where(lane < half, -sin, 0.0)
    reps = D_ATT // LANES
    cos4 = jnp.concatenate([cos] * reps, axis=1)
    sin_hi4 = jnp.concatenate([sin_hi] * reps, axis=1)
    sin_lo4 = jnp.concatenate([sin_lo] * reps, axis=1)

    def rope(t):
        return (t * cos4 + pltpu.roll(t, half, 1) * sin_hi4
                + pltpu.roll(t, D_ATT - half, 1) * sin_lo4)

    q = rope(proj(D_SSM, D_SSM + D_ATT))
    q_ref[...] = (q * (HEAD_DIM ** -0.5)).astype(BF16)
    k_ref[...] = rope(proj(D_SSM + D_ATT, D_SSM + 2 * D_ATT)).astype(BF16)
    v_ref[...] = proj(D_SSM + 2 * D_ATT, D_SSM + 3 * D_ATT).astype(BF16)
    o = D_SSM + 3 * D_ATT
    ga_ref[...] = jax.nn.sigmoid(proj(o, o + D_MODEL)).astype(BF16)
    gb_ref[...] = jax.nn.sigmoid(proj(o + D_MODEL, o + 2 * D_MODEL)).astype(BF16)


def _in_proj(x, positions, g_mix, w_in, t0, nt, after):
    B, S, _ = x.shape
    ts = min(IN_TS, nt)
    assert nt % ts == 0 and t0 % ts == 0
    i0 = t0 // ts
    inv_freq = ROPE_THETA ** (-jnp.arange(0, ROT_DIM, 2, dtype=F32) / ROT_DIM)
    lane = jnp.arange(LANES) % HEAD_DIM
    invf = jnp.where(lane < ROT_DIM, inv_freq[lane % (ROT_DIM // 2)], 0.0).reshape(1, LANES)
    d_in = w_in.shape[1]
    src = lambda d: pl.BlockSpec((None, ts, d), lambda b, i: (b, i0 + i, 0))
    tok = lambda d: pl.BlockSpec((None, ts, d), lambda b, i: (b, i, 0))
    full = lambda shape: pl.BlockSpec(shape, lambda b, i: (0,) * len(shape))
    return pl.pallas_call(
        _in_proj_kernel,
        grid=(B, nt // ts),
        in_specs=[src(D_MODEL), src(1), full((1, D_MODEL)), full((D_MODEL, d_in)), full((1, LANES)),
                  pl.BlockSpec(memory_space=pl.ANY), pl.BlockSpec(memory_space=pl.ANY)],
        out_specs=[pl.BlockSpec((ts, D_SSM), lambda b, i: (i, b)),
                   tok(D_ATT), tok(D_ATT), tok(D_ATT), tok(D_MODEL), tok(D_MODEL)],
        out_shape=[jax.ShapeDtypeStruct((nt, B * D_SSM), BF16),
                   jax.ShapeDtypeStruct((B, nt, D_ATT), BF16),
                   jax.ShapeDtypeStruct((B, nt, D_ATT), BF16),
                   jax.ShapeDtypeStruct((B, nt, D_ATT), BF16),
                   jax.ShapeDtypeStruct((B, nt, D_MODEL), BF16),
                   jax.ShapeDtypeStruct((B, nt, D_MODEL), BF16)],
        compiler_params=pltpu.CompilerParams(
            dimension_semantics=("parallel", "parallel"), vmem_limit_bytes=VMEM_LIMIT),
        name="in_proj",
    )(x, positions.reshape(B, S, 1), g_mix.reshape(1, D_MODEL), w_in, invf, *after)


S5_TS = 128
S5_BATCH = 4
S5_COLS = 512


def _s5_kernel(u_ref, c0_ref, bre_ref, bim_ref, a1r_ref, a1i_ref, pr_ref, pi_ref,
               cre_ref, cim_ref, d_ref, wglu_ref, y_ref, c1_ref,
               xr, xi, cr, ci, ysc):
    rows = xr.shape[0]
    ts = rows // S5_BATCH

    @pl.when(pl.program_id(0) == 0)
    def _():
        cr[...] = c0_ref[0]
        ci[...] = c0_ref[1]

    u = u_ref[...]
    xr[...] = _dot(u, bre_ref[...])
    xi[...] = _dot(u, bim_ref[...])

    hi_rows = lax.broadcasted_iota(jnp.int32, (SUBLANES, S5_COLS), 0) >= S5_BATCH
    for cb in range(D_STATE // S5_COLS):
        sl = slice(cb * S5_COLS, (cb + 1) * S5_COLS)
        a_r, a_i = a1r_ref[:, sl], a1i_ref[:, sl]
        p_r, p_i = pr_ref[:, sl], pi_ref[:, sl]

        def body(t, carry):
            c_r, c_i = carry
            r0 = pl.multiple_of(t * SUBLANES, SUBLANES)
            x_r = xr[pl.ds(r0, SUBLANES), sl]
            x_i = xi[pl.ds(r0, SUBLANES), sl]
            s_r = pltpu.roll(x_r, S5_BATCH, 0)
            s_i = pltpu.roll(x_i, S5_BATCH, 0)
            h_r = x_r + (a_r * s_r - a_i * s_i) + (p_r * c_r - p_i * c_i)
            h_i = x_i + (a_r * s_i + a_i * s_r) + (p_r * c_i + p_i * c_r)
            xr[pl.ds(r0, SUBLANES), sl] = h_r
            xi[pl.ds(r0, SUBLANES), sl] = h_i
            n_r = jnp.where(hi_rows, h_r, pltpu.roll(h_r, S5_BATCH, 0))
            n_i = jnp.where(hi_rows, h_i, pltpu.roll(h_i, S5_BATCH, 0))
            return n_r, n_i

        c_r, c_i = lax.fori_loop(0, rows // SUBLANES, body, (cr[:, sl], ci[:, sl]), unroll=2)
        cr[:, sl] = c_r
        ci[:, sl] = c_i

    y = (_dot(xr[...].astype(BF16), cre_ref[...]) - _dot(xi[...].astype(BF16), cim_ref[...])
         + d_ref[...] * u.astype(F32))
    y = jax.nn.gelu(y)
    y = y * jax.nn.sigmoid(_dot(y.astype(BF16), wglu_ref[...]))
    for c in range(D_SSM // LANES):
        ysc[c] = y[:, c * LANES:(c + 1) * LANES]
    for b in range(S5_BATCH):
        for c in range(D_SSM // LANES):
            y_ref[b, :, c * LANES:(c + 1) * LANES] = (
                ysc[c, pl.ds(b, ts, stride=S5_BATCH), :].astype(BF16))

    @pl.when(pl.program_id(0) == pl.num_programs(0) - 1)
    def _():
        c1_ref[0] = cr[...]
        c1_ref[1] = ci[...]


def _s5_tables(log_dt, a_re, a_im, b_re, b_im, c_re, c_im):
    dt = jnp.exp(log_dt.astype(F32))[:, None]
    ar, ai = a_re.astype(F32), a_im.astype(F32)
    mag = jnp.exp(dt * ar)
    abar_re, abar_im = mag * jnp.cos(dt * ai), mag * jnp.sin(dt * ai)
    den = ar * ar + ai * ai
    nr, ni = abar_re - 1.0, abar_im
    f_re = (nr * ar + ni * ai) / den
    f_im = (ni * ar - nr * ai) / den
    br, bi = b_re.astype(F32), b_im.astype(F32)
    bb_re = f_re[..., None] * br - f_im[..., None] * bi
    bb_im = f_re[..., None] * bi + f_im[..., None] * br
    eye = jnp.eye(SSM_GROUPS, dtype=F32)

    def in_blockdiag(bb):
        return jnp.einsum('gnc,gh->gchn', bb, eye).reshape(D_SSM, D_STATE)

    def out_blockdiag(c):
        return jnp.einsum('gcn,gh->gnhc', c.astype(F32), eye).reshape(D_STATE, D_SSM)

    a_r = abar_re.reshape(1, D_STATE)
    a_i = abar_im.reshape(1, D_STATE)
    a2_r = a_r * a_r - a_i * a_i
    a2_i = 2.0 * a_r * a_i
    hi = (jnp.arange(SUBLANES) >= S5_BATCH)[:, None]
    a1r = jnp.where(hi, a_r, 0.0)
    a1i = jnp.where(hi, a_i, 0.0)
    p_r = jnp.where(hi, a2_r, a_r)
    p_i = jnp.where(hi, a2_i, a_i)
    return (in_blockdiag(bb_re).astype(BF16), in_blockdiag(bb_im).astype(BF16),
            a1r, a1i, p_r, p_i,
            out_blockdiag(c_re).astype(BF16), out_blockdiag(c_im).astype(BF16))


def _s5(u_sb, carry, tables, d_skip, w_glu, B):
    assert B == S5_BATCH
    nt = u_sb.shape[0]
    ts = min(S5_TS, nt)
    rows = ts * B
    bre, bim, a1r, a1i, p_r, p_i, cre, cim = tables
    full = lambda shape: pl.BlockSpec(shape, lambda i: (0,) * len(shape))
    return pl.pallas_call(
        _s5_kernel,
        grid=(nt // ts,),
        in_specs=[pl.BlockSpec((rows, D_SSM), lambda i: (i, 0)),
                  full((2, SUBLANES, D_STATE)),
                  full((D_SSM, D_STATE)), full((D_SSM, D_STATE)),
                  full((SUBLANES, D_STATE)), full((SUBLANES, D_STATE)),
                  full((SUBLANES, D_STATE)), full((SUBLANES, D_STATE)),
                  full((D_STATE, D_SSM)), full((D_STATE, D_SSM)),
                  full((1, D_SSM)), full((D_SSM, D_SSM))],
        out_specs=[pl.BlockSpec((B, ts, D_SSM), lambda i: (0, i, 0)),
                   full((2, SUBLANES, D_STATE))],
        out_shape=[jax.ShapeDtypeStruct((B, nt, D_SSM), BF16),
                   jax.ShapeDtypeStruct((2, SUBLANES, D_STATE), F32)],
        scratch_shapes=[pltpu.VMEM((rows, D_STATE), F32), pltpu.VMEM((rows, D_STATE), F32),
                        pltpu.VMEM((SUBLANES, D_STATE), F32), pltpu.VMEM((SUBLANES, D_STATE), F32),
                        pltpu.VMEM((D_SSM // LANES, rows, LANES), F32)],
        compiler_params=pltpu.CompilerParams(
            dimension_semantics=("arbitrary",), vmem_limit_bytes=VMEM_LIMIT),
        name="s5",
    )(u_sb.reshape(nt * B, D_SSM), carry, bre, bim, a1r, a1i, p_r, p_i, cre, cim, d_skip, w_glu)


MOBA_PAIR = 2 * MOBA_BLOCK


def _moba_kernel(q0, q_ref, k_ref, v_ref, o_ref, kmean, kaug_a, kaug_b, vaug_a, vaug_b, qaug,
                 m_s, acc_s, s_buf):
    qi = pl.program_id(2) + q0
    nb = k_ref.shape[0] // MOBA_BLOCK
    nbp = kmean.shape[0]
    lane = lax.broadcasted_iota(jnp.int32, (1, LANES), 1)
    head_a = lane < HEAD_DIM

    @pl.when(pl.program_id(2) == 0)
    def _():
        kmean[...] = jnp.zeros_like(kmean)
        for j in range(nb):
            rows = pl.ds(j * MOBA_BLOCK, MOBA_BLOCK)
            kj = k_ref[rows, :].astype(F32)
            vj = v_ref[rows, :].astype(F32)
            kmean[j:j + 1, :] = jnp.sum(kj, axis=0, keepdims=True) * (1.0 / MOBA_BLOCK)
            kaug_a[rows, :] = jnp.where(head_a, kj, jnp.where(lane - HEAD_DIM == j, 1.0, 0.0)).astype(BF16)
            kaug_b[rows, :] = jnp.where(head_a, jnp.where(lane == j, 1.0, 0.0), kj).astype(BF16)
            vaug_a[rows, :] = jnp.where(head_a, vj, 1.0).astype(BF16)
            vaug_b[rows, :] = jnp.where(head_a, 1.0, vj).astype(BF16)
        blk_row = lax.broadcasted_iota(jnp.int32, (nbp, MOBA_BLOCK), 0)
        for t in range(q_ref.shape[0] // MOBA_BLOCK):
            qt = q0 + t
            qf = q_ref[t * MOBA_BLOCK:(t + 1) * MOBA_BLOCK, :].astype(F32)
            for hd, is_a in enumerate((True, False)):
                mine = head_a if is_a else jnp.logical_not(head_a)
                q_own = jnp.where(mine, qf, 0.0)
                g = _dot_nt(kmean[...], q_own, precision=HIGHEST)
                g = jnp.where(blk_row < qt, g, NEG)
                sel = jnp.zeros(g.shape, F32)
                for _ in range(MOBA_TOPK):
                    m = jnp.max(g, axis=0, keepdims=True)
                    idx = jnp.min(jnp.where(g == m, blk_row, nbp), axis=0, keepdims=True)
                    hit = blk_row == idx
                    sel = jnp.where(hit, jnp.where(idx < qt, 1.0, 0.0), sel)
                    g = jnp.where(hit, -jnp.inf, g)
                bias_t = jnp.where(sel > 0.0, 0.0, jnp.where(blk_row == qt, 0.0, NEG))
                bias_t = jnp.concatenate([bias_t, jnp.full((LANES - nbp, MOBA_BLOCK), NEG, F32)], axis=0)
                bias = jnp.transpose(bias_t)
                if is_a:
                    bias = pltpu.roll(bias, HEAD_DIM, 1)
                qaug[hd, t * MOBA_BLOCK:(t + 1) * MOBA_BLOCK, :] = jnp.where(mine, qf, bias).astype(BF16)

    tile_rows = pl.ds(pl.multiple_of(pl.program_id(2) * MOBA_BLOCK, MOBA_BLOCK), MOBA_BLOCK)
    q_augs = [qaug[0, tile_rows, :], qaug[1, tile_rows, :]]

    m_s[...] = jnp.full(m_s.shape, -jnp.inf, F32)
    acc_s[...] = jnp.zeros_like(acc_s)
    qpos = qi * MOBA_BLOCK + lax.broadcasted_iota(jnp.int32, (MOBA_BLOCK, MOBA_PAIR), 0)
    col = lax.broadcasted_iota(jnp.int32, (MOBA_BLOCK, MOBA_PAIR), 1)

    def kv_rows(jj):
        return pl.ds(pl.multiple_of(jj * MOBA_PAIR, MOBA_PAIR), MOBA_PAIR)

    def scores(jj, slot):
        for hd, kaug in enumerate((kaug_a, kaug_b)):
            s_buf[slot, hd] = _dot_nt(q_augs[hd], kaug[kv_rows(jj), :])

    def softmax_pv(jj, slot, causal):
        for hd, vaug in enumerate((vaug_a, vaug_b)):
            s = s_buf[slot, hd]
            if causal:
                s = jnp.where(jj * MOBA_PAIR + col <= qpos, s, NEG)
            m_old = m_s[hd]
            m_new = jnp.maximum(m_old, jnp.max(s, axis=-1, keepdims=True))
            alpha = jnp.exp(m_old - m_new)
            p = jnp.exp(s - m_new)
            m_s[hd] = m_new
            acc_s[hd] = alpha * acc_s[hd] + _dot(p.astype(BF16), vaug[kv_rows(jj), :])

    last = qi // 2
    scores(0, 0)

    def body(k, _):
        scores(2 * k + 1, 1)
        softmax_pv(2 * k, 0, False)
        scores(2 * k + 2, 0)
        softmax_pv(2 * k + 1, 1, False)
        return 0

    lax.fori_loop(0, last // 2, body, 0)

    @pl.when(last % 2 == 0)
    def _():
        softmax_pv(last, 0, True)

    @pl.when(last % 2 == 1)
    def _():
        scores(last, 1)
        softmax_pv(last - 1, 0, False)
        softmax_pv(last, 1, True)
    acc_a, acc_b = acc_s[0], acc_s[1]
    o_ref[...] = jnp.where(head_a, acc_a / pltpu.roll(acc_a, HEAD_DIM, 1),
                           acc_b / pltpu.roll(acc_b, HEAD_DIM, 1)).astype(BF16)


def _moba(q, k, v, q0):
    B = q.shape[0]
    nq = q.shape[1] // MOBA_BLOCK
    skv = (q0 + nq) * MOBA_BLOCK
    nb = skv // MOBA_BLOCK
    assert nb <= HEAD_DIM and nb % 2 == 0 and skv <= k.shape[1]
    nbp = -(-nb // SUBLANES) * SUBLANES
    blk = pl.BlockSpec((None, MOBA_BLOCK, LANES), lambda b, h, i: (b, i, h))
    seq = pl.BlockSpec((None, skv, LANES), lambda b, h, i: (b, 0, h))
    return pl.pallas_call(
        functools.partial(_moba_kernel, q0),
        grid=(B, D_ATT // LANES, nq),
        in_specs=[pl.BlockSpec((None, nq * MOBA_BLOCK, LANES), lambda b, h, i: (b, 0, h)), seq, seq],
        out_specs=blk,
        out_shape=jax.ShapeDtypeStruct(q.shape, BF16),
        scratch_shapes=[pltpu.VMEM((nbp, LANES), F32),
                        pltpu.VMEM((skv, LANES), BF16), pltpu.VMEM((skv, LANES), BF16),
                        pltpu.VMEM((skv, LANES), BF16), pltpu.VMEM((skv, LANES), BF16),
                        pltpu.VMEM((2, nq * MOBA_BLOCK, LANES), BF16),
                        pltpu.VMEM((2, MOBA_BLOCK, 1), F32),
                        pltpu.VMEM((2, MOBA_BLOCK, LANES), F32),
                        pltpu.VMEM((2, 2, MOBA_BLOCK, MOBA_PAIR), F32)],
        compiler_params=pltpu.CompilerParams(
            dimension_semantics=("parallel", "parallel", "arbitrary"), vmem_limit_bytes=VMEM_LIMIT),
        name="moba",
    )(q, k, v)


MERGE_TS = 256


def _bf16_bits(x):
    b = pltpu.bitcast(x, jnp.int32)
    r = b + 0x7FFF + (lax.shift_right_logical(b, 16) & 1)
    return lax.shift_right_logical(r, 16)


def _merge_kernel(x_ref, ys_ref, at_ref, ga_ref, gb_ref, wa_ref, wb_ref, wo_ref, g_ref,
                  wq_ref, k1_ref, k2_ref, x1_ref, hw_ref, sc_ref):
    ya = _dot(ys_ref[...], wa_ref[...])
    yb = _dot(at_ref[...], wb_ref[...])
    merged = ga_ref[...].astype(F32) * ya + gb_ref[...].astype(F32) * yb
    x1 = x_ref[...] + _dot(merged.astype(BF16), wo_ref[...])
    x1_ref[...] = x1
    hq = _rms(x1, g_ref[...])
    half = D_MODEL // 2
    hw_ref[...] = _bf16_bits(hq[:, :half]) | lax.shift_left(_bf16_bits(hq[:, half:]), 16)
    qp = _dot(hq.astype(BF16), wq_ref[...])
    for h in range(PEER_HEADS):
        o = h * PEER_QDIM
        sc_ref[2 * h] = _dot_nt(k1_ref[h], qp[:, o:o + PEER_HALF], precision=HIGHEST)
        sc_ref[2 * h + 1] = _dot_nt(k2_ref[h], qp[:, o + PEER_HALF:o + PEER_QDIM], precision=HIGHEST)


def _merge(x, ys, att, ga, gb, t0, w_proj_ssm, w_proj_att, w_out, g_ffn, peer_w_q, keys1, keys2):
    B, nt = ys.shape[0], ys.shape[1]
    ts = min(MERGE_TS, nt)
    nblk = nt // ts
    i0 = t0 // ts
    tok = lambda d: pl.BlockSpec((None, ts, d), lambda b, i: (b, i, 0))
    row = lambda d: pl.BlockSpec((ts, d), lambda b, i: (b * nblk + i, 0))
    full = lambda shape: pl.BlockSpec(shape, lambda b, i: (0,) * len(shape))
    qd = PEER_HEADS * PEER_QDIM
    return pl.pallas_call(
        _merge_kernel,
        grid=(B, nblk),
        in_specs=[pl.BlockSpec((None, ts, D_MODEL), lambda b, i: (b, i0 + i, 0)),
                  tok(D_SSM), tok(D_ATT), tok(D_MODEL), tok(D_MODEL),
                  full((D_SSM, D_MODEL)), full((D_ATT, D_MODEL)), full((D_MODEL, D_MODEL)),
                  full((1, D_MODEL)), full((D_MODEL, qd)),
                  full((PEER_HEADS, PEER_KEYS, PEER_HALF)), full((PEER_HEADS, PEER_KEYS, PEER_HALF))],
        out_specs=[row(D_MODEL), row(D_MODEL // 2),
                   pl.BlockSpec((2 * PEER_HEADS, PEER_KEYS, ts), lambda b, i: (0, 0, b * nblk + i))],
        out_shape=[jax.ShapeDtypeStruct((B * nt, D_MODEL), F32),
                   jax.ShapeDtypeStruct((B * nt, D_MODEL // 2), jnp.int32),
                   jax.ShapeDtypeStruct((2 * PEER_HEADS, PEER_KEYS, B * nt), F32)],
        compiler_params=pltpu.CompilerParams(
            dimension_semantics=("parallel", "parallel"), vmem_limit_bytes=VMEM_LIMIT),
        name="merge",
    )(x, ys, att, ga, gb, w_proj_ssm, w_proj_att, w_out, g_ffn, peer_w_q, keys1, keys2)


TOPK_TS = 256


def _top_rows(s, row, k):
    vals, idxs = [], []
    for _ in range(k):
        m = jnp.max(s, axis=0, keepdims=True)
        idx = jnp.min(jnp.where(s == m, row, s.shape[0]), axis=0, keepdims=True)
        vals.append(m)
        idxs.append(idx)
        s = jnp.where(row == idx, -jnp.inf, s)
    return vals, idxs


def _stack_rows(rows, row16):
    acc = jnp.zeros(row16.shape, rows[0].dtype)
    for r, v in enumerate(rows):
        acc = jnp.where(row16 == r, v, acc)
    return acc


def _topk_kernel(sc_ref, idx_ref, gate_ref):
    ts = sc_ref.shape[-1]
    row = lax.broadcasted_iota(jnp.int32, (PEER_KEYS, ts), 0).astype(F32)
    row16 = lax.broadcasted_iota(jnp.int32, (PEER_TOPK, ts), 0)
    row8 = lax.broadcasted_iota(jnp.int32, (SUBLANES, ts), 0)
    counts = [PEER_TOPK // (i + 1) for i in range(PEER_TOPK)]
    heights = [PEER_TOPK if c > SUBLANES else SUBLANES for c in counts]
    n_cand = sum(heights)
    rowc = lax.broadcasted_iota(jnp.int32, (n_cand, ts), 0).astype(F32)
    gate_rows, eid_rows = [], []
    for h in range(PEER_HEADS):
        v1, i1 = _top_rows(sc_ref[2 * h], row, PEER_TOPK)
        v2, i2 = _top_rows(sc_ref[2 * h + 1], row, PEER_TOPK)
        v2s = _stack_rows(v2, row16)
        i2s = _stack_rows(i2, row16)
        cand, eid = [], []
        for i in range(PEER_TOPK):
            n = heights[i]
            cand.append(jnp.where((row16 if n == PEER_TOPK else row8) < counts[i],
                                  v1[i] + v2s[:n], -jnp.inf))
            eid.append(i1[i] * PEER_KEYS + i2s[:n])
        cand = jnp.concatenate(cand, axis=0)
        eid = jnp.concatenate(eid, axis=0)
        tops, picks = [], []
        for _ in range(PEER_TOPK):
            m = jnp.max(cand, axis=0, keepdims=True)
            pos = jnp.min(jnp.where(cand == m, rowc, n_cand), axis=0, keepdims=True)
            hit = rowc == pos
            picks.append(jnp.max(jnp.where(hit, eid, -1.0), axis=0, keepdims=True))
            tops.append(m)
            cand = jnp.where(hit, -jnp.inf, cand)
        top = _stack_rows(tops, row16)
        p = jnp.exp(top - jnp.max(top, axis=0, keepdims=True))
        gate_rows.append(p / jnp.sum(p, axis=0, keepdims=True))
        eid_rows.append(_stack_rows(picks, row16))
    gate_ref[...] = jnp.transpose(jnp.concatenate(gate_rows, axis=0))
    idx_ref[...] = jnp.transpose(jnp.concatenate(eid_rows, axis=0)).astype(jnp.int32)


def _topk(scores):
    T = scores.shape[-1]
    ts = min(TOPK_TS, T)
    return pl.pallas_call(
        _topk_kernel,
        grid=(T // ts,),
        in_specs=[pl.BlockSpec((2 * PEER_HEADS, PEER_KEYS, ts), lambda i: (0, 0, i))],
        out_specs=[pl.BlockSpec((ts, PEER_SEL), lambda i: (i, 0)),
                   pl.BlockSpec((ts, PEER_SEL), lambda i: (i, 0))],
        out_shape=[jax.ShapeDtypeStruct((T, PEER_SEL), jnp.int32),
                   jax.ShapeDtypeStruct((T, PEER_SEL), F32)],
        compiler_params=pltpu.CompilerParams(
            dimension_semantics=("parallel",), vmem_limit_bytes=VMEM_LIMIT),
        name="topk",
    )(scores)


SC_CORES = 2
SC_SUBCORES = 16
SC_LANES = 16
SC_WORKERS = SC_CORES * SC_SUBCORES
PEER_CH = SC_LANES
PEER_NCH = PEER_SEL // PEER_CH
PEER_WORDS = D_MODEL // 2
PEER_NWG = PEER_WORDS // SC_LANES
PEER_RING = 4
PEER_QUAD = 4
HI_MASK = -65536
GELU_C = 0.7978845608028654


def _gelu_tanh_via_exp(x):
    z = GELU_C * (x + 0.044715 * (x * x * x))
    t = 1.0 - 2.0 / (jnp.exp(2.0 * z) + 1.0)
    return 0.5 * x * (1.0 + t)


def _unpack_pair(w):
    lo = plsc.bitcast(lax.shift_left(w, 16), F32)
    hi = plsc.bitcast(lax.bitwise_and(w, HI_MASK), F32)
    return lo, hi


def _peer_sc_body(idx_hbm, gate_hbm, h_hbm, uv_hbm, after_hbm, o_hbm,
                  idx_v, gate_v, h_v, buf, out_v, gsem, msem, osem):
    n_tok = o_hbm.shape[0] // SC_WORKERS
    base = (lax.axis_index("s") * SC_CORES + lax.axis_index("c")) * n_tok
    lane = lax.iota(jnp.int32, SC_LANES)
    zero_rows = jnp.zeros((SC_LANES,), jnp.int32)

    def meta_copies(tok, s):
        return (pltpu.make_async_copy(idx_hbm.at[tok], idx_v.at[s], msem.at[s]),
                pltpu.make_async_copy(gate_hbm.at[tok], gate_v.at[s], msem.at[s]),
                pltpu.make_async_copy(h_hbm.at[tok], h_v.at[s], msem.at[s]))

    def gather(slot, rows):
        return pltpu.make_async_copy(uv_hbm.at[rows], buf.at[slot], gsem.at[slot])

    def token(t, carry):
        s = t % 2
        tok = base + t
        nxt = base + jnp.minimum(t + 1, n_tok - 1)
        for cp in meta_copies(nxt, 1 - s):
            cp.start()

        @pl.when(t >= 2)
        def _():
            pltpu.make_async_copy(out_v.at[s], o_hbm.at[tok], osem.at[s]).wait()

        def chunk(c, carry):
            slot = c % PEER_RING
            gather(slot, zero_rows).wait()

            def dot_step(q, accs):
                cols = [pl.ds(pl.multiple_of((q * PEER_QUAD + j) * SC_LANES, SC_LANES), SC_LANES)
                        for j in range(PEER_QUAD)]
                hs = [plsc.bitcast(h_v[s, col], BF16) for col in cols]
                out = []
                for r in range(PEER_CH):
                    p = plsc.bitcast(buf[slot, r, cols[0]], BF16) * hs[0]
                    for j in range(1, PEER_QUAD):
                        p = p + plsc.bitcast(buf[slot, r, cols[j]], BF16) * hs[j]
                    lo, hi = _unpack_pair(plsc.bitcast(p, jnp.int32))
                    out.append(accs[r] + lo + hi)
                return tuple(out)

            accs = lax.fori_loop(0, PEER_NWG // PEER_QUAD, dot_step,
                                 tuple(jnp.zeros((SC_LANES,), F32) for _ in range(PEER_CH)))
            tot = jnp.zeros((SC_LANES,), F32)
            for r in range(PEER_CH):
                tot = jnp.where(lane == r, jnp.sum(accs[r]), tot)
            rows = pl.ds(pl.multiple_of(c * PEER_CH, PEER_CH), PEER_CH)
            wvec = gate_v[s, rows] * _gelu_tanh_via_exp(tot)
            ws = []
            for r in range(PEER_CH):
                w = wvec.at[jnp.full((SC_LANES,), r, jnp.int32)].get(mode="promise_in_bounds")
                ws.append(plsc.pack(w, w, format=plsc.PackFormat.INTERLEAVED,
                                    preferred_element_type=BF16))
            first = c == 0

            @plsc.parallel_loop(0, PEER_NWG, unroll=2)
            def acc_step(g):
                col = pl.ds(pl.multiple_of(g * SC_LANES, SC_LANES), SC_LANES)
                col_v = pl.ds(pl.multiple_of(PEER_WORDS + g * SC_LANES, SC_LANES), SC_LANES)
                o_lo = jnp.where(first, 0.0, out_v[s, col])
                o_hi = jnp.where(first, 0.0, out_v[s, col_v])
                for r0 in range(0, PEER_CH, PEER_QUAD):
                    p = plsc.bitcast(buf[slot, r0, col_v], BF16) * ws[r0]
                    for r in range(r0 + 1, r0 + PEER_QUAD):
                        p = p + plsc.bitcast(buf[slot, r, col_v], BF16) * ws[r]
                    lo, hi = _unpack_pair(plsc.bitcast(p, jnp.int32))
                    o_lo = o_lo + lo
                    o_hi = o_hi + hi
                out_v[s, col] = o_lo
                out_v[s, col_v] = o_hi

            @pl.when(c == PEER_NCH - PEER_RING)
            def _():
                for cp in meta_copies(nxt, 1 - s):
                    cp.wait()

            ahead = c + PEER_RING
            src = jnp.where(ahead < PEER_NCH, s, 1 - s)
            nrows = idx_v[src, pl.ds(pl.multiple_of((ahead % PEER_NCH) * PEER_CH, PEER_CH), PEER_CH)]
            gather(slot, nrows).start()
            return carry

        lax.fori_loop(0, PEER_NCH, chunk, 0)
        pltpu.make_async_copy(out_v.at[s], o_hbm.at[tok], osem.at[s]).start()
        return carry

    for cp in meta_copies(base, 0):
        cp.start()
    for cp in meta_copies(base, 0):
        cp.wait()
    for c in range(PEER_RING):
        gather(c, idx_v[0, pl.ds(c * PEER_CH, PEER_CH)]).start()
    lax.fori_loop(0, n_tok, token, 0)
    for c in range(PEER_RING):
        gather(c, zero_rows).wait()
    for s in range(2):
        pltpu.make_async_copy(out_v.at[s], o_hbm.at[base], osem.at[s]).wait()


def _pack_bf16_pairs(tab):
    b = lax.bitcast_convert_type(tab.astype(BF16), jnp.uint16).astype(jnp.uint32)
    half = tab.shape[1] // 2
    return lax.bitcast_convert_type(b[:, :half] | (b[:, half:] << 16), jnp.int32)


def _peer(idx, h_words, gates, uv_words, after):
    T = h_words.shape[0]
    assert T % (2 * SC_WORKERS) == 0
    mesh = plsc.VectorSubcoreMesh(core_axis_name="c", subcore_axis_name="s",
                                  num_cores=SC_CORES, num_subcores=SC_SUBCORES)
    return pl.kernel(
        _peer_sc_body,
        out_type=jax.ShapeDtypeStruct((T, D_MODEL), F32),
        mesh=mesh,
        scratch_types=[
            pltpu.VMEM((2, PEER_SEL), jnp.int32), pltpu.VMEM((2, PEER_SEL), F32),
            pltpu.VMEM((2, PEER_WORDS), jnp.int32),
            pltpu.VMEM((PEER_RING, PEER_CH, 2 * PEER_WORDS), jnp.int32),
            pltpu.VMEM((2, D_MODEL), F32),
            pltpu.SemaphoreType.DMA((PEER_RING,)),
            pltpu.SemaphoreType.DMA((2,)), pltpu.SemaphoreType.DMA((2,)),
        ],
        compiler_params=pltpu.CompilerParams(needs_layout_passes=False),
        name="peer_sc",
    )(idx, gates, h_words, uv_words, after)


FINAL_TS = 256


def _final_kernel(x1_ref, pe_ref, p_ref, gp_ref, wg_ref, wp_ref, gf_ref, o_ref):
    x2 = x1_ref[...] + pe_ref[...]
    e = _dot(p_ref[...].astype(BF16), wp_ref[...])
    gate = jax.nn.sigmoid(_dot(_rms(x2, gp_ref[...]).astype(BF16), wg_ref[...]))
    o_ref[...] = _rms(x2 + gate * e, gf_ref[...])


def _final(x1, peer_out, p, t0, nt, g_ple, ple_w_gate, ple_w_proj, g_final):
    B = p.shape[0]
    ts = min(FINAL_TS, nt)
    nblk = nt // ts
    i0 = t0 // ts
    row = lambda d: pl.BlockSpec((ts, d), lambda b, i: (b * nblk + i, 0))
    full = lambda shape: pl.BlockSpec(shape, lambda b, i: (0,) * len(shape))
    return pl.pallas_call(
        _final_kernel,
        grid=(B, nblk),
        in_specs=[row(D_MODEL), row(D_MODEL),
                  pl.BlockSpec((None, ts, D_PLE), lambda b, i: (b, i0 + i, 0)),
                  full((1, D_MODEL)), full((D_MODEL, D_MODEL)), full((D_PLE, D_MODEL)),
                  full((1, D_MODEL))],
        out_specs=pl.BlockSpec((None, ts, D_MODEL), lambda b, i: (b, i, 0)),
        out_shape=jax.ShapeDtypeStruct((B, nt, D_MODEL), F32),
        compiler_params=pltpu.CompilerParams(
            dimension_semantics=("parallel", "parallel"), vmem_limit_bytes=VMEM_LIMIT),
        name="final",
    )(x1, peer_out, p, g_ple, ple_w_gate, ple_w_proj, g_final)


CHUNK_STEPS = (512, 512, 1024, 1024, 1024, 1024, 1024, 1024, 512, 512)


def kernel(x, p, positions, g_mix, w_in, ssm_log_dt, ssm_a_re, ssm_a_im, ssm_b_re, ssm_b_im,
           ssm_c_re, ssm_c_im, ssm_d, ssm_w_glu, w_proj_ssm, w_proj_att, w_out, g_ffn,
           peer_w_q, peer_keys1, peer_keys2, peer_u, peer_v, g_ple, ple_w_gate, ple_w_proj,
           g_final):
    B, S, _ = x.shape
    assert w_in.shape[0] == 1, "the final rmsnorm is fused into the single layer's last stage"
    steps = CHUNK_STEPS if sum(CHUNK_STEPS) == S else (S,)
    i = 0
    tables = _s5_tables(ssm_log_dt[i], ssm_a_re[i], ssm_a_im[i], ssm_b_re[i], ssm_b_im[i],
                        ssm_c_re[i], ssm_c_im[i])
    w_in_b, w_glu_b = w_in[i].astype(BF16), ssm_w_glu[i].astype(BF16)
    d_skip = ssm_d[i].reshape(1, D_SSM).astype(F32)
    merge_w = (w_proj_ssm[i].astype(BF16), w_proj_att[i].astype(BF16), w_out[i].astype(BF16),
               g_ffn[i].reshape(1, D_MODEL), peer_w_q[i].astype(BF16), peer_keys1[i], peer_keys2[i])
    final_w = (g_ple[i].reshape(1, D_MODEL), ple_w_gate[i].astype(BF16),
               ple_w_proj[i].astype(BF16), g_final.reshape(1, D_MODEL))
    uv_words = jnp.concatenate([_pack_bf16_pairs(peer_u[i]), _pack_bf16_pairs(peer_v[i])], axis=1)
    k_all = jnp.zeros((B, S, D_ATT), BF16)
    v_all = jnp.zeros((B, S, D_ATT), BF16)
    carry = jnp.zeros((2, SUBLANES, D_STATE), F32)
    outs = []
    t0 = 0
    after = (carry, carry)
    peer_prev = carry
    for nt in steps:
        u_sb, q, k, v, ga, gb = _in_proj(x, positions, g_mix[i], w_in_b, t0, nt, after)
        k_all = lax.dynamic_update_slice(k_all, k, (0, t0, 0))
        v_all = lax.dynamic_update_slice(v_all, v, (0, t0, 0))
        ys, carry = _s5(u_sb, carry, tables, d_skip, w_glu_b, B)
        att = _moba(q, k_all, v_all, t0 // MOBA_BLOCK)
        x1, h_words, scores = _merge(x, ys, att, ga, gb, t0, *merge_w)
        idx, gates = _topk(scores)
        after = (gates, outs[-3] if len(outs) > 2 else carry)
        peer_out = _peer(idx, h_words, gates, uv_words, peer_prev)
        peer_prev = peer_out
        outs.append(_final(x1, peer_out, p[i], t0, nt, *final_w))
        t0 += nt
    return jnp.concatenate(outs, axis=1)
```

```python
import functools
import math

import jax
import jax.numpy as jnp
from jax import lax
from jax.experimental import pallas as pl
from jax.experimental.pallas import tpu as pltpu
from jax.experimental.pallas import tpu_sc as plsc

F32 = jnp.float32
BF16 = jnp.bfloat16

D_MODEL = 1024
D_SSM = 512
SSM_GROUP = 16
SSM_GROUPS = 32
SSM_STATE = 64
D_STATE = SSM_GROUPS * SSM_STATE
N_HEADS = 8
HEAD_DIM = 64
D_ATT = 512
ROT_DIM = 16
ROPE_THETA = 500000.0
MOBA_BLOCK = 256
MOBA_TOPK = 3
PEER_HEADS = 8
PEER_KEYS = 128
PEER_QDIM = 256
PEER_HALF = 128
PEER_TOPK = 16
PEER_SEL = PEER_HEADS * PEER_TOPK
D_PLE = 256
EPS = 1e-6
NEG = -1e30
LANES = 128
SUBLANES = 8
VMEM_LIMIT = 48 * 1024 * 1024
HIGHEST = lax.Precision.HIGHEST


def _rms(x, g):
    return x * lax.rsqrt(jnp.mean(x * x, axis=-1, keepdims=True) + EPS) * g


def _dot(a, b):
    return jnp.dot(a, b, preferred_element_type=F32)


def _dot_nt(a, b, precision=None):
    return lax.dot_general(a, b, (((1,), (1,)), ((), ())), precision=precision,
                           preferred_element_type=F32)


IN_TS = 512


def _in_proj_kernel(x_ref, pos_ref, g_ref, w_ref, invf_ref, after_a, after_b,
                    u_ref, q_ref, k_ref, v_ref, ga_ref, gb_ref):
    del after_a, after_b
    h = _rms(x_ref[...], g_ref[...]).astype(BF16)

    def proj(lo, hi):
        return _dot(h, w_ref[:, lo:hi])

    u_ref[...] = proj(0, D_SSM).astype(BF16)
    ang = pos_ref[...].astype(F32) * invf_ref[...]
    cos = jnp.cos(ang)
    sin = jnp.sin(ang)
    lane = lax.broadcasted_iota(jnp.int32, (1, LANES), 1) % HEAD_DIM
    half = ROT_DIM // 2
    sin_hi = jnp.where((lane >= half) & (lane < ROT_DIM), sin, 0.0)
    sin_lo = jnp.where(lane < half, -sin, 0.0)
    reps = D_ATT // LANES
    cos4 = jnp.concatenate([cos] * reps, axis=1)
    sin_hi4 = jnp.concatenate([sin_hi] * reps, axis=1)
    sin_lo4 = jnp.concatenate([sin_lo] * reps, axis=1)

    def rope(t):
        return (t * cos4 + pltpu.roll(t, half, 1) * sin_hi4
                + pltpu.roll(t, D_ATT - half, 1) * sin_lo4)

    q = rope(proj(D_SSM, D_SSM + D_ATT))
    q_ref[...] = (q * (HEAD_DIM ** -0.5)).astype(BF16)
    k_ref[...] = rope(proj(D_SSM + D_ATT, D_SSM + 2 * D_ATT)).astype(BF16)
    v_ref[...] = proj(D_SSM + 2 * D_ATT, D_SSM + 3 * D_ATT).astype(BF16)
    o = D_SSM + 3 * D_ATT
    ga_ref[...] = jax.nn.sigmoid(proj(o, o + D_MODEL)).astype(BF16)
    gb_ref[...] = jax.nn.sigmoid(proj(o + D_MODEL, o + 2 * D_MODEL)).astype(BF16)


def _in_proj(x, positions, g_mix, w_in, t0, nt, after):
    B, S, _ = x.shape
    ts = min(IN_TS, nt)
    assert nt % ts == 0 and t0 % ts == 0
    i0 = t0 // ts
    inv_freq = ROPE_THETA ** (-jnp.arange(0, ROT_DIM, 2, dtype=F32) / ROT_DIM)
    lane = jnp.arange(LANES) % HEAD_DIM
    invf = jnp.where(lane < ROT_DIM, inv_freq[lane % (ROT_DIM // 2)], 0.0).reshape(1, LANES)
    d_in = w_in.shape[1]
    src = lambda d: pl.BlockSpec((None, ts, d), lambda b, i: (b, i0 + i, 0))
    tok = lambda d: pl.BlockSpec((None, ts, d), lambda b, i: (b, i, 0))
    full = lambda shape: pl.BlockSpec(shape, lambda b, i: (0,) * len(shape))
    return pl.pallas_call(
        _in_proj_kernel,
        grid=(B, nt // ts),
        in_specs=[src(D_MODEL), src(1), full((1, D_MODEL)), full((D_MODEL, d_in)), full((1, LANES)),
                  pl.BlockSpec(memory_space=pl.ANY), pl.BlockSpec(memory_space=pl.ANY)],
        out_specs=[pl.BlockSpec((ts, D_SSM), lambda b, i: (i, b)),
                   tok(D_ATT), tok(D_ATT), tok(D_ATT), tok(D_MODEL), tok(D_MODEL)],
        out_shape=[jax.ShapeDtypeStruct((nt, B * D_SSM), BF16),
                   jax.ShapeDtypeStruct((B, nt, D_ATT), BF16),
                   jax.ShapeDtypeStruct((B, nt, D_ATT), BF16),
                   jax.ShapeDtypeStruct((B, nt, D_ATT), BF16),
                   jax.ShapeDtypeStruct((B, nt, D_MODEL), BF16),
                   jax.ShapeDtypeStruct((B, nt, D_MODEL), BF16)],
        compiler_params=pltpu.CompilerParams(
            dimension_semantics=("parallel", "parallel"), vmem_limit_bytes=VMEM_LIMIT),
        name="in_proj",
    )(x, positions.reshape(B, S, 1), g_mix.reshape(1, D_MODEL), w_in, invf, *after)


S5_TS = 128
S5_BATCH = 4
S5_COLS = 512


def _s5_kernel(u_ref, c0_ref, bre_ref, bim_ref, a1r_ref, a1i_ref, pr_ref, pi_ref,
               cre_ref, cim_ref, d_ref, wglu_ref, y_ref, c1_ref,
               xr, xi, cr, ci, ysc):
    rows = xr.shape[0]
    ts = rows // S5_BATCH

    @pl.when(pl.program_id(0) == 0)
    def _():
        cr[...] = c0_ref[0]
        ci[...] = c0_ref[1]

    u = u_ref[...]
    xr[...] = _dot(u, bre_ref[...])
    xi[...] = _dot(u, bim_ref[...])

    hi_rows = lax.broadcasted_iota(jnp.int32, (SUBLANES, S5_COLS), 0) >= S5_BATCH
    for cb in range(D_STATE // S5_COLS):
        sl = slice(cb * S5_COLS, (cb + 1) * S5_COLS)
        a_r, a_i = a1r_ref[:, sl], a1i_ref[:, sl]
        p_r, p_i = pr_ref[:, sl], pi_ref[:, sl]

        def body(t, carry):
            c_r, c_i = carry
            r0 = pl.multiple_of(t * SUBLANES, SUBLANES)
            x_r = xr[pl.ds(r0, SUBLANES), sl]
            x_i = xi[pl.ds(r0, SUBLANES), sl]
            s_r = pltpu.roll(x_r, S5_BATCH, 0)
            s_i = pltpu.roll(x_i, S5_BATCH, 0)
            h_r = x_r + (a_r * s_r - a_i * s_i) + (p_r * c_r - p_i * c_i)
            h_i = x_i + (a_r * s_i + a_i * s_r) + (p_r * c_i + p_i * c_r)
            xr[pl.ds(r0, SUBLANES), sl] = h_r
            xi[pl.ds(r0, SUBLANES), sl] = h_i
            n_r = jnp.where(hi_rows, h_r, pltpu.roll(h_r, S5_BATCH, 0))
            n_i = jnp.where(hi_rows, h_i, pltpu.roll(h_i, S5_BATCH, 0))
            return n_r, n_i

        c_r, c_i = lax.fori_loop(0, rows // SUBLANES, body, (cr[:, sl], ci[:, sl]), unroll=2)
        cr[:, sl] = c_r
        ci[:, sl] = c_i

    y = (_dot(xr[...].astype(BF16), cre_ref[...]) - _dot(xi[...].astype(BF16), cim_ref[...])
         + d_ref[...] * u.astype(F32))
    y = jax.nn.gelu(y)
    y = y * jax.nn.sigmoid(_dot(y.astype(BF16), wglu_ref[...]))
    for c in range(D_SSM // LANES):
        ysc[c] = y[:, c * LANES:(c + 1) * LANES]
    for b in range(S5_BATCH):
        for c in range(D_SSM // LANES):
            y_ref[b, :, c * LANES:(c + 1) * LANES] = (
                ysc[c, pl.ds(b, ts, stride=S5_BATCH), :].astype(BF16))

    @pl.when(pl.program_id(0) == pl.num_programs(0) - 1)
    def _():
        c1_ref[0] = cr[...]
        c1_ref[1] = ci[...]


def _s5_tables(log_dt, a_re, a_im, b_re, b_im, c_re, c_im):
    dt = jnp.exp(log_dt.astype(F32))[:, None]
    ar, ai = a_re.astype(F32), a_im.astype(F32)
    mag = jnp.exp(dt * ar)
    abar_re, abar_im = mag * jnp.cos(dt * ai), mag * jnp.sin(dt * ai)
    den = ar * ar + ai * ai
    nr, ni = abar_re - 1.0, abar_im
    f_re = (nr * ar + ni * ai) / den
    f_im = (ni * ar - nr * ai) / den
    br, bi = b_re.astype(F32), b_im.astype(F32)
    bb_re = f_re[..., None] * br - f_im[..., None] * bi
    bb_im = f_re[..., None] * bi + f_im[..., None] * br
    eye = jnp.eye(SSM_GROUPS, dtype=F32)

    def in_blockdiag(bb):
        return jnp.einsum('gnc,gh->gchn', bb, eye).reshape(D_SSM, D_STATE)

    def out_blockdiag(c):
        return jnp.einsum('gcn,gh->gnhc', c.astype(F32), eye).reshape(D_STATE, D_SSM)

    a_r = abar_re.reshape(1, D_STATE)
    a_i = abar_im.reshape(1, D_STATE)
    a2_r = a_r * a_r - a_i * a_i
    a2_i = 2.0 * a_r * a_i
    hi = (jnp.arange(SUBLANES) >= S5_BATCH)[:, None]
    a1r = jnp.where(hi, a_r, 0.0)
    a1i = jnp.where(hi, a_i, 0.0)
    p_r = jnp.where(hi, a2_r, a_r)
    p_i = jnp.where(hi, a2_i, a_i)
    return (in_blockdiag(bb_re).astype(BF16), in_blockdiag(bb_im).astype(BF16),
            a1r, a1i, p_r, p_i,
            out_blockdiag(c_re).astype(BF16), out_blockdiag(c_im).astype(BF16))


def _s5(u_sb, carry, tables, d_skip, w_glu, B):
    assert B == S5_BATCH
    nt = u_sb.shape[0]
    ts = min(S5_TS, nt)
    rows = ts * B
    bre, bim, a1r, a1i, p_r, p_i, cre, cim = tables
    full = lambda shape: pl.BlockSpec(shape, lambda i: (0,) * len(shape))
    return pl.pallas_call(
        _s5_kernel,
        grid=(nt // ts,),
        in_specs=[pl.BlockSpec((rows, D_SSM), lambda i: (i, 0)),
                  full((2, SUBLANES, D_STATE)),
                  full((D_SSM, D_STATE)), full((D_SSM, D_STATE)),
                  full((SUBLANES, D_STATE)), full((SUBLANES, D_STATE)),
                  full((SUBLANES, D_STATE)), full((SUBLANES, D_STATE)),
                  full((D_STATE, D_SSM)), full((D_STATE, D_SSM)),
                  full((1, D_SSM)), full((D_SSM, D_SSM))],
        out_specs=[pl.BlockSpec((B, ts, D_SSM), lambda i: (0, i, 0)),
                   full((2, SUBLANES, D_STATE))],
        out_shape=[jax.ShapeDtypeStruct((B, nt, D_SSM), BF16),
                   jax.ShapeDtypeStruct((2, SUBLANES, D_STATE), F32)],
        scratch_shapes=[pltpu.VMEM((rows, D_STATE), F32), pltpu.VMEM((rows, D_STATE), F32),
                        pltpu.VMEM((SUBLANES, D_STATE), F32), pltpu.VMEM((SUBLANES, D_STATE), F32),
                        pltpu.VMEM((D_SSM // LANES, rows, LANES), F32)],
        compiler_params=pltpu.CompilerParams(
            dimension_semantics=("arbitrary",), vmem_limit_bytes=VMEM_LIMIT),
        name="s5",
    )(u_sb.reshape(nt * B, D_SSM), carry, bre, bim, a1r, a1i, p_r, p_i, cre, cim, d_skip, w_glu)


MOBA_PAIR = 2 * MOBA_BLOCK


def _moba_kernel(q0, q_ref, k_ref, v_ref, o_ref, kmean, kaug_a, kaug_b, vaug_a, vaug_b, qaug,
                 m_s, acc_s, s_buf):
    qi = pl.program_id(2) + q0
    nb = k_ref.shape[0] // MOBA_BLOCK
    nbp = kmean.shape[0]
    lane = lax.broadcasted_iota(jnp.int32, (1, LANES), 1)
    head_a = lane < HEAD_DIM

    @pl.when(pl.program_id(2) == 0)
    def _():
        kmean[...] = jnp.zeros_like(kmean)
        for j in range(nb):
            rows = pl.ds(j * MOBA_BLOCK, MOBA_BLOCK)
            kj = k_ref[rows, :].astype(F32)
            vj = v_ref[rows, :].astype(F32)
            kmean[j:j + 1, :] = jnp.sum(kj, axis=0, keepdims=True) * (1.0 / MOBA_BLOCK)
            kaug_a[rows, :] = jnp.where(head_a, kj, jnp.where(lane - HEAD_DIM == j, 1.0, 0.0)).astype(BF16)
            kaug_b[rows, :] = jnp.where(head_a, jnp.where(lane == j, 1.0, 0.0), kj).astype(BF16)
            vaug_a[rows, :] = jnp.where(head_a, vj, 1.0).astype(BF16)
            vaug_b[rows, :] = jnp.where(head_a, 1.0, vj).astype(BF16)
        blk_row = lax.broadcasted_iota(jnp.int32, (nbp, MOBA_BLOCK), 0)
        for t in range(q_ref.shape[0] // MOBA_BLOCK):
            qt = q0 + t
            qf = q_ref[t * MOBA_BLOCK:(t + 1) * MOBA_BLOCK, :].astype(F32)
            for hd, is_a in enumerate((True, False)):
                mine = head_a if is_a else jnp.logical_not(head_a)
                q_own = jnp.where(mine, qf, 0.0)
                g = _dot_nt(kmean[...], q_own, precision=HIGHEST)
                g = jnp.where(blk_row < qt, g, NEG)
                sel = jnp.zeros(g.shape, F32)
                for _ in range(MOBA_TOPK):
                    m = jnp.max(g, axis=0, keepdims=True)
                    idx = jnp.min(jnp.where(g == m, blk_row, nbp), axis=0, keepdims=True)
                    hit = blk_row == idx
                    sel = jnp.where(hit, jnp.where(idx < qt, 1.0, 0.0), sel)
                    g = jnp.where(hit, -jnp.inf, g)
                bias_t = jnp.where(sel > 0.0, 0.0, jnp.where(blk_row == qt, 0.0, NEG))
                bias_t = jnp.concatenate([bias_t, jnp.full((LANES - nbp, MOBA_BLOCK), NEG, F32)], axis=0)
                bias = jnp.transpose(bias_t)
                if is_a:
                    bias = pltpu.roll(bias, HEAD_DIM, 1)
                qaug[hd, t * MOBA_BLOCK:(t + 1) * MOBA_BLOCK, :] = jnp.where(mine, qf, bias).astype(BF16)

    tile_rows = pl.ds(pl.multiple_of(pl.program_id(2) * MOBA_BLOCK, MOBA_BLOCK), MOBA_BLOCK)
    q_augs = [qaug[0, tile_rows, :], qaug[1, tile_rows, :]]

    m_s[...] = jnp.full(m_s.shape, -jnp.inf, F32)
    acc_s[...] = jnp.zeros_like(acc_s)
    qpos = qi * MOBA_BLOCK + lax.broadcasted_iota(jnp.int32, (MOBA_BLOCK, MOBA_PAIR), 0)
    col = lax.broadcasted_iota(jnp.int32, (MOBA_BLOCK, MOBA_PAIR), 1)

    def kv_rows(jj):
        return pl.ds(pl.multiple_of(jj * MOBA_PAIR, MOBA_PAIR), MOBA_PAIR)

    def scores(jj, slot):
        for hd, kaug in enumerate((kaug_a, kaug_b)):
            s_buf[slot, hd] = _dot_nt(q_augs[hd], kaug[kv_rows(jj), :])

    def softmax_pv(jj, slot, causal):
        for hd, vaug in enumerate((vaug_a, vaug_b)):
            s = s_buf[slot, hd]
            if causal:
                s = jnp.where(jj * MOBA_PAIR + col <= qpos, s, NEG)
            m_old = m_s[hd]
            m_new = jnp.maximum(m_old, jnp.max(s, axis=-1, keepdims=True))
            alpha = jnp.exp(m_old - m_new)
            p = jnp.exp(s - m_new)
            m_s[hd] = m_new
            acc_s[hd] = alpha * acc_s[hd] + _dot(p.astype(BF16), vaug[kv_rows(jj), :])

    last = qi // 2
    scores(0, 0)

    def body(k, _):
        scores(2 * k + 1, 1)
        softmax_pv(2 * k, 0, False)
        scores(2 * k + 2, 0)
        softmax_pv(2 * k + 1, 1, False)
        return 0

    lax.fori_loop(0, last // 2, body, 0)

    @pl.when(last % 2 == 0)
    def _():
        softmax_pv(last, 0, True)

    @pl.when(last % 2 == 1)
    def _():
        scores(last, 1)
        softmax_pv(last - 1, 0, False)
        softmax_pv(last, 1, True)
    acc_a, acc_b = acc_s[0], acc_s[1]
    o_ref[...] = jnp.where(head_a, acc_a / pltpu.roll(acc_a, HEAD_DIM, 1),
                           acc_b / pltpu.roll(acc_b, HEAD_DIM, 1)).astype(BF16)


def _moba(q, k, v, q0):
    B = q.shape[0]
    nq = q.shape[1] // MOBA_BLOCK
    skv = (q0 + nq) * MOBA_BLOCK
    nb = skv // MOBA_BLOCK
    assert nb <= HEAD_DIM and nb % 2 == 0 and skv <= k.shape[1]
    nbp = -(-nb // SUBLANES) * SUBLANES
    blk = pl.BlockSpec((None, MOBA_BLOCK, LANES), lambda b, h, i: (b, i, h))
    seq = pl.BlockSpec((None, skv, LANES), lambda b, h, i: (b, 0, h))
    return pl.pallas_call(
        functools.partial(_moba_kernel, q0),
        grid=(B, D_ATT // LANES, nq),
        in_specs=[pl.BlockSpec((None, nq * MOBA_BLOCK, LANES), lambda b, h, i: (b, 0, h)), seq, seq],
        out_specs=blk,
        out_shape=jax.ShapeDtypeStruct(q.shape, BF16),
        scratch_shapes=[pltpu.VMEM((nbp, LANES), F32),
                        pltpu.VMEM((skv, LANES), BF16), pltpu.VMEM((skv, LANES), BF16),
                        pltpu.VMEM((skv, LANES), BF16), pltpu.VMEM((skv, LANES), BF16),
                        pltpu.VMEM((2, nq * MOBA_BLOCK, LANES), BF16),
                        pltpu.VMEM((2, MOBA_BLOCK, 1), F32),
                        pltpu.VMEM((2, MOBA_BLOCK, LANES), F32),
                        pltpu.VMEM((2, 2, MOBA_BLOCK, MOBA_PAIR), F32)],
        compiler_params=pltpu.CompilerParams(
            dimension_semantics=("parallel", "parallel", "arbitrary"), vmem_limit_bytes=VMEM_LIMIT),
        name="moba",
    )(q, k, v)


MERGE_TS = 256


def _bf16_bits(x):
    b = pltpu.bitcast(x, jnp.int32)
    r = b + 0x7FFF + (lax.shift_right_logical(b, 16) & 1)
    return lax.shift_right_logical(r, 16)


def _merge_kernel(x_ref, ys_ref, at_ref, ga_ref, gb_ref, wa_ref, wb_ref, wo_ref, g_ref,
                  wq_ref, k1_ref, k2_ref, x1_ref, hw_ref, idx_ref, gate_ref, sc_ref):
    ya = _dot(ys_ref[...], wa_ref[...])
    yb = _dot(at_ref[...], wb_ref[...])
    merged = ga_ref[...].astype(F32) * ya + gb_ref[...].astype(F32) * yb
    x1 = x_ref[...] + _dot(merged.astype(BF16), wo_ref[...])
    x1_ref[...] = x1
    hq = _rms(x1, g_ref[...])
    half = D_MODEL // 2
    hw_ref[...] = _bf16_bits(hq[:, :half]) | lax.shift_left(_bf16_bits(hq[:, half:]), 16)
    qp = _dot(hq.astype(BF16), wq_ref[...])
    for h in range(PEER_HEADS):
        o = h * PEER_QDIM
        sc_ref[2 * h] = _dot_nt(k1_ref[h], qp[:, o:o + PEER_HALF], precision=HIGHEST)
        sc_ref[2 * h + 1] = _dot_nt(k2_ref[h], qp[:, o + PEER_HALF:o + PEER_QDIM], precision=HIGHEST)
    _topk_kernel(sc_ref, idx_ref, gate_ref)


def _merge(x, ys, att, ga, gb, t0, w_proj_ssm, w_proj_att, w_out, g_ffn, peer_w_q, keys1, keys2):
    B, nt = ys.shape[0], ys.shape[1]
    ts = min(MERGE_TS, nt)
    nblk = nt // ts
    i0 = t0 // ts
    tok = lambda d: pl.BlockSpec((None, ts, d), lambda b, i: (b, i, 0))
    row = lambda d: pl.BlockSpec((ts, d), lambda b, i: (b * nblk + i, 0))
    full = lambda shape: pl.BlockSpec(shape, lambda b, i: (0,) * len(shape))
    qd = PEER_HEADS * PEER_QDIM
    return pl.pallas_call(
        _merge_kernel,
        grid=(B, nblk),
        in_specs=[pl.BlockSpec((None, ts, D_MODEL), lambda b, i: (b, i0 + i, 0)),
                  tok(D_SSM), tok(D_ATT), tok(D_MODEL), tok(D_MODEL),
                  full((D_SSM, D_MODEL)), full((D_ATT, D_MODEL)), full((D_MODEL, D_MODEL)),
                  full((1, D_MODEL)), full((D_MODEL, qd)),
                  full((PEER_HEADS, PEER_KEYS, PEER_HALF)), full((PEER_HEADS, PEER_KEYS, PEER_HALF))],
        out_specs=[row(D_MODEL), row(D_MODEL // 2), row(PEER_SEL), row(PEER_SEL)],
        out_shape=[jax.ShapeDtypeStruct((B * nt, D_MODEL), F32),
                   jax.ShapeDtypeStruct((B * nt, D_MODEL // 2), jnp.int32),
                   jax.ShapeDtypeStruct((B * nt, PEER_SEL), jnp.int32),
                   jax.ShapeDtypeStruct((B * nt, PEER_SEL), F32)],
        scratch_shapes=[pltpu.VMEM((2 * PEER_HEADS, PEER_KEYS, ts), F32)],
        compiler_params=pltpu.CompilerParams(
            dimension_semantics=("parallel", "parallel"), vmem_limit_bytes=VMEM_LIMIT),
        name="merge",
    )(x, ys, att, ga, gb, w_proj_ssm, w_proj_att, w_out, g_ffn, peer_w_q, keys1, keys2)


def _top_rows(s, row, k):
    vals, idxs = [], []
    for _ in range(k):
        m = jnp.max(s, axis=0, keepdims=True)
        idx = jnp.min(jnp.where(s == m, row, s.shape[0]), axis=0, keepdims=True)
        vals.append(m)
        idxs.append(idx)
        s = jnp.where(row == idx, -jnp.inf, s)
    return vals, idxs


def _stack_rows(rows, row16):
    acc = jnp.zeros(row16.shape, rows[0].dtype)
    for r, v in enumerate(rows):
        acc = jnp.where(row16 == r, v, acc)
    return acc


def _topk_kernel(sc_ref, idx_ref, gate_ref):
    ts = sc_ref.shape[-1]
    row = lax.broadcasted_iota(jnp.int32, (PEER_KEYS, ts), 0).astype(F32)
    row16 = lax.broadcasted_iota(jnp.int32, (PEER_TOPK, ts), 0)
    row8 = lax.broadcasted_iota(jnp.int32, (SUBLANES, ts), 0)
    counts = [PEER_TOPK // (i + 1) for i in range(PEER_TOPK)]
    heights = [PEER_TOPK if c > SUBLANES else SUBLANES for c in counts]
    n_cand = sum(heights)
    rowc = lax.broadcasted_iota(jnp.int32, (n_cand, ts), 0).astype(F32)
    gate_rows, eid_rows = [], []
    for h in range(PEER_HEADS):
        v1, i1 = _top_rows(sc_ref[2 * h], row, PEER_TOPK)
        v2, i2 = _top_rows(sc_ref[2 * h + 1], row, PEER_TOPK)
        v2s = _stack_rows(v2, row16)
        i2s = _stack_rows(i2, row16)
        cand, eid = [], []
        for i in range(PEER_TOPK):
            n = heights[i]
            cand.append(jnp.where((row16 if n == PEER_TOPK else row8) < counts[i],
                                  v1[i] + v2s[:n], -jnp.inf))
            eid.append(i1[i] * PEER_KEYS + i2s[:n])
        cand = jnp.concatenate(cand, axis=0)
        eid = jnp.concatenate(eid, axis=0)
        tops, picks = [], []
        for _ in range(PEER_TOPK):
            m = jnp.max(cand, axis=0, keepdims=True)
            pos = jnp.min(jnp.where(cand == m, rowc, n_cand), axis=0, keepdims=True)
            hit = rowc == pos
            picks.append(jnp.max(jnp.where(hit, eid, -1.0), axis=0, keepdims=True))
            tops.append(m)
            cand = jnp.where(hit, -jnp.inf, cand)
        top = _stack_rows(tops, row16)
        p = jnp.exp(top - jnp.max(top, axis=0, keepdims=True))
        gate_rows.append(p / jnp.sum(p, axis=0, keepdims=True))
        eid_rows.append(_stack_rows(picks, row16))
    gate_ref[...] = jnp.transpose(jnp.concatenate(gate_rows, axis=0))
    idx_ref[...] = jnp.transpose(jnp.concatenate(eid_rows, axis=0)).astype(jnp.int32)


SC_CORES = 2
SC_SUBCORES = 16
SC_LANES = 16
SC_WORKERS = SC_CORES * SC_SUBCORES
PEER_CH = SC_LANES
PEER_NCH = PEER_SEL // PEER_CH
PEER_WORDS = D_MODEL // 2
PEER_NWG = PEER_WORDS // SC_LANES
PEER_RING = 4
PEER_QUAD = 4
HI_MASK = -65536
GELU_C = 0.7978845608028654


def _gelu_tanh_via_exp(x):
    z = GELU_C * (x + 0.044715 * (x * x * x))
    t = 1.0 - 2.0 / (jnp.exp(2.0 * z) + 1.0)
    return 0.5 * x * (1.0 + t)


def _unpack_pair(w):
    lo = plsc.bitcast(lax.shift_left(w, 16), F32)
    hi = plsc.bitcast(lax.bitwise_and(w, HI_MASK), F32)
    return lo, hi


def _peer_sc_body(idx_hbm, gate_hbm, h_hbm, uv_hbm, after_hbm, o_hbm,
                  idx_v, gate_v, h_v, buf, out_v, gsem, msem, osem):
    n_tok = o_hbm.shape[0] // SC_WORKERS
    base = (lax.axis_index("s") * SC_CORES + lax.axis_index("c")) * n_tok
    lane = lax.iota(jnp.int32, SC_LANES)
    zero_rows = jnp.zeros((SC_LANES,), jnp.int32)

    def meta_copies(tok, s):
        return (pltpu.make_async_copy(idx_hbm.at[tok], idx_v.at[s], msem.at[s]),
                pltpu.make_async_copy(gate_hbm.at[tok], gate_v.at[s], msem.at[s]),
                pltpu.make_async_copy(h_hbm.at[tok], h_v.at[s], msem.at[s]))

    def gather(slot, rows):
        return pltpu.make_async_copy(uv_hbm.at[rows], buf.at[slot], gsem.at[slot])

    def token(t, carry):
        s = t % 2
        tok = base + t
        nxt = base + jnp.minimum(t + 1, n_tok - 1)
        for cp in meta_copies(nxt, 1 - s):
            cp.start()

        @pl.when(t >= 2)
        def _():
            pltpu.make_async_copy(out_v.at[s], o_hbm.at[tok], osem.at[s]).wait()

        def chunk(c, carry):
            slot = c % PEER_RING
            gather(slot, zero_rows).wait()

            def dot_step(q, accs):
                cols = [pl.ds(pl.multiple_of((q * PEER_QUAD + j) * SC_LANES, SC_LANES), SC_LANES)
                        for j in range(PEER_QUAD)]
                hs = [plsc.bitcast(h_v[s, col], BF16) for col in cols]
                out = []
                for r in range(PEER_CH):
                    p = plsc.bitcast(buf[slot, r, cols[0]], BF16) * hs[0]
                    for j in range(1, PEER_QUAD):
                        p = p + plsc.bitcast(buf[slot, r, cols[j]], BF16) * hs[j]
                    lo, hi = _unpack_pair(plsc.bitcast(p, jnp.int32))
                    out.append(accs[r] + lo + hi)
                return tuple(out)

            accs = lax.fori_loop(0, PEER_NWG // PEER_QUAD, dot_step,
                                 tuple(jnp.zeros((SC_LANES,), F32) for _ in range(PEER_CH)))
            tot = jnp.zeros((SC_LANES,), F32)
            for r in range(PEER_CH):
                tot = jnp.where(lane == r, jnp.sum(accs[r]), tot)
            rows = pl.ds(pl.multiple_of(c * PEER_CH, PEER_CH), PEER_CH)
            wvec = gate_v[s, rows] * _gelu_tanh_via_exp(tot)
            ws = []
            for r in range(PEER_CH):
                w = wvec.at[jnp.full((SC_LANES,), r, jnp.int32)].get(mode="promise_in_bounds")
                ws.append(plsc.pack(w, w, format=plsc.PackFormat.INTERLEAVED,
                                    preferred_element_type=BF16))
            first = c == 0

            @plsc.parallel_loop(0, PEER_NWG, unroll=2)
            def acc_step(g):
                col = pl.ds(pl.multiple_of(g * SC_LANES, SC_LANES), SC_LANES)
                col_v = pl.ds(pl.multiple_of(PEER_WORDS + g * SC_LANES, SC_LANES), SC_LANES)
                o_lo = jnp.where(first, 0.0, out_v[s, col])
                o_hi = jnp.where(first, 0.0, out_v[s, col_v])
                for r0 in range(0, PEER_CH, PEER_QUAD):
                    p = plsc.bitcast(buf[slot, r0, col_v], BF16) * ws[r0]
                    for r in range(r0 + 1, r0 + PEER_QUAD):
                        p = p + plsc.bitcast(buf[slot, r, col_v], BF16) * ws[r]
                    lo, hi = _unpack_pair(plsc.bitcast(p, jnp.int32))
                    o_lo = o_lo + lo
                    o_hi = o_hi + hi
                out_v[s, col] = o_lo
                out_v[s, col_v] = o_hi

            @pl.when(c == PEER_NCH - PEER_RING)
            def _():
                for cp in meta_copies(nxt, 1 - s):
                    cp.wait()

            ahead = c + PEER_RING
            src = jnp.where(ahead < PEER_NCH, s, 1 - s)
            nrows = idx_v[src, pl.ds(pl.multiple_of((ahead % PEER_NCH) * PEER_CH, PEER_CH), PEER_CH)]
            gather(slot, nrows).start()
            return carry

        lax.fori_loop(0, PEER_NCH, chunk, 0)
        pltpu.make_async_copy(out_v.at[s], o_hbm.at[tok], osem.at[s]).start()
        return carry

    for cp in meta_copies(base, 0):
        cp.start()
    for cp in meta_copies(base, 0):
        cp.wait()
    for c in range(PEER_RING):
        gather(c, idx_v[0, pl.ds(c * PEER_CH, PEER_CH)]).start()
    lax.fori_loop(0, n_tok, token, 0)
    for c in range(PEER_RING):
        gather(c, zero_rows).wait()
    for s in range(2):
        pltpu.make_async_copy(out_v.at[s], o_hbm.at[base], osem.at[s]).wait()


def _pack_bf16_pairs(tab):
    b = lax.bitcast_convert_type(tab.astype(BF16), jnp.uint16).astype(jnp.uint32)
    half = tab.shape[1] // 2
    return lax.bitcast_convert_type(b[:, :half] | (b[:, half:] << 16), jnp.int32)


def _peer(idx, h_words, gates, uv_words, after):
    T = h_words.shape[0]
    assert T % (2 * SC_WORKERS) == 0
    mesh = plsc.VectorSubcoreMesh(core_axis_name="c", subcore_axis_name="s",
                                  num_cores=SC_CORES, num_subcores=SC_SUBCORES)
    return pl.kernel(
        _peer_sc_body,
        out_type=jax.ShapeDtypeStruct((T, D_MODEL), F32),
        mesh=mesh,
        scratch_types=[
            pltpu.VMEM((2, PEER_SEL), jnp.int32), pltpu.VMEM((2, PEER_SEL), F32),
            pltpu.VMEM((2, PEER_WORDS), jnp.int32),
            pltpu.VMEM((PEER_RING, PEER_CH, 2 * PEER_WORDS), jnp.int32),
            pltpu.VMEM((2, D_MODEL), F32),
            pltpu.SemaphoreType.DMA((PEER_RING,)),
            pltpu.SemaphoreType.DMA((2,)), pltpu.SemaphoreType.DMA((2,)),
        ],
        compiler_params=pltpu.CompilerParams(needs_layout_passes=False),
        name="peer_sc",
    )(idx, gates, h_words, uv_words, after)


FINAL_TS = 256


def _final_kernel(x1_ref, pe_ref, p_ref, gp_ref, wg_ref, wp_ref, gf_ref, o_ref):
    x2 = x1_ref[...] + pe_ref[...]
    e = _dot(p_ref[...].astype(BF16), wp_ref[...])
    gate = jax.nn.sigmoid(_dot(_rms(x2, gp_ref[...]).astype(BF16), wg_ref[...]))
    o_ref[...] = _rms(x2 + gate * e, gf_ref[...])


def _final(x1, peer_out, p, t0, nt, g_ple, ple_w_gate, ple_w_proj, g_final):
    B = p.shape[0]
    ts = min(FINAL_TS, nt)
    nblk = nt // ts
    i0 = t0 // ts
    row = lambda d: pl.BlockSpec((ts, d), lambda b, i: (b * nblk + i, 0))
    full = lambda shape: pl.BlockSpec(shape, lambda b, i: (0,) * len(shape))
    return pl.pallas_call(
        _final_kernel,
        grid=(B, nblk),
        in_specs=[row(D_MODEL), row(D_MODEL),
                  pl.BlockSpec((None, ts, D_PLE), lambda b, i: (b, i0 + i, 0)),
                  full((1, D_MODEL)), full((D_MODEL, D_MODEL)), full((D_PLE, D_MODEL)),
                  full((1, D_MODEL))],
        out_specs=pl.BlockSpec((None, ts, D_MODEL), lambda b, i: (b, i, 0)),
        out_shape=jax.ShapeDtypeStruct((B, nt, D_MODEL), F32),
        compiler_params=pltpu.CompilerParams(
            dimension_semantics=("parallel", "parallel"), vmem_limit_bytes=VMEM_LIMIT),
        name="final",
    )(x1, peer_out, p, g_ple, ple_w_gate, ple_w_proj, g_final)


CHUNK_STEPS = (512, 512, 1024, 1024, 1024, 1024, 1024, 1024, 512, 512)


def kernel(x, p, positions, g_mix, w_in, ssm_log_dt, ssm_a_re, ssm_a_im, ssm_b_re, ssm_b_im,
           ssm_c_re, ssm_c_im, ssm_d, ssm_w_glu, w_proj_ssm, w_proj_att, w_out, g_ffn,
           peer_w_q, peer_keys1, peer_keys2, peer_u, peer_v, g_ple, ple_w_gate, ple_w_proj,
           g_final):
    B, S, _ = x.shape
    assert w_in.shape[0] == 1, "the final rmsnorm is fused into the single layer's last stage"
    steps = CHUNK_STEPS if sum(CHUNK_STEPS) == S else (S,)
    i = 0
    tables = _s5_tables(ssm_log_dt[i], ssm_a_re[i], ssm_a_im[i], ssm_b_re[i], ssm_b_im[i],
                        ssm_c_re[i], ssm_c_im[i])
    w_in_b, w_glu_b = w_in[i].astype(BF16), ssm_w_glu[i].astype(BF16)
    d_skip = ssm_d[i].reshape(1, D_SSM).astype(F32)
    merge_w = (w_proj_ssm[i].astype(BF16), w_proj_att[i].astype(BF16), w_out[i].astype(BF16),
               g_ffn[i].reshape(1, D_MODEL), peer_w_q[i].astype(BF16), peer_keys1[i], peer_keys2[i])
    final_w = (g_ple[i].reshape(1, D_MODEL), ple_w_gate[i].astype(BF16),
               ple_w_proj[i].astype(BF16), g_final.reshape(1, D_MODEL))
    uv_words = jnp.concatenate([_pack_bf16_pairs(peer_u[i]), _pack_bf16_pairs(peer_v[i])], axis=1)
    k_all = jnp.zeros((B, S, D_ATT), BF16)
    v_all = jnp.zeros((B, S, D_ATT), BF16)
    carry = jnp.zeros((2, SUBLANES, D_STATE), F32)
    outs = []
    t0 = 0
    after = (carry, carry)
    peer_prev = carry
    for nt in steps:
        u_sb, q, k, v, ga, gb = _in_proj(x, positions, g_mix[i], w_in_b, t0, nt, after)
        k_all = lax.dynamic_update_slice(k_all, k, (0, t0, 0))
        v_all = lax.dynamic_update_slice(v_all, v, (0, t0, 0))
        ys, carry = _s5(u_sb, carry, tables, d_skip, w_glu_b, B)
        att = _moba(q, k_all, v_all, t0 // MOBA_BLOCK)
        x1, h_words, idx, gates = _merge(x, ys, att, ga, gb, t0, *merge_w)
        after = (gates, outs[-3] if len(outs) > 2 else carry)
        peer_out = _peer(idx, h_words, gates, uv_words, peer_prev)
        peer_prev = peer_out
        outs.append(_final(x1, peer_out, p[i], t0, nt, *final_w))
        t0 += nt
    return jnp.concatenate(outs, axis=1)
```

```python
import functools
import math

import jax
import jax.numpy as jnp
from jax import lax
from jax.experimental import pallas as pl
from jax.experimental.pallas import tpu as pltpu
from jax.experimental.pallas import tpu_sc as plsc

F32 = jnp.float32
BF16 = jnp.bfloat16

D_MODEL = 1024
D_SSM = 512
SSM_GROUP = 16
SSM_GROUPS = 32
SSM_STATE = 64
D_STATE = SSM_GROUPS * SSM_STATE
N_HEADS = 8
HEAD_DIM = 64
D_ATT = 512
ROT_DIM = 16
ROPE_THETA = 500000.0
MOBA_BLOCK = 256
MOBA_TOPK = 3
PEER_HEADS = 8
PEER_KEYS = 128
PEER_QDIM = 256
PEER_HALF = 128
PEER_TOPK = 16
PEER_SEL = PEER_HEADS * PEER_TOPK
D_PLE = 256
EPS = 1e-6
NEG = -1e30
LANES = 128
SUBLANES = 8
VMEM_LIMIT = 48 * 1024 * 1024
HIGHEST = lax.Precision.HIGHEST


def _rms(x, g):
    return x * lax.rsqrt(jnp.mean(x * x, axis=-1, keepdims=True) + EPS) * g


def _dot(a, b):
    return jnp.dot(a, b, preferred_element_type=F32)


def _dot_nt(a, b, precision=None):
    return lax.dot_general(a, b, (((1,), (1,)), ((), ())), precision=precision,
                           preferred_element_type=F32)


IN_TS = 512


def _in_proj_kernel(x_ref, pos_ref, g_ref, w_ref, invf_ref, after_a, after_b,
                    u_ref, q_ref, k_ref, v_ref, ga_ref, gb_ref):
    del after_a, after_b
    h = _rms(x_ref[...], g_ref[...]).astype(BF16)

    def proj(lo, hi):
        return _dot(h, w_ref[:, lo:hi])

    u_ref[...] = proj(0, D_SSM).astype(BF16)
    ang = pos_ref[...].astype(F32) * invf_ref[...]
    cos = jnp.cos(ang)
    sin = jnp.sin(ang)
    lane = lax.broadcasted_iota(jnp.int32, (1, LANES), 1) % HEAD_DIM
    half = ROT_DIM // 2
    sin_hi = jnp.where((lane >= half) & (lane < ROT_DIM), sin, 0.0)
    sin_lo = jnp.where(lane < half, -sin, 0.0)
    reps = D_ATT // LANES
    cos4 = jnp.concatenate([cos] * reps, axis=1)
    sin_hi4 = jnp.concatenate([sin_hi] * reps, axis=1)
    sin_lo4 = jnp.concatenate([sin_lo] * reps, axis=1)

    def rope(t):
        return (t * cos4 + pltpu.roll(t, half, 1) * sin_hi4
                + pltpu.roll(t, D_ATT - half, 1) * sin_lo4)

    q = rope(proj(D_SSM, D_SSM + D_ATT))
    q_ref[...] = (q * (HEAD_DIM ** -0.5)).astype(BF16)
    k_ref[...] = rope(proj(D_SSM + D_ATT, D_SSM + 2 * D_ATT)).astype(BF16)
    v_ref[...] = proj(D_SSM + 2 * D_ATT, D_SSM + 3 * D_ATT).astype(BF16)
    o = D_SSM + 3 * D_ATT
    ga_ref[...] = jax.nn.sigmoid(proj(o, o + D_MODEL)).astype(BF16)
    gb_ref[...] = jax.nn.sigmoid(proj(o + D_MODEL, o + 2 * D_MODEL)).astype(BF16)


def _in_proj(x, positions, g_mix, w_in, t0, nt, after):
    B, S, _ = x.shape
    ts = min(IN_TS, nt)
    assert nt % ts == 0 and t0 % ts == 0
    i0 = t0 // ts
    inv_freq = ROPE_THETA ** (-jnp.arange(0, ROT_DIM, 2, dtype=F32) / ROT_DIM)
    lane = jnp.arange(LANES) % HEAD_DIM
    invf = jnp.where(lane < ROT_DIM, inv_freq[lane % (ROT_DIM // 2)], 0.0).reshape(1, LANES)
    d_in = w_in.shape[1]
    src = lambda d: pl.BlockSpec((None, ts, d), lambda b, i: (b, i0 + i, 0))
    tok = lambda d: pl.BlockSpec((None, ts, d), lambda b, i: (b, i, 0))
    full = lambda shape: pl.BlockSpec(shape, lambda b, i: (0,) * len(shape))
    return pl.pallas_call(
        _in_proj_kernel,
        grid=(B, nt // ts),
        in_specs=[src(D_MODEL), src(1), full((1, D_MODEL)), full((D_MODEL, d_in)), full((1, LANES)),
                  pl.BlockSpec(memory_space=pl.ANY), pl.BlockSpec(memory_space=pl.ANY)],
        out_specs=[pl.BlockSpec((ts, D_SSM), lambda b, i: (i, b)),
                   tok(D_ATT), tok(D_ATT), tok(D_ATT), tok(D_MODEL), tok(D_MODEL)],
        out_shape=[jax.ShapeDtypeStruct((nt, B * D_SSM), BF16),
                   jax.ShapeDtypeStruct((B, nt, D_ATT), BF16),
                   jax.ShapeDtypeStruct((B, nt, D_ATT), BF16),
                   jax.ShapeDtypeStruct((B, nt, D_ATT), BF16),
                   jax.ShapeDtypeStruct((B, nt, D_MODEL), BF16),
                   jax.ShapeDtypeStruct((B, nt, D_MODEL), BF16)],
        compiler_params=pltpu.CompilerParams(
            dimension_semantics=("parallel", "parallel"), vmem_limit_bytes=VMEM_LIMIT),
        name="in_proj",
    )(x, positions.reshape(B, S, 1), g_mix.reshape(1, D_MODEL), w_in, invf, *after)


S5_TS = 128
S5_BATCH = 4
S5_COLS = 512


def _s5_kernel(u_ref, c0_ref, bre_ref, bim_ref, a1r_ref, a1i_ref, pr_ref, pi_ref,
               cre_ref, cim_ref, d_ref, wglu_ref, y_ref, c1_ref,
               xr, xi, cr, ci, ysc):
    rows = xr.shape[0]
    ts = rows // S5_BATCH

    @pl.when(pl.program_id(0) == 0)
    def _():
        cr[...] = c0_ref[0]
        ci[...] = c0_ref[1]

    u = u_ref[...]
    for cb in range(D_STATE // S5_COLS):
        sl = slice(cb * S5_COLS, (cb + 1) * S5_COLS)
        u_cb = u[:, cb * LANES:(cb + 1) * LANES]
        xr[:, sl] = _dot(u_cb, bre_ref[cb])
        xi[:, sl] = _dot(u_cb, bim_ref[cb])

    hi_rows = lax.broadcasted_iota(jnp.int32, (SUBLANES, S5_COLS), 0) >= S5_BATCH
    for cb in range(D_STATE // S5_COLS):
        sl = slice(cb * S5_COLS, (cb + 1) * S5_COLS)
        a_r, a_i = a1r_ref[:, sl], a1i_ref[:, sl]
        p_r, p_i = pr_ref[:, sl], pi_ref[:, sl]

        def body(t, carry):
            c_r, c_i = carry
            r0 = pl.multiple_of(t * SUBLANES, SUBLANES)
            x_r = xr[pl.ds(r0, SUBLANES), sl]
            x_i = xi[pl.ds(r0, SUBLANES), sl]
            s_r = pltpu.roll(x_r, S5_BATCH, 0)
            s_i = pltpu.roll(x_i, S5_BATCH, 0)
            h_r = x_r + (a_r * s_r - a_i * s_i) + (p_r * c_r - p_i * c_i)
            h_i = x_i + (a_r * s_i + a_i * s_r) + (p_r * c_i + p_i * c_r)
            xr[pl.ds(r0, SUBLANES), sl] = h_r
            xi[pl.ds(r0, SUBLANES), sl] = h_i
            n_r = jnp.where(hi_rows, h_r, pltpu.roll(h_r, S5_BATCH, 0))
            n_i = jnp.where(hi_rows, h_i, pltpu.roll(h_i, S5_BATCH, 0))
            return n_r, n_i

        c_r, c_i = lax.fori_loop(0, rows // SUBLANES, body, (cr[:, sl], ci[:, sl]), unroll=2)
        cr[:, sl] = c_r
        ci[:, sl] = c_i

    y = jnp.concatenate(
        [_dot(xr[:, cb * S5_COLS:(cb + 1) * S5_COLS].astype(BF16), cre_ref[cb])
         - _dot(xi[:, cb * S5_COLS:(cb + 1) * S5_COLS].astype(BF16), cim_ref[cb])
         for cb in range(D_STATE // S5_COLS)], axis=1) + d_ref[...] * u.astype(F32)
    y = jax.nn.gelu(y)
    y = y * jax.nn.sigmoid(_dot(y.astype(BF16), wglu_ref[...]))
    for c in range(D_SSM // LANES):
        ysc[c] = y[:, c * LANES:(c + 1) * LANES]
    for b in range(S5_BATCH):
        for c in range(D_SSM // LANES):
            y_ref[b, :, c * LANES:(c + 1) * LANES] = (
                ysc[c, pl.ds(b, ts, stride=S5_BATCH), :].astype(BF16))

    @pl.when(pl.program_id(0) == pl.num_programs(0) - 1)
    def _():
        c1_ref[0] = cr[...]
        c1_ref[1] = ci[...]


def _s5_tables(log_dt, a_re, a_im, b_re, b_im, c_re, c_im):
    dt = jnp.exp(log_dt.astype(F32))[:, None]
    ar, ai = a_re.astype(F32), a_im.astype(F32)
    mag = jnp.exp(dt * ar)
    abar_re, abar_im = mag * jnp.cos(dt * ai), mag * jnp.sin(dt * ai)
    den = ar * ar + ai * ai
    nr, ni = abar_re - 1.0, abar_im
    f_re = (nr * ar + ni * ai) / den
    f_im = (ni * ar - nr * ai) / den
    br, bi = b_re.astype(F32), b_im.astype(F32)
    bb_re = f_re[..., None] * br - f_im[..., None] * bi
    bb_im = f_re[..., None] * bi + f_im[..., None] * br
    eye = jnp.eye(SSM_GROUPS, dtype=F32)

    def in_blockdiag(bb):
        return jnp.einsum('gnc,gh->gchn', bb, eye).reshape(D_SSM, D_STATE)

    def out_blockdiag(c):
        return jnp.einsum('gcn,gh->gnhc', c.astype(F32), eye).reshape(D_STATE, D_SSM)

    a_r = abar_re.reshape(1, D_STATE)
    a_i = abar_im.reshape(1, D_STATE)
    a2_r = a_r * a_r - a_i * a_i
    a2_i = 2.0 * a_r * a_i
    hi = (jnp.arange(SUBLANES) >= S5_BATCH)[:, None]
    a1r = jnp.where(hi, a_r, 0.0)
    a1i = jnp.where(hi, a_i, 0.0)
    p_r = jnp.where(hi, a2_r, a_r)
    p_i = jnp.where(hi, a2_i, a_i)
    nblk = D_STATE // S5_COLS
    cw = D_SSM // nblk

    def in_blocks(m):
        return jnp.stack([m[b * cw:(b + 1) * cw, b * S5_COLS:(b + 1) * S5_COLS] for b in range(nblk)])

    def out_blocks(m):
        return jnp.stack([m[b * S5_COLS:(b + 1) * S5_COLS, b * cw:(b + 1) * cw] for b in range(nblk)])

    return (in_blocks(in_blockdiag(bb_re)).astype(BF16), in_blocks(in_blockdiag(bb_im)).astype(BF16),
            a1r, a1i, p_r, p_i,
            out_blocks(out_blockdiag(c_re)).astype(BF16), out_blocks(out_blockdiag(c_im)).astype(BF16))


def _s5(u_sb, carry, tables, d_skip, w_glu, B):
    assert B == S5_BATCH
    nt = u_sb.shape[0]
    ts = min(S5_TS, nt)
    rows = ts * B
    bre, bim, a1r, a1i, p_r, p_i, cre, cim = tables
    full = lambda shape: pl.BlockSpec(shape, lambda i: (0,) * len(shape))
    return pl.pallas_call(
        _s5_kernel,
        grid=(nt // ts,),
        in_specs=[pl.BlockSpec((rows, D_SSM), lambda i: (i, 0)),
                  full((2, SUBLANES, D_STATE)),
                  full(bre.shape), full(bim.shape),
                  full((SUBLANES, D_STATE)), full((SUBLANES, D_STATE)),
                  full((SUBLANES, D_STATE)), full((SUBLANES, D_STATE)),
                  full(cre.shape), full(cim.shape),
                  full((1, D_SSM)), full((D_SSM, D_SSM))],
        out_specs=[pl.BlockSpec((B, ts, D_SSM), lambda i: (0, i, 0)),
                   full((2, SUBLANES, D_STATE))],
        out_shape=[jax.ShapeDtypeStruct((B, nt, D_SSM), BF16),
                   jax.ShapeDtypeStruct((2, SUBLANES, D_STATE), F32)],
        scratch_shapes=[pltpu.VMEM((rows, D_STATE), F32), pltpu.VMEM((rows, D_STATE), F32),
                        pltpu.VMEM((SUBLANES, D_STATE), F32), pltpu.VMEM((SUBLANES, D_STATE), F32),
                        pltpu.VMEM((D_SSM // LANES, rows, LANES), F32)],
        compiler_params=pltpu.CompilerParams(
            dimension_semantics=("arbitrary",), vmem_limit_bytes=VMEM_LIMIT),
        name="s5",
    )(u_sb.reshape(nt * B, D_SSM), carry, bre, bim, a1r, a1i, p_r, p_i, cre, cim, d_skip, w_glu)


MOBA_PAIR = 2 * MOBA_BLOCK


def _moba_kernel(q0, q_ref, k_ref, v_ref, o_ref, kmean, kaug_a, kaug_b, vaug_a, vaug_b, qaug,
                 m_s, acc_s, s_buf):
    qi = pl.program_id(2) + q0
    nb = k_ref.shape[0] // MOBA_BLOCK
    nbp = kmean.shape[0]
    lane = lax.broadcasted_iota(jnp.int32, (1, LANES), 1)
    head_a = lane < HEAD_DIM

    @pl.when(pl.program_id(2) == 0)
    def _():
        kmean[...] = jnp.zeros_like(kmean)
        for j in range(nb):
            rows = pl.ds(j * MOBA_BLOCK, MOBA_BLOCK)
            kj = k_ref[rows, :].astype(F32)
            vj = v_ref[rows, :].astype(F32)
            kmean[j:j + 1, :] = jnp.sum(kj, axis=0, keepdims=True) * (1.0 / MOBA_BLOCK)
            kaug_a[rows, :] = jnp.where(head_a, kj, jnp.where(lane - HEAD_DIM == j, 1.0, 0.0)).astype(BF16)
            kaug_b[rows, :] = jnp.where(head_a, jnp.where(lane == j, 1.0, 0.0), kj).astype(BF16)
            vaug_a[rows, :] = jnp.where(head_a, vj, 1.0).astype(BF16)
            vaug_b[rows, :] = jnp.where(head_a, 1.0, vj).astype(BF16)
        blk_row = lax.broadcasted_iota(jnp.int32, (nbp, MOBA_BLOCK), 0)
        for t in range(q_ref.shape[0] // MOBA_BLOCK):
            qt = q0 + t
            qf = q_ref[t * MOBA_BLOCK:(t + 1) * MOBA_BLOCK, :].astype(F32)
            for hd, is_a in enumerate((True, False)):
                mine = head_a if is_a else jnp.logical_not(head_a)
                q_own = jnp.where(mine, qf, 0.0)
                g = _dot_nt(kmean[...], q_own, precision=HIGHEST)
                g = jnp.where(blk_row < qt, g, NEG)
                sel = jnp.zeros(g.shape, F32)
                for _ in range(MOBA_TOPK):
                    m = jnp.max(g, axis=0, keepdims=True)
                    idx = jnp.min(jnp.where(g == m, blk_row, nbp), axis=0, keepdims=True)
                    hit = blk_row == idx
                    sel = jnp.where(hit, jnp.where(idx < qt, 1.0, 0.0), sel)
                    g = jnp.where(hit, -jnp.inf, g)
                bias_t = jnp.where(sel > 0.0, 0.0, jnp.where(blk_row == qt, 0.0, NEG))
                bias_t = jnp.concatenate([bias_t, jnp.full((LANES - nbp, MOBA_BLOCK), NEG, F32)], axis=0)
                bias = jnp.transpose(bias_t)
                if is_a:
                    bias = pltpu.roll(bias, HEAD_DIM, 1)
                qaug[hd, t * MOBA_BLOCK:(t + 1) * MOBA_BLOCK, :] = jnp.where(mine, qf, bias).astype(BF16)

    tile_rows = pl.ds(pl.multiple_of(pl.program_id(2) * MOBA_BLOCK, MOBA_BLOCK), MOBA_BLOCK)
    q_augs = [qaug[0, tile_rows, :], qaug[1, tile_rows, :]]

    m_s[...] = jnp.full(m_s.shape, -jnp.inf, F32)
    acc_s[...] = jnp.zeros_like(acc_s)
    qpos = qi * MOBA_BLOCK + lax.broadcasted_iota(jnp.int32, (MOBA_BLOCK, MOBA_PAIR), 0)
    col = lax.broadcasted_iota(jnp.int32, (MOBA_BLOCK, MOBA_PAIR), 1)

    def kv_rows(jj):
        return pl.ds(pl.multiple_of(jj * MOBA_PAIR, MOBA_PAIR), MOBA_PAIR)

    def scores(jj, slot):
        for hd, kaug in enumerate((kaug_a, kaug_b)):
            s_buf[slot, hd] = _dot_nt(q_augs[hd], kaug[kv_rows(jj), :])

    def softmax_pv(jj, slot, causal):
        for hd, vaug in enumerate((vaug_a, vaug_b)):
            s = s_buf[slot, hd]
            if causal:
                s = jnp.where(jj * MOBA_PAIR + col <= qpos, s, NEG)
            m_old = m_s[hd]
            m_new = jnp.maximum(m_old, jnp.max(s, axis=-1, keepdims=True))
            alpha = jnp.exp(m_old - m_new)
            p = jnp.exp(s - m_new)
            m_s[hd] = m_new
            acc_s[hd] = alpha * acc_s[hd] + _dot(p.astype(BF16), vaug[kv_rows(jj), :])

    last = qi // 2
    scores(0, 0)

    def body(k, _):
        scores(2 * k + 1, 1)
        softmax_pv(2 * k, 0, False)
        scores(2 * k + 2, 0)
        softmax_pv(2 * k + 1, 1, False)
        return 0

    lax.fori_loop(0, last // 2, body, 0)

    @pl.when(last % 2 == 0)
    def _():
        softmax_pv(last, 0, True)

    @pl.when(last % 2 == 1)
    def _():
        scores(last, 1)
        softmax_pv(last - 1, 0, False)
        softmax_pv(last, 1, True)
    acc_a, acc_b = acc_s[0], acc_s[1]
    o_ref[...] = jnp.where(head_a, acc_a / pltpu.roll(acc_a, HEAD_DIM, 1),
                           acc_b / pltpu.roll(acc_b, HEAD_DIM, 1)).astype(BF16)


def _moba(q, k, v, q0):
    B = q.shape[0]
    nq = q.shape[1] // MOBA_BLOCK
    skv = (q0 + nq) * MOBA_BLOCK
    nb = skv // MOBA_BLOCK
    assert nb <= HEAD_DIM and nb % 2 == 0 and skv <= k.shape[1]
    nbp = -(-nb // SUBLANES) * SUBLANES
    blk = pl.BlockSpec((None, MOBA_BLOCK, LANES), lambda b, h, i: (b, i, h))
    seq = pl.BlockSpec((None, skv, LANES), lambda b, h, i: (b, 0, h))
    return pl.pallas_call(
        functools.partial(_moba_kernel, q0),
        grid=(B, D_ATT // LANES, nq),
        in_specs=[pl.BlockSpec((None, nq * MOBA_BLOCK, LANES), lambda b, h, i: (b, 0, h)), seq, seq],
        out_specs=blk,
        out_shape=jax.ShapeDtypeStruct(q.shape, BF16),
        scratch_shapes=[pltpu.VMEM((nbp, LANES), F32),
                        pltpu.VMEM((skv, LANES), BF16), pltpu.VMEM((skv, LANES), BF16),
                        pltpu.VMEM((skv, LANES), BF16), pltpu.VMEM((skv, LANES), BF16),
                        pltpu.VMEM((2, nq * MOBA_BLOCK, LANES), BF16),
                        pltpu.VMEM((2, MOBA_BLOCK, 1), F32),
                        pltpu.VMEM((2, MOBA_BLOCK, LANES), F32),
                        pltpu.VMEM((2, 2, MOBA_BLOCK, MOBA_PAIR), F32)],
        compiler_params=pltpu.CompilerParams(
            dimension_semantics=("parallel", "parallel", "arbitrary"), vmem_limit_bytes=VMEM_LIMIT),
        name="moba",
    )(q, k, v)


MERGE_TS = 256


def _bf16_bits(x):
    b = pltpu.bitcast(x, jnp.int32)
    r = b + 0x7FFF + (lax.shift_right_logical(b, 16) & 1)
    return lax.shift_right_logical(r, 16)


def _merge_kernel(x_ref, ys_ref, at_ref, ga_ref, gb_ref, wa_ref, wb_ref, wo_ref, g_ref,
                  wq_ref, k1_ref, k2_ref, x1_ref, hw_ref, idx_ref, gate_ref, sc_ref):
    ya = _dot(ys_ref[...], wa_ref[...])
    yb = _dot(at_ref[...], wb_ref[...])
    merged = ga_ref[...].astype(F32) * ya + gb_ref[...].astype(F32) * yb
    x1 = x_ref[...] + _dot(merged.astype(BF16), wo_ref[...])
    x1_ref[...] = x1
    hq = _rms(x1, g_ref[...])
    half = D_MODEL // 2
    hw_ref[...] = _bf16_bits(hq[:, :half]) | lax.shift_left(_bf16_bits(hq[:, half:]), 16)
    qp = _dot(hq.astype(BF16), wq_ref[...])
    for h in range(PEER_HEADS):
        o = h * PEER_QDIM
        sc_ref[2 * h] = _dot_nt(k1_ref[h], qp[:, o:o + PEER_HALF], precision=HIGHEST)
        sc_ref[2 * h + 1] = _dot_nt(k2_ref[h], qp[:, o + PEER_HALF:o + PEER_QDIM], precision=HIGHEST)
    _topk_kernel(sc_ref, idx_ref, gate_ref)


def _merge(x, ys, att, ga, gb, t0, w_proj_ssm, w_proj_att, w_out, g_ffn, peer_w_q, keys1, keys2):
    B, nt = ys.shape[0], ys.shape[1]
    ts = min(MERGE_TS, nt)
    nblk = nt // ts
    i0 = t0 // ts
    tok = lambda d: pl.BlockSpec((None, ts, d), lambda b, i: (b, i, 0))
    row = lambda d: pl.BlockSpec((ts, d), lambda b, i: (b * nblk + i, 0))
    full = lambda shape: pl.BlockSpec(shape, lambda b, i: (0,) * len(shape))
    qd = PEER_HEADS * PEER_QDIM
    return pl.pallas_call(
        _merge_kernel,
        grid=(B, nblk),
        in_specs=[pl.BlockSpec((None, ts, D_MODEL), lambda b, i: (b, i0 + i, 0)),
                  tok(D_SSM), tok(D_ATT), tok(D_MODEL), tok(D_MODEL),
                  full((D_SSM, D_MODEL)), full((D_ATT, D_MODEL)), full((D_MODEL, D_MODEL)),
                  full((1, D_MODEL)), full((D_MODEL, qd)),
                  full((PEER_HEADS, PEER_KEYS, PEER_HALF)), full((PEER_HEADS, PEER_KEYS, PEER_HALF))],
        out_specs=[row(D_MODEL), row(D_MODEL // 2), row(PEER_SEL), row(PEER_SEL)],
        out_shape=[jax.ShapeDtypeStruct((B * nt, D_MODEL), F32),
                   jax.ShapeDtypeStruct((B * nt, D_MODEL // 2), jnp.int32),
                   jax.ShapeDtypeStruct((B * nt, PEER_SEL), jnp.int32),
                   jax.ShapeDtypeStruct((B * nt, PEER_SEL), F32)],
        scratch_shapes=[pltpu.VMEM((2 * PEER_HEADS, PEER_KEYS, ts), F32)],
        compiler_params=pltpu.CompilerParams(
            dimension_semantics=("parallel", "parallel"), vmem_limit_bytes=VMEM_LIMIT),
        name="merge",
    )(x, ys, att, ga, gb, w_proj_ssm, w_proj_att, w_out, g_ffn, peer_w_q, keys1, keys2)


def _top_rows(s, row, k):
    vals, idxs = [], []
    for _ in range(k):
        m = jnp.max(s, axis=0, keepdims=True)
        idx = jnp.min(jnp.where(s == m, row, s.shape[0]), axis=0, keepdims=True)
        vals.append(m)
        idxs.append(idx)
        s = jnp.where(row == idx, -jnp.inf, s)
    return vals, idxs


def _stack_rows(rows, row16):
    acc = jnp.zeros(row16.shape, rows[0].dtype)
    for r, v in enumerate(rows):
        acc = jnp.where(row16 == r, v, acc)
    return acc


def _topk_kernel(sc_ref, idx_ref, gate_ref):
    ts = sc_ref.shape[-1]
    row = lax.broadcasted_iota(jnp.int32, (PEER_KEYS, ts), 0).astype(F32)
    row16 = lax.broadcasted_iota(jnp.int32, (PEER_TOPK, ts), 0)
    row8 = lax.broadcasted_iota(jnp.int32, (SUBLANES, ts), 0)
    counts = [PEER_TOPK // (i + 1) for i in range(PEER_TOPK)]
    heights = [PEER_TOPK if c > SUBLANES else SUBLANES for c in counts]
    n_cand = sum(heights)
    rowc = lax.broadcasted_iota(jnp.int32, (n_cand, ts), 0).astype(F32)
    gate_rows, eid_rows = [], []
    for h in range(PEER_HEADS):
        v1, i1 = _top_rows(sc_ref[2 * h], row, PEER_TOPK)
        v2, i2 = _top_rows(sc_ref[2 * h + 1], row, PEER_TOPK)
        v2s = _stack_rows(v2, row16)
        i2s = _stack_rows(i2, row16)
        cand, eid = [], []
        for i in range(PEER_TOPK):
            n = heights[i]
            cand.append(jnp.where((row16 if n == PEER_TOPK else row8) < counts[i],
                                  v1[i] + v2s[:n], -jnp.inf))
            eid.append(i1[i] * PEER_KEYS + i2s[:n])
        cand = jnp.concatenate(cand, axis=0)
        eid = jnp.concatenate(eid, axis=0)
        tops, picks = [], []
        for _ in range(PEER_TOPK):
            m = jnp.max(cand, axis=0, keepdims=True)
            pos = jnp.min(jnp.where(cand == m, rowc, n_cand), axis=0, keepdims=True)
            hit = rowc == pos
            picks.append(jnp.max(jnp.where(hit, eid, -1.0), axis=0, keepdims=True))
            tops.append(m)
            cand = jnp.where(hit, -jnp.inf, cand)
        top = _stack_rows(tops, row16)
        p = jnp.exp(top - jnp.max(top, axis=0, keepdims=True))
        gate_rows.append(p / jnp.sum(p, axis=0, keepdims=True))
        eid_rows.append(_stack_rows(picks, row16))
    gate_ref[...] = jnp.transpose(jnp.concatenate(gate_rows, axis=0))
    idx_ref[...] = jnp.transpose(jnp.concatenate(eid_rows, axis=0)).astype(jnp.int32)


SC_CORES = 2
SC_SUBCORES = 16
SC_LANES = 16
SC_WORKERS = SC_CORES * SC_SUBCORES
PEER_CH = SC_LANES
PEER_NCH = PEER_SEL // PEER_CH
PEER_WORDS = D_MODEL // 2
PEER_NWG = PEER_WORDS // SC_LANES
PEER_RING = 4
PEER_QUAD = 4
HI_MASK = -65536
GELU_C = 0.7978845608028654


def _gelu_tanh_via_exp(x):
    z = GELU_C * (x + 0.044715 * (x * x * x))
    t = 1.0 - 2.0 / (jnp.exp(2.0 * z) + 1.0)
    return 0.5 * x * (1.0 + t)


def _unpack_pair(w):
    lo = plsc.bitcast(lax.shift_left(w, 16), F32)
    hi = plsc.bitcast(lax.bitwise_and(w, HI_MASK), F32)
    return lo, hi


def _peer_sc_body(idx_hbm, gate_hbm, h_hbm, uv_hbm, after_hbm, o_hbm,
                  idx_v, gate_v, h_v, buf, out_v, gsem, msem, osem):
    n_tok = o_hbm.shape[0] // SC_WORKERS
    base = (lax.axis_index("s") * SC_CORES + lax.axis_index("c")) * n_tok
    lane = lax.iota(jnp.int32, SC_LANES)
    zero_rows = jnp.zeros((SC_LANES,), jnp.int32)

    def meta_copies(tok, s):
        return (pltpu.make_async_copy(idx_hbm.at[tok], idx_v.at[s], msem.at[s]),
                pltpu.make_async_copy(gate_hbm.at[tok], gate_v.at[s], msem.at[s]),
                pltpu.make_async_copy(h_hbm.at[tok], h_v.at[s], msem.at[s]))

    def gather(slot, rows):
        return pltpu.make_async_copy(uv_hbm.at[rows], buf.at[slot], gsem.at[slot])

    def token(t, carry):
        s = t % 2
        tok = base + t
        nxt = base + jnp.minimum(t + 1, n_tok - 1)
        for cp in meta_copies(nxt, 1 - s):
            cp.start()

        @pl.when(t >= 2)
        def _():
            pltpu.make_async_copy(out_v.at[s], o_hbm.at[tok], osem.at[s]).wait()

        def chunk(c, carry):
            slot = c % PEER_RING
            gather(slot, zero_rows).wait()

            def dot_step(q, accs):
                cols = [pl.ds(pl.multiple_of((q * PEER_QUAD + j) * SC_LANES, SC_LANES), SC_LANES)
                        for j in range(PEER_QUAD)]
                hs = [plsc.bitcast(h_v[s, col], BF16) for col in cols]
                out = []
                for r in range(PEER_CH):
                    p = plsc.bitcast(buf[slot, r, cols[0]], BF16) * hs[0]
                    for j in range(1, PEER_QUAD):
                        p = p + plsc.bitcast(buf[slot, r, cols[j]], BF16) * hs[j]
                    lo, hi = _unpack_pair(plsc.bitcast(p, jnp.int32))
                    out.append(accs[r] + lo + hi)
                return tuple(out)

            accs = lax.fori_loop(0, PEER_NWG // PEER_QUAD, dot_step,
                                 tuple(jnp.zeros((SC_LANES,), F32) for _ in range(PEER_CH)))
            tot = jnp.zeros((SC_LANES,), F32)
            for r in range(PEER_CH):
                tot = jnp.where(lane == r, jnp.sum(accs[r]), tot)
            rows = pl.ds(pl.multiple_of(c * PEER_CH, PEER_CH), PEER_CH)
            wvec = gate_v[s, rows] * _gelu_tanh_via_exp(tot)
            ws = []
            for r in range(PEER_CH):
                w = wvec.at[jnp.full((SC_LANES,), r, jnp.int32)].get(mode="promise_in_bounds")
                ws.append(plsc.pack(w, w, format=plsc.PackFormat.INTERLEAVED,
                                    preferred_element_type=BF16))
            first = c == 0

            @plsc.parallel_loop(0, PEER_NWG, unroll=2)
            def acc_step(g):
                col = pl.ds(pl.multiple_of(g * SC_LANES, SC_LANES), SC_LANES)
                col_v = pl.ds(pl.multiple_of(PEER_WORDS + g * SC_LANES, SC_LANES), SC_LANES)
                o_lo = jnp.where(first, 0.0, out_v[s, col])
                o_hi = jnp.where(first, 0.0, out_v[s, col_v])
                for r0 in range(0, PEER_CH, PEER_QUAD):
                    p = plsc.bitcast(buf[slot, r0, col_v], BF16) * ws[r0]
                    for r in range(r0 + 1, r0 + PEER_QUAD):
                        p = p + plsc.bitcast(buf[slot, r, col_v], BF16) * ws[r]
                    lo, hi = _unpack_pair(plsc.bitcast(p, jnp.int32))
                    o_lo = o_lo + lo
                    o_hi = o_hi + hi
                out_v[s, col] = o_lo
                out_v[s, col_v] = o_hi

            @pl.when(c == PEER_NCH - PEER_RING)
            def _():
                for cp in meta_copies(nxt, 1 - s):
                    cp.wait()

            ahead = c + PEER_RING
            src = jnp.where(ahead < PEER_NCH, s, 1 - s)
            nrows = idx_v[src, pl.ds(pl.multiple_of((ahead % PEER_NCH) * PEER_CH, PEER_CH), PEER_CH)]
            gather(slot, nrows).start()
            return carry

        lax.fori_loop(0, PEER_NCH, chunk, 0)
        pltpu.make_async_copy(out_v.at[s], o_hbm.at[tok], osem.at[s]).start()
        return carry

    for cp in meta_copies(base, 0):
        cp.start()
    for cp in meta_copies(base, 0):
        cp.wait()
    for c in range(PEER_RING):
        gather(c, idx_v[0, pl.ds(c * PEER_CH, PEER_CH)]).start()
    lax.fori_loop(0, n_tok, token, 0)
    for c in range(PEER_RING):
        gather(c, zero_rows).wait()
    for s in range(2):
        pltpu.make_async_copy(out_v.at[s], o_hbm.at[base], osem.at[s]).wait()


def _pack_bf16_pairs(tab):
    b = lax.bitcast_convert_type(tab.astype(BF16), jnp.uint16).astype(jnp.uint32)
    half = tab.shape[1] // 2
    return lax.bitcast_convert_type(b[:, :half] | (b[:, half:] << 16), jnp.int32)


def _peer(idx, h_words, gates, uv_words, after):
    T = h_words.shape[0]
    assert T % (2 * SC_WORKERS) == 0
    mesh = plsc.VectorSubcoreMesh(core_axis_name="c", subcore_axis_name="s",
                                  num_cores=SC_CORES, num_subcores=SC_SUBCORES)
    return pl.kernel(
        _peer_sc_body,
        out_type=jax.ShapeDtypeStruct((T, D_MODEL), F32),
        mesh=mesh,
        scratch_types=[
            pltpu.VMEM((2, PEER_SEL), jnp.int32), pltpu.VMEM((2, PEER_SEL), F32),
            pltpu.VMEM((2, PEER_WORDS), jnp.int32),
            pltpu.VMEM((PEER_RING, PEER_CH, 2 * PEER_WORDS), jnp.int32),
            pltpu.VMEM((2, D_MODEL), F32),
            pltpu.SemaphoreType.DMA((PEER_RING,)),
            pltpu.SemaphoreType.DMA((2,)), pltpu.SemaphoreType.DMA((2,)),
        ],
        compiler_params=pltpu.CompilerParams(needs_layout_passes=False),
        name="peer_sc",
    )(idx, gates, h_words, uv_words, after)


FINAL_TS = 256


def _final_kernel(x1_ref, pe_ref, p_ref, gp_ref, wg_ref, wp_ref, gf_ref, o_ref):
    x2 = x1_ref[...] + pe_ref[...]
    e = _dot(p_ref[...].astype(BF16), wp_ref[...])
    gate = jax.nn.sigmoid(_dot(_rms(x2, gp_ref[...]).astype(BF16), wg_ref[...]))
    o_ref[...] = _rms(x2 + gate * e, gf_ref[...])


def _final(x1, peer_out, p, t0, nt, g_ple, ple_w_gate, ple_w_proj, g_final):
    B = p.shape[0]
    ts = min(FINAL_TS, nt)
    nblk = nt // ts
    i0 = t0 // ts
    row = lambda d: pl.BlockSpec((ts, d), lambda b, i: (b * nblk + i, 0))
    full = lambda shape: pl.BlockSpec(shape, lambda b, i: (0,) * len(shape))
    return pl.pallas_call(
        _final_kernel,
        grid=(B, nblk),
        in_specs=[row(D_MODEL), row(D_MODEL),
                  pl.BlockSpec((None, ts, D_PLE), lambda b, i: (b, i0 + i, 0)),
                  full((1, D_MODEL)), full((D_MODEL, D_MODEL)), full((D_PLE, D_MODEL)),
                  full((1, D_MODEL))],
        out_specs=pl.BlockSpec((None, ts, D_MODEL), lambda b, i: (b, i, 0)),
        out_shape=jax.ShapeDtypeStruct((B, nt, D_MODEL), F32),
        compiler_params=pltpu.CompilerParams(
            dimension_semantics=("parallel", "parallel"), vmem_limit_bytes=VMEM_LIMIT),
        name="final",
    )(x1, peer_out, p, g_ple, ple_w_gate, ple_w_proj, g_final)


CHUNK_STEPS = (512, 512, 1024, 1024, 1024, 1024, 1024, 1024, 512, 512)


def kernel(x, p, positions, g_mix, w_in, ssm_log_dt, ssm_a_re, ssm_a_im, ssm_b_re, ssm_b_im,
           ssm_c_re, ssm_c_im, ssm_d, ssm_w_glu, w_proj_ssm, w_proj_att, w_out, g_ffn,
           peer_w_q, peer_keys1, peer_keys2, peer_u, peer_v, g_ple, ple_w_gate, ple_w_proj,
           g_final):
    B, S, _ = x.shape
    assert w_in.shape[0] == 1, "the final rmsnorm is fused into the single layer's last stage"
    steps = CHUNK_STEPS if sum(CHUNK_STEPS) == S else (S,)
    i = 0
    tables = _s5_tables(ssm_log_dt[i], ssm_a_re[i], ssm_a_im[i], ssm_b_re[i], ssm_b_im[i],
                        ssm_c_re[i], ssm_c_im[i])
    w_in_b, w_glu_b = w_in[i].astype(BF16), ssm_w_glu[i].astype(BF16)
    d_skip = ssm_d[i].reshape(1, D_SSM).astype(F32)
    merge_w = (w_proj_ssm[i].astype(BF16), w_proj_att[i].astype(BF16), w_out[i].astype(BF16),
               g_ffn[i].reshape(1, D_MODEL), peer_w_q[i].astype(BF16), peer_keys1[i], peer_keys2[i])
    final_w = (g_ple[i].reshape(1, D_MODEL), ple_w_gate[i].astype(BF16),
               ple_w_proj[i].astype(BF16), g_final.reshape(1, D_MODEL))
    uv_words = jnp.concatenate([_pack_bf16_pairs(peer_u[i]), _pack_bf16_pairs(peer_v[i])], axis=1)
    k_all = jnp.zeros((B, S, D_ATT), BF16)
    v_all = jnp.zeros((B, S, D_ATT), BF16)
    carry = jnp.zeros((2, SUBLANES, D_STATE), F32)
    outs = []
    t0 = 0
    after = (carry, carry)
    peer_prev = carry
    for nt in steps:
        u_sb, q, k, v, ga, gb = _in_proj(x, positions, g_mix[i], w_in_b, t0, nt, after)
        k_all = lax.dynamic_update_slice(k_all, k, (0, t0, 0))
        v_all = lax.dynamic_update_slice(v_all, v, (0, t0, 0))
        ys, carry = _s5(u_sb, carry, tables, d_skip, w_glu_b, B)
        att = _moba(q, k_all, v_all, t0 // MOBA_BLOCK)
        x1, h_words, idx, gates = _merge(x, ys, att, ga, gb, t0, *merge_w)
        after = (gates, outs[-3] if len(outs) > 2 else carry)
        peer_out = _peer(idx, h_words, gates, uv_words, peer_prev)
        peer_prev = peer_out
        outs.append(_final(x1, peer_out, p[i], t0, nt, *final_w))
        t0 += nt
    return jnp.concatenate(outs, axis=1)
```

```python
import functools
import math

import jax
import jax.numpy as jnp
from jax import lax
from jax.experimental import pallas as pl
from jax.experimental.pallas import tpu as pltpu
from jax.experimental.pallas import tpu_sc as plsc

F32 = jnp.float32
BF16 = jnp.bfloat16

D_MODEL = 1024
D_SSM = 512
SSM_GROUP = 16
SSM_GROUPS = 32
SSM_STATE = 64
D_STATE = SSM_GROUPS * SSM_STATE
N_HEADS = 8
HEAD_DIM = 64
D_ATT = 512
ROT_DIM = 16
ROPE_THETA = 500000.0
MOBA_BLOCK = 256
MOBA_TOPK = 3
PEER_HEADS = 8
PEER_KEYS = 128
PEER_QDIM = 256
PEER_HALF = 128
PEER_TOPK = 16
PEER_SEL = PEER_HEADS * PEER_TOPK
D_PLE = 256
EPS = 1e-6
NEG = -1e30
LANES = 128
SUBLANES = 8
VMEM_LIMIT = 48 * 1024 * 1024
HIGHEST = lax.Precision.HIGHEST


def _rms(x, g):
    return x * lax.rsqrt(jnp.mean(x * x, axis=-1, keepdims=True) + EPS) * g


def _dot(a, b):
    return jnp.dot(a, b, preferred_element_type=F32)


def _dot_nt(a, b, precision=None):
    return lax.dot_general(a, b, (((1,), (1,)), ((), ())), precision=precision,
                           preferred_element_type=F32)


IN_TS = 512


def _in_proj_kernel(x_ref, pos_ref, g_ref, w_ref, invf_ref, after_a, after_b,
                    u_ref, q_ref, k_ref, v_ref, ga_ref, gb_ref):
    del after_a, after_b
    h = _rms(x_ref[...], g_ref[...]).astype(BF16)

    def proj(lo, hi):
        return _dot(h, w_ref[:, lo:hi])

    u_ref[...] = proj(0, D_SSM).astype(BF16)
    ang = pos_ref[...].astype(F32) * invf_ref[...]
    cos = jnp.cos(ang)
    sin = jnp.sin(ang)
    lane = lax.broadcasted_iota(jnp.int32, (1, LANES), 1) % HEAD_DIM
    half = ROT_DIM // 2
    sin_hi = jnp.where((lane >= half) & (lane < ROT_DIM), sin, 0.0)
    sin_lo = jnp.where(lane < half, -sin, 0.0)
    reps = D_ATT // LANES
    cos4 = jnp.concatenate([cos] * reps, axis=1)
    sin_hi4 = jnp.concatenate([sin_hi] * reps, axis=1)
    sin_lo4 = jnp.concatenate([sin_lo] * reps, axis=1)

    def rope(t):
        return (t * cos4 + pltpu.roll(t, half, 1) * sin_hi4
                + pltpu.roll(t, D_ATT - half, 1) * sin_lo4)

    q = rope(proj(D_SSM, D_SSM + D_ATT))
    q_ref[...] = (q * (HEAD_DIM ** -0.5)).astype(BF16)
    k_ref[...] = rope(proj(D_SSM + D_ATT, D_SSM + 2 * D_ATT)).astype(BF16)
    v_ref[...] = proj(D_SSM + 2 * D_ATT, D_SSM + 3 * D_ATT).astype(BF16)
    o = D_SSM + 3 * D_ATT
    ga_ref[...] = jax.nn.sigmoid(proj(o, o + D_MODEL)).astype(BF16)
    gb_ref[...] = jax.nn.sigmoid(proj(o + D_MODEL, o + 2 * D_MODEL)).astype(BF16)


def _in_proj(x, positions, g_mix, w_in, t0, nt, after):
    B, S, _ = x.shape
    ts = min(IN_TS, nt)
    assert nt % ts == 0 and t0 % ts == 0
    i0 = t0 // ts
    inv_freq = ROPE_THETA ** (-jnp.arange(0, ROT_DIM, 2, dtype=F32) / ROT_DIM)
    lane = jnp.arange(LANES) % HEAD_DIM
    invf = jnp.where(lane < ROT_DIM, inv_freq[lane % (ROT_DIM // 2)], 0.0).reshape(1, LANES)
    d_in = w_in.shape[1]
    src = lambda d: pl.BlockSpec((None, ts, d), lambda b, i: (b, i0 + i, 0))
    tok = lambda d: pl.BlockSpec((None, ts, d), lambda b, i: (b, i, 0))
    full = lambda shape: pl.BlockSpec(shape, lambda b, i: (0,) * len(shape))
    return pl.pallas_call(
        _in_proj_kernel,
        grid=(B, nt // ts),
        in_specs=[src(D_MODEL), src(1), full((1, D_MODEL)), full((D_MODEL, d_in)), full((1, LANES)),
                  pl.BlockSpec(memory_space=pl.ANY), pl.BlockSpec(memory_space=pl.ANY)],
        out_specs=[pl.BlockSpec((ts, D_SSM), lambda b, i: (i, b)),
                   tok(D_ATT), tok(D_ATT), tok(D_ATT), tok(D_MODEL), tok(D_MODEL)],
        out_shape=[jax.ShapeDtypeStruct((nt, B * D_SSM), BF16),
                   jax.ShapeDtypeStruct((B, nt, D_ATT), BF16),
                   jax.ShapeDtypeStruct((B, nt, D_ATT), BF16),
                   jax.ShapeDtypeStruct((B, nt, D_ATT), BF16),
                   jax.ShapeDtypeStruct((B, nt, D_MODEL), BF16),
                   jax.ShapeDtypeStruct((B, nt, D_MODEL), BF16)],
        compiler_params=pltpu.CompilerParams(
            dimension_semantics=("parallel", "parallel"), vmem_limit_bytes=VMEM_LIMIT),
        name="in_proj",
    )(x, positions.reshape(B, S, 1), g_mix.reshape(1, D_MODEL), w_in, invf, *after)


S5_TS = 128
S5_BATCH = 4
S5_COLS = 512


def _s5_kernel(u_ref, c0_ref, bre_ref, bim_ref, a1r_ref, a1i_ref, pr_ref, pi_ref,
               cre_ref, cim_ref, d_ref, wglu_ref, y_ref, c1_ref,
               xr, xi, cr, ci, ysc):
    rows = xr.shape[0]
    ts = rows // S5_BATCH

    @pl.when(pl.program_id(0) == 0)
    def _():
        cr[...] = c0_ref[0]
        ci[...] = c0_ref[1]

    u = u_ref[...]
    for cb in range(D_STATE // S5_COLS):
        sl = slice(cb * S5_COLS, (cb + 1) * S5_COLS)
        u_cb = u[:, cb * LANES:(cb + 1) * LANES]
        xr[:, sl] = _dot(u_cb, bre_ref[cb])
        xi[:, sl] = _dot(u_cb, bim_ref[cb])

    hi_rows = lax.broadcasted_iota(jnp.int32, (SUBLANES, S5_COLS), 0) >= S5_BATCH
    for cb in range(D_STATE // S5_COLS):
        sl = slice(cb * S5_COLS, (cb + 1) * S5_COLS)
        a_r, a_i = a1r_ref[:, sl], a1i_ref[:, sl]
        p_r, p_i = pr_ref[:, sl], pi_ref[:, sl]

        def body(t, carry):
            c_r, c_i = carry
            r0 = pl.multiple_of(t * SUBLANES, SUBLANES)
            x_r = xr[pl.ds(r0, SUBLANES), sl]
            x_i = xi[pl.ds(r0, SUBLANES), sl]
            s_r = pltpu.roll(x_r, S5_BATCH, 0)
            s_i = pltpu.roll(x_i, S5_BATCH, 0)
            h_r = x_r + (a_r * s_r - a_i * s_i) + (p_r * c_r - p_i * c_i)
            h_i = x_i + (a_r * s_i + a_i * s_r) + (p_r * c_i + p_i * c_r)
            xr[pl.ds(r0, SUBLANES), sl] = h_r
            xi[pl.ds(r0, SUBLANES), sl] = h_i
            n_r = jnp.where(hi_rows, h_r, pltpu.roll(h_r, S5_BATCH, 0))
            n_i = jnp.where(hi_rows, h_i, pltpu.roll(h_i, S5_BATCH, 0))
            return n_r, n_i

        c_r, c_i = lax.fori_loop(0, rows // SUBLANES, body, (cr[:, sl], ci[:, sl]), unroll=2)
        cr[:, sl] = c_r
        ci[:, sl] = c_i

    y = jnp.concatenate(
        [_dot(xr[:, cb * S5_COLS:(cb + 1) * S5_COLS].astype(BF16), cre_ref[cb])
         - _dot(xi[:, cb * S5_COLS:(cb + 1) * S5_COLS].astype(BF16), cim_ref[cb])
         for cb in range(D_STATE // S5_COLS)], axis=1) + d_ref[...] * u.astype(F32)
    y = jax.nn.gelu(y)
    y = y * jax.nn.sigmoid(_dot(y.astype(BF16), wglu_ref[...]))
    for c in range(D_SSM // LANES):
        ysc[c] = y[:, c * LANES:(c + 1) * LANES]
    for b in range(S5_BATCH):
        for c in range(D_SSM // LANES):
            y_ref[b, :, c * LANES:(c + 1) * LANES] = (
                ysc[c, pl.ds(b, ts, stride=S5_BATCH), :].astype(BF16))

    @pl.when(pl.program_id(0) == pl.num_programs(0) - 1)
    def _():
        c1_ref[0] = cr[...]
        c1_ref[1] = ci[...]


def _s5_tables(log_dt, a_re, a_im, b_re, b_im, c_re, c_im):
    dt = jnp.exp(log_dt.astype(F32))[:, None]
    ar, ai = a_re.astype(F32), a_im.astype(F32)
    mag = jnp.exp(dt * ar)
    abar_re, abar_im = mag * jnp.cos(dt * ai), mag * jnp.sin(dt * ai)
    den = ar * ar + ai * ai
    nr, ni = abar_re - 1.0, abar_im
    f_re = (nr * ar + ni * ai) / den
    f_im = (ni * ar - nr * ai) / den
    br, bi = b_re.astype(F32), b_im.astype(F32)
    bb_re = f_re[..., None] * br - f_im[..., None] * bi
    bb_im = f_re[..., None] * bi + f_im[..., None] * br
    eye = jnp.eye(SSM_GROUPS, dtype=F32)

    def in_blockdiag(bb):
        return jnp.einsum('gnc,gh->gchn', bb, eye).reshape(D_SSM, D_STATE)

    def out_blockdiag(c):
        return jnp.einsum('gcn,gh->gnhc', c.astype(F32), eye).reshape(D_STATE, D_SSM)

    a_r = abar_re.reshape(1, D_STATE)
    a_i = abar_im.reshape(1, D_STATE)
    a2_r = a_r * a_r - a_i * a_i
    a2_i = 2.0 * a_r * a_i
    hi = (jnp.arange(SUBLANES) >= S5_BATCH)[:, None]
    a1r = jnp.where(hi, a_r, 0.0)
    a1i = jnp.where(hi, a_i, 0.0)
    p_r = jnp.where(hi, a2_r, a_r)
    p_i = jnp.where(hi, a2_i, a_i)
    nblk = D_STATE // S5_COLS
    cw = D_SSM // nblk

    def in_blocks(m):
        return jnp.stack([m[b * cw:(b + 1) * cw, b * S5_COLS:(b + 1) * S5_COLS] for b in range(nblk)])

    def out_blocks(m):
        return jnp.stack([m[b * S5_COLS:(b + 1) * S5_COLS, b * cw:(b + 1) * cw] for b in range(nblk)])

    return (in_blocks(in_blockdiag(bb_re)).astype(BF16), in_blocks(in_blockdiag(bb_im)).astype(BF16),
            a1r, a1i, p_r, p_i,
            out_blocks(out_blockdiag(c_re)).astype(BF16), out_blocks(out_blockdiag(c_im)).astype(BF16))


def _s5(u_sb, carry, tables, d_skip, w_glu, B):
    assert B == S5_BATCH
    nt = u_sb.shape[0]
    ts = min(S5_TS, nt)
    rows = ts * B
    bre, bim, a1r, a1i, p_r, p_i, cre, cim = tables
    full = lambda shape: pl.BlockSpec(shape, lambda i: (0,) * len(shape))
    return pl.pallas_call(
        _s5_kernel,
        grid=(nt // ts,),
        in_specs=[pl.BlockSpec((rows, D_SSM), lambda i: (i, 0)),
                  full((2, SUBLANES, D_STATE)),
                  full(bre.shape), full(bim.shape),
                  full((SUBLANES, D_STATE)), full((SUBLANES, D_STATE)),
                  full((SUBLANES, D_STATE)), full((SUBLANES, D_STATE)),
                  full(cre.shape), full(cim.shape),
                  full((1, D_SSM)), full((D_SSM, D_SSM))],
        out_specs=[pl.BlockSpec((B, ts, D_SSM), lambda i: (0, i, 0)),
                   full((2, SUBLANES, D_STATE))],
        out_shape=[jax.ShapeDtypeStruct((B, nt, D_SSM), BF16),
                   jax.ShapeDtypeStruct((2, SUBLANES, D_STATE), F32)],
        scratch_shapes=[pltpu.VMEM((rows, D_STATE), F32), pltpu.VMEM((rows, D_STATE), F32),
                        pltpu.VMEM((SUBLANES, D_STATE), F32), pltpu.VMEM((SUBLANES, D_STATE), F32),
                        pltpu.VMEM((D_SSM // LANES, rows, LANES), F32)],
        compiler_params=pltpu.CompilerParams(
            dimension_semantics=("arbitrary",), vmem_limit_bytes=VMEM_LIMIT),
        name="s5",
    )(u_sb.reshape(nt * B, D_SSM), carry, bre, bim, a1r, a1i, p_r, p_i, cre, cim, d_skip, w_glu)


MOBA_PAIR = 2 * MOBA_BLOCK


def _moba_kernel(q0, q_ref, k_ref, v_ref, o_ref, kmean, kaug_a, kaug_b, vaug_a, vaug_b, qaug,
                 m_s, acc_s, s_buf):
    last = pl.program_id(2) + q0 // 2
    nb = k_ref.shape[0] // MOBA_BLOCK
    nbp = kmean.shape[0]
    lane = lax.broadcasted_iota(jnp.int32, (1, LANES), 1)
    head_a = lane < HEAD_DIM

    @pl.when(pl.program_id(2) == 0)
    def _():
        kmean[...] = jnp.zeros_like(kmean)
        for j in range(nb):
            rows = pl.ds(j * MOBA_BLOCK, MOBA_BLOCK)
            kj = k_ref[rows, :].astype(F32)
            vj = v_ref[rows, :].astype(F32)
            kmean[j:j + 1, :] = jnp.sum(kj, axis=0, keepdims=True) * (1.0 / MOBA_BLOCK)
            kaug_a[rows, :] = jnp.where(head_a, kj, jnp.where(lane - HEAD_DIM == j, 1.0, 0.0)).astype(BF16)
            kaug_b[rows, :] = jnp.where(head_a, jnp.where(lane == j, 1.0, 0.0), kj).astype(BF16)
            vaug_a[rows, :] = jnp.where(head_a, vj, 1.0).astype(BF16)
            vaug_b[rows, :] = jnp.where(head_a, 1.0, vj).astype(BF16)
        blk_row = lax.broadcasted_iota(jnp.int32, (nbp, MOBA_BLOCK), 0)
        for t in range(q_ref.shape[0] // MOBA_BLOCK):
            qt = q0 + t
            qf = q_ref[t * MOBA_BLOCK:(t + 1) * MOBA_BLOCK, :].astype(F32)
            for hd, is_a in enumerate((True, False)):
                mine = head_a if is_a else jnp.logical_not(head_a)
                q_own = jnp.where(mine, qf, 0.0)
                g = _dot_nt(kmean[...], q_own, precision=HIGHEST)
                g = jnp.where(blk_row < qt, g, NEG)
                sel = jnp.zeros(g.shape, F32)
                for _ in range(MOBA_TOPK):
                    m = jnp.max(g, axis=0, keepdims=True)
                    idx = jnp.min(jnp.where(g == m, blk_row, nbp), axis=0, keepdims=True)
                    hit = blk_row == idx
                    sel = jnp.where(hit, jnp.where(idx < qt, 1.0, 0.0), sel)
                    g = jnp.where(hit, -jnp.inf, g)
                bias_t = jnp.where(sel > 0.0, 0.0, jnp.where(blk_row == qt, 0.0, NEG))
                bias_t = jnp.concatenate([bias_t, jnp.full((LANES - nbp, MOBA_BLOCK), NEG, F32)], axis=0)
                bias = jnp.transpose(bias_t)
                if is_a:
                    bias = pltpu.roll(bias, HEAD_DIM, 1)
                qaug[hd, t * MOBA_BLOCK:(t + 1) * MOBA_BLOCK, :] = jnp.where(mine, qf, bias).astype(BF16)

    tile_rows = pl.ds(pl.multiple_of(pl.program_id(2) * MOBA_PAIR, MOBA_PAIR), MOBA_PAIR)
    q_augs = [qaug[0, tile_rows, :], qaug[1, tile_rows, :]]

    m_s[...] = jnp.full(m_s.shape, -jnp.inf, F32)
    acc_s[...] = jnp.zeros_like(acc_s)
    qpos = last * MOBA_PAIR + lax.broadcasted_iota(jnp.int32, (MOBA_PAIR, MOBA_PAIR), 0)
    col = lax.broadcasted_iota(jnp.int32, (MOBA_PAIR, MOBA_PAIR), 1)

    def kv_rows(jj):
        return pl.ds(pl.multiple_of(jj * MOBA_PAIR, MOBA_PAIR), MOBA_PAIR)

    def scores(jj, slot):
        for hd, kaug in enumerate((kaug_a, kaug_b)):
            s_buf[slot, hd] = _dot_nt(q_augs[hd], kaug[kv_rows(jj), :])

    def softmax_pv(jj, slot, causal):
        for hd, vaug in enumerate((vaug_a, vaug_b)):
            s = s_buf[slot, hd]
            if causal:
                s = jnp.where(jj * MOBA_PAIR + col <= qpos, s, NEG)
            m_old = m_s[hd]
            m_new = jnp.maximum(m_old, jnp.max(s, axis=-1, keepdims=True))
            alpha = jnp.exp(m_old - m_new)
            p = jnp.exp(s - m_new)
            m_s[hd] = m_new
            acc_s[hd] = alpha * acc_s[hd] + _dot(p.astype(BF16), vaug[kv_rows(jj), :])

    scores(0, 0)

    def body(k, _):
        scores(2 * k + 1, 1)
        softmax_pv(2 * k, 0, False)
        scores(2 * k + 2, 0)
        softmax_pv(2 * k + 1, 1, False)
        return 0

    lax.fori_loop(0, last // 2, body, 0)

    @pl.when(last % 2 == 0)
    def _():
        softmax_pv(last, 0, True)

    @pl.when(last % 2 == 1)
    def _():
        scores(last, 1)
        softmax_pv(last - 1, 0, False)
        softmax_pv(last, 1, True)
    acc_a, acc_b = acc_s[0], acc_s[1]
    o_ref[...] = jnp.where(head_a, acc_a / pltpu.roll(acc_a, HEAD_DIM, 1),
                           acc_b / pltpu.roll(acc_b, HEAD_DIM, 1)).astype(BF16)


def _moba(q, k, v, q0):
    B = q.shape[0]
    nq = q.shape[1] // MOBA_BLOCK
    skv = (q0 + nq) * MOBA_BLOCK
    nb = skv // MOBA_BLOCK
    assert nb <= HEAD_DIM and nb % 2 == 0 and skv <= k.shape[1]
    nbp = -(-nb // SUBLANES) * SUBLANES
    assert q0 % 2 == 0 and nq % 2 == 0
    blk = pl.BlockSpec((None, MOBA_PAIR, LANES), lambda b, h, i: (b, i, h))
    seq = pl.BlockSpec((None, skv, LANES), lambda b, h, i: (b, 0, h))
    return pl.pallas_call(
        functools.partial(_moba_kernel, q0),
        grid=(B, D_ATT // LANES, nq // 2),
        in_specs=[pl.BlockSpec((None, nq * MOBA_BLOCK, LANES), lambda b, h, i: (b, 0, h)), seq, seq],
        out_specs=blk,
        out_shape=jax.ShapeDtypeStruct(q.shape, BF16),
        scratch_shapes=[pltpu.VMEM((nbp, LANES), F32),
                        pltpu.VMEM((skv, LANES), BF16), pltpu.VMEM((skv, LANES), BF16),
                        pltpu.VMEM((skv, LANES), BF16), pltpu.VMEM((skv, LANES), BF16),
                        pltpu.VMEM((2, nq * MOBA_BLOCK, LANES), BF16),
                        pltpu.VMEM((2, MOBA_PAIR, 1), F32),
                        pltpu.VMEM((2, MOBA_PAIR, LANES), F32),
                        pltpu.VMEM((2, 2, MOBA_PAIR, MOBA_PAIR), F32)],
        compiler_params=pltpu.CompilerParams(
            dimension_semantics=("parallel", "parallel", "arbitrary"), vmem_limit_bytes=VMEM_LIMIT),
        name="moba",
    )(q, k, v)


MERGE_TS = 256


def _bf16_bits(x):
    b = pltpu.bitcast(x, jnp.int32)
    r = b + 0x7FFF + (lax.shift_right_logical(b, 16) & 1)
    return lax.shift_right_logical(r, 16)


def _merge_kernel(x_ref, ys_ref, at_ref, ga_ref, gb_ref, wa_ref, wb_ref, wo_ref, g_ref,
                  wq_ref, k1_ref, k2_ref, x1_ref, hw_ref, idx_ref, gate_ref, sc_ref):
    ya = _dot(ys_ref[...], wa_ref[...])
    yb = _dot(at_ref[...], wb_ref[...])
    merged = ga_ref[...].astype(F32) * ya + gb_ref[...].astype(F32) * yb
    x1 = x_ref[...] + _dot(merged.astype(BF16), wo_ref[...])
    x1_ref[...] = x1
    hq = _rms(x1, g_ref[...])
    half = D_MODEL // 2
    hw_ref[...] = _bf16_bits(hq[:, :half]) | lax.shift_left(_bf16_bits(hq[:, half:]), 16)
    qp = _dot(hq.astype(BF16), wq_ref[...])
    for h in range(PEER_HEADS):
        o = h * PEER_QDIM
        sc_ref[2 * h] = _dot_nt(k1_ref[h], qp[:, o:o + PEER_HALF], precision=HIGHEST)
        sc_ref[2 * h + 1] = _dot_nt(k2_ref[h], qp[:, o + PEER_HALF:o + PEER_QDIM], precision=HIGHEST)
    _topk_kernel(sc_ref, idx_ref, gate_ref)


def _merge(x, ys, att, ga, gb, t0, w_proj_ssm, w_proj_att, w_out, g_ffn, peer_w_q, keys1, keys2):
    B, nt = ys.shape[0], ys.shape[1]
    ts = min(MERGE_TS, nt)
    nblk = nt // ts
    i0 = t0 // ts
    tok = lambda d: pl.BlockSpec((None, ts, d), lambda b, i: (b, i, 0))
    row = lambda d: pl.BlockSpec((ts, d), lambda b, i: (b * nblk + i, 0))
    full = lambda shape: pl.BlockSpec(shape, lambda b, i: (0,) * len(shape))
    qd = PEER_HEADS * PEER_QDIM
    return pl.pallas_call(
        _merge_kernel,
        grid=(B, nblk),
        in_specs=[pl.BlockSpec((None, ts, D_MODEL), lambda b, i: (b, i0 + i, 0)),
                  tok(D_SSM), tok(D_ATT), tok(D_MODEL), tok(D_MODEL),
                  full((D_SSM, D_MODEL)), full((D_ATT, D_MODEL)), full((D_MODEL, D_MODEL)),
                  full((1, D_MODEL)), full((D_MODEL, qd)),
                  full((PEER_HEADS, PEER_KEYS, PEER_HALF)), full((PEER_HEADS, PEER_KEYS, PEER_HALF))],
        out_specs=[row(D_MODEL), row(D_MODEL // 2), row(PEER_SEL), row(PEER_SEL)],
        out_shape=[jax.ShapeDtypeStruct((B * nt, D_MODEL), F32),
                   jax.ShapeDtypeStruct((B * nt, D_MODEL // 2), jnp.int32),
                   jax.ShapeDtypeStruct((B * nt, PEER_SEL), jnp.int32),
                   jax.ShapeDtypeStruct((B * nt, PEER_SEL), F32)],
        scratch_shapes=[pltpu.VMEM((2 * PEER_HEADS, PEER_KEYS, ts), F32)],
        compiler_params=pltpu.CompilerParams(
            dimension_semantics=("parallel", "parallel"), vmem_limit_bytes=VMEM_LIMIT),
        name="merge",
    )(x, ys, att, ga, gb, w_proj_ssm, w_proj_att, w_out, g_ffn, peer_w_q, keys1, keys2)


def _top_rows(s, row, k):
    vals, idxs = [], []
    for _ in range(k):
        m = jnp.max(s, axis=0, keepdims=True)
        idx = jnp.min(jnp.where(s == m, row, s.shape[0]), axis=0, keepdims=True)
        vals.append(m)
        idxs.append(idx)
        s = jnp.where(row == idx, -jnp.inf, s)
    return vals, idxs


def _stack_rows(rows, row16):
    acc = jnp.zeros(row16.shape, rows[0].dtype)
    for r, v in enumerate(rows):
        acc = jnp.where(row16 == r, v, acc)
    return acc


def _topk_kernel(sc_ref, idx_ref, gate_ref):
    ts = sc_ref.shape[-1]
    row = lax.broadcasted_iota(jnp.int32, (PEER_KEYS, ts), 0).astype(F32)
    row16 = lax.broadcasted_iota(jnp.int32, (PEER_TOPK, ts), 0)
    row8 = lax.broadcasted_iota(jnp.int32, (SUBLANES, ts), 0)
    counts = [PEER_TOPK // (i + 1) for i in range(PEER_TOPK)]
    heights = [PEER_TOPK if c > SUBLANES else SUBLANES for c in counts]
    n_cand = sum(heights)
    rowc = lax.broadcasted_iota(jnp.int32, (n_cand, ts), 0).astype(F32)
    gate_rows, eid_rows = [], []
    for h in range(PEER_HEADS):
        v1, i1 = _top_rows(sc_ref[2 * h], row, PEER_TOPK)
        v2, i2 = _top_rows(sc_ref[2 * h + 1], row, PEER_TOPK)
        v2s = _stack_rows(v2, row16)
        i2s = _stack_rows(i2, row16)
        cand, eid = [], []
        for i in range(PEER_TOPK):
            n = heights[i]
            cand.append(jnp.where((row16 if n == PEER_TOPK else row8) < counts[i],
                                  v1[i] + v2s[:n], -jnp.inf))
            eid.append(i1[i] * PEER_KEYS + i2s[:n])
        cand = jnp.concatenate(cand, axis=0)
        eid = jnp.concatenate(eid, axis=0)
        tops, picks = [], []
        for _ in range(PEER_TOPK):
            m = jnp.max(cand, axis=0, keepdims=True)
            pos = jnp.min(jnp.where(cand == m, rowc, n_cand), axis=0, keepdims=True)
            hit = rowc == pos
            picks.append(jnp.max(jnp.where(hit, eid, -1.0), axis=0, keepdims=True))
            tops.append(m)
            cand = jnp.where(hit, -jnp.inf, cand)
        top = _stack_rows(tops, row16)
        p = jnp.exp(top - jnp.max(top, axis=0, keepdims=True))
        gate_rows.append(p / jnp.sum(p, axis=0, keepdims=True))
        eid_rows.append(_stack_rows(picks, row16))
    gate_ref[...] = jnp.transpose(jnp.concatenate(gate_rows, axis=0))
    idx_ref[...] = jnp.transpose(jnp.concatenate(eid_rows, axis=0)).astype(jnp.int32)


SC_CORES = 2
SC_SUBCORES = 16
SC_LANES = 16
SC_WORKERS = SC_CORES * SC_SUBCORES
PEER_CH = SC_LANES
PEER_NCH = PEER_SEL // PEER_CH
PEER_WORDS = D_MODEL // 2
PEER_NWG = PEER_WORDS // SC_LANES
PEER_RING = 4
PEER_QUAD = 4
HI_MASK = -65536
GELU_C = 0.7978845608028654


def _gelu_tanh_via_exp(x):
    z = GELU_C * (x + 0.044715 * (x * x * x))
    t = 1.0 - 2.0 / (jnp.exp(2.0 * z) + 1.0)
    return 0.5 * x * (1.0 + t)


def _unpack_pair(w):
    lo = plsc.bitcast(lax.shift_left(w, 16), F32)
    hi = plsc.bitcast(lax.bitwise_and(w, HI_MASK), F32)
    return lo, hi


def _peer_sc_body(idx_hbm, gate_hbm, h_hbm, uv_hbm, after_hbm, o_hbm,
                  idx_v, gate_v, h_v, buf, out_v, gsem, msem, osem):
    n_tok = o_hbm.shape[0] // SC_WORKERS
    base = (lax.axis_index("s") * SC_CORES + lax.axis_index("c")) * n_tok
    lane = lax.iota(jnp.int32, SC_LANES)
    zero_rows = jnp.zeros((SC_LANES,), jnp.int32)

    def meta_copies(tok, s):
        return (pltpu.make_async_copy(idx_hbm.at[tok], idx_v.at[s], msem.at[s]),
                pltpu.make_async_copy(gate_hbm.at[tok], gate_v.at[s], msem.at[s]),
                pltpu.make_async_copy(h_hbm.at[tok], h_v.at[s], msem.at[s]))

    def gather(slot, rows):
        return pltpu.make_async_copy(uv_hbm.at[rows], buf.at[slot], gsem.at[slot])

    def token(t, carry):
        s = t % 2
        tok = base + t
        nxt = base + jnp.minimum(t + 1, n_tok - 1)
        for cp in meta_copies(nxt, 1 - s):
            cp.start()

        @pl.when(t >= 2)
        def _():
            pltpu.make_async_copy(out_v.at[s], o_hbm.at[tok], osem.at[s]).wait()

        def chunk(c, carry):
            slot = c % PEER_RING
            gather(slot, zero_rows).wait()

            def dot_step(q, accs):
                cols = [pl.ds(pl.multiple_of((q * PEER_QUAD + j) * SC_LANES, SC_LANES), SC_LANES)
                        for j in range(PEER_QUAD)]
                hs = [plsc.bitcast(h_v[s, col], BF16) for col in cols]
                out = []
                for r in range(PEER_CH):
                    p = plsc.bitcast(buf[slot, r, cols[0]], BF16) * hs[0]
                    for j in range(1, PEER_QUAD):
                        p = p + plsc.bitcast(buf[slot, r, cols[j]], BF16) * hs[j]
                    lo, hi = _unpack_pair(plsc.bitcast(p, jnp.int32))
                    out.append(accs[r] + lo + hi)
                return tuple(out)

            accs = lax.fori_loop(0, PEER_NWG // PEER_QUAD, dot_step,
                                 tuple(jnp.zeros((SC_LANES,), F32) for _ in range(PEER_CH)))
            tot = jnp.zeros((SC_LANES,), F32)
            for r in range(PEER_CH):
                tot = jnp.where(lane == r, jnp.sum(accs[r]), tot)
            rows = pl.ds(pl.multiple_of(c * PEER_CH, PEER_CH), PEER_CH)
            wvec = gate_v[s, rows] * _gelu_tanh_via_exp(tot)
            ws = []
            for r in range(PEER_CH):
                w = wvec.at[jnp.full((SC_LANES,), r, jnp.int32)].get(mode="promise_in_bounds")
                ws.append(plsc.pack(w, w, format=plsc.PackFormat.INTERLEAVED,
                                    preferred_element_type=BF16))
            first = c == 0

            @plsc.parallel_loop(0, PEER_NWG, unroll=2)
            def acc_step(g):
                col = pl.ds(pl.multiple_of(g * SC_LANES, SC_LANES), SC_LANES)
                col_v = pl.ds(pl.multiple_of(PEER_WORDS + g * SC_LANES, SC_LANES), SC_LANES)
                o_lo = jnp.where(first, 0.0, out_v[s, col])
                o_hi = jnp.where(first, 0.0, out_v[s, col_v])
                for r0 in range(0, PEER_CH, PEER_QUAD):
                    p = plsc.bitcast(buf[slot, r0, col_v], BF16) * ws[r0]
                    for r in range(r0 + 1, r0 + PEER_QUAD):
                        p = p + plsc.bitcast(buf[slot, r, col_v], BF16) * ws[r]
                    lo, hi = _unpack_pair(plsc.bitcast(p, jnp.int32))
                    o_lo = o_lo + lo
                    o_hi = o_hi + hi
                out_v[s, col] = o_lo
                out_v[s, col_v] = o_hi

            @pl.when(c == PEER_NCH - PEER_RING)
            def _():
                for cp in meta_copies(nxt, 1 - s):
                    cp.wait()

            ahead = c + PEER_RING
            src = jnp.where(ahead < PEER_NCH, s, 1 - s)
            nrows = idx_v[src, pl.ds(pl.multiple_of((ahead % PEER_NCH) * PEER_CH, PEER_CH), PEER_CH)]
            gather(slot, nrows).start()
            return carry

        lax.fori_loop(0, PEER_NCH, chunk, 0)
        pltpu.make_async_copy(out_v.at[s], o_hbm.at[tok], osem.at[s]).start()
        return carry

    for cp in meta_copies(base, 0):
        cp.start()
    for cp in meta_copies(base, 0):
        cp.wait()
    for c in range(PEER_RING):
        gather(c, idx_v[0, pl.ds(c * PEER_CH, PEER_CH)]).start()
    lax.fori_loop(0, n_tok, token, 0)
    for c in range(PEER_RING):
        gather(c, zero_rows).wait()
    for s in range(2):
        pltpu.make_async_copy(out_v.at[s], o_hbm.at[base], osem.at[s]).wait()


def _pack_bf16_pairs(tab):
    b = lax.bitcast_convert_type(tab.astype(BF16), jnp.uint16).astype(jnp.uint32)
    half = tab.shape[1] // 2
    return lax.bitcast_convert_type(b[:, :half] | (b[:, half:] << 16), jnp.int32)


def _peer(idx, h_words, gates, uv_words, after):
    T = h_words.shape[0]
    assert T % (2 * SC_WORKERS) == 0
    mesh = plsc.VectorSubcoreMesh(core_axis_name="c", subcore_axis_name="s",
                                  num_cores=SC_CORES, num_subcores=SC_SUBCORES)
    return pl.kernel(
        _peer_sc_body,
        out_type=jax.ShapeDtypeStruct((T, D_MODEL), F32),
        mesh=mesh,
        scratch_types=[
            pltpu.VMEM((2, PEER_SEL), jnp.int32), pltpu.VMEM((2, PEER_SEL), F32),
            pltpu.VMEM((2, PEER_WORDS), jnp.int32),
            pltpu.VMEM((PEER_RING, PEER_CH, 2 * PEER_WORDS), jnp.int32),
            pltpu.VMEM((2, D_MODEL), F32),
            pltpu.SemaphoreType.DMA((PEER_RING,)),
            pltpu.SemaphoreType.DMA((2,)), pltpu.SemaphoreType.DMA((2,)),
        ],
        compiler_params=pltpu.CompilerParams(needs_layout_passes=False),
        name="peer_sc",
    )(idx, gates, h_words, uv_words, after)


FINAL_TS = 256


def _final_kernel(x1_ref, pe_ref, p_ref, gp_ref, wg_ref, wp_ref, gf_ref, o_ref):
    x2 = x1_ref[...] + pe_ref[...]
    e = _dot(p_ref[...].astype(BF16), wp_ref[...])
    gate = jax.nn.sigmoid(_dot(_rms(x2, gp_ref[...]).astype(BF16), wg_ref[...]))
    o_ref[...] = _rms(x2 + gate * e, gf_ref[...])


def _final(x1, peer_out, p, t0, nt, g_ple, ple_w_gate, ple_w_proj, g_final):
    B = p.shape[0]
    ts = min(FINAL_TS, nt)
    nblk = nt // ts
    i0 = t0 // ts
    row = lambda d: pl.BlockSpec((ts, d), lambda b, i: (b * nblk + i, 0))
    full = lambda shape: pl.BlockSpec(shape, lambda b, i: (0,) * len(shape))
    return pl.pallas_call(
        _final_kernel,
        grid=(B, nblk),
        in_specs=[row(D_MODEL), row(D_MODEL),
                  pl.BlockSpec((None, ts, D_PLE), lambda b, i: (b, i0 + i, 0)),
                  full((1, D_MODEL)), full((D_MODEL, D_MODEL)), full((D_PLE, D_MODEL)),
                  full((1, D_MODEL))],
        out_specs=pl.BlockSpec((None, ts, D_MODEL), lambda b, i: (b, i, 0)),
        out_shape=jax.ShapeDtypeStruct((B, nt, D_MODEL), F32),
        compiler_params=pltpu.CompilerParams(
            dimension_semantics=("parallel", "parallel"), vmem_limit_bytes=VMEM_LIMIT),
        name="final",
    )(x1, peer_out, p, g_ple, ple_w_gate, ple_w_proj, g_final)


CHUNK_STEPS = (512, 512, 1024, 1024, 1024, 1024, 1024, 1024, 512, 512)


def kernel(x, p, positions, g_mix, w_in, ssm_log_dt, ssm_a_re, ssm_a_im, ssm_b_re, ssm_b_im,
           ssm_c_re, ssm_c_im, ssm_d, ssm_w_glu, w_proj_ssm, w_proj_att, w_out, g_ffn,
           peer_w_q, peer_keys1, peer_keys2, peer_u, peer_v, g_ple, ple_w_gate, ple_w_proj,
           g_final):
    B, S, _ = x.shape
    assert w_in.shape[0] == 1, "the final rmsnorm is fused into the single layer's last stage"
    steps = CHUNK_STEPS if sum(CHUNK_STEPS) == S else (S,)
    i = 0
    tables = _s5_tables(ssm_log_dt[i], ssm_a_re[i], ssm_a_im[i], ssm_b_re[i], ssm_b_im[i],
                        ssm_c_re[i], ssm_c_im[i])
    w_in_b, w_glu_b = w_in[i].astype(BF16), ssm_w_glu[i].astype(BF16)
    d_skip = ssm_d[i].reshape(1, D_SSM).astype(F32)
    merge_w = (w_proj_ssm[i].astype(BF16), w_proj_att[i].astype(BF16), w_out[i].astype(BF16),
               g_ffn[i].reshape(1, D_MODEL), peer_w_q[i].astype(BF16), peer_keys1[i], peer_keys2[i])
    final_w = (g_ple[i].reshape(1, D_MODEL), ple_w_gate[i].astype(BF16),
               ple_w_proj[i].astype(BF16), g_final.reshape(1, D_MODEL))
    uv_words = jnp.concatenate([_pack_bf16_pairs(peer_u[i]), _pack_bf16_pairs(peer_v[i])], axis=1)
    k_all = jnp.zeros((B, S, D_ATT), BF16)
    v_all = jnp.zeros((B, S, D_ATT), BF16)
    carry = jnp.zeros((2, SUBLANES, D_STATE), F32)
    outs = []
    t0 = 0
    after = (carry, carry)
    peer_prev = carry
    for nt in steps:
        u_sb, q, k, v, ga, gb = _in_proj(x, positions, g_mix[i], w_in_b, t0, nt, after)
        k_all = lax.dynamic_update_slice(k_all, k, (0, t0, 0))
        v_all = lax.dynamic_update_slice(v_all, v, (0, t0, 0))
        ys, carry = _s5(u_sb, carry, tables, d_skip, w_glu_b, B)
        att = _moba(q, k_all, v_all, t0 // MOBA_BLOCK)
        x1, h_words, idx, gates = _merge(x, ys, att, ga, gb, t0, *merge_w)
        after = (gates, outs[-3] if len(outs) > 2 else carry)
        peer_out = _peer(idx, h_words, gates, uv_words, peer_prev)
        peer_prev = peer_out
        outs.append(_final(x1, peer_out, p[i], t0, nt, *final_w))
        t0 += nt
    return jnp.concatenate(outs, axis=1)
```

```python
import functools
import math

import jax
import jax.numpy as jnp
from jax import lax
from jax.experimental import pallas as pl
from jax.experimental.pallas import tpu as pltpu
from jax.experimental.pallas import tpu_sc as plsc

F32 = jnp.float32
BF16 = jnp.bfloat16

D_MODEL = 1024
D_SSM = 512
SSM_GROUP = 16
SSM_GROUPS = 32
SSM_STATE = 64
D_STATE = SSM_GROUPS * SSM_STATE
N_HEADS = 8
HEAD_DIM = 64
D_ATT = 512
ROT_DIM = 16
ROPE_THETA = 500000.0
MOBA_BLOCK = 256
MOBA_TOPK = 3
PEER_HEADS = 8
PEER_KEYS = 128
PEER_QDIM = 256
PEER_HALF = 128
PEER_TOPK = 16
PEER_SEL = PEER_HEADS * PEER_TOPK
D_PLE = 256
EPS = 1e-6
NEG = -1e30
LANES = 128
SUBLANES = 8
VMEM_LIMIT = 48 * 1024 * 1024
HIGHEST = lax.Precision.HIGHEST


def _rms(x, g):
    return x * lax.rsqrt(jnp.mean(x * x, axis=-1, keepdims=True) + EPS) * g


def _dot(a, b):
    return jnp.dot(a, b, preferred_element_type=F32)


def _dot_nt(a, b, precision=None):
    return lax.dot_general(a, b, (((1,), (1,)), ((), ())), precision=precision,
                           preferred_element_type=F32)


IN_TS = 512


def _in_proj_kernel(x_ref, pos_ref, g_ref, w_ref, invf_ref, after_a, after_b,
                    u_ref, q_ref, k_ref, v_ref, ga_ref, gb_ref):
    del after_a, after_b
    h = _rms(x_ref[...], g_ref[...]).astype(BF16)

    def proj(lo, hi):
        return _dot(h, w_ref[:, lo:hi])

    u_ref[...] = proj(0, D_SSM).astype(BF16)
    ang = pos_ref[...].astype(F32) * invf_ref[...]
    cos = jnp.cos(ang)
    sin = jnp.sin(ang)
    lane = lax.broadcasted_iota(jnp.int32, (1, LANES), 1) % HEAD_DIM
    half = ROT_DIM // 2
    sin_hi = jnp.where((lane >= half) & (lane < ROT_DIM), sin, 0.0)
    sin_lo = jnp.where(lane < half, -sin, 0.0)
    reps = D_ATT // LANES
    cos4 = jnp.concatenate([cos] * reps, axis=1)
    sin_hi4 = jnp.concatenate([sin_hi] * reps, axis=1)
    sin_lo4 = jnp.concatenate([sin_lo] * reps, axis=1)

    def rope(t):
        return (t * cos4 + pltpu.roll(t, half, 1) * sin_hi4
                + pltpu.roll(t, D_ATT - half, 1) * sin_lo4)

    q = rope(proj(D_SSM, D_SSM + D_ATT))
    q_ref[...] = (q * (HEAD_DIM ** -0.5)).astype(BF16)
    k_ref[...] = rope(proj(D_SSM + D_ATT, D_SSM + 2 * D_ATT)).astype(BF16)
    v_ref[...] = proj(D_SSM + 2 * D_ATT, D_SSM + 3 * D_ATT).astype(BF16)
    o = D_SSM + 3 * D_ATT
    ga_ref[...] = jax.nn.sigmoid(proj(o, o + D_MODEL)).astype(BF16)
    gb_ref[...] = jax.nn.sigmoid(proj(o + D_MODEL, o + 2 * D_MODEL)).astype(BF16)


def _in_proj(x, positions, g_mix, w_in, t0, nt, after):
    B, S, _ = x.shape
    ts = min(IN_TS, nt)
    assert nt % ts == 0 and t0 % ts == 0
    i0 = t0 // ts
    inv_freq = ROPE_THETA ** (-jnp.arange(0, ROT_DIM, 2, dtype=F32) / ROT_DIM)
    lane = jnp.arange(LANES) % HEAD_DIM
    invf = jnp.where(lane < ROT_DIM, inv_freq[lane % (ROT_DIM // 2)], 0.0).reshape(1, LANES)
    d_in = w_in.shape[1]
    src = lambda d: pl.BlockSpec((None, ts, d), lambda b, i: (b, i0 + i, 0))
    tok = lambda d: pl.BlockSpec((None, ts, d), lambda b, i: (b, i, 0))
    full = lambda shape: pl.BlockSpec(shape, lambda b, i: (0,) * len(shape))
    return pl.pallas_call(
        _in_proj_kernel,
        grid=(B, nt // ts),
        in_specs=[src(D_MODEL), src(1), full((1, D_MODEL)), full((D_MODEL, d_in)), full((1, LANES)),
                  pl.BlockSpec(memory_space=pl.ANY), pl.BlockSpec(memory_space=pl.ANY)],
        out_specs=[pl.BlockSpec((ts, D_SSM), lambda b, i: (i, b)),
                   tok(D_ATT), tok(D_ATT), tok(D_ATT), tok(D_MODEL), tok(D_MODEL)],
        out_shape=[jax.ShapeDtypeStruct((nt, B * D_SSM), BF16),
                   jax.ShapeDtypeStruct((B, nt, D_ATT), BF16),
                   jax.ShapeDtypeStruct((B, nt, D_ATT), BF16),
                   jax.ShapeDtypeStruct((B, nt, D_ATT), BF16),
                   jax.ShapeDtypeStruct((B, nt, D_MODEL), BF16),
                   jax.ShapeDtypeStruct((B, nt, D_MODEL), BF16)],
        compiler_params=pltpu.CompilerParams(
            dimension_semantics=("parallel", "parallel"), vmem_limit_bytes=VMEM_LIMIT),
        name="in_proj",
    )(x, positions.reshape(B, S, 1), g_mix.reshape(1, D_MODEL), w_in, invf, *after)


S5_TS = 128
S5_BATCH = 4
S5_COLS = 512


def _s5_kernel(u_ref, c0_ref, bre_ref, bim_ref, a1r_ref, a1i_ref, pr_ref, pi_ref,
               cre_ref, cim_ref, d_ref, wglu_ref, y_ref, c1_ref,
               xr, xi, cr, ci, ysc):
    rows = xr.shape[0]
    ts = rows // S5_BATCH

    @pl.when(pl.program_id(0) == 0)
    def _():
        cr[...] = c0_ref[0]
        ci[...] = c0_ref[1]

    u = u_ref[...]
    for cb in range(D_STATE // S5_COLS):
        sl = slice(cb * S5_COLS, (cb + 1) * S5_COLS)
        u_cb = u[:, cb * LANES:(cb + 1) * LANES]
        xr[:, sl] = _dot(u_cb, bre_ref[cb])
        xi[:, sl] = _dot(u_cb, bim_ref[cb])

    hi_rows = lax.broadcasted_iota(jnp.int32, (SUBLANES, S5_COLS), 0) >= S5_BATCH
    for cb in range(D_STATE // S5_COLS):
        sl = slice(cb * S5_COLS, (cb + 1) * S5_COLS)
        a_r, a_i = a1r_ref[:, sl], a1i_ref[:, sl]
        p_r, p_i = pr_ref[:, sl], pi_ref[:, sl]

        def body(t, carry):
            c_r, c_i = carry
            r0 = pl.multiple_of(t * SUBLANES, SUBLANES)
            x_r = xr[pl.ds(r0, SUBLANES), sl]
            x_i = xi[pl.ds(r0, SUBLANES), sl]
            s_r = pltpu.roll(x_r, S5_BATCH, 0)
            s_i = pltpu.roll(x_i, S5_BATCH, 0)
            h_r = x_r + (a_r * s_r - a_i * s_i) + (p_r * c_r - p_i * c_i)
            h_i = x_i + (a_r * s_i + a_i * s_r) + (p_r * c_i + p_i * c_r)
            xr[pl.ds(r0, SUBLANES), sl] = h_r
            xi[pl.ds(r0, SUBLANES), sl] = h_i
            n_r = jnp.where(hi_rows, h_r, pltpu.roll(h_r, S5_BATCH, 0))
            n_i = jnp.where(hi_rows, h_i, pltpu.roll(h_i, S5_BATCH, 0))
            return n_r, n_i

        c_r, c_i = lax.fori_loop(0, rows // SUBLANES, body, (cr[:, sl], ci[:, sl]), unroll=2)
        cr[:, sl] = c_r
        ci[:, sl] = c_i

    y = jnp.concatenate(
        [_dot(xr[:, cb * S5_COLS:(cb + 1) * S5_COLS].astype(BF16), cre_ref[cb])
         - _dot(xi[:, cb * S5_COLS:(cb + 1) * S5_COLS].astype(BF16), cim_ref[cb])
         for cb in range(D_STATE // S5_COLS)], axis=1) + d_ref[...] * u.astype(F32)
    y = jax.nn.gelu(y)
    y = y * jax.nn.sigmoid(_dot(y.astype(BF16), wglu_ref[...]))
    for c in range(D_SSM // LANES):
        ysc[c] = y[:, c * LANES:(c + 1) * LANES]
    for b in range(S5_BATCH):
        for c in range(D_SSM // LANES):
            y_ref[b, :, c * LANES:(c + 1) * LANES] = (
                ysc[c, pl.ds(b, ts, stride=S5_BATCH), :].astype(BF16))

    @pl.when(pl.program_id(0) == pl.num_programs(0) - 1)
    def _():
        c1_ref[0] = cr[...]
        c1_ref[1] = ci[...]


def _s5_tables(log_dt, a_re, a_im, b_re, b_im, c_re, c_im):
    dt = jnp.exp(log_dt.astype(F32))[:, None]
    ar, ai = a_re.astype(F32), a_im.astype(F32)
    mag = jnp.exp(dt * ar)
    abar_re, abar_im = mag * jnp.cos(dt * ai), mag * jnp.sin(dt * ai)
    den = ar * ar + ai * ai
    nr, ni = abar_re - 1.0, abar_im
    f_re = (nr * ar + ni * ai) / den
    f_im = (ni * ar - nr * ai) / den
    br, bi = b_re.astype(F32), b_im.astype(F32)
    bb_re = f_re[..., None] * br - f_im[..., None] * bi
    bb_im = f_re[..., None] * bi + f_im[..., None] * br
    eye = jnp.eye(SSM_GROUPS, dtype=F32)

    def in_blockdiag(bb):
        return jnp.einsum('gnc,gh->gchn', bb, eye).reshape(D_SSM, D_STATE)

    def out_blockdiag(c):
        return jnp.einsum('gcn,gh->gnhc', c.astype(F32), eye).reshape(D_STATE, D_SSM)

    a_r = abar_re.reshape(1, D_STATE)
    a_i = abar_im.reshape(1, D_STATE)
    a2_r = a_r * a_r - a_i * a_i
    a2_i = 2.0 * a_r * a_i
    hi = (jnp.arange(SUBLANES) >= S5_BATCH)[:, None]
    a1r = jnp.where(hi, a_r, 0.0)
    a1i = jnp.where(hi, a_i, 0.0)
    p_r = jnp.where(hi, a2_r, a_r)
    p_i = jnp.where(hi, a2_i, a_i)
    nblk = D_STATE // S5_COLS
    cw = D_SSM // nblk

    def in_blocks(m):
        return jnp.stack([m[b * cw:(b + 1) * cw, b * S5_COLS:(b + 1) * S5_COLS] for b in range(nblk)])

    def out_blocks(m):
        return jnp.stack([m[b * S5_COLS:(b + 1) * S5_COLS, b * cw:(b + 1) * cw] for b in range(nblk)])

    return (in_blocks(in_blockdiag(bb_re)).astype(BF16), in_blocks(in_blockdiag(bb_im)).astype(BF16),
            a1r, a1i, p_r, p_i,
            out_blocks(out_blockdiag(c_re)).astype(BF16), out_blocks(out_blockdiag(c_im)).astype(BF16))


def _s5(u_sb, carry, tables, d_skip, w_glu, B):
    assert B == S5_BATCH
    nt = u_sb.shape[0]
    ts = min(S5_TS, nt)
    rows = ts * B
    bre, bim, a1r, a1i, p_r, p_i, cre, cim = tables
    full = lambda shape: pl.BlockSpec(shape, lambda i: (0,) * len(shape))
    return pl.pallas_call(
        _s5_kernel,
        grid=(nt // ts,),
        in_specs=[pl.BlockSpec((rows, D_SSM), lambda i: (i, 0)),
                  full((2, SUBLANES, D_STATE)),
                  full(bre.shape), full(bim.shape),
                  full((SUBLANES, D_STATE)), full((SUBLANES, D_STATE)),
                  full((SUBLANES, D_STATE)), full((SUBLANES, D_STATE)),
                  full(cre.shape), full(cim.shape),
                  full((1, D_SSM)), full((D_SSM, D_SSM))],
        out_specs=[pl.BlockSpec((B, ts, D_SSM), lambda i: (0, i, 0)),
                   full((2, SUBLANES, D_STATE))],
        out_shape=[jax.ShapeDtypeStruct((B, nt, D_SSM), BF16),
                   jax.ShapeDtypeStruct((2, SUBLANES, D_STATE), F32)],
        scratch_shapes=[pltpu.VMEM((rows, D_STATE), F32), pltpu.VMEM((rows, D_STATE), F32),
                        pltpu.VMEM((SUBLANES, D_STATE), F32), pltpu.VMEM((SUBLANES, D_STATE), F32),
                        pltpu.VMEM((D_SSM // LANES, rows, LANES), F32)],
        compiler_params=pltpu.CompilerParams(
            dimension_semantics=("arbitrary",), vmem_limit_bytes=VMEM_LIMIT),
        name="s5",
    )(u_sb.reshape(nt * B, D_SSM), carry, bre, bim, a1r, a1i, p_r, p_i, cre, cim, d_skip, w_glu)


MOBA_PAIR = 2 * MOBA_BLOCK


def _moba_kernel(q0, q_ref, k_ref, v_ref, o_ref, kmean, kaug_a, kaug_b, vaug_a, vaug_b, qaug,
                 m_s, acc_s, s_buf):
    last = pl.program_id(2) + q0 // 2
    nb = k_ref.shape[0] // MOBA_BLOCK
    nbp = kmean.shape[0]
    lane = lax.broadcasted_iota(jnp.int32, (1, LANES), 1)
    head_a = lane < HEAD_DIM

    @pl.when(pl.program_id(2) == 0)
    def _():
        kmean[...] = jnp.zeros_like(kmean)
        for j in range(nb):
            rows = pl.ds(j * MOBA_BLOCK, MOBA_BLOCK)
            kj = k_ref[rows, :].astype(F32)
            vj = v_ref[rows, :].astype(F32)
            kmean[j:j + 1, :] = jnp.sum(kj, axis=0, keepdims=True) * (1.0 / MOBA_BLOCK)
            kaug_a[rows, :] = jnp.where(head_a, kj, jnp.where(lane - HEAD_DIM == j, 1.0, 0.0)).astype(BF16)
            kaug_b[rows, :] = jnp.where(head_a, jnp.where(lane == j, 1.0, 0.0), kj).astype(BF16)
            vaug_a[rows, :] = jnp.where(head_a, vj, 1.0).astype(BF16)
            vaug_b[rows, :] = jnp.where(head_a, 1.0, vj).astype(BF16)
        blk_row = lax.broadcasted_iota(jnp.int32, (nbp, MOBA_BLOCK), 0)
        for t in range(q_ref.shape[0] // MOBA_BLOCK):
            qt = q0 + t
            qf = q_ref[t * MOBA_BLOCK:(t + 1) * MOBA_BLOCK, :].astype(F32)
            for hd, is_a in enumerate((True, False)):
                mine = head_a if is_a else jnp.logical_not(head_a)
                q_own = jnp.where(mine, qf, 0.0)
                g = _dot_nt(kmean[...], q_own, precision=HIGHEST)
                g = jnp.where(blk_row < qt, g, NEG)
                sel = jnp.zeros(g.shape, F32)
                for _ in range(MOBA_TOPK):
                    m = jnp.max(g, axis=0, keepdims=True)
                    idx = jnp.min(jnp.where(g == m, blk_row, nbp), axis=0, keepdims=True)
                    hit = blk_row == idx
                    sel = jnp.where(hit, jnp.where(idx < qt, 1.0, 0.0), sel)
                    g = jnp.where(hit, -jnp.inf, g)
                bias_t = jnp.where(sel > 0.0, 0.0, jnp.where(blk_row == qt, 0.0, NEG))
                bias_t = jnp.concatenate([bias_t, jnp.full((LANES - nbp, MOBA_BLOCK), NEG, F32)], axis=0)
                bias = jnp.transpose(bias_t)
                if is_a:
                    bias = pltpu.roll(bias, HEAD_DIM, 1)
                qaug[hd, t * MOBA_BLOCK:(t + 1) * MOBA_BLOCK, :] = jnp.where(mine, qf, bias).astype(BF16)

    tile_rows = pl.ds(pl.multiple_of(pl.program_id(2) * MOBA_PAIR, MOBA_PAIR), MOBA_PAIR)
    q_augs = [qaug[0, tile_rows, :], qaug[1, tile_rows, :]]

    m_s[...] = jnp.full(m_s.shape, -jnp.inf, F32)
    acc_s[...] = jnp.zeros_like(acc_s)
    qpos = last * MOBA_PAIR + lax.broadcasted_iota(jnp.int32, (MOBA_PAIR, MOBA_PAIR), 0)
    col = lax.broadcasted_iota(jnp.int32, (MOBA_PAIR, MOBA_PAIR), 1)

    def kv_rows(jj):
        return pl.ds(pl.multiple_of(jj * MOBA_PAIR, MOBA_PAIR), MOBA_PAIR)

    def scores(jj, slot):
        for hd, kaug in enumerate((kaug_a, kaug_b)):
            s_buf[slot, hd] = _dot_nt(q_augs[hd], kaug[kv_rows(jj), :])

    def softmax_pv(jj, slot, causal):
        for hd, vaug in enumerate((vaug_a, vaug_b)):
            s = s_buf[slot, hd]
            if causal:
                s = jnp.where(jj * MOBA_PAIR + col <= qpos, s, NEG)
            m_old = m_s[hd]
            m_new = jnp.maximum(m_old, jnp.max(s, axis=-1, keepdims=True))
            alpha = jnp.exp(m_old - m_new)
            p = jnp.exp(s - m_new)
            m_s[hd] = m_new
            acc_s[hd] = alpha * acc_s[hd] + _dot(p.astype(BF16), vaug[kv_rows(jj), :])

    scores(0, 0)

    def body(k, _):
        scores(2 * k + 1, 1)
        softmax_pv(2 * k, 0, False)
        scores(2 * k + 2, 0)
        softmax_pv(2 * k + 1, 1, False)
        return 0

    lax.fori_loop(0, last // 2, body, 0)

    @pl.when(last % 2 == 0)
    def _():
        softmax_pv(last, 0, True)

    @pl.when(last % 2 == 1)
    def _():
        scores(last, 1)
        softmax_pv(last - 1, 0, False)
        softmax_pv(last, 1, True)
    acc_a, acc_b = acc_s[0], acc_s[1]
    o_ref[...] = jnp.where(head_a, acc_a / pltpu.roll(acc_a, HEAD_DIM, 1),
                           acc_b / pltpu.roll(acc_b, HEAD_DIM, 1)).astype(BF16)


def _moba(q, k, v, q0):
    B = q.shape[0]
    nq = q.shape[1] // MOBA_BLOCK
    skv = (q0 + nq) * MOBA_BLOCK
    nb = skv // MOBA_BLOCK
    assert nb <= HEAD_DIM and nb % 2 == 0 and skv <= k.shape[1]
    nbp = -(-nb // SUBLANES) * SUBLANES
    assert q0 % 2 == 0 and nq % 2 == 0
    blk = pl.BlockSpec((None, MOBA_PAIR, LANES), lambda b, h, i: (b, i, h))
    seq = pl.BlockSpec((None, skv, LANES), lambda b, h, i: (b, 0, h))
    return pl.pallas_call(
        functools.partial(_moba_kernel, q0),
        grid=(B, D_ATT // LANES, nq // 2),
        in_specs=[pl.BlockSpec((None, nq * MOBA_BLOCK, LANES), lambda b, h, i: (b, 0, h)), seq, seq],
        out_specs=blk,
        out_shape=jax.ShapeDtypeStruct(q.shape, BF16),
        scratch_shapes=[pltpu.VMEM((nbp, LANES), F32),
                        pltpu.VMEM((skv, LANES), BF16), pltpu.VMEM((skv, LANES), BF16),
                        pltpu.VMEM((skv, LANES), BF16), pltpu.VMEM((skv, LANES), BF16),
                        pltpu.VMEM((2, nq * MOBA_BLOCK, LANES), BF16),
                        pltpu.VMEM((2, MOBA_PAIR, 1), F32),
                        pltpu.VMEM((2, MOBA_PAIR, LANES), F32),
                        pltpu.VMEM((2, 2, MOBA_PAIR, MOBA_PAIR), F32)],
        compiler_params=pltpu.CompilerParams(
            dimension_semantics=("parallel", "parallel", "arbitrary"), vmem_limit_bytes=VMEM_LIMIT),
        name="moba",
    )(q, k, v)


MERGE_TS = 256


def _bf16_bits(x):
    b = pltpu.bitcast(x, jnp.int32)
    r = b + 0x7FFF + (lax.shift_right_logical(b, 16) & 1)
    return lax.shift_right_logical(r, 16)


def _merge_kernel(x_ref, ys_ref, at_ref, ga_ref, gb_ref, wa_ref, wb_ref, wo_ref, g_ref,
                  wq_ref, k1_ref, k2_ref, x1_ref, hw_ref, idx_ref, gate_ref, sc_ref):
    ya = _dot(ys_ref[...], wa_ref[...])
    yb = _dot(at_ref[...], wb_ref[...])
    merged = ga_ref[...].astype(F32) * ya + gb_ref[...].astype(F32) * yb
    x1 = x_ref[...] + _dot(merged.astype(BF16), wo_ref[...])
    x1_ref[...] = x1
    hq = _rms(x1, g_ref[...])
    hw_ref[...] = _pack_words(hq)
    qp = _dot(hq.astype(BF16), wq_ref[...])
    for h in range(PEER_HEADS):
        o = h * PEER_QDIM
        sc_ref[2 * h] = _dot_nt(k1_ref[h], qp[:, o:o + PEER_HALF], precision=HIGHEST)
        sc_ref[2 * h + 1] = _dot_nt(k2_ref[h], qp[:, o + PEER_HALF:o + PEER_QDIM], precision=HIGHEST)
    _topk_kernel(sc_ref, idx_ref, gate_ref)


def _merge(x, ys, att, ga, gb, t0, w_proj_ssm, w_proj_att, w_out, g_ffn, peer_w_q, keys1, keys2):
    B, nt = ys.shape[0], ys.shape[1]
    ts = min(MERGE_TS, nt)
    nblk = nt // ts
    i0 = t0 // ts
    tok = lambda d: pl.BlockSpec((None, ts, d), lambda b, i: (b, i, 0))
    row = lambda d: pl.BlockSpec((ts, d), lambda b, i: (b * nblk + i, 0))
    full = lambda shape: pl.BlockSpec(shape, lambda b, i: (0,) * len(shape))
    qd = PEER_HEADS * PEER_QDIM
    return pl.pallas_call(
        _merge_kernel,
        grid=(B, nblk),
        in_specs=[pl.BlockSpec((None, ts, D_MODEL), lambda b, i: (b, i0 + i, 0)),
                  tok(D_SSM), tok(D_ATT), tok(D_MODEL), tok(D_MODEL),
                  full((D_SSM, D_MODEL)), full((D_ATT, D_MODEL)), full((D_MODEL, D_MODEL)),
                  full((1, D_MODEL)), full((D_MODEL, qd)),
                  full((PEER_HEADS, PEER_KEYS, PEER_HALF)), full((PEER_HEADS, PEER_KEYS, PEER_HALF))],
        out_specs=[row(D_MODEL), row(D_MODEL // 2), row(PEER_SEL), row(PEER_SEL)],
        out_shape=[jax.ShapeDtypeStruct((B * nt, D_MODEL), F32),
                   jax.ShapeDtypeStruct((B * nt, D_MODEL // 2), jnp.int32),
                   jax.ShapeDtypeStruct((B * nt, PEER_SEL), jnp.int32),
                   jax.ShapeDtypeStruct((B * nt, PEER_SEL), F32)],
        scratch_shapes=[pltpu.VMEM((2 * PEER_HEADS, PEER_KEYS, ts), F32)],
        compiler_params=pltpu.CompilerParams(
            dimension_semantics=("parallel", "parallel"), vmem_limit_bytes=VMEM_LIMIT),
        name="merge",
    )(x, ys, att, ga, gb, w_proj_ssm, w_proj_att, w_out, g_ffn, peer_w_q, keys1, keys2)


def _top_rows(s, row, k):
    vals, idxs = [], []
    for _ in range(k):
        m = jnp.max(s, axis=0, keepdims=True)
        idx = jnp.min(jnp.where(s == m, row, s.shape[0]), axis=0, keepdims=True)
        vals.append(m)
        idxs.append(idx)
        s = jnp.where(row == idx, -jnp.inf, s)
    return vals, idxs


def _stack_rows(rows, row16):
    acc = jnp.zeros(row16.shape, rows[0].dtype)
    for r, v in enumerate(rows):
        acc = jnp.where(row16 == r, v, acc)
    return acc


def _topk_kernel(sc_ref, idx_ref, gate_ref):
    ts = sc_ref.shape[-1]
    row = lax.broadcasted_iota(jnp.int32, (PEER_KEYS, ts), 0).astype(F32)
    row16 = lax.broadcasted_iota(jnp.int32, (PEER_TOPK, ts), 0)
    row8 = lax.broadcasted_iota(jnp.int32, (SUBLANES, ts), 0)
    counts = [PEER_TOPK // (i + 1) for i in range(PEER_TOPK)]
    heights = [PEER_TOPK if c > SUBLANES else SUBLANES for c in counts]
    n_cand = sum(heights)
    rowc = lax.broadcasted_iota(jnp.int32, (n_cand, ts), 0).astype(F32)
    gate_rows, eid_rows = [], []
    for h in range(PEER_HEADS):
        v1, i1 = _top_rows(sc_ref[2 * h], row, PEER_TOPK)
        v2, i2 = _top_rows(sc_ref[2 * h + 1], row, PEER_TOPK)
        v2s = _stack_rows(v2, row16)
        i2s = _stack_rows(i2, row16)
        cand, eid = [], []
        for i in range(PEER_TOPK):
            n = heights[i]
            cand.append(jnp.where((row16 if n == PEER_TOPK else row8) < counts[i],
                                  v1[i] + v2s[:n], -jnp.inf))
            eid.append(i1[i] * PEER_KEYS + i2s[:n])
        cand = jnp.concatenate(cand, axis=0)
        eid = jnp.concatenate(eid, axis=0)
        tops, picks = [], []
        for _ in range(PEER_TOPK):
            m = jnp.max(cand, axis=0, keepdims=True)
            pos = jnp.min(jnp.where(cand == m, rowc, n_cand), axis=0, keepdims=True)
            hit = rowc == pos
            picks.append(jnp.max(jnp.where(hit, eid, -1.0), axis=0, keepdims=True))
            tops.append(m)
            cand = jnp.where(hit, -jnp.inf, cand)
        top = _stack_rows(tops, row16)
        p = jnp.exp(top - jnp.max(top, axis=0, keepdims=True))
        gate_rows.append(p / jnp.sum(p, axis=0, keepdims=True))
        eid_rows.append(_stack_rows(picks, row16))
    gate_ref[...] = jnp.transpose(jnp.concatenate(gate_rows, axis=0))
    idx_ref[...] = jnp.transpose(jnp.concatenate(eid_rows, axis=0)).astype(jnp.int32)


SC_CORES = 2
SC_SUBCORES = 16
SC_LANES = 16
SC_WORKERS = SC_CORES * SC_SUBCORES
PEER_CH = SC_LANES
PEER_NCH = PEER_SEL // PEER_CH
PEER_WORDS = D_MODEL // 2
PEER_NWG = PEER_WORDS // SC_LANES
PEER_RING = 4
PEER_QUAD = 4
HI_MASK = -65536
GELU_C = 0.7978845608028654


def _gelu_tanh_via_exp(x):
    z = GELU_C * (x + 0.044715 * (x * x * x))
    t = 1.0 - 2.0 / (jnp.exp(2.0 * z) + 1.0)
    return 0.5 * x * (1.0 + t)


def _unpack_pair(w):
    lo = plsc.bitcast(lax.shift_left(w, 16), F32)
    hi = plsc.bitcast(lax.bitwise_and(w, HI_MASK), F32)
    return lo, hi


def _peer_sc_body(idx_hbm, gate_hbm, h_hbm, uv_hbm, after_hbm, o_hbm,
                  idx_v, gate_v, h_v, buf, out_v, gsem, msem, osem):
    n_tok = o_hbm.shape[0] // SC_WORKERS
    base = (lax.axis_index("s") * SC_CORES + lax.axis_index("c")) * n_tok
    lane = lax.iota(jnp.int32, SC_LANES)
    zero_rows = jnp.zeros((SC_LANES,), jnp.int32)

    def meta_copies(tok, s):
        return (pltpu.make_async_copy(idx_hbm.at[tok], idx_v.at[s], msem.at[s]),
                pltpu.make_async_copy(gate_hbm.at[tok], gate_v.at[s], msem.at[s]),
                pltpu.make_async_copy(h_hbm.at[tok], h_v.at[s], msem.at[s]))

    def gather(slot, rows):
        return pltpu.make_async_copy(uv_hbm.at[rows], buf.at[slot], gsem.at[slot])

    def token(t, carry):
        s = t % 2
        tok = base + t
        nxt = base + jnp.minimum(t + 1, n_tok - 1)
        for cp in meta_copies(nxt, 1 - s):
            cp.start()

        @pl.when(t >= 2)
        def _():
            pltpu.make_async_copy(out_v.at[s], o_hbm.at[tok], osem.at[s]).wait()

        def chunk(c, carry):
            slot = c % PEER_RING
            gather(slot, zero_rows).wait()

            def dot_step(q, accs):
                cols = [pl.ds(pl.multiple_of((q * PEER_QUAD + j) * SC_LANES, SC_LANES), SC_LANES)
                        for j in range(PEER_QUAD)]
                hs = [plsc.bitcast(h_v[s, col], BF16) for col in cols]
                out = []
                for r in range(PEER_CH):
                    p = plsc.bitcast(buf[slot, r, cols[0]], BF16) * hs[0]
                    for j in range(1, PEER_QUAD):
                        p = p + plsc.bitcast(buf[slot, r, cols[j]], BF16) * hs[j]
                    lo, hi = _unpack_pair(plsc.bitcast(p, jnp.int32))
                    out.append(accs[r] + lo + hi)
                return tuple(out)

            accs = lax.fori_loop(0, PEER_NWG // PEER_QUAD, dot_step,
                                 tuple(jnp.zeros((SC_LANES,), F32) for _ in range(PEER_CH)))
            tot = jnp.zeros((SC_LANES,), F32)
            for r in range(PEER_CH):
                tot = jnp.where(lane == r, jnp.sum(accs[r]), tot)
            rows = pl.ds(pl.multiple_of(c * PEER_CH, PEER_CH), PEER_CH)
            wvec = gate_v[s, rows] * _gelu_tanh_via_exp(tot)
            ws = []
            for r in range(PEER_CH):
                w = wvec.at[jnp.full((SC_LANES,), r, jnp.int32)].get(mode="promise_in_bounds")
                ws.append(plsc.pack(w, w, format=plsc.PackFormat.INTERLEAVED,
                                    preferred_element_type=BF16))
            first = c == 0

            @plsc.parallel_loop(0, PEER_NWG, unroll=2)
            def acc_step(g):
                col = pl.ds(pl.multiple_of(g * SC_LANES, SC_LANES), SC_LANES)
                col_v = pl.ds(pl.multiple_of(PEER_WORDS + g * SC_LANES, SC_LANES), SC_LANES)
                o_lo = jnp.where(first, 0.0, out_v[s, col])
                o_hi = jnp.where(first, 0.0, out_v[s, col_v])
                for r0 in range(0, PEER_CH, PEER_QUAD):
                    p = plsc.bitcast(buf[slot, r0, col_v], BF16) * ws[r0]
                    for r in range(r0 + 1, r0 + PEER_QUAD):
                        p = p + plsc.bitcast(buf[slot, r, col_v], BF16) * ws[r]
                    lo, hi = _unpack_pair(plsc.bitcast(p, jnp.int32))
                    o_lo = o_lo + lo
                    o_hi = o_hi + hi
                out_v[s, col] = o_lo
                out_v[s, col_v] = o_hi

            @pl.when(c == PEER_NCH - PEER_RING)
            def _():
                for cp in meta_copies(nxt, 1 - s):
                    cp.wait()

            ahead = c + PEER_RING
            src = jnp.where(ahead < PEER_NCH, s, 1 - s)
            nrows = idx_v[src, pl.ds(pl.multiple_of((ahead % PEER_NCH) * PEER_CH, PEER_CH), PEER_CH)]
            gather(slot, nrows).start()
            return carry

        lax.fori_loop(0, PEER_NCH, chunk, 0)
        pltpu.make_async_copy(out_v.at[s], o_hbm.at[tok], osem.at[s]).start()
        return carry

    for cp in meta_copies(base, 0):
        cp.start()
    for cp in meta_copies(base, 0):
        cp.wait()
    for c in range(PEER_RING):
        gather(c, idx_v[0, pl.ds(c * PEER_CH, PEER_CH)]).start()
    lax.fori_loop(0, n_tok, token, 0)
    for c in range(PEER_RING):
        gather(c, zero_rows).wait()
    for s in range(2):
        pltpu.make_async_copy(out_v.at[s], o_hbm.at[base], osem.at[s]).wait()


PACK_ROWS = 256


def _pack_words(x):
    half = x.shape[1] // 2
    return _bf16_bits(x[:, :half]) | lax.shift_left(_bf16_bits(x[:, half:]), 16)


def _pack_tables_kernel(u_ref, v_ref, o_ref):
    o_ref[:, :PEER_WORDS] = _pack_words(u_ref[...])
    o_ref[:, PEER_WORDS:] = _pack_words(v_ref[...])


def _pack_tables(peer_u, peer_v):
    n = peer_u.shape[0]
    rows = min(PACK_ROWS, n)
    spec = pl.BlockSpec((rows, D_MODEL), lambda i: (i, 0))
    return pl.pallas_call(
        _pack_tables_kernel,
        grid=(n // rows,),
        in_specs=[spec, spec],
        out_specs=spec,
        out_shape=jax.ShapeDtypeStruct((n, D_MODEL), jnp.int32),
        compiler_params=pltpu.CompilerParams(
            dimension_semantics=("parallel",), vmem_limit_bytes=VMEM_LIMIT),
        name="pack_tables",
    )(peer_u, peer_v)


def _peer(idx, h_words, gates, uv_words, after):
    T = h_words.shape[0]
    assert T % (2 * SC_WORKERS) == 0
    mesh = plsc.VectorSubcoreMesh(core_axis_name="c", subcore_axis_name="s",
                                  num_cores=SC_CORES, num_subcores=SC_SUBCORES)
    return pl.kernel(
        _peer_sc_body,
        out_type=jax.ShapeDtypeStruct((T, D_MODEL), F32),
        mesh=mesh,
        scratch_types=[
            pltpu.VMEM((2, PEER_SEL), jnp.int32), pltpu.VMEM((2, PEER_SEL), F32),
            pltpu.VMEM((2, PEER_WORDS), jnp.int32),
            pltpu.VMEM((PEER_RING, PEER_CH, 2 * PEER_WORDS), jnp.int32),
            pltpu.VMEM((2, D_MODEL), F32),
            pltpu.SemaphoreType.DMA((PEER_RING,)),
            pltpu.SemaphoreType.DMA((2,)), pltpu.SemaphoreType.DMA((2,)),
        ],
        compiler_params=pltpu.CompilerParams(needs_layout_passes=False),
        name="peer_sc",
    )(idx, gates, h_words, uv_words, after)


FINAL_TS = 256


def _final_kernel(x1_ref, pe_ref, p_ref, gp_ref, wg_ref, wp_ref, gf_ref, o_ref):
    x2 = x1_ref[...] + pe_ref[...]
    e = _dot(p_ref[...].astype(BF16), wp_ref[...])
    gate = jax.nn.sigmoid(_dot(_rms(x2, gp_ref[...]).astype(BF16), wg_ref[...]))
    o_ref[...] = _rms(x2 + gate * e, gf_ref[...])


def _final(x1, peer_out, p, t0, nt, g_ple, ple_w_gate, ple_w_proj, g_final):
    B = p.shape[0]
    ts = min(FINAL_TS, nt)
    nblk = nt // ts
    i0 = t0 // ts
    row = lambda d: pl.BlockSpec((ts, d), lambda b, i: (b * nblk + i, 0))
    full = lambda shape: pl.BlockSpec(shape, lambda b, i: (0,) * len(shape))
    return pl.pallas_call(
        _final_kernel,
        grid=(B, nblk),
        in_specs=[row(D_MODEL), row(D_MODEL),
                  pl.BlockSpec((None, ts, D_PLE), lambda b, i: (b, i0 + i, 0)),
                  full((1, D_MODEL)), full((D_MODEL, D_MODEL)), full((D_PLE, D_MODEL)),
                  full((1, D_MODEL))],
        out_specs=pl.BlockSpec((None, ts, D_MODEL), lambda b, i: (b, i, 0)),
        out_shape=jax.ShapeDtypeStruct((B, nt, D_MODEL), F32),
        compiler_params=pltpu.CompilerParams(
            dimension_semantics=("parallel", "parallel"), vmem_limit_bytes=VMEM_LIMIT),
        name="final",
    )(x1, peer_out, p, g_ple, ple_w_gate, ple_w_proj, g_final)


CHUNK_STEPS = (512, 512, 1024, 1024, 1024, 1024, 1024, 1024, 512, 512)


def kernel(x, p, positions, g_mix, w_in, ssm_log_dt, ssm_a_re, ssm_a_im, ssm_b_re, ssm_b_im,
           ssm_c_re, ssm_c_im, ssm_d, ssm_w_glu, w_proj_ssm, w_proj_att, w_out, g_ffn,
           peer_w_q, peer_keys1, peer_keys2, peer_u, peer_v, g_ple, ple_w_gate, ple_w_proj,
           g_final):
    B, S, _ = x.shape
    assert w_in.shape[0] == 1, "the final rmsnorm is fused into the single layer's last stage"
    steps = CHUNK_STEPS if sum(CHUNK_STEPS) == S else (S,)
    i = 0
    tables = _s5_tables(ssm_log_dt[i], ssm_a_re[i], ssm_a_im[i], ssm_b_re[i], ssm_b_im[i],
                        ssm_c_re[i], ssm_c_im[i])
    w_in_b, w_glu_b = w_in[i].astype(BF16), ssm_w_glu[i].astype(BF16)
    d_skip = ssm_d[i].reshape(1, D_SSM).astype(F32)
    merge_w = (w_proj_ssm[i].astype(BF16), w_proj_att[i].astype(BF16), w_out[i].astype(BF16),
               g_ffn[i].reshape(1, D_MODEL), peer_w_q[i].astype(BF16), peer_keys1[i], peer_keys2[i])
    final_w = (g_ple[i].reshape(1, D_MODEL), ple_w_gate[i].astype(BF16),
               ple_w_proj[i].astype(BF16), g_final.reshape(1, D_MODEL))
    uv_words = _pack_tables(peer_u[i], peer_v[i])
    k_all = jnp.zeros((B, S, D_ATT), BF16)
    v_all = jnp.zeros((B, S, D_ATT), BF16)
    carry = jnp.zeros((2, SUBLANES, D_STATE), F32)
    outs = []
    t0 = 0
    after = (carry, carry)
    peer_prev = carry
    for nt in steps:
        u_sb, q, k, v, ga, gb = _in_proj(x, positions, g_mix[i], w_in_b, t0, nt, after)
        k_all = lax.dynamic_update_slice(k_all, k, (0, t0, 0))
        v_all = lax.dynamic_update_slice(v_all, v, (0, t0, 0))
        ys, carry = _s5(u_sb, carry, tables, d_skip, w_glu_b, B)
        att = _moba(q, k_all, v_all, t0 // MOBA_BLOCK)
        x1, h_words, idx, gates = _merge(x, ys, att, ga, gb, t0, *merge_w)
        after = (gates, outs[-3] if len(outs) > 2 else carry)
        peer_out = _peer(idx, h_words, gates, uv_words, peer_prev)
        peer_prev = peer_out
        outs.append(_final(x1, peer_out, p[i], t0, nt, *final_w))
        t0 += nt
    return jnp.concatenate(outs, axis=1)
```

```python
import functools

import jax
import jax.numpy as jnp
from jax import lax
from jax.experimental import pallas as pl
from jax.experimental.pallas import tpu as pltpu
from jax.experimental.pallas import tpu_sc as plsc

F32 = jnp.float32
BF16 = jnp.bfloat16

D_MODEL = 1024
D_SSM = 512
SSM_GROUPS = 32
SSM_STATE = 64
D_STATE = SSM_GROUPS * SSM_STATE
HEAD_DIM = 64
D_ATT = 512
ROT_DIM = 16
ROPE_THETA = 500000.0
MOBA_BLOCK = 256
MOBA_TOPK = 3
PEER_HEADS = 8
PEER_KEYS = 128
PEER_QDIM = 256
PEER_HALF = 128
PEER_TOPK = 16
PEER_SEL = PEER_HEADS * PEER_TOPK
D_PLE = 256
EPS = 1e-6
NEG = -1e30
LANES = 128
SUBLANES = 8
VMEM_LIMIT = 48 * 1024 * 1024
HIGHEST = lax.Precision.HIGHEST


def _rms(x, g):
    return x * lax.rsqrt(jnp.mean(x * x, axis=-1, keepdims=True) + EPS) * g


def _dot(a, b):
    return jnp.dot(a, b, preferred_element_type=F32)


def _dot_nt(a, b, precision=None):
    return lax.dot_general(a, b, (((1,), (1,)), ((), ())), precision=precision,
                           preferred_element_type=F32)


IN_TS = 512


def _in_proj_kernel(x_ref, pos_ref, g_ref, w_ref, invf_ref, after_a, after_b,
                    u_ref, q_ref, k_ref, v_ref, ga_ref, gb_ref):
    del after_a, after_b
    h = _rms(x_ref[...], g_ref[...]).astype(BF16)

    def proj(lo, hi):
        return _dot(h, w_ref[:, lo:hi])

    u_ref[...] = proj(0, D_SSM).astype(BF16)
    ang = pos_ref[...].astype(F32) * invf_ref[...]
    cos = jnp.cos(ang)
    sin = jnp.sin(ang)
    lane = lax.broadcasted_iota(jnp.int32, (1, LANES), 1) % HEAD_DIM
    half = ROT_DIM // 2
    sin_hi = jnp.where((lane >= half) & (lane < ROT_DIM), sin, 0.0)
    sin_lo = jnp.where(lane < half, -sin, 0.0)
    reps = D_ATT // LANES
    cos4 = jnp.concatenate([cos] * reps, axis=1)
    sin_hi4 = jnp.concatenate([sin_hi] * reps, axis=1)
    sin_lo4 = jnp.concatenate([sin_lo] * reps, axis=1)

    def rope(t):
        return (t * cos4 + pltpu.roll(t, half, 1) * sin_hi4
                + pltpu.roll(t, D_ATT - half, 1) * sin_lo4)

    q = rope(proj(D_SSM, D_SSM + D_ATT))
    q_ref[...] = (q * (HEAD_DIM ** -0.5)).astype(BF16)
    k_ref[...] = rope(proj(D_SSM + D_ATT, D_SSM + 2 * D_ATT)).astype(BF16)
    v_ref[...] = proj(D_SSM + 2 * D_ATT, D_SSM + 3 * D_ATT).astype(BF16)
    o = D_SSM + 3 * D_ATT
    ga_ref[...] = jax.nn.sigmoid(proj(o, o + D_MODEL)).astype(BF16)
    gb_ref[...] = jax.nn.sigmoid(proj(o + D_MODEL, o + 2 * D_MODEL)).astype(BF16)


def _in_proj(x, positions, g_mix, w_in, t0, nt, after):
    B, S, _ = x.shape
    ts = min(IN_TS, nt)
    assert nt % ts == 0 and t0 % ts == 0
    i0 = t0 // ts
    inv_freq = ROPE_THETA ** (-jnp.arange(0, ROT_DIM, 2, dtype=F32) / ROT_DIM)
    lane = jnp.arange(LANES) % HEAD_DIM
    invf = jnp.where(lane < ROT_DIM, inv_freq[lane % (ROT_DIM // 2)], 0.0).reshape(1, LANES)
    d_in = w_in.shape[1]
    src = lambda d: pl.BlockSpec((None, ts, d), lambda b, i: (b, i0 + i, 0))
    tok = lambda d: pl.BlockSpec((None, ts, d), lambda b, i: (b, i, 0))
    full = lambda shape: pl.BlockSpec(shape, lambda b, i: (0,) * len(shape))
    return pl.pallas_call(
        _in_proj_kernel,
        grid=(B, nt // ts),
        in_specs=[src(D_MODEL), src(1), full((1, D_MODEL)), full((D_MODEL, d_in)), full((1, LANES)),
                  pl.BlockSpec(memory_space=pl.ANY), pl.BlockSpec(memory_space=pl.ANY)],
        out_specs=[pl.BlockSpec((ts, D_SSM), lambda b, i: (i, b)),
                   tok(D_ATT), tok(D_ATT), tok(D_ATT), tok(D_MODEL), tok(D_MODEL)],
        out_shape=[jax.ShapeDtypeStruct((nt, B * D_SSM), BF16),
                   jax.ShapeDtypeStruct((B, nt, D_ATT), BF16),
                   jax.ShapeDtypeStruct((B, nt, D_ATT), BF16),
                   jax.ShapeDtypeStruct((B, nt, D_ATT), BF16),
                   jax.ShapeDtypeStruct((B, nt, D_MODEL), BF16),
                   jax.ShapeDtypeStruct((B, nt, D_MODEL), BF16)],
        compiler_params=pltpu.CompilerParams(
            dimension_semantics=("parallel", "parallel"), vmem_limit_bytes=VMEM_LIMIT),
        name="in_proj",
    )(x, positions.reshape(B, S, 1), g_mix.reshape(1, D_MODEL), w_in, invf, *after)


S5_TS = 128
S5_BATCH = 4
S5_COLS = 512


def _s5_kernel(u_ref, c0_ref, bre_ref, bim_ref, a1r_ref, a1i_ref, pr_ref, pi_ref,
               cre_ref, cim_ref, d_ref, wglu_ref, y_ref, c1_ref,
               xr, xi, cr, ci, ysc):
    rows = xr.shape[0]
    ts = rows // S5_BATCH

    @pl.when(pl.program_id(0) == 0)
    def _():
        cr[...] = c0_ref[0]
        ci[...] = c0_ref[1]

    u = u_ref[...]
    for cb in range(D_STATE // S5_COLS):
        sl = slice(cb * S5_COLS, (cb + 1) * S5_COLS)
        u_cb = u[:, cb * LANES:(cb + 1) * LANES]
        xr[:, sl] = _dot(u_cb, bre_ref[cb])
        xi[:, sl] = _dot(u_cb, bim_ref[cb])

    hi_rows = lax.broadcasted_iota(jnp.int32, (SUBLANES, S5_COLS), 0) >= S5_BATCH
    for cb in range(D_STATE // S5_COLS):
        sl = slice(cb * S5_COLS, (cb + 1) * S5_COLS)
        a_r, a_i = a1r_ref[:, sl], a1i_ref[:, sl]
        p_r, p_i = pr_ref[:, sl], pi_ref[:, sl]

        def body(t, carry):
            c_r, c_i = carry
            r0 = pl.multiple_of(t * SUBLANES, SUBLANES)
            x_r = xr[pl.ds(r0, SUBLANES), sl]
            x_i = xi[pl.ds(r0, SUBLANES), sl]
            s_r = pltpu.roll(x_r, S5_BATCH, 0)
            s_i = pltpu.roll(x_i, S5_BATCH, 0)
            h_r = x_r + (a_r * s_r - a_i * s_i) + (p_r * c_r - p_i * c_i)
            h_i = x_i + (a_r * s_i + a_i * s_r) + (p_r * c_i + p_i * c_r)
            xr[pl.ds(r0, SUBLANES), sl] = h_r
            xi[pl.ds(r0, SUBLANES), sl] = h_i
            n_r = jnp.where(hi_rows, h_r, pltpu.roll(h_r, S5_BATCH, 0))
            n_i = jnp.where(hi_rows, h_i, pltpu.roll(h_i, S5_BATCH, 0))
            return n_r, n_i

        c_r, c_i = lax.fori_loop(0, rows // SUBLANES, body, (cr[:, sl], ci[:, sl]), unroll=2)
        cr[:, sl] = c_r
        ci[:, sl] = c_i

    y = jnp.concatenate(
        [_dot(xr[:, cb * S5_COLS:(cb + 1) * S5_COLS].astype(BF16), cre_ref[cb])
         - _dot(xi[:, cb * S5_COLS:(cb + 1) * S5_COLS].astype(BF16), cim_ref[cb])
         for cb in range(D_STATE // S5_COLS)], axis=1) + d_ref[...] * u.astype(F32)
    y = jax.nn.gelu(y)
    y = y * jax.nn.sigmoid(_dot(y.astype(BF16), wglu_ref[...]))
    for c in range(D_SSM // LANES):
        ysc[c] = y[:, c * LANES:(c + 1) * LANES]
    for b in range(S5_BATCH):
        for c in range(D_SSM // LANES):
            y_ref[b, :, c * LANES:(c + 1) * LANES] = (
                ysc[c, pl.ds(b, ts, stride=S5_BATCH), :].astype(BF16))

    @pl.when(pl.program_id(0) == pl.num_programs(0) - 1)
    def _():
        c1_ref[0] = cr[...]
        c1_ref[1] = ci[...]


def _s5_tables(log_dt, a_re, a_im, b_re, b_im, c_re, c_im):
    dt = jnp.exp(log_dt.astype(F32))[:, None]
    ar, ai = a_re.astype(F32), a_im.astype(F32)
    mag = jnp.exp(dt * ar)
    abar_re, abar_im = mag * jnp.cos(dt * ai), mag * jnp.sin(dt * ai)
    den = ar * ar + ai * ai
    nr, ni = abar_re - 1.0, abar_im
    f_re = (nr * ar + ni * ai) / den
    f_im = (ni * ar - nr * ai) / den
    br, bi = b_re.astype(F32), b_im.astype(F32)
    bb_re = f_re[..., None] * br - f_im[..., None] * bi
    bb_im = f_re[..., None] * bi + f_im[..., None] * br
    eye = jnp.eye(SSM_GROUPS, dtype=F32)

    def in_blockdiag(bb):
        return jnp.einsum('gnc,gh->gchn', bb, eye).reshape(D_SSM, D_STATE)

    def out_blockdiag(c):
        return jnp.einsum('gcn,gh->gnhc', c.astype(F32), eye).reshape(D_STATE, D_SSM)

    a_r = abar_re.reshape(1, D_STATE)
    a_i = abar_im.reshape(1, D_STATE)
    a2_r = a_r * a_r - a_i * a_i
    a2_i = 2.0 * a_r * a_i
    hi = (jnp.arange(SUBLANES) >= S5_BATCH)[:, None]
    a1r = jnp.where(hi, a_r, 0.0)
    a1i = jnp.where(hi, a_i, 0.0)
    p_r = jnp.where(hi, a2_r, a_r)
    p_i = jnp.where(hi, a2_i, a_i)
    nblk = D_STATE // S5_COLS
    cw = D_SSM // nblk

    def in_blocks(m):
        return jnp.stack([m[b * cw:(b + 1) * cw, b * S5_COLS:(b + 1) * S5_COLS] for b in range(nblk)])

    def out_blocks(m):
        return jnp.stack([m[b * S5_COLS:(b + 1) * S5_COLS, b * cw:(b + 1) * cw] for b in range(nblk)])

    return (in_blocks(in_blockdiag(bb_re)).astype(BF16), in_blocks(in_blockdiag(bb_im)).astype(BF16),
            a1r, a1i, p_r, p_i,
            out_blocks(out_blockdiag(c_re)).astype(BF16), out_blocks(out_blockdiag(c_im)).astype(BF16))


def _s5(u_sb, carry, tables, d_skip, w_glu, B):
    assert B == S5_BATCH
    nt = u_sb.shape[0]
    ts = min(S5_TS, nt)
    rows = ts * B
    bre, bim, a1r, a1i, p_r, p_i, cre, cim = tables
    full = lambda shape: pl.BlockSpec(shape, lambda i: (0,) * len(shape))
    return pl.pallas_call(
        _s5_kernel,
        grid=(nt // ts,),
        in_specs=[pl.BlockSpec((rows, D_SSM), lambda i: (i, 0)),
                  full((2, SUBLANES, D_STATE)),
                  full(bre.shape), full(bim.shape),
                  full((SUBLANES, D_STATE)), full((SUBLANES, D_STATE)),
                  full((SUBLANES, D_STATE)), full((SUBLANES, D_STATE)),
                  full(cre.shape), full(cim.shape),
                  full((1, D_SSM)), full((D_SSM, D_SSM))],
        out_specs=[pl.BlockSpec((B, ts, D_SSM), lambda i: (0, i, 0)),
                   full((2, SUBLANES, D_STATE))],
        out_shape=[jax.ShapeDtypeStruct((B, nt, D_SSM), BF16),
                   jax.ShapeDtypeStruct((2, SUBLANES, D_STATE), F32)],
        scratch_shapes=[pltpu.VMEM((rows, D_STATE), F32), pltpu.VMEM((rows, D_STATE), F32),
                        pltpu.VMEM((SUBLANES, D_STATE), F32), pltpu.VMEM((SUBLANES, D_STATE), F32),
                        pltpu.VMEM((D_SSM // LANES, rows, LANES), F32)],
        compiler_params=pltpu.CompilerParams(
            dimension_semantics=("arbitrary",), vmem_limit_bytes=VMEM_LIMIT),
        name="s5",
    )(u_sb.reshape(nt * B, D_SSM), carry, bre, bim, a1r, a1i, p_r, p_i, cre, cim, d_skip, w_glu)


MOBA_PAIR = 2 * MOBA_BLOCK


def _moba_kernel(q0, q_ref, k_ref, v_ref, o_ref, kmean, kaug_a, kaug_b, vaug_a, vaug_b, qaug,
                 m_s, acc_s, s_buf):
    last = pl.program_id(2) + q0 // 2
    nb = k_ref.shape[0] // MOBA_BLOCK
    nbp = kmean.shape[0]
    lane = lax.broadcasted_iota(jnp.int32, (1, LANES), 1)
    head_a = lane < HEAD_DIM

    @pl.when(pl.program_id(2) == 0)
    def _():
        kmean[...] = jnp.zeros_like(kmean)
        for j in range(nb):
            rows = pl.ds(j * MOBA_BLOCK, MOBA_BLOCK)
            kj = k_ref[rows, :].astype(F32)
            vj = v_ref[rows, :].astype(F32)
            kmean[j:j + 1, :] = jnp.sum(kj, axis=0, keepdims=True) * (1.0 / MOBA_BLOCK)
            kaug_a[rows, :] = jnp.where(head_a, kj, jnp.where(lane - HEAD_DIM == j, 1.0, 0.0)).astype(BF16)
            kaug_b[rows, :] = jnp.where(head_a, jnp.where(lane == j, 1.0, 0.0), kj).astype(BF16)
            vaug_a[rows, :] = jnp.where(head_a, vj, 1.0).astype(BF16)
            vaug_b[rows, :] = jnp.where(head_a, 1.0, vj).astype(BF16)
        blk_row = lax.broadcasted_iota(jnp.int32, (nbp, MOBA_BLOCK), 0)
        for t in range(q_ref.shape[0] // MOBA_BLOCK):
            qt = q0 + t
            qf = q_ref[t * MOBA_BLOCK:(t + 1) * MOBA_BLOCK, :].astype(F32)
            for hd, is_a in enumerate((True, False)):
                mine = head_a if is_a else jnp.logical_not(head_a)
                q_own = jnp.where(mine, qf, 0.0)
                g = _dot_nt(kmean[...], q_own, precision=HIGHEST)
                g = jnp.where(blk_row < qt, g, NEG)
                sel = jnp.zeros(g.shape, F32)
                for _ in range(MOBA_TOPK):
                    m = jnp.max(g, axis=0, keepdims=True)
                    idx = jnp.min(jnp.where(g == m, blk_row, nbp), axis=0, keepdims=True)
                    hit = blk_row == idx
                    sel = jnp.where(hit, jnp.where(idx < qt, 1.0, 0.0), sel)
                    g = jnp.where(hit, -jnp.inf, g)
                bias_t = jnp.where(sel > 0.0, 0.0, jnp.where(blk_row == qt, 0.0, NEG))
                bias_t = jnp.concatenate([bias_t, jnp.full((LANES - nbp, MOBA_BLOCK), NEG, F32)], axis=0)
                bias = jnp.transpose(bias_t)
                if is_a:
                    bias = pltpu.roll(bias, HEAD_DIM, 1)
                qaug[hd, t * MOBA_BLOCK:(t + 1) * MOBA_BLOCK, :] = jnp.where(mine, qf, bias).astype(BF16)

    tile_rows = pl.ds(pl.multiple_of(pl.program_id(2) * MOBA_PAIR, MOBA_PAIR), MOBA_PAIR)
    q_augs = [qaug[0, tile_rows, :], qaug[1, tile_rows, :]]

    m_s[...] = jnp.full(m_s.shape, -jnp.inf, F32)
    acc_s[...] = jnp.zeros_like(acc_s)
    qpos = last * MOBA_PAIR + lax.broadcasted_iota(jnp.int32, (MOBA_PAIR, MOBA_PAIR), 0)
    col = lax.broadcasted_iota(jnp.int32, (MOBA_PAIR, MOBA_PAIR), 1)

    def kv_rows(jj):
        return pl.ds(pl.multiple_of(jj * MOBA_PAIR, MOBA_PAIR), MOBA_PAIR)

    def scores(jj, slot):
        for hd, kaug in enumerate((kaug_a, kaug_b)):
            s_buf[slot, hd] = _dot_nt(q_augs[hd], kaug[kv_rows(jj), :])

    def softmax_pv(jj, slot, causal):
        for hd, vaug in enumerate((vaug_a, vaug_b)):
            s = s_buf[slot, hd]
            if causal:
                s = jnp.where(jj * MOBA_PAIR + col <= qpos, s, NEG)
            m_old = m_s[hd]
            m_new = jnp.maximum(m_old, jnp.max(s, axis=-1, keepdims=True))
            alpha = jnp.exp(m_old - m_new)
            p = jnp.exp(s - m_new)
            m_s[hd] = m_new
            acc_s[hd] = alpha * acc_s[hd] + _dot(p.astype(BF16), vaug[kv_rows(jj), :])

    scores(0, 0)

    def body(k, _):
        scores(2 * k + 1, 1)
        softmax_pv(2 * k, 0, False)
        scores(2 * k + 2, 0)
        softmax_pv(2 * k + 1, 1, False)
        return 0

    lax.fori_loop(0, last // 2, body, 0)

    @pl.when(last % 2 == 0)
    def _():
        softmax_pv(last, 0, True)

    @pl.when(last % 2 == 1)
    def _():
        scores(last, 1)
        softmax_pv(last - 1, 0, False)
        softmax_pv(last, 1, True)
    acc_a, acc_b = acc_s[0], acc_s[1]
    o_ref[...] = jnp.where(head_a, acc_a / pltpu.roll(acc_a, HEAD_DIM, 1),
                           acc_b / pltpu.roll(acc_b, HEAD_DIM, 1)).astype(BF16)


def _moba(q, k, v, q0):
    B = q.shape[0]
    nq = q.shape[1] // MOBA_BLOCK
    skv = (q0 + nq) * MOBA_BLOCK
    nb = skv // MOBA_BLOCK
    assert nb <= HEAD_DIM and nb % 2 == 0 and skv <= k.shape[1]
    nbp = -(-nb // SUBLANES) * SUBLANES
    assert q0 % 2 == 0 and nq % 2 == 0
    blk = pl.BlockSpec((None, MOBA_PAIR, LANES), lambda b, h, i: (b, i, h))
    seq = pl.BlockSpec((None, skv, LANES), lambda b, h, i: (b, 0, h))
    return pl.pallas_call(
        functools.partial(_moba_kernel, q0),
        grid=(B, D_ATT // LANES, nq // 2),
        in_specs=[pl.BlockSpec((None, nq * MOBA_BLOCK, LANES), lambda b, h, i: (b, 0, h)), seq, seq],
        out_specs=blk,
        out_shape=jax.ShapeDtypeStruct(q.shape, BF16),
        scratch_shapes=[pltpu.VMEM((nbp, LANES), F32),
                        pltpu.VMEM((skv, LANES), BF16), pltpu.VMEM((skv, LANES), BF16),
                        pltpu.VMEM((skv, LANES), BF16), pltpu.VMEM((skv, LANES), BF16),
                        pltpu.VMEM((2, nq * MOBA_BLOCK, LANES), BF16),
                        pltpu.VMEM((2, MOBA_PAIR, 1), F32),
                        pltpu.VMEM((2, MOBA_PAIR, LANES), F32),
                        pltpu.VMEM((2, 2, MOBA_PAIR, MOBA_PAIR), F32)],
        compiler_params=pltpu.CompilerParams(
            dimension_semantics=("parallel", "parallel", "arbitrary"), vmem_limit_bytes=VMEM_LIMIT),
        name="moba",
    )(q, k, v)


MERGE_TS = 256


def _bf16_bits(x):
    b = pltpu.bitcast(x, jnp.int32)
    r = b + 0x7FFF + (lax.shift_right_logical(b, 16) & 1)
    return lax.shift_right_logical(r, 16)


def _merge_kernel(x_ref, ys_ref, at_ref, ga_ref, gb_ref, wa_ref, wb_ref, wo_ref, g_ref,
                  wq_ref, k1_ref, k2_ref, x1_ref, hw_ref, idx_ref, gate_ref, sc_ref):
    ya = _dot(ys_ref[...], wa_ref[...])
    yb = _dot(at_ref[...], wb_ref[...])
    merged = ga_ref[...].astype(F32) * ya + gb_ref[...].astype(F32) * yb
    x1 = x_ref[...] + _dot(merged.astype(BF16), wo_ref[...])
    x1_ref[...] = x1
    hq = _rms(x1, g_ref[...])
    hw_ref[...] = _pack_words(hq)
    qp = _dot(hq.astype(BF16), wq_ref[...])
    for h in range(PEER_HEADS):
        o = h * PEER_QDIM
        sc_ref[2 * h] = _dot_nt(k1_ref[h], qp[:, o:o + PEER_HALF], precision=HIGHEST)
        sc_ref[2 * h + 1] = _dot_nt(k2_ref[h], qp[:, o + PEER_HALF:o + PEER_QDIM], precision=HIGHEST)
    _topk_kernel(sc_ref, idx_ref, gate_ref)


def _merge(x, ys, att, ga, gb, t0, w_proj_ssm, w_proj_att, w_out, g_ffn, peer_w_q, keys1, keys2):
    B, nt = ys.shape[0], ys.shape[1]
    ts = min(MERGE_TS, nt)
    nblk = nt // ts
    i0 = t0 // ts
    tok = lambda d: pl.BlockSpec((None, ts, d), lambda b, i: (b, i, 0))
    row = lambda d: pl.BlockSpec((ts, d), lambda b, i: (b * nblk + i, 0))
    full = lambda shape: pl.BlockSpec(shape, lambda b, i: (0,) * len(shape))
    qd = PEER_HEADS * PEER_QDIM
    return pl.pallas_call(
        _merge_kernel,
        grid=(B, nblk),
        in_specs=[pl.BlockSpec((None, ts, D_MODEL), lambda b, i: (b, i0 + i, 0)),
                  tok(D_SSM), tok(D_ATT), tok(D_MODEL), tok(D_MODEL),
                  full((D_SSM, D_MODEL)), full((D_ATT, D_MODEL)), full((D_MODEL, D_MODEL)),
                  full((1, D_MODEL)), full((D_MODEL, qd)),
                  full((PEER_HEADS, PEER_KEYS, PEER_HALF)), full((PEER_HEADS, PEER_KEYS, PEER_HALF))],
        out_specs=[row(D_MODEL), row(D_MODEL // 2), row(PEER_SEL), row(PEER_SEL)],
        out_shape=[jax.ShapeDtypeStruct((B * nt, D_MODEL), F32),
                   jax.ShapeDtypeStruct((B * nt, D_MODEL // 2), jnp.int32),
                   jax.ShapeDtypeStruct((B * nt, PEER_SEL), jnp.int32),
                   jax.ShapeDtypeStruct((B * nt, PEER_SEL), F32)],
        scratch_shapes=[pltpu.VMEM((2 * PEER_HEADS, PEER_KEYS, ts), F32)],
        compiler_params=pltpu.CompilerParams(
            dimension_semantics=("parallel", "parallel"), vmem_limit_bytes=VMEM_LIMIT),
        name="merge",
    )(x, ys, att, ga, gb, w_proj_ssm, w_proj_att, w_out, g_ffn, peer_w_q, keys1, keys2)


def _top_rows(s, row, k):
    vals, idxs = [], []
    for _ in range(k):
        m = jnp.max(s, axis=0, keepdims=True)
        idx = jnp.min(jnp.where(s == m, row, s.shape[0]), axis=0, keepdims=True)
        vals.append(m)
        idxs.append(idx)
        s = jnp.where(row == idx, -jnp.inf, s)
    return vals, idxs


def _stack_rows(rows, row16):
    acc = jnp.zeros(row16.shape, rows[0].dtype)
    for r, v in enumerate(rows):
        acc = jnp.where(row16 == r, v, acc)
    return acc


def _topk_kernel(sc_ref, idx_ref, gate_ref):
    ts = sc_ref.shape[-1]
    row = lax.broadcasted_iota(jnp.int32, (PEER_KEYS, ts), 0).astype(F32)
    row16 = lax.broadcasted_iota(jnp.int32, (PEER_TOPK, ts), 0)
    row8 = lax.broadcasted_iota(jnp.int32, (SUBLANES, ts), 0)
    counts = [PEER_TOPK // (i + 1) for i in range(PEER_TOPK)]
    heights = [PEER_TOPK if c > SUBLANES else SUBLANES for c in counts]
    n_cand = sum(heights)
    rowc = lax.broadcasted_iota(jnp.int32, (n_cand, ts), 0).astype(F32)
    gate_rows, eid_rows = [], []
    for h in range(PEER_HEADS):
        v1, i1 = _top_rows(sc_ref[2 * h], row, PEER_TOPK)
        v2, i2 = _top_rows(sc_ref[2 * h + 1], row, PEER_TOPK)
        v2s = _stack_rows(v2, row16)
        i2s = _stack_rows(i2, row16)
        cand, eid = [], []
        for i in range(PEER_TOPK):
            n = heights[i]
            cand.append(jnp.where((row16 if n == PEER_TOPK else row8) < counts[i],
                                  v1[i] + v2s[:n], -jnp.inf))
            eid.append(i1[i] * PEER_KEYS + i2s[:n])
        cand = jnp.concatenate(cand, axis=0)
        eid = jnp.concatenate(eid, axis=0)
        tops, picks = [], []
        for _ in range(PEER_TOPK):
            m = jnp.max(cand, axis=0, keepdims=True)
            pos = jnp.min(jnp.where(cand == m, rowc, n_cand), axis=0, keepdims=True)
            hit = rowc == pos
            picks.append(jnp.max(jnp.where(hit, eid, -1.0), axis=0, keepdims=True))
            tops.append(m)
            cand = jnp.where(hit, -jnp.inf, cand)
        top = _stack_rows(tops, row16)
        p = jnp.exp(top - jnp.max(top, axis=0, keepdims=True))
        gate_rows.append(p / jnp.sum(p, axis=0, keepdims=True))
        eid_rows.append(_stack_rows(picks, row16))
    gate_ref[...] = jnp.transpose(jnp.concatenate(gate_rows, axis=0))
    idx_ref[...] = jnp.transpose(jnp.concatenate(eid_rows, axis=0)).astype(jnp.int32)


SC_CORES = 2
SC_SUBCORES = 16
SC_LANES = 16
SC_WORKERS = SC_CORES * SC_SUBCORES
PEER_CH = SC_LANES
PEER_NCH = PEER_SEL // PEER_CH
PEER_WORDS = D_MODEL // 2
PEER_NWG = PEER_WORDS // SC_LANES
PEER_RING = 4
PEER_QUAD = 4
HI_MASK = -65536
GELU_C = 0.7978845608028654


def _gelu_tanh_via_exp(x):
    z = GELU_C * (x + 0.044715 * (x * x * x))
    t = 1.0 - 2.0 / (jnp.exp(2.0 * z) + 1.0)
    return 0.5 * x * (1.0 + t)


def _unpack_pair(w):
    lo = plsc.bitcast(lax.shift_left(w, 16), F32)
    hi = plsc.bitcast(lax.bitwise_and(w, HI_MASK), F32)
    return lo, hi


def _peer_sc_body(idx_hbm, gate_hbm, h_hbm, uv_hbm, after_hbm, o_hbm,
                  idx_v, gate_v, h_v, buf, out_v, gsem, msem, osem):
    n_tok = o_hbm.shape[0] // SC_WORKERS
    base = (lax.axis_index("s") * SC_CORES + lax.axis_index("c")) * n_tok
    lane = lax.iota(jnp.int32, SC_LANES)
    zero_rows = jnp.zeros((SC_LANES,), jnp.int32)

    def meta_copies(tok, s):
        return (pltpu.make_async_copy(idx_hbm.at[tok], idx_v.at[s], msem.at[s]),
                pltpu.make_async_copy(gate_hbm.at[tok], gate_v.at[s], msem.at[s]),
                pltpu.make_async_copy(h_hbm.at[tok], h_v.at[s], msem.at[s]))

    def gather(slot, rows):
        return pltpu.make_async_copy(uv_hbm.at[rows], buf.at[slot], gsem.at[slot])

    def token(t, carry):
        s = t % 2
        tok = base + t
        nxt = base + jnp.minimum(t + 1, n_tok - 1)
        for cp in meta_copies(nxt, 1 - s):
            cp.start()

        @pl.when(t >= 2)
        def _():
            pltpu.make_async_copy(out_v.at[s], o_hbm.at[tok], osem.at[s]).wait()

        def chunk(c, carry):
            slot = c % PEER_RING
            gather(slot, zero_rows).wait()

            def dot_step(q, accs):
                cols = [pl.ds(pl.multiple_of((q * PEER_QUAD + j) * SC_LANES, SC_LANES), SC_LANES)
                        for j in range(PEER_QUAD)]
                hs = [plsc.bitcast(h_v[s, col], BF16) for col in cols]
                out = []
                for r in range(PEER_CH):
                    p = plsc.bitcast(buf[slot, r, cols[0]], BF16) * hs[0]
                    for j in range(1, PEER_QUAD):
                        p = p + plsc.bitcast(buf[slot, r, cols[j]], BF16) * hs[j]
                    lo, hi = _unpack_pair(plsc.bitcast(p, jnp.int32))
                    out.append(accs[r] + lo + hi)
                return tuple(out)

            accs = lax.fori_loop(0, PEER_NWG // PEER_QUAD, dot_step,
                                 tuple(jnp.zeros((SC_LANES,), F32) for _ in range(PEER_CH)))
            tot = jnp.zeros((SC_LANES,), F32)
            for r in range(PEER_CH):
                tot = jnp.where(lane == r, jnp.sum(accs[r]), tot)
            rows = pl.ds(pl.multiple_of(c * PEER_CH, PEER_CH), PEER_CH)
            wvec = gate_v[s, rows] * _gelu_tanh_via_exp(tot)
            ws = []
            for r in range(PEER_CH):
                w = wvec.at[jnp.full((SC_LANES,), r, jnp.int32)].get(mode="promise_in_bounds")
                ws.append(plsc.pack(w, w, format=plsc.PackFormat.INTERLEAVED,
                                    preferred_element_type=BF16))
            first = c == 0

            @plsc.parallel_loop(0, PEER_NWG, unroll=2)
            def acc_step(g):
                col = pl.ds(pl.multiple_of(g * SC_LANES, SC_LANES), SC_LANES)
                col_v = pl.ds(pl.multiple_of(PEER_WORDS + g * SC_LANES, SC_LANES), SC_LANES)
                o_lo = jnp.where(first, 0.0, out_v[s, col])
                o_hi = jnp.where(first, 0.0, out_v[s, col_v])
                for r0 in range(0, PEER_CH, PEER_QUAD):
                    p = plsc.bitcast(buf[slot, r0, col_v], BF16) * ws[r0]
                    for r in range(r0 + 1, r0 + PEER_QUAD):
                        p = p + plsc.bitcast(buf[slot, r, col_v], BF16) * ws[r]
                    lo, hi = _unpack_pair(plsc.bitcast(p, jnp.int32))
                    o_lo = o_lo + lo
                    o_hi = o_hi + hi
                out_v[s, col] = o_lo
                out_v[s, col_v] = o_hi

            @pl.when(c == PEER_NCH - PEER_RING)
            def _():
                for cp in meta_copies(nxt, 1 - s):
                    cp.wait()

            ahead = c + PEER_RING
            src = jnp.where(ahead < PEER_NCH, s, 1 - s)
            nrows = idx_v[src, pl.ds(pl.multiple_of((ahead % PEER_NCH) * PEER_CH, PEER_CH), PEER_CH)]
            gather(slot, nrows).start()
            return carry

        lax.fori_loop(0, PEER_NCH, chunk, 0)
        pltpu.make_async_copy(out_v.at[s], o_hbm.at[tok], osem.at[s]).start()
        return carry

    for cp in meta_copies(base, 0):
        cp.start()
    for cp in meta_copies(base, 0):
        cp.wait()
    for c in range(PEER_RING):
        gather(c, idx_v[0, pl.ds(c * PEER_CH, PEER_CH)]).start()
    lax.fori_loop(0, n_tok, token, 0)
    for c in range(PEER_RING):
        gather(c, zero_rows).wait()
    for s in range(2):
        pltpu.make_async_copy(out_v.at[s], o_hbm.at[base], osem.at[s]).wait()


PACK_ROWS = 256


def _pack_words(x):
    half = x.shape[1] // 2
    return _bf16_bits(x[:, :half]) | lax.shift_left(_bf16_bits(x[:, half:]), 16)


def _pack_tables_kernel(u_ref, v_ref, o_ref):
    o_ref[:, :PEER_WORDS] = _pack_words(u_ref[...])
    o_ref[:, PEER_WORDS:] = _pack_words(v_ref[...])


def _pack_tables(peer_u, peer_v):
    n = peer_u.shape[0]
    rows = min(PACK_ROWS, n)
    spec = pl.BlockSpec((rows, D_MODEL), lambda i: (i, 0))
    return pl.pallas_call(
        _pack_tables_kernel,
        grid=(n // rows,),
        in_specs=[spec, spec],
        out_specs=spec,
        out_shape=jax.ShapeDtypeStruct((n, D_MODEL), jnp.int32),
        compiler_params=pltpu.CompilerParams(
            dimension_semantics=("parallel",), vmem_limit_bytes=VMEM_LIMIT),
        name="pack_tables",
    )(peer_u, peer_v)


def _peer(idx, h_words, gates, uv_words, after):
    T = h_words.shape[0]
    assert T % (2 * SC_WORKERS) == 0
    mesh = plsc.VectorSubcoreMesh(core_axis_name="c", subcore_axis_name="s",
                                  num_cores=SC_CORES, num_subcores=SC_SUBCORES)
    return pl.kernel(
        _peer_sc_body,
        out_type=jax.ShapeDtypeStruct((T, D_MODEL), F32),
        mesh=mesh,
        scratch_types=[
            pltpu.VMEM((2, PEER_SEL), jnp.int32), pltpu.VMEM((2, PEER_SEL), F32),
            pltpu.VMEM((2, PEER_WORDS), jnp.int32),
            pltpu.VMEM((PEER_RING, PEER_CH, 2 * PEER_WORDS), jnp.int32),
            pltpu.VMEM((2, D_MODEL), F32),
            pltpu.SemaphoreType.DMA((PEER_RING,)),
            pltpu.SemaphoreType.DMA((2,)), pltpu.SemaphoreType.DMA((2,)),
        ],
        compiler_params=pltpu.CompilerParams(needs_layout_passes=False),
        name="peer_sc",
    )(idx, gates, h_words, uv_words, after)


FINAL_TS = 256


def _final_kernel(x1_ref, pe_ref, p_ref, gp_ref, wg_ref, wp_ref, gf_ref, o_ref):
    x2 = x1_ref[...] + pe_ref[...]
    e = _dot(p_ref[...].astype(BF16), wp_ref[...])
    gate = jax.nn.sigmoid(_dot(_rms(x2, gp_ref[...]).astype(BF16), wg_ref[...]))
    o_ref[...] = _rms(x2 + gate * e, gf_ref[...])


def _final(x1, peer_out, p, t0, nt, g_ple, ple_w_gate, ple_w_proj, g_final):
    B = p.shape[0]
    ts = min(FINAL_TS, nt)
    nblk = nt // ts
    i0 = t0 // ts
    row = lambda d: pl.BlockSpec((ts, d), lambda b, i: (b * nblk + i, 0))
    full = lambda shape: pl.BlockSpec(shape, lambda b, i: (0,) * len(shape))
    return pl.pallas_call(
        _final_kernel,
        grid=(B, nblk),
        in_specs=[row(D_MODEL), row(D_MODEL),
                  pl.BlockSpec((None, ts, D_PLE), lambda b, i: (b, i0 + i, 0)),
                  full((1, D_MODEL)), full((D_MODEL, D_MODEL)), full((D_PLE, D_MODEL)),
                  full((1, D_MODEL))],
        out_specs=pl.BlockSpec((None, ts, D_MODEL), lambda b, i: (b, i, 0)),
        out_shape=jax.ShapeDtypeStruct((B, nt, D_MODEL), F32),
        compiler_params=pltpu.CompilerParams(
            dimension_semantics=("parallel", "parallel"), vmem_limit_bytes=VMEM_LIMIT),
        name="final",
    )(x1, peer_out, p, g_ple, ple_w_gate, ple_w_proj, g_final)


CHUNK_STEPS = (512, 512, 512, 512, 1024, 1024, 1024, 1024, 1024, 512, 512)


def kernel(x, p, positions, g_mix, w_in, ssm_log_dt, ssm_a_re, ssm_a_im, ssm_b_re, ssm_b_im,
           ssm_c_re, ssm_c_im, ssm_d, ssm_w_glu, w_proj_ssm, w_proj_att, w_out, g_ffn,
           peer_w_q, peer_keys1, peer_keys2, peer_u, peer_v, g_ple, ple_w_gate, ple_w_proj,
           g_final):
    B, S, _ = x.shape
    assert w_in.shape[0] == 1, "the final rmsnorm is fused into the single layer's last stage"
    steps = CHUNK_STEPS if sum(CHUNK_STEPS) == S else (S,)
    i = 0
    tables = _s5_tables(ssm_log_dt[i], ssm_a_re[i], ssm_a_im[i], ssm_b_re[i], ssm_b_im[i],
                        ssm_c_re[i], ssm_c_im[i])
    w_in_b, w_glu_b = w_in[i].astype(BF16), ssm_w_glu[i].astype(BF16)
    d_skip = ssm_d[i].reshape(1, D_SSM).astype(F32)
    merge_w = (w_proj_ssm[i].astype(BF16), w_proj_att[i].astype(BF16), w_out[i].astype(BF16),
               g_ffn[i].reshape(1, D_MODEL), peer_w_q[i].astype(BF16), peer_keys1[i], peer_keys2[i])
    final_w = (g_ple[i].reshape(1, D_MODEL), ple_w_gate[i].astype(BF16),
               ple_w_proj[i].astype(BF16), g_final.reshape(1, D_MODEL))
    uv_words = _pack_tables(peer_u[i], peer_v[i])
    k_all = jnp.zeros((B, S, D_ATT), BF16)
    v_all = jnp.zeros((B, S, D_ATT), BF16)
    carry = jnp.zeros((2, SUBLANES, D_STATE), F32)
    outs = []
    t0 = 0
    after = (carry, carry)
    peer_prev = carry
    for nt in steps:
        u_sb, q, k, v, ga, gb = _in_proj(x, positions, g_mix[i], w_in_b, t0, nt, after)
        k_all = lax.dynamic_update_slice(k_all, k, (0, t0, 0))
        v_all = lax.dynamic_update_slice(v_all, v, (0, t0, 0))
        ys, carry = _s5(u_sb, carry, tables, d_skip, w_glu_b, B)
        att = _moba(q, k_all, v_all, t0 // MOBA_BLOCK)
        x1, h_words, idx, gates = _merge(x, ys, att, ga, gb, t0, *merge_w)
        after = (gates, outs[-3] if len(outs) > 2 else carry)
        peer_out = _peer(idx, h_words, gates, uv_words, peer_prev)
        peer_prev = peer_out
        outs.append(_final(x1, peer_out, p[i], t0, nt, *final_w))
        t0 += nt
    return jnp.concatenate(outs, axis=1)
```

```python
import functools

import jax
import jax.numpy as jnp
from jax import lax
from jax.experimental import pallas as pl
from jax.experimental.pallas import tpu as pltpu
from jax.experimental.pallas import tpu_sc as plsc

F32 = jnp.float32
BF16 = jnp.bfloat16

D_MODEL = 1024
D_SSM = 512
SSM_GROUPS = 32
SSM_STATE = 64
D_STATE = SSM_GROUPS * SSM_STATE
HEAD_DIM = 64
D_ATT = 512
ROT_DIM = 16
ROPE_THETA = 500000.0
MOBA_BLOCK = 256
MOBA_TOPK = 3
PEER_HEADS = 8
PEER_KEYS = 128
PEER_QDIM = 256
PEER_HALF = 128
PEER_TOPK = 16
PEER_SEL = PEER_HEADS * PEER_TOPK
D_PLE = 256
EPS = 1e-6
NEG = -1e30
LANES = 128
SUBLANES = 8
VMEM_LIMIT = 48 * 1024 * 1024
HIGHEST = lax.Precision.HIGHEST


def _rms(x, g):
    return x * lax.rsqrt(jnp.mean(x * x, axis=-1, keepdims=True) + EPS) * g


def _dot(a, b):
    return jnp.dot(a, b, preferred_element_type=F32)


def _dot_nt(a, b, precision=None):
    return lax.dot_general(a, b, (((1,), (1,)), ((), ())), precision=precision,
                           preferred_element_type=F32)


IN_TS = 512


def _in_proj_kernel(x_ref, pos_ref, g_ref, w_ref, invf_ref, after_a, after_b,
                    u_ref, q_ref, k_ref, v_ref, ga_ref, gb_ref):
    del after_a, after_b
    h = _rms(x_ref[...], g_ref[...]).astype(BF16)

    def proj(lo, hi):
        return _dot(h, w_ref[:, lo:hi])

    u_ref[...] = proj(0, D_SSM).astype(BF16)
    ang = pos_ref[...].astype(F32) * invf_ref[...]
    cos = jnp.cos(ang)
    sin = jnp.sin(ang)
    lane = lax.broadcasted_iota(jnp.int32, (1, LANES), 1) % HEAD_DIM
    half = ROT_DIM // 2
    sin_hi = jnp.where((lane >= half) & (lane < ROT_DIM), sin, 0.0)
    sin_lo = jnp.where(lane < half, -sin, 0.0)
    reps = D_ATT // LANES
    cos4 = jnp.concatenate([cos] * reps, axis=1)
    sin_hi4 = jnp.concatenate([sin_hi] * reps, axis=1)
    sin_lo4 = jnp.concatenate([sin_lo] * reps, axis=1)

    def rope(t):
        return (t * cos4 + pltpu.roll(t, half, 1) * sin_hi4
                + pltpu.roll(t, D_ATT - half, 1) * sin_lo4)

    q = rope(proj(D_SSM, D_SSM + D_ATT))
    q_ref[...] = (q * (HEAD_DIM ** -0.5)).astype(BF16)
    k_ref[...] = rope(proj(D_SSM + D_ATT, D_SSM + 2 * D_ATT)).astype(BF16)
    v_ref[...] = proj(D_SSM + 2 * D_ATT, D_SSM + 3 * D_ATT).astype(BF16)
    o = D_SSM + 3 * D_ATT
    ga_ref[...] = jax.nn.sigmoid(proj(o, o + D_MODEL)).astype(BF16)
    gb_ref[...] = jax.nn.sigmoid(proj(o + D_MODEL, o + 2 * D_MODEL)).astype(BF16)


def _in_proj(x, positions, g_mix, w_in, t0, nt, after):
    B, S, _ = x.shape
    ts = min(IN_TS, nt)
    assert nt % ts == 0 and t0 % ts == 0
    i0 = t0 // ts
    inv_freq = ROPE_THETA ** (-jnp.arange(0, ROT_DIM, 2, dtype=F32) / ROT_DIM)
    lane = jnp.arange(LANES) % HEAD_DIM
    invf = jnp.where(lane < ROT_DIM, inv_freq[lane % (ROT_DIM // 2)], 0.0).reshape(1, LANES)
    d_in = w_in.shape[1]
    src = lambda d: pl.BlockSpec((None, ts, d), lambda b, i: (b, i0 + i, 0))
    tok = lambda d: pl.BlockSpec((None, ts, d), lambda b, i: (b, i, 0))
    full = lambda shape: pl.BlockSpec(shape, lambda b, i: (0,) * len(shape))
    return pl.pallas_call(
        _in_proj_kernel,
        grid=(B, nt // ts),
        in_specs=[src(D_MODEL), src(1), full((1, D_MODEL)), full((D_MODEL, d_in)), full((1, LANES)),
                  pl.BlockSpec(memory_space=pl.ANY), pl.BlockSpec(memory_space=pl.ANY)],
        out_specs=[pl.BlockSpec((ts, D_SSM), lambda b, i: (i, b)),
                   tok(D_ATT), tok(D_ATT), tok(D_ATT), tok(D_MODEL), tok(D_MODEL)],
        out_shape=[jax.ShapeDtypeStruct((nt, B * D_SSM), BF16),
                   jax.ShapeDtypeStruct((B, nt, D_ATT), BF16),
                   jax.ShapeDtypeStruct((B, nt, D_ATT), BF16),
                   jax.ShapeDtypeStruct((B, nt, D_ATT), BF16),
                   jax.ShapeDtypeStruct((B, nt, D_MODEL), BF16),
                   jax.ShapeDtypeStruct((B, nt, D_MODEL), BF16)],
        compiler_params=pltpu.CompilerParams(
            dimension_semantics=("parallel", "parallel"), vmem_limit_bytes=VMEM_LIMIT),
        name="in_proj",
    )(x, positions.reshape(B, S, 1), g_mix.reshape(1, D_MODEL), w_in, invf, *after)


S5_TS = 128
S5_BATCH = 4
S5_COLS = 512


def _s5_kernel(u_ref, c0_ref, bre_ref, bim_ref, a1r_ref, a1i_ref, pr_ref, pi_ref,
               cre_ref, cim_ref, d_ref, wglu_ref, y_ref, c1_ref,
               xr, xi, cr, ci, ysc):
    rows = xr.shape[0]
    ts = rows // S5_BATCH

    @pl.when(pl.program_id(0) == 0)
    def _():
        cr[...] = c0_ref[0]
        ci[...] = c0_ref[1]

    u = u_ref[...]
    for cb in range(D_STATE // S5_COLS):
        sl = slice(cb * S5_COLS, (cb + 1) * S5_COLS)
        u_cb = u[:, cb * LANES:(cb + 1) * LANES]
        xr[:, sl] = _dot(u_cb, bre_ref[cb])
        xi[:, sl] = _dot(u_cb, bim_ref[cb])

    hi_rows = lax.broadcasted_iota(jnp.int32, (SUBLANES, S5_COLS), 0) >= S5_BATCH
    for cb in range(D_STATE // S5_COLS):
        sl = slice(cb * S5_COLS, (cb + 1) * S5_COLS)
        a_r, a_i = a1r_ref[:, sl], a1i_ref[:, sl]
        p_r, p_i = pr_ref[:, sl], pi_ref[:, sl]

        def body(t, carry):
            c_r, c_i = carry
            r0 = pl.multiple_of(t * SUBLANES, SUBLANES)
            x_r = xr[pl.ds(r0, SUBLANES), sl]
            x_i = xi[pl.ds(r0, SUBLANES), sl]
            s_r = pltpu.roll(x_r, S5_BATCH, 0)
            s_i = pltpu.roll(x_i, S5_BATCH, 0)
            h_r = x_r + (a_r * s_r - a_i * s_i) + (p_r * c_r - p_i * c_i)
            h_i = x_i + (a_r * s_i + a_i * s_r) + (p_r * c_i + p_i * c_r)
            xr[pl.ds(r0, SUBLANES), sl] = h_r
            xi[pl.ds(r0, SUBLANES), sl] = h_i
            n_r = jnp.where(hi_rows, h_r, pltpu.roll(h_r, S5_BATCH, 0))
            n_i = jnp.where(hi_rows, h_i, pltpu.roll(h_i, S5_BATCH, 0))
            return n_r, n_i

        c_r, c_i = lax.fori_loop(0, rows // SUBLANES, body, (cr[:, sl], ci[:, sl]), unroll=2)
        cr[:, sl] = c_r
        ci[:, sl] = c_i

    y = jnp.concatenate(
        [_dot(xr[:, cb * S5_COLS:(cb + 1) * S5_COLS].astype(BF16), cre_ref[cb])
         - _dot(xi[:, cb * S5_COLS:(cb + 1) * S5_COLS].astype(BF16), cim_ref[cb])
         for cb in range(D_STATE // S5_COLS)], axis=1) + d_ref[...] * u.astype(F32)
    y = jax.nn.gelu(y)
    y = y * jax.nn.sigmoid(_dot(y.astype(BF16), wglu_ref[...]))
    for c in range(D_SSM // LANES):
        ysc[c] = y[:, c * LANES:(c + 1) * LANES]
    for b in range(S5_BATCH):
        for c in range(D_SSM // LANES):
            y_ref[b, :, c * LANES:(c + 1) * LANES] = (
                ysc[c, pl.ds(b, ts, stride=S5_BATCH), :].astype(BF16))

    @pl.when(pl.program_id(0) == pl.num_programs(0) - 1)
    def _():
        c1_ref[0] = cr[...]
        c1_ref[1] = ci[...]


def _s5_tables(log_dt, a_re, a_im, b_re, b_im, c_re, c_im):
    dt = jnp.exp(log_dt.astype(F32))[:, None]
    ar, ai = a_re.astype(F32), a_im.astype(F32)
    mag = jnp.exp(dt * ar)
    abar_re, abar_im = mag * jnp.cos(dt * ai), mag * jnp.sin(dt * ai)
    den = ar * ar + ai * ai
    nr, ni = abar_re - 1.0, abar_im
    f_re = (nr * ar + ni * ai) / den
    f_im = (ni * ar - nr * ai) / den
    br, bi = b_re.astype(F32), b_im.astype(F32)
    bb_re = f_re[..., None] * br - f_im[..., None] * bi
    bb_im = f_re[..., None] * bi + f_im[..., None] * br
    eye = jnp.eye(SSM_GROUPS, dtype=F32)

    def in_blockdiag(bb):
        return jnp.einsum('gnc,gh->gchn', bb, eye).reshape(D_SSM, D_STATE)

    def out_blockdiag(c):
        return jnp.einsum('gcn,gh->gnhc', c.astype(F32), eye).reshape(D_STATE, D_SSM)

    a_r = abar_re.reshape(1, D_STATE)
    a_i = abar_im.reshape(1, D_STATE)
    a2_r = a_r * a_r - a_i * a_i
    a2_i = 2.0 * a_r * a_i
    hi = (jnp.arange(SUBLANES) >= S5_BATCH)[:, None]
    a1r = jnp.where(hi, a_r, 0.0)
    a1i = jnp.where(hi, a_i, 0.0)
    p_r = jnp.where(hi, a2_r, a_r)
    p_i = jnp.where(hi, a2_i, a_i)
    nblk = D_STATE // S5_COLS
    cw = D_SSM // nblk

    def in_blocks(m):
        return jnp.stack([m[b * cw:(b + 1) * cw, b * S5_COLS:(b + 1) * S5_COLS] for b in range(nblk)])

    def out_blocks(m):
        return jnp.stack([m[b * S5_COLS:(b + 1) * S5_COLS, b * cw:(b + 1) * cw] for b in range(nblk)])

    return (in_blocks(in_blockdiag(bb_re)).astype(BF16), in_blocks(in_blockdiag(bb_im)).astype(BF16),
            a1r, a1i, p_r, p_i,
            out_blocks(out_blockdiag(c_re)).astype(BF16), out_blocks(out_blockdiag(c_im)).astype(BF16))


def _s5(u_sb, carry, tables, d_skip, w_glu, B):
    assert B == S5_BATCH
    nt = u_sb.shape[0]
    ts = min(S5_TS, nt)
    rows = ts * B
    bre, bim, a1r, a1i, p_r, p_i, cre, cim = tables
    full = lambda shape: pl.BlockSpec(shape, lambda i: (0,) * len(shape))
    return pl.pallas_call(
        _s5_kernel,
        grid=(nt // ts,),
        in_specs=[pl.BlockSpec((rows, D_SSM), lambda i: (i, 0)),
                  full((2, SUBLANES, D_STATE)),
                  full(bre.shape), full(bim.shape),
                  full((SUBLANES, D_STATE)), full((SUBLANES, D_STATE)),
                  full((SUBLANES, D_STATE)), full((SUBLANES, D_STATE)),
                  full(cre.shape), full(cim.shape),
                  full((1, D_SSM)), full((D_SSM, D_SSM))],
        out_specs=[pl.BlockSpec((B, ts, D_SSM), lambda i: (0, i, 0)),
                   full((2, SUBLANES, D_STATE))],
        out_shape=[jax.ShapeDtypeStruct((B, nt, D_SSM), BF16),
                   jax.ShapeDtypeStruct((2, SUBLANES, D_STATE), F32)],
        scratch_shapes=[pltpu.VMEM((rows, D_STATE), F32), pltpu.VMEM((rows, D_STATE), F32),
                        pltpu.VMEM((SUBLANES, D_STATE), F32), pltpu.VMEM((SUBLANES, D_STATE), F32),
                        pltpu.VMEM((D_SSM // LANES, rows, LANES), F32)],
        compiler_params=pltpu.CompilerParams(
            dimension_semantics=("arbitrary",), vmem_limit_bytes=VMEM_LIMIT),
        name="s5",
    )(u_sb.reshape(nt * B, D_SSM), carry, bre, bim, a1r, a1i, p_r, p_i, cre, cim, d_skip, w_glu)


MOBA_PAIR = 2 * MOBA_BLOCK


def _moba_kernel(q0, q_ref, k_ref, v_ref, o_ref, kmean, kaug_a, kaug_b, vaug_a, vaug_b, qaug,
                 m_s, acc_s, s_buf):
    last = pl.program_id(2) + q0 // 2
    nb = k_ref.shape[0] // MOBA_BLOCK
    nbp = kmean.shape[0]
    lane = lax.broadcasted_iota(jnp.int32, (1, LANES), 1)
    head_a = lane < HEAD_DIM

    @pl.when(pl.program_id(2) == 0)
    def _():
        kmean[...] = jnp.zeros_like(kmean)
        for j in range(nb):
            rows = pl.ds(j * MOBA_BLOCK, MOBA_BLOCK)
            kj = k_ref[rows, :].astype(F32)
            vj = v_ref[rows, :].astype(F32)
            kmean[j:j + 1, :] = jnp.sum(kj, axis=0, keepdims=True) * (1.0 / MOBA_BLOCK)
            kaug_a[rows, :] = jnp.where(head_a, kj, jnp.where(lane - HEAD_DIM == j, 1.0, 0.0)).astype(BF16)
            kaug_b[rows, :] = jnp.where(head_a, jnp.where(lane == j, 1.0, 0.0), kj).astype(BF16)
            vaug_a[rows, :] = jnp.where(head_a, vj, 1.0).astype(BF16)
            vaug_b[rows, :] = jnp.where(head_a, 1.0, vj).astype(BF16)
        blk_row = lax.broadcasted_iota(jnp.int32, (nbp, MOBA_BLOCK), 0)
        for t in range(q_ref.shape[0] // MOBA_BLOCK):
            qt = q0 + t
            qf = q_ref[t * MOBA_BLOCK:(t + 1) * MOBA_BLOCK, :].astype(F32)
            for hd, is_a in enumerate((True, False)):
                mine = head_a if is_a else jnp.logical_not(head_a)
                q_own = jnp.where(mine, qf, 0.0)
                g = _dot_nt(kmean[...], q_own, precision=HIGHEST)
                g = jnp.where(blk_row < qt, g, NEG)
                sel = jnp.zeros(g.shape, F32)
                for _ in range(MOBA_TOPK):
                    m = jnp.max(g, axis=0, keepdims=True)
                    idx = jnp.min(jnp.where(g == m, blk_row, nbp), axis=0, keepdims=True)
                    hit = blk_row == idx
                    sel = jnp.where(hit, jnp.where(idx < qt, 1.0, 0.0), sel)
                    g = jnp.where(hit, -jnp.inf, g)
                bias_t = jnp.where(sel > 0.0, 0.0, jnp.where(blk_row == qt, 0.0, NEG))
                bias_t = jnp.concatenate([bias_t, jnp.full((LANES - nbp, MOBA_BLOCK), NEG, F32)], axis=0)
                bias = jnp.transpose(bias_t)
                if is_a:
                    bias = pltpu.roll(bias, HEAD_DIM, 1)
                qaug[hd, t * MOBA_BLOCK:(t + 1) * MOBA_BLOCK, :] = jnp.where(mine, qf, bias).astype(BF16)

    tile_rows = pl.ds(pl.multiple_of(pl.program_id(2) * MOBA_PAIR, MOBA_PAIR), MOBA_PAIR)
    q_augs = [qaug[0, tile_rows, :], qaug[1, tile_rows, :]]

    m_s[...] = jnp.full(m_s.shape, -jnp.inf, F32)
    acc_s[...] = jnp.zeros_like(acc_s)
    qpos = last * MOBA_PAIR + lax.broadcasted_iota(jnp.int32, (MOBA_PAIR, MOBA_PAIR), 0)
    col = lax.broadcasted_iota(jnp.int32, (MOBA_PAIR, MOBA_PAIR), 1)

    def kv_rows(jj):
        return pl.ds(pl.multiple_of(jj * MOBA_PAIR, MOBA_PAIR), MOBA_PAIR)

    def scores(jj, slot):
        for hd, kaug in enumerate((kaug_a, kaug_b)):
            s_buf[slot, hd] = _dot_nt(q_augs[hd], kaug[kv_rows(jj), :])

    def softmax_pv(jj, slot, causal):
        for hd, vaug in enumerate((vaug_a, vaug_b)):
            s = s_buf[slot, hd]
            if causal:
                s = jnp.where(jj * MOBA_PAIR + col <= qpos, s, NEG)
            m_old = m_s[hd]
            m_new = jnp.maximum(m_old, jnp.max(s, axis=-1, keepdims=True))
            alpha = jnp.exp(m_old - m_new)
            p = jnp.exp(s - m_new)
            m_s[hd] = m_new
            acc_s[hd] = alpha * acc_s[hd] + _dot(p.astype(BF16), vaug[kv_rows(jj), :])

    scores(0, 0)

    def body(k, _):
        scores(2 * k + 1, 1)
        softmax_pv(2 * k, 0, False)
        scores(2 * k + 2, 0)
        softmax_pv(2 * k + 1, 1, False)
        return 0

    lax.fori_loop(0, last // 2, body, 0)

    @pl.when(last % 2 == 0)
    def _():
        softmax_pv(last, 0, True)

    @pl.when(last % 2 == 1)
    def _():
        scores(last, 1)
        softmax_pv(last - 1, 0, False)
        softmax_pv(last, 1, True)
    acc_a, acc_b = acc_s[0], acc_s[1]
    o_ref[...] = jnp.where(head_a, acc_a / pltpu.roll(acc_a, HEAD_DIM, 1),
                           acc_b / pltpu.roll(acc_b, HEAD_DIM, 1)).astype(BF16)


def _moba(q, k, v, q0):
    B = q.shape[0]
    nq = q.shape[1] // MOBA_BLOCK
    skv = (q0 + nq) * MOBA_BLOCK
    nb = skv // MOBA_BLOCK
    assert nb <= HEAD_DIM and nb % 2 == 0 and skv <= k.shape[1]
    nbp = -(-nb // SUBLANES) * SUBLANES
    assert q0 % 2 == 0 and nq % 2 == 0
    blk = pl.BlockSpec((None, MOBA_PAIR, LANES), lambda b, h, i: (b, i, h))
    seq = pl.BlockSpec((None, skv, LANES), lambda b, h, i: (b, 0, h))
    return pl.pallas_call(
        functools.partial(_moba_kernel, q0),
        grid=(B, D_ATT // LANES, nq // 2),
        in_specs=[pl.BlockSpec((None, nq * MOBA_BLOCK, LANES), lambda b, h, i: (b, 0, h)), seq, seq],
        out_specs=blk,
        out_shape=jax.ShapeDtypeStruct(q.shape, BF16),
        scratch_shapes=[pltpu.VMEM((nbp, LANES), F32),
                        pltpu.VMEM((skv, LANES), BF16), pltpu.VMEM((skv, LANES), BF16),
                        pltpu.VMEM((skv, LANES), BF16), pltpu.VMEM((skv, LANES), BF16),
                        pltpu.VMEM((2, nq * MOBA_BLOCK, LANES), BF16),
                        pltpu.VMEM((2, MOBA_PAIR, 1), F32),
                        pltpu.VMEM((2, MOBA_PAIR, LANES), F32),
                        pltpu.VMEM((2, 2, MOBA_PAIR, MOBA_PAIR), F32)],
        compiler_params=pltpu.CompilerParams(
            dimension_semantics=("parallel", "parallel", "arbitrary"), vmem_limit_bytes=VMEM_LIMIT),
        name="moba",
    )(q, k, v)


MERGE_TS = 256


def _bf16_bits(x):
    b = pltpu.bitcast(x, jnp.int32)
    r = b + 0x7FFF + (lax.shift_right_logical(b, 16) & 1)
    return lax.shift_right_logical(r, 16)


def _merge_kernel(x_ref, ys_ref, at_ref, ga_ref, gb_ref, wa_ref, wb_ref, wo_ref, g_ref,
                  wq_ref, k1_ref, k2_ref, x1_ref, hw_ref, idx_ref, gate_ref, sc_ref):
    ya = _dot(ys_ref[...], wa_ref[...])
    yb = _dot(at_ref[...], wb_ref[...])
    merged = ga_ref[...].astype(F32) * ya + gb_ref[...].astype(F32) * yb
    x1 = x_ref[...] + _dot(merged.astype(BF16), wo_ref[...])
    x1_ref[...] = x1
    hq = _rms(x1, g_ref[...])
    hw_ref[...] = _pack_words(hq)
    qp = _dot(hq.astype(BF16), wq_ref[...])
    for h in range(PEER_HEADS):
        o = h * PEER_QDIM
        sc_ref[2 * h] = _dot_nt(k1_ref[h], qp[:, o:o + PEER_HALF], precision=HIGHEST)
        sc_ref[2 * h + 1] = _dot_nt(k2_ref[h], qp[:, o + PEER_HALF:o + PEER_QDIM], precision=HIGHEST)
    _topk_kernel(sc_ref, idx_ref, gate_ref)


def _merge(x, ys, att, ga, gb, t0, w_proj_ssm, w_proj_att, w_out, g_ffn, peer_w_q, keys1, keys2):
    B, nt = ys.shape[0], ys.shape[1]
    ts = min(MERGE_TS, nt)
    nblk = nt // ts
    i0 = t0 // ts
    tok = lambda d: pl.BlockSpec((None, ts, d), lambda b, i: (b, i, 0))
    row = lambda d: pl.BlockSpec((ts, d), lambda b, i: (b * nblk + i, 0))
    full = lambda shape: pl.BlockSpec(shape, lambda b, i: (0,) * len(shape))
    qd = PEER_HEADS * PEER_QDIM
    return pl.pallas_call(
        _merge_kernel,
        grid=(B, nblk),
        in_specs=[pl.BlockSpec((None, ts, D_MODEL), lambda b, i: (b, i0 + i, 0)),
                  tok(D_SSM), tok(D_ATT), tok(D_MODEL), tok(D_MODEL),
                  full((D_SSM, D_MODEL)), full((D_ATT, D_MODEL)), full((D_MODEL, D_MODEL)),
                  full((1, D_MODEL)), full((D_MODEL, qd)),
                  full((PEER_HEADS, PEER_KEYS, PEER_HALF)), full((PEER_HEADS, PEER_KEYS, PEER_HALF))],
        out_specs=[row(D_MODEL), row(D_MODEL // 2), row(PEER_SEL), row(PEER_SEL)],
        out_shape=[jax.ShapeDtypeStruct((B * nt, D_MODEL), F32),
                   jax.ShapeDtypeStruct((B * nt, D_MODEL // 2), jnp.int32),
                   jax.ShapeDtypeStruct((B * nt, PEER_SEL), jnp.int32),
                   jax.ShapeDtypeStruct((B * nt, PEER_SEL), F32)],
        scratch_shapes=[pltpu.VMEM((2 * PEER_HEADS, PEER_KEYS, ts), F32)],
        compiler_params=pltpu.CompilerParams(
            dimension_semantics=("parallel", "parallel"), vmem_limit_bytes=VMEM_LIMIT),
        name="merge",
    )(x, ys, att, ga, gb, w_proj_ssm, w_proj_att, w_out, g_ffn, peer_w_q, keys1, keys2)


def _top_rows(s, row, k):
    vals, idxs = [], []
    for _ in range(k):
        m = jnp.max(s, axis=0, keepdims=True)
        idx = jnp.min(jnp.where(s == m, row, s.shape[0]), axis=0, keepdims=True)
        vals.append(m)
        idxs.append(idx)
        s = jnp.where(row == idx, -jnp.inf, s)
    return vals, idxs


def _stack_rows(rows, row16):
    acc = jnp.zeros(row16.shape, rows[0].dtype)
    for r, v in enumerate(rows):
        acc = jnp.where(row16 == r, v, acc)
    return acc


def _topk_kernel(sc_ref, idx_ref, gate_ref):
    ts = sc_ref.shape[-1]
    row = lax.broadcasted_iota(jnp.int32, (PEER_KEYS, ts), 0).astype(F32)
    row16 = lax.broadcasted_iota(jnp.int32, (PEER_TOPK, ts), 0)
    row8 = lax.broadcasted_iota(jnp.int32, (SUBLANES, ts), 0)
    counts = [PEER_TOPK // (i + 1) for i in range(PEER_TOPK)]
    heights = [PEER_TOPK if c > SUBLANES else SUBLANES for c in counts]
    n_cand = sum(heights)
    rowc = lax.broadcasted_iota(jnp.int32, (n_cand, ts), 0).astype(F32)
    gate_rows, eid_rows = [], []
    for h in range(PEER_HEADS):
        v1, i1 = _top_rows(sc_ref[2 * h], row, PEER_TOPK)
        v2, i2 = _top_rows(sc_ref[2 * h + 1], row, PEER_TOPK)
        v2s = _stack_rows(v2, row16)
        i2s = _stack_rows(i2, row16)
        cand, eid = [], []
        for i in range(PEER_TOPK):
            n = heights[i]
            cand.append(jnp.where((row16 if n == PEER_TOPK else row8) < counts[i],
                                  v1[i] + v2s[:n], -jnp.inf))
            eid.append(i1[i] * PEER_KEYS + i2s[:n])
        cand = jnp.concatenate(cand, axis=0)
        eid = jnp.concatenate(eid, axis=0)
        tops, picks = [], []
        for _ in range(PEER_TOPK):
            m = jnp.max(cand, axis=0, keepdims=True)
            pos = jnp.min(jnp.where(cand == m, rowc, n_cand), axis=0, keepdims=True)
            hit = rowc == pos
            picks.append(jnp.max(jnp.where(hit, eid, -1.0), axis=0, keepdims=True))
            tops.append(m)
            cand = jnp.where(hit, -jnp.inf, cand)
        top = _stack_rows(tops, row16)
        p = jnp.exp(top - jnp.max(top, axis=0, keepdims=True))
        gate_rows.append(p / jnp.sum(p, axis=0, keepdims=True))
        eid_rows.append(_stack_rows(picks, row16))
    gate_ref[...] = jnp.transpose(jnp.concatenate(gate_rows, axis=0))
    idx_ref[...] = jnp.transpose(jnp.concatenate(eid_rows, axis=0)).astype(jnp.int32)


SC_CORES = 2
SC_SUBCORES = 16
SC_LANES = 16
SC_WORKERS = SC_CORES * SC_SUBCORES
PEER_CH = SC_LANES
PEER_NCH = PEER_SEL // PEER_CH
PEER_WORDS = D_MODEL // 2
PEER_NWG = PEER_WORDS // SC_LANES
PEER_RING = 4
PEER_QUAD = 4
HI_MASK = -65536
GELU_C = 0.7978845608028654


def _gelu_tanh_via_exp(x):
    z = GELU_C * (x + 0.044715 * (x * x * x))
    t = 1.0 - 2.0 / (jnp.exp(2.0 * z) + 1.0)
    return 0.5 * x * (1.0 + t)


def _unpack_pair(w):
    lo = plsc.bitcast(lax.shift_left(w, 16), F32)
    hi = plsc.bitcast(lax.bitwise_and(w, HI_MASK), F32)
    return lo, hi


def _peer_sc_body(idx_hbm, gate_hbm, h_hbm, uv_hbm, after_hbm, o_hbm,
                  idx_v, gate_v, h_v, buf, out_v, gsem, msem, osem):
    n_tok = o_hbm.shape[0] // SC_WORKERS
    base = (lax.axis_index("s") * SC_CORES + lax.axis_index("c")) * n_tok
    lane = lax.iota(jnp.int32, SC_LANES)
    zero_rows = jnp.zeros((SC_LANES,), jnp.int32)

    def meta_copies(tok, s):
        return (pltpu.make_async_copy(idx_hbm.at[tok], idx_v.at[s], msem.at[s]),
                pltpu.make_async_copy(gate_hbm.at[tok], gate_v.at[s], msem.at[s]),
                pltpu.make_async_copy(h_hbm.at[tok], h_v.at[s], msem.at[s]))

    def gather(slot, rows):
        return pltpu.make_async_copy(uv_hbm.at[rows], buf.at[slot], gsem.at[slot])

    def token(t, carry):
        s = t % 2
        tok = base + t
        nxt = base + jnp.minimum(t + 1, n_tok - 1)
        for cp in meta_copies(nxt, 1 - s):
            cp.start()

        @pl.when(t >= 2)
        def _():
            pltpu.make_async_copy(out_v.at[s], o_hbm.at[tok], osem.at[s]).wait()

        def chunk(c, carry):
            slot = c % PEER_RING
            gather(slot, zero_rows).wait()

            def dot_step(q, accs):
                cols = [pl.ds(pl.multiple_of((q * PEER_QUAD + j) * SC_LANES, SC_LANES), SC_LANES)
                        for j in range(PEER_QUAD)]
                hs = [plsc.bitcast(h_v[s, col], BF16) for col in cols]
                out = []
                for r in range(PEER_CH):
                    p = plsc.bitcast(buf[slot, r, cols[0]], BF16) * hs[0]
                    for j in range(1, PEER_QUAD):
                        p = p + plsc.bitcast(buf[slot, r, cols[j]], BF16) * hs[j]
                    lo, hi = _unpack_pair(plsc.bitcast(p, jnp.int32))
                    out.append(accs[r] + lo + hi)
                return tuple(out)

            accs = lax.fori_loop(0, PEER_NWG // PEER_QUAD, dot_step,
                                 tuple(jnp.zeros((SC_LANES,), F32) for _ in range(PEER_CH)))
            tot = jnp.zeros((SC_LANES,), F32)
            for r in range(PEER_CH):
                tot = jnp.where(lane == r, jnp.sum(accs[r]), tot)
            rows = pl.ds(pl.multiple_of(c * PEER_CH, PEER_CH), PEER_CH)
            wvec = gate_v[s, rows] * _gelu_tanh_via_exp(tot)
            ws = []
            for r in range(PEER_CH):
                w = wvec.at[jnp.full((SC_LANES,), r, jnp.int32)].get(mode="promise_in_bounds")
                ws.append(plsc.pack(w, w, format=plsc.PackFormat.INTERLEAVED,
                                    preferred_element_type=BF16))
            first = c == 0

            @plsc.parallel_loop(0, PEER_NWG, unroll=2)
            def acc_step(g):
                col = pl.ds(pl.multiple_of(g * SC_LANES, SC_LANES), SC_LANES)
                col_v = pl.ds(pl.multiple_of(PEER_WORDS + g * SC_LANES, SC_LANES), SC_LANES)
                o_lo = jnp.where(first, 0.0, out_v[s, col])
                o_hi = jnp.where(first, 0.0, out_v[s, col_v])
                for r0 in range(0, PEER_CH, PEER_QUAD):
                    p = plsc.bitcast(buf[slot, r0, col_v], BF16) * ws[r0]
                    for r in range(r0 + 1, r0 + PEER_QUAD):
                        p = p + plsc.bitcast(buf[slot, r, col_v], BF16) * ws[r]
                    lo, hi = _unpack_pair(plsc.bitcast(p, jnp.int32))
                    o_lo = o_lo + lo
                    o_hi = o_hi + hi
                out_v[s, col] = o_lo
                out_v[s, col_v] = o_hi

            @pl.when(c == PEER_NCH - PEER_RING)
            def _():
                for cp in meta_copies(nxt, 1 - s):
                    cp.wait()

            ahead = c + PEER_RING
            src = jnp.where(ahead < PEER_NCH, s, 1 - s)
            nrows = idx_v[src, pl.ds(pl.multiple_of((ahead % PEER_NCH) * PEER_CH, PEER_CH), PEER_CH)]
            gather(slot, nrows).start()
            return carry

        lax.fori_loop(0, PEER_NCH, chunk, 0)
        pltpu.make_async_copy(out_v.at[s], o_hbm.at[tok], osem.at[s]).start()
        return carry

    for cp in meta_copies(base, 0):
        cp.start()
    for cp in meta_copies(base, 0):
        cp.wait()
    for c in range(PEER_RING):
        gather(c, idx_v[0, pl.ds(c * PEER_CH, PEER_CH)]).start()
    lax.fori_loop(0, n_tok, token, 0)
    for c in range(PEER_RING):
        gather(c, zero_rows).wait()
    for s in range(2):
        pltpu.make_async_copy(out_v.at[s], o_hbm.at[base], osem.at[s]).wait()


PACK_ROWS = 256


def _pack_words(x):
    half = x.shape[1] // 2
    return _bf16_bits(x[:, :half]) | lax.shift_left(_bf16_bits(x[:, half:]), 16)


def _pack_tables_kernel(u_ref, v_ref, o_ref):
    o_ref[:, :PEER_WORDS] = _pack_words(u_ref[...])
    o_ref[:, PEER_WORDS:] = _pack_words(v_ref[...])


def _pack_tables(peer_u, peer_v):
    n = peer_u.shape[0]
    rows = min(PACK_ROWS, n)
    spec = pl.BlockSpec((rows, D_MODEL), lambda i: (i, 0))
    return pl.pallas_call(
        _pack_tables_kernel,
        grid=(n // rows,),
        in_specs=[spec, spec],
        out_specs=spec,
        out_shape=jax.ShapeDtypeStruct((n, D_MODEL), jnp.int32),
        compiler_params=pltpu.CompilerParams(
            dimension_semantics=("parallel",), vmem_limit_bytes=VMEM_LIMIT),
        name="pack_tables",
    )(peer_u, peer_v)


def _peer(idx, h_words, gates, uv_words, after):
    T = h_words.shape[0]
    assert T % (2 * SC_WORKERS) == 0
    mesh = plsc.VectorSubcoreMesh(core_axis_name="c", subcore_axis_name="s",
                                  num_cores=SC_CORES, num_subcores=SC_SUBCORES)
    return pl.kernel(
        _peer_sc_body,
        out_type=jax.ShapeDtypeStruct((T, D_MODEL), F32),
        mesh=mesh,
        scratch_types=[
            pltpu.VMEM((2, PEER_SEL), jnp.int32), pltpu.VMEM((2, PEER_SEL), F32),
            pltpu.VMEM((2, PEER_WORDS), jnp.int32),
            pltpu.VMEM((PEER_RING, PEER_CH, 2 * PEER_WORDS), jnp.int32),
            pltpu.VMEM((2, D_MODEL), F32),
            pltpu.SemaphoreType.DMA((PEER_RING,)),
            pltpu.SemaphoreType.DMA((2,)), pltpu.SemaphoreType.DMA((2,)),
        ],
        compiler_params=pltpu.CompilerParams(needs_layout_passes=False),
        name="peer_sc",
    )(idx, gates, h_words, uv_words, after)


FINAL_TS = 256


def _final_kernel(x1_ref, pe_ref, p_ref, gp_ref, wg_ref, wp_ref, gf_ref, o_ref):
    x2 = x1_ref[...] + pe_ref[...]
    e = _dot(p_ref[...].astype(BF16), wp_ref[...])
    gate = jax.nn.sigmoid(_dot(_rms(x2, gp_ref[...]).astype(BF16), wg_ref[...]))
    o_ref[...] = _rms(x2 + gate * e, gf_ref[...])


def _final(x1, peer_out, p, t0, nt, g_ple, ple_w_gate, ple_w_proj, g_final):
    B = p.shape[0]
    ts = min(FINAL_TS, nt)
    nblk = nt // ts
    i0 = t0 // ts
    row = lambda d: pl.BlockSpec((ts, d), lambda b, i: (b * nblk + i, 0))
    full = lambda shape: pl.BlockSpec(shape, lambda b, i: (0,) * len(shape))
    return pl.pallas_call(
        _final_kernel,
        grid=(B, nblk),
        in_specs=[row(D_MODEL), row(D_MODEL),
                  pl.BlockSpec((None, ts, D_PLE), lambda b, i: (b, i0 + i, 0)),
                  full((1, D_MODEL)), full((D_MODEL, D_MODEL)), full((D_PLE, D_MODEL)),
                  full((1, D_MODEL))],
        out_specs=pl.BlockSpec((None, ts, D_MODEL), lambda b, i: (b, i, 0)),
        out_shape=jax.ShapeDtypeStruct((B, nt, D_MODEL), F32),
        compiler_params=pltpu.CompilerParams(
            dimension_semantics=("parallel", "parallel"), vmem_limit_bytes=VMEM_LIMIT),
        name="final",
    )(x1, peer_out, p, g_ple, ple_w_gate, ple_w_proj, g_final)


CHUNK_STEPS = (512, 512, 512, 512, 1024, 1024, 1024, 1024, 1024, 512, 512)


def kernel(x, p, positions, g_mix, w_in, ssm_log_dt, ssm_a_re, ssm_a_im, ssm_b_re, ssm_b_im,
           ssm_c_re, ssm_c_im, ssm_d, ssm_w_glu, w_proj_ssm, w_proj_att, w_out, g_ffn,
           peer_w_q, peer_keys1, peer_keys2, peer_u, peer_v, g_ple, ple_w_gate, ple_w_proj,
           g_final):
    B, S, _ = x.shape
    assert w_in.shape[0] == 1, "the final rmsnorm is fused into the single layer's last stage"
    steps = CHUNK_STEPS if sum(CHUNK_STEPS) == S else (S,)
    i = 0
    tables = _s5_tables(ssm_log_dt[i], ssm_a_re[i], ssm_a_im[i], ssm_b_re[i], ssm_b_im[i],
                        ssm_c_re[i], ssm_c_im[i])
    w_in_b, w_glu_b = w_in[i].astype(BF16), ssm_w_glu[i].astype(BF16)
    d_skip = ssm_d[i].reshape(1, D_SSM).astype(F32)
    merge_w = (w_proj_ssm[i].astype(BF16), w_proj_att[i].astype(BF16), w_out[i].astype(BF16),
               g_ffn[i].reshape(1, D_MODEL), peer_w_q[i].astype(BF16), peer_keys1[i], peer_keys2[i])
    final_w = (g_ple[i].reshape(1, D_MODEL), ple_w_gate[i].astype(BF16),
               ple_w_proj[i].astype(BF16), g_final.reshape(1, D_MODEL))
    uv_words = _pack_tables(peer_u[i], peer_v[i])
    k_all = jnp.zeros((B, S, D_ATT), BF16)
    v_all = jnp.zeros((B, S, D_ATT), BF16)
    carry = jnp.zeros((2, SUBLANES, D_STATE), F32)
    outs = []
    t0 = 0
    after = (carry, carry)
    peer_prev = carry
    for nt in steps:
        u_sb, q, k, v, ga, gb = _in_proj(x, positions, g_mix[i], w_in_b, t0, nt, after)
        k_all = lax.dynamic_update_slice(k_all, k, (0, t0, 0))
        v_all = lax.dynamic_update_slice(v_all, v, (0, t0, 0))
        ys, carry = _s5(u_sb, carry, tables, d_skip, w_glu_b, B)
        att = _moba(q, k_all, v_all, t0 // MOBA_BLOCK)
        x1, h_words, idx, gates = _merge(x, ys, att, ga, gb, t0, *merge_w)
        after = (gates, outs[-5] if len(outs) > 4 else carry)
        peer_out = _peer(idx, h_words, gates, uv_words, peer_prev)
        peer_prev = peer_out
        outs.append(_final(x1, peer_out, p[i], t0, nt, *final_w))
        t0 += nt
    return jnp.concatenate(outs, axis=1)
```

```python
import functools

import jax
import jax.numpy as jnp
from jax import lax
from jax.experimental import pallas as pl
from jax.experimental.pallas import tpu as pltpu
from jax.experimental.pallas import tpu_sc as plsc

F32 = jnp.float32
BF16 = jnp.bfloat16

D_MODEL = 1024
D_SSM = 512
SSM_GROUPS = 32
SSM_STATE = 64
D_STATE = SSM_GROUPS * SSM_STATE
HEAD_DIM = 64
D_ATT = 512
ROT_DIM = 16
ROPE_THETA = 500000.0
MOBA_BLOCK = 256
MOBA_TOPK = 3
PEER_HEADS = 8
PEER_KEYS = 128
PEER_QDIM = 256
PEER_HALF = 128
PEER_TOPK = 16
PEER_SEL = PEER_HEADS * PEER_TOPK
D_PLE = 256
EPS = 1e-6
NEG = -1e30
LANES = 128
SUBLANES = 8
VMEM_LIMIT = 48 * 1024 * 1024
HIGHEST = lax.Precision.HIGHEST


def _rms(x, g):
    return x * lax.rsqrt(jnp.mean(x * x, axis=-1, keepdims=True) + EPS) * g


def _dot(a, b):
    return jnp.dot(a, b, preferred_element_type=F32)


def _dot_nt(a, b, precision=None):
    return lax.dot_general(a, b, (((1,), (1,)), ((), ())), precision=precision,
                           preferred_element_type=F32)


IN_TS = 512


def _in_proj_kernel(x_ref, pos_ref, g_ref, w_ref, invf_ref, after_a, after_b,
                    u_ref, q_ref, k_ref, v_ref, ga_ref, gb_ref):
    del after_a, after_b
    h = _rms(x_ref[...], g_ref[...]).astype(BF16)

    def proj(lo, hi):
        return _dot(h, w_ref[:, lo:hi])

    u_ref[...] = proj(0, D_SSM).astype(BF16)
    ang = pos_ref[...].astype(F32) * invf_ref[...]
    cos = jnp.cos(ang)
    sin = jnp.sin(ang)
    lane = lax.broadcasted_iota(jnp.int32, (1, LANES), 1) % HEAD_DIM
    half = ROT_DIM // 2
    sin_hi = jnp.where((lane >= half) & (lane < ROT_DIM), sin, 0.0)
    sin_lo = jnp.where(lane < half, -sin, 0.0)
    reps = D_ATT // LANES
    cos4 = jnp.concatenate([cos] * reps, axis=1)
    sin_hi4 = jnp.concatenate([sin_hi] * reps, axis=1)
    sin_lo4 = jnp.concatenate([sin_lo] * reps, axis=1)

    def rope(t):
        return (t * cos4 + pltpu.roll(t, half, 1) * sin_hi4
                + pltpu.roll(t, D_ATT - half, 1) * sin_lo4)

    q = rope(proj(D_SSM, D_SSM + D_ATT))
    q_ref[...] = (q * (HEAD_DIM ** -0.5)).astype(BF16)
    k_ref[...] = rope(proj(D_SSM + D_ATT, D_SSM + 2 * D_ATT)).astype(BF16)
    v_ref[...] = proj(D_SSM + 2 * D_ATT, D_SSM + 3 * D_ATT).astype(BF16)
    o = D_SSM + 3 * D_ATT
    ga_ref[...] = jax.nn.sigmoid(proj(o, o + D_MODEL)).astype(BF16)
    gb_ref[...] = jax.nn.sigmoid(proj(o + D_MODEL, o + 2 * D_MODEL)).astype(BF16)


def _in_proj(x, positions, g_mix, w_in, t0, nt, after):
    B, S, _ = x.shape
    ts = min(IN_TS, nt)
    assert nt % ts == 0 and t0 % ts == 0
    i0 = t0 // ts
    inv_freq = ROPE_THETA ** (-jnp.arange(0, ROT_DIM, 2, dtype=F32) / ROT_DIM)
    lane = jnp.arange(LANES) % HEAD_DIM
    invf = jnp.where(lane < ROT_DIM, inv_freq[lane % (ROT_DIM // 2)], 0.0).reshape(1, LANES)
    d_in = w_in.shape[1]
    src = lambda d: pl.BlockSpec((None, ts, d), lambda b, i: (b, i0 + i, 0))
    tok = lambda d: pl.BlockSpec((None, ts, d), lambda b, i: (b, i, 0))
    full = lambda shape: pl.BlockSpec(shape, lambda b, i: (0,) * len(shape))
    return pl.pallas_call(
        _in_proj_kernel,
        grid=(B, nt // ts),
        in_specs=[src(D_MODEL), src(1), full((1, D_MODEL)), full((D_MODEL, d_in)), full((1, LANES)),
                  pl.BlockSpec(memory_space=pl.ANY), pl.BlockSpec(memory_space=pl.ANY)],
        out_specs=[pl.BlockSpec((ts, D_SSM), lambda b, i: (i, b)),
                   tok(D_ATT), tok(D_ATT), tok(D_ATT), tok(D_MODEL), tok(D_MODEL)],
        out_shape=[jax.ShapeDtypeStruct((nt, B * D_SSM), BF16),
                   jax.ShapeDtypeStruct((B, nt, D_ATT), BF16),
                   jax.ShapeDtypeStruct((B, nt, D_ATT), BF16),
                   jax.ShapeDtypeStruct((B, nt, D_ATT), BF16),
                   jax.ShapeDtypeStruct((B, nt, D_MODEL), BF16),
                   jax.ShapeDtypeStruct((B, nt, D_MODEL), BF16)],
        compiler_params=pltpu.CompilerParams(
            dimension_semantics=("parallel", "parallel"), vmem_limit_bytes=VMEM_LIMIT),
        name="in_proj",
    )(x, positions.reshape(B, S, 1), g_mix.reshape(1, D_MODEL), w_in, invf, *after)


S5_TS = 128
S5_BATCH = 4
S5_COLS = 512


def _s5_kernel(u_ref, c0_ref, bre_ref, bim_ref, a1r_ref, a1i_ref, pr_ref, pi_ref,
               cre_ref, cim_ref, d_ref, wglu_ref, y_ref, c1_ref,
               xr, xi, cr, ci, ysc):
    rows = xr.shape[0]
    ts = rows // S5_BATCH

    @pl.when(pl.program_id(0) == 0)
    def _():
        cr[...] = c0_ref[0]
        ci[...] = c0_ref[1]

    u = u_ref[...]
    for cb in range(D_STATE // S5_COLS):
        sl = slice(cb * S5_COLS, (cb + 1) * S5_COLS)
        u_cb = u[:, cb * LANES:(cb + 1) * LANES]
        xr[:, sl] = _dot(u_cb, bre_ref[cb])
        xi[:, sl] = _dot(u_cb, bim_ref[cb])

    hi_rows = lax.broadcasted_iota(jnp.int32, (SUBLANES, S5_COLS), 0) >= S5_BATCH
    for cb in range(D_STATE // S5_COLS):
        sl = slice(cb * S5_COLS, (cb + 1) * S5_COLS)
        a_r, a_i = a1r_ref[:, sl], a1i_ref[:, sl]
        p_r, p_i = pr_ref[:, sl], pi_ref[:, sl]

        def body(t, carry):
            c_r, c_i = carry
            r0 = pl.multiple_of(t * SUBLANES, SUBLANES)
            x_r = xr[pl.ds(r0, SUBLANES), sl]
            x_i = xi[pl.ds(r0, SUBLANES), sl]
            s_r = pltpu.roll(x_r, S5_BATCH, 0)
            s_i = pltpu.roll(x_i, S5_BATCH, 0)
            h_r = x_r + (a_r * s_r - a_i * s_i) + (p_r * c_r - p_i * c_i)
            h_i = x_i + (a_r * s_i + a_i * s_r) + (p_r * c_i + p_i * c_r)
            xr[pl.ds(r0, SUBLANES), sl] = h_r
            xi[pl.ds(r0, SUBLANES), sl] = h_i
            n_r = jnp.where(hi_rows, h_r, pltpu.roll(h_r, S5_BATCH, 0))
            n_i = jnp.where(hi_rows, h_i, pltpu.roll(h_i, S5_BATCH, 0))
            return n_r, n_i

        c_r, c_i = lax.fori_loop(0, rows // SUBLANES, body, (cr[:, sl], ci[:, sl]), unroll=2)
        cr[:, sl] = c_r
        ci[:, sl] = c_i

    y = jnp.concatenate(
        [_dot(xr[:, cb * S5_COLS:(cb + 1) * S5_COLS].astype(BF16), cre_ref[cb])
         - _dot(xi[:, cb * S5_COLS:(cb + 1) * S5_COLS].astype(BF16), cim_ref[cb])
         for cb in range(D_STATE // S5_COLS)], axis=1) + d_ref[...] * u.astype(F32)
    y = jax.nn.gelu(y)
    y = y * jax.nn.sigmoid(_dot(y.astype(BF16), wglu_ref[...]))
    for c in range(D_SSM // LANES):
        ysc[c] = y[:, c * LANES:(c + 1) * LANES]
    for b in range(S5_BATCH):
        for c in range(D_SSM // LANES):
            y_ref[b, :, c * LANES:(c + 1) * LANES] = (
                ysc[c, pl.ds(b, ts, stride=S5_BATCH), :].astype(BF16))

    @pl.when(pl.program_id(0) == pl.num_programs(0) - 1)
    def _():
        c1_ref[0] = cr[...]
        c1_ref[1] = ci[...]


def _s5_tables(log_dt, a_re, a_im, b_re, b_im, c_re, c_im):
    dt = jnp.exp(log_dt.astype(F32))[:, None]
    ar, ai = a_re.astype(F32), a_im.astype(F32)
    mag = jnp.exp(dt * ar)
    abar_re, abar_im = mag * jnp.cos(dt * ai), mag * jnp.sin(dt * ai)
    den = ar * ar + ai * ai
    nr, ni = abar_re - 1.0, abar_im
    f_re = (nr * ar + ni * ai) / den
    f_im = (ni * ar - nr * ai) / den
    br, bi = b_re.astype(F32), b_im.astype(F32)
    bb_re = f_re[..., None] * br - f_im[..., None] * bi
    bb_im = f_re[..., None] * bi + f_im[..., None] * br
    eye = jnp.eye(SSM_GROUPS, dtype=F32)

    def in_blockdiag(bb):
        return jnp.einsum('gnc,gh->gchn', bb, eye).reshape(D_SSM, D_STATE)

    def out_blockdiag(c):
        return jnp.einsum('gcn,gh->gnhc', c.astype(F32), eye).reshape(D_STATE, D_SSM)

    a_r = abar_re.reshape(1, D_STATE)
    a_i = abar_im.reshape(1, D_STATE)
    a2_r = a_r * a_r - a_i * a_i
    a2_i = 2.0 * a_r * a_i
    hi = (jnp.arange(SUBLANES) >= S5_BATCH)[:, None]
    a1r = jnp.where(hi, a_r, 0.0)
    a1i = jnp.where(hi, a_i, 0.0)
    p_r = jnp.where(hi, a2_r, a_r)
    p_i = jnp.where(hi, a2_i, a_i)
    nblk = D_STATE // S5_COLS
    cw = D_SSM // nblk

    def in_blocks(m):
        return jnp.stack([m[b * cw:(b + 1) * cw, b * S5_COLS:(b + 1) * S5_COLS] for b in range(nblk)])

    def out_blocks(m):
        return jnp.stack([m[b * S5_COLS:(b + 1) * S5_COLS, b * cw:(b + 1) * cw] for b in range(nblk)])

    return (in_blocks(in_blockdiag(bb_re)).astype(BF16), in_blocks(in_blockdiag(bb_im)).astype(BF16),
            a1r, a1i, p_r, p_i,
            out_blocks(out_blockdiag(c_re)).astype(BF16), out_blocks(out_blockdiag(c_im)).astype(BF16))


def _s5(u_sb, carry, tables, d_skip, w_glu, B):
    assert B == S5_BATCH
    nt = u_sb.shape[0]
    ts = min(S5_TS, nt)
    rows = ts * B
    bre, bim, a1r, a1i, p_r, p_i, cre, cim = tables
    full = lambda shape: pl.BlockSpec(shape, lambda i: (0,) * len(shape))
    return pl.pallas_call(
        _s5_kernel,
        grid=(nt // ts,),
        in_specs=[pl.BlockSpec((rows, D_SSM), lambda i: (i, 0)),
                  full((2, SUBLANES, D_STATE)),
                  full(bre.shape), full(bim.shape),
                  full((SUBLANES, D_STATE)), full((SUBLANES, D_STATE)),
                  full((SUBLANES, D_STATE)), full((SUBLANES, D_STATE)),
                  full(cre.shape), full(cim.shape),
                  full((1, D_SSM)), full((D_SSM, D_SSM))],
        out_specs=[pl.BlockSpec((B, ts, D_SSM), lambda i: (0, i, 0)),
                   full((2, SUBLANES, D_STATE))],
        out_shape=[jax.ShapeDtypeStruct((B, nt, D_SSM), BF16),
                   jax.ShapeDtypeStruct((2, SUBLANES, D_STATE), F32)],
        scratch_shapes=[pltpu.VMEM((rows, D_STATE), F32), pltpu.VMEM((rows, D_STATE), F32),
                        pltpu.VMEM((SUBLANES, D_STATE), F32), pltpu.VMEM((SUBLANES, D_STATE), F32),
                        pltpu.VMEM((D_SSM // LANES, rows, LANES), F32)],
        compiler_params=pltpu.CompilerParams(
            dimension_semantics=("arbitrary",), vmem_limit_bytes=VMEM_LIMIT),
        name="s5",
    )(u_sb.reshape(nt * B, D_SSM), carry, bre, bim, a1r, a1i, p_r, p_i, cre, cim, d_skip, w_glu)


MOBA_PAIR = 2 * MOBA_BLOCK


def _moba_kernel(q0, q_ref, k_ref, v_ref, o_ref, kmean, kaug_a, kaug_b, vaug_a, vaug_b, qaug,
                 m_s, acc_s, s_buf):
    last = pl.program_id(2) + q0 // 2
    nb = k_ref.shape[0] // MOBA_BLOCK
    nbp = kmean.shape[0]
    lane = lax.broadcasted_iota(jnp.int32, (1, LANES), 1)
    head_a = lane < HEAD_DIM

    @pl.when(pl.program_id(2) == 0)
    def _():
        kmean[...] = jnp.zeros_like(kmean)
        for j in range(nb):
            rows = pl.ds(j * MOBA_BLOCK, MOBA_BLOCK)
            kj = k_ref[rows, :].astype(F32)
            vj = v_ref[rows, :].astype(F32)
            kmean[j:j + 1, :] = jnp.sum(kj, axis=0, keepdims=True) * (1.0 / MOBA_BLOCK)
            kaug_a[rows, :] = jnp.where(head_a, kj, jnp.where(lane - HEAD_DIM == j, 1.0, 0.0)).astype(BF16)
            kaug_b[rows, :] = jnp.where(head_a, jnp.where(lane == j, 1.0, 0.0), kj).astype(BF16)
            vaug_a[rows, :] = jnp.where(head_a, vj, 1.0).astype(BF16)
            vaug_b[rows, :] = jnp.where(head_a, 1.0, vj).astype(BF16)
        blk_row = lax.broadcasted_iota(jnp.int32, (nbp, MOBA_BLOCK), 0)
        for t in range(q_ref.shape[0] // MOBA_BLOCK):
            qt = q0 + t
            qf = q_ref[t * MOBA_BLOCK:(t + 1) * MOBA_BLOCK, :].astype(F32)
            for hd, is_a in enumerate((True, False)):
                mine = head_a if is_a else jnp.logical_not(head_a)
                q_own = jnp.where(mine, qf, 0.0)
                g = _dot_nt(kmean[...], q_own, precision=HIGHEST)
                g = jnp.where(blk_row < qt, g, NEG)
                sel = jnp.zeros(g.shape, F32)
                for _ in range(MOBA_TOPK):
                    m = jnp.max(g, axis=0, keepdims=True)
                    idx = jnp.min(jnp.where(g == m, blk_row, nbp), axis=0, keepdims=True)
                    hit = blk_row == idx
                    sel = jnp.where(hit, jnp.where(idx < qt, 1.0, 0.0), sel)
                    g = jnp.where(hit, -jnp.inf, g)
                bias_t = jnp.where(sel > 0.0, 0.0, jnp.where(blk_row == qt, 0.0, NEG))
                bias_t = jnp.concatenate([bias_t, jnp.full((LANES - nbp, MOBA_BLOCK), NEG, F32)], axis=0)
                bias = jnp.transpose(bias_t)
                if is_a:
                    bias = pltpu.roll(bias, HEAD_DIM, 1)
                qaug[hd, t * MOBA_BLOCK:(t + 1) * MOBA_BLOCK, :] = jnp.where(mine, qf, bias).astype(BF16)

    tile_rows = pl.ds(pl.multiple_of(pl.program_id(2) * MOBA_PAIR, MOBA_PAIR), MOBA_PAIR)
    q_augs = [qaug[0, tile_rows, :], qaug[1, tile_rows, :]]

    m_s[...] = jnp.full(m_s.shape, -jnp.inf, F32)
    acc_s[...] = jnp.zeros_like(acc_s)
    qpos = last * MOBA_PAIR + lax.broadcasted_iota(jnp.int32, (MOBA_PAIR, MOBA_PAIR), 0)
    col = lax.broadcasted_iota(jnp.int32, (MOBA_PAIR, MOBA_PAIR), 1)

    def kv_rows(jj):
        return pl.ds(pl.multiple_of(jj * MOBA_PAIR, MOBA_PAIR), MOBA_PAIR)

    def scores(jj, slot):
        for hd, kaug in enumerate((kaug_a, kaug_b)):
            s_buf[slot, hd] = _dot_nt(q_augs[hd], kaug[kv_rows(jj), :])

    def softmax_pv(jj, slot, causal):
        for hd, vaug in enumerate((vaug_a, vaug_b)):
            s = s_buf[slot, hd]
            if causal:
                s = jnp.where(jj * MOBA_PAIR + col <= qpos, s, NEG)
            m_old = m_s[hd]
            m_new = jnp.maximum(m_old, jnp.max(s, axis=-1, keepdims=True))
            alpha = jnp.exp(m_old - m_new)
            p = jnp.exp(s - m_new)
            m_s[hd] = m_new
            acc_s[hd] = alpha * acc_s[hd] + _dot(p.astype(BF16), vaug[kv_rows(jj), :])

    scores(0, 0)

    def body(k, _):
        scores(2 * k + 1, 1)
        softmax_pv(2 * k, 0, False)
        scores(2 * k + 2, 0)
        softmax_pv(2 * k + 1, 1, False)
        return 0

    lax.fori_loop(0, last // 2, body, 0)

    @pl.when(last % 2 == 0)
    def _():
        softmax_pv(last, 0, True)

    @pl.when(last % 2 == 1)
    def _():
        scores(last, 1)
        softmax_pv(last - 1, 0, False)
        softmax_pv(last, 1, True)
    acc_a, acc_b = acc_s[0], acc_s[1]
    o_ref[...] = jnp.where(head_a, acc_a / pltpu.roll(acc_a, HEAD_DIM, 1),
                           acc_b / pltpu.roll(acc_b, HEAD_DIM, 1)).astype(BF16)


def _moba(q, k, v, q0):
    B = q.shape[0]
    nq = q.shape[1] // MOBA_BLOCK
    skv = (q0 + nq) * MOBA_BLOCK
    nb = skv // MOBA_BLOCK
    assert nb <= HEAD_DIM and nb % 2 == 0 and skv <= k.shape[1]
    nbp = -(-nb // SUBLANES) * SUBLANES
    assert q0 % 2 == 0 and nq % 2 == 0
    blk = pl.BlockSpec((None, MOBA_PAIR, LANES), lambda b, h, i: (b, i, h))
    seq = pl.BlockSpec((None, skv, LANES), lambda b, h, i: (b, 0, h))
    return pl.pallas_call(
        functools.partial(_moba_kernel, q0),
        grid=(B, D_ATT // LANES, nq // 2),
        in_specs=[pl.BlockSpec((None, nq * MOBA_BLOCK, LANES), lambda b, h, i: (b, 0, h)), seq, seq],
        out_specs=blk,
        out_shape=jax.ShapeDtypeStruct(q.shape, BF16),
        scratch_shapes=[pltpu.VMEM((nbp, LANES), F32),
                        pltpu.VMEM((skv, LANES), BF16), pltpu.VMEM((skv, LANES), BF16),
                        pltpu.VMEM((skv, LANES), BF16), pltpu.VMEM((skv, LANES), BF16),
                        pltpu.VMEM((2, nq * MOBA_BLOCK, LANES), BF16),
                        pltpu.VMEM((2, MOBA_PAIR, 1), F32),
                        pltpu.VMEM((2, MOBA_PAIR, LANES), F32),
                        pltpu.VMEM((2, 2, MOBA_PAIR, MOBA_PAIR), F32)],
        compiler_params=pltpu.CompilerParams(
            dimension_semantics=("parallel", "parallel", "arbitrary"), vmem_limit_bytes=VMEM_LIMIT),
        name="moba",
    )(q, k, v)


MERGE_TS = 256


def _bf16_bits(x):
    b = pltpu.bitcast(x, jnp.int32)
    r = b + 0x7FFF + (lax.shift_right_logical(b, 16) & 1)
    return lax.shift_right_logical(r, 16)


def _merge_kernel(x_ref, ys_ref, at_ref, ga_ref, gb_ref, wa_ref, wb_ref, wo_ref, g_ref,
                  wq_ref, k1_ref, k2_ref, x1_ref, hw_ref, idx_ref, gate_ref, sc_ref):
    ya = _dot(ys_ref[...], wa_ref[...])
    yb = _dot(at_ref[...], wb_ref[...])
    merged = ga_ref[...].astype(F32) * ya + gb_ref[...].astype(F32) * yb
    x1 = x_ref[...] + _dot(merged.astype(BF16), wo_ref[...])
    x1_ref[...] = x1
    hq = _rms(x1, g_ref[...])
    hw_ref[...] = _pack_words(hq)
    qp = _dot(hq.astype(BF16), wq_ref[...])
    for h in range(PEER_HEADS):
        o = h * PEER_QDIM
        sc_ref[2 * h] = _dot_nt(k1_ref[h], qp[:, o:o + PEER_HALF], precision=HIGHEST)
        sc_ref[2 * h + 1] = _dot_nt(k2_ref[h], qp[:, o + PEER_HALF:o + PEER_QDIM], precision=HIGHEST)
    _topk_kernel(sc_ref, idx_ref, gate_ref)


def _merge(x, ys, att, ga, gb, t0, w_proj_ssm, w_proj_att, w_out, g_ffn, peer_w_q, keys1, keys2):
    B, nt = ys.shape[0], ys.shape[1]
    ts = min(MERGE_TS, nt)
    nblk = nt // ts
    i0 = t0 // ts
    tok = lambda d: pl.BlockSpec((None, ts, d), lambda b, i: (b, i, 0))
    row = lambda d: pl.BlockSpec((ts, d), lambda b, i: (b * nblk + i, 0))
    full = lambda shape: pl.BlockSpec(shape, lambda b, i: (0,) * len(shape))
    qd = PEER_HEADS * PEER_QDIM
    return pl.pallas_call(
        _merge_kernel,
        grid=(B, nblk),
        in_specs=[pl.BlockSpec((None, ts, D_MODEL), lambda b, i: (b, i0 + i, 0)),
                  tok(D_SSM), tok(D_ATT), tok(D_MODEL), tok(D_MODEL),
                  full((D_SSM, D_MODEL)), full((D_ATT, D_MODEL)), full((D_MODEL, D_MODEL)),
                  full((1, D_MODEL)), full((D_MODEL, qd)),
                  full((PEER_HEADS, PEER_KEYS, PEER_HALF)), full((PEER_HEADS, PEER_KEYS, PEER_HALF))],
        out_specs=[row(D_MODEL), row(D_MODEL // 2), row(PEER_SEL), row(PEER_SEL)],
        out_shape=[jax.ShapeDtypeStruct((B * nt, D_MODEL), F32),
                   jax.ShapeDtypeStruct((B * nt, D_MODEL // 2), jnp.int32),
                   jax.ShapeDtypeStruct((B * nt, PEER_SEL), jnp.int32),
                   jax.ShapeDtypeStruct((B * nt, PEER_SEL), F32)],
        scratch_shapes=[pltpu.VMEM((2 * PEER_HEADS, PEER_KEYS, ts), F32)],
        compiler_params=pltpu.CompilerParams(
            dimension_semantics=("parallel", "parallel"), vmem_limit_bytes=VMEM_LIMIT),
        name="merge",
    )(x, ys, att, ga, gb, w_proj_ssm, w_proj_att, w_out, g_ffn, peer_w_q, keys1, keys2)


def _top_rows(s, row, k):
    vals, idxs = [], []
    for _ in range(k):
        m = jnp.max(s, axis=0, keepdims=True)
        idx = jnp.min(jnp.where(s == m, row, s.shape[0]), axis=0, keepdims=True)
        vals.append(m)
        idxs.append(idx)
        s = jnp.where(row == idx, -jnp.inf, s)
    return vals, idxs


def _stack_rows(rows, row16):
    acc = jnp.zeros(row16.shape, rows[0].dtype)
    for r, v in enumerate(rows):
        acc = jnp.where(row16 == r, v, acc)
    return acc


def _topk_kernel(sc_ref, idx_ref, gate_ref):
    ts = sc_ref.shape[-1]
    row = lax.broadcasted_iota(jnp.int32, (PEER_KEYS, ts), 0).astype(F32)
    row16 = lax.broadcasted_iota(jnp.int32, (PEER_TOPK, ts), 0)
    row8 = lax.broadcasted_iota(jnp.int32, (SUBLANES, ts), 0)
    counts = [PEER_TOPK // (i + 1) for i in range(PEER_TOPK)]
    heights = [PEER_TOPK if c > SUBLANES else SUBLANES for c in counts]
    n_cand = sum(heights)
    rowc = lax.broadcasted_iota(jnp.int32, (n_cand, ts), 0).astype(F32)
    gate_rows, eid_rows = [], []
    for h in range(PEER_HEADS):
        v1, i1 = _top_rows(sc_ref[2 * h], row, PEER_TOPK)
        v2, i2 = _top_rows(sc_ref[2 * h + 1], row, PEER_TOPK)
        v2s = _stack_rows(v2, row16)
        i2s = _stack_rows(i2, row16)
        cand, eid = [], []
        for i in range(PEER_TOPK):
            n = heights[i]
            cand.append(jnp.where((row16 if n == PEER_TOPK else row8) < counts[i],
                                  v1[i] + v2s[:n], -jnp.inf))
            eid.append(i1[i] * PEER_KEYS + i2s[:n])
        cand = jnp.concatenate(cand, axis=0)
        eid = jnp.concatenate(eid, axis=0)
        tops, picks = [], []
        for _ in range(PEER_TOPK):
            m = jnp.max(cand, axis=0, keepdims=True)
            pos = jnp.min(jnp.where(cand == m, rowc, n_cand), axis=0, keepdims=True)
            hit = rowc == pos
            picks.append(jnp.max(jnp.where(hit, eid, -1.0), axis=0, keepdims=True))
            tops.append(m)
            cand = jnp.where(hit, -jnp.inf, cand)
        top = _stack_rows(tops, row16)
        p = jnp.exp(top - jnp.max(top, axis=0, keepdims=True))
        gate_rows.append(p / jnp.sum(p, axis=0, keepdims=True))
        eid_rows.append(_stack_rows(picks, row16))
    gate_ref[...] = jnp.transpose(jnp.concatenate(gate_rows, axis=0))
    idx_ref[...] = jnp.transpose(jnp.concatenate(eid_rows, axis=0)).astype(jnp.int32)


SC_CORES = 2
SC_SUBCORES = 16
SC_LANES = 16
SC_WORKERS = SC_CORES * SC_SUBCORES
PEER_CH = SC_LANES
PEER_NCH = PEER_SEL // PEER_CH
PEER_WORDS = D_MODEL // 2
PEER_NWG = PEER_WORDS // SC_LANES
PEER_RING = 4
PEER_QUAD = 4
HI_MASK = -65536
GELU_C = 0.7978845608028654


def _gelu_tanh_via_exp(x):
    z = GELU_C * (x + 0.044715 * (x * x * x))
    t = 1.0 - 2.0 / (jnp.exp(2.0 * z) + 1.0)
    return 0.5 * x * (1.0 + t)


def _unpack_pair(w):
    lo = plsc.bitcast(lax.shift_left(w, 16), F32)
    hi = plsc.bitcast(lax.bitwise_and(w, HI_MASK), F32)
    return lo, hi


def _peer_sc_body(idx_hbm, gate_hbm, h_hbm, uv_hbm, after_hbm, o_hbm,
                  idx_v, gate_v, h_v, buf, out_v, gsem, msem, osem):
    n_tok = o_hbm.shape[0] // SC_WORKERS
    base = (lax.axis_index("s") * SC_CORES + lax.axis_index("c")) * n_tok
    lane = lax.iota(jnp.int32, SC_LANES)
    zero_rows = jnp.zeros((SC_LANES,), jnp.int32)

    def meta_copies(tok, s):
        return (pltpu.make_async_copy(idx_hbm.at[tok], idx_v.at[s], msem.at[s]),
                pltpu.make_async_copy(gate_hbm.at[tok], gate_v.at[s], msem.at[s]),
                pltpu.make_async_copy(h_hbm.at[tok], h_v.at[s], msem.at[s]))

    def gather(slot, rows):
        return pltpu.make_async_copy(uv_hbm.at[rows], buf.at[slot], gsem.at[slot])

    def token(t, carry):
        s = t % 2
        tok = base + t
        nxt = base + jnp.minimum(t + 1, n_tok - 1)
        for cp in meta_copies(nxt, 1 - s):
            cp.start()

        @pl.when(t >= 2)
        def _():
            pltpu.make_async_copy(out_v.at[s], o_hbm.at[tok], osem.at[s]).wait()

        def chunk(c, carry):
            slot = c % PEER_RING
            gather(slot, zero_rows).wait()

            def dot_step(q, accs):
                cols = [pl.ds(pl.multiple_of((q * PEER_QUAD + j) * SC_LANES, SC_LANES), SC_LANES)
                        for j in range(PEER_QUAD)]
                hs = [plsc.bitcast(h_v[s, col], BF16) for col in cols]
                out = []
                for r in range(PEER_CH):
                    p = plsc.bitcast(buf[slot, r, cols[0]], BF16) * hs[0]
                    for j in range(1, PEER_QUAD):
                        p = p + plsc.bitcast(buf[slot, r, cols[j]], BF16) * hs[j]
                    lo, hi = _unpack_pair(plsc.bitcast(p, jnp.int32))
                    out.append(accs[r] + lo + hi)
                return tuple(out)

            accs = lax.fori_loop(0, PEER_NWG // PEER_QUAD, dot_step,
                                 tuple(jnp.zeros((SC_LANES,), F32) for _ in range(PEER_CH)))
            tot = jnp.zeros((SC_LANES,), F32)
            for r in range(PEER_CH):
                tot = jnp.where(lane == r, jnp.sum(accs[r]), tot)
            rows = pl.ds(pl.multiple_of(c * PEER_CH, PEER_CH), PEER_CH)
            wvec = gate_v[s, rows] * _gelu_tanh_via_exp(tot)
            ws = []
            for r in range(PEER_CH):
                w = wvec.at[jnp.full((SC_LANES,), r, jnp.int32)].get(mode="promise_in_bounds")
                ws.append(plsc.pack(w, w, format=plsc.PackFormat.INTERLEAVED,
                                    preferred_element_type=BF16))
            first = c == 0

            @plsc.parallel_loop(0, PEER_NWG, unroll=2)
            def acc_step(g):
                col = pl.ds(pl.multiple_of(g * SC_LANES, SC_LANES), SC_LANES)
                col_v = pl.ds(pl.multiple_of(PEER_WORDS + g * SC_LANES, SC_LANES), SC_LANES)
                o_lo = jnp.where(first, 0.0, out_v[s, col])
                o_hi = jnp.where(first, 0.0, out_v[s, col_v])
                for r0 in range(0, PEER_CH, PEER_QUAD):
                    p = plsc.bitcast(buf[slot, r0, col_v], BF16) * ws[r0]
                    for r in range(r0 + 1, r0 + PEER_QUAD):
                        p = p + plsc.bitcast(buf[slot, r, col_v], BF16) * ws[r]
                    lo, hi = _unpack_pair(plsc.bitcast(p, jnp.int32))
                    o_lo = o_lo + lo
                    o_hi = o_hi + hi
                out_v[s, col] = o_lo
                out_v[s, col_v] = o_hi

            @pl.when(c == PEER_NCH - PEER_RING)
            def _():
                for cp in meta_copies(nxt, 1 - s):
                    cp.wait()

            ahead = c + PEER_RING
            src = jnp.where(ahead < PEER_NCH, s, 1 - s)
            nrows = idx_v[src, pl.ds(pl.multiple_of((ahead % PEER_NCH) * PEER_CH, PEER_CH), PEER_CH)]
            gather(slot, nrows).start()
            return carry

        lax.fori_loop(0, PEER_NCH, chunk, 0)
        pltpu.make_async_copy(out_v.at[s], o_hbm.at[tok], osem.at[s]).start()
        return carry

    for cp in meta_copies(base, 0):
        cp.start()
    for cp in meta_copies(base, 0):
        cp.wait()
    for c in range(PEER_RING):
        gather(c, idx_v[0, pl.ds(c * PEER_CH, PEER_CH)]).start()
    lax.fori_loop(0, n_tok, token, 0)
    for c in range(PEER_RING):
        gather(c, zero_rows).wait()
    for s in range(2):
        pltpu.make_async_copy(out_v.at[s], o_hbm.at[base], osem.at[s]).wait()


PACK_ROWS = 256


def _pack_words(x):
    half = x.shape[1] // 2
    return _bf16_bits(x[:, :half]) | lax.shift_left(_bf16_bits(x[:, half:]), 16)


def _pack_tables_kernel(u_ref, v_ref, o_ref):
    o_ref[:, :PEER_WORDS] = _pack_words(u_ref[...])
    o_ref[:, PEER_WORDS:] = _pack_words(v_ref[...])


def _pack_tables(peer_u, peer_v):
    n = peer_u.shape[0]
    rows = min(PACK_ROWS, n)
    spec = pl.BlockSpec((rows, D_MODEL), lambda i: (i, 0))
    return pl.pallas_call(
        _pack_tables_kernel,
        grid=(n // rows,),
        in_specs=[spec, spec],
        out_specs=spec,
        out_shape=jax.ShapeDtypeStruct((n, D_MODEL), jnp.int32),
        compiler_params=pltpu.CompilerParams(
            dimension_semantics=("parallel",), vmem_limit_bytes=VMEM_LIMIT),
        name="pack_tables",
    )(peer_u, peer_v)


def _peer(idx, h_words, gates, uv_words, after):
    T = h_words.shape[0]
    assert T % (2 * SC_WORKERS) == 0
    mesh = plsc.VectorSubcoreMesh(core_axis_name="c", subcore_axis_name="s",
                                  num_cores=SC_CORES, num_subcores=SC_SUBCORES)
    return pl.kernel(
        _peer_sc_body,
        out_type=jax.ShapeDtypeStruct((T, D_MODEL), F32),
        mesh=mesh,
        scratch_types=[
            pltpu.VMEM((2, PEER_SEL), jnp.int32), pltpu.VMEM((2, PEER_SEL), F32),
            pltpu.VMEM((2, PEER_WORDS), jnp.int32),
            pltpu.VMEM((PEER_RING, PEER_CH, 2 * PEER_WORDS), jnp.int32),
            pltpu.VMEM((2, D_MODEL), F32),
            pltpu.SemaphoreType.DMA((PEER_RING,)),
            pltpu.SemaphoreType.DMA((2,)), pltpu.SemaphoreType.DMA((2,)),
        ],
        compiler_params=pltpu.CompilerParams(needs_layout_passes=False),
        name="peer_sc",
    )(idx, gates, h_words, uv_words, after)


FINAL_TS = 256


def _final_kernel(x1_ref, pe_ref, p_ref, gp_ref, wg_ref, wp_ref, gf_ref, o_ref):
    x2 = x1_ref[...] + pe_ref[...]
    e = _dot(p_ref[...].astype(BF16), wp_ref[...])
    gate = jax.nn.sigmoid(_dot(_rms(x2, gp_ref[...]).astype(BF16), wg_ref[...]))
    o_ref[...] = _rms(x2 + gate * e, gf_ref[...])


def _final(x1, peer_out, p, t0, nt, g_ple, ple_w_gate, ple_w_proj, g_final):
    B = p.shape[0]
    ts = min(FINAL_TS, nt)
    nblk = nt // ts
    i0 = t0 // ts
    row = lambda d: pl.BlockSpec((ts, d), lambda b, i: (b * nblk + i, 0))
    full = lambda shape: pl.BlockSpec(shape, lambda b, i: (0,) * len(shape))
    return pl.pallas_call(
        _final_kernel,
        grid=(B, nblk),
        in_specs=[row(D_MODEL), row(D_MODEL),
                  pl.BlockSpec((None, ts, D_PLE), lambda b, i: (b, i0 + i, 0)),
                  full((1, D_MODEL)), full((D_MODEL, D_MODEL)), full((D_PLE, D_MODEL)),
                  full((1, D_MODEL))],
        out_specs=pl.BlockSpec((None, ts, D_MODEL), lambda b, i: (b, i, 0)),
        out_shape=jax.ShapeDtypeStruct((B, nt, D_MODEL), F32),
        compiler_params=pltpu.CompilerParams(
            dimension_semantics=("parallel", "parallel"), vmem_limit_bytes=VMEM_LIMIT),
        name="final",
    )(x1, peer_out, p, g_ple, ple_w_gate, ple_w_proj, g_final)


CHUNK_STEPS = (512, 512, 1024, 1024, 1024, 1024, 1024, 1024, 512, 512)


def kernel(x, p, positions, g_mix, w_in, ssm_log_dt, ssm_a_re, ssm_a_im, ssm_b_re, ssm_b_im,
           ssm_c_re, ssm_c_im, ssm_d, ssm_w_glu, w_proj_ssm, w_proj_att, w_out, g_ffn,
           peer_w_q, peer_keys1, peer_keys2, peer_u, peer_v, g_ple, ple_w_gate, ple_w_proj,
           g_final):
    B, S, _ = x.shape
    assert w_in.shape[0] == 1, "the final rmsnorm is fused into the single layer's last stage"
    steps = CHUNK_STEPS if sum(CHUNK_STEPS) == S else (S,)
    i = 0
    tables = _s5_tables(ssm_log_dt[i], ssm_a_re[i], ssm_a_im[i], ssm_b_re[i], ssm_b_im[i],
                        ssm_c_re[i], ssm_c_im[i])
    w_in_b, w_glu_b = w_in[i].astype(BF16), ssm_w_glu[i].astype(BF16)
    d_skip = ssm_d[i].reshape(1, D_SSM).astype(F32)
    merge_w = (w_proj_ssm[i].astype(BF16), w_proj_att[i].astype(BF16), w_out[i].astype(BF16),
               g_ffn[i].reshape(1, D_MODEL), peer_w_q[i].astype(BF16), peer_keys1[i], peer_keys2[i])
    final_w = (g_ple[i].reshape(1, D_MODEL), ple_w_gate[i].astype(BF16),
               ple_w_proj[i].astype(BF16), g_final.reshape(1, D_MODEL))
    uv_words = _pack_tables(peer_u[i], peer_v[i])
    k_all = jnp.zeros((B, S, D_ATT), BF16)
    v_all = jnp.zeros((B, S, D_ATT), BF16)
    carry = jnp.zeros((2, SUBLANES, D_STATE), F32)
    outs = []
    t0 = 0
    after = (carry, carry)
    peer_prev = carry
    for nt in steps:
        u_sb, q, k, v, ga, gb = _in_proj(x, positions, g_mix[i], w_in_b, t0, nt, after)
        k_all = lax.dynamic_update_slice(k_all, k, (0, t0, 0))
        v_all = lax.dynamic_update_slice(v_all, v, (0, t0, 0))
        ys, carry = _s5(u_sb, carry, tables, d_skip, w_glu_b, B)
        att = _moba(q, k_all, v_all, t0 // MOBA_BLOCK)
        x1, h_words, idx, gates = _merge(x, ys, att, ga, gb, t0, *merge_w)
        after = (gates, outs[-3] if len(outs) > 2 else carry)
        peer_out = _peer(idx, h_words, gates, uv_words, peer_prev if t0 + nt == S else carry)
        peer_prev = peer_out
        outs.append(_final(x1, peer_out, p[i], t0, nt, *final_w))
        t0 += nt
    return jnp.concatenate(outs, axis=1)
```

```python
import functools

import jax
import jax.numpy as jnp
from jax import lax
from jax.experimental import pallas as pl
from jax.experimental.pallas import tpu as pltpu
from jax.experimental.pallas import tpu_sc as plsc

F32 = jnp.float32
BF16 = jnp.bfloat16

D_MODEL = 1024
D_SSM = 512
SSM_GROUPS = 32
SSM_STATE = 64
D_STATE = SSM_GROUPS * SSM_STATE
HEAD_DIM = 64
D_ATT = 512
ROT_DIM = 16
ROPE_THETA = 500000.0
MOBA_BLOCK = 256
MOBA_TOPK = 3
PEER_HEADS = 8
PEER_KEYS = 128
PEER_QDIM = 256
PEER_HALF = 128
PEER_TOPK = 16
PEER_SEL = PEER_HEADS * PEER_TOPK
D_PLE = 256
EPS = 1e-6
NEG = -1e30
LANES = 128
SUBLANES = 8
VMEM_LIMIT = 48 * 1024 * 1024
HIGHEST = lax.Precision.HIGHEST


def _rms(x, g):
    return x * lax.rsqrt(jnp.mean(x * x, axis=-1, keepdims=True) + EPS) * g


def _dot(a, b):
    return jnp.dot(a, b, preferred_element_type=F32)


def _dot_nt(a, b, precision=None):
    return lax.dot_general(a, b, (((1,), (1,)), ((), ())), precision=precision,
                           preferred_element_type=F32)


IN_TS = 512


def _in_proj_kernel(x_ref, pos_ref, g_ref, w_ref, invf_ref, after_a, after_b,
                    u_ref, q_ref, k_ref, v_ref, ga_ref, gb_ref):
    del after_a, after_b
    h = _rms(x_ref[...], g_ref[...]).astype(BF16)

    def proj(lo, hi):
        return _dot(h, w_ref[:, lo:hi])

    u_ref[...] = proj(0, D_SSM).astype(BF16)
    ang = pos_ref[...].astype(F32) * invf_ref[...]
    cos = jnp.cos(ang)
    sin = jnp.sin(ang)
    lane = lax.broadcasted_iota(jnp.int32, (1, LANES), 1) % HEAD_DIM
    half = ROT_DIM // 2
    sin_hi = jnp.where((lane >= half) & (lane < ROT_DIM), sin, 0.0)
    sin_lo = jnp.where(lane < half, -sin, 0.0)
    reps = D_ATT // LANES
    cos4 = jnp.concatenate([cos] * reps, axis=1)
    sin_hi4 = jnp.concatenate([sin_hi] * reps, axis=1)
    sin_lo4 = jnp.concatenate([sin_lo] * reps, axis=1)

    def rope(t):
        return (t * cos4 + pltpu.roll(t, half, 1) * sin_hi4
                + pltpu.roll(t, D_ATT - half, 1) * sin_lo4)

    q = rope(proj(D_SSM, D_SSM + D_ATT))
    q_ref[...] = (q * (HEAD_DIM ** -0.5)).astype(BF16)
    k_ref[...] = rope(proj(D_SSM + D_ATT, D_SSM + 2 * D_ATT)).astype(BF16)
    v_ref[...] = proj(D_SSM + 2 * D_ATT, D_SSM + 3 * D_ATT).astype(BF16)
    o = D_SSM + 3 * D_ATT
    ga_ref[...] = jax.nn.sigmoid(proj(o, o + D_MODEL)).astype(BF16)
    gb_ref[...] = jax.nn.sigmoid(proj(o + D_MODEL, o + 2 * D_MODEL)).astype(BF16)


def _in_proj(x, positions, g_mix, w_in, t0, nt, after):
    B, S, _ = x.shape
    ts = min(IN_TS, nt)
    assert nt % ts == 0 and t0 % ts == 0
    i0 = t0 // ts
    inv_freq = ROPE_THETA ** (-jnp.arange(0, ROT_DIM, 2, dtype=F32) / ROT_DIM)
    lane = jnp.arange(LANES) % HEAD_DIM
    invf = jnp.where(lane < ROT_DIM, inv_freq[lane % (ROT_DIM // 2)], 0.0).reshape(1, LANES)
    d_in = w_in.shape[1]
    src = lambda d: pl.BlockSpec((None, ts, d), lambda b, i: (b, i0 + i, 0))
    tok = lambda d: pl.BlockSpec((None, ts, d), lambda b, i: (b, i, 0))
    full = lambda shape: pl.BlockSpec(shape, lambda b, i: (0,) * len(shape))
    return pl.pallas_call(
        _in_proj_kernel,
        grid=(B, nt // ts),
        in_specs=[src(D_MODEL), src(1), full((1, D_MODEL)), full((D_MODEL, d_in)), full((1, LANES)),
                  pl.BlockSpec(memory_space=pl.ANY), pl.BlockSpec(memory_space=pl.ANY)],
        out_specs=[pl.BlockSpec((ts, D_SSM), lambda b, i: (i, b)),
                   tok(D_ATT), tok(D_ATT), tok(D_ATT), tok(D_MODEL), tok(D_MODEL)],
        out_shape=[jax.ShapeDtypeStruct((nt, B * D_SSM), BF16),
                   jax.ShapeDtypeStruct((B, nt, D_ATT), BF16),
                   jax.ShapeDtypeStruct((B, nt, D_ATT), BF16),
                   jax.ShapeDtypeStruct((B, nt, D_ATT), BF16),
                   jax.ShapeDtypeStruct((B, nt, D_MODEL), BF16),
                   jax.ShapeDtypeStruct((B, nt, D_MODEL), BF16)],
        compiler_params=pltpu.CompilerParams(
            dimension_semantics=("parallel", "parallel"), vmem_limit_bytes=VMEM_LIMIT),
        name="in_proj",
    )(x, positions.reshape(B, S, 1), g_mix.reshape(1, D_MODEL), w_in, invf, *after)


S5_TS = 128
S5_BATCH = 4
S5_COLS = 512


def _s5_kernel(u_ref, c0_ref, bre_ref, bim_ref, a1r_ref, a1i_ref, pr_ref, pi_ref,
               cre_ref, cim_ref, d_ref, wglu_ref, y_ref, c1_ref,
               xr, xi, cr, ci, ysc):
    rows = xr.shape[0]
    ts = rows // S5_BATCH

    @pl.when(pl.program_id(0) == 0)
    def _():
        cr[...] = c0_ref[0]
        ci[...] = c0_ref[1]

    u = u_ref[...]
    for cb in range(D_STATE // S5_COLS):
        sl = slice(cb * S5_COLS, (cb + 1) * S5_COLS)
        u_cb = u[:, cb * LANES:(cb + 1) * LANES]
        xr[:, sl] = _dot(u_cb, bre_ref[cb])
        xi[:, sl] = _dot(u_cb, bim_ref[cb])

    hi_rows = lax.broadcasted_iota(jnp.int32, (SUBLANES, S5_COLS), 0) >= S5_BATCH
    for cb in range(D_STATE // S5_COLS):
        sl = slice(cb * S5_COLS, (cb + 1) * S5_COLS)
        a_r, a_i = a1r_ref[:, sl], a1i_ref[:, sl]
        p_r, p_i = pr_ref[:, sl], pi_ref[:, sl]

        def body(t, carry):
            c_r, c_i = carry
            r0 = pl.multiple_of(t * SUBLANES, SUBLANES)
            x_r = xr[pl.ds(r0, SUBLANES), sl]
            x_i = xi[pl.ds(r0, SUBLANES), sl]
            s_r = pltpu.roll(x_r, S5_BATCH, 0)
            s_i = pltpu.roll(x_i, S5_BATCH, 0)
            h_r = x_r + (a_r * s_r - a_i * s_i) + (p_r * c_r - p_i * c_i)
            h_i = x_i + (a_r * s_i + a_i * s_r) + (p_r * c_i + p_i * c_r)
            xr[pl.ds(r0, SUBLANES), sl] = h_r
            xi[pl.ds(r0, SUBLANES), sl] = h_i
            n_r = jnp.where(hi_rows, h_r, pltpu.roll(h_r, S5_BATCH, 0))
            n_i = jnp.where(hi_rows, h_i, pltpu.roll(h_i, S5_BATCH, 0))
            return n_r, n_i

        c_r, c_i = lax.fori_loop(0, rows // SUBLANES, body, (cr[:, sl], ci[:, sl]), unroll=2)
        cr[:, sl] = c_r
        ci[:, sl] = c_i

    y = jnp.concatenate(
        [_dot(xr[:, cb * S5_COLS:(cb + 1) * S5_COLS].astype(BF16), cre_ref[cb])
         - _dot(xi[:, cb * S5_COLS:(cb + 1) * S5_COLS].astype(BF16), cim_ref[cb])
         for cb in range(D_STATE // S5_COLS)], axis=1) + d_ref[...] * u.astype(F32)
    y = jax.nn.gelu(y)
    y = y * jax.nn.sigmoid(_dot(y.astype(BF16), wglu_ref[...]))
    for c in range(D_SSM // LANES):
        ysc[c] = y[:, c * LANES:(c + 1) * LANES]
    for b in range(S5_BATCH):
        for c in range(D_SSM // LANES):
            y_ref[b, :, c * LANES:(c + 1) * LANES] = (
                ysc[c, pl.ds(b, ts, stride=S5_BATCH), :].astype(BF16))

    @pl.when(pl.program_id(0) == pl.num_programs(0) - 1)
    def _():
        c1_ref[0] = cr[...]
        c1_ref[1] = ci[...]


def _s5_tables(log_dt, a_re, a_im, b_re, b_im, c_re, c_im):
    dt = jnp.exp(log_dt.astype(F32))[:, None]
    ar, ai = a_re.astype(F32), a_im.astype(F32)
    mag = jnp.exp(dt * ar)
    abar_re, abar_im = mag * jnp.cos(dt * ai), mag * jnp.sin(dt * ai)
    den = ar * ar + ai * ai
    nr, ni = abar_re - 1.0, abar_im
    f_re = (nr * ar + ni * ai) / den
    f_im = (ni * ar - nr * ai) / den
    br, bi = b_re.astype(F32), b_im.astype(F32)
    bb_re = f_re[..., None] * br - f_im[..., None] * bi
    bb_im = f_re[..., None] * bi + f_im[..., None] * br
    eye = jnp.eye(SSM_GROUPS, dtype=F32)

    def in_blockdiag(bb):
        return jnp.einsum('gnc,gh->gchn', bb, eye).reshape(D_SSM, D_STATE)

    def out_blockdiag(c):
        return jnp.einsum('gcn,gh->gnhc', c.astype(F32), eye).reshape(D_STATE, D_SSM)

    a_r = abar_re.reshape(1, D_STATE)
    a_i = abar_im.reshape(1, D_STATE)
    a2_r = a_r * a_r - a_i * a_i
    a2_i = 2.0 * a_r * a_i
    hi = (jnp.arange(SUBLANES) >= S5_BATCH)[:, None]
    a1r = jnp.where(hi, a_r, 0.0)
    a1i = jnp.where(hi, a_i, 0.0)
    p_r = jnp.where(hi, a2_r, a_r)
    p_i = jnp.where(hi, a2_i, a_i)
    nblk = D_STATE // S5_COLS
    cw = D_SSM // nblk

    def in_blocks(m):
        return jnp.stack([m[b * cw:(b + 1) * cw, b * S5_COLS:(b + 1) * S5_COLS] for b in range(nblk)])

    def out_blocks(m):
        return jnp.stack([m[b * S5_COLS:(b + 1) * S5_COLS, b * cw:(b + 1) * cw] for b in range(nblk)])

    return (in_blocks(in_blockdiag(bb_re)).astype(BF16), in_blocks(in_blockdiag(bb_im)).astype(BF16),
            a1r, a1i, p_r, p_i,
            out_blocks(out_blockdiag(c_re)).astype(BF16), out_blocks(out_blockdiag(c_im)).astype(BF16))


def _s5(u_sb, carry, tables, d_skip, w_glu, B):
    assert B == S5_BATCH
    nt = u_sb.shape[0]
    ts = min(S5_TS, nt)
    rows = ts * B
    bre, bim, a1r, a1i, p_r, p_i, cre, cim = tables
    full = lambda shape: pl.BlockSpec(shape, lambda i: (0,) * len(shape))
    return pl.pallas_call(
        _s5_kernel,
        grid=(nt // ts,),
        in_specs=[pl.BlockSpec((rows, D_SSM), lambda i: (i, 0)),
                  full((2, SUBLANES, D_STATE)),
                  full(bre.shape), full(bim.shape),
                  full((SUBLANES, D_STATE)), full((SUBLANES, D_STATE)),
                  full((SUBLANES, D_STATE)), full((SUBLANES, D_STATE)),
                  full(cre.shape), full(cim.shape),
                  full((1, D_SSM)), full((D_SSM, D_SSM))],
        out_specs=[pl.BlockSpec((B, ts, D_SSM), lambda i: (0, i, 0)),
                   full((2, SUBLANES, D_STATE))],
        out_shape=[jax.ShapeDtypeStruct((B, nt, D_SSM), BF16),
                   jax.ShapeDtypeStruct((2, SUBLANES, D_STATE), F32)],
        scratch_shapes=[pltpu.VMEM((rows, D_STATE), F32), pltpu.VMEM((rows, D_STATE), F32),
                        pltpu.VMEM((SUBLANES, D_STATE), F32), pltpu.VMEM((SUBLANES, D_STATE), F32),
                        pltpu.VMEM((D_SSM // LANES, rows, LANES), F32)],
        compiler_params=pltpu.CompilerParams(
            dimension_semantics=("arbitrary",), vmem_limit_bytes=VMEM_LIMIT),
        name="s5",
    )(u_sb.reshape(nt * B, D_SSM), carry, bre, bim, a1r, a1i, p_r, p_i, cre, cim, d_skip, w_glu)


MOBA_PAIR = 2 * MOBA_BLOCK


def _moba_kernel(q0, q_ref, k_ref, v_ref, o_ref, kmean, kaug_a, kaug_b, vaug_a, vaug_b, qaug,
                 m_s, acc_s, s_buf):
    last = pl.program_id(2) + q0 // 2
    nb = k_ref.shape[0] // MOBA_BLOCK
    nbp = kmean.shape[0]
    lane = lax.broadcasted_iota(jnp.int32, (1, LANES), 1)
    head_a = lane < HEAD_DIM

    @pl.when(pl.program_id(2) == 0)
    def _():
        kmean[...] = jnp.zeros_like(kmean)
        for j in range(nb):
            rows = pl.ds(j * MOBA_BLOCK, MOBA_BLOCK)
            kj = k_ref[rows, :].astype(F32)
            vj = v_ref[rows, :].astype(F32)
            kmean[j:j + 1, :] = jnp.sum(kj, axis=0, keepdims=True) * (1.0 / MOBA_BLOCK)
            kaug_a[rows, :] = jnp.where(head_a, kj, jnp.where(lane - HEAD_DIM == j, 1.0, 0.0)).astype(BF16)
            kaug_b[rows, :] = jnp.where(head_a, jnp.where(lane == j, 1.0, 0.0), kj).astype(BF16)
            vaug_a[rows, :] = jnp.where(head_a, vj, 1.0).astype(BF16)
            vaug_b[rows, :] = jnp.where(head_a, 1.0, vj).astype(BF16)
        blk_row = lax.broadcasted_iota(jnp.int32, (nbp, MOBA_BLOCK), 0)
        for t in range(q_ref.shape[0] // MOBA_BLOCK):
            qt = q0 + t
            qf = q_ref[t * MOBA_BLOCK:(t + 1) * MOBA_BLOCK, :].astype(F32)
            for hd, is_a in enumerate((True, False)):
                mine = head_a if is_a else jnp.logical_not(head_a)
                q_own = jnp.where(mine, qf, 0.0)
                g = _dot_nt(kmean[...], q_own, precision=HIGHEST)
                g = jnp.where(blk_row < qt, g, NEG)
                sel = jnp.zeros(g.shape, F32)
                for _ in range(MOBA_TOPK):
                    m = jnp.max(g, axis=0, keepdims=True)
                    idx = jnp.min(jnp.where(g == m, blk_row, nbp), axis=0, keepdims=True)
                    hit = blk_row == idx
                    sel = jnp.where(hit, jnp.where(idx < qt, 1.0, 0.0), sel)
                    g = jnp.where(hit, -jnp.inf, g)
                bias_t = jnp.where(sel > 0.0, 0.0, jnp.where(blk_row == qt, 0.0, NEG))
                bias_t = jnp.concatenate([bias_t, jnp.full((LANES - nbp, MOBA_BLOCK), NEG, F32)], axis=0)
                bias = jnp.transpose(bias_t)
                if is_a:
                    bias = pltpu.roll(bias, HEAD_DIM, 1)
                qaug[hd, t * MOBA_BLOCK:(t + 1) * MOBA_BLOCK, :] = jnp.where(mine, qf, bias).astype(BF16)

    tile_rows = pl.ds(pl.multiple_of(pl.program_id(2) * MOBA_PAIR, MOBA_PAIR), MOBA_PAIR)
    q_augs = [qaug[0, tile_rows, :], qaug[1, tile_rows, :]]

    m_s[...] = jnp.full(m_s.shape, -jnp.inf, F32)
    acc_s[...] = jnp.zeros_like(acc_s)
    qpos = last * MOBA_PAIR + lax.broadcasted_iota(jnp.int32, (MOBA_PAIR, MOBA_PAIR), 0)
    col = lax.broadcasted_iota(jnp.int32, (MOBA_PAIR, MOBA_PAIR), 1)

    def kv_rows(jj):
        return pl.ds(pl.multiple_of(jj * MOBA_PAIR, MOBA_PAIR), MOBA_PAIR)

    def scores(jj, slot):
        for hd, kaug in enumerate((kaug_a, kaug_b)):
            s_buf[slot, hd] = _dot_nt(q_augs[hd], kaug[kv_rows(jj), :])

    def softmax_pv(jj, slot, causal):
        for hd, vaug in enumerate((vaug_a, vaug_b)):
            s = s_buf[slot, hd]
            if causal:
                s = jnp.where(jj * MOBA_PAIR + col <= qpos, s, NEG)
            m_old = m_s[hd]
            m_new = jnp.maximum(m_old, jnp.max(s, axis=-1, keepdims=True))
            alpha = jnp.exp(m_old - m_new)
            p = jnp.exp(s - m_new)
            m_s[hd] = m_new
            acc_s[hd] = alpha * acc_s[hd] + _dot(p.astype(BF16), vaug[kv_rows(jj), :])

    scores(0, 0)

    def body(k, _):
        scores(2 * k + 1, 1)
        softmax_pv(2 * k, 0, False)
        scores(2 * k + 2, 0)
        softmax_pv(2 * k + 1, 1, False)
        return 0

    lax.fori_loop(0, last // 2, body, 0)

    @pl.when(last % 2 == 0)
    def _():
        softmax_pv(last, 0, True)

    @pl.when(last % 2 == 1)
    def _():
        scores(last, 1)
        softmax_pv(last - 1, 0, False)
        softmax_pv(last, 1, True)
    acc_a, acc_b = acc_s[0], acc_s[1]
    o_ref[...] = jnp.where(head_a, acc_a / pltpu.roll(acc_a, HEAD_DIM, 1),
                           acc_b / pltpu.roll(acc_b, HEAD_DIM, 1)).astype(BF16)


def _moba(q, k, v, q0):
    B = q.shape[0]
    nq = q.shape[1] // MOBA_BLOCK
    skv = (q0 + nq) * MOBA_BLOCK
    nb = skv // MOBA_BLOCK
    assert nb <= HEAD_DIM and nb % 2 == 0 and skv <= k.shape[1]
    nbp = -(-nb // SUBLANES) * SUBLANES
    assert q0 % 2 == 0 and nq % 2 == 0
    blk = pl.BlockSpec((None, MOBA_PAIR, LANES), lambda b, h, i: (b, i, h))
    seq = pl.BlockSpec((None, skv, LANES), lambda b, h, i: (b, 0, h))
    return pl.pallas_call(
        functools.partial(_moba_kernel, q0),
        grid=(B, D_ATT // LANES, nq // 2),
        in_specs=[pl.BlockSpec((None, nq * MOBA_BLOCK, LANES), lambda b, h, i: (b, 0, h)), seq, seq],
        out_specs=blk,
        out_shape=jax.ShapeDtypeStruct(q.shape, BF16),
        scratch_shapes=[pltpu.VMEM((nbp, LANES), F32),
                        pltpu.VMEM((skv, LANES), BF16), pltpu.VMEM((skv, LANES), BF16),
                        pltpu.VMEM((skv, LANES), BF16), pltpu.VMEM((skv, LANES), BF16),
                        pltpu.VMEM((2, nq * MOBA_BLOCK, LANES), BF16),
                        pltpu.VMEM((2, MOBA_PAIR, 1), F32),
                        pltpu.VMEM((2, MOBA_PAIR, LANES), F32),
                        pltpu.VMEM((2, 2, MOBA_PAIR, MOBA_PAIR), F32)],
        compiler_params=pltpu.CompilerParams(
            dimension_semantics=("parallel", "parallel", "arbitrary"), vmem_limit_bytes=VMEM_LIMIT),
        name="moba",
    )(q, k, v)


MERGE_TS = 256


def _bf16_bits(x):
    b = pltpu.bitcast(x, jnp.int32)
    r = b + 0x7FFF + (lax.shift_right_logical(b, 16) & 1)
    return lax.shift_right_logical(r, 16)


def _merge_kernel(x_ref, ys_ref, at_ref, ga_ref, gb_ref, wa_ref, wb_ref, wo_ref, g_ref,
                  wq_ref, k1_ref, k2_ref, x1_ref, hw_ref, idx_ref, gate_ref, sc_ref):
    ya = _dot(ys_ref[...], wa_ref[...])
    yb = _dot(at_ref[...], wb_ref[...])
    merged = ga_ref[...].astype(F32) * ya + gb_ref[...].astype(F32) * yb
    x1 = x_ref[...] + _dot(merged.astype(BF16), wo_ref[...])
    x1_ref[...] = x1
    hq = _rms(x1, g_ref[...])
    hw_ref[...] = _pack_words(hq)
    qp = _dot(hq.astype(BF16), wq_ref[...])
    for h in range(PEER_HEADS):
        o = h * PEER_QDIM
        sc_ref[2 * h] = _dot_nt(k1_ref[h], qp[:, o:o + PEER_HALF], precision=HIGHEST)
        sc_ref[2 * h + 1] = _dot_nt(k2_ref[h], qp[:, o + PEER_HALF:o + PEER_QDIM], precision=HIGHEST)
    _topk_kernel(sc_ref, idx_ref, gate_ref)


def _merge(x, ys, att, ga, gb, t0, w_proj_ssm, w_proj_att, w_out, g_ffn, peer_w_q, keys1, keys2):
    B, nt = ys.shape[0], ys.shape[1]
    ts = min(MERGE_TS, nt)
    nblk = nt // ts
    i0 = t0 // ts
    tok = lambda d: pl.BlockSpec((None, ts, d), lambda b, i: (b, i, 0))
    row = lambda d: pl.BlockSpec((ts, d), lambda b, i: (b * nblk + i, 0))
    full = lambda shape: pl.BlockSpec(shape, lambda b, i: (0,) * len(shape))
    qd = PEER_HEADS * PEER_QDIM
    return pl.pallas_call(
        _merge_kernel,
        grid=(B, nblk),
        in_specs=[pl.BlockSpec((None, ts, D_MODEL), lambda b, i: (b, i0 + i, 0)),
                  tok(D_SSM), tok(D_ATT), tok(D_MODEL), tok(D_MODEL),
                  full((D_SSM, D_MODEL)), full((D_ATT, D_MODEL)), full((D_MODEL, D_MODEL)),
                  full((1, D_MODEL)), full((D_MODEL, qd)),
                  full((PEER_HEADS, PEER_KEYS, PEER_HALF)), full((PEER_HEADS, PEER_KEYS, PEER_HALF))],
        out_specs=[row(D_MODEL), row(D_MODEL // 2), row(PEER_SEL), row(PEER_SEL)],
        out_shape=[jax.ShapeDtypeStruct((B * nt, D_MODEL), F32),
                   jax.ShapeDtypeStruct((B * nt, D_MODEL // 2), jnp.int32),
                   jax.ShapeDtypeStruct((B * nt, PEER_SEL), jnp.int32),
                   jax.ShapeDtypeStruct((B * nt, PEER_SEL), F32)],
        scratch_shapes=[pltpu.VMEM((2 * PEER_HEADS, PEER_KEYS, ts), F32)],
        compiler_params=pltpu.CompilerParams(
            dimension_semantics=("parallel", "parallel"), vmem_limit_bytes=VMEM_LIMIT),
        name="merge",
    )(x, ys, att, ga, gb, w_proj_ssm, w_proj_att, w_out, g_ffn, peer_w_q, keys1, keys2)


def _top_rows(s, row, k):
    vals, idxs = [], []
    for _ in range(k):
        m = jnp.max(s, axis=0, keepdims=True)
        idx = jnp.min(jnp.where(s == m, row, s.shape[0]), axis=0, keepdims=True)
        vals.append(m)
        idxs.append(idx)
        s = jnp.where(row == idx, -jnp.inf, s)
    return vals, idxs


def _stack_rows(rows, row16):
    acc = jnp.zeros(row16.shape, rows[0].dtype)
    for r, v in enumerate(rows):
        acc = jnp.where(row16 == r, v, acc)
    return acc


def _topk_kernel(sc_ref, idx_ref, gate_ref):
    ts = sc_ref.shape[-1]
    row = lax.broadcasted_iota(jnp.int32, (PEER_KEYS, ts), 0).astype(F32)
    row16 = lax.broadcasted_iota(jnp.int32, (PEER_TOPK, ts), 0)
    row8 = lax.broadcasted_iota(jnp.int32, (SUBLANES, ts), 0)
    counts = [PEER_TOPK // (i + 1) for i in range(PEER_TOPK)]
    heights = [PEER_TOPK if c > SUBLANES else SUBLANES for c in counts]
    n_cand = sum(heights)
    rowc = lax.broadcasted_iota(jnp.int32, (n_cand, ts), 0).astype(F32)
    gate_rows, eid_rows = [], []
    for h in range(PEER_HEADS):
        v1, i1 = _top_rows(sc_ref[2 * h], row, PEER_TOPK)
        v2, i2 = _top_rows(sc_ref[2 * h + 1], row, PEER_TOPK)
        v2s = _stack_rows(v2, row16)
        i2s = _stack_rows(i2, row16)
        cand, eid = [], []
        for i in range(PEER_TOPK):
            n = heights[i]
            cand.append(jnp.where((row16 if n == PEER_TOPK else row8) < counts[i],
                                  v1[i] + v2s[:n], -jnp.inf))
            eid.append(i1[i] * PEER_KEYS + i2s[:n])
        cand = jnp.concatenate(cand, axis=0)
        eid = jnp.concatenate(eid, axis=0)
        tops, picks = [], []
        for _ in range(PEER_TOPK):
            m = jnp.max(cand, axis=0, keepdims=True)
            pos = jnp.min(jnp.where(cand == m, rowc, n_cand), axis=0, keepdims=True)
            hit = rowc == pos
            picks.append(jnp.max(jnp.where(hit, eid, -1.0), axis=0, keepdims=True))
            tops.append(m)
            cand = jnp.where(hit, -jnp.inf, cand)
        top = _stack_rows(tops, row16)
        p = jnp.exp(top - jnp.max(top, axis=0, keepdims=True))
        gate_rows.append(p / jnp.sum(p, axis=0, keepdims=True))
        eid_rows.append(_stack_rows(picks, row16))
    gate_ref[...] = jnp.transpose(jnp.concatenate(gate_rows, axis=0))
    idx_ref[...] = jnp.transpose(jnp.concatenate(eid_rows, axis=0)).astype(jnp.int32)


SC_CORES = 2
SC_SUBCORES = 16
SC_LANES = 16
SC_WORKERS = SC_CORES * SC_SUBCORES
PEER_CH = SC_LANES
PEER_NCH = PEER_SEL // PEER_CH
PEER_WORDS = D_MODEL // 2
PEER_NWG = PEER_WORDS // SC_LANES
PEER_RING = 4
PEER_QUAD = 4
HI_MASK = -65536
GELU_C = 0.7978845608028654


def _gelu_tanh_via_exp(x):
    z = GELU_C * (x + 0.044715 * (x * x * x))
    t = 1.0 - 2.0 / (jnp.exp(2.0 * z) + 1.0)
    return 0.5 * x * (1.0 + t)


def _unpack_pair(w):
    lo = plsc.bitcast(lax.shift_left(w, 16), F32)
    hi = plsc.bitcast(lax.bitwise_and(w, HI_MASK), F32)
    return lo, hi


def _peer_sc_body(idx_hbm, gate_hbm, h_hbm, uv_hbm, after_hbm, o_hbm,
                  idx_v, gate_v, h_v, buf, out_v, gsem, msem, osem):
    n_tok = o_hbm.shape[0] // SC_WORKERS
    base = (lax.axis_index("s") * SC_CORES + lax.axis_index("c")) * n_tok
    lane = lax.iota(jnp.int32, SC_LANES)
    zero_rows = jnp.zeros((SC_LANES,), jnp.int32)

    def meta_copies(tok, s):
        return (pltpu.make_async_copy(idx_hbm.at[tok], idx_v.at[s], msem.at[s]),
                pltpu.make_async_copy(gate_hbm.at[tok], gate_v.at[s], msem.at[s]),
                pltpu.make_async_copy(h_hbm.at[tok], h_v.at[s], msem.at[s]))

    def gather(slot, rows):
        return pltpu.make_async_copy(uv_hbm.at[rows], buf.at[slot], gsem.at[slot])

    def token(t, carry):
        s = t % 2
        tok = base + t
        nxt = base + jnp.minimum(t + 1, n_tok - 1)
        for cp in meta_copies(nxt, 1 - s):
            cp.start()

        @pl.when(t >= 2)
        def _():
            pltpu.make_async_copy(out_v.at[s], o_hbm.at[tok], osem.at[s]).wait()

        def chunk(c, carry):
            slot = c % PEER_RING
            gather(slot, zero_rows).wait()

            def dot_step(q, accs):
                cols = [pl.ds(pl.multiple_of((q * PEER_QUAD + j) * SC_LANES, SC_LANES), SC_LANES)
                        for j in range(PEER_QUAD)]
                hs = [plsc.bitcast(h_v[s, col], BF16) for col in cols]
                out = []
                for r in range(PEER_CH):
                    p = plsc.bitcast(buf[slot, r, cols[0]], BF16) * hs[0]
                    for j in range(1, PEER_QUAD):
                        p = p + plsc.bitcast(buf[slot, r, cols[j]], BF16) * hs[j]
                    lo, hi = _unpack_pair(plsc.bitcast(p, jnp.int32))
                    out.append(accs[r] + lo + hi)
                return tuple(out)

            accs = lax.fori_loop(0, PEER_NWG // PEER_QUAD, dot_step,
                                 tuple(jnp.zeros((SC_LANES,), F32) for _ in range(PEER_CH)))
            tot = jnp.zeros((SC_LANES,), F32)
            for r in range(PEER_CH):
                tot = jnp.where(lane == r, jnp.sum(accs[r]), tot)
            rows = pl.ds(pl.multiple_of(c * PEER_CH, PEER_CH), PEER_CH)
            wvec = gate_v[s, rows] * _gelu_tanh_via_exp(tot)
            ws = []
            for r in range(PEER_CH):
                w = wvec.at[jnp.full((SC_LANES,), r, jnp.int32)].get(mode="promise_in_bounds")
                ws.append(plsc.pack(w, w, format=plsc.PackFormat.INTERLEAVED,
                                    preferred_element_type=BF16))
            first = c == 0

            @plsc.parallel_loop(0, PEER_NWG, unroll=2)
            def acc_step(g):
                col = pl.ds(pl.multiple_of(g * SC_LANES, SC_LANES), SC_LANES)
                col_v = pl.ds(pl.multiple_of(PEER_WORDS + g * SC_LANES, SC_LANES), SC_LANES)
                o_lo = jnp.where(first, 0.0, out_v[s, col])
                o_hi = jnp.where(first, 0.0, out_v[s, col_v])
                for r0 in range(0, PEER_CH, PEER_QUAD):
                    p = plsc.bitcast(buf[slot, r0, col_v], BF16) * ws[r0]
                    for r in range(r0 + 1, r0 + PEER_QUAD):
                        p = p + plsc.bitcast(buf[slot, r, col_v], BF16) * ws[r]
                    lo, hi = _unpack_pair(plsc.bitcast(p, jnp.int32))
                    o_lo = o_lo + lo
                    o_hi = o_hi + hi
                out_v[s, col] = o_lo
                out_v[s, col_v] = o_hi

            @pl.when(c == PEER_NCH - PEER_RING)
            def _():
                for cp in meta_copies(nxt, 1 - s):
                    cp.wait()

            ahead = c + PEER_RING
            src = jnp.where(ahead < PEER_NCH, s, 1 - s)
            nrows = idx_v[src, pl.ds(pl.multiple_of((ahead % PEER_NCH) * PEER_CH, PEER_CH), PEER_CH)]
            gather(slot, nrows).start()
            return carry

        lax.fori_loop(0, PEER_NCH, chunk, 0)
        pltpu.make_async_copy(out_v.at[s], o_hbm.at[tok], osem.at[s]).start()
        return carry

    for cp in meta_copies(base, 0):
        cp.start()
    for cp in meta_copies(base, 0):
        cp.wait()
    for c in range(PEER_RING):
        gather(c, idx_v[0, pl.ds(c * PEER_CH, PEER_CH)]).start()
    lax.fori_loop(0, n_tok, token, 0)
    for c in range(PEER_RING):
        gather(c, zero_rows).wait()
    for s in range(2):
        pltpu.make_async_copy(out_v.at[s], o_hbm.at[base], osem.at[s]).wait()


PACK_ROWS = 256


def _pack_words(x):
    half = x.shape[1] // 2
    return _bf16_bits(x[:, :half]) | lax.shift_left(_bf16_bits(x[:, half:]), 16)


def _pack_tables_kernel(u_ref, v_ref, o_ref):
    o_ref[:, :PEER_WORDS] = _pack_words(u_ref[...])
    o_ref[:, PEER_WORDS:] = _pack_words(v_ref[...])


def _pack_tables(peer_u, peer_v):
    n = peer_u.shape[0]
    rows = min(PACK_ROWS, n)
    spec = pl.BlockSpec((rows, D_MODEL), lambda i: (i, 0))
    return pl.pallas_call(
        _pack_tables_kernel,
        grid=(n // rows,),
        in_specs=[spec, spec],
        out_specs=spec,
        out_shape=jax.ShapeDtypeStruct((n, D_MODEL), jnp.int32),
        compiler_params=pltpu.CompilerParams(
            dimension_semantics=("parallel",), vmem_limit_bytes=VMEM_LIMIT),
        name="pack_tables",
    )(peer_u, peer_v)


def _peer(idx, h_words, gates, uv_words, after):
    T = h_words.shape[0]
    assert T % (2 * SC_WORKERS) == 0
    mesh = plsc.VectorSubcoreMesh(core_axis_name="c", subcore_axis_name="s",
                                  num_cores=SC_CORES, num_subcores=SC_SUBCORES)
    return pl.kernel(
        _peer_sc_body,
        out_type=jax.ShapeDtypeStruct((T, D_MODEL), F32),
        mesh=mesh,
        scratch_types=[
            pltpu.VMEM((2, PEER_SEL), jnp.int32), pltpu.VMEM((2, PEER_SEL), F32),
            pltpu.VMEM((2, PEER_WORDS), jnp.int32),
            pltpu.VMEM((PEER_RING, PEER_CH, 2 * PEER_WORDS), jnp.int32),
            pltpu.VMEM((2, D_MODEL), F32),
            pltpu.SemaphoreType.DMA((PEER_RING,)),
            pltpu.SemaphoreType.DMA((2,)), pltpu.SemaphoreType.DMA((2,)),
        ],
        compiler_params=pltpu.CompilerParams(needs_layout_passes=False),
        name="peer_sc",
    )(idx, gates, h_words, uv_words, after)


FINAL_TS = 256


def _final_kernel(x1_ref, pe_ref, p_ref, gp_ref, wg_ref, wp_ref, gf_ref, o_ref):
    x2 = x1_ref[...] + pe_ref[...]
    e = _dot(p_ref[...].astype(BF16), wp_ref[...])
    gate = jax.nn.sigmoid(_dot(_rms(x2, gp_ref[...]).astype(BF16), wg_ref[...]))
    o_ref[...] = _rms(x2 + gate * e, gf_ref[...])


def _final(x1, peer_out, p, t0, nt, g_ple, ple_w_gate, ple_w_proj, g_final):
    B = p.shape[0]
    ts = min(FINAL_TS, nt)
    nblk = nt // ts
    i0 = t0 // ts
    row = lambda d: pl.BlockSpec((ts, d), lambda b, i: (b * nblk + i, 0))
    full = lambda shape: pl.BlockSpec(shape, lambda b, i: (0,) * len(shape))
    return pl.pallas_call(
        _final_kernel,
        grid=(B, nblk),
        in_specs=[row(D_MODEL), row(D_MODEL),
                  pl.BlockSpec((None, ts, D_PLE), lambda b, i: (b, i0 + i, 0)),
                  full((1, D_MODEL)), full((D_MODEL, D_MODEL)), full((D_PLE, D_MODEL)),
                  full((1, D_MODEL))],
        out_specs=pl.BlockSpec((None, ts, D_MODEL), lambda b, i: (b, i, 0)),
        out_shape=jax.ShapeDtypeStruct((B, nt, D_MODEL), F32),
        compiler_params=pltpu.CompilerParams(
            dimension_semantics=("parallel", "parallel"), vmem_limit_bytes=VMEM_LIMIT),
        name="final",
    )(x1, peer_out, p, g_ple, ple_w_gate, ple_w_proj, g_final)


CHUNK_STEPS = (512, 512, 1024, 1024, 1024, 1024, 1024, 1024, 512, 512)


def kernel(x, p, positions, g_mix, w_in, ssm_log_dt, ssm_a_re, ssm_a_im, ssm_b_re, ssm_b_im,
           ssm_c_re, ssm_c_im, ssm_d, ssm_w_glu, w_proj_ssm, w_proj_att, w_out, g_ffn,
           peer_w_q, peer_keys1, peer_keys2, peer_u, peer_v, g_ple, ple_w_gate, ple_w_proj,
           g_final):
    B, S, _ = x.shape
    assert w_in.shape[0] == 1, "the final rmsnorm is fused into the single layer's last stage"
    steps = CHUNK_STEPS if sum(CHUNK_STEPS) == S else (S,)
    i = 0
    tables = _s5_tables(ssm_log_dt[i], ssm_a_re[i], ssm_a_im[i], ssm_b_re[i], ssm_b_im[i],
                        ssm_c_re[i], ssm_c_im[i])
    w_in_b, w_glu_b = w_in[i].astype(BF16), ssm_w_glu[i].astype(BF16)
    d_skip = ssm_d[i].reshape(1, D_SSM).astype(F32)
    merge_w = (w_proj_ssm[i].astype(BF16), w_proj_att[i].astype(BF16), w_out[i].astype(BF16),
               g_ffn[i].reshape(1, D_MODEL), peer_w_q[i].astype(BF16), peer_keys1[i], peer_keys2[i])
    final_w = (g_ple[i].reshape(1, D_MODEL), ple_w_gate[i].astype(BF16),
               ple_w_proj[i].astype(BF16), g_final.reshape(1, D_MODEL))
    uv_words = _pack_tables(peer_u[i], peer_v[i])
    k_all = jnp.zeros((B, S, D_ATT), BF16)
    v_all = jnp.zeros((B, S, D_ATT), BF16)
    carry = jnp.zeros((2, SUBLANES, D_STATE), F32)
    outs = []
    t0 = 0
    after = (carry, carry)
    peer_prev = carry
    for nt in steps:
        u_sb, q, k, v, ga, gb = _in_proj(x, positions, g_mix[i], w_in_b, t0, nt, after)
        k_all = lax.dynamic_update_slice(k_all, k, (0, t0, 0))
        v_all = lax.dynamic_update_slice(v_all, v, (0, t0, 0))
        ys, carry = _s5(u_sb, carry, tables, d_skip, w_glu_b, B)
        att = _moba(q, k_all, v_all, t0 // MOBA_BLOCK)
        x1, h_words, idx, gates = _merge(x, ys, att, ga, gb, t0, *merge_w)
        after = (gates, outs[-3] if len(outs) > 2 else carry)
        peer_out = _peer(idx, h_words, gates, uv_words, peer_prev)
        peer_prev = peer_out
        outs.append(_final(x1, peer_out, p[i], t0, nt, *final_w))
        t0 += nt
    return jnp.concatenate(outs, axis=1)
```

```python
import functools

import jax
import jax.numpy as jnp
from jax import lax
from jax.experimental import pallas as pl
from jax.experimental.pallas import tpu as pltpu
from jax.experimental.pallas import tpu_sc as plsc

F32 = jnp.float32
BF16 = jnp.bfloat16

D_MODEL = 1024
D_SSM = 512
SSM_GROUPS = 32
SSM_STATE = 64
D_STATE = SSM_GROUPS * SSM_STATE
HEAD_DIM = 64
D_ATT = 512
ROT_DIM = 16
ROPE_THETA = 500000.0
MOBA_BLOCK = 256
MOBA_TOPK = 3
PEER_HEADS = 8
PEER_KEYS = 128
PEER_QDIM = 256
PEER_HALF = 128
PEER_TOPK = 16
PEER_SEL = PEER_HEADS * PEER_TOPK
D_PLE = 256
EPS = 1e-6
NEG = -1e30
LANES = 128
SUBLANES = 8
VMEM_LIMIT = 48 * 1024 * 1024
HIGHEST = lax.Precision.HIGHEST


def _rms(x, g):
    return x * lax.rsqrt(jnp.mean(x * x, axis=-1, keepdims=True) + EPS) * g


def _dot(a, b):
    return jnp.dot(a, b, preferred_element_type=F32)


def _dot_nt(a, b, precision=None):
    return lax.dot_general(a, b, (((1,), (1,)), ((), ())), precision=precision,
                           preferred_element_type=F32)


IN_TS = 512


def _in_proj_kernel(x_ref, pos_ref, g_ref, w_ref, invf_ref, after_a, after_b,
                    u_ref, q_ref, k_ref, v_ref, ga_ref, gb_ref):
    del after_a, after_b
    h = _rms(x_ref[...], g_ref[...]).astype(BF16)

    def proj(lo, hi):
        return _dot(h, w_ref[:, lo:hi])

    u_ref[...] = proj(0, D_SSM).astype(BF16)
    ang = pos_ref[...].astype(F32) * invf_ref[...]
    cos = jnp.cos(ang)
    sin = jnp.sin(ang)
    lane = lax.broadcasted_iota(jnp.int32, (1, LANES), 1) % HEAD_DIM
    half = ROT_DIM // 2
    sin_hi = jnp.where((lane >= half) & (lane < ROT_DIM), sin, 0.0)
    sin_lo = jnp.where(lane < half, -sin, 0.0)
    reps = D_ATT // LANES
    cos4 = jnp.concatenate([cos] * reps, axis=1)
    sin_hi4 = jnp.concatenate([sin_hi] * reps, axis=1)
    sin_lo4 = jnp.concatenate([sin_lo] * reps, axis=1)

    def rope(t):
        return (t * cos4 + pltpu.roll(t, half, 1) * sin_hi4
                + pltpu.roll(t, D_ATT - half, 1) * sin_lo4)

    q = rope(proj(D_SSM, D_SSM + D_ATT))
    q_ref[...] = (q * (HEAD_DIM ** -0.5)).astype(BF16)
    k_ref[...] = rope(proj(D_SSM + D_ATT, D_SSM + 2 * D_ATT)).astype(BF16)
    v_ref[...] = proj(D_SSM + 2 * D_ATT, D_SSM + 3 * D_ATT).astype(BF16)
    o = D_SSM + 3 * D_ATT
    ga_ref[...] = jax.nn.sigmoid(proj(o, o + D_MODEL)).astype(BF16)
    gb_ref[...] = jax.nn.sigmoid(proj(o + D_MODEL, o + 2 * D_MODEL)).astype(BF16)


def _in_proj(x, positions, g_mix, w_in, t0, nt, after):
    B, S, _ = x.shape
    ts = min(IN_TS, nt)
    assert nt % ts == 0 and t0 % ts == 0
    i0 = t0 // ts
    inv_freq = ROPE_THETA ** (-jnp.arange(0, ROT_DIM, 2, dtype=F32) / ROT_DIM)
    lane = jnp.arange(LANES) % HEAD_DIM
    invf = jnp.where(lane < ROT_DIM, inv_freq[lane % (ROT_DIM // 2)], 0.0).reshape(1, LANES)
    d_in = w_in.shape[1]
    src = lambda d: pl.BlockSpec((None, ts, d), lambda b, i: (b, i0 + i, 0))
    tok = lambda d: pl.BlockSpec((None, ts, d), lambda b, i: (b, i, 0))
    full = lambda shape: pl.BlockSpec(shape, lambda b, i: (0,) * len(shape))
    return pl.pallas_call(
        _in_proj_kernel,
        grid=(B, nt // ts),
        in_specs=[src(D_MODEL), src(1), full((1, D_MODEL)), full((D_MODEL, d_in)), full((1, LANES)),
                  pl.BlockSpec(memory_space=pl.ANY), pl.BlockSpec(memory_space=pl.ANY)],
        out_specs=[pl.BlockSpec((ts, D_SSM), lambda b, i: (i, b)),
                   tok(D_ATT), tok(D_ATT), tok(D_ATT), tok(D_MODEL), tok(D_MODEL)],
        out_shape=[jax.ShapeDtypeStruct((nt, B * D_SSM), BF16),
                   jax.ShapeDtypeStruct((B, nt, D_ATT), BF16),
                   jax.ShapeDtypeStruct((B, nt, D_ATT), BF16),
                   jax.ShapeDtypeStruct((B, nt, D_ATT), BF16),
                   jax.ShapeDtypeStruct((B, nt, D_MODEL), BF16),
                   jax.ShapeDtypeStruct((B, nt, D_MODEL), BF16)],
        compiler_params=pltpu.CompilerParams(
            dimension_semantics=("parallel", "parallel"), vmem_limit_bytes=VMEM_LIMIT),
        name="in_proj",
    )(x, positions.reshape(B, S, 1), g_mix.reshape(1, D_MODEL), w_in, invf, *after)


S5_TS = 128
S5_BATCH = 4
S5_COLS = 512


def _s5_kernel(u_ref, c0_ref, bre_ref, bim_ref, a1r_ref, a1i_ref, pr_ref, pi_ref,
               cre_ref, cim_ref, d_ref, wglu_ref, y_ref, c1_ref,
               xr, xi, cr, ci, ysc):
    rows = xr.shape[0]
    ts = rows // S5_BATCH

    @pl.when(pl.program_id(0) == 0)
    def _():
        cr[...] = c0_ref[0]
        ci[...] = c0_ref[1]

    u = u_ref[...]
    for cb in range(D_STATE // S5_COLS):
        sl = slice(cb * S5_COLS, (cb + 1) * S5_COLS)
        u_cb = u[:, cb * LANES:(cb + 1) * LANES]
        xr[:, sl] = _dot(u_cb, bre_ref[cb])
        xi[:, sl] = _dot(u_cb, bim_ref[cb])

    hi_rows = lax.broadcasted_iota(jnp.int32, (SUBLANES, S5_COLS), 0) >= S5_BATCH
    for cb in range(D_STATE // S5_COLS):
        sl = slice(cb * S5_COLS, (cb + 1) * S5_COLS)
        a_r, a_i = a1r_ref[:, sl], a1i_ref[:, sl]
        p_r, p_i = pr_ref[:, sl], pi_ref[:, sl]

        def body(t, carry):
            c_r, c_i = carry
            r0 = pl.multiple_of(t * SUBLANES, SUBLANES)
            x_r = xr[pl.ds(r0, SUBLANES), sl]
            x_i = xi[pl.ds(r0, SUBLANES), sl]
            s_r = pltpu.roll(x_r, S5_BATCH, 0)
            s_i = pltpu.roll(x_i, S5_BATCH, 0)
            h_r = x_r + (a_r * s_r - a_i * s_i) + (p_r * c_r - p_i * c_i)
            h_i = x_i + (a_r * s_i + a_i * s_r) + (p_r * c_i + p_i * c_r)
            xr[pl.ds(r0, SUBLANES), sl] = h_r
            xi[pl.ds(r0, SUBLANES), sl] = h_i
            n_r = jnp.where(hi_rows, h_r, pltpu.roll(h_r, S5_BATCH, 0))
            n_i = jnp.where(hi_rows, h_i, pltpu.roll(h_i, S5_BATCH, 0))
            return n_r, n_i

        c_r, c_i = lax.fori_loop(0, rows // SUBLANES, body, (cr[:, sl], ci[:, sl]), unroll=2)
        cr[:, sl] = c_r
        ci[:, sl] = c_i

    y = jnp.concatenate(
        [_dot(xr[:, cb * S5_COLS:(cb + 1) * S5_COLS].astype(BF16), cre_ref[cb])
         - _dot(xi[:, cb * S5_COLS:(cb + 1) * S5_COLS].astype(BF16), cim_ref[cb])
         for cb in range(D_STATE // S5_COLS)], axis=1) + d_ref[...] * u.astype(F32)
    y = jax.nn.gelu(y)
    y = y * jax.nn.sigmoid(_dot(y.astype(BF16), wglu_ref[...]))
    for c in range(D_SSM // LANES):
        ysc[c] = y[:, c * LANES:(c + 1) * LANES]
    for b in range(S5_BATCH):
        for c in range(D_SSM // LANES):
            y_ref[b, :, c * LANES:(c + 1) * LANES] = (
                ysc[c, pl.ds(b, ts, stride=S5_BATCH), :].astype(BF16))

    @pl.when(pl.program_id(0) == pl.num_programs(0) - 1)
    def _():
        c1_ref[0] = cr[...]
        c1_ref[1] = ci[...]


def _s5_tables(log_dt, a_re, a_im, b_re, b_im, c_re, c_im):
    dt = jnp.exp(log_dt.astype(F32))[:, None]
    ar, ai = a_re.astype(F32), a_im.astype(F32)
    mag = jnp.exp(dt * ar)
    abar_re, abar_im = mag * jnp.cos(dt * ai), mag * jnp.sin(dt * ai)
    den = ar * ar + ai * ai
    nr, ni = abar_re - 1.0, abar_im
    f_re = (nr * ar + ni * ai) / den
    f_im = (ni * ar - nr * ai) / den
    br, bi = b_re.astype(F32), b_im.astype(F32)
    bb_re = f_re[..., None] * br - f_im[..., None] * bi
    bb_im = f_re[..., None] * bi + f_im[..., None] * br
    eye = jnp.eye(SSM_GROUPS, dtype=F32)

    def in_blockdiag(bb):
        return jnp.einsum('gnc,gh->gchn', bb, eye).reshape(D_SSM, D_STATE)

    def out_blockdiag(c):
        return jnp.einsum('gcn,gh->gnhc', c.astype(F32), eye).reshape(D_STATE, D_SSM)

    a_r = abar_re.reshape(1, D_STATE)
    a_i = abar_im.reshape(1, D_STATE)
    a2_r = a_r * a_r - a_i * a_i
    a2_i = 2.0 * a_r * a_i
    hi = (jnp.arange(SUBLANES) >= S5_BATCH)[:, None]
    a1r = jnp.where(hi, a_r, 0.0)
    a1i = jnp.where(hi, a_i, 0.0)
    p_r = jnp.where(hi, a2_r, a_r)
    p_i = jnp.where(hi, a2_i, a_i)
    nblk = D_STATE // S5_COLS
    cw = D_SSM // nblk

    def in_blocks(m):
        return jnp.stack([m[b * cw:(b + 1) * cw, b * S5_COLS:(b + 1) * S5_COLS] for b in range(nblk)])

    def out_blocks(m):
        return jnp.stack([m[b * S5_COLS:(b + 1) * S5_COLS, b * cw:(b + 1) * cw] for b in range(nblk)])

    return (in_blocks(in_blockdiag(bb_re)).astype(BF16), in_blocks(in_blockdiag(bb_im)).astype(BF16),
            a1r, a1i, p_r, p_i,
            out_blocks(out_blockdiag(c_re)).astype(BF16), out_blocks(out_blockdiag(c_im)).astype(BF16))


def _s5(u_sb, carry, tables, d_skip, w_glu, B):
    assert B == S5_BATCH
    nt = u_sb.shape[0]
    ts = min(S5_TS, nt)
    rows = ts * B
    bre, bim, a1r, a1i, p_r, p_i, cre, cim = tables
    full = lambda shape: pl.BlockSpec(shape, lambda i: (0,) * len(shape))
    return pl.pallas_call(
        _s5_kernel,
        grid=(nt // ts,),
        in_specs=[pl.BlockSpec((rows, D_SSM), lambda i: (i, 0)),
                  full((2, SUBLANES, D_STATE)),
                  full(bre.shape), full(bim.shape),
                  full((SUBLANES, D_STATE)), full((SUBLANES, D_STATE)),
                  full((SUBLANES, D_STATE)), full((SUBLANES, D_STATE)),
                  full(cre.shape), full(cim.shape),
                  full((1, D_SSM)), full((D_SSM, D_SSM))],
        out_specs=[pl.BlockSpec((B, ts, D_SSM), lambda i: (0, i, 0)),
                   full((2, SUBLANES, D_STATE))],
        out_shape=[jax.ShapeDtypeStruct((B, nt, D_SSM), BF16),
                   jax.ShapeDtypeStruct((2, SUBLANES, D_STATE), F32)],
        scratch_shapes=[pltpu.VMEM((rows, D_STATE), F32), pltpu.VMEM((rows, D_STATE), F32),
                        pltpu.VMEM((SUBLANES, D_STATE), F32), pltpu.VMEM((SUBLANES, D_STATE), F32),
                        pltpu.VMEM((D_SSM // LANES, rows, LANES), F32)],
        compiler_params=pltpu.CompilerParams(
            dimension_semantics=("arbitrary",), vmem_limit_bytes=VMEM_LIMIT),
        name="s5",
    )(u_sb.reshape(nt * B, D_SSM), carry, bre, bim, a1r, a1i, p_r, p_i, cre, cim, d_skip, w_glu)


MOBA_PAIR = 2 * MOBA_BLOCK


def _moba_kernel(q0, q_ref, k_ref, v_ref, o_ref, kmean, kaug_a, kaug_b, vaug_a, vaug_b, qaug,
                 m_s, acc_s, s_buf):
    last = pl.program_id(2) + q0 // 2
    nb = k_ref.shape[0] // MOBA_BLOCK
    nbp = kmean.shape[0]
    lane = lax.broadcasted_iota(jnp.int32, (1, LANES), 1)
    head_a = lane < HEAD_DIM

    @pl.when(pl.program_id(2) == 0)
    def _():
        kmean[...] = jnp.zeros_like(kmean)
        for j in range(nb):
            rows = pl.ds(j * MOBA_BLOCK, MOBA_BLOCK)
            kj = k_ref[rows, :].astype(F32)
            vj = v_ref[rows, :].astype(F32)
            kmean[j:j + 1, :] = jnp.sum(kj, axis=0, keepdims=True) * (1.0 / MOBA_BLOCK)
            kaug_a[rows, :] = jnp.where(head_a, kj, jnp.where(lane - HEAD_DIM == j, 1.0, 0.0)).astype(BF16)
            kaug_b[rows, :] = jnp.where(head_a, jnp.where(lane == j, 1.0, 0.0), kj).astype(BF16)
            vaug_a[rows, :] = jnp.where(head_a, vj, 1.0).astype(BF16)
            vaug_b[rows, :] = jnp.where(head_a, 1.0, vj).astype(BF16)
        blk_row = lax.broadcasted_iota(jnp.int32, (nbp, MOBA_BLOCK), 0)
        for t in range(q_ref.shape[0] // MOBA_BLOCK):
            qt = q0 + t
            qf = q_ref[t * MOBA_BLOCK:(t + 1) * MOBA_BLOCK, :].astype(F32)
            for hd, is_a in enumerate((True, False)):
                mine = head_a if is_a else jnp.logical_not(head_a)
                q_own = jnp.where(mine, qf, 0.0)
                g = _dot_nt(kmean[...], q_own, precision=HIGHEST)
                g = jnp.where(blk_row < qt, g, NEG)
                sel = jnp.zeros(g.shape, F32)
                for _ in range(MOBA_TOPK):
                    m = jnp.max(g, axis=0, keepdims=True)
                    idx = jnp.min(jnp.where(g == m, blk_row, nbp), axis=0, keepdims=True)
                    hit = blk_row == idx
                    sel = jnp.where(hit, jnp.where(idx < qt, 1.0, 0.0), sel)
                    g = jnp.where(hit, -jnp.inf, g)
                bias_t = jnp.where(sel > 0.0, 0.0, jnp.where(blk_row == qt, 0.0, NEG))
                bias_t = jnp.concatenate([bias_t, jnp.full((LANES - nbp, MOBA_BLOCK), NEG, F32)], axis=0)
                bias = jnp.transpose(bias_t)
                if is_a:
                    bias = pltpu.roll(bias, HEAD_DIM, 1)
                qaug[hd, t * MOBA_BLOCK:(t + 1) * MOBA_BLOCK, :] = jnp.where(mine, qf, bias).astype(BF16)

    tile_rows = pl.ds(pl.multiple_of(pl.program_id(2) * MOBA_PAIR, MOBA_PAIR), MOBA_PAIR)
    q_augs = [qaug[0, tile_rows, :], qaug[1, tile_rows, :]]

    m_s[...] = jnp.full(m_s.shape, -jnp.inf, F32)
    acc_s[...] = jnp.zeros_like(acc_s)
    qpos = last * MOBA_PAIR + lax.broadcasted_iota(jnp.int32, (MOBA_PAIR, MOBA_PAIR), 0)
    col = lax.broadcasted_iota(jnp.int32, (MOBA_PAIR, MOBA_PAIR), 1)

    def kv_rows(jj):
        return pl.ds(pl.multiple_of(jj * MOBA_PAIR, MOBA_PAIR), MOBA_PAIR)

    def scores(jj, slot):
        for hd, kaug in enumerate((kaug_a, kaug_b)):
            s_buf[slot, hd] = _dot_nt(q_augs[hd], kaug[kv_rows(jj), :])

    def softmax_pv(jj, slot, causal):
        for hd, vaug in enumerate((vaug_a, vaug_b)):
            s = s_buf[slot, hd]
            if causal:
                s = jnp.where(jj * MOBA_PAIR + col <= qpos, s, NEG)
            m_old = m_s[hd]
            m_new = jnp.maximum(m_old, jnp.max(s, axis=-1, keepdims=True))
            alpha = jnp.exp(m_old - m_new)
            p = jnp.exp(s - m_new)
            m_s[hd] = m_new
            acc_s[hd] = alpha * acc_s[hd] + _dot(p.astype(BF16), vaug[kv_rows(jj), :])

    scores(0, 0)

    def body(k, _):
        scores(2 * k + 1, 1)
        softmax_pv(2 * k, 0, False)
        scores(2 * k + 2, 0)
        softmax_pv(2 * k + 1, 1, False)
        return 0

    lax.fori_loop(0, last // 2, body, 0)

    @pl.when(last % 2 == 0)
    def _():
        softmax_pv(last, 0, True)

    @pl.when(last % 2 == 1)
    def _():
        scores(last, 1)
        softmax_pv(last - 1, 0, False)
        softmax_pv(last, 1, True)
    acc_a, acc_b = acc_s[0], acc_s[1]
    o_ref[...] = jnp.where(head_a, acc_a / pltpu.roll(acc_a, HEAD_DIM, 1),
                           acc_b / pltpu.roll(acc_b, HEAD_DIM, 1)).astype(BF16)


def _moba(q, k, v, q0):
    B = q.shape[0]
    nq = q.shape[1] // MOBA_BLOCK
    skv = (q0 + nq) * MOBA_BLOCK
    nb = skv // MOBA_BLOCK
    assert nb <= HEAD_DIM and nb % 2 == 0 and skv <= k.shape[1]
    nbp = -(-nb // SUBLANES) * SUBLANES
    assert q0 % 2 == 0 and nq % 2 == 0
    blk = pl.BlockSpec((None, MOBA_PAIR, LANES), lambda b, h, i: (b, i, h))
    seq = pl.BlockSpec((None, skv, LANES), lambda b, h, i: (b, 0, h))
    return pl.pallas_call(
        functools.partial(_moba_kernel, q0),
        grid=(B, D_ATT // LANES, nq // 2),
        in_specs=[pl.BlockSpec((None, nq * MOBA_BLOCK, LANES), lambda b, h, i: (b, 0, h)), seq, seq],
        out_specs=blk,
        out_shape=jax.ShapeDtypeStruct(q.shape, BF16),
        scratch_shapes=[pltpu.VMEM((nbp, LANES), F32),
                        pltpu.VMEM((skv, LANES), BF16), pltpu.VMEM((skv, LANES), BF16),
                        pltpu.VMEM((skv, LANES), BF16), pltpu.VMEM((skv, LANES), BF16),
                        pltpu.VMEM((2, nq * MOBA_BLOCK, LANES), BF16),
                        pltpu.VMEM((2, MOBA_PAIR, 1), F32),
                        pltpu.VMEM((2, MOBA_PAIR, LANES), F32),
                        pltpu.VMEM((2, 2, MOBA_PAIR, MOBA_PAIR), F32)],
        compiler_params=pltpu.CompilerParams(
            dimension_semantics=("parallel", "parallel", "arbitrary"), vmem_limit_bytes=VMEM_LIMIT),
        name="moba",
    )(q, k, v)


MERGE_TS = 256


def _bf16_bits(x):
    b = pltpu.bitcast(x, jnp.int32)
    r = b + 0x7FFF + (lax.shift_right_logical(b, 16) & 1)
    return lax.shift_right_logical(r, 16)


def _merge_kernel(x_ref, ys_ref, at_ref, ga_ref, gb_ref, wa_ref, wb_ref, wo_ref, g_ref,
                  wq_ref, k1_ref, k2_ref, after_ref, x1_ref, hw_ref, idx_ref, gate_ref, sc_ref):
    del after_ref
    ya = _dot(ys_ref[...], wa_ref[...])
    yb = _dot(at_ref[...], wb_ref[...])
    merged = ga_ref[...].astype(F32) * ya + gb_ref[...].astype(F32) * yb
    x1 = x_ref[...] + _dot(merged.astype(BF16), wo_ref[...])
    x1_ref[...] = x1
    hq = _rms(x1, g_ref[...])
    hw_ref[...] = _pack_words(hq)
    qp = _dot(hq.astype(BF16), wq_ref[...])
    for h in range(PEER_HEADS):
        o = h * PEER_QDIM
        sc_ref[2 * h] = _dot_nt(k1_ref[h], qp[:, o:o + PEER_HALF], precision=HIGHEST)
        sc_ref[2 * h + 1] = _dot_nt(k2_ref[h], qp[:, o + PEER_HALF:o + PEER_QDIM], precision=HIGHEST)
    _topk_kernel(sc_ref, idx_ref, gate_ref)


def _merge(x, ys, att, ga, gb, t0, after, w_proj_ssm, w_proj_att, w_out, g_ffn, peer_w_q, keys1, keys2):
    B, nt = ys.shape[0], ys.shape[1]
    ts = min(MERGE_TS, nt)
    nblk = nt // ts
    i0 = t0 // ts
    tok = lambda d: pl.BlockSpec((None, ts, d), lambda b, i: (b, i, 0))
    row = lambda d: pl.BlockSpec((ts, d), lambda b, i: (b * nblk + i, 0))
    full = lambda shape: pl.BlockSpec(shape, lambda b, i: (0,) * len(shape))
    qd = PEER_HEADS * PEER_QDIM
    return pl.pallas_call(
        _merge_kernel,
        grid=(B, nblk),
        in_specs=[pl.BlockSpec((None, ts, D_MODEL), lambda b, i: (b, i0 + i, 0)),
                  tok(D_SSM), tok(D_ATT), tok(D_MODEL), tok(D_MODEL),
                  full((D_SSM, D_MODEL)), full((D_ATT, D_MODEL)), full((D_MODEL, D_MODEL)),
                  full((1, D_MODEL)), full((D_MODEL, qd)),
                  full((PEER_HEADS, PEER_KEYS, PEER_HALF)), full((PEER_HEADS, PEER_KEYS, PEER_HALF)),
                  pl.BlockSpec(memory_space=pl.ANY)],
        out_specs=[row(D_MODEL), row(D_MODEL // 2), row(PEER_SEL), row(PEER_SEL)],
        out_shape=[jax.ShapeDtypeStruct((B * nt, D_MODEL), F32),
                   jax.ShapeDtypeStruct((B * nt, D_MODEL // 2), jnp.int32),
                   jax.ShapeDtypeStruct((B * nt, PEER_SEL), jnp.int32),
                   jax.ShapeDtypeStruct((B * nt, PEER_SEL), F32)],
        scratch_shapes=[pltpu.VMEM((2 * PEER_HEADS, PEER_KEYS, ts), F32)],
        compiler_params=pltpu.CompilerParams(
            dimension_semantics=("parallel", "parallel"), vmem_limit_bytes=VMEM_LIMIT),
        name="merge",
    )(x, ys, att, ga, gb, w_proj_ssm, w_proj_att, w_out, g_ffn, peer_w_q, keys1, keys2, after)


def _top_rows(s, row, k):
    vals, idxs = [], []
    for _ in range(k):
        m = jnp.max(s, axis=0, keepdims=True)
        idx = jnp.min(jnp.where(s == m, row, s.shape[0]), axis=0, keepdims=True)
        vals.append(m)
        idxs.append(idx)
        s = jnp.where(row == idx, -jnp.inf, s)
    return vals, idxs


def _stack_rows(rows, row16):
    acc = jnp.zeros(row16.shape, rows[0].dtype)
    for r, v in enumerate(rows):
        acc = jnp.where(row16 == r, v, acc)
    return acc


def _topk_kernel(sc_ref, idx_ref, gate_ref):
    ts = sc_ref.shape[-1]
    row = lax.broadcasted_iota(jnp.int32, (PEER_KEYS, ts), 0).astype(F32)
    row16 = lax.broadcasted_iota(jnp.int32, (PEER_TOPK, ts), 0)
    row8 = lax.broadcasted_iota(jnp.int32, (SUBLANES, ts), 0)
    counts = [PEER_TOPK // (i + 1) for i in range(PEER_TOPK)]
    heights = [PEER_TOPK if c > SUBLANES else SUBLANES for c in counts]
    n_cand = sum(heights)
    rowc = lax.broadcasted_iota(jnp.int32, (n_cand, ts), 0).astype(F32)
    gate_rows, eid_rows = [], []
    for h in range(PEER_HEADS):
        v1, i1 = _top_rows(sc_ref[2 * h], row, PEER_TOPK)
        v2, i2 = _top_rows(sc_ref[2 * h + 1], row, PEER_TOPK)
        v2s = _stack_rows(v2, row16)
        i2s = _stack_rows(i2, row16)
        cand, eid = [], []
        for i in range(PEER_TOPK):
            n = heights[i]
            cand.append(jnp.where((row16 if n == PEER_TOPK else row8) < counts[i],
                                  v1[i] + v2s[:n], -jnp.inf))
            eid.append(i1[i] * PEER_KEYS + i2s[:n])
        cand = jnp.concatenate(cand, axis=0)
        eid = jnp.concatenate(eid, axis=0)
        tops, picks = [], []
        for _ in range(PEER_TOPK):
            m = jnp.max(cand, axis=0, keepdims=True)
            pos = jnp.min(jnp.where(cand == m, rowc, n_cand), axis=0, keepdims=True)
            hit = rowc == pos
            picks.append(jnp.max(jnp.where(hit, eid, -1.0), axis=0, keepdims=True))
            tops.append(m)
            cand = jnp.where(hit, -jnp.inf, cand)
        top = _stack_rows(tops, row16)
        p = jnp.exp(top - jnp.max(top, axis=0, keepdims=True))
        gate_rows.append(p / jnp.sum(p, axis=0, keepdims=True))
        eid_rows.append(_stack_rows(picks, row16))
    gate_ref[...] = jnp.transpose(jnp.concatenate(gate_rows, axis=0))
    idx_ref[...] = jnp.transpose(jnp.concatenate(eid_rows, axis=0)).astype(jnp.int32)


SC_CORES = 2
SC_SUBCORES = 16
SC_LANES = 16
SC_WORKERS = SC_CORES * SC_SUBCORES
PEER_CH = SC_LANES
PEER_NCH = PEER_SEL // PEER_CH
PEER_WORDS = D_MODEL // 2
PEER_NWG = PEER_WORDS // SC_LANES
PEER_RING = 4
PEER_QUAD = 4
HI_MASK = -65536
GELU_C = 0.7978845608028654


def _gelu_tanh_via_exp(x):
    z = GELU_C * (x + 0.044715 * (x * x * x))
    t = 1.0 - 2.0 / (jnp.exp(2.0 * z) + 1.0)
    return 0.5 * x * (1.0 + t)


def _unpack_pair(w):
    lo = plsc.bitcast(lax.shift_left(w, 16), F32)
    hi = plsc.bitcast(lax.bitwise_and(w, HI_MASK), F32)
    return lo, hi


def _peer_sc_body(idx_hbm, gate_hbm, h_hbm, uv_hbm, after_hbm, o_hbm,
                  idx_v, gate_v, h_v, buf, out_v, gsem, msem, osem):
    n_tok = o_hbm.shape[0] // SC_WORKERS
    base = (lax.axis_index("s") * SC_CORES + lax.axis_index("c")) * n_tok
    lane = lax.iota(jnp.int32, SC_LANES)
    zero_rows = jnp.zeros((SC_LANES,), jnp.int32)

    def meta_copies(tok, s):
        return (pltpu.make_async_copy(idx_hbm.at[tok], idx_v.at[s], msem.at[s]),
                pltpu.make_async_copy(gate_hbm.at[tok], gate_v.at[s], msem.at[s]),
                pltpu.make_async_copy(h_hbm.at[tok], h_v.at[s], msem.at[s]))

    def gather(slot, rows):
        return pltpu.make_async_copy(uv_hbm.at[rows], buf.at[slot], gsem.at[slot])

    def token(t, carry):
        s = t % 2
        tok = base + t
        nxt = base + jnp.minimum(t + 1, n_tok - 1)
        for cp in meta_copies(nxt, 1 - s):
            cp.start()

        @pl.when(t >= 2)
        def _():
            pltpu.make_async_copy(out_v.at[s], o_hbm.at[tok], osem.at[s]).wait()

        def chunk(c, carry):
            slot = c % PEER_RING
            gather(slot, zero_rows).wait()

            def dot_step(q, accs):
                cols = [pl.ds(pl.multiple_of((q * PEER_QUAD + j) * SC_LANES, SC_LANES), SC_LANES)
                        for j in range(PEER_QUAD)]
                hs = [plsc.bitcast(h_v[s, col], BF16) for col in cols]
                out = []
                for r in range(PEER_CH):
                    p = plsc.bitcast(buf[slot, r, cols[0]], BF16) * hs[0]
                    for j in range(1, PEER_QUAD):
                        p = p + plsc.bitcast(buf[slot, r, cols[j]], BF16) * hs[j]
                    lo, hi = _unpack_pair(plsc.bitcast(p, jnp.int32))
                    out.append(accs[r] + lo + hi)
                return tuple(out)

            accs = lax.fori_loop(0, PEER_NWG // PEER_QUAD, dot_step,
                                 tuple(jnp.zeros((SC_LANES,), F32) for _ in range(PEER_CH)))
            tot = jnp.zeros((SC_LANES,), F32)
            for r in range(PEER_CH):
                tot = jnp.where(lane == r, jnp.sum(accs[r]), tot)
            rows = pl.ds(pl.multiple_of(c * PEER_CH, PEER_CH), PEER_CH)
            wvec = gate_v[s, rows] * _gelu_tanh_via_exp(tot)
            ws = []
            for r in range(PEER_CH):
                w = wvec.at[jnp.full((SC_LANES,), r, jnp.int32)].get(mode="promise_in_bounds")
                ws.append(plsc.pack(w, w, format=plsc.PackFormat.INTERLEAVED,
                                    preferred_element_type=BF16))
            first = c == 0

            @plsc.parallel_loop(0, PEER_NWG, unroll=2)
            def acc_step(g):
                col = pl.ds(pl.multiple_of(g * SC_LANES, SC_LANES), SC_LANES)
                col_v = pl.ds(pl.multiple_of(PEER_WORDS + g * SC_LANES, SC_LANES), SC_LANES)
                o_lo = jnp.where(first, 0.0, out_v[s, col])
                o_hi = jnp.where(first, 0.0, out_v[s, col_v])
                for r0 in range(0, PEER_CH, PEER_QUAD):
                    p = plsc.bitcast(buf[slot, r0, col_v], BF16) * ws[r0]
                    for r in range(r0 + 1, r0 + PEER_QUAD):
                        p = p + plsc.bitcast(buf[slot, r, col_v], BF16) * ws[r]
                    lo, hi = _unpack_pair(plsc.bitcast(p, jnp.int32))
                    o_lo = o_lo + lo
                    o_hi = o_hi + hi
                out_v[s, col] = o_lo
                out_v[s, col_v] = o_hi

            @pl.when(c == PEER_NCH - PEER_RING)
            def _():
                for cp in meta_copies(nxt, 1 - s):
                    cp.wait()

            ahead = c + PEER_RING
            src = jnp.where(ahead < PEER_NCH, s, 1 - s)
            nrows = idx_v[src, pl.ds(pl.multiple_of((ahead % PEER_NCH) * PEER_CH, PEER_CH), PEER_CH)]
            gather(slot, nrows).start()
            return carry

        lax.fori_loop(0, PEER_NCH, chunk, 0)
        pltpu.make_async_copy(out_v.at[s], o_hbm.at[tok], osem.at[s]).start()
        return carry

    for cp in meta_copies(base, 0):
        cp.start()
    for cp in meta_copies(base, 0):
        cp.wait()
    for c in range(PEER_RING):
        gather(c, idx_v[0, pl.ds(c * PEER_CH, PEER_CH)]).start()
    lax.fori_loop(0, n_tok, token, 0)
    for c in range(PEER_RING):
        gather(c, zero_rows).wait()
    for s in range(2):
        pltpu.make_async_copy(out_v.at[s], o_hbm.at[base], osem.at[s]).wait()


PACK_ROWS = 256


def _pack_words(x):
    half = x.shape[1] // 2
    return _bf16_bits(x[:, :half]) | lax.shift_left(_bf16_bits(x[:, half:]), 16)


def _pack_tables_kernel(u_ref, v_ref, o_ref):
    o_ref[:, :PEER_WORDS] = _pack_words(u_ref[...])
    o_ref[:, PEER_WORDS:] = _pack_words(v_ref[...])


def _pack_tables(peer_u, peer_v):
    n = peer_u.shape[0]
    rows = min(PACK_ROWS, n)
    spec = pl.BlockSpec((rows, D_MODEL), lambda i: (i, 0))
    return pl.pallas_call(
        _pack_tables_kernel,
        grid=(n // rows,),
        in_specs=[spec, spec],
        out_specs=spec,
        out_shape=jax.ShapeDtypeStruct((n, D_MODEL), jnp.int32),
        compiler_params=pltpu.CompilerParams(
            dimension_semantics=("parallel",), vmem_limit_bytes=VMEM_LIMIT),
        name="pack_tables",
    )(peer_u, peer_v)


def _peer(idx, h_words, gates, uv_words, after):
    T = h_words.shape[0]
    assert T % (2 * SC_WORKERS) == 0
    mesh = plsc.VectorSubcoreMesh(core_axis_name="c", subcore_axis_name="s",
                                  num_cores=SC_CORES, num_subcores=SC_SUBCORES)
    return pl.kernel(
        _peer_sc_body,
        out_type=jax.ShapeDtypeStruct((T, D_MODEL), F32),
        mesh=mesh,
        scratch_types=[
            pltpu.VMEM((2, PEER_SEL), jnp.int32), pltpu.VMEM((2, PEER_SEL), F32),
            pltpu.VMEM((2, PEER_WORDS), jnp.int32),
            pltpu.VMEM((PEER_RING, PEER_CH, 2 * PEER_WORDS), jnp.int32),
            pltpu.VMEM((2, D_MODEL), F32),
            pltpu.SemaphoreType.DMA((PEER_RING,)),
            pltpu.SemaphoreType.DMA((2,)), pltpu.SemaphoreType.DMA((2,)),
        ],
        compiler_params=pltpu.CompilerParams(needs_layout_passes=False),
        name="peer_sc",
    )(idx, gates, h_words, uv_words, after)


FINAL_TS = 256


def _final_kernel(x1_ref, pe_ref, p_ref, gp_ref, wg_ref, wp_ref, gf_ref, o_ref):
    x2 = x1_ref[...] + pe_ref[...]
    e = _dot(p_ref[...].astype(BF16), wp_ref[...])
    gate = jax.nn.sigmoid(_dot(_rms(x2, gp_ref[...]).astype(BF16), wg_ref[...]))
    o_ref[...] = _rms(x2 + gate * e, gf_ref[...])


def _final(x1, peer_out, p, t0, nt, g_ple, ple_w_gate, ple_w_proj, g_final):
    B = p.shape[0]
    ts = min(FINAL_TS, nt)
    nblk = nt // ts
    i0 = t0 // ts
    row = lambda d: pl.BlockSpec((ts, d), lambda b, i: (b * nblk + i, 0))
    full = lambda shape: pl.BlockSpec(shape, lambda b, i: (0,) * len(shape))
    return pl.pallas_call(
        _final_kernel,
        grid=(B, nblk),
        in_specs=[row(D_MODEL), row(D_MODEL),
                  pl.BlockSpec((None, ts, D_PLE), lambda b, i: (b, i0 + i, 0)),
                  full((1, D_MODEL)), full((D_MODEL, D_MODEL)), full((D_PLE, D_MODEL)),
                  full((1, D_MODEL))],
        out_specs=pl.BlockSpec((None, ts, D_MODEL), lambda b, i: (b, i, 0)),
        out_shape=jax.ShapeDtypeStruct((B, nt, D_MODEL), F32),
        compiler_params=pltpu.CompilerParams(
            dimension_semantics=("parallel", "parallel"), vmem_limit_bytes=VMEM_LIMIT),
        name="final",
    )(x1, peer_out, p, g_ple, ple_w_gate, ple_w_proj, g_final)


CHUNK_STEPS = (512, 512, 1024, 1024, 1024, 1024, 1024, 1024, 512, 512)


def kernel(x, p, positions, g_mix, w_in, ssm_log_dt, ssm_a_re, ssm_a_im, ssm_b_re, ssm_b_im,
           ssm_c_re, ssm_c_im, ssm_d, ssm_w_glu, w_proj_ssm, w_proj_att, w_out, g_ffn,
           peer_w_q, peer_keys1, peer_keys2, peer_u, peer_v, g_ple, ple_w_gate, ple_w_proj,
           g_final):
    B, S, _ = x.shape
    assert w_in.shape[0] == 1, "the final rmsnorm is fused into the single layer's last stage"
    steps = CHUNK_STEPS if sum(CHUNK_STEPS) == S else (S,)
    i = 0
    tables = _s5_tables(ssm_log_dt[i], ssm_a_re[i], ssm_a_im[i], ssm_b_re[i], ssm_b_im[i],
                        ssm_c_re[i], ssm_c_im[i])
    w_in_b, w_glu_b = w_in[i].astype(BF16), ssm_w_glu[i].astype(BF16)
    d_skip = ssm_d[i].reshape(1, D_SSM).astype(F32)
    merge_w = (w_proj_ssm[i].astype(BF16), w_proj_att[i].astype(BF16), w_out[i].astype(BF16),
               g_ffn[i].reshape(1, D_MODEL), peer_w_q[i].astype(BF16), peer_keys1[i], peer_keys2[i])
    final_w = (g_ple[i].reshape(1, D_MODEL), ple_w_gate[i].astype(BF16),
               ple_w_proj[i].astype(BF16), g_final.reshape(1, D_MODEL))
    uv_words = _pack_tables(peer_u[i], peer_v[i])
    k_all = jnp.zeros((B, S, D_ATT), BF16)
    v_all = jnp.zeros((B, S, D_ATT), BF16)
    carry = jnp.zeros((2, SUBLANES, D_STATE), F32)
    outs = []
    t0 = 0
    after = (carry, carry)
    peers = []
    for nt in steps:
        u_sb, q, k, v, ga, gb = _in_proj(x, positions, g_mix[i], w_in_b, t0, nt, after)
        k_all = lax.dynamic_update_slice(k_all, k, (0, t0, 0))
        v_all = lax.dynamic_update_slice(v_all, v, (0, t0, 0))
        ys, carry = _s5(u_sb, carry, tables, d_skip, w_glu_b, B)
        att = _moba(q, k_all, v_all, t0 // MOBA_BLOCK)
        x1, h_words, idx, gates = _merge(x, ys, att, ga, gb, t0,
                                         peers[-2] if len(peers) > 1 else carry, *merge_w)
        after = (gates, outs[-3] if len(outs) > 2 else carry)
        peer_out = _peer(idx, h_words, gates, uv_words, peers[-1] if t0 + nt == S else carry)
        peers.append(peer_out)
        outs.append(_final(x1, peer_out, p[i], t0, nt, *final_w))
        t0 += nt
    return jnp.concatenate(outs, axis=1)
```

```python
import functools

import jax
import jax.numpy as jnp
from jax import lax
from jax.experimental import pallas as pl
from jax.experimental.pallas import tpu as pltpu
from jax.experimental.pallas import tpu_sc as plsc

F32 = jnp.float32
BF16 = jnp.bfloat16

D_MODEL = 1024
D_SSM = 512
SSM_GROUPS = 32
SSM_STATE = 64
D_STATE = SSM_GROUPS * SSM_STATE
HEAD_DIM = 64
D_ATT = 512
ROT_DIM = 16
ROPE_THETA = 500000.0
MOBA_BLOCK = 256
MOBA_TOPK = 3
PEER_HEADS = 8
PEER_KEYS = 128
PEER_QDIM = 256
PEER_HALF = 128
PEER_TOPK = 16
PEER_SEL = PEER_HEADS * PEER_TOPK
D_PLE = 256
EPS = 1e-6
NEG = -1e30
LANES = 128
SUBLANES = 8
VMEM_LIMIT = 48 * 1024 * 1024
HIGHEST = lax.Precision.HIGHEST


def _rms(x, g):
    return x * lax.rsqrt(jnp.mean(x * x, axis=-1, keepdims=True) + EPS) * g


def _dot(a, b):
    return jnp.dot(a, b, preferred_element_type=F32)


def _dot_nt(a, b, precision=None):
    return lax.dot_general(a, b, (((1,), (1,)), ((), ())), precision=precision,
                           preferred_element_type=F32)


IN_TS = 512


def _in_proj_kernel(x_ref, pos_ref, g_ref, w_ref, invf_ref, after_a, after_b,
                    u_ref, q_ref, k_ref, v_ref, ga_ref, gb_ref):
    del after_a, after_b
    h = _rms(x_ref[...], g_ref[...]).astype(BF16)

    def proj(lo, hi):
        return _dot(h, w_ref[:, lo:hi])

    u_ref[...] = proj(0, D_SSM).astype(BF16)
    ang = pos_ref[...].astype(F32) * invf_ref[...]
    cos = jnp.cos(ang)
    sin = jnp.sin(ang)
    lane = lax.broadcasted_iota(jnp.int32, (1, LANES), 1) % HEAD_DIM
    half = ROT_DIM // 2
    sin_hi = jnp.where((lane >= half) & (lane < ROT_DIM), sin, 0.0)
    sin_lo = jnp.where(lane < half, -sin, 0.0)
    reps = D_ATT // LANES
    cos4 = jnp.concatenate([cos] * reps, axis=1)
    sin_hi4 = jnp.concatenate([sin_hi] * reps, axis=1)
    sin_lo4 = jnp.concatenate([sin_lo] * reps, axis=1)

    def rope(t):
        return (t * cos4 + pltpu.roll(t, half, 1) * sin_hi4
                + pltpu.roll(t, D_ATT - half, 1) * sin_lo4)

    q = rope(proj(D_SSM, D_SSM + D_ATT))
    q_ref[...] = (q * (HEAD_DIM ** -0.5)).astype(BF16)
    k_ref[...] = rope(proj(D_SSM + D_ATT, D_SSM + 2 * D_ATT)).astype(BF16)
    v_ref[...] = proj(D_SSM + 2 * D_ATT, D_SSM + 3 * D_ATT).astype(BF16)
    o = D_SSM + 3 * D_ATT
    ga_ref[...] = jax.nn.sigmoid(proj(o, o + D_MODEL)).astype(BF16)
    gb_ref[...] = jax.nn.sigmoid(proj(o + D_MODEL, o + 2 * D_MODEL)).astype(BF16)


def _in_proj(x, positions, g_mix, w_in, t0, nt, after):
    B, S, _ = x.shape
    ts = min(IN_TS, nt)
    assert nt % ts == 0 and t0 % ts == 0
    i0 = t0 // ts
    inv_freq = ROPE_THETA ** (-jnp.arange(0, ROT_DIM, 2, dtype=F32) / ROT_DIM)
    lane = jnp.arange(LANES) % HEAD_DIM
    invf = jnp.where(lane < ROT_DIM, inv_freq[lane % (ROT_DIM // 2)], 0.0).reshape(1, LANES)
    d_in = w_in.shape[1]
    src = lambda d: pl.BlockSpec((None, ts, d), lambda b, i: (b, i0 + i, 0))
    tok = lambda d: pl.BlockSpec((None, ts, d), lambda b, i: (b, i, 0))
    full = lambda shape: pl.BlockSpec(shape, lambda b, i: (0,) * len(shape))
    return pl.pallas_call(
        _in_proj_kernel,
        grid=(B, nt // ts),
        in_specs=[src(D_MODEL), src(1), full((1, D_MODEL)), full((D_MODEL, d_in)), full((1, LANES)),
                  pl.BlockSpec(memory_space=pl.ANY), pl.BlockSpec(memory_space=pl.ANY)],
        out_specs=[pl.BlockSpec((ts, D_SSM), lambda b, i: (i, b)),
                   tok(D_ATT), tok(D_ATT), tok(D_ATT), tok(D_MODEL), tok(D_MODEL)],
        out_shape=[jax.ShapeDtypeStruct((nt, B * D_SSM), BF16),
                   jax.ShapeDtypeStruct((B, nt, D_ATT), BF16),
                   jax.ShapeDtypeStruct((B, nt, D_ATT), BF16),
                   jax.ShapeDtypeStruct((B, nt, D_ATT), BF16),
                   jax.ShapeDtypeStruct((B, nt, D_MODEL), BF16),
                   jax.ShapeDtypeStruct((B, nt, D_MODEL), BF16)],
        compiler_params=pltpu.CompilerParams(
            dimension_semantics=("parallel", "parallel"), vmem_limit_bytes=VMEM_LIMIT),
        name="in_proj",
    )(x, positions.reshape(B, S, 1), g_mix.reshape(1, D_MODEL), w_in, invf, *after)


S5_TS = 128
S5_BATCH = 4
S5_COLS = 512


def _s5_kernel(u_ref, c0_ref, bre_ref, bim_ref, a1r_ref, a1i_ref, pr_ref, pi_ref,
               cre_ref, cim_ref, d_ref, wglu_ref, y_ref, c1_ref,
               xr, xi, cr, ci, ysc):
    rows = xr.shape[0]
    ts = rows // S5_BATCH

    @pl.when(pl.program_id(0) == 0)
    def _():
        cr[...] = c0_ref[0]
        ci[...] = c0_ref[1]

    u = u_ref[...]
    for cb in range(D_STATE // S5_COLS):
        sl = slice(cb * S5_COLS, (cb + 1) * S5_COLS)
        u_cb = u[:, cb * LANES:(cb + 1) * LANES]
        xr[:, sl] = _dot(u_cb, bre_ref[cb])
        xi[:, sl] = _dot(u_cb, bim_ref[cb])

    hi_rows = lax.broadcasted_iota(jnp.int32, (SUBLANES, S5_COLS), 0) >= S5_BATCH
    for cb in range(D_STATE // S5_COLS):
        sl = slice(cb * S5_COLS, (cb + 1) * S5_COLS)
        a_r, a_i = a1r_ref[:, sl], a1i_ref[:, sl]
        p_r, p_i = pr_ref[:, sl], pi_ref[:, sl]

        def body(t, carry):
            c_r, c_i = carry
            r0 = pl.multiple_of(t * SUBLANES, SUBLANES)
            x_r = xr[pl.ds(r0, SUBLANES), sl]
            x_i = xi[pl.ds(r0, SUBLANES), sl]
            s_r = pltpu.roll(x_r, S5_BATCH, 0)
            s_i = pltpu.roll(x_i, S5_BATCH, 0)
            h_r = x_r + (a_r * s_r - a_i * s_i) + (p_r * c_r - p_i * c_i)
            h_i = x_i + (a_r * s_i + a_i * s_r) + (p_r * c_i + p_i * c_r)
            xr[pl.ds(r0, SUBLANES), sl] = h_r
            xi[pl.ds(r0, SUBLANES), sl] = h_i
            n_r = jnp.where(hi_rows, h_r, pltpu.roll(h_r, S5_BATCH, 0))
            n_i = jnp.where(hi_rows, h_i, pltpu.roll(h_i, S5_BATCH, 0))
            return n_r, n_i

        c_r, c_i = lax.fori_loop(0, rows // SUBLANES, body, (cr[:, sl], ci[:, sl]), unroll=2)
        cr[:, sl] = c_r
        ci[:, sl] = c_i

    y = jnp.concatenate(
        [_dot(xr[:, cb * S5_COLS:(cb + 1) * S5_COLS].astype(BF16), cre_ref[cb])
         - _dot(xi[:, cb * S5_COLS:(cb + 1) * S5_COLS].astype(BF16), cim_ref[cb])
         for cb in range(D_STATE // S5_COLS)], axis=1) + d_ref[...] * u.astype(F32)
    y = jax.nn.gelu(y)
    y = y * jax.nn.sigmoid(_dot(y.astype(BF16), wglu_ref[...]))
    for c in range(D_SSM // LANES):
        ysc[c] = y[:, c * LANES:(c + 1) * LANES]
    for b in range(S5_BATCH):
        for c in range(D_SSM // LANES):
            y_ref[b, :, c * LANES:(c + 1) * LANES] = (
                ysc[c, pl.ds(b, ts, stride=S5_BATCH), :].astype(BF16))

    @pl.when(pl.program_id(0) == pl.num_programs(0) - 1)
    def _():
        c1_ref[0] = cr[...]
        c1_ref[1] = ci[...]


def _s5_tables(log_dt, a_re, a_im, b_re, b_im, c_re, c_im):
    dt = jnp.exp(log_dt.astype(F32))[:, None]
    ar, ai = a_re.astype(F32), a_im.astype(F32)
    mag = jnp.exp(dt * ar)
    abar_re, abar_im = mag * jnp.cos(dt * ai), mag * jnp.sin(dt * ai)
    den = ar * ar + ai * ai
    nr, ni = abar_re - 1.0, abar_im
    f_re = (nr * ar + ni * ai) / den
    f_im = (ni * ar - nr * ai) / den
    br, bi = b_re.astype(F32), b_im.astype(F32)
    bb_re = f_re[..., None] * br - f_im[..., None] * bi
    bb_im = f_re[..., None] * bi + f_im[..., None] * br
    eye = jnp.eye(SSM_GROUPS, dtype=F32)

    def in_blockdiag(bb):
        return jnp.einsum('gnc,gh->gchn', bb, eye).reshape(D_SSM, D_STATE)

    def out_blockdiag(c):
        return jnp.einsum('gcn,gh->gnhc', c.astype(F32), eye).reshape(D_STATE, D_SSM)

    a_r = abar_re.reshape(1, D_STATE)
    a_i = abar_im.reshape(1, D_STATE)
    a2_r = a_r * a_r - a_i * a_i
    a2_i = 2.0 * a_r * a_i
    hi = (jnp.arange(SUBLANES) >= S5_BATCH)[:, None]
    a1r = jnp.where(hi, a_r, 0.0)
    a1i = jnp.where(hi, a_i, 0.0)
    p_r = jnp.where(hi, a2_r, a_r)
    p_i = jnp.where(hi, a2_i, a_i)
    nblk = D_STATE // S5_COLS
    cw = D_SSM // nblk

    def in_blocks(m):
        return jnp.stack([m[b * cw:(b + 1) * cw, b * S5_COLS:(b + 1) * S5_COLS] for b in range(nblk)])

    def out_blocks(m):
        return jnp.stack([m[b * S5_COLS:(b + 1) * S5_COLS, b * cw:(b + 1) * cw] for b in range(nblk)])

    return (in_blocks(in_blockdiag(bb_re)).astype(BF16), in_blocks(in_blockdiag(bb_im)).astype(BF16),
            a1r, a1i, p_r, p_i,
            out_blocks(out_blockdiag(c_re)).astype(BF16), out_blocks(out_blockdiag(c_im)).astype(BF16))


def _s5(u_sb, carry, tables, d_skip, w_glu, B):
    assert B == S5_BATCH
    nt = u_sb.shape[0]
    ts = min(S5_TS, nt)
    rows = ts * B
    bre, bim, a1r, a1i, p_r, p_i, cre, cim = tables
    full = lambda shape: pl.BlockSpec(shape, lambda i: (0,) * len(shape))
    return pl.pallas_call(
        _s5_kernel,
        grid=(nt // ts,),
        in_specs=[pl.BlockSpec((rows, D_SSM), lambda i: (i, 0)),
                  full((2, SUBLANES, D_STATE)),
                  full(bre.shape), full(bim.shape),
                  full((SUBLANES, D_STATE)), full((SUBLANES, D_STATE)),
                  full((SUBLANES, D_STATE)), full((SUBLANES, D_STATE)),
                  full(cre.shape), full(cim.shape),
                  full((1, D_SSM)), full((D_SSM, D_SSM))],
        out_specs=[pl.BlockSpec((B, ts, D_SSM), lambda i: (0, i, 0)),
                   full((2, SUBLANES, D_STATE))],
        out_shape=[jax.ShapeDtypeStruct((B, nt, D_SSM), BF16),
                   jax.ShapeDtypeStruct((2, SUBLANES, D_STATE), F32)],
        scratch_shapes=[pltpu.VMEM((rows, D_STATE), F32), pltpu.VMEM((rows, D_STATE), F32),
                        pltpu.VMEM((SUBLANES, D_STATE), F32), pltpu.VMEM((SUBLANES, D_STATE), F32),
                        pltpu.VMEM((D_SSM // LANES, rows, LANES), F32)],
        compiler_params=pltpu.CompilerParams(
            dimension_semantics=("arbitrary",), vmem_limit_bytes=VMEM_LIMIT),
        name="s5",
    )(u_sb.reshape(nt * B, D_SSM), carry, bre, bim, a1r, a1i, p_r, p_i, cre, cim, d_skip, w_glu)


MOBA_PAIR = 2 * MOBA_BLOCK


def _moba_kernel(q0, q_ref, k_ref, v_ref, o_ref, kmean, kaug_a, kaug_b, vaug_a, vaug_b, qaug,
                 m_s, acc_s, s_buf):
    last = pl.program_id(2) + q0 // 2
    nb = k_ref.shape[0] // MOBA_BLOCK
    nbp = kmean.shape[0]
    lane = lax.broadcasted_iota(jnp.int32, (1, LANES), 1)
    head_a = lane < HEAD_DIM

    @pl.when(pl.program_id(2) == 0)
    def _():
        kmean[...] = jnp.zeros_like(kmean)
        for j in range(nb):
            rows = pl.ds(j * MOBA_BLOCK, MOBA_BLOCK)
            kj = k_ref[rows, :].astype(F32)
            vj = v_ref[rows, :].astype(F32)
            kmean[j:j + 1, :] = jnp.sum(kj, axis=0, keepdims=True) * (1.0 / MOBA_BLOCK)
            kaug_a[rows, :] = jnp.where(head_a, kj, jnp.where(lane - HEAD_DIM == j, 1.0, 0.0)).astype(BF16)
            kaug_b[rows, :] = jnp.where(head_a, jnp.where(lane == j, 1.0, 0.0), kj).astype(BF16)
            vaug_a[rows, :] = jnp.where(head_a, vj, 1.0).astype(BF16)
            vaug_b[rows, :] = jnp.where(head_a, 1.0, vj).astype(BF16)
        blk_row = lax.broadcasted_iota(jnp.int32, (nbp, MOBA_BLOCK), 0)
        for t in range(q_ref.shape[0] // MOBA_BLOCK):
            qt = q0 + t
            qf = q_ref[t * MOBA_BLOCK:(t + 1) * MOBA_BLOCK, :].astype(F32)
            for hd, is_a in enumerate((True, False)):
                mine = head_a if is_a else jnp.logical_not(head_a)
                q_own = jnp.where(mine, qf, 0.0)
                g = _dot_nt(kmean[...], q_own, precision=HIGHEST)
                g = jnp.where(blk_row < qt, g, NEG)
                sel = jnp.zeros(g.shape, F32)
                for _ in range(MOBA_TOPK):
                    m = jnp.max(g, axis=0, keepdims=True)
                    idx = jnp.min(jnp.where(g == m, blk_row, nbp), axis=0, keepdims=True)
                    hit = blk_row == idx
                    sel = jnp.where(hit, jnp.where(idx < qt, 1.0, 0.0), sel)
                    g = jnp.where(hit, -jnp.inf, g)
                bias_t = jnp.where(sel > 0.0, 0.0, jnp.where(blk_row == qt, 0.0, NEG))
                bias_t = jnp.concatenate([bias_t, jnp.full((LANES - nbp, MOBA_BLOCK), NEG, F32)], axis=0)
                bias = jnp.transpose(bias_t)
                if is_a:
                    bias = pltpu.roll(bias, HEAD_DIM, 1)
                qaug[hd, t * MOBA_BLOCK:(t + 1) * MOBA_BLOCK, :] = jnp.where(mine, qf, bias).astype(BF16)

    tile_rows = pl.ds(pl.multiple_of(pl.program_id(2) * MOBA_PAIR, MOBA_PAIR), MOBA_PAIR)
    q_augs = [qaug[0, tile_rows, :], qaug[1, tile_rows, :]]

    m_s[...] = jnp.full(m_s.shape, -jnp.inf, F32)
    acc_s[...] = jnp.zeros_like(acc_s)
    qpos = last * MOBA_PAIR + lax.broadcasted_iota(jnp.int32, (MOBA_PAIR, MOBA_PAIR), 0)
    col = lax.broadcasted_iota(jnp.int32, (MOBA_PAIR, MOBA_PAIR), 1)

    def kv_rows(jj):
        return pl.ds(pl.multiple_of(jj * MOBA_PAIR, MOBA_PAIR), MOBA_PAIR)

    def scores(jj, slot):
        for hd, kaug in enumerate((kaug_a, kaug_b)):
            s_buf[slot, hd] = _dot_nt(q_augs[hd], kaug[kv_rows(jj), :])

    def softmax_pv(jj, slot, causal):
        for hd, vaug in enumerate((vaug_a, vaug_b)):
            s = s_buf[slot, hd]
            if causal:
                s = jnp.where(jj * MOBA_PAIR + col <= qpos, s, NEG)
            m_old = m_s[hd]
            m_new = jnp.maximum(m_old, jnp.max(s, axis=-1, keepdims=True))
            alpha = jnp.exp(m_old - m_new)
            p = jnp.exp(s - m_new)
            m_s[hd] = m_new
            acc_s[hd] = alpha * acc_s[hd] + _dot(p.astype(BF16), vaug[kv_rows(jj), :])

    scores(0, 0)

    def body(k, _):
        scores(2 * k + 1, 1)
        softmax_pv(2 * k, 0, False)
        scores(2 * k + 2, 0)
        softmax_pv(2 * k + 1, 1, False)
        return 0

    lax.fori_loop(0, last // 2, body, 0)

    @pl.when(last % 2 == 0)
    def _():
        softmax_pv(last, 0, True)

    @pl.when(last % 2 == 1)
    def _():
        scores(last, 1)
        softmax_pv(last - 1, 0, False)
        softmax_pv(last, 1, True)
    acc_a, acc_b = acc_s[0], acc_s[1]
    o_ref[...] = jnp.where(head_a, acc_a / pltpu.roll(acc_a, HEAD_DIM, 1),
                           acc_b / pltpu.roll(acc_b, HEAD_DIM, 1)).astype(BF16)


def _moba(q, k, v, q0):
    B = q.shape[0]
    nq = q.shape[1] // MOBA_BLOCK
    skv = (q0 + nq) * MOBA_BLOCK
    nb = skv // MOBA_BLOCK
    assert nb <= HEAD_DIM and nb % 2 == 0 and skv <= k.shape[1]
    nbp = -(-nb // SUBLANES) * SUBLANES
    assert q0 % 2 == 0 and nq % 2 == 0
    blk = pl.BlockSpec((None, MOBA_PAIR, LANES), lambda b, h, i: (b, i, h))
    seq = pl.BlockSpec((None, skv, LANES), lambda b, h, i: (b, 0, h))
    return pl.pallas_call(
        functools.partial(_moba_kernel, q0),
        grid=(B, D_ATT // LANES, nq // 2),
        in_specs=[pl.BlockSpec((None, nq * MOBA_BLOCK, LANES), lambda b, h, i: (b, 0, h)), seq, seq],
        out_specs=blk,
        out_shape=jax.ShapeDtypeStruct(q.shape, BF16),
        scratch_shapes=[pltpu.VMEM((nbp, LANES), F32),
                        pltpu.VMEM((skv, LANES), BF16), pltpu.VMEM((skv, LANES), BF16),
                        pltpu.VMEM((skv, LANES), BF16), pltpu.VMEM((skv, LANES), BF16),
                        pltpu.VMEM((2, nq * MOBA_BLOCK, LANES), BF16),
                        pltpu.VMEM((2, MOBA_PAIR, 1), F32),
                        pltpu.VMEM((2, MOBA_PAIR, LANES), F32),
                        pltpu.VMEM((2, 2, MOBA_PAIR, MOBA_PAIR), F32)],
        compiler_params=pltpu.CompilerParams(
            dimension_semantics=("parallel", "parallel", "arbitrary"), vmem_limit_bytes=VMEM_LIMIT),
        name="moba",
    )(q, k, v)


MERGE_TS = 256


def _bf16_bits(x):
    b = pltpu.bitcast(x, jnp.int32)
    r = b + 0x7FFF + (lax.shift_right_logical(b, 16) & 1)
    return lax.shift_right_logical(r, 16)


def _merge_kernel(x_ref, ys_ref, at_ref, ga_ref, gb_ref, wa_ref, wb_ref, wo_ref, g_ref,
                  wq_ref, k1_ref, k2_ref, after_ref, x1_ref, hw_ref, idx_ref, gate_ref, sc_ref):
    del after_ref
    ya = _dot(ys_ref[...], wa_ref[...])
    yb = _dot(at_ref[...], wb_ref[...])
    merged = ga_ref[...].astype(F32) * ya + gb_ref[...].astype(F32) * yb
    x1 = x_ref[...] + _dot(merged.astype(BF16), wo_ref[...])
    x1_ref[...] = x1
    hq = _rms(x1, g_ref[...])
    hw_ref[...] = _pack_words(hq)
    qp = _dot(hq.astype(BF16), wq_ref[...])
    for h in range(PEER_HEADS):
        o = h * PEER_QDIM
        sc_ref[2 * h] = _dot_nt(k1_ref[h], qp[:, o:o + PEER_HALF], precision=HIGHEST)
        sc_ref[2 * h + 1] = _dot_nt(k2_ref[h], qp[:, o + PEER_HALF:o + PEER_QDIM], precision=HIGHEST)
    _topk_kernel(sc_ref, idx_ref, gate_ref)


def _merge(x, ys, att, ga, gb, t0, off, nt, after, w_proj_ssm, w_proj_att, w_out, g_ffn, peer_w_q,
           keys1, keys2):
    B = ys.shape[0]
    ts = min(MERGE_TS, nt)
    nblk = nt // ts
    i0 = (t0 + off) // ts
    o0 = off // ts
    tok = lambda d: pl.BlockSpec((None, ts, d), lambda b, i: (b, o0 + i, 0))
    row = lambda d: pl.BlockSpec((ts, d), lambda b, i: (b * nblk + i, 0))
    full = lambda shape: pl.BlockSpec(shape, lambda b, i: (0,) * len(shape))
    qd = PEER_HEADS * PEER_QDIM
    return pl.pallas_call(
        _merge_kernel,
        grid=(B, nblk),
        in_specs=[pl.BlockSpec((None, ts, D_MODEL), lambda b, i: (b, i0 + i, 0)),
                  tok(D_SSM), tok(D_ATT), tok(D_MODEL), tok(D_MODEL),
                  full((D_SSM, D_MODEL)), full((D_ATT, D_MODEL)), full((D_MODEL, D_MODEL)),
                  full((1, D_MODEL)), full((D_MODEL, qd)),
                  full((PEER_HEADS, PEER_KEYS, PEER_HALF)), full((PEER_HEADS, PEER_KEYS, PEER_HALF)),
                  pl.BlockSpec(memory_space=pl.ANY)],
        out_specs=[row(D_MODEL), row(D_MODEL // 2), row(PEER_SEL), row(PEER_SEL)],
        out_shape=[jax.ShapeDtypeStruct((B * nt, D_MODEL), F32),
                   jax.ShapeDtypeStruct((B * nt, D_MODEL // 2), jnp.int32),
                   jax.ShapeDtypeStruct((B * nt, PEER_SEL), jnp.int32),
                   jax.ShapeDtypeStruct((B * nt, PEER_SEL), F32)],
        scratch_shapes=[pltpu.VMEM((2 * PEER_HEADS, PEER_KEYS, ts), F32)],
        compiler_params=pltpu.CompilerParams(
            dimension_semantics=("parallel", "parallel"), vmem_limit_bytes=VMEM_LIMIT),
        name="merge",
    )(x, ys, att, ga, gb, w_proj_ssm, w_proj_att, w_out, g_ffn, peer_w_q, keys1, keys2, after)


def _top_rows(s, row, k):
    vals, idxs = [], []
    for _ in range(k):
        m = jnp.max(s, axis=0, keepdims=True)
        idx = jnp.min(jnp.where(s == m, row, s.shape[0]), axis=0, keepdims=True)
        vals.append(m)
        idxs.append(idx)
        s = jnp.where(row == idx, -jnp.inf, s)
    return vals, idxs


def _stack_rows(rows, row16):
    acc = jnp.zeros(row16.shape, rows[0].dtype)
    for r, v in enumerate(rows):
        acc = jnp.where(row16 == r, v, acc)
    return acc


def _topk_kernel(sc_ref, idx_ref, gate_ref):
    ts = sc_ref.shape[-1]
    row = lax.broadcasted_iota(jnp.int32, (PEER_KEYS, ts), 0).astype(F32)
    row16 = lax.broadcasted_iota(jnp.int32, (PEER_TOPK, ts), 0)
    row8 = lax.broadcasted_iota(jnp.int32, (SUBLANES, ts), 0)
    counts = [PEER_TOPK // (i + 1) for i in range(PEER_TOPK)]
    heights = [PEER_TOPK if c > SUBLANES else SUBLANES for c in counts]
    n_cand = sum(heights)
    rowc = lax.broadcasted_iota(jnp.int32, (n_cand, ts), 0).astype(F32)
    gate_rows, eid_rows = [], []
    for h in range(PEER_HEADS):
        v1, i1 = _top_rows(sc_ref[2 * h], row, PEER_TOPK)
        v2, i2 = _top_rows(sc_ref[2 * h + 1], row, PEER_TOPK)
        v2s = _stack_rows(v2, row16)
        i2s = _stack_rows(i2, row16)
        cand, eid = [], []
        for i in range(PEER_TOPK):
            n = heights[i]
            cand.append(jnp.where((row16 if n == PEER_TOPK else row8) < counts[i],
                                  v1[i] + v2s[:n], -jnp.inf))
            eid.append(i1[i] * PEER_KEYS + i2s[:n])
        cand = jnp.concatenate(cand, axis=0)
        eid = jnp.concatenate(eid, axis=0)
        tops, picks = [], []
        for _ in range(PEER_TOPK):
            m = jnp.max(cand, axis=0, keepdims=True)
            pos = jnp.min(jnp.where(cand == m, rowc, n_cand), axis=0, keepdims=True)
            hit = rowc == pos
            picks.append(jnp.max(jnp.where(hit, eid, -1.0), axis=0, keepdims=True))
            tops.append(m)
            cand = jnp.where(hit, -jnp.inf, cand)
        top = _stack_rows(tops, row16)
        p = jnp.exp(top - jnp.max(top, axis=0, keepdims=True))
        gate_rows.append(p / jnp.sum(p, axis=0, keepdims=True))
        eid_rows.append(_stack_rows(picks, row16))
    gate_ref[...] = jnp.transpose(jnp.concatenate(gate_rows, axis=0))
    idx_ref[...] = jnp.transpose(jnp.concatenate(eid_rows, axis=0)).astype(jnp.int32)


SC_CORES = 2
SC_SUBCORES = 16
SC_LANES = 16
SC_WORKERS = SC_CORES * SC_SUBCORES
PEER_CH = SC_LANES
PEER_NCH = PEER_SEL // PEER_CH
PEER_WORDS = D_MODEL // 2
PEER_NWG = PEER_WORDS // SC_LANES
PEER_RING = 4
PEER_QUAD = 4
HI_MASK = -65536
GELU_C = 0.7978845608028654


def _gelu_tanh_via_exp(x):
    z = GELU_C * (x + 0.044715 * (x * x * x))
    t = 1.0 - 2.0 / (jnp.exp(2.0 * z) + 1.0)
    return 0.5 * x * (1.0 + t)


def _unpack_pair(w):
    lo = plsc.bitcast(lax.shift_left(w, 16), F32)
    hi = plsc.bitcast(lax.bitwise_and(w, HI_MASK), F32)
    return lo, hi


def _peer_sc_body(idx_hbm, gate_hbm, h_hbm, uv_hbm, after_hbm, o_hbm,
                  idx_v, gate_v, h_v, buf, out_v, gsem, msem, osem):
    n_tok = o_hbm.shape[0] // SC_WORKERS
    base = (lax.axis_index("s") * SC_CORES + lax.axis_index("c")) * n_tok
    lane = lax.iota(jnp.int32, SC_LANES)
    zero_rows = jnp.zeros((SC_LANES,), jnp.int32)

    def meta_copies(tok, s):
        return (pltpu.make_async_copy(idx_hbm.at[tok], idx_v.at[s], msem.at[s]),
                pltpu.make_async_copy(gate_hbm.at[tok], gate_v.at[s], msem.at[s]),
                pltpu.make_async_copy(h_hbm.at[tok], h_v.at[s], msem.at[s]))

    def gather(slot, rows):
        return pltpu.make_async_copy(uv_hbm.at[rows], buf.at[slot], gsem.at[slot])

    def token(t, carry):
        s = t % 2
        tok = base + t
        nxt = base + jnp.minimum(t + 1, n_tok - 1)
        for cp in meta_copies(nxt, 1 - s):
            cp.start()

        @pl.when(t >= 2)
        def _():
            pltpu.make_async_copy(out_v.at[s], o_hbm.at[tok], osem.at[s]).wait()

        def chunk(c, carry):
            slot = c % PEER_RING
            gather(slot, zero_rows).wait()

            def dot_step(q, accs):
                cols = [pl.ds(pl.multiple_of((q * PEER_QUAD + j) * SC_LANES, SC_LANES), SC_LANES)
                        for j in range(PEER_QUAD)]
                hs = [plsc.bitcast(h_v[s, col], BF16) for col in cols]
                out = []
                for r in range(PEER_CH):
                    p = plsc.bitcast(buf[slot, r, cols[0]], BF16) * hs[0]
                    for j in range(1, PEER_QUAD):
                        p = p + plsc.bitcast(buf[slot, r, cols[j]], BF16) * hs[j]
                    lo, hi = _unpack_pair(plsc.bitcast(p, jnp.int32))
                    out.append(accs[r] + lo + hi)
                return tuple(out)

            accs = lax.fori_loop(0, PEER_NWG // PEER_QUAD, dot_step,
                                 tuple(jnp.zeros((SC_LANES,), F32) for _ in range(PEER_CH)))
            tot = jnp.zeros((SC_LANES,), F32)
            for r in range(PEER_CH):
                tot = jnp.where(lane == r, jnp.sum(accs[r]), tot)
            rows = pl.ds(pl.multiple_of(c * PEER_CH, PEER_CH), PEER_CH)
            wvec = gate_v[s, rows] * _gelu_tanh_via_exp(tot)
            ws = []
            for r in range(PEER_CH):
                w = wvec.at[jnp.full((SC_LANES,), r, jnp.int32)].get(mode="promise_in_bounds")
                ws.append(plsc.pack(w, w, format=plsc.PackFormat.INTERLEAVED,
                                    preferred_element_type=BF16))
            first = c == 0

            @plsc.parallel_loop(0, PEER_NWG, unroll=2)
            def acc_step(g):
                col = pl.ds(pl.multiple_of(g * SC_LANES, SC_LANES), SC_LANES)
                col_v = pl.ds(pl.multiple_of(PEER_WORDS + g * SC_LANES, SC_LANES), SC_LANES)
                o_lo = jnp.where(first, 0.0, out_v[s, col])
                o_hi = jnp.where(first, 0.0, out_v[s, col_v])
                for r0 in range(0, PEER_CH, PEER_QUAD):
                    p = plsc.bitcast(buf[slot, r0, col_v], BF16) * ws[r0]
                    for r in range(r0 + 1, r0 + PEER_QUAD):
                        p = p + plsc.bitcast(buf[slot, r, col_v], BF16) * ws[r]
                    lo, hi = _unpack_pair(plsc.bitcast(p, jnp.int32))
                    o_lo = o_lo + lo
                    o_hi = o_hi + hi
                out_v[s, col] = o_lo
                out_v[s, col_v] = o_hi

            @pl.when(c == PEER_NCH - PEER_RING)
            def _():
                for cp in meta_copies(nxt, 1 - s):
                    cp.wait()

            ahead = c + PEER_RING
            src = jnp.where(ahead < PEER_NCH, s, 1 - s)
            nrows = idx_v[src, pl.ds(pl.multiple_of((ahead % PEER_NCH) * PEER_CH, PEER_CH), PEER_CH)]
            gather(slot, nrows).start()
            return carry

        lax.fori_loop(0, PEER_NCH, chunk, 0)
        pltpu.make_async_copy(out_v.at[s], o_hbm.at[tok], osem.at[s]).start()
        return carry

    for cp in meta_copies(base, 0):
        cp.start()
    for cp in meta_copies(base, 0):
        cp.wait()
    for c in range(PEER_RING):
        gather(c, idx_v[0, pl.ds(c * PEER_CH, PEER_CH)]).start()
    lax.fori_loop(0, n_tok, token, 0)
    for c in range(PEER_RING):
        gather(c, zero_rows).wait()
    for s in range(2):
        pltpu.make_async_copy(out_v.at[s], o_hbm.at[base], osem.at[s]).wait()


PACK_ROWS = 256


def _pack_words(x):
    half = x.shape[1] // 2
    return _bf16_bits(x[:, :half]) | lax.shift_left(_bf16_bits(x[:, half:]), 16)


def _pack_tables_kernel(u_ref, v_ref, o_ref):
    o_ref[:, :PEER_WORDS] = _pack_words(u_ref[...])
    o_ref[:, PEER_WORDS:] = _pack_words(v_ref[...])


def _pack_tables(peer_u, peer_v):
    n = peer_u.shape[0]
    rows = min(PACK_ROWS, n)
    spec = pl.BlockSpec((rows, D_MODEL), lambda i: (i, 0))
    return pl.pallas_call(
        _pack_tables_kernel,
        grid=(n // rows,),
        in_specs=[spec, spec],
        out_specs=spec,
        out_shape=jax.ShapeDtypeStruct((n, D_MODEL), jnp.int32),
        compiler_params=pltpu.CompilerParams(
            dimension_semantics=("parallel",), vmem_limit_bytes=VMEM_LIMIT),
        name="pack_tables",
    )(peer_u, peer_v)


def _peer(idx, h_words, gates, uv_words, after):
    T = h_words.shape[0]
    assert T % (2 * SC_WORKERS) == 0
    mesh = plsc.VectorSubcoreMesh(core_axis_name="c", subcore_axis_name="s",
                                  num_cores=SC_CORES, num_subcores=SC_SUBCORES)
    return pl.kernel(
        _peer_sc_body,
        out_type=jax.ShapeDtypeStruct((T, D_MODEL), F32),
        mesh=mesh,
        scratch_types=[
            pltpu.VMEM((2, PEER_SEL), jnp.int32), pltpu.VMEM((2, PEER_SEL), F32),
            pltpu.VMEM((2, PEER_WORDS), jnp.int32),
            pltpu.VMEM((PEER_RING, PEER_CH, 2 * PEER_WORDS), jnp.int32),
            pltpu.VMEM((2, D_MODEL), F32),
            pltpu.SemaphoreType.DMA((PEER_RING,)),
            pltpu.SemaphoreType.DMA((2,)), pltpu.SemaphoreType.DMA((2,)),
        ],
        compiler_params=pltpu.CompilerParams(needs_layout_passes=False),
        name="peer_sc",
    )(idx, gates, h_words, uv_words, after)


FINAL_TS = 256


def _final_kernel(x1_ref, pe_ref, p_ref, gp_ref, wg_ref, wp_ref, gf_ref, o_ref):
    x2 = x1_ref[...] + pe_ref[...]
    e = _dot(p_ref[...].astype(BF16), wp_ref[...])
    gate = jax.nn.sigmoid(_dot(_rms(x2, gp_ref[...]).astype(BF16), wg_ref[...]))
    o_ref[...] = _rms(x2 + gate * e, gf_ref[...])


def _final(x1, peer_out, p, t0, nt, g_ple, ple_w_gate, ple_w_proj, g_final):
    B = p.shape[0]
    ts = min(FINAL_TS, nt)
    nblk = nt // ts
    i0 = t0 // ts
    row = lambda d: pl.BlockSpec((ts, d), lambda b, i: (b * nblk + i, 0))
    full = lambda shape: pl.BlockSpec(shape, lambda b, i: (0,) * len(shape))
    return pl.pallas_call(
        _final_kernel,
        grid=(B, nblk),
        in_specs=[row(D_MODEL), row(D_MODEL),
                  pl.BlockSpec((None, ts, D_PLE), lambda b, i: (b, i0 + i, 0)),
                  full((1, D_MODEL)), full((D_MODEL, D_MODEL)), full((D_PLE, D_MODEL)),
                  full((1, D_MODEL))],
        out_specs=pl.BlockSpec((None, ts, D_MODEL), lambda b, i: (b, i, 0)),
        out_shape=jax.ShapeDtypeStruct((B, nt, D_MODEL), F32),
        compiler_params=pltpu.CompilerParams(
            dimension_semantics=("parallel", "parallel"), vmem_limit_bytes=VMEM_LIMIT),
        name="final",
    )(x1, peer_out, p, g_ple, ple_w_gate, ple_w_proj, g_final)


CHUNK_STEPS = (512, 512, 1024, 1024, 1024, 1024, 1024, 1024, 512, 512)


def kernel(x, p, positions, g_mix, w_in, ssm_log_dt, ssm_a_re, ssm_a_im, ssm_b_re, ssm_b_im,
           ssm_c_re, ssm_c_im, ssm_d, ssm_w_glu, w_proj_ssm, w_proj_att, w_out, g_ffn,
           peer_w_q, peer_keys1, peer_keys2, peer_u, peer_v, g_ple, ple_w_gate, ple_w_proj,
           g_final):
    B, S, _ = x.shape
    assert w_in.shape[0] == 1, "the final rmsnorm is fused into the single layer's last stage"
    steps = CHUNK_STEPS if sum(CHUNK_STEPS) == S else (S,)
    i = 0
    tables = _s5_tables(ssm_log_dt[i], ssm_a_re[i], ssm_a_im[i], ssm_b_re[i], ssm_b_im[i],
                        ssm_c_re[i], ssm_c_im[i])
    w_in_b, w_glu_b = w_in[i].astype(BF16), ssm_w_glu[i].astype(BF16)
    d_skip = ssm_d[i].reshape(1, D_SSM).astype(F32)
    merge_w = (w_proj_ssm[i].astype(BF16), w_proj_att[i].astype(BF16), w_out[i].astype(BF16),
               g_ffn[i].reshape(1, D_MODEL), peer_w_q[i].astype(BF16), peer_keys1[i], peer_keys2[i])
    final_w = (g_ple[i].reshape(1, D_MODEL), ple_w_gate[i].astype(BF16),
               ple_w_proj[i].astype(BF16), g_final.reshape(1, D_MODEL))
    uv_words = _pack_tables(peer_u[i], peer_v[i])
    k_all = jnp.zeros((B, S, D_ATT), BF16)
    v_all = jnp.zeros((B, S, D_ATT), BF16)
    carry = jnp.zeros((2, SUBLANES, D_STATE), F32)
    outs = []
    t0 = 0
    after = (carry, carry)
    peers = []
    for nt in steps:
        u_sb, q, k, v, ga, gb = _in_proj(x, positions, g_mix[i], w_in_b, t0, nt, after)
        k_all = lax.dynamic_update_slice(k_all, k, (0, t0, 0))
        v_all = lax.dynamic_update_slice(v_all, v, (0, t0, 0))
        ys, carry = _s5(u_sb, carry, tables, d_skip, w_glu_b, B)
        att = _moba(q, k_all, v_all, t0 // MOBA_BLOCK)
        subs = ((0, nt // 2), (nt // 2, nt // 2)) if t0 == 0 and nt >= 2 * MERGE_TS else ((0, nt),)
        for off, n in subs:
            x1, h_words, idx, gates = _merge(x, ys, att, ga, gb, t0, off, n,
                                             peers[-2] if len(peers) > 1 else carry, *merge_w)
            peer_out = _peer(idx, h_words, gates, uv_words,
                             peers[-1] if t0 + off + n == S else carry)
            peers.append(peer_out)
            outs.append(_final(x1, peer_out, p[i], t0 + off, n, *final_w))
        after = (gates, outs[-3] if len(outs) > 2 else carry)
        t0 += nt
    return jnp.concatenate(outs, axis=1)
```

```python
import functools

import jax
import jax.numpy as jnp
from jax import lax
from jax.experimental import pallas as pl
from jax.experimental.pallas import tpu as pltpu
from jax.experimental.pallas import tpu_sc as plsc

F32 = jnp.float32
BF16 = jnp.bfloat16

D_MODEL = 1024
D_SSM = 512
SSM_GROUPS = 32
SSM_STATE = 64
D_STATE = SSM_GROUPS * SSM_STATE
HEAD_DIM = 64
D_ATT = 512
ROT_DIM = 16
ROPE_THETA = 500000.0
MOBA_BLOCK = 256
MOBA_TOPK = 3
PEER_HEADS = 8
PEER_KEYS = 128
PEER_QDIM = 256
PEER_HALF = 128
PEER_TOPK = 16
PEER_SEL = PEER_HEADS * PEER_TOPK
D_PLE = 256
EPS = 1e-6
NEG = -1e30
LANES = 128
SUBLANES = 8
VMEM_LIMIT = 48 * 1024 * 1024
HIGHEST = lax.Precision.HIGHEST


def _rms(x, g):
    return x * lax.rsqrt(jnp.mean(x * x, axis=-1, keepdims=True) + EPS) * g


def _dot(a, b):
    return jnp.dot(a, b, preferred_element_type=F32)


def _dot_nt(a, b, precision=None):
    return lax.dot_general(a, b, (((1,), (1,)), ((), ())), precision=precision,
                           preferred_element_type=F32)


IN_TS = 512


def _in_proj_kernel(x_ref, pos_ref, g_ref, w_ref, invf_ref, after_a, after_b,
                    u_ref, q_ref, k_ref, v_ref, ga_ref, gb_ref):
    del after_a, after_b
    h = _rms(x_ref[...], g_ref[...]).astype(BF16)

    def proj(lo, hi):
        return _dot(h, w_ref[:, lo:hi])

    u_ref[...] = proj(0, D_SSM).astype(BF16)
    ang = pos_ref[...].astype(F32) * invf_ref[...]
    cos = jnp.cos(ang)
    sin = jnp.sin(ang)
    lane = lax.broadcasted_iota(jnp.int32, (1, LANES), 1) % HEAD_DIM
    half = ROT_DIM // 2
    sin_hi = jnp.where((lane >= half) & (lane < ROT_DIM), sin, 0.0)
    sin_lo = jnp.where(lane < half, -sin, 0.0)
    reps = D_ATT // LANES
    cos4 = jnp.concatenate([cos] * reps, axis=1)
    sin_hi4 = jnp.concatenate([sin_hi] * reps, axis=1)
    sin_lo4 = jnp.concatenate([sin_lo] * reps, axis=1)

    def rope(t):
        return (t * cos4 + pltpu.roll(t, half, 1) * sin_hi4
                + pltpu.roll(t, D_ATT - half, 1) * sin_lo4)

    q = rope(proj(D_SSM, D_SSM + D_ATT))
    q_ref[...] = (q * (HEAD_DIM ** -0.5)).astype(BF16)
    k_ref[...] = rope(proj(D_SSM + D_ATT, D_SSM + 2 * D_ATT)).astype(BF16)
    v_ref[...] = proj(D_SSM + 2 * D_ATT, D_SSM + 3 * D_ATT).astype(BF16)
    o = D_SSM + 3 * D_ATT
    ga_ref[...] = jax.nn.sigmoid(proj(o, o + D_MODEL)).astype(BF16)
    gb_ref[...] = jax.nn.sigmoid(proj(o + D_MODEL, o + 2 * D_MODEL)).astype(BF16)


def _in_proj(x, positions, g_mix, w_in, t0, nt, after):
    B, S, _ = x.shape
    ts = min(IN_TS, nt)
    assert nt % ts == 0 and t0 % ts == 0
    i0 = t0 // ts
    inv_freq = ROPE_THETA ** (-jnp.arange(0, ROT_DIM, 2, dtype=F32) / ROT_DIM)
    lane = jnp.arange(LANES) % HEAD_DIM
    invf = jnp.where(lane < ROT_DIM, inv_freq[lane % (ROT_DIM // 2)], 0.0).reshape(1, LANES)
    d_in = w_in.shape[1]
    src = lambda d: pl.BlockSpec((None, ts, d), lambda b, i: (b, i0 + i, 0))
    tok = lambda d: pl.BlockSpec((None, ts, d), lambda b, i: (b, i, 0))
    full = lambda shape: pl.BlockSpec(shape, lambda b, i: (0,) * len(shape))
    return pl.pallas_call(
        _in_proj_kernel,
        grid=(B, nt // ts),
        in_specs=[src(D_MODEL), src(1), full((1, D_MODEL)), full((D_MODEL, d_in)), full((1, LANES)),
                  pl.BlockSpec(memory_space=pl.ANY), pl.BlockSpec(memory_space=pl.ANY)],
        out_specs=[pl.BlockSpec((ts, D_SSM), lambda b, i: (i, b)),
                   tok(D_ATT), tok(D_ATT), tok(D_ATT), tok(D_MODEL), tok(D_MODEL)],
        out_shape=[jax.ShapeDtypeStruct((nt, B * D_SSM), BF16),
                   jax.ShapeDtypeStruct((B, nt, D_ATT), BF16),
                   jax.ShapeDtypeStruct((B, nt, D_ATT), BF16),
                   jax.ShapeDtypeStruct((B, nt, D_ATT), BF16),
                   jax.ShapeDtypeStruct((B, nt, D_MODEL), BF16),
                   jax.ShapeDtypeStruct((B, nt, D_MODEL), BF16)],
        compiler_params=pltpu.CompilerParams(
            dimension_semantics=("parallel", "parallel"), vmem_limit_bytes=VMEM_LIMIT),
        name="in_proj",
    )(x, positions.reshape(B, S, 1), g_mix.reshape(1, D_MODEL), w_in, invf, *after)


S5_TS = 128
S5_BATCH = 4
S5_COLS = 512


def _s5_kernel(u_ref, c0_ref, bre_ref, bim_ref, a1r_ref, a1i_ref, pr_ref, pi_ref,
               cre_ref, cim_ref, d_ref, wglu_ref, y_ref, c1_ref,
               xr, xi, cr, ci, ysc):
    rows = xr.shape[0]
    ts = rows // S5_BATCH

    @pl.when(pl.program_id(0) == 0)
    def _():
        cr[...] = c0_ref[0]
        ci[...] = c0_ref[1]

    u = u_ref[...]
    for cb in range(D_STATE // S5_COLS):
        sl = slice(cb * S5_COLS, (cb + 1) * S5_COLS)
        u_cb = u[:, cb * LANES:(cb + 1) * LANES]
        xr[:, sl] = _dot(u_cb, bre_ref[cb])
        xi[:, sl] = _dot(u_cb, bim_ref[cb])

    hi_rows = lax.broadcasted_iota(jnp.int32, (SUBLANES, S5_COLS), 0) >= S5_BATCH
    for cb in range(D_STATE // S5_COLS):
        sl = slice(cb * S5_COLS, (cb + 1) * S5_COLS)
        a_r, a_i = a1r_ref[:, sl], a1i_ref[:, sl]
        p_r, p_i = pr_ref[:, sl], pi_ref[:, sl]

        def body(t, carry):
            c_r, c_i = carry
            r0 = pl.multiple_of(t * SUBLANES, SUBLANES)
            x_r = xr[pl.ds(r0, SUBLANES), sl]
            x_i = xi[pl.ds(r0, SUBLANES), sl]
            s_r = pltpu.roll(x_r, S5_BATCH, 0)
            s_i = pltpu.roll(x_i, S5_BATCH, 0)
            h_r = x_r + (a_r * s_r - a_i * s_i) + (p_r * c_r - p_i * c_i)
            h_i = x_i + (a_r * s_i + a_i * s_r) + (p_r * c_i + p_i * c_r)
            xr[pl.ds(r0, SUBLANES), sl] = h_r
            xi[pl.ds(r0, SUBLANES), sl] = h_i
            n_r = jnp.where(hi_rows, h_r, pltpu.roll(h_r, S5_BATCH, 0))
            n_i = jnp.where(hi_rows, h_i, pltpu.roll(h_i, S5_BATCH, 0))
            return n_r, n_i

        c_r, c_i = lax.fori_loop(0, rows // SUBLANES, body, (cr[:, sl], ci[:, sl]), unroll=2)
        cr[:, sl] = c_r
        ci[:, sl] = c_i

    y = jnp.concatenate(
        [_dot(xr[:, cb * S5_COLS:(cb + 1) * S5_COLS].astype(BF16), cre_ref[cb])
         - _dot(xi[:, cb * S5_COLS:(cb + 1) * S5_COLS].astype(BF16), cim_ref[cb])
         for cb in range(D_STATE // S5_COLS)], axis=1) + d_ref[...] * u.astype(F32)
    y = jax.nn.gelu(y)
    y = y * jax.nn.sigmoid(_dot(y.astype(BF16), wglu_ref[...]))
    for c in range(D_SSM // LANES):
        ysc[c] = y[:, c * LANES:(c + 1) * LANES]
    for b in range(S5_BATCH):
        for c in range(D_SSM // LANES):
            y_ref[b, :, c * LANES:(c + 1) * LANES] = (
                ysc[c, pl.ds(b, ts, stride=S5_BATCH), :].astype(BF16))

    @pl.when(pl.program_id(0) == pl.num_programs(0) - 1)
    def _():
        c1_ref[0] = cr[...]
        c1_ref[1] = ci[...]


def _s5_tables(log_dt, a_re, a_im, b_re, b_im, c_re, c_im):
    dt = jnp.exp(log_dt.astype(F32))[:, None]
    ar, ai = a_re.astype(F32), a_im.astype(F32)
    mag = jnp.exp(dt * ar)
    abar_re, abar_im = mag * jnp.cos(dt * ai), mag * jnp.sin(dt * ai)
    den = ar * ar + ai * ai
    nr, ni = abar_re - 1.0, abar_im
    f_re = (nr * ar + ni * ai) / den
    f_im = (ni * ar - nr * ai) / den
    br, bi = b_re.astype(F32), b_im.astype(F32)
    bb_re = f_re[..., None] * br - f_im[..., None] * bi
    bb_im = f_re[..., None] * bi + f_im[..., None] * br
    eye = jnp.eye(SSM_GROUPS, dtype=F32)

    def in_blockdiag(bb):
        return jnp.einsum('gnc,gh->gchn', bb, eye).reshape(D_SSM, D_STATE)

    def out_blockdiag(c):
        return jnp.einsum('gcn,gh->gnhc', c.astype(F32), eye).reshape(D_STATE, D_SSM)

    a_r = abar_re.reshape(1, D_STATE)
    a_i = abar_im.reshape(1, D_STATE)
    a2_r = a_r * a_r - a_i * a_i
    a2_i = 2.0 * a_r * a_i
    hi = (jnp.arange(SUBLANES) >= S5_BATCH)[:, None]
    a1r = jnp.where(hi, a_r, 0.0)
    a1i = jnp.where(hi, a_i, 0.0)
    p_r = jnp.where(hi, a2_r, a_r)
    p_i = jnp.where(hi, a2_i, a_i)
    nblk = D_STATE // S5_COLS
    cw = D_SSM // nblk

    def in_blocks(m):
        return jnp.stack([m[b * cw:(b + 1) * cw, b * S5_COLS:(b + 1) * S5_COLS] for b in range(nblk)])

    def out_blocks(m):
        return jnp.stack([m[b * S5_COLS:(b + 1) * S5_COLS, b * cw:(b + 1) * cw] for b in range(nblk)])

    return (in_blocks(in_blockdiag(bb_re)).astype(BF16), in_blocks(in_blockdiag(bb_im)).astype(BF16),
            a1r, a1i, p_r, p_i,
            out_blocks(out_blockdiag(c_re)).astype(BF16), out_blocks(out_blockdiag(c_im)).astype(BF16))


def _s5(u_sb, carry, tables, d_skip, w_glu, B):
    assert B == S5_BATCH
    nt = u_sb.shape[0]
    ts = min(S5_TS, nt)
    rows = ts * B
    bre, bim, a1r, a1i, p_r, p_i, cre, cim = tables
    full = lambda shape: pl.BlockSpec(shape, lambda i: (0,) * len(shape))
    return pl.pallas_call(
        _s5_kernel,
        grid=(nt // ts,),
        in_specs=[pl.BlockSpec((rows, D_SSM), lambda i: (i, 0)),
                  full((2, SUBLANES, D_STATE)),
                  full(bre.shape), full(bim.shape),
                  full((SUBLANES, D_STATE)), full((SUBLANES, D_STATE)),
                  full((SUBLANES, D_STATE)), full((SUBLANES, D_STATE)),
                  full(cre.shape), full(cim.shape),
                  full((1, D_SSM)), full((D_SSM, D_SSM))],
        out_specs=[pl.BlockSpec((B, ts, D_SSM), lambda i: (0, i, 0)),
                   full((2, SUBLANES, D_STATE))],
        out_shape=[jax.ShapeDtypeStruct((B, nt, D_SSM), BF16),
                   jax.ShapeDtypeStruct((2, SUBLANES, D_STATE), F32)],
        scratch_shapes=[pltpu.VMEM((rows, D_STATE), F32), pltpu.VMEM((rows, D_STATE), F32),
                        pltpu.VMEM((SUBLANES, D_STATE), F32), pltpu.VMEM((SUBLANES, D_STATE), F32),
                        pltpu.VMEM((D_SSM // LANES, rows, LANES), F32)],
        compiler_params=pltpu.CompilerParams(
            dimension_semantics=("arbitrary",), vmem_limit_bytes=VMEM_LIMIT),
        name="s5",
    )(u_sb.reshape(nt * B, D_SSM), carry, bre, bim, a1r, a1i, p_r, p_i, cre, cim, d_skip, w_glu)


MOBA_PAIR = 2 * MOBA_BLOCK


def _moba_kernel(q0, q_ref, k_ref, v_ref, o_ref, kmean, kaug_a, kaug_b, vaug_a, vaug_b, qaug,
                 m_s, acc_s, s_buf):
    last = pl.program_id(2) + q0 // 2
    nb = k_ref.shape[0] // MOBA_BLOCK
    nbp = kmean.shape[0]
    lane = lax.broadcasted_iota(jnp.int32, (1, LANES), 1)
    head_a = lane < HEAD_DIM

    @pl.when(pl.program_id(2) == 0)
    def _():
        kmean[...] = jnp.zeros_like(kmean)
        for j in range(nb):
            rows = pl.ds(j * MOBA_BLOCK, MOBA_BLOCK)
            kj = k_ref[rows, :].astype(F32)
            vj = v_ref[rows, :].astype(F32)
            kmean[j:j + 1, :] = jnp.sum(kj, axis=0, keepdims=True) * (1.0 / MOBA_BLOCK)
            kaug_a[rows, :] = jnp.where(head_a, kj, jnp.where(lane - HEAD_DIM == j, 1.0, 0.0)).astype(BF16)
            kaug_b[rows, :] = jnp.where(head_a, jnp.where(lane == j, 1.0, 0.0), kj).astype(BF16)
            vaug_a[rows, :] = jnp.where(head_a, vj, 1.0).astype(BF16)
            vaug_b[rows, :] = jnp.where(head_a, 1.0, vj).astype(BF16)
        blk_row = lax.broadcasted_iota(jnp.int32, (nbp, MOBA_BLOCK), 0)
        for t in range(q_ref.shape[0] // MOBA_BLOCK):
            qt = q0 + t
            qf = q_ref[t * MOBA_BLOCK:(t + 1) * MOBA_BLOCK, :].astype(F32)
            for hd, is_a in enumerate((True, False)):
                mine = head_a if is_a else jnp.logical_not(head_a)
                q_own = jnp.where(mine, qf, 0.0)
                g = _dot_nt(kmean[...], q_own, precision=HIGHEST)
                g = jnp.where(blk_row < qt, g, NEG)
                sel = jnp.zeros(g.shape, F32)
                for _ in range(MOBA_TOPK):
                    m = jnp.max(g, axis=0, keepdims=True)
                    idx = jnp.min(jnp.where(g == m, blk_row, nbp), axis=0, keepdims=True)
                    hit = blk_row == idx
                    sel = jnp.where(hit, jnp.where(idx < qt, 1.0, 0.0), sel)
                    g = jnp.where(hit, -jnp.inf, g)
                bias_t = jnp.where(sel > 0.0, 0.0, jnp.where(blk_row == qt, 0.0, NEG))
                bias_t = jnp.concatenate([bias_t, jnp.full((LANES - nbp, MOBA_BLOCK), NEG, F32)], axis=0)
                bias = jnp.transpose(bias_t)
                if is_a:
                    bias = pltpu.roll(bias, HEAD_DIM, 1)
                qaug[hd, t * MOBA_BLOCK:(t + 1) * MOBA_BLOCK, :] = jnp.where(mine, qf, bias).astype(BF16)

    tile_rows = pl.ds(pl.multiple_of(pl.program_id(2) * MOBA_PAIR, MOBA_PAIR), MOBA_PAIR)
    q_augs = [qaug[0, tile_rows, :], qaug[1, tile_rows, :]]

    m_s[...] = jnp.full(m_s.shape, -jnp.inf, F32)
    acc_s[...] = jnp.zeros_like(acc_s)
    qpos = last * MOBA_PAIR + lax.broadcasted_iota(jnp.int32, (MOBA_PAIR, MOBA_PAIR), 0)
    col = lax.broadcasted_iota(jnp.int32, (MOBA_PAIR, MOBA_PAIR), 1)

    def kv_rows(jj):
        return pl.ds(pl.multiple_of(jj * MOBA_PAIR, MOBA_PAIR), MOBA_PAIR)

    def scores(jj, slot):
        for hd, kaug in enumerate((kaug_a, kaug_b)):
            s_buf[slot, hd] = _dot_nt(q_augs[hd], kaug[kv_rows(jj), :])

    def softmax_pv(jj, slot, causal):
        for hd, vaug in enumerate((vaug_a, vaug_b)):
            s = s_buf[slot, hd]
            if causal:
                s = jnp.where(jj * MOBA_PAIR + col <= qpos, s, NEG)
            m_old = m_s[hd]
            m_new = jnp.maximum(m_old, jnp.max(s, axis=-1, keepdims=True))
            alpha = jnp.exp(m_old - m_new)
            p = jnp.exp(s - m_new)
            m_s[hd] = m_new
            acc_s[hd] = alpha * acc_s[hd] + _dot(p.astype(BF16), vaug[kv_rows(jj), :])

    scores(0, 0)

    def body(k, _):
        scores(2 * k + 1, 1)
        softmax_pv(2 * k, 0, False)
        scores(2 * k + 2, 0)
        softmax_pv(2 * k + 1, 1, False)
        return 0

    lax.fori_loop(0, last // 2, body, 0)

    @pl.when(last % 2 == 0)
    def _():
        softmax_pv(last, 0, True)

    @pl.when(last % 2 == 1)
    def _():
        scores(last, 1)
        softmax_pv(last - 1, 0, False)
        softmax_pv(last, 1, True)
    acc_a, acc_b = acc_s[0], acc_s[1]
    o_ref[...] = jnp.where(head_a, acc_a / pltpu.roll(acc_a, HEAD_DIM, 1),
                           acc_b / pltpu.roll(acc_b, HEAD_DIM, 1)).astype(BF16)


def _moba(q, k, v, q0):
    B = q.shape[0]
    nq = q.shape[1] // MOBA_BLOCK
    skv = (q0 + nq) * MOBA_BLOCK
    nb = skv // MOBA_BLOCK
    assert nb <= HEAD_DIM and nb % 2 == 0 and skv <= k.shape[1]
    nbp = -(-nb // SUBLANES) * SUBLANES
    assert q0 % 2 == 0 and nq % 2 == 0
    blk = pl.BlockSpec((None, MOBA_PAIR, LANES), lambda b, h, i: (b, i, h))
    seq = pl.BlockSpec((None, skv, LANES), lambda b, h, i: (b, 0, h))
    return pl.pallas_call(
        functools.partial(_moba_kernel, q0),
        grid=(B, D_ATT // LANES, nq // 2),
        in_specs=[pl.BlockSpec((None, nq * MOBA_BLOCK, LANES), lambda b, h, i: (b, 0, h)), seq, seq],
        out_specs=blk,
        out_shape=jax.ShapeDtypeStruct(q.shape, BF16),
        scratch_shapes=[pltpu.VMEM((nbp, LANES), F32),
                        pltpu.VMEM((skv, LANES), BF16), pltpu.VMEM((skv, LANES), BF16),
                        pltpu.VMEM((skv, LANES), BF16), pltpu.VMEM((skv, LANES), BF16),
                        pltpu.VMEM((2, nq * MOBA_BLOCK, LANES), BF16),
                        pltpu.VMEM((2, MOBA_PAIR, 1), F32),
                        pltpu.VMEM((2, MOBA_PAIR, LANES), F32),
                        pltpu.VMEM((2, 2, MOBA_PAIR, MOBA_PAIR), F32)],
        compiler_params=pltpu.CompilerParams(
            dimension_semantics=("parallel", "parallel", "arbitrary"), vmem_limit_bytes=VMEM_LIMIT),
        name="moba",
    )(q, k, v)


MERGE_TS = 256


def _bf16_bits(x):
    b = pltpu.bitcast(x, jnp.int32)
    r = b + 0x7FFF + (lax.shift_right_logical(b, 16) & 1)
    return lax.shift_right_logical(r, 16)


def _merge_kernel(x_ref, ys_ref, at_ref, ga_ref, gb_ref, wa_ref, wb_ref, wo_ref, g_ref,
                  wq_ref, k1_ref, k2_ref, after_ref, x1_ref, hw_ref, idx_ref, gate_ref, sc_ref):
    del after_ref
    ya = _dot(ys_ref[...], wa_ref[...])
    yb = _dot(at_ref[...], wb_ref[...])
    merged = ga_ref[...].astype(F32) * ya + gb_ref[...].astype(F32) * yb
    x1 = x_ref[...] + _dot(merged.astype(BF16), wo_ref[...])
    x1_ref[...] = x1
    hq = _rms(x1, g_ref[...])
    hw_ref[...] = _pack_words(hq)
    qp = _dot(hq.astype(BF16), wq_ref[...])
    for h in range(PEER_HEADS):
        o = h * PEER_QDIM
        sc_ref[2 * h] = _dot_nt(k1_ref[h], qp[:, o:o + PEER_HALF], precision=HIGHEST)
        sc_ref[2 * h + 1] = _dot_nt(k2_ref[h], qp[:, o + PEER_HALF:o + PEER_QDIM], precision=HIGHEST)
    _topk_kernel(sc_ref, idx_ref, gate_ref)


def _merge(x, ys, att, ga, gb, t0, off, nt, after, w_proj_ssm, w_proj_att, w_out, g_ffn, peer_w_q,
           keys1, keys2):
    B = ys.shape[0]
    ts = min(MERGE_TS, nt)
    nblk = nt // ts
    i0 = (t0 + off) // ts
    o0 = off // ts
    tok = lambda d: pl.BlockSpec((None, ts, d), lambda b, i: (b, o0 + i, 0))
    row = lambda d: pl.BlockSpec((ts, d), lambda b, i: (b * nblk + i, 0))
    full = lambda shape: pl.BlockSpec(shape, lambda b, i: (0,) * len(shape))
    qd = PEER_HEADS * PEER_QDIM
    return pl.pallas_call(
        _merge_kernel,
        grid=(B, nblk),
        in_specs=[pl.BlockSpec((None, ts, D_MODEL), lambda b, i: (b, i0 + i, 0)),
                  tok(D_SSM), tok(D_ATT), tok(D_MODEL), tok(D_MODEL),
                  full((D_SSM, D_MODEL)), full((D_ATT, D_MODEL)), full((D_MODEL, D_MODEL)),
                  full((1, D_MODEL)), full((D_MODEL, qd)),
                  full((PEER_HEADS, PEER_KEYS, PEER_HALF)), full((PEER_HEADS, PEER_KEYS, PEER_HALF)),
                  pl.BlockSpec(memory_space=pl.ANY)],
        out_specs=[row(D_MODEL), row(D_MODEL // 2), row(PEER_SEL), row(PEER_SEL)],
        out_shape=[jax.ShapeDtypeStruct((B * nt, D_MODEL), F32),
                   jax.ShapeDtypeStruct((B * nt, D_MODEL // 2), jnp.int32),
                   jax.ShapeDtypeStruct((B * nt, PEER_SEL), jnp.int32),
                   jax.ShapeDtypeStruct((B * nt, PEER_SEL), F32)],
        scratch_shapes=[pltpu.VMEM((2 * PEER_HEADS, PEER_KEYS, ts), F32)],
        compiler_params=pltpu.CompilerParams(
            dimension_semantics=("parallel", "parallel"), vmem_limit_bytes=VMEM_LIMIT),
        name="merge",
    )(x, ys, att, ga, gb, w_proj_ssm, w_proj_att, w_out, g_ffn, peer_w_q, keys1, keys2, after)


def _top_rows(s, row, k):
    vals, idxs = [], []
    for _ in range(k):
        m = jnp.max(s, axis=0, keepdims=True)
        idx = jnp.min(jnp.where(s == m, row, s.shape[0]), axis=0, keepdims=True)
        vals.append(m)
        idxs.append(idx)
        s = jnp.where(row == idx, -jnp.inf, s)
    return vals, idxs


def _stack_rows(rows, row16):
    acc = jnp.zeros(row16.shape, rows[0].dtype)
    for r, v in enumerate(rows):
        acc = jnp.where(row16 == r, v, acc)
    return acc


def _topk_kernel(sc_ref, idx_ref, gate_ref):
    ts = sc_ref.shape[-1]
    row = lax.broadcasted_iota(jnp.int32, (PEER_KEYS, ts), 0).astype(F32)
    row16 = lax.broadcasted_iota(jnp.int32, (PEER_TOPK, ts), 0)
    row8 = lax.broadcasted_iota(jnp.int32, (SUBLANES, ts), 0)
    counts = [PEER_TOPK // (i + 1) for i in range(PEER_TOPK)]
    heights = [PEER_TOPK if c > SUBLANES else SUBLANES for c in counts]
    n_cand = sum(heights)
    rowc = lax.broadcasted_iota(jnp.int32, (n_cand, ts), 0).astype(F32)
    gate_rows, eid_rows = [], []
    for h in range(PEER_HEADS):
        v1, i1 = _top_rows(sc_ref[2 * h], row, PEER_TOPK)
        v2, i2 = _top_rows(sc_ref[2 * h + 1], row, PEER_TOPK)
        v2s = _stack_rows(v2, row16)
        i2s = _stack_rows(i2, row16)
        cand, eid = [], []
        for i in range(PEER_TOPK):
            n = heights[i]
            cand.append(jnp.where((row16 if n == PEER_TOPK else row8) < counts[i],
                                  v1[i] + v2s[:n], -jnp.inf))
            eid.append(i1[i] * PEER_KEYS + i2s[:n])
        cand = jnp.concatenate(cand, axis=0)
        eid = jnp.concatenate(eid, axis=0)
        tops, picks = [], []
        for _ in range(PEER_TOPK):
            m = jnp.max(cand, axis=0, keepdims=True)
            pos = jnp.min(jnp.where(cand == m, rowc, n_cand), axis=0, keepdims=True)
            hit = rowc == pos
            picks.append(jnp.max(jnp.where(hit, eid, -1.0), axis=0, keepdims=True))
            tops.append(m)
            cand = jnp.where(hit, -jnp.inf, cand)
        top = _stack_rows(tops, row16)
        p = jnp.exp(top - jnp.max(top, axis=0, keepdims=True))
        gate_rows.append(p / jnp.sum(p, axis=0, keepdims=True))
        eid_rows.append(_stack_rows(picks, row16))
    gate_ref[...] = jnp.transpose(jnp.concatenate(gate_rows, axis=0))
    idx_ref[...] = jnp.transpose(jnp.concatenate(eid_rows, axis=0)).astype(jnp.int32)


SC_CORES = 2
SC_SUBCORES = 16
SC_LANES = 16
SC_WORKERS = SC_CORES * SC_SUBCORES
PEER_CH = SC_LANES
PEER_NCH = PEER_SEL // PEER_CH
PEER_WORDS = D_MODEL // 2
PEER_NWG = PEER_WORDS // SC_LANES
PEER_RING = 4
PEER_QUAD = 4
HI_MASK = -65536
GELU_C = 0.7978845608028654


def _gelu_tanh_via_exp(x):
    z = GELU_C * (x + 0.044715 * (x * x * x))
    t = 1.0 - 2.0 / (jnp.exp(2.0 * z) + 1.0)
    return 0.5 * x * (1.0 + t)


def _unpack_pair(w):
    lo = plsc.bitcast(lax.shift_left(w, 16), F32)
    hi = plsc.bitcast(lax.bitwise_and(w, HI_MASK), F32)
    return lo, hi


def _peer_sc_body(idx_hbm, gate_hbm, h_hbm, uv_hbm, after_hbm, o_hbm,
                  idx_v, gate_v, h_v, buf, out_v, gsem, msem, osem):
    n_tok = o_hbm.shape[0] // SC_WORKERS
    base = (lax.axis_index("s") * SC_CORES + lax.axis_index("c")) * n_tok
    lane = lax.iota(jnp.int32, SC_LANES)
    zero_rows = jnp.zeros((SC_LANES,), jnp.int32)

    def meta_copies(tok, s):
        return (pltpu.make_async_copy(idx_hbm.at[tok], idx_v.at[s], msem.at[s]),
                pltpu.make_async_copy(gate_hbm.at[tok], gate_v.at[s], msem.at[s]),
                pltpu.make_async_copy(h_hbm.at[tok], h_v.at[s], msem.at[s]))

    def gather(slot, rows):
        return pltpu.make_async_copy(uv_hbm.at[rows], buf.at[slot], gsem.at[slot])

    def token(t, carry):
        s = t % 2
        tok = base + t
        nxt = base + jnp.minimum(t + 1, n_tok - 1)
        for cp in meta_copies(nxt, 1 - s):
            cp.start()

        @pl.when(t >= 2)
        def _():
            pltpu.make_async_copy(out_v.at[s], o_hbm.at[tok], osem.at[s]).wait()

        def chunk(c, carry):
            slot = c % PEER_RING
            gather(slot, zero_rows).wait()

            def dot_step(q, accs):
                cols = [pl.ds(pl.multiple_of((q * PEER_QUAD + j) * SC_LANES, SC_LANES), SC_LANES)
                        for j in range(PEER_QUAD)]
                hs = [plsc.bitcast(h_v[s, col], BF16) for col in cols]
                out = []
                for r in range(PEER_CH):
                    p = plsc.bitcast(buf[slot, r, cols[0]], BF16) * hs[0]
                    for j in range(1, PEER_QUAD):
                        p = p + plsc.bitcast(buf[slot, r, cols[j]], BF16) * hs[j]
                    lo, hi = _unpack_pair(plsc.bitcast(p, jnp.int32))
                    out.append(accs[r] + lo + hi)
                return tuple(out)

            accs = lax.fori_loop(0, PEER_NWG // PEER_QUAD, dot_step,
                                 tuple(jnp.zeros((SC_LANES,), F32) for _ in range(PEER_CH)))
            tot = jnp.zeros((SC_LANES,), F32)
            for r in range(PEER_CH):
                tot = jnp.where(lane == r, jnp.sum(accs[r]), tot)
            rows = pl.ds(pl.multiple_of(c * PEER_CH, PEER_CH), PEER_CH)
            wvec = gate_v[s, rows] * _gelu_tanh_via_exp(tot)
            ws = []
            for r in range(PEER_CH):
                w = wvec.at[jnp.full((SC_LANES,), r, jnp.int32)].get(mode="promise_in_bounds")
                ws.append(plsc.pack(w, w, format=plsc.PackFormat.INTERLEAVED,
                                    preferred_element_type=BF16))
            first = c == 0

            @plsc.parallel_loop(0, PEER_NWG, unroll=2)
            def acc_step(g):
                col = pl.ds(pl.multiple_of(g * SC_LANES, SC_LANES), SC_LANES)
                col_v = pl.ds(pl.multiple_of(PEER_WORDS + g * SC_LANES, SC_LANES), SC_LANES)
                o_lo = jnp.where(first, 0.0, out_v[s, col])
                o_hi = jnp.where(first, 0.0, out_v[s, col_v])
                for r0 in range(0, PEER_CH, PEER_QUAD):
                    p = plsc.bitcast(buf[slot, r0, col_v], BF16) * ws[r0]
                    for r in range(r0 + 1, r0 + PEER_QUAD):
                        p = p + plsc.bitcast(buf[slot, r, col_v], BF16) * ws[r]
                    lo, hi = _unpack_pair(plsc.bitcast(p, jnp.int32))
                    o_lo = o_lo + lo
                    o_hi = o_hi + hi
                out_v[s, col] = o_lo
                out_v[s, col_v] = o_hi

            @pl.when(c == PEER_NCH - PEER_RING)
            def _():
                for cp in meta_copies(nxt, 1 - s):
                    cp.wait()

            ahead = c + PEER_RING
            src = jnp.where(ahead < PEER_NCH, s, 1 - s)
            nrows = idx_v[src, pl.ds(pl.multiple_of((ahead % PEER_NCH) * PEER_CH, PEER_CH), PEER_CH)]
            gather(slot, nrows).start()
            return carry

        lax.fori_loop(0, PEER_NCH, chunk, 0)
        pltpu.make_async_copy(out_v.at[s], o_hbm.at[tok], osem.at[s]).start()
        return carry

    for cp in meta_copies(base, 0):
        cp.start()
    for cp in meta_copies(base, 0):
        cp.wait()
    for c in range(PEER_RING):
        gather(c, idx_v[0, pl.ds(c * PEER_CH, PEER_CH)]).start()
    lax.fori_loop(0, n_tok, token, 0)
    for c in range(PEER_RING):
        gather(c, zero_rows).wait()
    for s in range(2):
        pltpu.make_async_copy(out_v.at[s], o_hbm.at[base], osem.at[s]).wait()


PACK_ROWS = 256


def _pack_words(x):
    half = x.shape[1] // 2
    return _bf16_bits(x[:, :half]) | lax.shift_left(_bf16_bits(x[:, half:]), 16)


def _pack_tables_kernel(u_ref, v_ref, o_ref):
    o_ref[:, :PEER_WORDS] = _pack_words(u_ref[...])
    o_ref[:, PEER_WORDS:] = _pack_words(v_ref[...])


def _pack_tables(peer_u, peer_v):
    n = peer_u.shape[0]
    rows = min(PACK_ROWS, n)
    spec = pl.BlockSpec((rows, D_MODEL), lambda i: (i, 0))
    return pl.pallas_call(
        _pack_tables_kernel,
        grid=(n // rows,),
        in_specs=[spec, spec],
        out_specs=spec,
        out_shape=jax.ShapeDtypeStruct((n, D_MODEL), jnp.int32),
        compiler_params=pltpu.CompilerParams(
            dimension_semantics=("parallel",), vmem_limit_bytes=VMEM_LIMIT),
        name="pack_tables",
    )(peer_u, peer_v)


def _peer(idx, h_words, gates, uv_words, after):
    T = h_words.shape[0]
    assert T % (2 * SC_WORKERS) == 0
    mesh = plsc.VectorSubcoreMesh(core_axis_name="c", subcore_axis_name="s",
                                  num_cores=SC_CORES, num_subcores=SC_SUBCORES)
    return pl.kernel(
        _peer_sc_body,
        out_type=jax.ShapeDtypeStruct((T, D_MODEL), F32),
        mesh=mesh,
        scratch_types=[
            pltpu.VMEM((2, PEER_SEL), jnp.int32), pltpu.VMEM((2, PEER_SEL), F32),
            pltpu.VMEM((2, PEER_WORDS), jnp.int32),
            pltpu.VMEM((PEER_RING, PEER_CH, 2 * PEER_WORDS), jnp.int32),
            pltpu.VMEM((2, D_MODEL), F32),
            pltpu.SemaphoreType.DMA((PEER_RING,)),
            pltpu.SemaphoreType.DMA((2,)), pltpu.SemaphoreType.DMA((2,)),
        ],
        compiler_params=pltpu.CompilerParams(needs_layout_passes=False),
        name="peer_sc",
    )(idx, gates, h_words, uv_words, after)


FINAL_TS = 256


def _final_kernel(x1_ref, pe_ref, p_ref, gp_ref, wg_ref, wp_ref, gf_ref, o_ref):
    x2 = x1_ref[...] + pe_ref[...]
    e = _dot(p_ref[...].astype(BF16), wp_ref[...])
    gate = jax.nn.sigmoid(_dot(_rms(x2, gp_ref[...]).astype(BF16), wg_ref[...]))
    o_ref[...] = _rms(x2 + gate * e, gf_ref[...])


def _final(x1, peer_out, p, t0, nt, g_ple, ple_w_gate, ple_w_proj, g_final):
    B = p.shape[0]
    ts = min(FINAL_TS, nt)
    nblk = nt // ts
    i0 = t0 // ts
    row = lambda d: pl.BlockSpec((ts, d), lambda b, i: (b * nblk + i, 0))
    full = lambda shape: pl.BlockSpec(shape, lambda b, i: (0,) * len(shape))
    return pl.pallas_call(
        _final_kernel,
        grid=(B, nblk),
        in_specs=[row(D_MODEL), row(D_MODEL),
                  pl.BlockSpec((None, ts, D_PLE), lambda b, i: (b, i0 + i, 0)),
                  full((1, D_MODEL)), full((D_MODEL, D_MODEL)), full((D_PLE, D_MODEL)),
                  full((1, D_MODEL))],
        out_specs=pl.BlockSpec((None, ts, D_MODEL), lambda b, i: (b, i, 0)),
        out_shape=jax.ShapeDtypeStruct((B, nt, D_MODEL), F32),
        compiler_params=pltpu.CompilerParams(
            dimension_semantics=("parallel", "parallel"), vmem_limit_bytes=VMEM_LIMIT),
        name="final",
    )(x1, peer_out, p, g_ple, ple_w_gate, ple_w_proj, g_final)


CHUNK_STEPS = (512, 512, 512, 512, 512, 1024, 1024, 1024, 1024, 1024, 512)


def kernel(x, p, positions, g_mix, w_in, ssm_log_dt, ssm_a_re, ssm_a_im, ssm_b_re, ssm_b_im,
           ssm_c_re, ssm_c_im, ssm_d, ssm_w_glu, w_proj_ssm, w_proj_att, w_out, g_ffn,
           peer_w_q, peer_keys1, peer_keys2, peer_u, peer_v, g_ple, ple_w_gate, ple_w_proj,
           g_final):
    B, S, _ = x.shape
    assert w_in.shape[0] == 1, "the final rmsnorm is fused into the single layer's last stage"
    steps = CHUNK_STEPS if sum(CHUNK_STEPS) == S else (S,)
    i = 0
    tables = _s5_tables(ssm_log_dt[i], ssm_a_re[i], ssm_a_im[i], ssm_b_re[i], ssm_b_im[i],
                        ssm_c_re[i], ssm_c_im[i])
    w_in_b, w_glu_b = w_in[i].astype(BF16), ssm_w_glu[i].astype(BF16)
    d_skip = ssm_d[i].reshape(1, D_SSM).astype(F32)
    merge_w = (w_proj_ssm[i].astype(BF16), w_proj_att[i].astype(BF16), w_out[i].astype(BF16),
               g_ffn[i].reshape(1, D_MODEL), peer_w_q[i].astype(BF16), peer_keys1[i], peer_keys2[i])
    final_w = (g_ple[i].reshape(1, D_MODEL), ple_w_gate[i].astype(BF16),
               ple_w_proj[i].astype(BF16), g_final.reshape(1, D_MODEL))
    uv_words = _pack_tables(peer_u[i], peer_v[i])
    k_all = jnp.zeros((B, S, D_ATT), BF16)
    v_all = jnp.zeros((B, S, D_ATT), BF16)
    carry = jnp.zeros((2, SUBLANES, D_STATE), F32)
    outs = []
    t0 = 0
    after = (carry, carry)
    peers = []
    for nt in steps:
        u_sb, q, k, v, ga, gb = _in_proj(x, positions, g_mix[i], w_in_b, t0, nt, after)
        k_all = lax.dynamic_update_slice(k_all, k, (0, t0, 0))
        v_all = lax.dynamic_update_slice(v_all, v, (0, t0, 0))
        ys, carry = _s5(u_sb, carry, tables, d_skip, w_glu_b, B)
        att = _moba(q, k_all, v_all, t0 // MOBA_BLOCK)
        subs = ((0, nt // 2), (nt // 2, nt // 2)) if t0 == 0 and nt >= 2 * MERGE_TS else ((0, nt),)
        for off, n in subs:
            x1, h_words, idx, gates = _merge(x, ys, att, ga, gb, t0, off, n,
                                             peers[-2] if len(peers) > 1 else carry, *merge_w)
            peer_out = _peer(idx, h_words, gates, uv_words,
                             peers[-1] if t0 + off + n == S else carry)
            peers.append(peer_out)
            outs.append(_final(x1, peer_out, p[i], t0 + off, n, *final_w))
        after = (gates, outs[-3] if len(outs) > 2 else carry)
        t0 += nt
    return jnp.concatenate(outs, axis=1)
```

```python
import functools

import jax
import jax.numpy as jnp
from jax import lax
from jax.experimental import pallas as pl
from jax.experimental.pallas import tpu as pltpu
from jax.experimental.pallas import tpu_sc as plsc

F32 = jnp.float32
BF16 = jnp.bfloat16

D_MODEL = 1024
D_SSM = 512
SSM_GROUPS = 32
SSM_STATE = 64
D_STATE = SSM_GROUPS * SSM_STATE
HEAD_DIM = 64
D_ATT = 512
ROT_DIM = 16
ROPE_THETA = 500000.0
MOBA_BLOCK = 256
MOBA_TOPK = 3
PEER_HEADS = 8
PEER_KEYS = 128
PEER_QDIM = 256
PEER_HALF = 128
PEER_TOPK = 16
PEER_SEL = PEER_HEADS * PEER_TOPK
D_PLE = 256
EPS = 1e-6
NEG = -1e30
LANES = 128
SUBLANES = 8
VMEM_LIMIT = 48 * 1024 * 1024
HIGHEST = lax.Precision.HIGHEST


def _rms(x, g):
    return x * lax.rsqrt(jnp.mean(x * x, axis=-1, keepdims=True) + EPS) * g


def _dot(a, b):
    return jnp.dot(a, b, preferred_element_type=F32)


def _dot_nt(a, b, precision=None):
    return lax.dot_general(a, b, (((1,), (1,)), ((), ())), precision=precision,
                           preferred_element_type=F32)


IN_TS = 512


def _in_proj_kernel(x_ref, pos_ref, g_ref, w_ref, invf_ref, after_a, after_b,
                    u_ref, q_ref, k_ref, v_ref, ga_ref, gb_ref):
    del after_a, after_b
    h = _rms(x_ref[...], g_ref[...]).astype(BF16)

    def proj(lo, hi):
        return _dot(h, w_ref[:, lo:hi])

    u_ref[...] = proj(0, D_SSM).astype(BF16)
    ang = pos_ref[...].astype(F32) * invf_ref[...]
    cos = jnp.cos(ang)
    sin = jnp.sin(ang)
    lane = lax.broadcasted_iota(jnp.int32, (1, LANES), 1) % HEAD_DIM
    half = ROT_DIM // 2
    sin_hi = jnp.where((lane >= half) & (lane < ROT_DIM), sin, 0.0)
    sin_lo = jnp.where(lane < half, -sin, 0.0)
    reps = D_ATT // LANES
    cos4 = jnp.concatenate([cos] * reps, axis=1)
    sin_hi4 = jnp.concatenate([sin_hi] * reps, axis=1)
    sin_lo4 = jnp.concatenate([sin_lo] * reps, axis=1)

    def rope(t):
        return (t * cos4 + pltpu.roll(t, half, 1) * sin_hi4
                + pltpu.roll(t, D_ATT - half, 1) * sin_lo4)

    q = rope(proj(D_SSM, D_SSM + D_ATT))
    q_ref[...] = (q * (HEAD_DIM ** -0.5)).astype(BF16)
    k_ref[...] = rope(proj(D_SSM + D_ATT, D_SSM + 2 * D_ATT)).astype(BF16)
    v_ref[...] = proj(D_SSM + 2 * D_ATT, D_SSM + 3 * D_ATT).astype(BF16)
    o = D_SSM + 3 * D_ATT
    ga_ref[...] = jax.nn.sigmoid(proj(o, o + D_MODEL)).astype(BF16)
    gb_ref[...] = jax.nn.sigmoid(proj(o + D_MODEL, o + 2 * D_MODEL)).astype(BF16)


def _in_proj(x, positions, g_mix, w_in, t0, nt, after):
    B, S, _ = x.shape
    ts = min(IN_TS, nt)
    assert nt % ts == 0 and t0 % ts == 0
    i0 = t0 // ts
    inv_freq = ROPE_THETA ** (-jnp.arange(0, ROT_DIM, 2, dtype=F32) / ROT_DIM)
    lane = jnp.arange(LANES) % HEAD_DIM
    invf = jnp.where(lane < ROT_DIM, inv_freq[lane % (ROT_DIM // 2)], 0.0).reshape(1, LANES)
    d_in = w_in.shape[1]
    src = lambda d: pl.BlockSpec((None, ts, d), lambda b, i: (b, i0 + i, 0))
    tok = lambda d: pl.BlockSpec((None, ts, d), lambda b, i: (b, i, 0))
    full = lambda shape: pl.BlockSpec(shape, lambda b, i: (0,) * len(shape))
    return pl.pallas_call(
        _in_proj_kernel,
        grid=(B, nt // ts),
        in_specs=[src(D_MODEL), src(1), full((1, D_MODEL)), full((D_MODEL, d_in)), full((1, LANES)),
                  pl.BlockSpec(memory_space=pl.ANY), pl.BlockSpec(memory_space=pl.ANY)],
        out_specs=[pl.BlockSpec((ts, D_SSM), lambda b, i: (i, b)),
                   tok(D_ATT), tok(D_ATT), tok(D_ATT), tok(D_MODEL), tok(D_MODEL)],
        out_shape=[jax.ShapeDtypeStruct((nt, B * D_SSM), BF16),
                   jax.ShapeDtypeStruct((B, nt, D_ATT), BF16),
                   jax.ShapeDtypeStruct((B, nt, D_ATT), BF16),
                   jax.ShapeDtypeStruct((B, nt, D_ATT), BF16),
                   jax.ShapeDtypeStruct((B, nt, D_MODEL), BF16),
                   jax.ShapeDtypeStruct((B, nt, D_MODEL), BF16)],
        compiler_params=pltpu.CompilerParams(
            dimension_semantics=("parallel", "parallel"), vmem_limit_bytes=VMEM_LIMIT),
        name="in_proj",
    )(x, positions.reshape(B, S, 1), g_mix.reshape(1, D_MODEL), w_in, invf, *after)


S5_TS = 128
S5_BATCH = 4
S5_COLS = 512


def _s5_kernel(u_ref, c0_ref, bre_ref, bim_ref, a1r_ref, a1i_ref, pr_ref, pi_ref,
               cre_ref, cim_ref, d_ref, wglu_ref, y_ref, c1_ref,
               xr, xi, cr, ci, ysc):
    rows = xr.shape[0]
    ts = rows // S5_BATCH

    @pl.when(pl.program_id(0) == 0)
    def _():
        cr[...] = c0_ref[0]
        ci[...] = c0_ref[1]

    u = u_ref[...]
    for cb in range(D_STATE // S5_COLS):
        sl = slice(cb * S5_COLS, (cb + 1) * S5_COLS)
        u_cb = u[:, cb * LANES:(cb + 1) * LANES]
        xr[:, sl] = _dot(u_cb, bre_ref[cb])
        xi[:, sl] = _dot(u_cb, bim_ref[cb])

    hi_rows = lax.broadcasted_iota(jnp.int32, (SUBLANES, S5_COLS), 0) >= S5_BATCH
    for cb in range(D_STATE // S5_COLS):
        sl = slice(cb * S5_COLS, (cb + 1) * S5_COLS)
        a_r, a_i = a1r_ref[:, sl], a1i_ref[:, sl]
        p_r, p_i = pr_ref[:, sl], pi_ref[:, sl]

        def body(t, carry):
            c_r, c_i = carry
            r0 = pl.multiple_of(t * SUBLANES, SUBLANES)
            x_r = xr[pl.ds(r0, SUBLANES), sl]
            x_i = xi[pl.ds(r0, SUBLANES), sl]
            s_r = pltpu.roll(x_r, S5_BATCH, 0)
            s_i = pltpu.roll(x_i, S5_BATCH, 0)
            h_r = x_r + (a_r * s_r - a_i * s_i) + (p_r * c_r - p_i * c_i)
            h_i = x_i + (a_r * s_i + a_i * s_r) + (p_r * c_i + p_i * c_r)
            xr[pl.ds(r0, SUBLANES), sl] = h_r
            xi[pl.ds(r0, SUBLANES), sl] = h_i
            n_r = jnp.where(hi_rows, h_r, pltpu.roll(h_r, S5_BATCH, 0))
            n_i = jnp.where(hi_rows, h_i, pltpu.roll(h_i, S5_BATCH, 0))
            return n_r, n_i

        c_r, c_i = lax.fori_loop(0, rows // SUBLANES, body, (cr[:, sl], ci[:, sl]), unroll=2)
        cr[:, sl] = c_r
        ci[:, sl] = c_i

    y = jnp.concatenate(
        [_dot(xr[:, cb * S5_COLS:(cb + 1) * S5_COLS].astype(BF16), cre_ref[cb])
         - _dot(xi[:, cb * S5_COLS:(cb + 1) * S5_COLS].astype(BF16), cim_ref[cb])
         for cb in range(D_STATE // S5_COLS)], axis=1) + d_ref[...] * u.astype(F32)
    y = jax.nn.gelu(y)
    y = y * jax.nn.sigmoid(_dot(y.astype(BF16), wglu_ref[...]))
    for c in range(D_SSM // LANES):
        ysc[c] = y[:, c * LANES:(c + 1) * LANES]
    for b in range(S5_BATCH):
        for c in range(D_SSM // LANES):
            y_ref[b, :, c * LANES:(c + 1) * LANES] = (
                ysc[c, pl.ds(b, ts, stride=S5_BATCH), :].astype(BF16))

    @pl.when(pl.program_id(0) == pl.num_programs(0) - 1)
    def _():
        c1_ref[0] = cr[...]
        c1_ref[1] = ci[...]


def _s5_tables(log_dt, a_re, a_im, b_re, b_im, c_re, c_im):
    dt = jnp.exp(log_dt.astype(F32))[:, None]
    ar, ai = a_re.astype(F32), a_im.astype(F32)
    mag = jnp.exp(dt * ar)
    abar_re, abar_im = mag * jnp.cos(dt * ai), mag * jnp.sin(dt * ai)
    den = ar * ar + ai * ai
    nr, ni = abar_re - 1.0, abar_im
    f_re = (nr * ar + ni * ai) / den
    f_im = (ni * ar - nr * ai) / den
    br, bi = b_re.astype(F32), b_im.astype(F32)
    bb_re = f_re[..., None] * br - f_im[..., None] * bi
    bb_im = f_re[..., None] * bi + f_im[..., None] * br
    eye = jnp.eye(SSM_GROUPS, dtype=F32)

    def in_blockdiag(bb):
        return jnp.einsum('gnc,gh->gchn', bb, eye).reshape(D_SSM, D_STATE)

    def out_blockdiag(c):
        return jnp.einsum('gcn,gh->gnhc', c.astype(F32), eye).reshape(D_STATE, D_SSM)

    a_r = abar_re.reshape(1, D_STATE)
    a_i = abar_im.reshape(1, D_STATE)
    a2_r = a_r * a_r - a_i * a_i
    a2_i = 2.0 * a_r * a_i
    hi = (jnp.arange(SUBLANES) >= S5_BATCH)[:, None]
    a1r = jnp.where(hi, a_r, 0.0)
    a1i = jnp.where(hi, a_i, 0.0)
    p_r = jnp.where(hi, a2_r, a_r)
    p_i = jnp.where(hi, a2_i, a_i)
    nblk = D_STATE // S5_COLS
    cw = D_SSM // nblk

    def in_blocks(m):
        return jnp.stack([m[b * cw:(b + 1) * cw, b * S5_COLS:(b + 1) * S5_COLS] for b in range(nblk)])

    def out_blocks(m):
        return jnp.stack([m[b * S5_COLS:(b + 1) * S5_COLS, b * cw:(b + 1) * cw] for b in range(nblk)])

    return (in_blocks(in_blockdiag(bb_re)).astype(BF16), in_blocks(in_blockdiag(bb_im)).astype(BF16),
            a1r, a1i, p_r, p_i,
            out_blocks(out_blockdiag(c_re)).astype(BF16), out_blocks(out_blockdiag(c_im)).astype(BF16))


def _s5(u_sb, carry, tables, d_skip, w_glu, B):
    assert B == S5_BATCH
    nt = u_sb.shape[0]
    ts = min(S5_TS, nt)
    rows = ts * B
    bre, bim, a1r, a1i, p_r, p_i, cre, cim = tables
    full = lambda shape: pl.BlockSpec(shape, lambda i: (0,) * len(shape))
    return pl.pallas_call(
        _s5_kernel,
        grid=(nt // ts,),
        in_specs=[pl.BlockSpec((rows, D_SSM), lambda i: (i, 0)),
                  full((2, SUBLANES, D_STATE)),
                  full(bre.shape), full(bim.shape),
                  full((SUBLANES, D_STATE)), full((SUBLANES, D_STATE)),
                  full((SUBLANES, D_STATE)), full((SUBLANES, D_STATE)),
                  full(cre.shape), full(cim.shape),
                  full((1, D_SSM)), full((D_SSM, D_SSM))],
        out_specs=[pl.BlockSpec((B, ts, D_SSM), lambda i: (0, i, 0)),
                   full((2, SUBLANES, D_STATE))],
        out_shape=[jax.ShapeDtypeStruct((B, nt, D_SSM), BF16),
                   jax.ShapeDtypeStruct((2, SUBLANES, D_STATE), F32)],
        scratch_shapes=[pltpu.VMEM((rows, D_STATE), F32), pltpu.VMEM((rows, D_STATE), F32),
                        pltpu.VMEM((SUBLANES, D_STATE), F32), pltpu.VMEM((SUBLANES, D_STATE), F32),
                        pltpu.VMEM((D_SSM // LANES, rows, LANES), F32)],
        compiler_params=pltpu.CompilerParams(
            dimension_semantics=("arbitrary",), vmem_limit_bytes=VMEM_LIMIT),
        name="s5",
    )(u_sb.reshape(nt * B, D_SSM), carry, bre, bim, a1r, a1i, p_r, p_i, cre, cim, d_skip, w_glu)


MOBA_PAIR = 2 * MOBA_BLOCK


def _moba_kernel(q0, q_ref, k_ref, v_ref, o_ref, kmean, kaug_a, kaug_b, vaug_a, vaug_b, qaug,
                 m_s, acc_s, s_buf):
    last = pl.program_id(2) + q0 // 2
    nb = k_ref.shape[0] // MOBA_BLOCK
    nbp = kmean.shape[0]
    lane = lax.broadcasted_iota(jnp.int32, (1, LANES), 1)
    head_a = lane < HEAD_DIM

    @pl.when(pl.program_id(2) == 0)
    def _():
        kmean[...] = jnp.zeros_like(kmean)
        for j in range(nb):
            rows = pl.ds(j * MOBA_BLOCK, MOBA_BLOCK)
            kj = k_ref[rows, :].astype(F32)
            vj = v_ref[rows, :].astype(F32)
            kmean[j:j + 1, :] = jnp.sum(kj, axis=0, keepdims=True) * (1.0 / MOBA_BLOCK)
            kaug_a[rows, :] = jnp.where(head_a, kj, jnp.where(lane - HEAD_DIM == j, 1.0, 0.0)).astype(BF16)
            kaug_b[rows, :] = jnp.where(head_a, jnp.where(lane == j, 1.0, 0.0), kj).astype(BF16)
            vaug_a[rows, :] = jnp.where(head_a, vj, 1.0).astype(BF16)
            vaug_b[rows, :] = jnp.where(head_a, 1.0, vj).astype(BF16)
        blk_row = lax.broadcasted_iota(jnp.int32, (nbp, MOBA_BLOCK), 0)
        for t in range(q_ref.shape[0] // MOBA_BLOCK):
            qt = q0 + t
            qf = q_ref[t * MOBA_BLOCK:(t + 1) * MOBA_BLOCK, :].astype(F32)
            for hd, is_a in enumerate((True, False)):
                mine = head_a if is_a else jnp.logical_not(head_a)
                q_own = jnp.where(mine, qf, 0.0)
                g = _dot_nt(kmean[...], q_own, precision=HIGHEST)
                g = jnp.where(blk_row < qt, g, NEG)
                sel = jnp.zeros(g.shape, F32)
                for _ in range(MOBA_TOPK):
                    m = jnp.max(g, axis=0, keepdims=True)
                    idx = jnp.min(jnp.where(g == m, blk_row, nbp), axis=0, keepdims=True)
                    hit = blk_row == idx
                    sel = jnp.where(hit, jnp.where(idx < qt, 1.0, 0.0), sel)
                    g = jnp.where(hit, -jnp.inf, g)
                bias_t = jnp.where(sel > 0.0, 0.0, jnp.where(blk_row == qt, 0.0, NEG))
                bias_t = jnp.concatenate([bias_t, jnp.full((LANES - nbp, MOBA_BLOCK), NEG, F32)], axis=0)
                bias = jnp.transpose(bias_t)
                if is_a:
                    bias = pltpu.roll(bias, HEAD_DIM, 1)
                qaug[hd, t * MOBA_BLOCK:(t + 1) * MOBA_BLOCK, :] = jnp.where(mine, qf, bias).astype(BF16)

    tile_rows = pl.ds(pl.multiple_of(pl.program_id(2) * MOBA_PAIR, MOBA_PAIR), MOBA_PAIR)
    q_augs = [qaug[0, tile_rows, :], qaug[1, tile_rows, :]]

    m_s[...] = jnp.full(m_s.shape, -jnp.inf, F32)
    acc_s[...] = jnp.zeros_like(acc_s)
    qpos = last * MOBA_PAIR + lax.broadcasted_iota(jnp.int32, (MOBA_PAIR, MOBA_PAIR), 0)
    col = lax.broadcasted_iota(jnp.int32, (MOBA_PAIR, MOBA_PAIR), 1)

    def kv_rows(jj):
        return pl.ds(pl.multiple_of(jj * MOBA_PAIR, MOBA_PAIR), MOBA_PAIR)

    def scores(jj, slot):
        for hd, kaug in enumerate((kaug_a, kaug_b)):
            s_buf[slot, hd] = _dot_nt(q_augs[hd], kaug[kv_rows(jj), :])

    def softmax_pv(jj, slot, causal):
        for hd, vaug in enumerate((vaug_a, vaug_b)):
            s = s_buf[slot, hd]
            if causal:
                s = jnp.where(jj * MOBA_PAIR + col <= qpos, s, NEG)
            m_old = m_s[hd]
            m_new = jnp.maximum(m_old, jnp.max(s, axis=-1, keepdims=True))
            alpha = jnp.exp(m_old - m_new)
            p = jnp.exp(s - m_new)
            m_s[hd] = m_new
            acc_s[hd] = alpha * acc_s[hd] + _dot(p.astype(BF16), vaug[kv_rows(jj), :])

    scores(0, 0)

    def body(k, _):
        scores(2 * k + 1, 1)
        softmax_pv(2 * k, 0, False)
        scores(2 * k + 2, 0)
        softmax_pv(2 * k + 1, 1, False)
        return 0

    lax.fori_loop(0, last // 2, body, 0)

    @pl.when(last % 2 == 0)
    def _():
        softmax_pv(last, 0, True)

    @pl.when(last % 2 == 1)
    def _():
        scores(last, 1)
        softmax_pv(last - 1, 0, False)
        softmax_pv(last, 1, True)
    acc_a, acc_b = acc_s[0], acc_s[1]
    o_ref[...] = jnp.where(head_a, acc_a / pltpu.roll(acc_a, HEAD_DIM, 1),
                           acc_b / pltpu.roll(acc_b, HEAD_DIM, 1)).astype(BF16)


def _moba(q, k, v, q0):
    B = q.shape[0]
    nq = q.shape[1] // MOBA_BLOCK
    skv = (q0 + nq) * MOBA_BLOCK
    nb = skv // MOBA_BLOCK
    assert nb <= HEAD_DIM and nb % 2 == 0 and skv <= k.shape[1]
    nbp = -(-nb // SUBLANES) * SUBLANES
    assert q0 % 2 == 0 and nq % 2 == 0
    blk = pl.BlockSpec((None, MOBA_PAIR, LANES), lambda b, h, i: (b, i, h))
    seq = pl.BlockSpec((None, skv, LANES), lambda b, h, i: (b, 0, h))
    return pl.pallas_call(
        functools.partial(_moba_kernel, q0),
        grid=(B, D_ATT // LANES, nq // 2),
        in_specs=[pl.BlockSpec((None, nq * MOBA_BLOCK, LANES), lambda b, h, i: (b, 0, h)), seq, seq],
        out_specs=blk,
        out_shape=jax.ShapeDtypeStruct(q.shape, BF16),
        scratch_shapes=[pltpu.VMEM((nbp, LANES), F32),
                        pltpu.VMEM((skv, LANES), BF16), pltpu.VMEM((skv, LANES), BF16),
                        pltpu.VMEM((skv, LANES), BF16), pltpu.VMEM((skv, LANES), BF16),
                        pltpu.VMEM((2, nq * MOBA_BLOCK, LANES), BF16),
                        pltpu.VMEM((2, MOBA_PAIR, 1), F32),
                        pltpu.VMEM((2, MOBA_PAIR, LANES), F32),
                        pltpu.VMEM((2, 2, MOBA_PAIR, MOBA_PAIR), F32)],
        compiler_params=pltpu.CompilerParams(
            dimension_semantics=("parallel", "parallel", "arbitrary"), vmem_limit_bytes=VMEM_LIMIT),
        name="moba",
    )(q, k, v)


MERGE_TS = 256


def _bf16_bits(x):
    b = pltpu.bitcast(x, jnp.int32)
    r = b + 0x7FFF + (lax.shift_right_logical(b, 16) & 1)
    return lax.shift_right_logical(r, 16)


def _merge_kernel(x_ref, ys_ref, at_ref, ga_ref, gb_ref, wa_ref, wb_ref, wo_ref, g_ref,
                  wq_ref, k1_ref, k2_ref, after_ref, x1_ref, hw_ref, idx_ref, gate_ref, sc_ref):
    del after_ref
    ya = _dot(ys_ref[...], wa_ref[...])
    yb = _dot(at_ref[...], wb_ref[...])
    merged = ga_ref[...].astype(F32) * ya + gb_ref[...].astype(F32) * yb
    x1 = x_ref[...] + _dot(merged.astype(BF16), wo_ref[...])
    x1_ref[...] = x1
    hq = _rms(x1, g_ref[...])
    hw_ref[...] = _pack_words(hq)
    qp = _dot(hq.astype(BF16), wq_ref[...])
    for h in range(PEER_HEADS):
        o = h * PEER_QDIM
        sc_ref[2 * h] = _dot_nt(k1_ref[h], qp[:, o:o + PEER_HALF], precision=HIGHEST)
        sc_ref[2 * h + 1] = _dot_nt(k2_ref[h], qp[:, o + PEER_HALF:o + PEER_QDIM], precision=HIGHEST)
    _topk_kernel(sc_ref, idx_ref, gate_ref)


def _merge(x, ys, att, ga, gb, t0, off, nt, after, w_proj_ssm, w_proj_att, w_out, g_ffn, peer_w_q,
           keys1, keys2):
    B = ys.shape[0]
    ts = min(MERGE_TS, nt)
    nblk = nt // ts
    i0 = (t0 + off) // ts
    o0 = off // ts
    tok = lambda d: pl.BlockSpec((None, ts, d), lambda b, i: (b, o0 + i, 0))
    row = lambda d: pl.BlockSpec((ts, d), lambda b, i: (b * nblk + i, 0))
    full = lambda shape: pl.BlockSpec(shape, lambda b, i: (0,) * len(shape))
    qd = PEER_HEADS * PEER_QDIM
    return pl.pallas_call(
        _merge_kernel,
        grid=(B, nblk),
        in_specs=[pl.BlockSpec((None, ts, D_MODEL), lambda b, i: (b, i0 + i, 0)),
                  tok(D_SSM), tok(D_ATT), tok(D_MODEL), tok(D_MODEL),
                  full((D_SSM, D_MODEL)), full((D_ATT, D_MODEL)), full((D_MODEL, D_MODEL)),
                  full((1, D_MODEL)), full((D_MODEL, qd)),
                  full((PEER_HEADS, PEER_KEYS, PEER_HALF)), full((PEER_HEADS, PEER_KEYS, PEER_HALF)),
                  pl.BlockSpec(memory_space=pl.ANY)],
        out_specs=[row(D_MODEL), row(D_MODEL // 2), row(PEER_SEL), row(PEER_SEL)],
        out_shape=[jax.ShapeDtypeStruct((B * nt, D_MODEL), F32),
                   jax.ShapeDtypeStruct((B * nt, D_MODEL // 2), jnp.int32),
                   jax.ShapeDtypeStruct((B * nt, PEER_SEL), jnp.int32),
                   jax.ShapeDtypeStruct((B * nt, PEER_SEL), F32)],
        scratch_shapes=[pltpu.VMEM((2 * PEER_HEADS, PEER_KEYS, ts), F32)],
        compiler_params=pltpu.CompilerParams(
            dimension_semantics=("parallel", "parallel"), vmem_limit_bytes=VMEM_LIMIT),
        name="merge",
    )(x, ys, att, ga, gb, w_proj_ssm, w_proj_att, w_out, g_ffn, peer_w_q, keys1, keys2, after)


def _top_rows(s, row, k):
    vals, idxs = [], []
    for _ in range(k):
        m = jnp.max(s, axis=0, keepdims=True)
        idx = jnp.min(jnp.where(s == m, row, s.shape[0]), axis=0, keepdims=True)
        vals.append(m)
        idxs.append(idx)
        s = jnp.where(row == idx, -jnp.inf, s)
    return vals, idxs


def _stack_rows(rows, row16):
    acc = jnp.zeros(row16.shape, rows[0].dtype)
    for r, v in enumerate(rows):
        acc = jnp.where(row16 == r, v, acc)
    return acc


def _topk_kernel(sc_ref, idx_ref, gate_ref):
    ts = sc_ref.shape[-1]
    row = lax.broadcasted_iota(jnp.int32, (PEER_KEYS, ts), 0).astype(F32)
    row16 = lax.broadcasted_iota(jnp.int32, (PEER_TOPK, ts), 0)
    row8 = lax.broadcasted_iota(jnp.int32, (SUBLANES, ts), 0)
    counts = [PEER_TOPK // (i + 1) for i in range(PEER_TOPK)]
    heights = [PEER_TOPK if c > SUBLANES else SUBLANES for c in counts]
    n_cand = sum(heights)
    rowc = lax.broadcasted_iota(jnp.int32, (n_cand, ts), 0).astype(F32)
    gate_rows, eid_rows = [], []
    for h in range(PEER_HEADS):
        v1, i1 = _top_rows(sc_ref[2 * h], row, PEER_TOPK)
        v2, i2 = _top_rows(sc_ref[2 * h + 1], row, PEER_TOPK)
        v2s = _stack_rows(v2, row16)
        i2s = _stack_rows(i2, row16)
        cand, eid = [], []
        for i in range(PEER_TOPK):
            n = heights[i]
            cand.append(jnp.where((row16 if n == PEER_TOPK else row8) < counts[i],
                                  v1[i] + v2s[:n], -jnp.inf))
            eid.append(i1[i] * PEER_KEYS + i2s[:n])
        cand = jnp.concatenate(cand, axis=0)
        eid = jnp.concatenate(eid, axis=0)
        tops, picks = [], []
        for _ in range(PEER_TOPK):
            m = jnp.max(cand, axis=0, keepdims=True)
            pos = jnp.min(jnp.where(cand == m, rowc, n_cand), axis=0, keepdims=True)
            hit = rowc == pos
            picks.append(jnp.max(jnp.where(hit, eid, -1.0), axis=0, keepdims=True))
            tops.append(m)
            cand = jnp.where(hit, -jnp.inf, cand)
        top = _stack_rows(tops, row16)
        p = jnp.exp(top - jnp.max(top, axis=0, keepdims=True))
        gate_rows.append(p / jnp.sum(p, axis=0, keepdims=True))
        eid_rows.append(_stack_rows(picks, row16))
    gate_ref[...] = jnp.transpose(jnp.concatenate(gate_rows, axis=0))
    idx_ref[...] = jnp.transpose(jnp.concatenate(eid_rows, axis=0)).astype(jnp.int32)


SC_CORES = 2
SC_SUBCORES = 16
SC_LANES = 16
SC_WORKERS = SC_CORES * SC_SUBCORES
PEER_CH = SC_LANES
PEER_NCH = PEER_SEL // PEER_CH
PEER_WORDS = D_MODEL // 2
PEER_NWG = PEER_WORDS // SC_LANES
PEER_RING = 4
PEER_QUAD = 4
HI_MASK = -65536
GELU_C = 0.7978845608028654


def _gelu_tanh_via_exp(x):
    z = GELU_C * (x + 0.044715 * (x * x * x))
    t = 1.0 - 2.0 / (jnp.exp(2.0 * z) + 1.0)
    return 0.5 * x * (1.0 + t)


def _unpack_pair(w):
    lo = plsc.bitcast(lax.shift_left(w, 16), F32)
    hi = plsc.bitcast(lax.bitwise_and(w, HI_MASK), F32)
    return lo, hi


def _peer_sc_body(idx_hbm, gate_hbm, h_hbm, uv_hbm, after_hbm, o_hbm,
                  idx_v, gate_v, h_v, buf, out_v, gsem, msem, osem):
    n_tok = o_hbm.shape[0] // SC_WORKERS
    base = (lax.axis_index("s") * SC_CORES + lax.axis_index("c")) * n_tok
    lane = lax.iota(jnp.int32, SC_LANES)
    zero_rows = jnp.zeros((SC_LANES,), jnp.int32)

    def meta_copies(tok, s):
        return (pltpu.make_async_copy(idx_hbm.at[tok], idx_v.at[s], msem.at[s]),
                pltpu.make_async_copy(gate_hbm.at[tok], gate_v.at[s], msem.at[s]),
                pltpu.make_async_copy(h_hbm.at[tok], h_v.at[s], msem.at[s]))

    def gather(slot, rows):
        return pltpu.make_async_copy(uv_hbm.at[rows], buf.at[slot], gsem.at[slot])

    def token(t, carry):
        s = t % 2
        tok = base + t
        nxt = base + jnp.minimum(t + 1, n_tok - 1)
        for cp in meta_copies(nxt, 1 - s):
            cp.start()

        @pl.when(t >= 2)
        def _():
            pltpu.make_async_copy(out_v.at[s], o_hbm.at[tok], osem.at[s]).wait()

        def chunk(c, carry):
            slot = c % PEER_RING
            gather(slot, zero_rows).wait()

            def dot_step(q, accs):
                cols = [pl.ds(pl.multiple_of((q * PEER_QUAD + j) * SC_LANES, SC_LANES), SC_LANES)
                        for j in range(PEER_QUAD)]
                hs = [plsc.bitcast(h_v[s, col], BF16) for col in cols]
                out = []
                for r in range(PEER_CH):
                    p = plsc.bitcast(buf[slot, r, cols[0]], BF16) * hs[0]
                    for j in range(1, PEER_QUAD):
                        p = p + plsc.bitcast(buf[slot, r, cols[j]], BF16) * hs[j]
                    lo, hi = _unpack_pair(plsc.bitcast(p, jnp.int32))
                    out.append(accs[r] + lo + hi)
                return tuple(out)

            accs = lax.fori_loop(0, PEER_NWG // PEER_QUAD, dot_step,
                                 tuple(jnp.zeros((SC_LANES,), F32) for _ in range(PEER_CH)))
            tot = jnp.zeros((SC_LANES,), F32)
            for r in range(PEER_CH):
                tot = jnp.where(lane == r, jnp.sum(accs[r]), tot)
            rows = pl.ds(pl.multiple_of(c * PEER_CH, PEER_CH), PEER_CH)
            wvec = gate_v[s, rows] * _gelu_tanh_via_exp(tot)
            ws = []
            for r in range(PEER_CH):
                w = wvec.at[jnp.full((SC_LANES,), r, jnp.int32)].get(mode="promise_in_bounds")
                ws.append(plsc.pack(w, w, format=plsc.PackFormat.INTERLEAVED,
                                    preferred_element_type=BF16))
            first = c == 0

            @plsc.parallel_loop(0, PEER_NWG, unroll=2)
            def acc_step(g):
                col = pl.ds(pl.multiple_of(g * SC_LANES, SC_LANES), SC_LANES)
                col_v = pl.ds(pl.multiple_of(PEER_WORDS + g * SC_LANES, SC_LANES), SC_LANES)
                o_lo = jnp.where(first, 0.0, out_v[s, col])
                o_hi = jnp.where(first, 0.0, out_v[s, col_v])
                for r0 in range(0, PEER_CH, PEER_QUAD):
                    p = plsc.bitcast(buf[slot, r0, col_v], BF16) * ws[r0]
                    for r in range(r0 + 1, r0 + PEER_QUAD):
                        p = p + plsc.bitcast(buf[slot, r, col_v], BF16) * ws[r]
                    lo, hi = _unpack_pair(plsc.bitcast(p, jnp.int32))
                    o_lo = o_lo + lo
                    o_hi = o_hi + hi
                out_v[s, col] = o_lo
                out_v[s, col_v] = o_hi

            @pl.when(c == PEER_NCH - PEER_RING)
            def _():
                for cp in meta_copies(nxt, 1 - s):
                    cp.wait()

            ahead = c + PEER_RING
            src = jnp.where(ahead < PEER_NCH, s, 1 - s)
            nrows = idx_v[src, pl.ds(pl.multiple_of((ahead % PEER_NCH) * PEER_CH, PEER_CH), PEER_CH)]
            gather(slot, nrows).start()
            return carry

        lax.fori_loop(0, PEER_NCH, chunk, 0)
        pltpu.make_async_copy(out_v.at[s], o_hbm.at[tok], osem.at[s]).start()
        return carry

    for cp in meta_copies(base, 0):
        cp.start()
    for cp in meta_copies(base, 0):
        cp.wait()
    for c in range(PEER_RING):
        gather(c, idx_v[0, pl.ds(c * PEER_CH, PEER_CH)]).start()
    lax.fori_loop(0, n_tok, token, 0)
    for c in range(PEER_RING):
        gather(c, zero_rows).wait()
    for s in range(2):
        pltpu.make_async_copy(out_v.at[s], o_hbm.at[base], osem.at[s]).wait()


PACK_ROWS = 512


def _pack_words(x):
    half = x.shape[1] // 2
    return _bf16_bits(x[:, :half]) | lax.shift_left(_bf16_bits(x[:, half:]), 16)


def _pack_tables_kernel(u_ref, v_ref, o_ref):
    o_ref[:, :PEER_WORDS] = _pack_words(u_ref[...])
    o_ref[:, PEER_WORDS:] = _pack_words(v_ref[...])


def _pack_tables(peer_u, peer_v):
    n = peer_u.shape[0]
    rows = min(PACK_ROWS, n)
    spec = pl.BlockSpec((rows, D_MODEL), lambda i: (i, 0))
    return pl.pallas_call(
        _pack_tables_kernel,
        grid=(n // rows,),
        in_specs=[spec, spec],
        out_specs=spec,
        out_shape=jax.ShapeDtypeStruct((n, D_MODEL), jnp.int32),
        compiler_params=pltpu.CompilerParams(
            dimension_semantics=("parallel",), vmem_limit_bytes=VMEM_LIMIT),
        name="pack_tables",
    )(peer_u, peer_v)


def _peer(idx, h_words, gates, uv_words, after):
    T = h_words.shape[0]
    assert T % (2 * SC_WORKERS) == 0
    mesh = plsc.VectorSubcoreMesh(core_axis_name="c", subcore_axis_name="s",
                                  num_cores=SC_CORES, num_subcores=SC_SUBCORES)
    return pl.kernel(
        _peer_sc_body,
        out_type=jax.ShapeDtypeStruct((T, D_MODEL), F32),
        mesh=mesh,
        scratch_types=[
            pltpu.VMEM((2, PEER_SEL), jnp.int32), pltpu.VMEM((2, PEER_SEL), F32),
            pltpu.VMEM((2, PEER_WORDS), jnp.int32),
            pltpu.VMEM((PEER_RING, PEER_CH, 2 * PEER_WORDS), jnp.int32),
            pltpu.VMEM((2, D_MODEL), F32),
            pltpu.SemaphoreType.DMA((PEER_RING,)),
            pltpu.SemaphoreType.DMA((2,)), pltpu.SemaphoreType.DMA((2,)),
        ],
        compiler_params=pltpu.CompilerParams(needs_layout_passes=False),
        name="peer_sc",
    )(idx, gates, h_words, uv_words, after)


FINAL_TS = 512


def _final_kernel(x1_ref, pe_ref, p_ref, gp_ref, wg_ref, wp_ref, gf_ref, o_ref):
    x2 = x1_ref[...] + pe_ref[...]
    e = _dot(p_ref[...].astype(BF16), wp_ref[...])
    gate = jax.nn.sigmoid(_dot(_rms(x2, gp_ref[...]).astype(BF16), wg_ref[...]))
    o_ref[...] = _rms(x2 + gate * e, gf_ref[...])


def _final(x1, peer_out, p, t0, nt, g_ple, ple_w_gate, ple_w_proj, g_final):
    B = p.shape[0]
    ts = min(FINAL_TS, nt)
    nblk = nt // ts
    i0 = t0 // ts
    row = lambda d: pl.BlockSpec((ts, d), lambda b, i: (b * nblk + i, 0))
    full = lambda shape: pl.BlockSpec(shape, lambda b, i: (0,) * len(shape))
    return pl.pallas_call(
        _final_kernel,
        grid=(B, nblk),
        in_specs=[row(D_MODEL), row(D_MODEL),
                  pl.BlockSpec((None, ts, D_PLE), lambda b, i: (b, i0 + i, 0)),
                  full((1, D_MODEL)), full((D_MODEL, D_MODEL)), full((D_PLE, D_MODEL)),
                  full((1, D_MODEL))],
        out_specs=pl.BlockSpec((None, ts, D_MODEL), lambda b, i: (b, i, 0)),
        out_shape=jax.ShapeDtypeStruct((B, nt, D_MODEL), F32),
        compiler_params=pltpu.CompilerParams(
            dimension_semantics=("parallel", "parallel"), vmem_limit_bytes=VMEM_LIMIT),
        name="final",
    )(x1, peer_out, p, g_ple, ple_w_gate, ple_w_proj, g_final)


CHUNK_STEPS = (512, 512, 512, 512, 512, 1024, 1024, 1024, 1024, 1024, 512)


def kernel(x, p, positions, g_mix, w_in, ssm_log_dt, ssm_a_re, ssm_a_im, ssm_b_re, ssm_b_im,
           ssm_c_re, ssm_c_im, ssm_d, ssm_w_glu, w_proj_ssm, w_proj_att, w_out, g_ffn,
           peer_w_q, peer_keys1, peer_keys2, peer_u, peer_v, g_ple, ple_w_gate, ple_w_proj,
           g_final):
    B, S, _ = x.shape
    assert w_in.shape[0] == 1, "the final rmsnorm is fused into the single layer's last stage"
    steps = CHUNK_STEPS if sum(CHUNK_STEPS) == S else (S,)
    i = 0
    tables = _s5_tables(ssm_log_dt[i], ssm_a_re[i], ssm_a_im[i], ssm_b_re[i], ssm_b_im[i],
                        ssm_c_re[i], ssm_c_im[i])
    w_in_b, w_glu_b = w_in[i].astype(BF16), ssm_w_glu[i].astype(BF16)
    d_skip = ssm_d[i].reshape(1, D_SSM).astype(F32)
    merge_w = (w_proj_ssm[i].astype(BF16), w_proj_att[i].astype(BF16), w_out[i].astype(BF16),
               g_ffn[i].reshape(1, D_MODEL), peer_w_q[i].astype(BF16), peer_keys1[i], peer_keys2[i])
    final_w = (g_ple[i].reshape(1, D_MODEL), ple_w_gate[i].astype(BF16),
               ple_w_proj[i].astype(BF16), g_final.reshape(1, D_MODEL))
    uv_words = _pack_tables(peer_u[i], peer_v[i])
    k_all = jnp.zeros((B, S, D_ATT), BF16)
    v_all = jnp.zeros((B, S, D_ATT), BF16)
    carry = jnp.zeros((2, SUBLANES, D_STATE), F32)
    outs = []
    t0 = 0
    after = (carry, carry)
    peers = []
    for nt in steps:
        u_sb, q, k, v, ga, gb = _in_proj(x, positions, g_mix[i], w_in_b, t0, nt, after)
        k_all = lax.dynamic_update_slice(k_all, k, (0, t0, 0))
        v_all = lax.dynamic_update_slice(v_all, v, (0, t0, 0))
        ys, carry = _s5(u_sb, carry, tables, d_skip, w_glu_b, B)
        att = _moba(q, k_all, v_all, t0 // MOBA_BLOCK)
        subs = ((0, nt // 2), (nt // 2, nt // 2)) if t0 == 0 and nt >= 2 * MERGE_TS else ((0, nt),)
        for off, n in subs:
            x1, h_words, idx, gates = _merge(x, ys, att, ga, gb, t0, off, n,
                                             peers[-2] if len(peers) > 1 else carry, *merge_w)
            peer_out = _peer(idx, h_words, gates, uv_words,
                             peers[-1] if t0 + off + n == S else carry)
            peers.append(peer_out)
            outs.append(_final(x1, peer_out, p[i], t0 + off, n, *final_w))
        after = (gates, outs[-3] if len(outs) > 2 else carry)
        t0 += nt
    return jnp.concatenate(outs, axis=1)
```

```python
import functools

import jax
import jax.numpy as jnp
from jax import lax
from jax.experimental import pallas as pl
from jax.experimental.pallas import tpu as pltpu
from jax.experimental.pallas import tpu_sc as plsc

F32 = jnp.float32
BF16 = jnp.bfloat16

D_MODEL = 1024
D_SSM = 512
SSM_GROUPS = 32
SSM_STATE = 64
D_STATE = SSM_GROUPS * SSM_STATE
HEAD_DIM = 64
D_ATT = 512
ROT_DIM = 16
ROPE_THETA = 500000.0
MOBA_BLOCK = 256
MOBA_TOPK = 3
PEER_HEADS = 8
PEER_KEYS = 128
PEER_QDIM = 256
PEER_HALF = 128
PEER_TOPK = 16
PEER_SEL = PEER_HEADS * PEER_TOPK
D_PLE = 256
EPS = 1e-6
NEG = -1e30
LANES = 128
SUBLANES = 8
VMEM_LIMIT = 48 * 1024 * 1024
HIGHEST = lax.Precision.HIGHEST


def _rms(x, g):
    return x * lax.rsqrt(jnp.mean(x * x, axis=-1, keepdims=True) + EPS) * g


def _dot(a, b):
    return jnp.dot(a, b, preferred_element_type=F32)


def _dot_nt(a, b, precision=None):
    return lax.dot_general(a, b, (((1,), (1,)), ((), ())), precision=precision,
                           preferred_element_type=F32)


IN_TS = 512


def _in_proj_kernel(x_ref, pos_ref, g_ref, w_ref, invf_ref, after_a, after_b,
                    u_ref, q_ref, k_ref, v_ref, ga_ref, gb_ref):
    del after_a, after_b
    h = _rms(x_ref[...], g_ref[...]).astype(BF16)

    def proj(lo, hi):
        return _dot(h, w_ref[:, lo:hi])

    u_ref[...] = proj(0, D_SSM).astype(BF16)
    ang = pos_ref[...].astype(F32) * invf_ref[...]
    cos = jnp.cos(ang)
    sin = jnp.sin(ang)
    lane = lax.broadcasted_iota(jnp.int32, (1, LANES), 1) % HEAD_DIM
    half = ROT_DIM // 2
    sin_hi = jnp.where((lane >= half) & (lane < ROT_DIM), sin, 0.0)
    sin_lo = jnp.where(lane < half, -sin, 0.0)
    reps = D_ATT // LANES
    cos4 = jnp.concatenate([cos] * reps, axis=1)
    sin_hi4 = jnp.concatenate([sin_hi] * reps, axis=1)
    sin_lo4 = jnp.concatenate([sin_lo] * reps, axis=1)

    def rope(t):
        return (t * cos4 + pltpu.roll(t, half, 1) * sin_hi4
                + pltpu.roll(t, D_ATT - half, 1) * sin_lo4)

    q = rope(proj(D_SSM, D_SSM + D_ATT))
    q_ref[...] = (q * (HEAD_DIM ** -0.5)).astype(BF16)
    k_ref[...] = rope(proj(D_SSM + D_ATT, D_SSM + 2 * D_ATT)).astype(BF16)
    v_ref[...] = proj(D_SSM + 2 * D_ATT, D_SSM + 3 * D_ATT).astype(BF16)
    o = D_SSM + 3 * D_ATT
    ga_ref[...] = jax.nn.sigmoid(proj(o, o + D_MODEL)).astype(BF16)
    gb_ref[...] = jax.nn.sigmoid(proj(o + D_MODEL, o + 2 * D_MODEL)).astype(BF16)


def _in_proj(x, positions, g_mix, w_in, t0, nt, after):
    B, S, _ = x.shape
    ts = min(IN_TS, nt)
    assert nt % ts == 0 and t0 % ts == 0
    i0 = t0 // ts
    inv_freq = ROPE_THETA ** (-jnp.arange(0, ROT_DIM, 2, dtype=F32) / ROT_DIM)
    lane = jnp.arange(LANES) % HEAD_DIM
    invf = jnp.where(lane < ROT_DIM, inv_freq[lane % (ROT_DIM // 2)], 0.0).reshape(1, LANES)
    d_in = w_in.shape[1]
    src = lambda d: pl.BlockSpec((None, ts, d), lambda b, i: (b, i0 + i, 0))
    tok = lambda d: pl.BlockSpec((None, ts, d), lambda b, i: (b, i, 0))
    full = lambda shape: pl.BlockSpec(shape, lambda b, i: (0,) * len(shape))
    return pl.pallas_call(
        _in_proj_kernel,
        grid=(B, nt // ts),
        in_specs=[src(D_MODEL), src(1), full((1, D_MODEL)), full((D_MODEL, d_in)), full((1, LANES)),
                  pl.BlockSpec(memory_space=pl.ANY), pl.BlockSpec(memory_space=pl.ANY)],
        out_specs=[pl.BlockSpec((ts, D_SSM), lambda b, i: (i, b)),
                   tok(D_ATT), tok(D_ATT), tok(D_ATT), tok(D_MODEL), tok(D_MODEL)],
        out_shape=[jax.ShapeDtypeStruct((nt, B * D_SSM), BF16),
                   jax.ShapeDtypeStruct((B, nt, D_ATT), BF16),
                   jax.ShapeDtypeStruct((B, nt, D_ATT), BF16),
                   jax.ShapeDtypeStruct((B, nt, D_ATT), BF16),
                   jax.ShapeDtypeStruct((B, nt, D_MODEL), BF16),
                   jax.ShapeDtypeStruct((B, nt, D_MODEL), BF16)],
        compiler_params=pltpu.CompilerParams(
            dimension_semantics=("parallel", "parallel"), vmem_limit_bytes=VMEM_LIMIT),
        name="in_proj",
    )(x, positions.reshape(B, S, 1), g_mix.reshape(1, D_MODEL), w_in, invf, *after)


S5_TS = 128
S5_BATCH = 4
S5_COLS = 512


def _s5_kernel(u_ref, c0_ref, bre_ref, bim_ref, a1r_ref, a1i_ref, pr_ref, pi_ref,
               cre_ref, cim_ref, d_ref, wglu_ref, y_ref, c1_ref,
               xr, xi, cr, ci, ysc):
    rows = xr.shape[0]
    ts = rows // S5_BATCH

    @pl.when(pl.program_id(0) == 0)
    def _():
        cr[...] = c0_ref[0]
        ci[...] = c0_ref[1]

    u = u_ref[...]
    for cb in range(D_STATE // S5_COLS):
        sl = slice(cb * S5_COLS, (cb + 1) * S5_COLS)
        u_cb = u[:, cb * LANES:(cb + 1) * LANES]
        xr[:, sl] = _dot(u_cb, bre_ref[cb])
        xi[:, sl] = _dot(u_cb, bim_ref[cb])

    hi_rows = lax.broadcasted_iota(jnp.int32, (SUBLANES, S5_COLS), 0) >= S5_BATCH
    for cb in range(D_STATE // S5_COLS):
        sl = slice(cb * S5_COLS, (cb + 1) * S5_COLS)
        a_r, a_i = a1r_ref[:, sl], a1i_ref[:, sl]
        p_r, p_i = pr_ref[:, sl], pi_ref[:, sl]

        def body(t, carry):
            c_r, c_i = carry
            r0 = pl.multiple_of(t * SUBLANES, SUBLANES)
            x_r = xr[pl.ds(r0, SUBLANES), sl]
            x_i = xi[pl.ds(r0, SUBLANES), sl]
            s_r = pltpu.roll(x_r, S5_BATCH, 0)
            s_i = pltpu.roll(x_i, S5_BATCH, 0)
            h_r = x_r + (a_r * s_r - a_i * s_i) + (p_r * c_r - p_i * c_i)
            h_i = x_i + (a_r * s_i + a_i * s_r) + (p_r * c_i + p_i * c_r)
            xr[pl.ds(r0, SUBLANES), sl] = h_r
            xi[pl.ds(r0, SUBLANES), sl] = h_i
            n_r = jnp.where(hi_rows, h_r, pltpu.roll(h_r, S5_BATCH, 0))
            n_i = jnp.where(hi_rows, h_i, pltpu.roll(h_i, S5_BATCH, 0))
            return n_r, n_i

        c_r, c_i = lax.fori_loop(0, rows // SUBLANES, body, (cr[:, sl], ci[:, sl]), unroll=2)
        cr[:, sl] = c_r
        ci[:, sl] = c_i

    y = jnp.concatenate(
        [_dot(xr[:, cb * S5_COLS:(cb + 1) * S5_COLS].astype(BF16), cre_ref[cb])
         - _dot(xi[:, cb * S5_COLS:(cb + 1) * S5_COLS].astype(BF16), cim_ref[cb])
         for cb in range(D_STATE // S5_COLS)], axis=1) + d_ref[...] * u.astype(F32)
    y = jax.nn.gelu(y)
    y = y * jax.nn.sigmoid(_dot(y.astype(BF16), wglu_ref[...]))
    for c in range(D_SSM // LANES):
        ysc[c] = y[:, c * LANES:(c + 1) * LANES]
    for b in range(S5_BATCH):
        for c in range(D_SSM // LANES):
            y_ref[b, :, c * LANES:(c + 1) * LANES] = (
                ysc[c, pl.ds(b, ts, stride=S5_BATCH), :].astype(BF16))

    @pl.when(pl.program_id(0) == pl.num_programs(0) - 1)
    def _():
        c1_ref[0] = cr[...]
        c1_ref[1] = ci[...]


def _s5_tables(log_dt, a_re, a_im, b_re, b_im, c_re, c_im):
    dt = jnp.exp(log_dt.astype(F32))[:, None]
    ar, ai = a_re.astype(F32), a_im.astype(F32)
    mag = jnp.exp(dt * ar)
    abar_re, abar_im = mag * jnp.cos(dt * ai), mag * jnp.sin(dt * ai)
    den = ar * ar + ai * ai
    nr, ni = abar_re - 1.0, abar_im
    f_re = (nr * ar + ni * ai) / den
    f_im = (ni * ar - nr * ai) / den
    br, bi = b_re.astype(F32), b_im.astype(F32)
    bb_re = f_re[..., None] * br - f_im[..., None] * bi
    bb_im = f_re[..., None] * bi + f_im[..., None] * br
    nblk = D_STATE // S5_COLS
    gpb = SSM_GROUPS // nblk
    eye = jnp.eye(gpb, dtype=F32)

    def in_blocks(bb):
        b4 = bb.reshape(nblk, gpb, SSM_STATE, -1)
        return jnp.einsum('bgnc,gh->bgchn', b4, eye).reshape(nblk, D_SSM // nblk, S5_COLS)

    def out_blocks(c):
        c4 = c.astype(F32).reshape(nblk, gpb, -1, SSM_STATE)
        return jnp.einsum('bgcn,gh->bgnhc', c4, eye).reshape(nblk, S5_COLS, D_SSM // nblk)

    a_r = abar_re.reshape(1, D_STATE)
    a_i = abar_im.reshape(1, D_STATE)
    a2_r = a_r * a_r - a_i * a_i
    a2_i = 2.0 * a_r * a_i
    hi = (jnp.arange(SUBLANES) >= S5_BATCH)[:, None]
    a1r = jnp.where(hi, a_r, 0.0)
    a1i = jnp.where(hi, a_i, 0.0)
    p_r = jnp.where(hi, a2_r, a_r)
    p_i = jnp.where(hi, a2_i, a_i)
    return (in_blocks(bb_re).astype(BF16), in_blocks(bb_im).astype(BF16),
            a1r, a1i, p_r, p_i,
            out_blocks(c_re).astype(BF16), out_blocks(c_im).astype(BF16))


def _s5(u_sb, carry, tables, d_skip, w_glu, B):
    assert B == S5_BATCH
    nt = u_sb.shape[0]
    ts = min(S5_TS, nt)
    rows = ts * B
    bre, bim, a1r, a1i, p_r, p_i, cre, cim = tables
    full = lambda shape: pl.BlockSpec(shape, lambda i: (0,) * len(shape))
    return pl.pallas_call(
        _s5_kernel,
        grid=(nt // ts,),
        in_specs=[pl.BlockSpec((rows, D_SSM), lambda i: (i, 0)),
                  full((2, SUBLANES, D_STATE)),
                  full(bre.shape), full(bim.shape),
                  full((SUBLANES, D_STATE)), full((SUBLANES, D_STATE)),
                  full((SUBLANES, D_STATE)), full((SUBLANES, D_STATE)),
                  full(cre.shape), full(cim.shape),
                  full((1, D_SSM)), full((D_SSM, D_SSM))],
        out_specs=[pl.BlockSpec((B, ts, D_SSM), lambda i: (0, i, 0)),
                   full((2, SUBLANES, D_STATE))],
        out_shape=[jax.ShapeDtypeStruct((B, nt, D_SSM), BF16),
                   jax.ShapeDtypeStruct((2, SUBLANES, D_STATE), F32)],
        scratch_shapes=[pltpu.VMEM((rows, D_STATE), F32), pltpu.VMEM((rows, D_STATE), F32),
                        pltpu.VMEM((SUBLANES, D_STATE), F32), pltpu.VMEM((SUBLANES, D_STATE), F32),
                        pltpu.VMEM((D_SSM // LANES, rows, LANES), F32)],
        compiler_params=pltpu.CompilerParams(
            dimension_semantics=("arbitrary",), vmem_limit_bytes=VMEM_LIMIT),
        name="s5",
    )(u_sb.reshape(nt * B, D_SSM), carry, bre, bim, a1r, a1i, p_r, p_i, cre, cim, d_skip, w_glu)


MOBA_PAIR = 2 * MOBA_BLOCK


def _moba_kernel(q0, q_ref, k_ref, v_ref, o_ref, kmean, kaug_a, kaug_b, vaug_a, vaug_b, qaug,
                 m_s, acc_s, s_buf):
    last = pl.program_id(2) + q0 // 2
    nb = k_ref.shape[0] // MOBA_BLOCK
    nbp = kmean.shape[0]
    lane = lax.broadcasted_iota(jnp.int32, (1, LANES), 1)
    head_a = lane < HEAD_DIM

    @pl.when(pl.program_id(2) == 0)
    def _():
        kmean[...] = jnp.zeros_like(kmean)
        for j in range(nb):
            rows = pl.ds(j * MOBA_BLOCK, MOBA_BLOCK)
            kj = k_ref[rows, :].astype(F32)
            vj = v_ref[rows, :].astype(F32)
            kmean[j:j + 1, :] = jnp.sum(kj, axis=0, keepdims=True) * (1.0 / MOBA_BLOCK)
            kaug_a[rows, :] = jnp.where(head_a, kj, jnp.where(lane - HEAD_DIM == j, 1.0, 0.0)).astype(BF16)
            kaug_b[rows, :] = jnp.where(head_a, jnp.where(lane == j, 1.0, 0.0), kj).astype(BF16)
            vaug_a[rows, :] = jnp.where(head_a, vj, 1.0).astype(BF16)
            vaug_b[rows, :] = jnp.where(head_a, 1.0, vj).astype(BF16)
        blk_row = lax.broadcasted_iota(jnp.int32, (nbp, MOBA_BLOCK), 0)
        for t in range(q_ref.shape[0] // MOBA_BLOCK):
            qt = q0 + t
            qf = q_ref[t * MOBA_BLOCK:(t + 1) * MOBA_BLOCK, :].astype(F32)
            for hd, is_a in enumerate((True, False)):
                mine = head_a if is_a else jnp.logical_not(head_a)
                q_own = jnp.where(mine, qf, 0.0)
                g = _dot_nt(kmean[...], q_own, precision=HIGHEST)
                g = jnp.where(blk_row < qt, g, NEG)
                sel = jnp.zeros(g.shape, F32)
                for _ in range(MOBA_TOPK):
                    m = jnp.max(g, axis=0, keepdims=True)
                    idx = jnp.min(jnp.where(g == m, blk_row, nbp), axis=0, keepdims=True)
                    hit = blk_row == idx
                    sel = jnp.where(hit, jnp.where(idx < qt, 1.0, 0.0), sel)
                    g = jnp.where(hit, -jnp.inf, g)
                bias_t = jnp.where(sel > 0.0, 0.0, jnp.where(blk_row == qt, 0.0, NEG))
                bias_t = jnp.concatenate([bias_t, jnp.full((LANES - nbp, MOBA_BLOCK), NEG, F32)], axis=0)
                bias = jnp.transpose(bias_t)
                if is_a:
                    bias = pltpu.roll(bias, HEAD_DIM, 1)
                qaug[hd, t * MOBA_BLOCK:(t + 1) * MOBA_BLOCK, :] = jnp.where(mine, qf, bias).astype(BF16)

    tile_rows = pl.ds(pl.multiple_of(pl.program_id(2) * MOBA_PAIR, MOBA_PAIR), MOBA_PAIR)
    q_augs = [qaug[0, tile_rows, :], qaug[1, tile_rows, :]]

    m_s[...] = jnp.full(m_s.shape, -jnp.inf, F32)
    acc_s[...] = jnp.zeros_like(acc_s)
    qpos = last * MOBA_PAIR + lax.broadcasted_iota(jnp.int32, (MOBA_PAIR, MOBA_PAIR), 0)
    col = lax.broadcasted_iota(jnp.int32, (MOBA_PAIR, MOBA_PAIR), 1)

    def kv_rows(jj):
        return pl.ds(pl.multiple_of(jj * MOBA_PAIR, MOBA_PAIR), MOBA_PAIR)

    def scores(jj, slot):
        for hd, kaug in enumerate((kaug_a, kaug_b)):
            s_buf[slot, hd] = _dot_nt(q_augs[hd], kaug[kv_rows(jj), :])

    def softmax_pv(jj, slot, causal):
        for hd, vaug in enumerate((vaug_a, vaug_b)):
            s = s_buf[slot, hd]
            if causal:
                s = jnp.where(jj * MOBA_PAIR + col <= qpos, s, NEG)
            m_old = m_s[hd]
            m_new = jnp.maximum(m_old, jnp.max(s, axis=-1, keepdims=True))
            alpha = jnp.exp(m_old - m_new)
            p = jnp.exp(s - m_new)
            m_s[hd] = m_new
            acc_s[hd] = alpha * acc_s[hd] + _dot(p.astype(BF16), vaug[kv_rows(jj), :])

    scores(0, 0)

    def body(k, _):
        scores(2 * k + 1, 1)
        softmax_pv(2 * k, 0, False)
        scores(2 * k + 2, 0)
        softmax_pv(2 * k + 1, 1, False)
        return 0

    lax.fori_loop(0, last // 2, body, 0)

    @pl.when(last % 2 == 0)
    def _():
        softmax_pv(last, 0, True)

    @pl.when(last % 2 == 1)
    def _():
        scores(last, 1)
        softmax_pv(last - 1, 0, False)
        softmax_pv(last, 1, True)
    acc_a, acc_b = acc_s[0], acc_s[1]
    o_ref[...] = jnp.where(head_a, acc_a / pltpu.roll(acc_a, HEAD_DIM, 1),
                           acc_b / pltpu.roll(acc_b, HEAD_DIM, 1)).astype(BF16)


def _moba(q, k, v, q0):
    B = q.shape[0]
    nq = q.shape[1] // MOBA_BLOCK
    skv = (q0 + nq) * MOBA_BLOCK
    nb = skv // MOBA_BLOCK
    assert nb <= HEAD_DIM and nb % 2 == 0 and skv <= k.shape[1]
    nbp = -(-nb // SUBLANES) * SUBLANES
    assert q0 % 2 == 0 and nq % 2 == 0
    blk = pl.BlockSpec((None, MOBA_PAIR, LANES), lambda b, h, i: (b, i, h))
    seq = pl.BlockSpec((None, skv, LANES), lambda b, h, i: (b, 0, h))
    return pl.pallas_call(
        functools.partial(_moba_kernel, q0),
        grid=(B, D_ATT // LANES, nq // 2),
        in_specs=[pl.BlockSpec((None, nq * MOBA_BLOCK, LANES), lambda b, h, i: (b, 0, h)), seq, seq],
        out_specs=blk,
        out_shape=jax.ShapeDtypeStruct(q.shape, BF16),
        scratch_shapes=[pltpu.VMEM((nbp, LANES), F32),
                        pltpu.VMEM((skv, LANES), BF16), pltpu.VMEM((skv, LANES), BF16),
                        pltpu.VMEM((skv, LANES), BF16), pltpu.VMEM((skv, LANES), BF16),
                        pltpu.VMEM((2, nq * MOBA_BLOCK, LANES), BF16),
                        pltpu.VMEM((2, MOBA_PAIR, 1), F32),
                        pltpu.VMEM((2, MOBA_PAIR, LANES), F32),
                        pltpu.VMEM((2, 2, MOBA_PAIR, MOBA_PAIR), F32)],
        compiler_params=pltpu.CompilerParams(
            dimension_semantics=("parallel", "parallel", "arbitrary"), vmem_limit_bytes=VMEM_LIMIT),
        name="moba",
    )(q, k, v)


MERGE_TS = 256


def _bf16_bits(x):
    b = pltpu.bitcast(x, jnp.int32)
    r = b + 0x7FFF + (lax.shift_right_logical(b, 16) & 1)
    return lax.shift_right_logical(r, 16)


def _merge_kernel(x_ref, ys_ref, at_ref, ga_ref, gb_ref, wa_ref, wb_ref, wo_ref, g_ref,
                  wq_ref, k1_ref, k2_ref, after_ref, x1_ref, hw_ref, idx_ref, gate_ref, sc_ref):
    del after_ref
    ya = _dot(ys_ref[...], wa_ref[...])
    yb = _dot(at_ref[...], wb_ref[...])
    merged = ga_ref[...].astype(F32) * ya + gb_ref[...].astype(F32) * yb
    x1 = x_ref[...] + _dot(merged.astype(BF16), wo_ref[...])
    x1_ref[...] = x1
    hq = _rms(x1, g_ref[...])
    hw_ref[...] = _pack_words(hq)
    qp = _dot(hq.astype(BF16), wq_ref[...])
    for h in range(PEER_HEADS):
        o = h * PEER_QDIM
        sc_ref[2 * h] = _dot_nt(k1_ref[h], qp[:, o:o + PEER_HALF], precision=HIGHEST)
        sc_ref[2 * h + 1] = _dot_nt(k2_ref[h], qp[:, o + PEER_HALF:o + PEER_QDIM], precision=HIGHEST)
    _topk_kernel(sc_ref, idx_ref, gate_ref)


def _merge(x, ys, att, ga, gb, t0, off, nt, after, w_proj_ssm, w_proj_att, w_out, g_ffn, peer_w_q,
           keys1, keys2):
    B = ys.shape[0]
    ts = min(MERGE_TS, nt)
    nblk = nt // ts
    i0 = (t0 + off) // ts
    o0 = off // ts
    tok = lambda d: pl.BlockSpec((None, ts, d), lambda b, i: (b, o0 + i, 0))
    row = lambda d: pl.BlockSpec((ts, d), lambda b, i: (b * nblk + i, 0))
    full = lambda shape: pl.BlockSpec(shape, lambda b, i: (0,) * len(shape))
    qd = PEER_HEADS * PEER_QDIM
    return pl.pallas_call(
        _merge_kernel,
        grid=(B, nblk),
        in_specs=[pl.BlockSpec((None, ts, D_MODEL), lambda b, i: (b, i0 + i, 0)),
                  tok(D_SSM), tok(D_ATT), tok(D_MODEL), tok(D_MODEL),
                  full((D_SSM, D_MODEL)), full((D_ATT, D_MODEL)), full((D_MODEL, D_MODEL)),
                  full((1, D_MODEL)), full((D_MODEL, qd)),
                  full((PEER_HEADS, PEER_KEYS, PEER_HALF)), full((PEER_HEADS, PEER_KEYS, PEER_HALF)),
                  pl.BlockSpec(memory_space=pl.ANY)],
        out_specs=[row(D_MODEL), row(D_MODEL // 2), row(PEER_SEL), row(PEER_SEL)],
        out_shape=[jax.ShapeDtypeStruct((B * nt, D_MODEL), F32),
                   jax.ShapeDtypeStruct((B * nt, D_MODEL // 2), jnp.int32),
                   jax.ShapeDtypeStruct((B * nt, PEER_SEL), jnp.int32),
                   jax.ShapeDtypeStruct((B * nt, PEER_SEL), F32)],
        scratch_shapes=[pltpu.VMEM((2 * PEER_HEADS, PEER_KEYS, ts), F32)],
        compiler_params=pltpu.CompilerParams(
            dimension_semantics=("parallel", "parallel"), vmem_limit_bytes=VMEM_LIMIT),
        name="merge",
    )(x, ys, att, ga, gb, w_proj_ssm, w_proj_att, w_out, g_ffn, peer_w_q, keys1, keys2, after)


def _top_rows(s, row, k):
    vals, idxs = [], []
    for _ in range(k):
        m = jnp.max(s, axis=0, keepdims=True)
        idx = jnp.min(jnp.where(s == m, row, s.shape[0]), axis=0, keepdims=True)
        vals.append(m)
        idxs.append(idx)
        s = jnp.where(row == idx, -jnp.inf, s)
    return vals, idxs


def _stack_rows(rows, row16):
    acc = jnp.zeros(row16.shape, rows[0].dtype)
    for r, v in enumerate(rows):
        acc = jnp.where(row16 == r, v, acc)
    return acc


def _topk_kernel(sc_ref, idx_ref, gate_ref):
    ts = sc_ref.shape[-1]
    row = lax.broadcasted_iota(jnp.int32, (PEER_KEYS, ts), 0).astype(F32)
    row16 = lax.broadcasted_iota(jnp.int32, (PEER_TOPK, ts), 0)
    row8 = lax.broadcasted_iota(jnp.int32, (SUBLANES, ts), 0)
    counts = [PEER_TOPK // (i + 1) for i in range(PEER_TOPK)]
    heights = [PEER_TOPK if c > SUBLANES else SUBLANES for c in counts]
    n_cand = sum(heights)
    rowc = lax.broadcasted_iota(jnp.int32, (n_cand, ts), 0).astype(F32)
    gate_rows, eid_rows = [], []
    for h in range(PEER_HEADS):
        v1, i1 = _top_rows(sc_ref[2 * h], row, PEER_TOPK)
        v2, i2 = _top_rows(sc_ref[2 * h + 1], row, PEER_TOPK)
        v2s = _stack_rows(v2, row16)
        i2s = _stack_rows(i2, row16)
        cand, eid = [], []
        for i in range(PEER_TOPK):
            n = heights[i]
            cand.append(jnp.where((row16 if n == PEER_TOPK else row8) < counts[i],
                                  v1[i] + v2s[:n], -jnp.inf))
            eid.append(i1[i] * PEER_KEYS + i2s[:n])
        cand = jnp.concatenate(cand, axis=0)
        eid = jnp.concatenate(eid, axis=0)
        tops, picks = [], []
        for _ in range(PEER_TOPK):
            m = jnp.max(cand, axis=0, keepdims=True)
            pos = jnp.min(jnp.where(cand == m, rowc, n_cand), axis=0, keepdims=True)
            hit = rowc == pos
            picks.append(jnp.max(jnp.where(hit, eid, -1.0), axis=0, keepdims=True))
            tops.append(m)
            cand = jnp.where(hit, -jnp.inf, cand)
        top = _stack_rows(tops, row16)
        p = jnp.exp(top - jnp.max(top, axis=0, keepdims=True))
        gate_rows.append(p / jnp.sum(p, axis=0, keepdims=True))
        eid_rows.append(_stack_rows(picks, row16))
    gate_ref[...] = jnp.transpose(jnp.concatenate(gate_rows, axis=0))
    idx_ref[...] = jnp.transpose(jnp.concatenate(eid_rows, axis=0)).astype(jnp.int32)


SC_CORES = 2
SC_SUBCORES = 16
SC_LANES = 16
SC_WORKERS = SC_CORES * SC_SUBCORES
PEER_CH = SC_LANES
PEER_NCH = PEER_SEL // PEER_CH
PEER_WORDS = D_MODEL // 2
PEER_NWG = PEER_WORDS // SC_LANES
PEER_RING = 4
PEER_QUAD = 4
HI_MASK = -65536
GELU_C = 0.7978845608028654


def _gelu_tanh_via_exp(x):
    z = GELU_C * (x + 0.044715 * (x * x * x))
    t = 1.0 - 2.0 / (jnp.exp(2.0 * z) + 1.0)
    return 0.5 * x * (1.0 + t)


def _unpack_pair(w):
    lo = plsc.bitcast(lax.shift_left(w, 16), F32)
    hi = plsc.bitcast(lax.bitwise_and(w, HI_MASK), F32)
    return lo, hi


def _peer_sc_body(idx_hbm, gate_hbm, h_hbm, uv_hbm, after_hbm, o_hbm,
                  idx_v, gate_v, h_v, buf, out_v, gsem, msem, osem):
    n_tok = o_hbm.shape[0] // SC_WORKERS
    base = (lax.axis_index("s") * SC_CORES + lax.axis_index("c")) * n_tok
    lane = lax.iota(jnp.int32, SC_LANES)
    zero_rows = jnp.zeros((SC_LANES,), jnp.int32)

    def meta_copies(tok, s):
        return (pltpu.make_async_copy(idx_hbm.at[tok], idx_v.at[s], msem.at[s]),
                pltpu.make_async_copy(gate_hbm.at[tok], gate_v.at[s], msem.at[s]),
                pltpu.make_async_copy(h_hbm.at[tok], h_v.at[s], msem.at[s]))

    def gather(slot, rows):
        return pltpu.make_async_copy(uv_hbm.at[rows], buf.at[slot], gsem.at[slot])

    def token(t, carry):
        s = t % 2
        tok = base + t
        nxt = base + jnp.minimum(t + 1, n_tok - 1)
        for cp in meta_copies(nxt, 1 - s):
            cp.start()

        @pl.when(t >= 2)
        def _():
            pltpu.make_async_copy(out_v.at[s], o_hbm.at[tok], osem.at[s]).wait()

        def chunk(c, carry):
            slot = c % PEER_RING
            gather(slot, zero_rows).wait()

            def dot_step(q, accs):
                cols = [pl.ds(pl.multiple_of((q * PEER_QUAD + j) * SC_LANES, SC_LANES), SC_LANES)
                        for j in range(PEER_QUAD)]
                hs = [plsc.bitcast(h_v[s, col], BF16) for col in cols]
                out = []
                for r in range(PEER_CH):
                    p = plsc.bitcast(buf[slot, r, cols[0]], BF16) * hs[0]
                    for j in range(1, PEER_QUAD):
                        p = p + plsc.bitcast(buf[slot, r, cols[j]], BF16) * hs[j]
                    lo, hi = _unpack_pair(plsc.bitcast(p, jnp.int32))
                    out.append(accs[r] + lo + hi)
                return tuple(out)

            accs = lax.fori_loop(0, PEER_NWG // PEER_QUAD, dot_step,
                                 tuple(jnp.zeros((SC_LANES,), F32) for _ in range(PEER_CH)))
            tot = jnp.zeros((SC_LANES,), F32)
            for r in range(PEER_CH):
                tot = jnp.where(lane == r, jnp.sum(accs[r]), tot)
            rows = pl.ds(pl.multiple_of(c * PEER_CH, PEER_CH), PEER_CH)
            wvec = gate_v[s, rows] * _gelu_tanh_via_exp(tot)
            ws = []
            for r in range(PEER_CH):
                w = wvec.at[jnp.full((SC_LANES,), r, jnp.int32)].get(mode="promise_in_bounds")
                ws.append(plsc.pack(w, w, format=plsc.PackFormat.INTERLEAVED,
                                    preferred_element_type=BF16))
            first = c == 0

            @plsc.parallel_loop(0, PEER_NWG, unroll=2)
            def acc_step(g):
                col = pl.ds(pl.multiple_of(g * SC_LANES, SC_LANES), SC_LANES)
                col_v = pl.ds(pl.multiple_of(PEER_WORDS + g * SC_LANES, SC_LANES), SC_LANES)
                o_lo = jnp.where(first, 0.0, out_v[s, col])
                o_hi = jnp.where(first, 0.0, out_v[s, col_v])
                for r0 in range(0, PEER_CH, PEER_QUAD):
                    p = plsc.bitcast(buf[slot, r0, col_v], BF16) * ws[r0]
                    for r in range(r0 + 1, r0 + PEER_QUAD):
                        p = p + plsc.bitcast(buf[slot, r, col_v], BF16) * ws[r]
                    lo, hi = _unpack_pair(plsc.bitcast(p, jnp.int32))
                    o_lo = o_lo + lo
                    o_hi = o_hi + hi
                out_v[s, col] = o_lo
                out_v[s, col_v] = o_hi

            @pl.when(c == PEER_NCH - PEER_RING)
            def _():
                for cp in meta_copies(nxt, 1 - s):
                    cp.wait()

            ahead = c + PEER_RING
            src = jnp.where(ahead < PEER_NCH, s, 1 - s)
            nrows = idx_v[src, pl.ds(pl.multiple_of((ahead % PEER_NCH) * PEER_CH, PEER_CH), PEER_CH)]
            gather(slot, nrows).start()
            return carry

        lax.fori_loop(0, PEER_NCH, chunk, 0)
        pltpu.make_async_copy(out_v.at[s], o_hbm.at[tok], osem.at[s]).start()
        return carry

    for cp in meta_copies(base, 0):
        cp.start()
    for cp in meta_copies(base, 0):
        cp.wait()
    for c in range(PEER_RING):
        gather(c, idx_v[0, pl.ds(c * PEER_CH, PEER_CH)]).start()
    lax.fori_loop(0, n_tok, token, 0)
    for c in range(PEER_RING):
        gather(c, zero_rows).wait()
    for s in range(2):
        pltpu.make_async_copy(out_v.at[s], o_hbm.at[base], osem.at[s]).wait()


PACK_ROWS = 1024


def _pack_words(x):
    half = x.shape[1] // 2
    return _bf16_bits(x[:, :half]) | lax.shift_left(_bf16_bits(x[:, half:]), 16)


def _pack_tables_kernel(u_ref, v_ref, o_ref):
    o_ref[:, :PEER_WORDS] = _pack_words(u_ref[...])
    o_ref[:, PEER_WORDS:] = _pack_words(v_ref[...])


def _pack_tables(peer_u, peer_v):
    n = peer_u.shape[0]
    rows = min(PACK_ROWS, n)
    spec = pl.BlockSpec((rows, D_MODEL), lambda i: (i, 0))
    return pl.pallas_call(
        _pack_tables_kernel,
        grid=(n // rows,),
        in_specs=[spec, spec],
        out_specs=spec,
        out_shape=jax.ShapeDtypeStruct((n, D_MODEL), jnp.int32),
        compiler_params=pltpu.CompilerParams(
            dimension_semantics=("parallel",), vmem_limit_bytes=VMEM_LIMIT),
        name="pack_tables",
    )(peer_u, peer_v)


def _peer(idx, h_words, gates, uv_words, after):
    T = h_words.shape[0]
    assert T % (2 * SC_WORKERS) == 0
    mesh = plsc.VectorSubcoreMesh(core_axis_name="c", subcore_axis_name="s",
                                  num_cores=SC_CORES, num_subcores=SC_SUBCORES)
    return pl.kernel(
        _peer_sc_body,
        out_type=jax.ShapeDtypeStruct((T, D_MODEL), F32),
        mesh=mesh,
        scratch_types=[
            pltpu.VMEM((2, PEER_SEL), jnp.int32), pltpu.VMEM((2, PEER_SEL), F32),
            pltpu.VMEM((2, PEER_WORDS), jnp.int32),
            pltpu.VMEM((PEER_RING, PEER_CH, 2 * PEER_WORDS), jnp.int32),
            pltpu.VMEM((2, D_MODEL), F32),
            pltpu.SemaphoreType.DMA((PEER_RING,)),
            pltpu.SemaphoreType.DMA((2,)), pltpu.SemaphoreType.DMA((2,)),
        ],
        compiler_params=pltpu.CompilerParams(needs_layout_passes=False),
        name="peer_sc",
    )(idx, gates, h_words, uv_words, after)


FINAL_TS = 512


def _final_kernel(x1_ref, pe_ref, p_ref, gp_ref, wg_ref, wp_ref, gf_ref, o_ref):
    x2 = x1_ref[...] + pe_ref[...]
    e = _dot(p_ref[...].astype(BF16), wp_ref[...])
    gate = jax.nn.sigmoid(_dot(_rms(x2, gp_ref[...]).astype(BF16), wg_ref[...]))
    o_ref[...] = _rms(x2 + gate * e, gf_ref[...])


def _final(x1, peer_out, p, t0, nt, g_ple, ple_w_gate, ple_w_proj, g_final):
    B = p.shape[0]
    ts = min(FINAL_TS, nt)
    nblk = nt // ts
    i0 = t0 // ts
    row = lambda d: pl.BlockSpec((ts, d), lambda b, i: (b * nblk + i, 0))
    full = lambda shape: pl.BlockSpec(shape, lambda b, i: (0,) * len(shape))
    return pl.pallas_call(
        _final_kernel,
        grid=(B, nblk),
        in_specs=[row(D_MODEL), row(D_MODEL),
                  pl.BlockSpec((None, ts, D_PLE), lambda b, i: (b, i0 + i, 0)),
                  full((1, D_MODEL)), full((D_MODEL, D_MODEL)), full((D_PLE, D_MODEL)),
                  full((1, D_MODEL))],
        out_specs=pl.BlockSpec((None, ts, D_MODEL), lambda b, i: (b, i, 0)),
        out_shape=jax.ShapeDtypeStruct((B, nt, D_MODEL), F32),
        compiler_params=pltpu.CompilerParams(
            dimension_semantics=("parallel", "parallel"), vmem_limit_bytes=VMEM_LIMIT),
        name="final",
    )(x1, peer_out, p, g_ple, ple_w_gate, ple_w_proj, g_final)


CHUNK_STEPS = (512, 512, 512, 512, 512, 1024, 1024, 1024, 1024, 1024, 512)


def kernel(x, p, positions, g_mix, w_in, ssm_log_dt, ssm_a_re, ssm_a_im, ssm_b_re, ssm_b_im,
           ssm_c_re, ssm_c_im, ssm_d, ssm_w_glu, w_proj_ssm, w_proj_att, w_out, g_ffn,
           peer_w_q, peer_keys1, peer_keys2, peer_u, peer_v, g_ple, ple_w_gate, ple_w_proj,
           g_final):
    B, S, _ = x.shape
    assert w_in.shape[0] == 1, "the final rmsnorm is fused into the single layer's last stage"
    steps = CHUNK_STEPS if sum(CHUNK_STEPS) == S else (S,)
    i = 0
    tables = _s5_tables(ssm_log_dt[i], ssm_a_re[i], ssm_a_im[i], ssm_b_re[i], ssm_b_im[i],
                        ssm_c_re[i], ssm_c_im[i])
    w_in_b, w_glu_b = w_in[i].astype(BF16), ssm_w_glu[i].astype(BF16)
    d_skip = ssm_d[i].reshape(1, D_SSM).astype(F32)
    merge_w = (w_proj_ssm[i].astype(BF16), w_proj_att[i].astype(BF16), w_out[i].astype(BF16),
               g_ffn[i].reshape(1, D_MODEL), peer_w_q[i].astype(BF16), peer_keys1[i], peer_keys2[i])
    final_w = (g_ple[i].reshape(1, D_MODEL), ple_w_gate[i].astype(BF16),
               ple_w_proj[i].astype(BF16), g_final.reshape(1, D_MODEL))
    uv_words = _pack_tables(peer_u[i], peer_v[i])
    k_all = jnp.zeros((B, S, D_ATT), BF16)
    v_all = jnp.zeros((B, S, D_ATT), BF16)
    carry = jnp.zeros((2, SUBLANES, D_STATE), F32)
    outs = []
    t0 = 0
    after = (carry, carry)
    peers = []
    for nt in steps:
        u_sb, q, k, v, ga, gb = _in_proj(x, positions, g_mix[i], w_in_b, t0, nt, after)
        k_all = lax.dynamic_update_slice(k_all, k, (0, t0, 0))
        v_all = lax.dynamic_update_slice(v_all, v, (0, t0, 0))
        ys, carry = _s5(u_sb, carry, tables, d_skip, w_glu_b, B)
        att = _moba(q, k_all, v_all, t0 // MOBA_BLOCK)
        subs = ((0, nt // 2), (nt // 2, nt // 2)) if t0 == 0 and nt >= 2 * MERGE_TS else ((0, nt),)
        for off, n in subs:
            x1, h_words, idx, gates = _merge(x, ys, att, ga, gb, t0, off, n,
                                             peers[-2] if len(peers) > 1 else carry, *merge_w)
            peer_out = _peer(idx, h_words, gates, uv_words,
                             peers[-1] if t0 + off + n == S else carry)
            peers.append(peer_out)
            outs.append(_final(x1, peer_out, p[i], t0 + off, n, *final_w))
        after = (gates, outs[-3] if len(outs) > 2 else carry)
        t0 += nt
    return jnp.concatenate(outs, axis=1)
```

```python
import functools

import jax
import jax.numpy as jnp
from jax import lax
from jax.experimental import pallas as pl
from jax.experimental.pallas import tpu as pltpu
from jax.experimental.pallas import tpu_sc as plsc

F32 = jnp.float32
BF16 = jnp.bfloat16

D_MODEL = 1024
D_SSM = 512
SSM_GROUPS = 32
SSM_STATE = 64
D_STATE = SSM_GROUPS * SSM_STATE
HEAD_DIM = 64
D_ATT = 512
ROT_DIM = 16
ROPE_THETA = 500000.0
MOBA_BLOCK = 256
MOBA_TOPK = 3
PEER_HEADS = 8
PEER_KEYS = 128
PEER_QDIM = 256
PEER_HALF = 128
PEER_TOPK = 16
PEER_SEL = PEER_HEADS * PEER_TOPK
D_PLE = 256
EPS = 1e-6
NEG = -1e30
LANES = 128
SUBLANES = 8
VMEM_LIMIT = 48 * 1024 * 1024
HIGHEST = lax.Precision.HIGHEST


def _rms(x, g):
    return x * lax.rsqrt(jnp.mean(x * x, axis=-1, keepdims=True) + EPS) * g


def _dot(a, b):
    return jnp.dot(a, b, preferred_element_type=F32)


def _dot_nt(a, b, precision=None):
    return lax.dot_general(a, b, (((1,), (1,)), ((), ())), precision=precision,
                           preferred_element_type=F32)


IN_TS = 512


def _in_proj_kernel(x_ref, pos_ref, g_ref, w_ref, invf_ref, after_a, after_b,
                    u_ref, q_ref, k_ref, v_ref, ga_ref, gb_ref):
    del after_a, after_b
    h = _rms(x_ref[...], g_ref[...]).astype(BF16)

    def proj(lo, hi):
        return _dot(h, w_ref[:, lo:hi])

    u_ref[...] = proj(0, D_SSM).astype(BF16)
    ang = pos_ref[...].astype(F32) * invf_ref[...]
    cos = jnp.cos(ang)
    sin = jnp.sin(ang)
    lane = lax.broadcasted_iota(jnp.int32, (1, LANES), 1) % HEAD_DIM
    half = ROT_DIM // 2
    sin_hi = jnp.where((lane >= half) & (lane < ROT_DIM), sin, 0.0)
    sin_lo = jnp.where(lane < half, -sin, 0.0)
    reps = D_ATT // LANES
    cos4 = jnp.concatenate([cos] * reps, axis=1)
    sin_hi4 = jnp.concatenate([sin_hi] * reps, axis=1)
    sin_lo4 = jnp.concatenate([sin_lo] * reps, axis=1)

    def rope(t):
        return (t * cos4 + pltpu.roll(t, half, 1) * sin_hi4
                + pltpu.roll(t, D_ATT - half, 1) * sin_lo4)

    q = rope(proj(D_SSM, D_SSM + D_ATT))
    q_ref[...] = (q * (HEAD_DIM ** -0.5)).astype(BF16)
    k_ref[...] = rope(proj(D_SSM + D_ATT, D_SSM + 2 * D_ATT)).astype(BF16)
    v_ref[...] = proj(D_SSM + 2 * D_ATT, D_SSM + 3 * D_ATT).astype(BF16)
    o = D_SSM + 3 * D_ATT
    ga_ref[...] = jax.nn.sigmoid(proj(o, o + D_MODEL)).astype(BF16)
    gb_ref[...] = jax.nn.sigmoid(proj(o + D_MODEL, o + 2 * D_MODEL)).astype(BF16)


def _in_proj(x, positions, g_mix, w_in, t0, nt, kv_len, after):
    B, S, _ = x.shape
    ts = min(IN_TS, nt)
    assert nt % ts == 0 and t0 % ts == 0
    i0 = t0 // ts
    inv_freq = ROPE_THETA ** (-jnp.arange(0, ROT_DIM, 2, dtype=F32) / ROT_DIM)
    lane = jnp.arange(LANES) % HEAD_DIM
    invf = jnp.where(lane < ROT_DIM, inv_freq[lane % (ROT_DIM // 2)], 0.0).reshape(1, LANES)
    d_in = w_in.shape[1]
    src = lambda d: pl.BlockSpec((None, ts, d), lambda b, i: (b, i0 + i, 0))
    tok = lambda d: pl.BlockSpec((None, ts, d), lambda b, i: (b, i, 0))
    full = lambda shape: pl.BlockSpec(shape, lambda b, i: (0,) * len(shape))
    return pl.pallas_call(
        _in_proj_kernel,
        grid=(B, nt // ts),
        in_specs=[src(D_MODEL), src(1), full((1, D_MODEL)), full((D_MODEL, d_in)), full((1, LANES)),
                  pl.BlockSpec(memory_space=pl.ANY), pl.BlockSpec(memory_space=pl.ANY)],
        out_specs=[pl.BlockSpec((ts, D_SSM), lambda b, i: (i, b)),
                   tok(D_ATT), tok(D_ATT), tok(D_ATT), tok(D_MODEL), tok(D_MODEL)],
        out_shape=[jax.ShapeDtypeStruct((nt, B * D_SSM), BF16),
                   jax.ShapeDtypeStruct((B, nt, D_ATT), BF16),
                   jax.ShapeDtypeStruct((B, kv_len, D_ATT), BF16),
                   jax.ShapeDtypeStruct((B, kv_len, D_ATT), BF16),
                   jax.ShapeDtypeStruct((B, nt, D_MODEL), BF16),
                   jax.ShapeDtypeStruct((B, nt, D_MODEL), BF16)],
        compiler_params=pltpu.CompilerParams(
            dimension_semantics=("parallel", "parallel"), vmem_limit_bytes=VMEM_LIMIT),
        name="in_proj",
    )(x, positions.reshape(B, S, 1), g_mix.reshape(1, D_MODEL), w_in, invf, *after)


S5_TS = 128
S5_BATCH = 4
S5_COLS = 512


def _s5_kernel(u_ref, c0_ref, bre_ref, bim_ref, a1r_ref, a1i_ref, pr_ref, pi_ref,
               cre_ref, cim_ref, d_ref, wglu_ref, y_ref, c1_ref,
               xr, xi, cr, ci, ysc):
    rows = xr.shape[0]
    ts = rows // S5_BATCH

    @pl.when(pl.program_id(0) == 0)
    def _():
        cr[...] = c0_ref[0]
        ci[...] = c0_ref[1]

    u = u_ref[...]
    for cb in range(D_STATE // S5_COLS):
        sl = slice(cb * S5_COLS, (cb + 1) * S5_COLS)
        u_cb = u[:, cb * LANES:(cb + 1) * LANES]
        xr[:, sl] = _dot(u_cb, bre_ref[cb])
        xi[:, sl] = _dot(u_cb, bim_ref[cb])

    hi_rows = lax.broadcasted_iota(jnp.int32, (SUBLANES, S5_COLS), 0) >= S5_BATCH
    for cb in range(D_STATE // S5_COLS):
        sl = slice(cb * S5_COLS, (cb + 1) * S5_COLS)
        a_r, a_i = a1r_ref[:, sl], a1i_ref[:, sl]
        p_r, p_i = pr_ref[:, sl], pi_ref[:, sl]

        def body(t, carry):
            c_r, c_i = carry
            r0 = pl.multiple_of(t * SUBLANES, SUBLANES)
            x_r = xr[pl.ds(r0, SUBLANES), sl]
            x_i = xi[pl.ds(r0, SUBLANES), sl]
            s_r = pltpu.roll(x_r, S5_BATCH, 0)
            s_i = pltpu.roll(x_i, S5_BATCH, 0)
            h_r = x_r + (a_r * s_r - a_i * s_i) + (p_r * c_r - p_i * c_i)
            h_i = x_i + (a_r * s_i + a_i * s_r) + (p_r * c_i + p_i * c_r)
            xr[pl.ds(r0, SUBLANES), sl] = h_r
            xi[pl.ds(r0, SUBLANES), sl] = h_i
            n_r = jnp.where(hi_rows, h_r, pltpu.roll(h_r, S5_BATCH, 0))
            n_i = jnp.where(hi_rows, h_i, pltpu.roll(h_i, S5_BATCH, 0))
            return n_r, n_i

        c_r, c_i = lax.fori_loop(0, rows // SUBLANES, body, (cr[:, sl], ci[:, sl]), unroll=2)
        cr[:, sl] = c_r
        ci[:, sl] = c_i

    y = jnp.concatenate(
        [_dot(xr[:, cb * S5_COLS:(cb + 1) * S5_COLS].astype(BF16), cre_ref[cb])
         - _dot(xi[:, cb * S5_COLS:(cb + 1) * S5_COLS].astype(BF16), cim_ref[cb])
         for cb in range(D_STATE // S5_COLS)], axis=1) + d_ref[...] * u.astype(F32)
    y = jax.nn.gelu(y)
    y = y * jax.nn.sigmoid(_dot(y.astype(BF16), wglu_ref[...]))
    for c in range(D_SSM // LANES):
        ysc[c] = y[:, c * LANES:(c + 1) * LANES]
    for b in range(S5_BATCH):
        for c in range(D_SSM // LANES):
            y_ref[b, :, c * LANES:(c + 1) * LANES] = (
                ysc[c, pl.ds(b, ts, stride=S5_BATCH), :].astype(BF16))

    @pl.when(pl.program_id(0) == pl.num_programs(0) - 1)
    def _():
        c1_ref[0] = cr[...]
        c1_ref[1] = ci[...]


def _s5_tables(log_dt, a_re, a_im, b_re, b_im, c_re, c_im):
    dt = jnp.exp(log_dt.astype(F32))[:, None]
    ar, ai = a_re.astype(F32), a_im.astype(F32)
    mag = jnp.exp(dt * ar)
    abar_re, abar_im = mag * jnp.cos(dt * ai), mag * jnp.sin(dt * ai)
    den = ar * ar + ai * ai
    nr, ni = abar_re - 1.0, abar_im
    f_re = (nr * ar + ni * ai) / den
    f_im = (ni * ar - nr * ai) / den
    br, bi = b_re.astype(F32), b_im.astype(F32)
    bb_re = f_re[..., None] * br - f_im[..., None] * bi
    bb_im = f_re[..., None] * bi + f_im[..., None] * br
    nblk = D_STATE // S5_COLS
    gpb = SSM_GROUPS // nblk
    eye = jnp.eye(gpb, dtype=F32)

    def in_blocks(bb):
        b4 = bb.reshape(nblk, gpb, SSM_STATE, -1)
        return jnp.einsum('bgnc,gh->bgchn', b4, eye).reshape(nblk, D_SSM // nblk, S5_COLS)

    def out_blocks(c):
        c4 = c.astype(F32).reshape(nblk, gpb, -1, SSM_STATE)
        return jnp.einsum('bgcn,gh->bgnhc', c4, eye).reshape(nblk, S5_COLS, D_SSM // nblk)

    a_r = abar_re.reshape(1, D_STATE)
    a_i = abar_im.reshape(1, D_STATE)
    a2_r = a_r * a_r - a_i * a_i
    a2_i = 2.0 * a_r * a_i
    hi = (jnp.arange(SUBLANES) >= S5_BATCH)[:, None]
    a1r = jnp.where(hi, a_r, 0.0)
    a1i = jnp.where(hi, a_i, 0.0)
    p_r = jnp.where(hi, a2_r, a_r)
    p_i = jnp.where(hi, a2_i, a_i)
    return (in_blocks(bb_re).astype(BF16), in_blocks(bb_im).astype(BF16),
            a1r, a1i, p_r, p_i,
            out_blocks(c_re).astype(BF16), out_blocks(c_im).astype(BF16))


def _s5(u_sb, carry, tables, d_skip, w_glu, B):
    assert B == S5_BATCH
    nt = u_sb.shape[0]
    ts = min(S5_TS, nt)
    rows = ts * B
    bre, bim, a1r, a1i, p_r, p_i, cre, cim = tables
    full = lambda shape: pl.BlockSpec(shape, lambda i: (0,) * len(shape))
    return pl.pallas_call(
        _s5_kernel,
        grid=(nt // ts,),
        in_specs=[pl.BlockSpec((rows, D_SSM), lambda i: (i, 0)),
                  full((2, SUBLANES, D_STATE)),
                  full(bre.shape), full(bim.shape),
                  full((SUBLANES, D_STATE)), full((SUBLANES, D_STATE)),
                  full((SUBLANES, D_STATE)), full((SUBLANES, D_STATE)),
                  full(cre.shape), full(cim.shape),
                  full((1, D_SSM)), full((D_SSM, D_SSM))],
        out_specs=[pl.BlockSpec((B, ts, D_SSM), lambda i: (0, i, 0)),
                   full((2, SUBLANES, D_STATE))],
        out_shape=[jax.ShapeDtypeStruct((B, nt, D_SSM), BF16),
                   jax.ShapeDtypeStruct((2, SUBLANES, D_STATE), F32)],
        scratch_shapes=[pltpu.VMEM((rows, D_STATE), F32), pltpu.VMEM((rows, D_STATE), F32),
                        pltpu.VMEM((SUBLANES, D_STATE), F32), pltpu.VMEM((SUBLANES, D_STATE), F32),
                        pltpu.VMEM((D_SSM // LANES, rows, LANES), F32)],
        compiler_params=pltpu.CompilerParams(
            dimension_semantics=("arbitrary",), vmem_limit_bytes=VMEM_LIMIT),
        name="s5",
    )(u_sb.reshape(nt * B, D_SSM), carry, bre, bim, a1r, a1i, p_r, p_i, cre, cim, d_skip, w_glu)


MOBA_PAIR = 2 * MOBA_BLOCK


def _moba_kernel(q0, q_ref, k_ref, v_ref, o_ref, kmean, kaug_a, kaug_b, vaug_a, vaug_b, qaug,
                 m_s, acc_s, s_buf):
    last = pl.program_id(2) + q0 // 2
    nb = k_ref.shape[0] // MOBA_BLOCK
    nbp = kmean.shape[0]
    lane = lax.broadcasted_iota(jnp.int32, (1, LANES), 1)
    head_a = lane < HEAD_DIM

    @pl.when(pl.program_id(2) == 0)
    def _():
        kmean[...] = jnp.zeros_like(kmean)
        for j in range(nb):
            rows = pl.ds(j * MOBA_BLOCK, MOBA_BLOCK)
            kj = k_ref[rows, :].astype(F32)
            vj = v_ref[rows, :].astype(F32)
            kmean[j:j + 1, :] = jnp.sum(kj, axis=0, keepdims=True) * (1.0 / MOBA_BLOCK)
            kaug_a[rows, :] = jnp.where(head_a, kj, jnp.where(lane - HEAD_DIM == j, 1.0, 0.0)).astype(BF16)
            kaug_b[rows, :] = jnp.where(head_a, jnp.where(lane == j, 1.0, 0.0), kj).astype(BF16)
            vaug_a[rows, :] = jnp.where(head_a, vj, 1.0).astype(BF16)
            vaug_b[rows, :] = jnp.where(head_a, 1.0, vj).astype(BF16)
        blk_row = lax.broadcasted_iota(jnp.int32, (nbp, MOBA_BLOCK), 0)
        for t in range(q_ref.shape[0] // MOBA_BLOCK):
            qt = q0 + t
            qf = q_ref[t * MOBA_BLOCK:(t + 1) * MOBA_BLOCK, :].astype(F32)
            for hd, is_a in enumerate((True, False)):
                mine = head_a if is_a else jnp.logical_not(head_a)
                q_own = jnp.where(mine, qf, 0.0)
                g = _dot_nt(kmean[...], q_own, precision=HIGHEST)
                g = jnp.where(blk_row < qt, g, NEG)
                sel = jnp.zeros(g.shape, F32)
                for _ in range(MOBA_TOPK):
                    m = jnp.max(g, axis=0, keepdims=True)
                    idx = jnp.min(jnp.where(g == m, blk_row, nbp), axis=0, keepdims=True)
                    hit = blk_row == idx
                    sel = jnp.where(hit, jnp.where(idx < qt, 1.0, 0.0), sel)
                    g = jnp.where(hit, -jnp.inf, g)
                bias_t = jnp.where(sel > 0.0, 0.0, jnp.where(blk_row == qt, 0.0, NEG))
                bias_t = jnp.concatenate([bias_t, jnp.full((LANES - nbp, MOBA_BLOCK), NEG, F32)], axis=0)
                bias = jnp.transpose(bias_t)
                if is_a:
                    bias = pltpu.roll(bias, HEAD_DIM, 1)
                qaug[hd, t * MOBA_BLOCK:(t + 1) * MOBA_BLOCK, :] = jnp.where(mine, qf, bias).astype(BF16)

    tile_rows = pl.ds(pl.multiple_of(pl.program_id(2) * MOBA_PAIR, MOBA_PAIR), MOBA_PAIR)
    q_augs = [qaug[0, tile_rows, :], qaug[1, tile_rows, :]]

    m_s[...] = jnp.full(m_s.shape, -jnp.inf, F32)
    acc_s[...] = jnp.zeros_like(acc_s)
    qpos = last * MOBA_PAIR + lax.broadcasted_iota(jnp.int32, (MOBA_PAIR, MOBA_PAIR), 0)
    col = lax.broadcasted_iota(jnp.int32, (MOBA_PAIR, MOBA_PAIR), 1)

    def kv_rows(jj):
        return pl.ds(pl.multiple_of(jj * MOBA_PAIR, MOBA_PAIR), MOBA_PAIR)

    def scores(jj, slot):
        for hd, kaug in enumerate((kaug_a, kaug_b)):
            s_buf[slot, hd] = _dot_nt(q_augs[hd], kaug[kv_rows(jj), :])

    def softmax_pv(jj, slot, causal):
        for hd, vaug in enumerate((vaug_a, vaug_b)):
            s = s_buf[slot, hd]
            if causal:
                s = jnp.where(jj * MOBA_PAIR + col <= qpos, s, NEG)
            m_old = m_s[hd]
            m_new = jnp.maximum(m_old, jnp.max(s, axis=-1, keepdims=True))
            alpha = jnp.exp(m_old - m_new)
            p = jnp.exp(s - m_new)
            m_s[hd] = m_new
            acc_s[hd] = alpha * acc_s[hd] + _dot(p.astype(BF16), vaug[kv_rows(jj), :])

    scores(0, 0)

    def body(k, _):
        scores(2 * k + 1, 1)
        softmax_pv(2 * k, 0, False)
        scores(2 * k + 2, 0)
        softmax_pv(2 * k + 1, 1, False)
        return 0

    lax.fori_loop(0, last // 2, body, 0)

    @pl.when(last % 2 == 0)
    def _():
        softmax_pv(last, 0, True)

    @pl.when(last % 2 == 1)
    def _():
        scores(last, 1)
        softmax_pv(last - 1, 0, False)
        softmax_pv(last, 1, True)
    acc_a, acc_b = acc_s[0], acc_s[1]
    o_ref[...] = jnp.where(head_a, acc_a / pltpu.roll(acc_a, HEAD_DIM, 1),
                           acc_b / pltpu.roll(acc_b, HEAD_DIM, 1)).astype(BF16)


def _moba(q, k, v, q0):
    B = q.shape[0]
    nq = q.shape[1] // MOBA_BLOCK
    skv = (q0 + nq) * MOBA_BLOCK
    nb = skv // MOBA_BLOCK
    assert nb <= HEAD_DIM and nb % 2 == 0 and skv <= k.shape[1]
    nbp = -(-nb // SUBLANES) * SUBLANES
    assert q0 % 2 == 0 and nq % 2 == 0
    blk = pl.BlockSpec((None, MOBA_PAIR, LANES), lambda b, h, i: (b, i, h))
    seq = pl.BlockSpec((None, skv, LANES), lambda b, h, i: (b, 0, h))
    return pl.pallas_call(
        functools.partial(_moba_kernel, q0),
        grid=(B, D_ATT // LANES, nq // 2),
        in_specs=[pl.BlockSpec((None, nq * MOBA_BLOCK, LANES), lambda b, h, i: (b, 0, h)), seq, seq],
        out_specs=blk,
        out_shape=jax.ShapeDtypeStruct(q.shape, BF16),
        scratch_shapes=[pltpu.VMEM((nbp, LANES), F32),
                        pltpu.VMEM((skv, LANES), BF16), pltpu.VMEM((skv, LANES), BF16),
                        pltpu.VMEM((skv, LANES), BF16), pltpu.VMEM((skv, LANES), BF16),
                        pltpu.VMEM((2, nq * MOBA_BLOCK, LANES), BF16),
                        pltpu.VMEM((2, MOBA_PAIR, 1), F32),
                        pltpu.VMEM((2, MOBA_PAIR, LANES), F32),
                        pltpu.VMEM((2, 2, MOBA_PAIR, MOBA_PAIR), F32)],
        compiler_params=pltpu.CompilerParams(
            dimension_semantics=("parallel", "parallel", "arbitrary"), vmem_limit_bytes=VMEM_LIMIT),
        name="moba",
    )(q, k, v)


MERGE_TS = 256


def _bf16_bits(x):
    b = pltpu.bitcast(x, jnp.int32)
    r = b + 0x7FFF + (lax.shift_right_logical(b, 16) & 1)
    return lax.shift_right_logical(r, 16)


def _merge_kernel(x_ref, ys_ref, at_ref, ga_ref, gb_ref, wa_ref, wb_ref, wo_ref, g_ref,
                  wq_ref, k1_ref, k2_ref, after_ref, x1_ref, hw_ref, idx_ref, gate_ref, sc_ref):
    del after_ref
    ya = _dot(ys_ref[...], wa_ref[...])
    yb = _dot(at_ref[...], wb_ref[...])
    merged = ga_ref[...].astype(F32) * ya + gb_ref[...].astype(F32) * yb
    x1 = x_ref[...] + _dot(merged.astype(BF16), wo_ref[...])
    x1_ref[...] = x1
    hq = _rms(x1, g_ref[...])
    hw_ref[...] = _pack_words(hq)
    qp = _dot(hq.astype(BF16), wq_ref[...])
    for h in range(PEER_HEADS):
        o = h * PEER_QDIM
        sc_ref[2 * h] = _dot_nt(k1_ref[h], qp[:, o:o + PEER_HALF], precision=HIGHEST)
        sc_ref[2 * h + 1] = _dot_nt(k2_ref[h], qp[:, o + PEER_HALF:o + PEER_QDIM], precision=HIGHEST)
    _topk_kernel(sc_ref, idx_ref, gate_ref)


def _merge(x, ys, att, ga, gb, t0, off, nt, after, w_proj_ssm, w_proj_att, w_out, g_ffn, peer_w_q,
           keys1, keys2):
    B = ys.shape[0]
    ts = min(MERGE_TS, nt)
    nblk = nt // ts
    i0 = (t0 + off) // ts
    o0 = off // ts
    tok = lambda d: pl.BlockSpec((None, ts, d), lambda b, i: (b, o0 + i, 0))
    row = lambda d: pl.BlockSpec((ts, d), lambda b, i: (b * nblk + i, 0))
    full = lambda shape: pl.BlockSpec(shape, lambda b, i: (0,) * len(shape))
    qd = PEER_HEADS * PEER_QDIM
    return pl.pallas_call(
        _merge_kernel,
        grid=(B, nblk),
        in_specs=[pl.BlockSpec((None, ts, D_MODEL), lambda b, i: (b, i0 + i, 0)),
                  tok(D_SSM), tok(D_ATT), tok(D_MODEL), tok(D_MODEL),
                  full((D_SSM, D_MODEL)), full((D_ATT, D_MODEL)), full((D_MODEL, D_MODEL)),
                  full((1, D_MODEL)), full((D_MODEL, qd)),
                  full((PEER_HEADS, PEER_KEYS, PEER_HALF)), full((PEER_HEADS, PEER_KEYS, PEER_HALF)),
                  pl.BlockSpec(memory_space=pl.ANY)],
        out_specs=[row(D_MODEL), row(D_MODEL // 2), row(PEER_SEL), row(PEER_SEL)],
        out_shape=[jax.ShapeDtypeStruct((B * nt, D_MODEL), F32),
                   jax.ShapeDtypeStruct((B * nt, D_MODEL // 2), jnp.int32),
                   jax.ShapeDtypeStruct((B * nt, PEER_SEL), jnp.int32),
                   jax.ShapeDtypeStruct((B * nt, PEER_SEL), F32)],
        scratch_shapes=[pltpu.VMEM((2 * PEER_HEADS, PEER_KEYS, ts), F32)],
        compiler_params=pltpu.CompilerParams(
            dimension_semantics=("parallel", "parallel"), vmem_limit_bytes=VMEM_LIMIT),
        name="merge",
    )(x, ys, att, ga, gb, w_proj_ssm, w_proj_att, w_out, g_ffn, peer_w_q, keys1, keys2, after)


def _top_rows(s, row, k):
    vals, idxs = [], []
    for _ in range(k):
        m = jnp.max(s, axis=0, keepdims=True)
        idx = jnp.min(jnp.where(s == m, row, s.shape[0]), axis=0, keepdims=True)
        vals.append(m)
        idxs.append(idx)
        s = jnp.where(row == idx, -jnp.inf, s)
    return vals, idxs


def _stack_rows(rows, row16):
    acc = jnp.zeros(row16.shape, rows[0].dtype)
    for r, v in enumerate(rows):
        acc = jnp.where(row16 == r, v, acc)
    return acc


def _topk_kernel(sc_ref, idx_ref, gate_ref):
    ts = sc_ref.shape[-1]
    row = lax.broadcasted_iota(jnp.int32, (PEER_KEYS, ts), 0).astype(F32)
    row16 = lax.broadcasted_iota(jnp.int32, (PEER_TOPK, ts), 0)
    row8 = lax.broadcasted_iota(jnp.int32, (SUBLANES, ts), 0)
    counts = [PEER_TOPK // (i + 1) for i in range(PEER_TOPK)]
    heights = [PEER_TOPK if c > SUBLANES else SUBLANES for c in counts]
    n_cand = sum(heights)
    rowc = lax.broadcasted_iota(jnp.int32, (n_cand, ts), 0).astype(F32)
    gate_rows, eid_rows = [], []
    for h in range(PEER_HEADS):
        v1, i1 = _top_rows(sc_ref[2 * h], row, PEER_TOPK)
        v2, i2 = _top_rows(sc_ref[2 * h + 1], row, PEER_TOPK)
        v2s = _stack_rows(v2, row16)
        i2s = _stack_rows(i2, row16)
        cand, eid = [], []
        for i in range(PEER_TOPK):
            n = heights[i]
            cand.append(jnp.where((row16 if n == PEER_TOPK else row8) < counts[i],
                                  v1[i] + v2s[:n], -jnp.inf))
            eid.append(i1[i] * PEER_KEYS + i2s[:n])
        cand = jnp.concatenate(cand, axis=0)
        eid = jnp.concatenate(eid, axis=0)
        tops, picks = [], []
        for _ in range(PEER_TOPK):
            m = jnp.max(cand, axis=0, keepdims=True)
            pos = jnp.min(jnp.where(cand == m, rowc, n_cand), axis=0, keepdims=True)
            hit = rowc == pos
            picks.append(jnp.max(jnp.where(hit, eid, -1.0), axis=0, keepdims=True))
            tops.append(m)
            cand = jnp.where(hit, -jnp.inf, cand)
        top = _stack_rows(tops, row16)
        p = jnp.exp(top - jnp.max(top, axis=0, keepdims=True))
        gate_rows.append(p / jnp.sum(p, axis=0, keepdims=True))
        eid_rows.append(_stack_rows(picks, row16))
    gate_ref[...] = jnp.transpose(jnp.concatenate(gate_rows, axis=0))
    idx_ref[...] = jnp.transpose(jnp.concatenate(eid_rows, axis=0)).astype(jnp.int32)


SC_CORES = 2
SC_SUBCORES = 16
SC_LANES = 16
SC_WORKERS = SC_CORES * SC_SUBCORES
PEER_CH = SC_LANES
PEER_NCH = PEER_SEL // PEER_CH
PEER_WORDS = D_MODEL // 2
PEER_NWG = PEER_WORDS // SC_LANES
PEER_RING = 4
PEER_QUAD = 4
HI_MASK = -65536
GELU_C = 0.7978845608028654


def _gelu_tanh_via_exp(x):
    z = GELU_C * (x + 0.044715 * (x * x * x))
    t = 1.0 - 2.0 / (jnp.exp(2.0 * z) + 1.0)
    return 0.5 * x * (1.0 + t)


def _unpack_pair(w):
    lo = plsc.bitcast(lax.shift_left(w, 16), F32)
    hi = plsc.bitcast(lax.bitwise_and(w, HI_MASK), F32)
    return lo, hi


def _peer_sc_body(idx_hbm, gate_hbm, h_hbm, uv_hbm, after_hbm, o_hbm,
                  idx_v, gate_v, h_v, buf, out_v, gsem, msem, osem):
    n_tok = o_hbm.shape[0] // SC_WORKERS
    base = (lax.axis_index("s") * SC_CORES + lax.axis_index("c")) * n_tok
    lane = lax.iota(jnp.int32, SC_LANES)
    zero_rows = jnp.zeros((SC_LANES,), jnp.int32)

    def meta_copies(tok, s):
        return (pltpu.make_async_copy(idx_hbm.at[tok], idx_v.at[s], msem.at[s]),
                pltpu.make_async_copy(gate_hbm.at[tok], gate_v.at[s], msem.at[s]),
                pltpu.make_async_copy(h_hbm.at[tok], h_v.at[s], msem.at[s]))

    def gather(slot, rows):
        return pltpu.make_async_copy(uv_hbm.at[rows], buf.at[slot], gsem.at[slot])

    def token(t, carry):
        s = t % 2
        tok = base + t
        nxt = base + jnp.minimum(t + 1, n_tok - 1)
        for cp in meta_copies(nxt, 1 - s):
            cp.start()

        @pl.when(t >= 2)
        def _():
            pltpu.make_async_copy(out_v.at[s], o_hbm.at[tok], osem.at[s]).wait()

        def chunk(c, carry):
            slot = c % PEER_RING
            gather(slot, zero_rows).wait()

            def dot_step(q, accs):
                cols = [pl.ds(pl.multiple_of((q * PEER_QUAD + j) * SC_LANES, SC_LANES), SC_LANES)
                        for j in range(PEER_QUAD)]
                hs = [plsc.bitcast(h_v[s, col], BF16) for col in cols]
                out = []
                for r in range(PEER_CH):
                    p = plsc.bitcast(buf[slot, r, cols[0]], BF16) * hs[0]
                    for j in range(1, PEER_QUAD):
                        p = p + plsc.bitcast(buf[slot, r, cols[j]], BF16) * hs[j]
                    lo, hi = _unpack_pair(plsc.bitcast(p, jnp.int32))
                    out.append(accs[r] + lo + hi)
                return tuple(out)

            accs = lax.fori_loop(0, PEER_NWG // PEER_QUAD, dot_step,
                                 tuple(jnp.zeros((SC_LANES,), F32) for _ in range(PEER_CH)))
            tot = jnp.zeros((SC_LANES,), F32)
            for r in range(PEER_CH):
                tot = jnp.where(lane == r, jnp.sum(accs[r]), tot)
            rows = pl.ds(pl.multiple_of(c * PEER_CH, PEER_CH), PEER_CH)
            wvec = gate_v[s, rows] * _gelu_tanh_via_exp(tot)
            ws = []
            for r in range(PEER_CH):
                w = wvec.at[jnp.full((SC_LANES,), r, jnp.int32)].get(mode="promise_in_bounds")
                ws.append(plsc.pack(w, w, format=plsc.PackFormat.INTERLEAVED,
                                    preferred_element_type=BF16))
            first = c == 0

            @plsc.parallel_loop(0, PEER_NWG, unroll=2)
            def acc_step(g):
                col = pl.ds(pl.multiple_of(g * SC_LANES, SC_LANES), SC_LANES)
                col_v = pl.ds(pl.multiple_of(PEER_WORDS + g * SC_LANES, SC_LANES), SC_LANES)
                o_lo = jnp.where(first, 0.0, out_v[s, col])
                o_hi = jnp.where(first, 0.0, out_v[s, col_v])
                for r0 in range(0, PEER_CH, PEER_QUAD):
                    p = plsc.bitcast(buf[slot, r0, col_v], BF16) * ws[r0]
                    for r in range(r0 + 1, r0 + PEER_QUAD):
                        p = p + plsc.bitcast(buf[slot, r, col_v], BF16) * ws[r]
                    lo, hi = _unpack_pair(plsc.bitcast(p, jnp.int32))
                    o_lo = o_lo + lo
                    o_hi = o_hi + hi
                out_v[s, col] = o_lo
                out_v[s, col_v] = o_hi

            @pl.when(c == PEER_NCH - PEER_RING)
            def _():
                for cp in meta_copies(nxt, 1 - s):
                    cp.wait()

            ahead = c + PEER_RING
            src = jnp.where(ahead < PEER_NCH, s, 1 - s)
            nrows = idx_v[src, pl.ds(pl.multiple_of((ahead % PEER_NCH) * PEER_CH, PEER_CH), PEER_CH)]
            gather(slot, nrows).start()
            return carry

        lax.fori_loop(0, PEER_NCH, chunk, 0)
        pltpu.make_async_copy(out_v.at[s], o_hbm.at[tok], osem.at[s]).start()
        return carry

    for cp in meta_copies(base, 0):
        cp.start()
    for cp in meta_copies(base, 0):
        cp.wait()
    for c in range(PEER_RING):
        gather(c, idx_v[0, pl.ds(c * PEER_CH, PEER_CH)]).start()
    lax.fori_loop(0, n_tok, token, 0)
    for c in range(PEER_RING):
        gather(c, zero_rows).wait()
    for s in range(2):
        pltpu.make_async_copy(out_v.at[s], o_hbm.at[base], osem.at[s]).wait()


PACK_ROWS = 1024


def _pack_words(x):
    half = x.shape[1] // 2
    return _bf16_bits(x[:, :half]) | lax.shift_left(_bf16_bits(x[:, half:]), 16)


def _pack_tables_kernel(u_ref, v_ref, o_ref):
    o_ref[:, :PEER_WORDS] = _pack_words(u_ref[...])
    o_ref[:, PEER_WORDS:] = _pack_words(v_ref[...])


def _pack_tables(peer_u, peer_v):
    n = peer_u.shape[0]
    rows = min(PACK_ROWS, n)
    spec = pl.BlockSpec((rows, D_MODEL), lambda i: (i, 0))
    return pl.pallas_call(
        _pack_tables_kernel,
        grid=(n // rows,),
        in_specs=[spec, spec],
        out_specs=spec,
        out_shape=jax.ShapeDtypeStruct((n, D_MODEL), jnp.int32),
        compiler_params=pltpu.CompilerParams(
            dimension_semantics=("parallel",), vmem_limit_bytes=VMEM_LIMIT),
        name="pack_tables",
    )(peer_u, peer_v)


def _peer(idx, h_words, gates, uv_words, after):
    T = h_words.shape[0]
    assert T % (2 * SC_WORKERS) == 0
    mesh = plsc.VectorSubcoreMesh(core_axis_name="c", subcore_axis_name="s",
                                  num_cores=SC_CORES, num_subcores=SC_SUBCORES)
    return pl.kernel(
        _peer_sc_body,
        out_type=jax.ShapeDtypeStruct((T, D_MODEL), F32),
        mesh=mesh,
        scratch_types=[
            pltpu.VMEM((2, PEER_SEL), jnp.int32), pltpu.VMEM((2, PEER_SEL), F32),
            pltpu.VMEM((2, PEER_WORDS), jnp.int32),
            pltpu.VMEM((PEER_RING, PEER_CH, 2 * PEER_WORDS), jnp.int32),
            pltpu.VMEM((2, D_MODEL), F32),
            pltpu.SemaphoreType.DMA((PEER_RING,)),
            pltpu.SemaphoreType.DMA((2,)), pltpu.SemaphoreType.DMA((2,)),
        ],
        compiler_params=pltpu.CompilerParams(needs_layout_passes=False),
        name="peer_sc",
    )(idx, gates, h_words, uv_words, after)


FINAL_TS = 512


def _final_kernel(x1_ref, pe_ref, p_ref, gp_ref, wg_ref, wp_ref, gf_ref, o_ref):
    x2 = x1_ref[...] + pe_ref[...]
    e = _dot(p_ref[...].astype(BF16), wp_ref[...])
    gate = jax.nn.sigmoid(_dot(_rms(x2, gp_ref[...]).astype(BF16), wg_ref[...]))
    o_ref[...] = _rms(x2 + gate * e, gf_ref[...])


def _final(x1, peer_out, p, t0, nt, g_ple, ple_w_gate, ple_w_proj, g_final):
    B = p.shape[0]
    ts = min(FINAL_TS, nt)
    nblk = nt // ts
    i0 = t0 // ts
    row = lambda d: pl.BlockSpec((ts, d), lambda b, i: (b * nblk + i, 0))
    full = lambda shape: pl.BlockSpec(shape, lambda b, i: (0,) * len(shape))
    return pl.pallas_call(
        _final_kernel,
        grid=(B, nblk),
        in_specs=[row(D_MODEL), row(D_MODEL),
                  pl.BlockSpec((None, ts, D_PLE), lambda b, i: (b, i0 + i, 0)),
                  full((1, D_MODEL)), full((D_MODEL, D_MODEL)), full((D_PLE, D_MODEL)),
                  full((1, D_MODEL))],
        out_specs=pl.BlockSpec((None, ts, D_MODEL), lambda b, i: (b, i, 0)),
        out_shape=jax.ShapeDtypeStruct((B, nt, D_MODEL), F32),
        compiler_params=pltpu.CompilerParams(
            dimension_semantics=("parallel", "parallel"), vmem_limit_bytes=VMEM_LIMIT),
        name="final",
    )(x1, peer_out, p, g_ple, ple_w_gate, ple_w_proj, g_final)


CHUNK_STEPS = (512, 512, 512, 512, 512, 1024, 1024, 1024, 1024, 1024, 512)


def kernel(x, p, positions, g_mix, w_in, ssm_log_dt, ssm_a_re, ssm_a_im, ssm_b_re, ssm_b_im,
           ssm_c_re, ssm_c_im, ssm_d, ssm_w_glu, w_proj_ssm, w_proj_att, w_out, g_ffn,
           peer_w_q, peer_keys1, peer_keys2, peer_u, peer_v, g_ple, ple_w_gate, ple_w_proj,
           g_final):
    B, S, _ = x.shape
    assert w_in.shape[0] == 1, "the final rmsnorm is fused into the single layer's last stage"
    steps = CHUNK_STEPS if sum(CHUNK_STEPS) == S else (S,)
    i = 0
    tables = _s5_tables(ssm_log_dt[i], ssm_a_re[i], ssm_a_im[i], ssm_b_re[i], ssm_b_im[i],
                        ssm_c_re[i], ssm_c_im[i])
    w_in_b, w_glu_b = w_in[i].astype(BF16), ssm_w_glu[i].astype(BF16)
    d_skip = ssm_d[i].reshape(1, D_SSM).astype(F32)
    merge_w = (w_proj_ssm[i].astype(BF16), w_proj_att[i].astype(BF16), w_out[i].astype(BF16),
               g_ffn[i].reshape(1, D_MODEL), peer_w_q[i].astype(BF16), peer_keys1[i], peer_keys2[i])
    final_w = (g_ple[i].reshape(1, D_MODEL), ple_w_gate[i].astype(BF16),
               ple_w_proj[i].astype(BF16), g_final.reshape(1, D_MODEL))
    uv_words = _pack_tables(peer_u[i], peer_v[i])
    k_all = v_all = None
    carry = jnp.zeros((2, SUBLANES, D_STATE), F32)
    outs = []
    t0 = 0
    after = (carry, carry)
    peers = []
    for nt in steps:
        first = t0 == 0
        u_sb, q, k, v, ga, gb = _in_proj(x, positions, g_mix[i], w_in_b, t0, nt,
                                         S if first else nt, after)
        k_all = k if first else lax.dynamic_update_slice(k_all, k, (0, t0, 0))
        v_all = v if first else lax.dynamic_update_slice(v_all, v, (0, t0, 0))
        ys, carry = _s5(u_sb, carry, tables, d_skip, w_glu_b, B)
        att = _moba(q, k_all, v_all, t0 // MOBA_BLOCK)
        subs = ((0, nt // 2), (nt // 2, nt // 2)) if t0 == 0 and nt >= 2 * MERGE_TS else ((0, nt),)
        for off, n in subs:
            x1, h_words, idx, gates = _merge(x, ys, att, ga, gb, t0, off, n,
                                             peers[-2] if len(peers) > 1 else carry, *merge_w)
            peer_out = _peer(idx, h_words, gates, uv_words,
                             peers[-1] if t0 + off + n == S else carry)
            peers.append(peer_out)
            outs.append(_final(x1, peer_out, p[i], t0 + off, n, *final_w))
        after = (gates, outs[-3] if len(outs) > 2 else carry)
        t0 += nt
    return jnp.concatenate(outs, axis=1)
```

```python
import functools

import jax
import jax.numpy as jnp
from jax import lax
from jax.experimental import pallas as pl
from jax.experimental.pallas import tpu as pltpu
from jax.experimental.pallas import tpu_sc as plsc

F32 = jnp.float32
BF16 = jnp.bfloat16

D_MODEL = 1024
D_SSM = 512
SSM_GROUPS = 32
SSM_STATE = 64
D_STATE = SSM_GROUPS * SSM_STATE
HEAD_DIM = 64
D_ATT = 512
ROT_DIM = 16
ROPE_THETA = 500000.0
MOBA_BLOCK = 256
MOBA_TOPK = 3
PEER_HEADS = 8
PEER_KEYS = 128
PEER_QDIM = 256
PEER_HALF = 128
PEER_TOPK = 16
PEER_SEL = PEER_HEADS * PEER_TOPK
D_PLE = 256
EPS = 1e-6
NEG = -1e30
LANES = 128
SUBLANES = 8
VMEM_LIMIT = 48 * 1024 * 1024
HIGHEST = lax.Precision.HIGHEST


def _rms(x, g):
    return x * lax.rsqrt(jnp.mean(x * x, axis=-1, keepdims=True) + EPS) * g


def _dot(a, b):
    return jnp.dot(a, b, preferred_element_type=F32)


def _dot_nt(a, b, precision=None):
    return lax.dot_general(a, b, (((1,), (1,)), ((), ())), precision=precision,
                           preferred_element_type=F32)


IN_TS = 512


def _in_proj_kernel(x_ref, pos_ref, g_ref, w_ref, invf_ref, after_a, after_b,
                    u_ref, q_ref, k_ref, v_ref, ga_ref, gb_ref):
    del after_a, after_b
    h = _rms(x_ref[...], g_ref[...]).astype(BF16)

    def proj(lo, hi):
        return _dot(h, w_ref[:, lo:hi])

    u_ref[...] = proj(0, D_SSM).astype(BF16)
    ang = pos_ref[...].astype(F32) * invf_ref[...]
    cos = jnp.cos(ang)
    sin = jnp.sin(ang)
    lane = lax.broadcasted_iota(jnp.int32, (1, LANES), 1) % HEAD_DIM
    half = ROT_DIM // 2
    sin_hi = jnp.where((lane >= half) & (lane < ROT_DIM), sin, 0.0)
    sin_lo = jnp.where(lane < half, -sin, 0.0)
    reps = D_ATT // LANES
    cos4 = jnp.concatenate([cos] * reps, axis=1)
    sin_hi4 = jnp.concatenate([sin_hi] * reps, axis=1)
    sin_lo4 = jnp.concatenate([sin_lo] * reps, axis=1)

    def rope(t):
        return (t * cos4 + pltpu.roll(t, half, 1) * sin_hi4
                + pltpu.roll(t, D_ATT - half, 1) * sin_lo4)

    q = rope(proj(D_SSM, D_SSM + D_ATT))
    q_ref[...] = (q * (HEAD_DIM ** -0.5)).astype(BF16)
    k_ref[...] = rope(proj(D_SSM + D_ATT, D_SSM + 2 * D_ATT)).astype(BF16)
    v_ref[...] = proj(D_SSM + 2 * D_ATT, D_SSM + 3 * D_ATT).astype(BF16)
    o = D_SSM + 3 * D_ATT
    ga_ref[...] = jax.nn.sigmoid(proj(o, o + D_MODEL)).astype(BF16)
    gb_ref[...] = jax.nn.sigmoid(proj(o + D_MODEL, o + 2 * D_MODEL)).astype(BF16)


def _in_proj(x, positions, g_mix, w_in, t0, nt, after):
    B, S, _ = x.shape
    ts = min(IN_TS, nt)
    assert nt % ts == 0 and t0 % ts == 0
    i0 = t0 // ts
    inv_freq = ROPE_THETA ** (-jnp.arange(0, ROT_DIM, 2, dtype=F32) / ROT_DIM)
    lane = jnp.arange(LANES) % HEAD_DIM
    invf = jnp.where(lane < ROT_DIM, inv_freq[lane % (ROT_DIM // 2)], 0.0).reshape(1, LANES)
    d_in = w_in.shape[1]
    src = lambda d: pl.BlockSpec((None, ts, d), lambda b, i: (b, i0 + i, 0))
    tok = lambda d: pl.BlockSpec((None, ts, d), lambda b, i: (b, i, 0))
    full = lambda shape: pl.BlockSpec(shape, lambda b, i: (0,) * len(shape))
    return pl.pallas_call(
        _in_proj_kernel,
        grid=(B, nt // ts),
        in_specs=[src(D_MODEL), src(1), full((1, D_MODEL)), full((D_MODEL, d_in)), full((1, LANES)),
                  pl.BlockSpec(memory_space=pl.ANY), pl.BlockSpec(memory_space=pl.ANY)],
        out_specs=[pl.BlockSpec((ts, D_SSM), lambda b, i: (i, b)),
                   tok(D_ATT), tok(D_ATT), tok(D_ATT), tok(D_MODEL), tok(D_MODEL)],
        out_shape=[jax.ShapeDtypeStruct((nt, B * D_SSM), BF16),
                   jax.ShapeDtypeStruct((B, nt, D_ATT), BF16),
                   jax.ShapeDtypeStruct((B, nt, D_ATT), BF16),
                   jax.ShapeDtypeStruct((B, nt, D_ATT), BF16),
                   jax.ShapeDtypeStruct((B, nt, D_MODEL), BF16),
                   jax.ShapeDtypeStruct((B, nt, D_MODEL), BF16)],
        compiler_params=pltpu.CompilerParams(
            dimension_semantics=("parallel", "parallel"), vmem_limit_bytes=VMEM_LIMIT),
        name="in_proj",
    )(x, positions.reshape(B, S, 1), g_mix.reshape(1, D_MODEL), w_in, invf, *after)


S5_TS = 128
S5_BATCH = 4
S5_COLS = 512


def _s5_kernel(u_ref, c0_ref, bre_ref, bim_ref, a1r_ref, a1i_ref, pr_ref, pi_ref,
               cre_ref, cim_ref, d_ref, wglu_ref, y_ref, c1_ref,
               xr, xi, cr, ci, ysc):
    rows = xr.shape[0]
    ts = rows // S5_BATCH

    @pl.when(pl.program_id(0) == 0)
    def _():
        cr[...] = c0_ref[0]
        ci[...] = c0_ref[1]

    u = u_ref[...]
    for cb in range(D_STATE // S5_COLS):
        sl = slice(cb * S5_COLS, (cb + 1) * S5_COLS)
        u_cb = u[:, cb * LANES:(cb + 1) * LANES]
        xr[:, sl] = _dot(u_cb, bre_ref[cb])
        xi[:, sl] = _dot(u_cb, bim_ref[cb])

    hi_rows = lax.broadcasted_iota(jnp.int32, (SUBLANES, S5_COLS), 0) >= S5_BATCH
    for cb in range(D_STATE // S5_COLS):
        sl = slice(cb * S5_COLS, (cb + 1) * S5_COLS)
        a_r, a_i = a1r_ref[:, sl], a1i_ref[:, sl]
        p_r, p_i = pr_ref[:, sl], pi_ref[:, sl]

        def body(t, carry):
            c_r, c_i = carry
            r0 = pl.multiple_of(t * SUBLANES, SUBLANES)
            x_r = xr[pl.ds(r0, SUBLANES), sl]
            x_i = xi[pl.ds(r0, SUBLANES), sl]
            s_r = pltpu.roll(x_r, S5_BATCH, 0)
            s_i = pltpu.roll(x_i, S5_BATCH, 0)
            h_r = x_r + (a_r * s_r - a_i * s_i) + (p_r * c_r - p_i * c_i)
            h_i = x_i + (a_r * s_i + a_i * s_r) + (p_r * c_i + p_i * c_r)
            xr[pl.ds(r0, SUBLANES), sl] = h_r
            xi[pl.ds(r0, SUBLANES), sl] = h_i
            n_r = jnp.where(hi_rows, h_r, pltpu.roll(h_r, S5_BATCH, 0))
            n_i = jnp.where(hi_rows, h_i, pltpu.roll(h_i, S5_BATCH, 0))
            return n_r, n_i

        c_r, c_i = lax.fori_loop(0, rows // SUBLANES, body, (cr[:, sl], ci[:, sl]), unroll=2)
        cr[:, sl] = c_r
        ci[:, sl] = c_i

    y = jnp.concatenate(
        [_dot(xr[:, cb * S5_COLS:(cb + 1) * S5_COLS].astype(BF16), cre_ref[cb])
         - _dot(xi[:, cb * S5_COLS:(cb + 1) * S5_COLS].astype(BF16), cim_ref[cb])
         for cb in range(D_STATE // S5_COLS)], axis=1) + d_ref[...] * u.astype(F32)
    y = jax.nn.gelu(y)
    y = y * jax.nn.sigmoid(_dot(y.astype(BF16), wglu_ref[...]))
    for c in range(D_SSM // LANES):
        ysc[c] = y[:, c * LANES:(c + 1) * LANES]
    for b in range(S5_BATCH):
        for c in range(D_SSM // LANES):
            y_ref[b, :, c * LANES:(c + 1) * LANES] = (
                ysc[c, pl.ds(b, ts, stride=S5_BATCH), :].astype(BF16))

    @pl.when(pl.program_id(0) == pl.num_programs(0) - 1)
    def _():
        c1_ref[0] = cr[...]
        c1_ref[1] = ci[...]


def _s5_tables(log_dt, a_re, a_im, b_re, b_im, c_re, c_im):
    dt = jnp.exp(log_dt.astype(F32))[:, None]
    ar, ai = a_re.astype(F32), a_im.astype(F32)
    mag = jnp.exp(dt * ar)
    abar_re, abar_im = mag * jnp.cos(dt * ai), mag * jnp.sin(dt * ai)
    den = ar * ar + ai * ai
    nr, ni = abar_re - 1.0, abar_im
    f_re = (nr * ar + ni * ai) / den
    f_im = (ni * ar - nr * ai) / den
    br, bi = b_re.astype(F32), b_im.astype(F32)
    bb_re = f_re[..., None] * br - f_im[..., None] * bi
    bb_im = f_re[..., None] * bi + f_im[..., None] * br
    nblk = D_STATE // S5_COLS
    gpb = SSM_GROUPS // nblk
    eye = jnp.eye(gpb, dtype=F32)

    def in_blocks(bb):
        b4 = bb.reshape(nblk, gpb, SSM_STATE, -1)
        return jnp.einsum('bgnc,gh->bgchn', b4, eye).reshape(nblk, D_SSM // nblk, S5_COLS)

    def out_blocks(c):
        c4 = c.astype(F32).reshape(nblk, gpb, -1, SSM_STATE)
        return jnp.einsum('bgcn,gh->bgnhc', c4, eye).reshape(nblk, S5_COLS, D_SSM // nblk)

    a_r = abar_re.reshape(1, D_STATE)
    a_i = abar_im.reshape(1, D_STATE)
    a2_r = a_r * a_r - a_i * a_i
    a2_i = 2.0 * a_r * a_i
    hi = (jnp.arange(SUBLANES) >= S5_BATCH)[:, None]
    a1r = jnp.where(hi, a_r, 0.0)
    a1i = jnp.where(hi, a_i, 0.0)
    p_r = jnp.where(hi, a2_r, a_r)
    p_i = jnp.where(hi, a2_i, a_i)
    return (in_blocks(bb_re).astype(BF16), in_blocks(bb_im).astype(BF16),
            a1r, a1i, p_r, p_i,
            out_blocks(c_re).astype(BF16), out_blocks(c_im).astype(BF16))


def _s5(u_sb, carry, tables, d_skip, w_glu, B):
    assert B == S5_BATCH
    nt = u_sb.shape[0]
    ts = min(S5_TS, nt)
    rows = ts * B
    bre, bim, a1r, a1i, p_r, p_i, cre, cim = tables
    full = lambda shape: pl.BlockSpec(shape, lambda i: (0,) * len(shape))
    return pl.pallas_call(
        _s5_kernel,
        grid=(nt // ts,),
        in_specs=[pl.BlockSpec((rows, D_SSM), lambda i: (i, 0)),
                  full((2, SUBLANES, D_STATE)),
                  full(bre.shape), full(bim.shape),
                  full((SUBLANES, D_STATE)), full((SUBLANES, D_STATE)),
                  full((SUBLANES, D_STATE)), full((SUBLANES, D_STATE)),
                  full(cre.shape), full(cim.shape),
                  full((1, D_SSM)), full((D_SSM, D_SSM))],
        out_specs=[pl.BlockSpec((B, ts, D_SSM), lambda i: (0, i, 0)),
                   full((2, SUBLANES, D_STATE))],
        out_shape=[jax.ShapeDtypeStruct((B, nt, D_SSM), BF16),
                   jax.ShapeDtypeStruct((2, SUBLANES, D_STATE), F32)],
        scratch_shapes=[pltpu.VMEM((rows, D_STATE), F32), pltpu.VMEM((rows, D_STATE), F32),
                        pltpu.VMEM((SUBLANES, D_STATE), F32), pltpu.VMEM((SUBLANES, D_STATE), F32),
                        pltpu.VMEM((D_SSM // LANES, rows, LANES), F32)],
        compiler_params=pltpu.CompilerParams(
            dimension_semantics=("arbitrary",), vmem_limit_bytes=VMEM_LIMIT),
        name="s5",
    )(u_sb.reshape(nt * B, D_SSM), carry, bre, bim, a1r, a1i, p_r, p_i, cre, cim, d_skip, w_glu)


MOBA_PAIR = 2 * MOBA_BLOCK


def _moba_kernel(q0, q_ref, k_ref, v_ref, o_ref, kmean, kaug_a, kaug_b, vaug_a, vaug_b, qaug,
                 m_s, acc_s, s_buf):
    last = pl.program_id(2) + q0 // 2
    nb = k_ref.shape[0] // MOBA_BLOCK
    nbp = kmean.shape[0]
    lane = lax.broadcasted_iota(jnp.int32, (1, LANES), 1)
    head_a = lane < HEAD_DIM

    @pl.when(pl.program_id(2) == 0)
    def _():
        kmean[...] = jnp.zeros_like(kmean)
        for j in range(nb):
            rows = pl.ds(j * MOBA_BLOCK, MOBA_BLOCK)
            kj = k_ref[rows, :].astype(F32)
            vj = v_ref[rows, :].astype(F32)
            kmean[j:j + 1, :] = jnp.sum(kj, axis=0, keepdims=True) * (1.0 / MOBA_BLOCK)
            kaug_a[rows, :] = jnp.where(head_a, kj, jnp.where(lane - HEAD_DIM == j, 1.0, 0.0)).astype(BF16)
            kaug_b[rows, :] = jnp.where(head_a, jnp.where(lane == j, 1.0, 0.0), kj).astype(BF16)
            vaug_a[rows, :] = jnp.where(head_a, vj, 1.0).astype(BF16)
            vaug_b[rows, :] = jnp.where(head_a, 1.0, vj).astype(BF16)
        blk_row = lax.broadcasted_iota(jnp.int32, (nbp, MOBA_BLOCK), 0)
        for t in range(q_ref.shape[0] // MOBA_BLOCK):
            qt = q0 + t
            qf = q_ref[t * MOBA_BLOCK:(t + 1) * MOBA_BLOCK, :].astype(F32)
            for hd, is_a in enumerate((True, False)):
                mine = head_a if is_a else jnp.logical_not(head_a)
                q_own = jnp.where(mine, qf, 0.0)
                g = _dot_nt(kmean[...], q_own, precision=HIGHEST)
                g = jnp.where(blk_row < qt, g, NEG)
                sel = jnp.zeros(g.shape, F32)
                for _ in range(MOBA_TOPK):
                    m = jnp.max(g, axis=0, keepdims=True)
                    idx = jnp.min(jnp.where(g == m, blk_row, nbp), axis=0, keepdims=True)
                    hit = blk_row == idx
                    sel = jnp.where(hit, jnp.where(idx < qt, 1.0, 0.0), sel)
                    g = jnp.where(hit, -jnp.inf, g)
                bias_t = jnp.where(sel > 0.0, 0.0, jnp.where(blk_row == qt, 0.0, NEG))
                bias_t = jnp.concatenate([bias_t, jnp.full((LANES - nbp, MOBA_BLOCK), NEG, F32)], axis=0)
                bias = jnp.transpose(bias_t)
                if is_a:
                    bias = pltpu.roll(bias, HEAD_DIM, 1)
                qaug[hd, t * MOBA_BLOCK:(t + 1) * MOBA_BLOCK, :] = jnp.where(mine, qf, bias).astype(BF16)

    tile_rows = pl.ds(pl.multiple_of(pl.program_id(2) * MOBA_PAIR, MOBA_PAIR), MOBA_PAIR)
    q_augs = [qaug[0, tile_rows, :], qaug[1, tile_rows, :]]

    m_s[...] = jnp.full(m_s.shape, -jnp.inf, F32)
    acc_s[...] = jnp.zeros_like(acc_s)
    qpos = last * MOBA_PAIR + lax.broadcasted_iota(jnp.int32, (MOBA_PAIR, MOBA_PAIR), 0)
    col = lax.broadcasted_iota(jnp.int32, (MOBA_PAIR, MOBA_PAIR), 1)

    def kv_rows(jj):
        return pl.ds(pl.multiple_of(jj * MOBA_PAIR, MOBA_PAIR), MOBA_PAIR)

    def scores(jj, slot):
        for hd, kaug in enumerate((kaug_a, kaug_b)):
            s_buf[slot, hd] = _dot_nt(q_augs[hd], kaug[kv_rows(jj), :])

    def softmax_pv(jj, slot, causal):
        for hd, vaug in enumerate((vaug_a, vaug_b)):
            s = s_buf[slot, hd]
            if causal:
                s = jnp.where(jj * MOBA_PAIR + col <= qpos, s, NEG)
            m_old = m_s[hd]
            m_new = jnp.maximum(m_old, jnp.max(s, axis=-1, keepdims=True))
            alpha = jnp.exp(m_old - m_new)
            p = jnp.exp(s - m_new)
            m_s[hd] = m_new
            acc_s[hd] = alpha * acc_s[hd] + _dot(p.astype(BF16), vaug[kv_rows(jj), :])

    scores(0, 0)

    def body(k, _):
        scores(2 * k + 1, 1)
        softmax_pv(2 * k, 0, False)
        scores(2 * k + 2, 0)
        softmax_pv(2 * k + 1, 1, False)
        return 0

    lax.fori_loop(0, last // 2, body, 0)

    @pl.when(last % 2 == 0)
    def _():
        softmax_pv(last, 0, True)

    @pl.when(last % 2 == 1)
    def _():
        scores(last, 1)
        softmax_pv(last - 1, 0, False)
        softmax_pv(last, 1, True)
    acc_a, acc_b = acc_s[0], acc_s[1]
    o_ref[...] = jnp.where(head_a, acc_a / pltpu.roll(acc_a, HEAD_DIM, 1),
                           acc_b / pltpu.roll(acc_b, HEAD_DIM, 1)).astype(BF16)


def _moba(q, k, v, q0):
    B = q.shape[0]
    nq = q.shape[1] // MOBA_BLOCK
    skv = (q0 + nq) * MOBA_BLOCK
    nb = skv // MOBA_BLOCK
    assert nb <= HEAD_DIM and nb % 2 == 0 and skv <= k.shape[1]
    nbp = -(-nb // SUBLANES) * SUBLANES
    assert q0 % 2 == 0 and nq % 2 == 0
    blk = pl.BlockSpec((None, MOBA_PAIR, LANES), lambda b, h, i: (b, i, h))
    seq = pl.BlockSpec((None, skv, LANES), lambda b, h, i: (b, 0, h))
    return pl.pallas_call(
        functools.partial(_moba_kernel, q0),
        grid=(B, D_ATT // LANES, nq // 2),
        in_specs=[pl.BlockSpec((None, nq * MOBA_BLOCK, LANES), lambda b, h, i: (b, 0, h)), seq, seq],
        out_specs=blk,
        out_shape=jax.ShapeDtypeStruct(q.shape, BF16),
        scratch_shapes=[pltpu.VMEM((nbp, LANES), F32),
                        pltpu.VMEM((skv, LANES), BF16), pltpu.VMEM((skv, LANES), BF16),
                        pltpu.VMEM((skv, LANES), BF16), pltpu.VMEM((skv, LANES), BF16),
                        pltpu.VMEM((2, nq * MOBA_BLOCK, LANES), BF16),
                        pltpu.VMEM((2, MOBA_PAIR, 1), F32),
                        pltpu.VMEM((2, MOBA_PAIR, LANES), F32),
                        pltpu.VMEM((2, 2, MOBA_PAIR, MOBA_PAIR), F32)],
        compiler_params=pltpu.CompilerParams(
            dimension_semantics=("parallel", "parallel", "arbitrary"), vmem_limit_bytes=VMEM_LIMIT),
        name="moba",
    )(q, k, v)


MERGE_TS = 256


def _bf16_bits(x):
    b = pltpu.bitcast(x, jnp.int32)
    r = b + 0x7FFF + (lax.shift_right_logical(b, 16) & 1)
    return lax.shift_right_logical(r, 16)


def _merge_kernel(x_ref, ys_ref, at_ref, ga_ref, gb_ref, wa_ref, wb_ref, wo_ref, g_ref,
                  wq_ref, k1_ref, k2_ref, after_ref, x1_ref, hw_ref, idx_ref, gate_ref, sc_ref):
    del after_ref
    ya = _dot(ys_ref[...], wa_ref[...])
    yb = _dot(at_ref[...], wb_ref[...])
    merged = ga_ref[...].astype(F32) * ya + gb_ref[...].astype(F32) * yb
    x1 = x_ref[...] + _dot(merged.astype(BF16), wo_ref[...])
    x1_ref[...] = x1
    hq = _rms(x1, g_ref[...])
    hw_ref[...] = _pack_words(hq)
    qp = _dot(hq.astype(BF16), wq_ref[...])
    for h in range(PEER_HEADS):
        o = h * PEER_QDIM
        sc_ref[2 * h] = _dot_nt(k1_ref[h], qp[:, o:o + PEER_HALF], precision=HIGHEST)
        sc_ref[2 * h + 1] = _dot_nt(k2_ref[h], qp[:, o + PEER_HALF:o + PEER_QDIM], precision=HIGHEST)
    _topk_kernel(sc_ref, idx_ref, gate_ref)


def _merge(x, ys, att, ga, gb, t0, off, nt, after, w_proj_ssm, w_proj_att, w_out, g_ffn, peer_w_q,
           keys1, keys2):
    B = ys.shape[0]
    ts = min(MERGE_TS, nt)
    nblk = nt // ts
    i0 = (t0 + off) // ts
    o0 = off // ts
    tok = lambda d: pl.BlockSpec((None, ts, d), lambda b, i: (b, o0 + i, 0))
    row = lambda d: pl.BlockSpec((ts, d), lambda b, i: (b * nblk + i, 0))
    full = lambda shape: pl.BlockSpec(shape, lambda b, i: (0,) * len(shape))
    qd = PEER_HEADS * PEER_QDIM
    return pl.pallas_call(
        _merge_kernel,
        grid=(B, nblk),
        in_specs=[pl.BlockSpec((None, ts, D_MODEL), lambda b, i: (b, i0 + i, 0)),
                  tok(D_SSM), tok(D_ATT), tok(D_MODEL), tok(D_MODEL),
                  full((D_SSM, D_MODEL)), full((D_ATT, D_MODEL)), full((D_MODEL, D_MODEL)),
                  full((1, D_MODEL)), full((D_MODEL, qd)),
                  full((PEER_HEADS, PEER_KEYS, PEER_HALF)), full((PEER_HEADS, PEER_KEYS, PEER_HALF)),
                  pl.BlockSpec(memory_space=pl.ANY)],
        out_specs=[row(D_MODEL), row(D_MODEL // 2), row(PEER_SEL), row(PEER_SEL)],
        out_shape=[jax.ShapeDtypeStruct((B * nt, D_MODEL), F32),
                   jax.ShapeDtypeStruct((B * nt, D_MODEL // 2), jnp.int32),
                   jax.ShapeDtypeStruct((B * nt, PEER_SEL), jnp.int32),
                   jax.ShapeDtypeStruct((B * nt, PEER_SEL), F32)],
        scratch_shapes=[pltpu.VMEM((2 * PEER_HEADS, PEER_KEYS, ts), F32)],
        compiler_params=pltpu.CompilerParams(
            dimension_semantics=("parallel", "parallel"), vmem_limit_bytes=VMEM_LIMIT),
        name="merge",
    )(x, ys, att, ga, gb, w_proj_ssm, w_proj_att, w_out, g_ffn, peer_w_q, keys1, keys2, after)


def _top_rows(s, row, k):
    vals, idxs = [], []
    for _ in range(k):
        m = jnp.max(s, axis=0, keepdims=True)
        idx = jnp.min(jnp.where(s == m, row, s.shape[0]), axis=0, keepdims=True)
        vals.append(m)
        idxs.append(idx)
        s = jnp.where(row == idx, -jnp.inf, s)
    return vals, idxs


def _stack_rows(rows, row16):
    acc = jnp.zeros(row16.shape, rows[0].dtype)
    for r, v in enumerate(rows):
        acc = jnp.where(row16 == r, v, acc)
    return acc


def _topk_kernel(sc_ref, idx_ref, gate_ref):
    ts = sc_ref.shape[-1]
    row = lax.broadcasted_iota(jnp.int32, (PEER_KEYS, ts), 0).astype(F32)
    row16 = lax.broadcasted_iota(jnp.int32, (PEER_TOPK, ts), 0)
    row8 = lax.broadcasted_iota(jnp.int32, (SUBLANES, ts), 0)
    counts = [PEER_TOPK // (i + 1) for i in range(PEER_TOPK)]
    heights = [PEER_TOPK if c > SUBLANES else SUBLANES for c in counts]
    n_cand = sum(heights)
    rowc = lax.broadcasted_iota(jnp.int32, (n_cand, ts), 0).astype(F32)
    gate_rows, eid_rows = [], []
    for h in range(PEER_HEADS):
        v1, i1 = _top_rows(sc_ref[2 * h], row, PEER_TOPK)
        v2, i2 = _top_rows(sc_ref[2 * h + 1], row, PEER_TOPK)
        v2s = _stack_rows(v2, row16)
        i2s = _stack_rows(i2, row16)
        cand, eid = [], []
        for i in range(PEER_TOPK):
            n = heights[i]
            cand.append(jnp.where((row16 if n == PEER_TOPK else row8) < counts[i],
                                  v1[i] + v2s[:n], -jnp.inf))
            eid.append(i1[i] * PEER_KEYS + i2s[:n])
        cand = jnp.concatenate(cand, axis=0)
        eid = jnp.concatenate(eid, axis=0)
        tops, picks = [], []
        for _ in range(PEER_TOPK):
            m = jnp.max(cand, axis=0, keepdims=True)
            pos = jnp.min(jnp.where(cand == m, rowc, n_cand), axis=0, keepdims=True)
            hit = rowc == pos
            picks.append(jnp.max(jnp.where(hit, eid, -1.0), axis=0, keepdims=True))
            tops.append(m)
            cand = jnp.where(hit, -jnp.inf, cand)
        top = _stack_rows(tops, row16)
        p = jnp.exp(top - jnp.max(top, axis=0, keepdims=True))
        gate_rows.append(p / jnp.sum(p, axis=0, keepdims=True))
        eid_rows.append(_stack_rows(picks, row16))
    gate_ref[...] = jnp.transpose(jnp.concatenate(gate_rows, axis=0))
    idx_ref[...] = jnp.transpose(jnp.concatenate(eid_rows, axis=0)).astype(jnp.int32)


SC_CORES = 2
SC_SUBCORES = 16
SC_LANES = 16
SC_WORKERS = SC_CORES * SC_SUBCORES
PEER_CH = SC_LANES
PEER_NCH = PEER_SEL // PEER_CH
PEER_WORDS = D_MODEL // 2
PEER_NWG = PEER_WORDS // SC_LANES
PEER_RING = 4
PEER_QUAD = 4
HI_MASK = -65536
GELU_C = 0.7978845608028654


def _gelu_tanh_via_exp(x):
    z = GELU_C * (x + 0.044715 * (x * x * x))
    t = 1.0 - 2.0 / (jnp.exp(2.0 * z) + 1.0)
    return 0.5 * x * (1.0 + t)


def _unpack_pair(w):
    lo = plsc.bitcast(lax.shift_left(w, 16), F32)
    hi = plsc.bitcast(lax.bitwise_and(w, HI_MASK), F32)
    return lo, hi


def _peer_sc_body(idx_hbm, gate_hbm, h_hbm, uv_hbm, after_hbm, o_hbm,
                  idx_v, gate_v, h_v, buf, out_v, gsem, msem, osem):
    n_tok = o_hbm.shape[0] // SC_WORKERS
    base = (lax.axis_index("s") * SC_CORES + lax.axis_index("c")) * n_tok
    lane = lax.iota(jnp.int32, SC_LANES)
    zero_rows = jnp.zeros((SC_LANES,), jnp.int32)

    def meta_copies(tok, s):
        return (pltpu.make_async_copy(idx_hbm.at[tok], idx_v.at[s], msem.at[s]),
                pltpu.make_async_copy(gate_hbm.at[tok], gate_v.at[s], msem.at[s]),
                pltpu.make_async_copy(h_hbm.at[tok], h_v.at[s], msem.at[s]))

    def gather(slot, rows):
        return pltpu.make_async_copy(uv_hbm.at[rows], buf.at[slot], gsem.at[slot])

    def token(t, carry):
        s = t % 2
        tok = base + t
        nxt = base + jnp.minimum(t + 1, n_tok - 1)
        for cp in meta_copies(nxt, 1 - s):
            cp.start()

        @pl.when(t >= 2)
        def _():
            pltpu.make_async_copy(out_v.at[s], o_hbm.at[tok], osem.at[s]).wait()

        def chunk(c, carry):
            slot = c % PEER_RING
            gather(slot, zero_rows).wait()

            def dot_step(q, accs):
                cols = [pl.ds(pl.multiple_of((q * PEER_QUAD + j) * SC_LANES, SC_LANES), SC_LANES)
                        for j in range(PEER_QUAD)]
                hs = [plsc.bitcast(h_v[s, col], BF16) for col in cols]
                out = []
                for r in range(PEER_CH):
                    p = plsc.bitcast(buf[slot, r, cols[0]], BF16) * hs[0]
                    for j in range(1, PEER_QUAD):
                        p = p + plsc.bitcast(buf[slot, r, cols[j]], BF16) * hs[j]
                    lo, hi = _unpack_pair(plsc.bitcast(p, jnp.int32))
                    out.append(accs[r] + lo + hi)
                return tuple(out)

            accs = lax.fori_loop(0, PEER_NWG // PEER_QUAD, dot_step,
                                 tuple(jnp.zeros((SC_LANES,), F32) for _ in range(PEER_CH)))
            tot = jnp.zeros((SC_LANES,), F32)
            for r in range(PEER_CH):
                tot = jnp.where(lane == r, jnp.sum(accs[r]), tot)
            rows = pl.ds(pl.multiple_of(c * PEER_CH, PEER_CH), PEER_CH)
            wvec = gate_v[s, rows] * _gelu_tanh_via_exp(tot)
            ws = []
            for r in range(PEER_CH):
                w = wvec.at[jnp.full((SC_LANES,), r, jnp.int32)].get(mode="promise_in_bounds")
                ws.append(plsc.pack(w, w, format=plsc.PackFormat.INTERLEAVED,
                                    preferred_element_type=BF16))
            first = c == 0

            @plsc.parallel_loop(0, PEER_NWG, unroll=2)
            def acc_step(g):
                col = pl.ds(pl.multiple_of(g * SC_LANES, SC_LANES), SC_LANES)
                col_v = pl.ds(pl.multiple_of(PEER_WORDS + g * SC_LANES, SC_LANES), SC_LANES)
                o_lo = jnp.where(first, 0.0, out_v[s, col])
                o_hi = jnp.where(first, 0.0, out_v[s, col_v])
                for r0 in range(0, PEER_CH, PEER_QUAD):
                    p = plsc.bitcast(buf[slot, r0, col_v], BF16) * ws[r0]
                    for r in range(r0 + 1, r0 + PEER_QUAD):
                        p = p + plsc.bitcast(buf[slot, r, col_v], BF16) * ws[r]
                    lo, hi = _unpack_pair(plsc.bitcast(p, jnp.int32))
                    o_lo = o_lo + lo
                    o_hi = o_hi + hi
                out_v[s, col] = o_lo
                out_v[s, col_v] = o_hi

            @pl.when(c == PEER_NCH - PEER_RING)
            def _():
                for cp in meta_copies(nxt, 1 - s):
                    cp.wait()

            ahead = c + PEER_RING
            src = jnp.where(ahead < PEER_NCH, s, 1 - s)
            nrows = idx_v[src, pl.ds(pl.multiple_of((ahead % PEER_NCH) * PEER_CH, PEER_CH), PEER_CH)]
            gather(slot, nrows).start()
            return carry

        lax.fori_loop(0, PEER_NCH, chunk, 0)
        pltpu.make_async_copy(out_v.at[s], o_hbm.at[tok], osem.at[s]).start()
        return carry

    for cp in meta_copies(base, 0):
        cp.start()
    for cp in meta_copies(base, 0):
        cp.wait()
    for c in range(PEER_RING):
        gather(c, idx_v[0, pl.ds(c * PEER_CH, PEER_CH)]).start()
    lax.fori_loop(0, n_tok, token, 0)
    for c in range(PEER_RING):
        gather(c, zero_rows).wait()
    for s in range(2):
        pltpu.make_async_copy(out_v.at[s], o_hbm.at[base], osem.at[s]).wait()


PACK_ROWS = 1024


def _pack_words(x):
    half = x.shape[1] // 2
    return _bf16_bits(x[:, :half]) | lax.shift_left(_bf16_bits(x[:, half:]), 16)


def _pack_tables_kernel(u_ref, v_ref, o_ref):
    o_ref[:, :PEER_WORDS] = _pack_words(u_ref[...])
    o_ref[:, PEER_WORDS:] = _pack_words(v_ref[...])


def _pack_tables(peer_u, peer_v):
    n = peer_u.shape[0]
    rows = min(PACK_ROWS, n)
    spec = pl.BlockSpec((rows, D_MODEL), lambda i: (i, 0))
    return pl.pallas_call(
        _pack_tables_kernel,
        grid=(n // rows,),
        in_specs=[spec, spec],
        out_specs=spec,
        out_shape=jax.ShapeDtypeStruct((n, D_MODEL), jnp.int32),
        compiler_params=pltpu.CompilerParams(
            dimension_semantics=("parallel",), vmem_limit_bytes=VMEM_LIMIT),
        name="pack_tables",
    )(peer_u, peer_v)


def _peer(idx, h_words, gates, uv_words, after):
    T = h_words.shape[0]
    assert T % (2 * SC_WORKERS) == 0
    mesh = plsc.VectorSubcoreMesh(core_axis_name="c", subcore_axis_name="s",
                                  num_cores=SC_CORES, num_subcores=SC_SUBCORES)
    return pl.kernel(
        _peer_sc_body,
        out_type=jax.ShapeDtypeStruct((T, D_MODEL), F32),
        mesh=mesh,
        scratch_types=[
            pltpu.VMEM((2, PEER_SEL), jnp.int32), pltpu.VMEM((2, PEER_SEL), F32),
            pltpu.VMEM((2, PEER_WORDS), jnp.int32),
            pltpu.VMEM((PEER_RING, PEER_CH, 2 * PEER_WORDS), jnp.int32),
            pltpu.VMEM((2, D_MODEL), F32),
            pltpu.SemaphoreType.DMA((PEER_RING,)),
            pltpu.SemaphoreType.DMA((2,)), pltpu.SemaphoreType.DMA((2,)),
        ],
        compiler_params=pltpu.CompilerParams(needs_layout_passes=False),
        name="peer_sc",
    )(idx, gates, h_words, uv_words, after)


FINAL_TS = 512


def _final_kernel(x1_ref, pe_ref, p_ref, gp_ref, wg_ref, wp_ref, gf_ref, o_ref):
    x2 = x1_ref[...] + pe_ref[...]
    e = _dot(p_ref[...].astype(BF16), wp_ref[...])
    gate = jax.nn.sigmoid(_dot(_rms(x2, gp_ref[...]).astype(BF16), wg_ref[...]))
    o_ref[...] = _rms(x2 + gate * e, gf_ref[...])


def _final(x1, peer_out, p, t0, nt, g_ple, ple_w_gate, ple_w_proj, g_final):
    B = p.shape[0]
    ts = min(FINAL_TS, nt)
    nblk = nt // ts
    i0 = t0 // ts
    row = lambda d: pl.BlockSpec((ts, d), lambda b, i: (b * nblk + i, 0))
    full = lambda shape: pl.BlockSpec(shape, lambda b, i: (0,) * len(shape))
    return pl.pallas_call(
        _final_kernel,
        grid=(B, nblk),
        in_specs=[row(D_MODEL), row(D_MODEL),
                  pl.BlockSpec((None, ts, D_PLE), lambda b, i: (b, i0 + i, 0)),
                  full((1, D_MODEL)), full((D_MODEL, D_MODEL)), full((D_PLE, D_MODEL)),
                  full((1, D_MODEL))],
        out_specs=pl.BlockSpec((None, ts, D_MODEL), lambda b, i: (b, i, 0)),
        out_shape=jax.ShapeDtypeStruct((B, nt, D_MODEL), F32),
        compiler_params=pltpu.CompilerParams(
            dimension_semantics=("parallel", "parallel"), vmem_limit_bytes=VMEM_LIMIT),
        name="final",
    )(x1, peer_out, p, g_ple, ple_w_gate, ple_w_proj, g_final)


CHUNK_STEPS = (512, 512, 512, 512, 512, 1024, 1024, 1024, 1024, 1024, 512)


def kernel(x, p, positions, g_mix, w_in, ssm_log_dt, ssm_a_re, ssm_a_im, ssm_b_re, ssm_b_im,
           ssm_c_re, ssm_c_im, ssm_d, ssm_w_glu, w_proj_ssm, w_proj_att, w_out, g_ffn,
           peer_w_q, peer_keys1, peer_keys2, peer_u, peer_v, g_ple, ple_w_gate, ple_w_proj,
           g_final):
    B, S, _ = x.shape
    assert w_in.shape[0] == 1, "the final rmsnorm is fused into the single layer's last stage"
    steps = CHUNK_STEPS if sum(CHUNK_STEPS) == S else (S,)
    i = 0
    tables = _s5_tables(ssm_log_dt[i], ssm_a_re[i], ssm_a_im[i], ssm_b_re[i], ssm_b_im[i],
                        ssm_c_re[i], ssm_c_im[i])
    w_in_b, w_glu_b = w_in[i].astype(BF16), ssm_w_glu[i].astype(BF16)
    d_skip = ssm_d[i].reshape(1, D_SSM).astype(F32)
    merge_w = (w_proj_ssm[i].astype(BF16), w_proj_att[i].astype(BF16), w_out[i].astype(BF16),
               g_ffn[i].reshape(1, D_MODEL), peer_w_q[i].astype(BF16), peer_keys1[i], peer_keys2[i])
    final_w = (g_ple[i].reshape(1, D_MODEL), ple_w_gate[i].astype(BF16),
               ple_w_proj[i].astype(BF16), g_final.reshape(1, D_MODEL))
    uv_words = _pack_tables(peer_u[i], peer_v[i])
    k_all = jnp.zeros((B, S, D_ATT), BF16)
    v_all = jnp.zeros((B, S, D_ATT), BF16)
    carry = jnp.zeros((2, SUBLANES, D_STATE), F32)
    outs = []
    t0 = 0
    after = (carry, carry)
    peers = []
    for nt in steps:
        u_sb, q, k, v, ga, gb = _in_proj(x, positions, g_mix[i], w_in_b, t0, nt, after)
        k_all = lax.dynamic_update_slice(k_all, k, (0, t0, 0))
        v_all = lax.dynamic_update_slice(v_all, v, (0, t0, 0))
        ys, carry = _s5(u_sb, carry, tables, d_skip, w_glu_b, B)
        att = _moba(q, k_all, v_all, t0 // MOBA_BLOCK)
        subs = ((0, nt // 2), (nt // 2, nt // 2)) if t0 == 0 and nt >= 2 * MERGE_TS else ((0, nt),)
        for off, n in subs:
            x1, h_words, idx, gates = _merge(x, ys, att, ga, gb, t0, off, n,
                                             peers[-2] if len(peers) > 1 else carry, *merge_w)
            peer_out = _peer(idx, h_words, gates, uv_words,
                             peers[-1] if t0 + off + n == S else carry)
            peers.append(peer_out)
            outs.append(_final(x1, peer_out, p[i], t0 + off, n, *final_w))
        after = (gates, outs[-4] if len(outs) > 3 else carry)
        t0 += nt
    return jnp.concatenate(outs, axis=1)
```

```python
import functools

import jax
import jax.numpy as jnp
from jax import lax
from jax.experimental import pallas as pl
from jax.experimental.pallas import tpu as pltpu
from jax.experimental.pallas import tpu_sc as plsc

F32 = jnp.float32
BF16 = jnp.bfloat16

D_MODEL = 1024
D_SSM = 512
SSM_GROUPS = 32
SSM_STATE = 64
D_STATE = SSM_GROUPS * SSM_STATE
HEAD_DIM = 64
D_ATT = 512
ROT_DIM = 16
ROPE_THETA = 500000.0
MOBA_BLOCK = 256
MOBA_TOPK = 3
PEER_HEADS = 8
PEER_KEYS = 128
PEER_QDIM = 256
PEER_HALF = 128
PEER_TOPK = 16
PEER_SEL = PEER_HEADS * PEER_TOPK
D_PLE = 256
EPS = 1e-6
NEG = -1e30
LANES = 128
SUBLANES = 8
VMEM_LIMIT = 48 * 1024 * 1024
HIGHEST = lax.Precision.HIGHEST


def _rms(x, g):
    return x * lax.rsqrt(jnp.mean(x * x, axis=-1, keepdims=True) + EPS) * g


def _dot(a, b):
    return jnp.dot(a, b, preferred_element_type=F32)


def _dot_nt(a, b, precision=None):
    return lax.dot_general(a, b, (((1,), (1,)), ((), ())), precision=precision,
                           preferred_element_type=F32)


IN_TS = 512


def _in_proj_kernel(x_ref, pos_ref, g_ref, w_ref, invf_ref, after_a, after_b,
                    u_ref, q_ref, k_ref, v_ref, ga_ref, gb_ref):
    del after_a, after_b
    h = _rms(x_ref[...], g_ref[...]).astype(BF16)

    def proj(lo, hi):
        return _dot(h, w_ref[:, lo:hi])

    u_ref[...] = proj(0, D_SSM).astype(BF16)
    ang = pos_ref[...].astype(F32) * invf_ref[...]
    cos = jnp.cos(ang)
    sin = jnp.sin(ang)
    lane = lax.broadcasted_iota(jnp.int32, (1, LANES), 1) % HEAD_DIM
    half = ROT_DIM // 2
    sin_hi = jnp.where((lane >= half) & (lane < ROT_DIM), sin, 0.0)
    sin_lo = jnp.where(lane < half, -sin, 0.0)
    reps = D_ATT // LANES
    cos4 = jnp.concatenate([cos] * reps, axis=1)
    sin_hi4 = jnp.concatenate([sin_hi] * reps, axis=1)
    sin_lo4 = jnp.concatenate([sin_lo] * reps, axis=1)

    def rope(t):
        return (t * cos4 + pltpu.roll(t, half, 1) * sin_hi4
                + pltpu.roll(t, D_ATT - half, 1) * sin_lo4)

    q = rope(proj(D_SSM, D_SSM + D_ATT))
    q_ref[...] = (q * (HEAD_DIM ** -0.5)).astype(BF16)
    k_ref[...] = rope(proj(D_SSM + D_ATT, D_SSM + 2 * D_ATT)).astype(BF16)
    v_ref[...] = proj(D_SSM + 2 * D_ATT, D_SSM + 3 * D_ATT).astype(BF16)
    o = D_SSM + 3 * D_ATT
    ga_ref[...] = jax.nn.sigmoid(proj(o, o + D_MODEL)).astype(BF16)
    gb_ref[...] = jax.nn.sigmoid(proj(o + D_MODEL, o + 2 * D_MODEL)).astype(BF16)


def _in_proj(x, positions, g_mix, w_in, t0, nt, after):
    B, S, _ = x.shape
    ts = min(IN_TS, nt)
    assert nt % ts == 0 and t0 % ts == 0
    i0 = t0 // ts
    inv_freq = ROPE_THETA ** (-jnp.arange(0, ROT_DIM, 2, dtype=F32) / ROT_DIM)
    lane = jnp.arange(LANES) % HEAD_DIM
    invf = jnp.where(lane < ROT_DIM, inv_freq[lane % (ROT_DIM // 2)], 0.0).reshape(1, LANES)
    d_in = w_in.shape[1]
    src = lambda d: pl.BlockSpec((None, ts, d), lambda b, i: (b, i0 + i, 0))
    tok = lambda d: pl.BlockSpec((None, ts, d), lambda b, i: (b, i, 0))
    full = lambda shape: pl.BlockSpec(shape, lambda b, i: (0,) * len(shape))
    return pl.pallas_call(
        _in_proj_kernel,
        grid=(B, nt // ts),
        in_specs=[src(D_MODEL), src(1), full((1, D_MODEL)), full((D_MODEL, d_in)), full((1, LANES)),
                  pl.BlockSpec(memory_space=pl.ANY), pl.BlockSpec(memory_space=pl.ANY)],
        out_specs=[pl.BlockSpec((ts, D_SSM), lambda b, i: (i, b)),
                   tok(D_ATT), tok(D_ATT), tok(D_ATT), tok(D_MODEL), tok(D_MODEL)],
        out_shape=[jax.ShapeDtypeStruct((nt, B * D_SSM), BF16),
                   jax.ShapeDtypeStruct((B, nt, D_ATT), BF16),
                   jax.ShapeDtypeStruct((B, nt, D_ATT), BF16),
                   jax.ShapeDtypeStruct((B, nt, D_ATT), BF16),
                   jax.ShapeDtypeStruct((B, nt, D_MODEL), BF16),
                   jax.ShapeDtypeStruct((B, nt, D_MODEL), BF16)],
        compiler_params=pltpu.CompilerParams(
            dimension_semantics=("parallel", "parallel"), vmem_limit_bytes=VMEM_LIMIT),
        name="in_proj",
    )(x, positions.reshape(B, S, 1), g_mix.reshape(1, D_MODEL), w_in, invf, *after)


S5_TS = 128
S5_BATCH = 4
S5_COLS = 512


def _s5_kernel(u_ref, c0_ref, bre_ref, bim_ref, a1r_ref, a1i_ref, pr_ref, pi_ref,
               cre_ref, cim_ref, d_ref, wglu_ref, y_ref, c1_ref,
               xr, xi, cr, ci, ysc):
    rows = xr.shape[0]
    ts = rows // S5_BATCH

    @pl.when(pl.program_id(0) == 0)
    def _():
        cr[...] = c0_ref[0]
        ci[...] = c0_ref[1]

    u = u_ref[...]
    for cb in range(D_STATE // S5_COLS):
        sl = slice(cb * S5_COLS, (cb + 1) * S5_COLS)
        u_cb = u[:, cb * LANES:(cb + 1) * LANES]
        xr[:, sl] = _dot(u_cb, bre_ref[cb])
        xi[:, sl] = _dot(u_cb, bim_ref[cb])

    hi_rows = lax.broadcasted_iota(jnp.int32, (SUBLANES, S5_COLS), 0) >= S5_BATCH
    for cb in range(D_STATE // S5_COLS):
        sl = slice(cb * S5_COLS, (cb + 1) * S5_COLS)
        a_r, a_i = a1r_ref[:, sl], a1i_ref[:, sl]
        p_r, p_i = pr_ref[:, sl], pi_ref[:, sl]

        def body(t, carry):
            c_r, c_i = carry
            r0 = pl.multiple_of(t * SUBLANES, SUBLANES)
            x_r = xr[pl.ds(r0, SUBLANES), sl]
            x_i = xi[pl.ds(r0, SUBLANES), sl]
            s_r = pltpu.roll(x_r, S5_BATCH, 0)
            s_i = pltpu.roll(x_i, S5_BATCH, 0)
            h_r = x_r + (a_r * s_r - a_i * s_i) + (p_r * c_r - p_i * c_i)
            h_i = x_i + (a_r * s_i + a_i * s_r) + (p_r * c_i + p_i * c_r)
            xr[pl.ds(r0, SUBLANES), sl] = h_r
            xi[pl.ds(r0, SUBLANES), sl] = h_i
            n_r = jnp.where(hi_rows, h_r, pltpu.roll(h_r, S5_BATCH, 0))
            n_i = jnp.where(hi_rows, h_i, pltpu.roll(h_i, S5_BATCH, 0))
            return n_r, n_i

        c_r, c_i = lax.fori_loop(0, rows // SUBLANES, body, (cr[:, sl], ci[:, sl]), unroll=2)
        cr[:, sl] = c_r
        ci[:, sl] = c_i

    y = jnp.concatenate(
        [_dot(xr[:, cb * S5_COLS:(cb + 1) * S5_COLS].astype(BF16), cre_ref[cb])
         - _dot(xi[:, cb * S5_COLS:(cb + 1) * S5_COLS].astype(BF16), cim_ref[cb])
         for cb in range(D_STATE // S5_COLS)], axis=1) + d_ref[...] * u.astype(F32)
    y = jax.nn.gelu(y)
    y = y * jax.nn.sigmoid(_dot(y.astype(BF16), wglu_ref[...]))
    for c in range(D_SSM // LANES):
        ysc[c] = y[:, c * LANES:(c + 1) * LANES]
    for b in range(S5_BATCH):
        for c in range(D_SSM // LANES):
            y_ref[b, :, c * LANES:(c + 1) * LANES] = (
                ysc[c, pl.ds(b, ts, stride=S5_BATCH), :].astype(BF16))

    @pl.when(pl.program_id(0) == pl.num_programs(0) - 1)
    def _():
        c1_ref[0] = cr[...]
        c1_ref[1] = ci[...]


def _s5_tables(log_dt, a_re, a_im, b_re, b_im, c_re, c_im):
    dt = jnp.exp(log_dt.astype(F32))[:, None]
    ar, ai = a_re.astype(F32), a_im.astype(F32)
    mag = jnp.exp(dt * ar)
    abar_re, abar_im = mag * jnp.cos(dt * ai), mag * jnp.sin(dt * ai)
    den = ar * ar + ai * ai
    nr, ni = abar_re - 1.0, abar_im
    f_re = (nr * ar + ni * ai) / den
    f_im = (ni * ar - nr * ai) / den
    br, bi = b_re.astype(F32), b_im.astype(F32)
    bb_re = f_re[..., None] * br - f_im[..., None] * bi
    bb_im = f_re[..., None] * bi + f_im[..., None] * br
    nblk = D_STATE // S5_COLS
    gpb = SSM_GROUPS // nblk
    eye = jnp.eye(gpb, dtype=F32)

    def in_blocks(bb):
        b4 = bb.reshape(nblk, gpb, SSM_STATE, -1)
        return jnp.einsum('bgnc,gh->bgchn', b4, eye).reshape(nblk, D_SSM // nblk, S5_COLS)

    def out_blocks(c):
        c4 = c.astype(F32).reshape(nblk, gpb, -1, SSM_STATE)
        return jnp.einsum('bgcn,gh->bgnhc', c4, eye).reshape(nblk, S5_COLS, D_SSM // nblk)

    a_r = abar_re.reshape(1, D_STATE)
    a_i = abar_im.reshape(1, D_STATE)
    a2_r = a_r * a_r - a_i * a_i
    a2_i = 2.0 * a_r * a_i
    hi = (jnp.arange(SUBLANES) >= S5_BATCH)[:, None]
    a1r = jnp.where(hi, a_r, 0.0)
    a1i = jnp.where(hi, a_i, 0.0)
    p_r = jnp.where(hi, a2_r, a_r)
    p_i = jnp.where(hi, a2_i, a_i)
    return (in_blocks(bb_re).astype(BF16), in_blocks(bb_im).astype(BF16),
            a1r, a1i, p_r, p_i,
            out_blocks(c_re).astype(BF16), out_blocks(c_im).astype(BF16))


def _s5(u_sb, carry, tables, d_skip, w_glu, B):
    assert B == S5_BATCH
    nt = u_sb.shape[0]
    ts = min(S5_TS, nt)
    rows = ts * B
    bre, bim, a1r, a1i, p_r, p_i, cre, cim = tables
    full = lambda shape: pl.BlockSpec(shape, lambda i: (0,) * len(shape))
    return pl.pallas_call(
        _s5_kernel,
        grid=(nt // ts,),
        in_specs=[pl.BlockSpec((rows, D_SSM), lambda i: (i, 0)),
                  full((2, SUBLANES, D_STATE)),
                  full(bre.shape), full(bim.shape),
                  full((SUBLANES, D_STATE)), full((SUBLANES, D_STATE)),
                  full((SUBLANES, D_STATE)), full((SUBLANES, D_STATE)),
                  full(cre.shape), full(cim.shape),
                  full((1, D_SSM)), full((D_SSM, D_SSM))],
        out_specs=[pl.BlockSpec((B, ts, D_SSM), lambda i: (0, i, 0)),
                   full((2, SUBLANES, D_STATE))],
        out_shape=[jax.ShapeDtypeStruct((B, nt, D_SSM), BF16),
                   jax.ShapeDtypeStruct((2, SUBLANES, D_STATE), F32)],
        scratch_shapes=[pltpu.VMEM((rows, D_STATE), F32), pltpu.VMEM((rows, D_STATE), F32),
                        pltpu.VMEM((SUBLANES, D_STATE), F32), pltpu.VMEM((SUBLANES, D_STATE), F32),
                        pltpu.VMEM((D_SSM // LANES, rows, LANES), F32)],
        compiler_params=pltpu.CompilerParams(
            dimension_semantics=("arbitrary",), vmem_limit_bytes=VMEM_LIMIT),
        name="s5",
    )(u_sb.reshape(nt * B, D_SSM), carry, bre, bim, a1r, a1i, p_r, p_i, cre, cim, d_skip, w_glu)


MOBA_PAIR = 2 * MOBA_BLOCK


def _moba_kernel(q0, q_ref, k_ref, v_ref, o_ref, kmean, kaug_a, kaug_b, vaug_a, vaug_b, qaug,
                 m_s, acc_s, s_buf):
    last = pl.program_id(2) + q0 // 2
    nb = k_ref.shape[0] // MOBA_BLOCK
    nbp = kmean.shape[0]
    lane = lax.broadcasted_iota(jnp.int32, (1, LANES), 1)
    head_a = lane < HEAD_DIM

    @pl.when(pl.program_id(2) == 0)
    def _():
        kmean[...] = jnp.zeros_like(kmean)
        for j in range(nb):
            rows = pl.ds(j * MOBA_BLOCK, MOBA_BLOCK)
            kj = k_ref[rows, :].astype(F32)
            vj = v_ref[rows, :].astype(F32)
            kmean[j:j + 1, :] = jnp.sum(kj, axis=0, keepdims=True) * (1.0 / MOBA_BLOCK)
            kaug_a[rows, :] = jnp.where(head_a, kj, jnp.where(lane - HEAD_DIM == j, 1.0, 0.0)).astype(BF16)
            kaug_b[rows, :] = jnp.where(head_a, jnp.where(lane == j, 1.0, 0.0), kj).astype(BF16)
            vaug_a[rows, :] = jnp.where(head_a, vj, 1.0).astype(BF16)
            vaug_b[rows, :] = jnp.where(head_a, 1.0, vj).astype(BF16)
        blk_row = lax.broadcasted_iota(jnp.int32, (nbp, MOBA_BLOCK), 0)
        for t in range(q_ref.shape[0] // MOBA_BLOCK):
            qt = q0 + t
            qf = q_ref[t * MOBA_BLOCK:(t + 1) * MOBA_BLOCK, :].astype(F32)
            for hd, is_a in enumerate((True, False)):
                mine = head_a if is_a else jnp.logical_not(head_a)
                q_own = jnp.where(mine, qf, 0.0)
                g = _dot_nt(kmean[...], q_own, precision=HIGHEST)
                g = jnp.where(blk_row < qt, g, NEG)
                sel = jnp.zeros(g.shape, F32)
                for _ in range(MOBA_TOPK):
                    m = jnp.max(g, axis=0, keepdims=True)
                    idx = jnp.min(jnp.where(g == m, blk_row, nbp), axis=0, keepdims=True)
                    hit = blk_row == idx
                    sel = jnp.where(hit, jnp.where(idx < qt, 1.0, 0.0), sel)
                    g = jnp.where(hit, -jnp.inf, g)
                bias_t = jnp.where(sel > 0.0, 0.0, jnp.where(blk_row == qt, 0.0, NEG))
                bias_t = jnp.concatenate([bias_t, jnp.full((LANES - nbp, MOBA_BLOCK), NEG, F32)], axis=0)
                bias = jnp.transpose(bias_t)
                if is_a:
                    bias = pltpu.roll(bias, HEAD_DIM, 1)
                qaug[hd, t * MOBA_BLOCK:(t + 1) * MOBA_BLOCK, :] = jnp.where(mine, qf, bias).astype(BF16)

    tile_rows = pl.ds(pl.multiple_of(pl.program_id(2) * MOBA_PAIR, MOBA_PAIR), MOBA_PAIR)
    q_augs = [qaug[0, tile_rows, :], qaug[1, tile_rows, :]]

    m_s[...] = jnp.full(m_s.shape, -jnp.inf, F32)
    acc_s[...] = jnp.zeros_like(acc_s)
    qpos = last * MOBA_PAIR + lax.broadcasted_iota(jnp.int32, (MOBA_PAIR, MOBA_PAIR), 0)
    col = lax.broadcasted_iota(jnp.int32, (MOBA_PAIR, MOBA_PAIR), 1)

    def kv_rows(jj):
        return pl.ds(pl.multiple_of(jj * MOBA_PAIR, MOBA_PAIR), MOBA_PAIR)

    def scores(jj, slot):
        for hd, kaug in enumerate((kaug_a, kaug_b)):
            s_buf[slot, hd] = _dot_nt(q_augs[hd], kaug[kv_rows(jj), :])

    def softmax_pv(jj, slot, causal):
        for hd, vaug in enumerate((vaug_a, vaug_b)):
            s = s_buf[slot, hd]
            if causal:
                s = jnp.where(jj * MOBA_PAIR + col <= qpos, s, NEG)
            m_old = m_s[hd]
            m_new = jnp.maximum(m_old, jnp.max(s, axis=-1, keepdims=True))
            alpha = jnp.exp(m_old - m_new)
            p = jnp.exp(s - m_new)
            m_s[hd] = m_new
            acc_s[hd] = alpha * acc_s[hd] + _dot(p.astype(BF16), vaug[kv_rows(jj), :])

    scores(0, 0)

    def body(k, _):
        scores(2 * k + 1, 1)
        softmax_pv(2 * k, 0, False)
        scores(2 * k + 2, 0)
        softmax_pv(2 * k + 1, 1, False)
        return 0

    lax.fori_loop(0, last // 2, body, 0)

    @pl.when(last % 2 == 0)
    def _():
        softmax_pv(last, 0, True)

    @pl.when(last % 2 == 1)
    def _():
        scores(last, 1)
        softmax_pv(last - 1, 0, False)
        softmax_pv(last, 1, True)
    acc_a, acc_b = acc_s[0], acc_s[1]
    o_ref[...] = jnp.where(head_a, acc_a / pltpu.roll(acc_a, HEAD_DIM, 1),
                           acc_b / pltpu.roll(acc_b, HEAD_DIM, 1)).astype(BF16)


def _moba(q, k, v, q0):
    B = q.shape[0]
    nq = q.shape[1] // MOBA_BLOCK
    skv = (q0 + nq) * MOBA_BLOCK
    nb = skv // MOBA_BLOCK
    assert nb <= HEAD_DIM and nb % 2 == 0 and skv <= k.shape[1]
    nbp = -(-nb // SUBLANES) * SUBLANES
    assert q0 % 2 == 0 and nq % 2 == 0
    blk = pl.BlockSpec((None, MOBA_PAIR, LANES), lambda b, h, i: (b, i, h))
    seq = pl.BlockSpec((None, skv, LANES), lambda b, h, i: (b, 0, h))
    return pl.pallas_call(
        functools.partial(_moba_kernel, q0),
        grid=(B, D_ATT // LANES, nq // 2),
        in_specs=[pl.BlockSpec((None, nq * MOBA_BLOCK, LANES), lambda b, h, i: (b, 0, h)), seq, seq],
        out_specs=blk,
        out_shape=jax.ShapeDtypeStruct(q.shape, BF16),
        scratch_shapes=[pltpu.VMEM((nbp, LANES), F32),
                        pltpu.VMEM((skv, LANES), BF16), pltpu.VMEM((skv, LANES), BF16),
                        pltpu.VMEM((skv, LANES), BF16), pltpu.VMEM((skv, LANES), BF16),
                        pltpu.VMEM((2, nq * MOBA_BLOCK, LANES), BF16),
                        pltpu.VMEM((2, MOBA_PAIR, 1), F32),
                        pltpu.VMEM((2, MOBA_PAIR, LANES), F32),
                        pltpu.VMEM((2, 2, MOBA_PAIR, MOBA_PAIR), F32)],
        compiler_params=pltpu.CompilerParams(
            dimension_semantics=("parallel", "parallel", "arbitrary"), vmem_limit_bytes=VMEM_LIMIT),
        name="moba",
    )(q, k, v)


MERGE_TS = 256


def _bf16_bits(x):
    b = pltpu.bitcast(x, jnp.int32)
    r = b + 0x7FFF + (lax.shift_right_logical(b, 16) & 1)
    return lax.shift_right_logical(r, 16)


def _merge_kernel(x_ref, ys_ref, at_ref, ga_ref, gb_ref, wa_ref, wb_ref, wo_ref, g_ref,
                  wq_ref, k1_ref, k2_ref, after_ref, x1_ref, hw_ref, idx_ref, gate_ref, sc_ref):
    del after_ref
    ya = _dot(ys_ref[...], wa_ref[...])
    yb = _dot(at_ref[...], wb_ref[...])
    merged = ga_ref[...].astype(F32) * ya + gb_ref[...].astype(F32) * yb
    x1 = x_ref[...] + _dot(merged.astype(BF16), wo_ref[...])
    x1_ref[...] = x1
    hq = _rms(x1, g_ref[...])
    hw_ref[...] = _pack_words(hq)
    qp = _dot(hq.astype(BF16), wq_ref[...])
    for h in range(PEER_HEADS):
        o = h * PEER_QDIM
        sc_ref[2 * h] = _dot_nt(k1_ref[h], qp[:, o:o + PEER_HALF], precision=HIGHEST)
        sc_ref[2 * h + 1] = _dot_nt(k2_ref[h], qp[:, o + PEER_HALF:o + PEER_QDIM], precision=HIGHEST)
    _topk_kernel(sc_ref, idx_ref, gate_ref)


def _merge(x, ys, att, ga, gb, t0, off, nt, after, w_proj_ssm, w_proj_att, w_out, g_ffn, peer_w_q,
           keys1, keys2):
    B = ys.shape[0]
    ts = min(MERGE_TS, nt)
    nblk = nt // ts
    i0 = (t0 + off) // ts
    o0 = off // ts
    tok = lambda d: pl.BlockSpec((None, ts, d), lambda b, i: (b, o0 + i, 0))
    row = lambda d: pl.BlockSpec((ts, d), lambda b, i: (b * nblk + i, 0))
    full = lambda shape: pl.BlockSpec(shape, lambda b, i: (0,) * len(shape))
    qd = PEER_HEADS * PEER_QDIM
    return pl.pallas_call(
        _merge_kernel,
        grid=(B, nblk),
        in_specs=[pl.BlockSpec((None, ts, D_MODEL), lambda b, i: (b, i0 + i, 0)),
                  tok(D_SSM), tok(D_ATT), tok(D_MODEL), tok(D_MODEL),
                  full((D_SSM, D_MODEL)), full((D_ATT, D_MODEL)), full((D_MODEL, D_MODEL)),
                  full((1, D_MODEL)), full((D_MODEL, qd)),
                  full((PEER_HEADS, PEER_KEYS, PEER_HALF)), full((PEER_HEADS, PEER_KEYS, PEER_HALF)),
                  pl.BlockSpec(memory_space=pl.ANY)],
        out_specs=[row(D_MODEL), row(D_MODEL // 2), row(PEER_SEL), row(PEER_SEL)],
        out_shape=[jax.ShapeDtypeStruct((B * nt, D_MODEL), F32),
                   jax.ShapeDtypeStruct((B * nt, D_MODEL // 2), jnp.int32),
                   jax.ShapeDtypeStruct((B * nt, PEER_SEL), jnp.int32),
                   jax.ShapeDtypeStruct((B * nt, PEER_SEL), F32)],
        scratch_shapes=[pltpu.VMEM((2 * PEER_HEADS, PEER_KEYS, ts), F32)],
        compiler_params=pltpu.CompilerParams(
            dimension_semantics=("parallel", "parallel"), vmem_limit_bytes=VMEM_LIMIT),
        name="merge",
    )(x, ys, att, ga, gb, w_proj_ssm, w_proj_att, w_out, g_ffn, peer_w_q, keys1, keys2, after)


def _top_rows(s, row, k):
    vals, idxs = [], []
    for _ in range(k):
        m = jnp.max(s, axis=0, keepdims=True)
        idx = jnp.min(jnp.where(s == m, row, s.shape[0]), axis=0, keepdims=True)
        vals.append(m)
        idxs.append(idx)
        s = jnp.where(row == idx, -jnp.inf, s)
    return vals, idxs


def _stack_rows(rows, row16):
    acc = jnp.zeros(row16.shape, rows[0].dtype)
    for r, v in enumerate(rows):
        acc = jnp.where(row16 == r, v, acc)
    return acc


def _topk_kernel(sc_ref, idx_ref, gate_ref):
    ts = sc_ref.shape[-1]
    row = lax.broadcasted_iota(jnp.int32, (PEER_KEYS, ts), 0).astype(F32)
    row16 = lax.broadcasted_iota(jnp.int32, (PEER_TOPK, ts), 0)
    row8 = lax.broadcasted_iota(jnp.int32, (SUBLANES, ts), 0)
    counts = [PEER_TOPK // (i + 1) for i in range(PEER_TOPK)]
    heights = [PEER_TOPK if c > SUBLANES else SUBLANES for c in counts]
    n_cand = sum(heights)
    rowc = lax.broadcasted_iota(jnp.int32, (n_cand, ts), 0).astype(F32)
    gate_rows, eid_rows = [], []
    for h in range(PEER_HEADS):
        v1, i1 = _top_rows(sc_ref[2 * h], row, PEER_TOPK)
        v2, i2 = _top_rows(sc_ref[2 * h + 1], row, PEER_TOPK)
        v2s = _stack_rows(v2, row16)
        i2s = _stack_rows(i2, row16)
        cand, eid = [], []
        for i in range(PEER_TOPK):
            n = heights[i]
            cand.append(jnp.where((row16 if n == PEER_TOPK else row8) < counts[i],
                                  v1[i] + v2s[:n], -jnp.inf))
            eid.append(i1[i] * PEER_KEYS + i2s[:n])
        cand = jnp.concatenate(cand, axis=0)
        eid = jnp.concatenate(eid, axis=0)
        tops, picks = [], []
        for _ in range(PEER_TOPK):
            m = jnp.max(cand, axis=0, keepdims=True)
            pos = jnp.min(jnp.where(cand == m, rowc, n_cand), axis=0, keepdims=True)
            hit = rowc == pos
            picks.append(jnp.max(jnp.where(hit, eid, -1.0), axis=0, keepdims=True))
            tops.append(m)
            cand = jnp.where(hit, -jnp.inf, cand)
        top = _stack_rows(tops, row16)
        p = jnp.exp(top - jnp.max(top, axis=0, keepdims=True))
        gate_rows.append(p / jnp.sum(p, axis=0, keepdims=True))
        eid_rows.append(_stack_rows(picks, row16))
    gate_ref[...] = jnp.transpose(jnp.concatenate(gate_rows, axis=0))
    idx_ref[...] = jnp.transpose(jnp.concatenate(eid_rows, axis=0)).astype(jnp.int32)


SC_CORES = 2
SC_SUBCORES = 16
SC_LANES = 16
SC_WORKERS = SC_CORES * SC_SUBCORES
PEER_CH = SC_LANES
PEER_NCH = PEER_SEL // PEER_CH
PEER_WORDS = D_MODEL // 2
PEER_NWG = PEER_WORDS // SC_LANES
PEER_RING = 4
PEER_QUAD = 4
HI_MASK = -65536
GELU_C = 0.7978845608028654


def _gelu_tanh_via_exp(x):
    z = GELU_C * (x + 0.044715 * (x * x * x))
    t = 1.0 - 2.0 / (jnp.exp(2.0 * z) + 1.0)
    return 0.5 * x * (1.0 + t)


def _unpack_pair(w):
    lo = plsc.bitcast(lax.shift_left(w, 16), F32)
    hi = plsc.bitcast(lax.bitwise_and(w, HI_MASK), F32)
    return lo, hi


def _peer_sc_body(idx_hbm, gate_hbm, h_hbm, uv_hbm, after_hbm, o_hbm,
                  idx_v, gate_v, h_v, buf, out_v, gsem, msem, osem):
    n_tok = o_hbm.shape[0] // SC_WORKERS
    base = (lax.axis_index("s") * SC_CORES + lax.axis_index("c")) * n_tok
    lane = lax.iota(jnp.int32, SC_LANES)
    zero_rows = jnp.zeros((SC_LANES,), jnp.int32)

    def meta_copies(tok, s):
        return (pltpu.make_async_copy(idx_hbm.at[tok], idx_v.at[s], msem.at[s]),
                pltpu.make_async_copy(gate_hbm.at[tok], gate_v.at[s], msem.at[s]),
                pltpu.make_async_copy(h_hbm.at[tok], h_v.at[s], msem.at[s]))

    def gather(slot, rows):
        return pltpu.make_async_copy(uv_hbm.at[rows], buf.at[slot], gsem.at[slot])

    def token(t, carry):
        s = t % 2
        tok = base + t
        nxt = base + jnp.minimum(t + 1, n_tok - 1)
        for cp in meta_copies(nxt, 1 - s):
            cp.start()

        @pl.when(t >= 2)
        def _():
            pltpu.make_async_copy(out_v.at[s], o_hbm.at[tok], osem.at[s]).wait()

        def chunk(c, carry):
            slot = c % PEER_RING
            gather(slot, zero_rows).wait()

            def dot_step(q, accs):
                cols = [pl.ds(pl.multiple_of((q * PEER_QUAD + j) * SC_LANES, SC_LANES), SC_LANES)
                        for j in range(PEER_QUAD)]
                hs = [plsc.bitcast(h_v[s, col], BF16) for col in cols]
                out = []
                for r in range(PEER_CH):
                    p = plsc.bitcast(buf[slot, r, cols[0]], BF16) * hs[0]
                    for j in range(1, PEER_QUAD):
                        p = p + plsc.bitcast(buf[slot, r, cols[j]], BF16) * hs[j]
                    lo, hi = _unpack_pair(plsc.bitcast(p, jnp.int32))
                    out.append(accs[r] + lo + hi)
                return tuple(out)

            accs = lax.fori_loop(0, PEER_NWG // PEER_QUAD, dot_step,
                                 tuple(jnp.zeros((SC_LANES,), F32) for _ in range(PEER_CH)))
            tot = jnp.zeros((SC_LANES,), F32)
            for r in range(PEER_CH):
                tot = jnp.where(lane == r, jnp.sum(accs[r]), tot)
            rows = pl.ds(pl.multiple_of(c * PEER_CH, PEER_CH), PEER_CH)
            wvec = gate_v[s, rows] * _gelu_tanh_via_exp(tot)
            ws = []
            for r in range(PEER_CH):
                w = wvec.at[jnp.full((SC_LANES,), r, jnp.int32)].get(mode="promise_in_bounds")
                ws.append(plsc.pack(w, w, format=plsc.PackFormat.INTERLEAVED,
                                    preferred_element_type=BF16))
            first = c == 0

            @plsc.parallel_loop(0, PEER_NWG, unroll=2)
            def acc_step(g):
                col = pl.ds(pl.multiple_of(g * SC_LANES, SC_LANES), SC_LANES)
                col_v = pl.ds(pl.multiple_of(PEER_WORDS + g * SC_LANES, SC_LANES), SC_LANES)
                o_lo = jnp.where(first, 0.0, out_v[s, col])
                o_hi = jnp.where(first, 0.0, out_v[s, col_v])
                for r0 in range(0, PEER_CH, PEER_QUAD):
                    p = plsc.bitcast(buf[slot, r0, col_v], BF16) * ws[r0]
                    for r in range(r0 + 1, r0 + PEER_QUAD):
                        p = p + plsc.bitcast(buf[slot, r, col_v], BF16) * ws[r]
                    lo, hi = _unpack_pair(plsc.bitcast(p, jnp.int32))
                    o_lo = o_lo + lo
                    o_hi = o_hi + hi
                out_v[s, col] = o_lo
                out_v[s, col_v] = o_hi

            @pl.when(c == PEER_NCH - PEER_RING)
            def _():
                for cp in meta_copies(nxt, 1 - s):
                    cp.wait()

            ahead = c + PEER_RING
            src = jnp.where(ahead < PEER_NCH, s, 1 - s)
            nrows = idx_v[src, pl.ds(pl.multiple_of((ahead % PEER_NCH) * PEER_CH, PEER_CH), PEER_CH)]
            gather(slot, nrows).start()
            return carry

        lax.fori_loop(0, PEER_NCH, chunk, 0)
        pltpu.make_async_copy(out_v.at[s], o_hbm.at[tok], osem.at[s]).start()
        return carry

    for cp in meta_copies(base, 0):
        cp.start()
    for cp in meta_copies(base, 0):
        cp.wait()
    for c in range(PEER_RING):
        gather(c, idx_v[0, pl.ds(c * PEER_CH, PEER_CH)]).start()
    lax.fori_loop(0, n_tok, token, 0)
    for c in range(PEER_RING):
        gather(c, zero_rows).wait()
    for s in range(2):
        pltpu.make_async_copy(out_v.at[s], o_hbm.at[base], osem.at[s]).wait()


PACK_ROWS = 1024


def _pack_words(x):
    half = x.shape[1] // 2
    return _bf16_bits(x[:, :half]) | lax.shift_left(_bf16_bits(x[:, half:]), 16)


def _pack_tables_kernel(u_ref, v_ref, o_ref):
    o_ref[:, :PEER_WORDS] = _pack_words(u_ref[...])
    o_ref[:, PEER_WORDS:] = _pack_words(v_ref[...])


def _pack_tables(peer_u, peer_v):
    n = peer_u.shape[0]
    rows = min(PACK_ROWS, n)
    spec = pl.BlockSpec((rows, D_MODEL), lambda i: (i, 0))
    return pl.pallas_call(
        _pack_tables_kernel,
        grid=(n // rows,),
        in_specs=[spec, spec],
        out_specs=spec,
        out_shape=jax.ShapeDtypeStruct((n, D_MODEL), jnp.int32),
        compiler_params=pltpu.CompilerParams(
            dimension_semantics=("parallel",), vmem_limit_bytes=VMEM_LIMIT),
        name="pack_tables",
    )(peer_u, peer_v)


def _peer(idx, h_words, gates, uv_words, after):
    T = h_words.shape[0]
    assert T % (2 * SC_WORKERS) == 0
    mesh = plsc.VectorSubcoreMesh(core_axis_name="c", subcore_axis_name="s",
                                  num_cores=SC_CORES, num_subcores=SC_SUBCORES)
    return pl.kernel(
        _peer_sc_body,
        out_type=jax.ShapeDtypeStruct((T, D_MODEL), F32),
        mesh=mesh,
        scratch_types=[
            pltpu.VMEM((2, PEER_SEL), jnp.int32), pltpu.VMEM((2, PEER_SEL), F32),
            pltpu.VMEM((2, PEER_WORDS), jnp.int32),
            pltpu.VMEM((PEER_RING, PEER_CH, 2 * PEER_WORDS), jnp.int32),
            pltpu.VMEM((2, D_MODEL), F32),
            pltpu.SemaphoreType.DMA((PEER_RING,)),
            pltpu.SemaphoreType.DMA((2,)), pltpu.SemaphoreType.DMA((2,)),
        ],
        compiler_params=pltpu.CompilerParams(needs_layout_passes=False),
        name="peer_sc",
    )(idx, gates, h_words, uv_words, after)


FINAL_TS = 512


def _final_kernel(x1_ref, pe_ref, p_ref, gp_ref, wg_ref, wp_ref, gf_ref, o_ref):
    x2 = x1_ref[...] + pe_ref[...]
    e = _dot(p_ref[...].astype(BF16), wp_ref[...])
    gate = jax.nn.sigmoid(_dot(_rms(x2, gp_ref[...]).astype(BF16), wg_ref[...]))
    o_ref[...] = _rms(x2 + gate * e, gf_ref[...])


def _final(x1, peer_out, p, t0, nt, g_ple, ple_w_gate, ple_w_proj, g_final):
    B = p.shape[0]
    ts = min(FINAL_TS, nt)
    nblk = nt // ts
    i0 = t0 // ts
    row = lambda d: pl.BlockSpec((ts, d), lambda b, i: (b * nblk + i, 0))
    full = lambda shape: pl.BlockSpec(shape, lambda b, i: (0,) * len(shape))
    return pl.pallas_call(
        _final_kernel,
        grid=(B, nblk),
        in_specs=[row(D_MODEL), row(D_MODEL),
                  pl.BlockSpec((None, ts, D_PLE), lambda b, i: (b, i0 + i, 0)),
                  full((1, D_MODEL)), full((D_MODEL, D_MODEL)), full((D_PLE, D_MODEL)),
                  full((1, D_MODEL))],
        out_specs=pl.BlockSpec((None, ts, D_MODEL), lambda b, i: (b, i, 0)),
        out_shape=jax.ShapeDtypeStruct((B, nt, D_MODEL), F32),
        compiler_params=pltpu.CompilerParams(
            dimension_semantics=("parallel", "parallel"), vmem_limit_bytes=VMEM_LIMIT),
        name="final",
    )(x1, peer_out, p, g_ple, ple_w_gate, ple_w_proj, g_final)


CHUNK_STEPS = (512, 512, 512, 512, 512, 1024, 1024, 1024, 1024, 1024, 512)


def kernel(x, p, positions, g_mix, w_in, ssm_log_dt, ssm_a_re, ssm_a_im, ssm_b_re, ssm_b_im,
           ssm_c_re, ssm_c_im, ssm_d, ssm_w_glu, w_proj_ssm, w_proj_att, w_out, g_ffn,
           peer_w_q, peer_keys1, peer_keys2, peer_u, peer_v, g_ple, ple_w_gate, ple_w_proj,
           g_final):
    B, S, _ = x.shape
    assert w_in.shape[0] == 1, "the final rmsnorm is fused into the single layer's last stage"
    steps = CHUNK_STEPS if sum(CHUNK_STEPS) == S else (S,)
    i = 0
    tables = _s5_tables(ssm_log_dt[i], ssm_a_re[i], ssm_a_im[i], ssm_b_re[i], ssm_b_im[i],
                        ssm_c_re[i], ssm_c_im[i])
    w_in_b, w_glu_b = w_in[i].astype(BF16), ssm_w_glu[i].astype(BF16)
    d_skip = ssm_d[i].reshape(1, D_SSM).astype(F32)
    merge_w = (w_proj_ssm[i].astype(BF16), w_proj_att[i].astype(BF16), w_out[i].astype(BF16),
               g_ffn[i].reshape(1, D_MODEL), peer_w_q[i].astype(BF16), peer_keys1[i], peer_keys2[i])
    final_w = (g_ple[i].reshape(1, D_MODEL), ple_w_gate[i].astype(BF16),
               ple_w_proj[i].astype(BF16), g_final.reshape(1, D_MODEL))
    uv_words = _pack_tables(peer_u[i], peer_v[i])
    k_all = jnp.zeros((B, S, D_ATT), BF16)
    v_all = jnp.zeros((B, S, D_ATT), BF16)
    carry = jnp.zeros((2, SUBLANES, D_STATE), F32)
    outs = []
    t0 = 0
    after = (carry, carry)
    peers = []
    for nt in steps:
        u_sb, q, k, v, ga, gb = _in_proj(x, positions, g_mix[i], w_in_b, t0, nt, after)
        k_all = lax.dynamic_update_slice(k_all, k, (0, t0, 0))
        v_all = lax.dynamic_update_slice(v_all, v, (0, t0, 0))
        ys, carry = _s5(u_sb, carry, tables, d_skip, w_glu_b, B)
        att = _moba(q, k_all, v_all, t0 // MOBA_BLOCK)
        subs = (((0, nt // 4), (nt // 4, nt // 4), (nt // 2, nt // 2))
                if t0 == 0 and nt >= 2 * MERGE_TS else ((0, nt),))
        for off, n in subs:
            x1, h_words, idx, gates = _merge(x, ys, att, ga, gb, t0, off, n,
                                             peers[-2] if len(peers) > 1 else carry, *merge_w)
            peer_out = _peer(idx, h_words, gates, uv_words,
                             peers[-1] if t0 + off + n == S else carry)
            peers.append(peer_out)
            outs.append(_final(x1, peer_out, p[i], t0 + off, n, *final_w))
        after = (gates, outs[-3] if len(outs) > 2 else carry)
        t0 += nt
    return jnp.concatenate(outs, axis=1)
```

```python
import functools

import jax
import jax.numpy as jnp
from jax import lax
from jax.experimental import pallas as pl
from jax.experimental.pallas import tpu as pltpu
from jax.experimental.pallas import tpu_sc as plsc

F32 = jnp.float32
BF16 = jnp.bfloat16

D_MODEL = 1024
D_SSM = 512
SSM_GROUPS = 32
SSM_STATE = 64
D_STATE = SSM_GROUPS * SSM_STATE
HEAD_DIM = 64
D_ATT = 512
ROT_DIM = 16
ROPE_THETA = 500000.0
MOBA_BLOCK = 256
MOBA_TOPK = 3
PEER_HEADS = 8
PEER_KEYS = 128
PEER_QDIM = 256
PEER_HALF = 128
PEER_TOPK = 16
PEER_SEL = PEER_HEADS * PEER_TOPK
D_PLE = 256
EPS = 1e-6
NEG = -1e30
LANES = 128
SUBLANES = 8
VMEM_LIMIT = 48 * 1024 * 1024
HIGHEST = lax.Precision.HIGHEST


def _rms(x, g):
    return x * lax.rsqrt(jnp.mean(x * x, axis=-1, keepdims=True) + EPS) * g


def _dot(a, b):
    return jnp.dot(a, b, preferred_element_type=F32)


def _dot_nt(a, b, precision=None):
    return lax.dot_general(a, b, (((1,), (1,)), ((), ())), precision=precision,
                           preferred_element_type=F32)


IN_TS = 512


def _in_proj_kernel(x_ref, pos_ref, g_ref, w_ref, invf_ref, after_a, after_b,
                    u_ref, q_ref, k_ref, v_ref, ga_ref, gb_ref):
    del after_a, after_b
    h = _rms(x_ref[...], g_ref[...]).astype(BF16)

    def proj(lo, hi):
        return _dot(h, w_ref[:, lo:hi])

    u_ref[...] = proj(0, D_SSM).astype(BF16)
    ang = pos_ref[...].astype(F32) * invf_ref[...]
    cos = jnp.cos(ang)
    sin = jnp.sin(ang)
    lane = lax.broadcasted_iota(jnp.int32, (1, LANES), 1) % HEAD_DIM
    half = ROT_DIM // 2
    sin_hi = jnp.where((lane >= half) & (lane < ROT_DIM), sin, 0.0)
    sin_lo = jnp.where(lane < half, -sin, 0.0)
    reps = D_ATT // LANES
    cos4 = jnp.concatenate([cos] * reps, axis=1)
    sin_hi4 = jnp.concatenate([sin_hi] * reps, axis=1)
    sin_lo4 = jnp.concatenate([sin_lo] * reps, axis=1)

    def rope(t):
        return (t * cos4 + pltpu.roll(t, half, 1) * sin_hi4
                + pltpu.roll(t, D_ATT - half, 1) * sin_lo4)

    q = rope(proj(D_SSM, D_SSM + D_ATT))
    q_ref[...] = (q * (HEAD_DIM ** -0.5)).astype(BF16)
    k_ref[...] = rope(proj(D_SSM + D_ATT, D_SSM + 2 * D_ATT)).astype(BF16)
    v_ref[...] = proj(D_SSM + 2 * D_ATT, D_SSM + 3 * D_ATT).astype(BF16)
    o = D_SSM + 3 * D_ATT
    ga_ref[...] = jax.nn.sigmoid(proj(o, o + D_MODEL)).astype(BF16)
    gb_ref[...] = jax.nn.sigmoid(proj(o + D_MODEL, o + 2 * D_MODEL)).astype(BF16)


def _in_proj(x, positions, g_mix, w_in, t0, nt, after):
    B, S, _ = x.shape
    ts = min(IN_TS, nt)
    assert nt % ts == 0 and t0 % ts == 0
    i0 = t0 // ts
    inv_freq = ROPE_THETA ** (-jnp.arange(0, ROT_DIM, 2, dtype=F32) / ROT_DIM)
    lane = jnp.arange(LANES) % HEAD_DIM
    invf = jnp.where(lane < ROT_DIM, inv_freq[lane % (ROT_DIM // 2)], 0.0).reshape(1, LANES)
    d_in = w_in.shape[1]
    src = lambda d: pl.BlockSpec((None, ts, d), lambda b, i: (b, i0 + i, 0))
    tok = lambda d: pl.BlockSpec((None, ts, d), lambda b, i: (b, i, 0))
    full = lambda shape: pl.BlockSpec(shape, lambda b, i: (0,) * len(shape))
    return pl.pallas_call(
        _in_proj_kernel,
        grid=(B, nt // ts),
        in_specs=[src(D_MODEL), src(1), full((1, D_MODEL)), full((D_MODEL, d_in)), full((1, LANES)),
                  pl.BlockSpec(memory_space=pl.ANY), pl.BlockSpec(memory_space=pl.ANY)],
        out_specs=[pl.BlockSpec((ts, D_SSM), lambda b, i: (i, b)),
                   tok(D_ATT), tok(D_ATT), tok(D_ATT), tok(D_MODEL), tok(D_MODEL)],
        out_shape=[jax.ShapeDtypeStruct((nt, B * D_SSM), BF16),
                   jax.ShapeDtypeStruct((B, nt, D_ATT), BF16),
                   jax.ShapeDtypeStruct((B, nt, D_ATT), BF16),
                   jax.ShapeDtypeStruct((B, nt, D_ATT), BF16),
                   jax.ShapeDtypeStruct((B, nt, D_MODEL), BF16),
                   jax.ShapeDtypeStruct((B, nt, D_MODEL), BF16)],
        compiler_params=pltpu.CompilerParams(
            dimension_semantics=("parallel", "parallel"), vmem_limit_bytes=VMEM_LIMIT),
        name="in_proj",
    )(x, positions.reshape(B, S, 1), g_mix.reshape(1, D_MODEL), w_in, invf, *after)


S5_TS = 128
S5_BATCH = 4
S5_COLS = 512


def _s5_kernel(u_ref, c0_ref, bre_ref, bim_ref, a1r_ref, a1i_ref, pr_ref, pi_ref,
               cre_ref, cim_ref, d_ref, wglu_ref, y_ref, c1_ref,
               xr, xi, cr, ci, ysc):
    rows = xr.shape[0]
    ts = rows // S5_BATCH

    @pl.when(pl.program_id(0) == 0)
    def _():
        cr[...] = c0_ref[0]
        ci[...] = c0_ref[1]

    u = u_ref[...]
    for cb in range(D_STATE // S5_COLS):
        sl = slice(cb * S5_COLS, (cb + 1) * S5_COLS)
        u_cb = u[:, cb * LANES:(cb + 1) * LANES]
        xr[:, sl] = _dot(u_cb, bre_ref[cb])
        xi[:, sl] = _dot(u_cb, bim_ref[cb])

    hi_rows = lax.broadcasted_iota(jnp.int32, (SUBLANES, S5_COLS), 0) >= S5_BATCH
    for cb in range(D_STATE // S5_COLS):
        sl = slice(cb * S5_COLS, (cb + 1) * S5_COLS)
        a_r, a_i = a1r_ref[:, sl], a1i_ref[:, sl]
        p_r, p_i = pr_ref[:, sl], pi_ref[:, sl]

        def body(t, carry):
            c_r, c_i = carry
            r0 = pl.multiple_of(t * SUBLANES, SUBLANES)
            x_r = xr[pl.ds(r0, SUBLANES), sl]
            x_i = xi[pl.ds(r0, SUBLANES), sl]
            s_r = pltpu.roll(x_r, S5_BATCH, 0)
            s_i = pltpu.roll(x_i, S5_BATCH, 0)
            h_r = x_r + (a_r * s_r - a_i * s_i) + (p_r * c_r - p_i * c_i)
            h_i = x_i + (a_r * s_i + a_i * s_r) + (p_r * c_i + p_i * c_r)
            xr[pl.ds(r0, SUBLANES), sl] = h_r
            xi[pl.ds(r0, SUBLANES), sl] = h_i
            n_r = jnp.where(hi_rows, h_r, pltpu.roll(h_r, S5_BATCH, 0))
            n_i = jnp.where(hi_rows, h_i, pltpu.roll(h_i, S5_BATCH, 0))
            return n_r, n_i

        c_r, c_i = lax.fori_loop(0, rows // SUBLANES, body, (cr[:, sl], ci[:, sl]), unroll=2)
        cr[:, sl] = c_r
        ci[:, sl] = c_i

    y = jnp.concatenate(
        [_dot(xr[:, cb * S5_COLS:(cb + 1) * S5_COLS].astype(BF16), cre_ref[cb])
         - _dot(xi[:, cb * S5_COLS:(cb + 1) * S5_COLS].astype(BF16), cim_ref[cb])
         for cb in range(D_STATE // S5_COLS)], axis=1) + d_ref[...] * u.astype(F32)
    y = jax.nn.gelu(y)
    y = y * jax.nn.sigmoid(_dot(y.astype(BF16), wglu_ref[...]))
    for c in range(D_SSM // LANES):
        ysc[c] = y[:, c * LANES:(c + 1) * LANES]
    for b in range(S5_BATCH):
        for c in range(D_SSM // LANES):
            y_ref[b, :, c * LANES:(c + 1) * LANES] = (
                ysc[c, pl.ds(b, ts, stride=S5_BATCH), :].astype(BF16))

    @pl.when(pl.program_id(0) == pl.num_programs(0) - 1)
    def _():
        c1_ref[0] = cr[...]
        c1_ref[1] = ci[...]


def _s5_tables(log_dt, a_re, a_im, b_re, b_im, c_re, c_im):
    dt = jnp.exp(log_dt.astype(F32))[:, None]
    ar, ai = a_re.astype(F32), a_im.astype(F32)
    mag = jnp.exp(dt * ar)
    abar_re, abar_im = mag * jnp.cos(dt * ai), mag * jnp.sin(dt * ai)
    den = ar * ar + ai * ai
    nr, ni = abar_re - 1.0, abar_im
    f_re = (nr * ar + ni * ai) / den
    f_im = (ni * ar - nr * ai) / den
    br, bi = b_re.astype(F32), b_im.astype(F32)
    bb_re = f_re[..., None] * br - f_im[..., None] * bi
    bb_im = f_re[..., None] * bi + f_im[..., None] * br
    nblk = D_STATE // S5_COLS
    gpb = SSM_GROUPS // nblk
    eye = jnp.eye(gpb, dtype=F32)

    def in_blocks(bb):
        b4 = bb.reshape(nblk, gpb, SSM_STATE, -1)
        return jnp.einsum('bgnc,gh->bgchn', b4, eye).reshape(nblk, D_SSM // nblk, S5_COLS)

    def out_blocks(c):
        c4 = c.astype(F32).reshape(nblk, gpb, -1, SSM_STATE)
        return jnp.einsum('bgcn,gh->bgnhc', c4, eye).reshape(nblk, S5_COLS, D_SSM // nblk)

    a_r = abar_re.reshape(1, D_STATE)
    a_i = abar_im.reshape(1, D_STATE)
    a2_r = a_r * a_r - a_i * a_i
    a2_i = 2.0 * a_r * a_i
    hi = (jnp.arange(SUBLANES) >= S5_BATCH)[:, None]
    a1r = jnp.where(hi, a_r, 0.0)
    a1i = jnp.where(hi, a_i, 0.0)
    p_r = jnp.where(hi, a2_r, a_r)
    p_i = jnp.where(hi, a2_i, a_i)
    return (in_blocks(bb_re).astype(BF16), in_blocks(bb_im).astype(BF16),
            a1r, a1i, p_r, p_i,
            out_blocks(c_re).astype(BF16), out_blocks(c_im).astype(BF16))


def _s5(u_sb, carry, tables, d_skip, w_glu, B):
    assert B == S5_BATCH
    nt = u_sb.shape[0]
    ts = min(S5_TS, nt)
    rows = ts * B
    bre, bim, a1r, a1i, p_r, p_i, cre, cim = tables
    full = lambda shape: pl.BlockSpec(shape, lambda i: (0,) * len(shape))
    return pl.pallas_call(
        _s5_kernel,
        grid=(nt // ts,),
        in_specs=[pl.BlockSpec((rows, D_SSM), lambda i: (i, 0)),
                  full((2, SUBLANES, D_STATE)),
                  full(bre.shape), full(bim.shape),
                  full((SUBLANES, D_STATE)), full((SUBLANES, D_STATE)),
                  full((SUBLANES, D_STATE)), full((SUBLANES, D_STATE)),
                  full(cre.shape), full(cim.shape),
                  full((1, D_SSM)), full((D_SSM, D_SSM))],
        out_specs=[pl.BlockSpec((B, ts, D_SSM), lambda i: (0, i, 0)),
                   full((2, SUBLANES, D_STATE))],
        out_shape=[jax.ShapeDtypeStruct((B, nt, D_SSM), BF16),
                   jax.ShapeDtypeStruct((2, SUBLANES, D_STATE), F32)],
        scratch_shapes=[pltpu.VMEM((rows, D_STATE), F32), pltpu.VMEM((rows, D_STATE), F32),
                        pltpu.VMEM((SUBLANES, D_STATE), F32), pltpu.VMEM((SUBLANES, D_STATE), F32),
                        pltpu.VMEM((D_SSM // LANES, rows, LANES), F32)],
        compiler_params=pltpu.CompilerParams(
            dimension_semantics=("arbitrary",), vmem_limit_bytes=VMEM_LIMIT),
        name="s5",
    )(u_sb.reshape(nt * B, D_SSM), carry, bre, bim, a1r, a1i, p_r, p_i, cre, cim, d_skip, w_glu)


MOBA_PAIR = 2 * MOBA_BLOCK


def _moba_kernel(q0, q_ref, k_ref, v_ref, o_ref, kmean, kaug_a, kaug_b, vaug_a, vaug_b, qaug,
                 m_s, acc_s, s_buf):
    last = pl.program_id(2) + q0 // 2
    nb = k_ref.shape[0] // MOBA_BLOCK
    nbp = kmean.shape[0]
    lane = lax.broadcasted_iota(jnp.int32, (1, LANES), 1)
    head_a = lane < HEAD_DIM

    @pl.when(pl.program_id(2) == 0)
    def _():
        kmean[...] = jnp.zeros_like(kmean)
        for j in range(nb):
            rows = pl.ds(j * MOBA_BLOCK, MOBA_BLOCK)
            kj = k_ref[rows, :].astype(F32)
            vj = v_ref[rows, :].astype(F32)
            kmean[j:j + 1, :] = jnp.sum(kj, axis=0, keepdims=True) * (1.0 / MOBA_BLOCK)
            kaug_a[rows, :] = jnp.where(head_a, kj, jnp.where(lane - HEAD_DIM == j, 1.0, 0.0)).astype(BF16)
            kaug_b[rows, :] = jnp.where(head_a, jnp.where(lane == j, 1.0, 0.0), kj).astype(BF16)
            vaug_a[rows, :] = jnp.where(head_a, vj, 1.0).astype(BF16)
            vaug_b[rows, :] = jnp.where(head_a, 1.0, vj).astype(BF16)
        blk_row = lax.broadcasted_iota(jnp.int32, (nbp, MOBA_BLOCK), 0)
        for t in range(q_ref.shape[0] // MOBA_BLOCK):
            qt = q0 + t
            qf = q_ref[t * MOBA_BLOCK:(t + 1) * MOBA_BLOCK, :].astype(F32)
            for hd, is_a in enumerate((True, False)):
                mine = head_a if is_a else jnp.logical_not(head_a)
                q_own = jnp.where(mine, qf, 0.0)
                g = _dot_nt(kmean[...], q_own, precision=HIGHEST)
                g = jnp.where(blk_row < qt, g, NEG)
                sel = jnp.zeros(g.shape, F32)
                for _ in range(MOBA_TOPK):
                    m = jnp.max(g, axis=0, keepdims=True)
                    idx = jnp.min(jnp.where(g == m, blk_row, nbp), axis=0, keepdims=True)
                    hit = blk_row == idx
                    sel = jnp.where(hit, jnp.where(idx < qt, 1.0, 0.0), sel)
                    g = jnp.where(hit, -jnp.inf, g)
                bias_t = jnp.where(sel > 0.0, 0.0, jnp.where(blk_row == qt, 0.0, NEG))
                bias_t = jnp.concatenate([bias_t, jnp.full((LANES - nbp, MOBA_BLOCK), NEG, F32)], axis=0)
                bias = jnp.transpose(bias_t)
                if is_a:
                    bias = pltpu.roll(bias, HEAD_DIM, 1)
                qaug[hd, t * MOBA_BLOCK:(t + 1) * MOBA_BLOCK, :] = jnp.where(mine, qf, bias).astype(BF16)

    tile_rows = pl.ds(pl.multiple_of(pl.program_id(2) * MOBA_PAIR, MOBA_PAIR), MOBA_PAIR)
    q_augs = [qaug[0, tile_rows, :], qaug[1, tile_rows, :]]

    m_s[...] = jnp.full(m_s.shape, -jnp.inf, F32)
    acc_s[...] = jnp.zeros_like(acc_s)
    qpos = last * MOBA_PAIR + lax.broadcasted_iota(jnp.int32, (MOBA_PAIR, MOBA_PAIR), 0)
    col = lax.broadcasted_iota(jnp.int32, (MOBA_PAIR, MOBA_PAIR), 1)

    def kv_rows(jj):
        return pl.ds(pl.multiple_of(jj * MOBA_PAIR, MOBA_PAIR), MOBA_PAIR)

    def scores(jj, slot):
        for hd, kaug in enumerate((kaug_a, kaug_b)):
            s_buf[slot, hd] = _dot_nt(q_augs[hd], kaug[kv_rows(jj), :])

    def softmax_pv(jj, slot, causal):
        for hd, vaug in enumerate((vaug_a, vaug_b)):
            s = s_buf[slot, hd]
            if causal:
                s = jnp.where(jj * MOBA_PAIR + col <= qpos, s, NEG)
            m_old = m_s[hd]
            m_new = jnp.maximum(m_old, jnp.max(s, axis=-1, keepdims=True))
            alpha = jnp.exp(m_old - m_new)
            p = jnp.exp(s - m_new)
            m_s[hd] = m_new
            acc_s[hd] = alpha * acc_s[hd] + _dot(p.astype(BF16), vaug[kv_rows(jj), :])

    scores(0, 0)

    def body(k, _):
        scores(2 * k + 1, 1)
        softmax_pv(2 * k, 0, False)
        scores(2 * k + 2, 0)
        softmax_pv(2 * k + 1, 1, False)
        return 0

    lax.fori_loop(0, last // 2, body, 0)

    @pl.when(last % 2 == 0)
    def _():
        softmax_pv(last, 0, True)

    @pl.when(last % 2 == 1)
    def _():
        scores(last, 1)
        softmax_pv(last - 1, 0, False)
        softmax_pv(last, 1, True)
    acc_a, acc_b = acc_s[0], acc_s[1]
    o_ref[...] = jnp.where(head_a, acc_a / pltpu.roll(acc_a, HEAD_DIM, 1),
                           acc_b / pltpu.roll(acc_b, HEAD_DIM, 1)).astype(BF16)


def _moba(q, k, v, q0):
    B = q.shape[0]
    nq = q.shape[1] // MOBA_BLOCK
    skv = (q0 + nq) * MOBA_BLOCK
    nb = skv // MOBA_BLOCK
    assert nb <= HEAD_DIM and nb % 2 == 0 and skv <= k.shape[1]
    nbp = -(-nb // SUBLANES) * SUBLANES
    assert q0 % 2 == 0 and nq % 2 == 0
    blk = pl.BlockSpec((None, MOBA_PAIR, LANES), lambda b, h, i: (b, i, h))
    seq = pl.BlockSpec((None, skv, LANES), lambda b, h, i: (b, 0, h))
    return pl.pallas_call(
        functools.partial(_moba_kernel, q0),
        grid=(B, D_ATT // LANES, nq // 2),
        in_specs=[pl.BlockSpec((None, nq * MOBA_BLOCK, LANES), lambda b, h, i: (b, 0, h)), seq, seq],
        out_specs=blk,
        out_shape=jax.ShapeDtypeStruct(q.shape, BF16),
        scratch_shapes=[pltpu.VMEM((nbp, LANES), F32),
                        pltpu.VMEM((skv, LANES), BF16), pltpu.VMEM((skv, LANES), BF16),
                        pltpu.VMEM((skv, LANES), BF16), pltpu.VMEM((skv, LANES), BF16),
                        pltpu.VMEM((2, nq * MOBA_BLOCK, LANES), BF16),
                        pltpu.VMEM((2, MOBA_PAIR, 1), F32),
                        pltpu.VMEM((2, MOBA_PAIR, LANES), F32),
                        pltpu.VMEM((2, 2, MOBA_PAIR, MOBA_PAIR), F32)],
        compiler_params=pltpu.CompilerParams(
            dimension_semantics=("parallel", "parallel", "arbitrary"), vmem_limit_bytes=VMEM_LIMIT),
        name="moba",
    )(q, k, v)


MERGE_TS = 512


def _bf16_bits(x):
    b = pltpu.bitcast(x, jnp.int32)
    r = b + 0x7FFF + (lax.shift_right_logical(b, 16) & 1)
    return lax.shift_right_logical(r, 16)


def _merge_kernel(x_ref, ys_ref, at_ref, ga_ref, gb_ref, wa_ref, wb_ref, wo_ref, g_ref,
                  wq_ref, k1_ref, k2_ref, after_ref, x1_ref, hw_ref, idx_ref, gate_ref, sc_ref):
    del after_ref
    ya = _dot(ys_ref[...], wa_ref[...])
    yb = _dot(at_ref[...], wb_ref[...])
    merged = ga_ref[...].astype(F32) * ya + gb_ref[...].astype(F32) * yb
    x1 = x_ref[...] + _dot(merged.astype(BF16), wo_ref[...])
    x1_ref[...] = x1
    hq = _rms(x1, g_ref[...])
    hw_ref[...] = _pack_words(hq)
    qp = _dot(hq.astype(BF16), wq_ref[...])
    for h in range(PEER_HEADS):
        o = h * PEER_QDIM
        sc_ref[2 * h] = _dot_nt(k1_ref[h], qp[:, o:o + PEER_HALF], precision=HIGHEST)
        sc_ref[2 * h + 1] = _dot_nt(k2_ref[h], qp[:, o + PEER_HALF:o + PEER_QDIM], precision=HIGHEST)
    _topk_kernel(sc_ref, idx_ref, gate_ref)


def _merge(x, ys, att, ga, gb, t0, off, nt, after, w_proj_ssm, w_proj_att, w_out, g_ffn, peer_w_q,
           keys1, keys2):
    B = ys.shape[0]
    ts = min(MERGE_TS, nt)
    nblk = nt // ts
    i0 = (t0 + off) // ts
    o0 = off // ts
    tok = lambda d: pl.BlockSpec((None, ts, d), lambda b, i: (b, o0 + i, 0))
    row = lambda d: pl.BlockSpec((ts, d), lambda b, i: (b * nblk + i, 0))
    full = lambda shape: pl.BlockSpec(shape, lambda b, i: (0,) * len(shape))
    qd = PEER_HEADS * PEER_QDIM
    return pl.pallas_call(
        _merge_kernel,
        grid=(B, nblk),
        in_specs=[pl.BlockSpec((None, ts, D_MODEL), lambda b, i: (b, i0 + i, 0)),
                  tok(D_SSM), tok(D_ATT), tok(D_MODEL), tok(D_MODEL),
                  full((D_SSM, D_MODEL)), full((D_ATT, D_MODEL)), full((D_MODEL, D_MODEL)),
                  full((1, D_MODEL)), full((D_MODEL, qd)),
                  full((PEER_HEADS, PEER_KEYS, PEER_HALF)), full((PEER_HEADS, PEER_KEYS, PEER_HALF)),
                  pl.BlockSpec(memory_space=pl.ANY)],
        out_specs=[row(D_MODEL), row(D_MODEL // 2), row(PEER_SEL), row(PEER_SEL)],
        out_shape=[jax.ShapeDtypeStruct((B * nt, D_MODEL), F32),
                   jax.ShapeDtypeStruct((B * nt, D_MODEL // 2), jnp.int32),
                   jax.ShapeDtypeStruct((B * nt, PEER_SEL), jnp.int32),
                   jax.ShapeDtypeStruct((B * nt, PEER_SEL), F32)],
        scratch_shapes=[pltpu.VMEM((2 * PEER_HEADS, PEER_KEYS, ts), F32)],
        compiler_params=pltpu.CompilerParams(
            dimension_semantics=("parallel", "parallel"), vmem_limit_bytes=VMEM_LIMIT),
        name="merge",
    )(x, ys, att, ga, gb, w_proj_ssm, w_proj_att, w_out, g_ffn, peer_w_q, keys1, keys2, after)


def _top_rows(s, row, k):
    vals, idxs = [], []
    for _ in range(k):
        m = jnp.max(s, axis=0, keepdims=True)
        idx = jnp.min(jnp.where(s == m, row, s.shape[0]), axis=0, keepdims=True)
        vals.append(m)
        idxs.append(idx)
        s = jnp.where(row == idx, -jnp.inf, s)
    return vals, idxs


def _stack_rows(rows, row16):
    acc = jnp.zeros(row16.shape, rows[0].dtype)
    for r, v in enumerate(rows):
        acc = jnp.where(row16 == r, v, acc)
    return acc


def _topk_kernel(sc_ref, idx_ref, gate_ref):
    ts = sc_ref.shape[-1]
    row = lax.broadcasted_iota(jnp.int32, (PEER_KEYS, ts), 0).astype(F32)
    row16 = lax.broadcasted_iota(jnp.int32, (PEER_TOPK, ts), 0)
    row8 = lax.broadcasted_iota(jnp.int32, (SUBLANES, ts), 0)
    counts = [PEER_TOPK // (i + 1) for i in range(PEER_TOPK)]
    heights = [PEER_TOPK if c > SUBLANES else SUBLANES for c in counts]
    n_cand = sum(heights)
    rowc = lax.broadcasted_iota(jnp.int32, (n_cand, ts), 0).astype(F32)
    gate_rows, eid_rows = [], []
    for h in range(PEER_HEADS):
        v1, i1 = _top_rows(sc_ref[2 * h], row, PEER_TOPK)
        v2, i2 = _top_rows(sc_ref[2 * h + 1], row, PEER_TOPK)
        v2s = _stack_rows(v2, row16)
        i2s = _stack_rows(i2, row16)
        cand, eid = [], []
        for i in range(PEER_TOPK):
            n = heights[i]
            cand.append(jnp.where((row16 if n == PEER_TOPK else row8) < counts[i],
                                  v1[i] + v2s[:n], -jnp.inf))
            eid.append(i1[i] * PEER_KEYS + i2s[:n])
        cand = jnp.concatenate(cand, axis=0)
        eid = jnp.concatenate(eid, axis=0)
        tops, picks = [], []
        for _ in range(PEER_TOPK):
            m = jnp.max(cand, axis=0, keepdims=True)
            pos = jnp.min(jnp.where(cand == m, rowc, n_cand), axis=0, keepdims=True)
            hit = rowc == pos
            picks.append(jnp.max(jnp.where(hit, eid, -1.0), axis=0, keepdims=True))
            tops.append(m)
            cand = jnp.where(hit, -jnp.inf, cand)
        top = _stack_rows(tops, row16)
        p = jnp.exp(top - jnp.max(top, axis=0, keepdims=True))
        gate_rows.append(p / jnp.sum(p, axis=0, keepdims=True))
        eid_rows.append(_stack_rows(picks, row16))
    gate_ref[...] = jnp.transpose(jnp.concatenate(gate_rows, axis=0))
    idx_ref[...] = jnp.transpose(jnp.concatenate(eid_rows, axis=0)).astype(jnp.int32)


SC_CORES = 2
SC_SUBCORES = 16
SC_LANES = 16
SC_WORKERS = SC_CORES * SC_SUBCORES
PEER_CH = SC_LANES
PEER_NCH = PEER_SEL // PEER_CH
PEER_WORDS = D_MODEL // 2
PEER_NWG = PEER_WORDS // SC_LANES
PEER_RING = 4
PEER_QUAD = 4
HI_MASK = -65536
GELU_C = 0.7978845608028654


def _gelu_tanh_via_exp(x):
    z = GELU_C * (x + 0.044715 * (x * x * x))
    t = 1.0 - 2.0 / (jnp.exp(2.0 * z) + 1.0)
    return 0.5 * x * (1.0 + t)


def _unpack_pair(w):
    lo = plsc.bitcast(lax.shift_left(w, 16), F32)
    hi = plsc.bitcast(lax.bitwise_and(w, HI_MASK), F32)
    return lo, hi


def _peer_sc_body(idx_hbm, gate_hbm, h_hbm, uv_hbm, after_hbm, o_hbm,
                  idx_v, gate_v, h_v, buf, out_v, gsem, msem, osem):
    n_tok = o_hbm.shape[0] // SC_WORKERS
    base = (lax.axis_index("s") * SC_CORES + lax.axis_index("c")) * n_tok
    lane = lax.iota(jnp.int32, SC_LANES)
    zero_rows = jnp.zeros((SC_LANES,), jnp.int32)

    def meta_copies(tok, s):
        return (pltpu.make_async_copy(idx_hbm.at[tok], idx_v.at[s], msem.at[s]),
                pltpu.make_async_copy(gate_hbm.at[tok], gate_v.at[s], msem.at[s]),
                pltpu.make_async_copy(h_hbm.at[tok], h_v.at[s], msem.at[s]))

    def gather(slot, rows):
        return pltpu.make_async_copy(uv_hbm.at[rows], buf.at[slot], gsem.at[slot])

    def token(t, carry):
        s = t % 2
        tok = base + t
        nxt = base + jnp.minimum(t + 1, n_tok - 1)
        for cp in meta_copies(nxt, 1 - s):
            cp.start()

        @pl.when(t >= 2)
        def _():
            pltpu.make_async_copy(out_v.at[s], o_hbm.at[tok], osem.at[s]).wait()

        def chunk(c, carry):
            slot = c % PEER_RING
            gather(slot, zero_rows).wait()

            def dot_step(q, accs):
                cols = [pl.ds(pl.multiple_of((q * PEER_QUAD + j) * SC_LANES, SC_LANES), SC_LANES)
                        for j in range(PEER_QUAD)]
                hs = [plsc.bitcast(h_v[s, col], BF16) for col in cols]
                out = []
                for r in range(PEER_CH):
                    p = plsc.bitcast(buf[slot, r, cols[0]], BF16) * hs[0]
                    for j in range(1, PEER_QUAD):
                        p = p + plsc.bitcast(buf[slot, r, cols[j]], BF16) * hs[j]
                    lo, hi = _unpack_pair(plsc.bitcast(p, jnp.int32))
                    out.append(accs[r] + lo + hi)
                return tuple(out)

            accs = lax.fori_loop(0, PEER_NWG // PEER_QUAD, dot_step,
                                 tuple(jnp.zeros((SC_LANES,), F32) for _ in range(PEER_CH)))
            tot = jnp.zeros((SC_LANES,), F32)
            for r in range(PEER_CH):
                tot = jnp.where(lane == r, jnp.sum(accs[r]), tot)
            rows = pl.ds(pl.multiple_of(c * PEER_CH, PEER_CH), PEER_CH)
            wvec = gate_v[s, rows] * _gelu_tanh_via_exp(tot)
            ws = []
            for r in range(PEER_CH):
                w = wvec.at[jnp.full((SC_LANES,), r, jnp.int32)].get(mode="promise_in_bounds")
                ws.append(plsc.pack(w, w, format=plsc.PackFormat.INTERLEAVED,
                                    preferred_element_type=BF16))
            first = c == 0

            @plsc.parallel_loop(0, PEER_NWG, unroll=2)
            def acc_step(g):
                col = pl.ds(pl.multiple_of(g * SC_LANES, SC_LANES), SC_LANES)
                col_v = pl.ds(pl.multiple_of(PEER_WORDS + g * SC_LANES, SC_LANES), SC_LANES)
                o_lo = jnp.where(first, 0.0, out_v[s, col])
                o_hi = jnp.where(first, 0.0, out_v[s, col_v])
                for r0 in range(0, PEER_CH, PEER_QUAD):
                    p = plsc.bitcast(buf[slot, r0, col_v], BF16) * ws[r0]
                    for r in range(r0 + 1, r0 + PEER_QUAD):
                        p = p + plsc.bitcast(buf[slot, r, col_v], BF16) * ws[r]
                    lo, hi = _unpack_pair(plsc.bitcast(p, jnp.int32))
                    o_lo = o_lo + lo
                    o_hi = o_hi + hi
                out_v[s, col] = o_lo
                out_v[s, col_v] = o_hi

            @pl.when(c == PEER_NCH - PEER_RING)
            def _():
                for cp in meta_copies(nxt, 1 - s):
                    cp.wait()

            ahead = c + PEER_RING
            src = jnp.where(ahead < PEER_NCH, s, 1 - s)
            nrows = idx_v[src, pl.ds(pl.multiple_of((ahead % PEER_NCH) * PEER_CH, PEER_CH), PEER_CH)]
            gather(slot, nrows).start()
            return carry

        lax.fori_loop(0, PEER_NCH, chunk, 0)
        pltpu.make_async_copy(out_v.at[s], o_hbm.at[tok], osem.at[s]).start()
        return carry

    for cp in meta_copies(base, 0):
        cp.start()
    for cp in meta_copies(base, 0):
        cp.wait()
    for c in range(PEER_RING):
        gather(c, idx_v[0, pl.ds(c * PEER_CH, PEER_CH)]).start()
    lax.fori_loop(0, n_tok, token, 0)
    for c in range(PEER_RING):
        gather(c, zero_rows).wait()
    for s in range(2):
        pltpu.make_async_copy(out_v.at[s], o_hbm.at[base], osem.at[s]).wait()


PACK_ROWS = 1024


def _pack_words(x):
    half = x.shape[1] // 2
    return _bf16_bits(x[:, :half]) | lax.shift_left(_bf16_bits(x[:, half:]), 16)


def _pack_tables_kernel(u_ref, v_ref, o_ref):
    o_ref[:, :PEER_WORDS] = _pack_words(u_ref[...])
    o_ref[:, PEER_WORDS:] = _pack_words(v_ref[...])


def _pack_tables(peer_u, peer_v):
    n = peer_u.shape[0]
    rows = min(PACK_ROWS, n)
    spec = pl.BlockSpec((rows, D_MODEL), lambda i: (i, 0))
    return pl.pallas_call(
        _pack_tables_kernel,
        grid=(n // rows,),
        in_specs=[spec, spec],
        out_specs=spec,
        out_shape=jax.ShapeDtypeStruct((n, D_MODEL), jnp.int32),
        compiler_params=pltpu.CompilerParams(
            dimension_semantics=("parallel",), vmem_limit_bytes=VMEM_LIMIT),
        name="pack_tables",
    )(peer_u, peer_v)


def _peer(idx, h_words, gates, uv_words, after):
    T = h_words.shape[0]
    assert T % (2 * SC_WORKERS) == 0
    mesh = plsc.VectorSubcoreMesh(core_axis_name="c", subcore_axis_name="s",
                                  num_cores=SC_CORES, num_subcores=SC_SUBCORES)
    return pl.kernel(
        _peer_sc_body,
        out_type=jax.ShapeDtypeStruct((T, D_MODEL), F32),
        mesh=mesh,
        scratch_types=[
            pltpu.VMEM((2, PEER_SEL), jnp.int32), pltpu.VMEM((2, PEER_SEL), F32),
            pltpu.VMEM((2, PEER_WORDS), jnp.int32),
            pltpu.VMEM((PEER_RING, PEER_CH, 2 * PEER_WORDS), jnp.int32),
            pltpu.VMEM((2, D_MODEL), F32),
            pltpu.SemaphoreType.DMA((PEER_RING,)),
            pltpu.SemaphoreType.DMA((2,)), pltpu.SemaphoreType.DMA((2,)),
        ],
        compiler_params=pltpu.CompilerParams(needs_layout_passes=False),
        name="peer_sc",
    )(idx, gates, h_words, uv_words, after)


FINAL_TS = 512


def _final_kernel(x1_ref, pe_ref, p_ref, gp_ref, wg_ref, wp_ref, gf_ref, o_ref):
    x2 = x1_ref[...] + pe_ref[...]
    e = _dot(p_ref[...].astype(BF16), wp_ref[...])
    gate = jax.nn.sigmoid(_dot(_rms(x2, gp_ref[...]).astype(BF16), wg_ref[...]))
    o_ref[...] = _rms(x2 + gate * e, gf_ref[...])


def _final(x1, peer_out, p, t0, nt, g_ple, ple_w_gate, ple_w_proj, g_final):
    B = p.shape[0]
    ts = min(FINAL_TS, nt)
    nblk = nt // ts
    i0 = t0 // ts
    row = lambda d: pl.BlockSpec((ts, d), lambda b, i: (b * nblk + i, 0))
    full = lambda shape: pl.BlockSpec(shape, lambda b, i: (0,) * len(shape))
    return pl.pallas_call(
        _final_kernel,
        grid=(B, nblk),
        in_specs=[row(D_MODEL), row(D_MODEL),
                  pl.BlockSpec((None, ts, D_PLE), lambda b, i: (b, i0 + i, 0)),
                  full((1, D_MODEL)), full((D_MODEL, D_MODEL)), full((D_PLE, D_MODEL)),
                  full((1, D_MODEL))],
        out_specs=pl.BlockSpec((None, ts, D_MODEL), lambda b, i: (b, i, 0)),
        out_shape=jax.ShapeDtypeStruct((B, nt, D_MODEL), F32),
        compiler_params=pltpu.CompilerParams(
            dimension_semantics=("parallel", "parallel"), vmem_limit_bytes=VMEM_LIMIT),
        name="final",
    )(x1, peer_out, p, g_ple, ple_w_gate, ple_w_proj, g_final)


CHUNK_STEPS = (512, 512, 512, 512, 512, 1024, 1024, 1024, 1024, 1024, 512)


def kernel(x, p, positions, g_mix, w_in, ssm_log_dt, ssm_a_re, ssm_a_im, ssm_b_re, ssm_b_im,
           ssm_c_re, ssm_c_im, ssm_d, ssm_w_glu, w_proj_ssm, w_proj_att, w_out, g_ffn,
           peer_w_q, peer_keys1, peer_keys2, peer_u, peer_v, g_ple, ple_w_gate, ple_w_proj,
           g_final):
    B, S, _ = x.shape
    assert w_in.shape[0] == 1, "the final rmsnorm is fused into the single layer's last stage"
    steps = CHUNK_STEPS if sum(CHUNK_STEPS) == S else (S,)
    i = 0
    tables = _s5_tables(ssm_log_dt[i], ssm_a_re[i], ssm_a_im[i], ssm_b_re[i], ssm_b_im[i],
                        ssm_c_re[i], ssm_c_im[i])
    w_in_b, w_glu_b = w_in[i].astype(BF16), ssm_w_glu[i].astype(BF16)
    d_skip = ssm_d[i].reshape(1, D_SSM).astype(F32)
    merge_w = (w_proj_ssm[i].astype(BF16), w_proj_att[i].astype(BF16), w_out[i].astype(BF16),
               g_ffn[i].reshape(1, D_MODEL), peer_w_q[i].astype(BF16), peer_keys1[i], peer_keys2[i])
    final_w = (g_ple[i].reshape(1, D_MODEL), ple_w_gate[i].astype(BF16),
               ple_w_proj[i].astype(BF16), g_final.reshape(1, D_MODEL))
    uv_words = _pack_tables(peer_u[i], peer_v[i])
    k_all = jnp.zeros((B, S, D_ATT), BF16)
    v_all = jnp.zeros((B, S, D_ATT), BF16)
    carry = jnp.zeros((2, SUBLANES, D_STATE), F32)
    outs = []
    t0 = 0
    after = (carry, carry)
    peers = []
    for nt in steps:
        u_sb, q, k, v, ga, gb = _in_proj(x, positions, g_mix[i], w_in_b, t0, nt, after)
        k_all = lax.dynamic_update_slice(k_all, k, (0, t0, 0))
        v_all = lax.dynamic_update_slice(v_all, v, (0, t0, 0))
        ys, carry = _s5(u_sb, carry, tables, d_skip, w_glu_b, B)
        att = _moba(q, k_all, v_all, t0 // MOBA_BLOCK)
        subs = ((0, nt // 2), (nt // 2, nt // 2)) if t0 == 0 and nt >= 512 else ((0, nt),)
        for off, n in subs:
            x1, h_words, idx, gates = _merge(x, ys, att, ga, gb, t0, off, n,
                                             peers[-2] if len(peers) > 1 else carry, *merge_w)
            peer_out = _peer(idx, h_words, gates, uv_words,
                             peers[-1] if t0 + off + n == S else carry)
            peers.append(peer_out)
            outs.append(_final(x1, peer_out, p[i], t0 + off, n, *final_w))
        after = (gates, outs[-3] if len(outs) > 2 else carry)
        t0 += nt
    return jnp.concatenate(outs, axis=1)
```

```python
import functools

import jax
import jax.numpy as jnp
from jax import lax
from jax.experimental import pallas as pl
from jax.experimental.pallas import tpu as pltpu
from jax.experimental.pallas import tpu_sc as plsc

F32 = jnp.float32
BF16 = jnp.bfloat16

D_MODEL = 1024
D_SSM = 512
SSM_GROUPS = 32
SSM_STATE = 64
D_STATE = SSM_GROUPS * SSM_STATE
HEAD_DIM = 64
D_ATT = 512
ROT_DIM = 16
ROPE_THETA = 500000.0
MOBA_BLOCK = 256
MOBA_TOPK = 3
PEER_HEADS = 8
PEER_KEYS = 128
PEER_QDIM = 256
PEER_HALF = 128
PEER_TOPK = 16
PEER_SEL = PEER_HEADS * PEER_TOPK
D_PLE = 256
EPS = 1e-6
NEG = -1e30
LANES = 128
SUBLANES = 8
VMEM_LIMIT = 48 * 1024 * 1024
HIGHEST = lax.Precision.HIGHEST


def _rms(x, g):
    return x * lax.rsqrt(jnp.mean(x * x, axis=-1, keepdims=True) + EPS) * g


def _dot(a, b):
    return jnp.dot(a, b, preferred_element_type=F32)


def _dot_nt(a, b, precision=None):
    return lax.dot_general(a, b, (((1,), (1,)), ((), ())), precision=precision,
                           preferred_element_type=F32)


IN_TS = 512


def _in_proj_kernel(x_ref, pos_ref, g_ref, w_ref, invf_ref, after_a, after_b,
                    u_ref, q_ref, k_ref, v_ref, ga_ref, gb_ref):
    del after_a, after_b
    h = _rms(x_ref[...], g_ref[...]).astype(BF16)

    def proj(lo, hi):
        return _dot(h, w_ref[:, lo:hi])

    u_ref[...] = proj(0, D_SSM).astype(BF16)
    ang = pos_ref[...].astype(F32) * invf_ref[...]
    cos = jnp.cos(ang)
    sin = jnp.sin(ang)
    lane = lax.broadcasted_iota(jnp.int32, (1, LANES), 1) % HEAD_DIM
    half = ROT_DIM // 2
    sin_hi = jnp.where((lane >= half) & (lane < ROT_DIM), sin, 0.0)
    sin_lo = jnp.where(lane < half, -sin, 0.0)
    reps = D_ATT // LANES
    cos4 = jnp.concatenate([cos] * reps, axis=1)
    sin_hi4 = jnp.concatenate([sin_hi] * reps, axis=1)
    sin_lo4 = jnp.concatenate([sin_lo] * reps, axis=1)

    def rope(t):
        return (t * cos4 + pltpu.roll(t, half, 1) * sin_hi4
                + pltpu.roll(t, D_ATT - half, 1) * sin_lo4)

    q = rope(proj(D_SSM, D_SSM + D_ATT))
    q_ref[...] = (q * (HEAD_DIM ** -0.5)).astype(BF16)
    k_ref[...] = rope(proj(D_SSM + D_ATT, D_SSM + 2 * D_ATT)).astype(BF16)
    v_ref[...] = proj(D_SSM + 2 * D_ATT, D_SSM + 3 * D_ATT).astype(BF16)
    o = D_SSM + 3 * D_ATT
    ga_ref[...] = jax.nn.sigmoid(proj(o, o + D_MODEL)).astype(BF16)
    gb_ref[...] = jax.nn.sigmoid(proj(o + D_MODEL, o + 2 * D_MODEL)).astype(BF16)


def _in_proj(x, positions, g_mix, w_in, t0, nt, after):
    B, S, _ = x.shape
    ts = min(IN_TS, nt)
    assert nt % ts == 0 and t0 % ts == 0
    i0 = t0 // ts
    inv_freq = ROPE_THETA ** (-jnp.arange(0, ROT_DIM, 2, dtype=F32) / ROT_DIM)
    lane = jnp.arange(LANES) % HEAD_DIM
    invf = jnp.where(lane < ROT_DIM, inv_freq[lane % (ROT_DIM // 2)], 0.0).reshape(1, LANES)
    d_in = w_in.shape[1]
    src = lambda d: pl.BlockSpec((None, ts, d), lambda b, i: (b, i0 + i, 0))
    tok = lambda d: pl.BlockSpec((None, ts, d), lambda b, i: (b, i, 0))
    full = lambda shape: pl.BlockSpec(shape, lambda b, i: (0,) * len(shape))
    return pl.pallas_call(
        _in_proj_kernel,
        grid=(B, nt // ts),
        in_specs=[src(D_MODEL), src(1), full((1, D_MODEL)), full((D_MODEL, d_in)), full((1, LANES)),
                  pl.BlockSpec(memory_space=pl.ANY), pl.BlockSpec(memory_space=pl.ANY)],
        out_specs=[pl.BlockSpec((ts, D_SSM), lambda b, i: (i, b)),
                   tok(D_ATT), tok(D_ATT), tok(D_ATT), tok(D_MODEL), tok(D_MODEL)],
        out_shape=[jax.ShapeDtypeStruct((nt, B * D_SSM), BF16),
                   jax.ShapeDtypeStruct((B, nt, D_ATT), BF16),
                   jax.ShapeDtypeStruct((B, nt, D_ATT), BF16),
                   jax.ShapeDtypeStruct((B, nt, D_ATT), BF16),
                   jax.ShapeDtypeStruct((B, nt, D_MODEL), BF16),
                   jax.ShapeDtypeStruct((B, nt, D_MODEL), BF16)],
        compiler_params=pltpu.CompilerParams(
            dimension_semantics=("parallel", "parallel"), vmem_limit_bytes=VMEM_LIMIT),
        name="in_proj",
    )(x, positions.reshape(B, S, 1), g_mix.reshape(1, D_MODEL), w_in, invf, *after)


S5_TS = 128
S5_BATCH = 4
S5_COLS = 512


def _s5_kernel(u_ref, c0_ref, bre_ref, bim_ref, a1r_ref, a1i_ref, pr_ref, pi_ref,
               cre_ref, cim_ref, d_ref, wglu_ref, y_ref, c1_ref,
               xr, xi, cr, ci, ysc):
    rows = xr.shape[0]
    ts = rows // S5_BATCH

    @pl.when(pl.program_id(0) == 0)
    def _():
        cr[...] = c0_ref[0]
        ci[...] = c0_ref[1]

    u = u_ref[...]
    for cb in range(D_STATE // S5_COLS):
        sl = slice(cb * S5_COLS, (cb + 1) * S5_COLS)
        u_cb = u[:, cb * LANES:(cb + 1) * LANES]
        xr[:, sl] = _dot(u_cb, bre_ref[cb])
        xi[:, sl] = _dot(u_cb, bim_ref[cb])

    hi_rows = lax.broadcasted_iota(jnp.int32, (SUBLANES, S5_COLS), 0) >= S5_BATCH
    for cb in range(D_STATE // S5_COLS):
        sl = slice(cb * S5_COLS, (cb + 1) * S5_COLS)
        a_r, a_i = a1r_ref[:, sl], a1i_ref[:, sl]
        p_r, p_i = pr_ref[:, sl], pi_ref[:, sl]

        def body(t, carry):
            c_r, c_i = carry
            r0 = pl.multiple_of(t * SUBLANES, SUBLANES)
            x_r = xr[pl.ds(r0, SUBLANES), sl]
            x_i = xi[pl.ds(r0, SUBLANES), sl]
            s_r = pltpu.roll(x_r, S5_BATCH, 0)
            s_i = pltpu.roll(x_i, S5_BATCH, 0)
            h_r = x_r + (a_r * s_r - a_i * s_i) + (p_r * c_r - p_i * c_i)
            h_i = x_i + (a_r * s_i + a_i * s_r) + (p_r * c_i + p_i * c_r)
            xr[pl.ds(r0, SUBLANES), sl] = h_r
            xi[pl.ds(r0, SUBLANES), sl] = h_i
            n_r = jnp.where(hi_rows, h_r, pltpu.roll(h_r, S5_BATCH, 0))
            n_i = jnp.where(hi_rows, h_i, pltpu.roll(h_i, S5_BATCH, 0))
            return n_r, n_i

        c_r, c_i = lax.fori_loop(0, rows // SUBLANES, body, (cr[:, sl], ci[:, sl]), unroll=2)
        cr[:, sl] = c_r
        ci[:, sl] = c_i

    y = jnp.concatenate(
        [_dot(xr[:, cb * S5_COLS:(cb + 1) * S5_COLS].astype(BF16), cre_ref[cb])
         - _dot(xi[:, cb * S5_COLS:(cb + 1) * S5_COLS].astype(BF16), cim_ref[cb])
         for cb in range(D_STATE // S5_COLS)], axis=1) + d_ref[...] * u.astype(F32)
    y = jax.nn.gelu(y)
    y = y * jax.nn.sigmoid(_dot(y.astype(BF16), wglu_ref[...]))
    for c in range(D_SSM // LANES):
        ysc[c] = y[:, c * LANES:(c + 1) * LANES]
    for b in range(S5_BATCH):
        for c in range(D_SSM // LANES):
            y_ref[b, :, c * LANES:(c + 1) * LANES] = (
                ysc[c, pl.ds(b, ts, stride=S5_BATCH), :].astype(BF16))

    @pl.when(pl.program_id(0) == pl.num_programs(0) - 1)
    def _():
        c1_ref[0] = cr[...]
        c1_ref[1] = ci[...]


def _s5_tables(log_dt, a_re, a_im, b_re, b_im, c_re, c_im):
    dt = jnp.exp(log_dt.astype(F32))[:, None]
    ar, ai = a_re.astype(F32), a_im.astype(F32)
    mag = jnp.exp(dt * ar)
    abar_re, abar_im = mag * jnp.cos(dt * ai), mag * jnp.sin(dt * ai)
    den = ar * ar + ai * ai
    nr, ni = abar_re - 1.0, abar_im
    f_re = (nr * ar + ni * ai) / den
    f_im = (ni * ar - nr * ai) / den
    br, bi = b_re.astype(F32), b_im.astype(F32)
    bb_re = f_re[..., None] * br - f_im[..., None] * bi
    bb_im = f_re[..., None] * bi + f_im[..., None] * br
    nblk = D_STATE // S5_COLS
    gpb = SSM_GROUPS // nblk
    eye = jnp.eye(gpb, dtype=F32)

    def in_blocks(bb):
        b4 = bb.reshape(nblk, gpb, SSM_STATE, -1)
        return jnp.einsum('bgnc,gh->bgchn', b4, eye).reshape(nblk, D_SSM // nblk, S5_COLS)

    def out_blocks(c):
        c4 = c.astype(F32).reshape(nblk, gpb, -1, SSM_STATE)
        return jnp.einsum('bgcn,gh->bgnhc', c4, eye).reshape(nblk, S5_COLS, D_SSM // nblk)

    a_r = abar_re.reshape(1, D_STATE)
    a_i = abar_im.reshape(1, D_STATE)
    a2_r = a_r * a_r - a_i * a_i
    a2_i = 2.0 * a_r * a_i
    hi = (jnp.arange(SUBLANES) >= S5_BATCH)[:, None]
    a1r = jnp.where(hi, a_r, 0.0)
    a1i = jnp.where(hi, a_i, 0.0)
    p_r = jnp.where(hi, a2_r, a_r)
    p_i = jnp.where(hi, a2_i, a_i)
    return (in_blocks(bb_re).astype(BF16), in_blocks(bb_im).astype(BF16),
            a1r, a1i, p_r, p_i,
            out_blocks(c_re).astype(BF16), out_blocks(c_im).astype(BF16))


def _s5(u_sb, carry, tables, d_skip, w_glu, B):
    assert B == S5_BATCH
    nt = u_sb.shape[0]
    ts = min(S5_TS, nt)
    rows = ts * B
    bre, bim, a1r, a1i, p_r, p_i, cre, cim = tables
    full = lambda shape: pl.BlockSpec(shape, lambda i: (0,) * len(shape))
    return pl.pallas_call(
        _s5_kernel,
        grid=(nt // ts,),
        in_specs=[pl.BlockSpec((rows, D_SSM), lambda i: (i, 0)),
                  full((2, SUBLANES, D_STATE)),
                  full(bre.shape), full(bim.shape),
                  full((SUBLANES, D_STATE)), full((SUBLANES, D_STATE)),
                  full((SUBLANES, D_STATE)), full((SUBLANES, D_STATE)),
                  full(cre.shape), full(cim.shape),
                  full((1, D_SSM)), full((D_SSM, D_SSM))],
        out_specs=[pl.BlockSpec((B, ts, D_SSM), lambda i: (0, i, 0)),
                   full((2, SUBLANES, D_STATE))],
        out_shape=[jax.ShapeDtypeStruct((B, nt, D_SSM), BF16),
                   jax.ShapeDtypeStruct((2, SUBLANES, D_STATE), F32)],
        scratch_shapes=[pltpu.VMEM((rows, D_STATE), F32), pltpu.VMEM((rows, D_STATE), F32),
                        pltpu.VMEM((SUBLANES, D_STATE), F32), pltpu.VMEM((SUBLANES, D_STATE), F32),
                        pltpu.VMEM((D_SSM // LANES, rows, LANES), F32)],
        compiler_params=pltpu.CompilerParams(
            dimension_semantics=("arbitrary",), vmem_limit_bytes=VMEM_LIMIT),
        name="s5",
    )(u_sb.reshape(nt * B, D_SSM), carry, bre, bim, a1r, a1i, p_r, p_i, cre, cim, d_skip, w_glu)


MOBA_PAIR = 2 * MOBA_BLOCK


def _moba_kernel(q0, q_ref, k_ref, v_ref, o_ref, kmean, kaug_a, kaug_b, vaug_a, vaug_b, qaug,
                 m_s, acc_s, s_buf):
    last = pl.program_id(2) + q0 // 2
    nb = k_ref.shape[0] // MOBA_BLOCK
    nbp = kmean.shape[0]
    lane = lax.broadcasted_iota(jnp.int32, (1, LANES), 1)
    head_a = lane < HEAD_DIM

    @pl.when(pl.program_id(2) == 0)
    def _():
        kmean[...] = jnp.zeros_like(kmean)
        for j in range(nb):
            rows = pl.ds(j * MOBA_BLOCK, MOBA_BLOCK)
            kj = k_ref[rows, :].astype(F32)
            vj = v_ref[rows, :].astype(F32)
            kmean[j:j + 1, :] = jnp.sum(kj, axis=0, keepdims=True) * (1.0 / MOBA_BLOCK)
            kaug_a[rows, :] = jnp.where(head_a, kj, jnp.where(lane - HEAD_DIM == j, 1.0, 0.0)).astype(BF16)
            kaug_b[rows, :] = jnp.where(head_a, jnp.where(lane == j, 1.0, 0.0), kj).astype(BF16)
            vaug_a[rows, :] = jnp.where(head_a, vj, 1.0).astype(BF16)
            vaug_b[rows, :] = jnp.where(head_a, 1.0, vj).astype(BF16)
        blk_row = lax.broadcasted_iota(jnp.int32, (nbp, MOBA_BLOCK), 0)
        for t in range(q_ref.shape[0] // MOBA_BLOCK):
            qt = q0 + t
            qf = q_ref[t * MOBA_BLOCK:(t + 1) * MOBA_BLOCK, :].astype(F32)
            for hd, is_a in enumerate((True, False)):
                mine = head_a if is_a else jnp.logical_not(head_a)
                q_own = jnp.where(mine, qf, 0.0)
                g = _dot_nt(kmean[...], q_own, precision=HIGHEST)
                g = jnp.where(blk_row < qt, g, NEG)
                sel = jnp.zeros(g.shape, F32)
                for _ in range(MOBA_TOPK):
                    m = jnp.max(g, axis=0, keepdims=True)
                    idx = jnp.min(jnp.where(g == m, blk_row, nbp), axis=0, keepdims=True)
                    hit = blk_row == idx
                    sel = jnp.where(hit, jnp.where(idx < qt, 1.0, 0.0), sel)
                    g = jnp.where(hit, -jnp.inf, g)
                bias_t = jnp.where(sel > 0.0, 0.0, jnp.where(blk_row == qt, 0.0, NEG))
                bias_t = jnp.concatenate([bias_t, jnp.full((LANES - nbp, MOBA_BLOCK), NEG, F32)], axis=0)
                bias = jnp.transpose(bias_t)
                if is_a:
                    bias = pltpu.roll(bias, HEAD_DIM, 1)
                qaug[hd, t * MOBA_BLOCK:(t + 1) * MOBA_BLOCK, :] = jnp.where(mine, qf, bias).astype(BF16)

    tile_rows = pl.ds(pl.multiple_of(pl.program_id(2) * MOBA_PAIR, MOBA_PAIR), MOBA_PAIR)
    q_augs = [qaug[0, tile_rows, :], qaug[1, tile_rows, :]]

    m_s[...] = jnp.full(m_s.shape, -jnp.inf, F32)
    acc_s[...] = jnp.zeros_like(acc_s)
    qpos = last * MOBA_PAIR + lax.broadcasted_iota(jnp.int32, (MOBA_PAIR, MOBA_PAIR), 0)
    col = lax.broadcasted_iota(jnp.int32, (MOBA_PAIR, MOBA_PAIR), 1)

    def kv_rows(jj):
        return pl.ds(pl.multiple_of(jj * MOBA_PAIR, MOBA_PAIR), MOBA_PAIR)

    def scores(jj, slot):
        for hd, kaug in enumerate((kaug_a, kaug_b)):
            s_buf[slot, hd] = _dot_nt(q_augs[hd], kaug[kv_rows(jj), :])

    def softmax_pv(jj, slot, causal):
        for hd, vaug in enumerate((vaug_a, vaug_b)):
            s = s_buf[slot, hd]
            if causal:
                s = jnp.where(jj * MOBA_PAIR + col <= qpos, s, NEG)
            m_old = m_s[hd]
            m_new = jnp.maximum(m_old, jnp.max(s, axis=-1, keepdims=True))
            alpha = jnp.exp(m_old - m_new)
            p = jnp.exp(s - m_new)
            m_s[hd] = m_new
            acc_s[hd] = alpha * acc_s[hd] + _dot(p.astype(BF16), vaug[kv_rows(jj), :])

    scores(0, 0)

    def body(k, _):
        scores(2 * k + 1, 1)
        softmax_pv(2 * k, 0, False)
        scores(2 * k + 2, 0)
        softmax_pv(2 * k + 1, 1, False)
        return 0

    lax.fori_loop(0, last // 2, body, 0)

    @pl.when(last % 2 == 0)
    def _():
        softmax_pv(last, 0, True)

    @pl.when(last % 2 == 1)
    def _():
        scores(last, 1)
        softmax_pv(last - 1, 0, False)
        softmax_pv(last, 1, True)
    acc_a, acc_b = acc_s[0], acc_s[1]
    o_ref[...] = jnp.where(head_a, acc_a / pltpu.roll(acc_a, HEAD_DIM, 1),
                           acc_b / pltpu.roll(acc_b, HEAD_DIM, 1)).astype(BF16)


def _moba(q, k, v, q0):
    B = q.shape[0]
    nq = q.shape[1] // MOBA_BLOCK
    skv = (q0 + nq) * MOBA_BLOCK
    nb = skv // MOBA_BLOCK
    assert nb <= HEAD_DIM and nb % 2 == 0 and skv <= k.shape[1]
    nbp = -(-nb // SUBLANES) * SUBLANES
    assert q0 % 2 == 0 and nq % 2 == 0
    blk = pl.BlockSpec((None, MOBA_PAIR, LANES), lambda b, h, i: (b, i, h))
    seq = pl.BlockSpec((None, skv, LANES), lambda b, h, i: (b, 0, h))
    return pl.pallas_call(
        functools.partial(_moba_kernel, q0),
        grid=(B, D_ATT // LANES, nq // 2),
        in_specs=[pl.BlockSpec((None, nq * MOBA_BLOCK, LANES), lambda b, h, i: (b, 0, h)), seq, seq],
        out_specs=blk,
        out_shape=jax.ShapeDtypeStruct(q.shape, BF16),
        scratch_shapes=[pltpu.VMEM((nbp, LANES), F32),
                        pltpu.VMEM((skv, LANES), BF16), pltpu.VMEM((skv, LANES), BF16),
                        pltpu.VMEM((skv, LANES), BF16), pltpu.VMEM((skv, LANES), BF16),
                        pltpu.VMEM((2, nq * MOBA_BLOCK, LANES), BF16),
                        pltpu.VMEM((2, MOBA_PAIR, 1), F32),
                        pltpu.VMEM((2, MOBA_PAIR, LANES), F32),
                        pltpu.VMEM((2, 2, MOBA_PAIR, MOBA_PAIR), F32)],
        compiler_params=pltpu.CompilerParams(
            dimension_semantics=("parallel", "parallel", "arbitrary"), vmem_limit_bytes=VMEM_LIMIT),
        name="moba",
    )(q, k, v)


MERGE_TS = 256


def _bf16_bits(x):
    b = pltpu.bitcast(x, jnp.int32)
    r = b + 0x7FFF + (lax.shift_right_logical(b, 16) & 1)
    return lax.shift_right_logical(r, 16)


def _merge_kernel(x_ref, ys_ref, at_ref, ga_ref, gb_ref, wa_ref, wb_ref, wo_ref, g_ref,
                  wq_ref, k1_ref, k2_ref, after_ref, x1_ref, hw_ref, idx_ref, gate_ref, sc_ref):
    del after_ref
    ya = _dot(ys_ref[...], wa_ref[...])
    yb = _dot(at_ref[...], wb_ref[...])
    merged = ga_ref[...].astype(F32) * ya + gb_ref[...].astype(F32) * yb
    x1 = x_ref[...] + _dot(merged.astype(BF16), wo_ref[...])
    x1_ref[...] = x1
    hq = _rms(x1, g_ref[...])
    hw_ref[...] = _pack_words(hq)
    qp = _dot(hq.astype(BF16), wq_ref[...])
    for h in range(PEER_HEADS):
        o = h * PEER_QDIM
        sc_ref[2 * h] = _dot_nt(k1_ref[h], qp[:, o:o + PEER_HALF], precision=HIGHEST)
        sc_ref[2 * h + 1] = _dot_nt(k2_ref[h], qp[:, o + PEER_HALF:o + PEER_QDIM], precision=HIGHEST)
    _topk_kernel(sc_ref, idx_ref, gate_ref)


def _merge(x, ys, att, ga, gb, t0, off, nt, after, w_proj_ssm, w_proj_att, w_out, g_ffn, peer_w_q,
           keys1, keys2):
    B = ys.shape[0]
    ts = min(MERGE_TS, nt)
    nblk = nt // ts
    i0 = (t0 + off) // ts
    o0 = off // ts
    tok = lambda d: pl.BlockSpec((None, ts, d), lambda b, i: (b, o0 + i, 0))
    row = lambda d: pl.BlockSpec((ts, d), lambda b, i: (b * nblk + i, 0))
    full = lambda shape: pl.BlockSpec(shape, lambda b, i: (0,) * len(shape))
    qd = PEER_HEADS * PEER_QDIM
    return pl.pallas_call(
        _merge_kernel,
        grid=(B, nblk),
        in_specs=[pl.BlockSpec((None, ts, D_MODEL), lambda b, i: (b, i0 + i, 0)),
                  tok(D_SSM), tok(D_ATT), tok(D_MODEL), tok(D_MODEL),
                  full((D_SSM, D_MODEL)), full((D_ATT, D_MODEL)), full((D_MODEL, D_MODEL)),
                  full((1, D_MODEL)), full((D_MODEL, qd)),
                  full((PEER_HEADS, PEER_KEYS, PEER_HALF)), full((PEER_HEADS, PEER_KEYS, PEER_HALF)),
                  pl.BlockSpec(memory_space=pl.ANY)],
        out_specs=[row(D_MODEL), row(D_MODEL // 2), row(PEER_SEL), row(PEER_SEL)],
        out_shape=[jax.ShapeDtypeStruct((B * nt, D_MODEL), F32),
                   jax.ShapeDtypeStruct((B * nt, D_MODEL // 2), jnp.int32),
                   jax.ShapeDtypeStruct((B * nt, PEER_SEL), jnp.int32),
                   jax.ShapeDtypeStruct((B * nt, PEER_SEL), F32)],
        scratch_shapes=[pltpu.VMEM((2 * PEER_HEADS, PEER_KEYS, ts), F32)],
        compiler_params=pltpu.CompilerParams(
            dimension_semantics=("parallel", "parallel"), vmem_limit_bytes=VMEM_LIMIT),
        name="merge",
    )(x, ys, att, ga, gb, w_proj_ssm, w_proj_att, w_out, g_ffn, peer_w_q, keys1, keys2, after)


def _top_rows(s, row, k):
    vals, idxs = [], []
    for _ in range(k):
        m = jnp.max(s, axis=0, keepdims=True)
        idx = jnp.min(jnp.where(s == m, row, s.shape[0]), axis=0, keepdims=True)
        vals.append(m)
        idxs.append(idx)
        s = jnp.where(row == idx, -jnp.inf, s)
    return vals, idxs


def _stack_rows(rows, row16):
    acc = jnp.zeros(row16.shape, rows[0].dtype)
    for r, v in enumerate(rows):
        acc = jnp.where(row16 == r, v, acc)
    return acc


def _topk_kernel(sc_ref, idx_ref, gate_ref):
    ts = sc_ref.shape[-1]
    row = lax.broadcasted_iota(jnp.int32, (PEER_KEYS, ts), 0).astype(F32)
    row16 = lax.broadcasted_iota(jnp.int32, (PEER_TOPK, ts), 0)
    row8 = lax.broadcasted_iota(jnp.int32, (SUBLANES, ts), 0)
    counts = [PEER_TOPK // (i + 1) for i in range(PEER_TOPK)]
    heights = [PEER_TOPK if c > SUBLANES else SUBLANES for c in counts]
    n_cand = sum(heights)
    rowc = lax.broadcasted_iota(jnp.int32, (n_cand, ts), 0).astype(F32)
    gate_rows, eid_rows = [], []
    for h in range(PEER_HEADS):
        v1, i1 = _top_rows(sc_ref[2 * h], row, PEER_TOPK)
        v2, i2 = _top_rows(sc_ref[2 * h + 1], row, PEER_TOPK)
        v2s = _stack_rows(v2, row16)
        i2s = _stack_rows(i2, row16)
        cand, eid = [], []
        for i in range(PEER_TOPK):
            n = heights[i]
            cand.append(jnp.where((row16 if n == PEER_TOPK else row8) < counts[i],
                                  v1[i] + v2s[:n], -jnp.inf))
            eid.append(i1[i] * PEER_KEYS + i2s[:n])
        cand = jnp.concatenate(cand, axis=0)
        eid = jnp.concatenate(eid, axis=0)
        tops, picks = [], []
        for _ in range(PEER_TOPK):
            m = jnp.max(cand, axis=0, keepdims=True)
            pos = jnp.min(jnp.where(cand == m, rowc, n_cand), axis=0, keepdims=True)
            hit = rowc == pos
            picks.append(jnp.max(jnp.where(hit, eid, -1.0), axis=0, keepdims=True))
            tops.append(m)
            cand = jnp.where(hit, -jnp.inf, cand)
        top = _stack_rows(tops, row16)
        p = jnp.exp(top - jnp.max(top, axis=0, keepdims=True))
        gate_rows.append(p / jnp.sum(p, axis=0, keepdims=True))
        eid_rows.append(_stack_rows(picks, row16))
    gate_ref[...] = jnp.transpose(jnp.concatenate(gate_rows, axis=0))
    idx_ref[...] = jnp.transpose(jnp.concatenate(eid_rows, axis=0)).astype(jnp.int32)


SC_CORES = 2
SC_SUBCORES = 16
SC_LANES = 16
SC_WORKERS = SC_CORES * SC_SUBCORES
PEER_CH = SC_LANES
PEER_NCH = PEER_SEL // PEER_CH
PEER_WORDS = D_MODEL // 2
PEER_NWG = PEER_WORDS // SC_LANES
PEER_RING = 4
PEER_QUAD = 4
HI_MASK = -65536
GELU_C = 0.7978845608028654


def _gelu_tanh_via_exp(x):
    z = GELU_C * (x + 0.044715 * (x * x * x))
    t = 1.0 - 2.0 / (jnp.exp(2.0 * z) + 1.0)
    return 0.5 * x * (1.0 + t)


def _unpack_pair(w):
    lo = plsc.bitcast(lax.shift_left(w, 16), F32)
    hi = plsc.bitcast(lax.bitwise_and(w, HI_MASK), F32)
    return lo, hi


def _peer_sc_body(idx_hbm, gate_hbm, h_hbm, uv_hbm, after_hbm, o_hbm,
                  idx_v, gate_v, h_v, buf, out_v, gsem, msem, osem):
    n_tok = o_hbm.shape[0] // SC_WORKERS
    base = (lax.axis_index("s") * SC_CORES + lax.axis_index("c")) * n_tok
    lane = lax.iota(jnp.int32, SC_LANES)
    zero_rows = jnp.zeros((SC_LANES,), jnp.int32)

    def meta_copies(tok, s):
        return (pltpu.make_async_copy(idx_hbm.at[tok], idx_v.at[s], msem.at[s]),
                pltpu.make_async_copy(gate_hbm.at[tok], gate_v.at[s], msem.at[s]),
                pltpu.make_async_copy(h_hbm.at[tok], h_v.at[s], msem.at[s]))

    def gather(slot, rows):
        return pltpu.make_async_copy(uv_hbm.at[rows], buf.at[slot], gsem.at[slot])

    def token(t, carry):
        s = t % 2
        tok = base + t
        nxt = base + jnp.minimum(t + 1, n_tok - 1)
        for cp in meta_copies(nxt, 1 - s):
            cp.start()

        @pl.when(t >= 2)
        def _():
            pltpu.make_async_copy(out_v.at[s], o_hbm.at[tok], osem.at[s]).wait()

        def chunk(c, prio):
            slot = c % PEER_RING
            gather(slot, zero_rows).wait()

            def dot_step(q, accs):
                cols = [pl.ds(pl.multiple_of((q * PEER_QUAD + j) * SC_LANES, SC_LANES), SC_LANES)
                        for j in range(PEER_QUAD)]
                hs = [plsc.bitcast(h_v[s, col], BF16) for col in cols]
                out = []
                for r in range(PEER_CH):
                    p = plsc.bitcast(buf[slot, r, cols[0]], BF16) * hs[0]
                    for j in range(1, PEER_QUAD):
                        p = p + plsc.bitcast(buf[slot, r, cols[j]], BF16) * hs[j]
                    lo, hi = _unpack_pair(plsc.bitcast(p, jnp.int32))
                    out.append(accs[r] + lo + hi)
                return tuple(out)

            accs = lax.fori_loop(0, PEER_NWG // PEER_QUAD, dot_step,
                                 tuple(jnp.zeros((SC_LANES,), F32) for _ in range(PEER_CH)))
            tot = jnp.zeros((SC_LANES,), F32)
            for r in range(PEER_CH):
                tot = jnp.where(lane == r, jnp.sum(accs[r]), tot)
            rows = pl.ds(pl.multiple_of(c * PEER_CH, PEER_CH), PEER_CH)
            wvec = gate_v[s, rows] * _gelu_tanh_via_exp(tot)
            ws = []
            for r in range(PEER_CH):
                w = wvec.at[jnp.full((SC_LANES,), r, jnp.int32)].get(mode="promise_in_bounds")
                ws.append(plsc.pack(w, w, format=plsc.PackFormat.INTERLEAVED,
                                    preferred_element_type=BF16))
            first = c == 0

            @plsc.parallel_loop(0, PEER_NWG, unroll=2)
            def acc_step(g):
                col = pl.ds(pl.multiple_of(g * SC_LANES, SC_LANES), SC_LANES)
                col_v = pl.ds(pl.multiple_of(PEER_WORDS + g * SC_LANES, SC_LANES), SC_LANES)
                o_lo = jnp.where(first, 0.0, out_v[s, col])
                o_hi = jnp.where(first, 0.0, out_v[s, col_v])
                for r0 in range(0, PEER_CH, PEER_QUAD):
                    p = plsc.bitcast(buf[slot, r0, col_v], BF16) * ws[r0]
                    for r in range(r0 + 1, r0 + PEER_QUAD):
                        p = p + plsc.bitcast(buf[slot, r, col_v], BF16) * ws[r]
                    lo, hi = _unpack_pair(plsc.bitcast(p, jnp.int32))
                    o_lo = o_lo + lo
                    o_hi = o_hi + hi
                out_v[s, col] = o_lo
                out_v[s, col_v] = o_hi

            @pl.when(c == PEER_NCH - PEER_RING)
            def _():
                for cp in meta_copies(nxt, 1 - s):
                    cp.wait()

            ahead = c + PEER_RING
            src = jnp.where(ahead < PEER_NCH, s, 1 - s)
            nrows = idx_v[src, pl.ds(pl.multiple_of((ahead % PEER_NCH) * PEER_CH, PEER_CH), PEER_CH)]
            gather(slot, nrows).start(priority=prio)

        def chunk_pair(i, carry):
            chunk(2 * i, 0)
            chunk(2 * i + 1, 1)
            return carry

        lax.fori_loop(0, PEER_NCH // 2, chunk_pair, 0)
        pltpu.make_async_copy(out_v.at[s], o_hbm.at[tok], osem.at[s]).start()
        return carry

    for cp in meta_copies(base, 0):
        cp.start()
    for cp in meta_copies(base, 0):
        cp.wait()
    for c in range(PEER_RING):
        gather(c, idx_v[0, pl.ds(c * PEER_CH, PEER_CH)]).start(priority=c % 2)
    lax.fori_loop(0, n_tok, token, 0)
    for c in range(PEER_RING):
        gather(c, zero_rows).wait()
    for s in range(2):
        pltpu.make_async_copy(out_v.at[s], o_hbm.at[base], osem.at[s]).wait()


PACK_ROWS = 1024


def _pack_words(x):
    half = x.shape[1] // 2
    return _bf16_bits(x[:, :half]) | lax.shift_left(_bf16_bits(x[:, half:]), 16)


def _pack_tables_kernel(u_ref, v_ref, o_ref):
    o_ref[:, :PEER_WORDS] = _pack_words(u_ref[...])
    o_ref[:, PEER_WORDS:] = _pack_words(v_ref[...])


def _pack_tables(peer_u, peer_v):
    n = peer_u.shape[0]
    rows = min(PACK_ROWS, n)
    spec = pl.BlockSpec((rows, D_MODEL), lambda i: (i, 0))
    return pl.pallas_call(
        _pack_tables_kernel,
        grid=(n // rows,),
        in_specs=[spec, spec],
        out_specs=spec,
        out_shape=jax.ShapeDtypeStruct((n, D_MODEL), jnp.int32),
        compiler_params=pltpu.CompilerParams(
            dimension_semantics=("parallel",), vmem_limit_bytes=VMEM_LIMIT),
        name="pack_tables",
    )(peer_u, peer_v)


def _peer(idx, h_words, gates, uv_words, after):
    T = h_words.shape[0]
    assert T % (2 * SC_WORKERS) == 0
    mesh = plsc.VectorSubcoreMesh(core_axis_name="c", subcore_axis_name="s",
                                  num_cores=SC_CORES, num_subcores=SC_SUBCORES)
    return pl.kernel(
        _peer_sc_body,
        out_type=jax.ShapeDtypeStruct((T, D_MODEL), F32),
        mesh=mesh,
        scratch_types=[
            pltpu.VMEM((2, PEER_SEL), jnp.int32), pltpu.VMEM((2, PEER_SEL), F32),
            pltpu.VMEM((2, PEER_WORDS), jnp.int32),
            pltpu.VMEM((PEER_RING, PEER_CH, 2 * PEER_WORDS), jnp.int32),
            pltpu.VMEM((2, D_MODEL), F32),
            pltpu.SemaphoreType.DMA((PEER_RING,)),
            pltpu.SemaphoreType.DMA((2,)), pltpu.SemaphoreType.DMA((2,)),
        ],
        compiler_params=pltpu.CompilerParams(needs_layout_passes=False),
        name="peer_sc",
    )(idx, gates, h_words, uv_words, after)


FINAL_TS = 512


def _final_kernel(x1_ref, pe_ref, p_ref, gp_ref, wg_ref, wp_ref, gf_ref, o_ref):
    x2 = x1_ref[...] + pe_ref[...]
    e = _dot(p_ref[...].astype(BF16), wp_ref[...])
    gate = jax.nn.sigmoid(_dot(_rms(x2, gp_ref[...]).astype(BF16), wg_ref[...]))
    o_ref[...] = _rms(x2 + gate * e, gf_ref[...])


def _final(x1, peer_out, p, t0, nt, g_ple, ple_w_gate, ple_w_proj, g_final):
    B = p.shape[0]
    ts = min(FINAL_TS, nt)
    nblk = nt // ts
    i0 = t0 // ts
    row = lambda d: pl.BlockSpec((ts, d), lambda b, i: (b * nblk + i, 0))
    full = lambda shape: pl.BlockSpec(shape, lambda b, i: (0,) * len(shape))
    return pl.pallas_call(
        _final_kernel,
        grid=(B, nblk),
        in_specs=[row(D_MODEL), row(D_MODEL),
                  pl.BlockSpec((None, ts, D_PLE), lambda b, i: (b, i0 + i, 0)),
                  full((1, D_MODEL)), full((D_MODEL, D_MODEL)), full((D_PLE, D_MODEL)),
                  full((1, D_MODEL))],
        out_specs=pl.BlockSpec((None, ts, D_MODEL), lambda b, i: (b, i, 0)),
        out_shape=jax.ShapeDtypeStruct((B, nt, D_MODEL), F32),
        compiler_params=pltpu.CompilerParams(
            dimension_semantics=("parallel", "parallel"), vmem_limit_bytes=VMEM_LIMIT),
        name="final",
    )(x1, peer_out, p, g_ple, ple_w_gate, ple_w_proj, g_final)


CHUNK_STEPS = (512, 512, 512, 512, 512, 1024, 1024, 1024, 1024, 1024, 512)


def kernel(x, p, positions, g_mix, w_in, ssm_log_dt, ssm_a_re, ssm_a_im, ssm_b_re, ssm_b_im,
           ssm_c_re, ssm_c_im, ssm_d, ssm_w_glu, w_proj_ssm, w_proj_att, w_out, g_ffn,
           peer_w_q, peer_keys1, peer_keys2, peer_u, peer_v, g_ple, ple_w_gate, ple_w_proj,
           g_final):
    B, S, _ = x.shape
    assert w_in.shape[0] == 1, "the final rmsnorm is fused into the single layer's last stage"
    steps = CHUNK_STEPS if sum(CHUNK_STEPS) == S else (S,)
    i = 0
    tables = _s5_tables(ssm_log_dt[i], ssm_a_re[i], ssm_a_im[i], ssm_b_re[i], ssm_b_im[i],
                        ssm_c_re[i], ssm_c_im[i])
    w_in_b, w_glu_b = w_in[i].astype(BF16), ssm_w_glu[i].astype(BF16)
    d_skip = ssm_d[i].reshape(1, D_SSM).astype(F32)
    merge_w = (w_proj_ssm[i].astype(BF16), w_proj_att[i].astype(BF16), w_out[i].astype(BF16),
               g_ffn[i].reshape(1, D_MODEL), peer_w_q[i].astype(BF16), peer_keys1[i], peer_keys2[i])
    final_w = (g_ple[i].reshape(1, D_MODEL), ple_w_gate[i].astype(BF16),
               ple_w_proj[i].astype(BF16), g_final.reshape(1, D_MODEL))
    uv_words = _pack_tables(peer_u[i], peer_v[i])
    k_all = jnp.zeros((B, S, D_ATT), BF16)
    v_all = jnp.zeros((B, S, D_ATT), BF16)
    carry = jnp.zeros((2, SUBLANES, D_STATE), F32)
    outs = []
    t0 = 0
    after = (carry, carry)
    peers = []
    for nt in steps:
        u_sb, q, k, v, ga, gb = _in_proj(x, positions, g_mix[i], w_in_b, t0, nt, after)
        k_all = lax.dynamic_update_slice(k_all, k, (0, t0, 0))
        v_all = lax.dynamic_update_slice(v_all, v, (0, t0, 0))
        ys, carry = _s5(u_sb, carry, tables, d_skip, w_glu_b, B)
        att = _moba(q, k_all, v_all, t0 // MOBA_BLOCK)
        subs = ((0, nt // 2), (nt // 2, nt // 2)) if t0 == 0 and nt >= 2 * MERGE_TS else ((0, nt),)
        for off, n in subs:
            x1, h_words, idx, gates = _merge(x, ys, att, ga, gb, t0, off, n,
                                             peers[-2] if len(peers) > 1 else carry, *merge_w)
            peer_out = _peer(idx, h_words, gates, uv_words,
                             peers[-1] if t0 + off + n == S else carry)
            peers.append(peer_out)
            outs.append(_final(x1, peer_out, p[i], t0 + off, n, *final_w))
        after = (gates, outs[-3] if len(outs) > 2 else carry)
        t0 += nt
    return jnp.concatenate(outs, axis=1)
```
